```python
import math
import jax, jax.numpy as jnp
from jax import lax
import numpy as np

D_MODEL = 1024
BATCH = 2
SEQ = 16384
DEPTH = 4

GRID_W = 64
CTX_LEN = 256
N_MIXERS = 4
NORM_EPS = 1e-6
N_MOD = 6
FFN_HIDDEN = ((8 * D_MODEL // 3 + 255) // 256) * 256

MLA_HEADS = D_MODEL // 128
MLA_NOPE = 128
MLA_ROPE = 64
MLA_V = 128
KV_LORA_RANK = D_MODEL // 4
Q_LORA_RANK = 3 * KV_LORA_RANK
ROPE_THETA = 10000.0
Q_BLOCK = 128

HYENA_ORDER = 2
HYENA_BANDS = 16
HYENA_EMB = 2 * HYENA_BANDS + 1
HYENA_FILTER_WIDTH = 64
HYENA_INNER = 2
HYENA_TARGET = 1e-2
HYENA_FAST = 0.3
HYENA_SLOW = 1.5
SHORT_CONV = 3

S5_GROUP = 16
S5_GROUPS = D_MODEL // S5_GROUP
S5_STATE = 64
S5_DT_MIN = 1e-3
S5_DT_MAX = 1e-1

FNET_GROUPS = 8
FNET_GC = D_MODEL // FNET_GROUPS

kernel_name = "hybrid_interleaved_mla_hyena_s5_fnet_dit"


def _n_layers_of(m):
    return (DEPTH - m + N_MIXERS - 1) // N_MIXERS


def rmsnorm(x, g):
    xf = x.astype(jnp.float32)
    y = xf * lax.rsqrt(jnp.mean(jnp.square(xf), axis=-1, keepdims=True) + NORM_EPS)
    return (y * g.astype(jnp.float32)).astype(x.dtype)


def modulate(h, shift, scale):
    return h * (1.0 + scale) + shift


def swiglu(h, w13, w2):
    a, b = jnp.split(h @ w13, 2, axis=-1)
    return (jax.nn.silu(a) * b) @ w2


def axial_rope(x, rope):
    cos_r, sin_r, cos_c, sin_c = rope
    if x.ndim == 4:
        cos_r, sin_r, cos_c, sin_c = (t[:, None, :] for t in rope)
    half = x.shape[-1] // 2

    def rot(xa, cs, sn):
        x1, x2 = jnp.split(xa, 2, axis=-1)
        return jnp.concatenate([x1 * cs - x2 * sn, x2 * cs + x1 * sn], axis=-1)

    out = jnp.concatenate([rot(x[..., :half], cos_r, sin_r), rot(x[..., half:], cos_c, sin_c)], axis=-1)
    return out.astype(x.dtype)


def mla_mixer(h_ctx, h_lat, rope, need_ctx, w_in, q_g, kv_g, w_uq, w_ukv, w_o):
    B, L, _ = h_lat.shape
    C = h_ctx.shape[1]
    H = MLA_HEADS

    def queries(zq, n):
        q = (rmsnorm(zq, q_g) @ w_uq).reshape(B, n, H, MLA_NOPE + MLA_ROPE)
        return q[..., :MLA_NOPE], q[..., MLA_NOPE:]

    def keys_values(zkv, n):
        ckv, k_rope = zkv[..., :KV_LORA_RANK], zkv[..., KV_LORA_RANK:]
        kv = (rmsnorm(ckv, kv_g) @ w_ukv).reshape(B, n, H, MLA_NOPE + MLA_V)
        return kv[..., :MLA_NOPE], k_rope, kv[..., MLA_NOPE:]

    z_lat = h_lat @ w_in
    ql_nope, ql_rope = queries(z_lat[..., :Q_LORA_RANK], L)
    kl_nope, kl_rope, vl = keys_values(z_lat[..., Q_LORA_RANK:], L)
    ql_rope_pos = axial_rope(ql_rope, rope)
    kl_rope_pos = axial_rope(kl_rope, rope)
    z_ctx = h_ctx @ (w_in if need_ctx else w_in[:, Q_LORA_RANK:])
    kc_nope, kc_rope, vc = keys_values(z_ctx[..., -(KV_LORA_RANK + MLA_ROPE):], C)
    scale = (MLA_NOPE + MLA_ROPE) ** -0.5

    def attend_block(blk):
        qn, qr_pos, qr = blk
        s_ctx = jnp.einsum('bqhd,bkhd->bhqk', qn, kc_nope) + jnp.einsum('bqhr,bkr->bhqk', qr, kc_rope)
        s_lat = jnp.einsum('bqhd,bkhd->bhqk', qn, kl_nope) + jnp.einsum('bqhr,bkr->bhqk', qr_pos, kl_rope_pos)
        s = jnp.concatenate([s_ctx, s_lat], axis=-1).astype(jnp.float32) * scale
        p = jax.nn.softmax(s, axis=-1).astype(vl.dtype)
        return jnp.einsum('bhqk,bkhd->bqhd', p[..., :C], vc) + jnp.einsum('bhqk,bkhd->bqhd', p[..., C:], vl)

    nb = L // Q_BLOCK

    def to_blocks(t):
        return jnp.moveaxis(t.reshape(B, nb, Q_BLOCK, *t.shape[2:]), 1, 0)

    o = lax.map(attend_block, (to_blocks(ql_nope), to_blocks(ql_rope_pos), to_blocks(ql_rope)))
    o_lat = jnp.moveaxis(o, 0, 1).reshape(B, L, H * MLA_V) @ w_o
    if not need_ctx:
        return None, o_lat
    qc_nope, qc_rope = queries(z_ctx[..., :Q_LORA_RANK], C)
    s = (jnp.einsum('bqhd,bkhd->bhqk', qc_nope, kc_nope)
         + jnp.einsum('bqhr,bkr->bhqk', qc_rope, kc_rope)).astype(jnp.float32) * scale
    p = jax.nn.softmax(s, axis=-1).astype(vc.dtype)
    o_ctx = jnp.einsum('bhqk,bkhd->bqhd', p, vc).reshape(B, C, H * MLA_V) @ w_o
    return o_ctx, o_lat


def short_conv(u, w, b):
    n = u.shape[1]
    pad = SHORT_CONV // 2
    up = jnp.pad(u, ((0, 0), (pad, pad), (0, 0)))
    out = b
    for k in range(SHORT_CONV):
        out = out + up[:, k:k + n] * w[k]
    return out


def hyena_kernel(n, f_w1, f_b1, f_freq, f_w2, f_b2, f_w3):
    f32 = jnp.float32
    t = jnp.linspace(0.0, 1.0, n, dtype=f32)[:, None]
    w = (2.0 * math.pi / n) * jnp.arange(n, dtype=f32)[:, None]
    bands = jnp.linspace(1e-4, HYENA_BANDS - 1, HYENA_BANDS, dtype=f32)[None, :]
    z = jnp.concatenate([t, jnp.cos(w * bands), -jnp.sin(w * bands)], axis=-1)
    freq = f_freq.astype(f32)
    a = jnp.sin(freq * (z @ f_w1.astype(f32) + f_b1.astype(f32)))
    for k in range(HYENA_INNER):
        a = jnp.sin(freq * (a @ f_w2[k].astype(f32) + f_b2[k].astype(f32)))
    h = (a @ f_w3.astype(f32)).reshape(n, HYENA_ORDER - 1, 2, D_MODEL)
    deltas = jnp.abs(jnp.linspace(math.log(HYENA_TARGET) / HYENA_SLOW, math.log(HYENA_TARGET) / HYENA_FAST,
                                  D_MODEL, dtype=f32))
    h = h * jnp.exp(-t * deltas)[:, None, None, :]
    k = jnp.concatenate([h[:, :, 0], jnp.zeros((1, HYENA_ORDER - 1, D_MODEL), f32), h[:0:-1, :, 1]], axis=0)
    return k / jnp.sum(jnp.abs(k), axis=0, keepdims=True)


def hyena_seq(h, w_in, b_in, conv_w, conv_b, filt, skip, w_out, b_out):
    f32 = jnp.float32
    n = h.shape[1]
    u = short_conv(h @ w_in + b_in, conv_w, conv_b)
    chunks = jnp.split(u, HYENA_ORDER + 1, axis=-1)
    gates, v = chunks[:-1], chunks[-1]
    kf = jnp.fft.rfft(hyena_kernel(n, *filt), n=2 * n, axis=0)
    for o, g in enumerate(reversed(gates[1:])):
        v = v * g
        vf = jnp.fft.rfft(v.astype(f32), n=2 * n, axis=1)
        y = jnp.fft.irfft(vf * kf[None, :, o, :], n=2 * n, axis=1)[:, :n]
        v = (y + v.astype(f32) * skip[o].astype(f32)).astype(h.dtype)
    return (v * gates[0]) @ w_out + b_out


def _lin_rec(e1, e2):
    a1, b1 = e1
    a2, b2 = e2
    return a1 * a2, a2 * b1 + b2


def s5_scan(u, lam_bar, b_bar, h0, reverse):
    bu = jnp.einsum('bngc,gsc->bngs', u.astype(jnp.complex64), b_bar)
    a = jnp.broadcast_to(lam_bar, bu.shape)
    a_cum, xs = lax.associative_scan(_lin_rec, (a, bu), axis=1, reverse=reverse)
    if h0 is None:
        return xs
    return xs + a_cum * h0[:, None]


def s5_mixer(h_ctx, h_lat, need_ctx, lam_re, lam_im, log_dt, b_re, b_im, c_re, c_im, d_skip, w_glu, b_glu):
    f32 = jnp.float32
    B, L, D = h_lat.shape
    C = h_ctx.shape[1]
    u_lat = h_lat.astype(f32).reshape(B, L, S5_GROUPS, S5_GROUP)
    u_ctx = h_ctx.astype(f32).reshape(B, C, S5_GROUPS, S5_GROUP)
    dsk = d_skip.astype(f32)
    y_lat = dsk * h_lat.astype(f32)
    y_ctx = dsk * h_ctx.astype(f32) if need_ctx else None
    for direction, reverse in ((0, False), (1, True)):
        lam = lax.complex(lam_re[direction].astype(f32), lam_im[direction].astype(f32))
        dt = jnp.exp(log_dt[direction].astype(f32))[:, None]
        lam_bar = jnp.exp(lam * dt)
        b_bar = ((lam_bar - 1.0) / lam)[..., None] * lax.complex(b_re[direction].astype(f32), b_im[direction].astype(f32))
        c_mat = lax.complex(c_re[direction].astype(f32), c_im[direction].astype(f32))
        xs_ctx = s5_scan(u_ctx, lam_bar, b_bar, None, reverse)
        h0 = xs_ctx[:, 0] if reverse else xs_ctx[:, -1]
        xs_lat = s5_scan(u_lat, lam_bar, b_bar, h0, reverse)
        y_lat = y_lat + jnp.einsum('bngs,gcs->bngc', xs_lat, c_mat).real.reshape(B, L, D)
        if need_ctx:
            y_ctx = y_ctx + jnp.einsum('bngs,gcs->bngc', xs_ctx, c_mat).real.reshape(B, C, D)

    def glu(y, dtype):
        g = jax.nn.gelu(y.astype(dtype), approximate=False)
        a, b = jnp.split(g @ w_glu + b_glu, 2, axis=-1)
        return a * jax.nn.sigmoid(b)

    return (glu(y_ctx, h_ctx.dtype) if need_ctx else None), glu(y_lat, h_lat.dtype)


def fnet_seq(h, w_o, b_o):
    B, n, D = h.shape
    hg = h.astype(jnp.float32).reshape(B, n, FNET_GROUPS, FNET_GC)
    f = jnp.fft.fftn(hg, axes=(1, 3), norm="ortho").real
    return f.reshape(B, n, D).astype(h.dtype) @ w_o + b_o


def setup_inputs(seed: int = 0) -> dict:
    key = jax.random.key(seed)
    ks = iter(jax.random.split(key, 64))
    f32 = jnp.float32

    def nrm(shape, scale):
        return scale * jax.random.normal(next(ks), shape, f32)

    nA, nB, nC, nD = (_n_layers_of(m) for m in range(N_MIXERS))
    D, F, H, G, N = D_MODEL, FFN_HIDDEN, MLA_HEADS, S5_GROUPS, S5_STATE
    P = (HYENA_ORDER + 1) * D
    return {
        "x": nrm((BATCH, SEQ, D), 1.0),
        "c": nrm((BATCH, D), 1.0),
        "ctx": nrm((BATCH, CTX_LEN, D), 1.0),
        "c_ctx": nrm((D,), 1.0),
        "mod_w": nrm((DEPTH, D, N_MOD * D), D ** -0.5),
        "mod_b": nrm((DEPTH, N_MOD * D), 0.02),
        "mix_pre_g": 1.0 + nrm((DEPTH, D), 0.02),
        "mix_post_g": 1.0 + nrm((DEPTH, D), 0.02),
        "ffn_pre_g": 1.0 + nrm((DEPTH, D), 0.02),
        "ffn_post_g": 1.0 + nrm((DEPTH, D), 0.02),
        "ffn_w13": nrm((DEPTH, D, 2 * F), D ** -0.5),
        "ffn_w2": nrm((DEPTH, F, D), F ** -0.5),
        "mla_w_in": nrm((nA, D, Q_LORA_RANK + KV_LORA_RANK + MLA_ROPE), D ** -0.5),
        "mla_q_norm_g": 1.0 + nrm((nA, Q_LORA_RANK), 0.02),
        "mla_kv_norm_g": 1.0 + nrm((nA, KV_LORA_RANK), 0.02),
        "mla_w_uq": nrm((nA, Q_LORA_RANK, H * (MLA_NOPE + MLA_ROPE)), Q_LORA_RANK ** -0.5),
        "mla_w_ukv": nrm((nA, KV_LORA_RANK, H * (MLA_NOPE + MLA_V)), KV_LORA_RANK ** -0.5),
        "mla_w_o": nrm((nA, H * MLA_V, D), (H * MLA_V) ** -0.5),
        "hy_w_in": nrm((nB, D, P), D ** -0.5),
        "hy_b_in": nrm((nB, P), 0.02),
        "hy_conv_w": nrm((nB, SHORT_CONV, P), SHORT_CONV ** -0.5),
        "hy_conv_b": nrm((nB, P), 0.02),
        "hy_f_w1": nrm((nB, HYENA_EMB, HYENA_FILTER_WIDTH), HYENA_EMB ** -0.5),
        "hy_f_b1": nrm((nB, HYENA_FILTER_WIDTH), 0.02),
        "hy_f_freq": 1.0 + nrm((nB, HYENA_FILTER_WIDTH), 0.02),
        "hy_f_w2": nrm((nB, HYENA_INNER, HYENA_FILTER_WIDTH, HYENA_FILTER_WIDTH), HYENA_FILTER_WIDTH ** -0.5),
        "hy_f_b2": nrm((nB, HYENA_INNER, HYENA_FILTER_WIDTH), 0.02),
        "hy_f_w3": nrm((nB, HYENA_FILTER_WIDTH, (HYENA_ORDER - 1) * 2 * D), HYENA_FILTER_WIDTH ** -0.5),
        "hy_skip": nrm((nB, HYENA_ORDER - 1, D), 0.5),
        "hy_w_out": nrm((nB, D, D), D ** -0.5),
        "hy_b_out": nrm((nB, D), 0.02),
        "s5_lambda_re": -0.5 + nrm((nC, 2, G, N), 0.01),
        "s5_lambda_im": jnp.broadcast_to(math.pi * jnp.arange(N, dtype=f32), (nC, 2, G, N)),
        "s5_log_dt": jax.random.uniform(next(ks), (nC, 2, G), f32, math.log(S5_DT_MIN), math.log(S5_DT_MAX)),
        "s5_b_re": nrm((nC, 2, G, N, S5_GROUP), (2 * S5_GROUP) ** -0.5),
        "s5_b_im": nrm((nC, 2, G, N, S5_GROUP), (2 * S5_GROUP) ** -0.5),
        "s5_c_re": nrm((nC, 2, G, S5_GROUP, N), (2 * N) ** -0.5),
        "s5_c_im": nrm((nC, 2, G, S5_GROUP, N), (2 * N) ** -0.5),
        "s5_d": nrm((nC, D), 1.0),
        "s5_w_glu": nrm((nC, D, 2 * D), D ** -0.5),
        "s5_b_glu": nrm((nC, 2 * D), 0.02),
        "fn_w_o": nrm((nD, D, D), D ** -0.5),
        "fn_b_o": nrm((nD, D), 0.02),
    }


def reference(x, c, ctx, c_ctx, mod_w, mod_b, mix_pre_g, mix_post_g, ffn_pre_g, ffn_post_g, ffn_w13, ffn_w2,
              mla_w_in, mla_q_norm_g, mla_kv_norm_g, mla_w_uq, mla_w_ukv, mla_w_o,
              hy_w_in, hy_b_in, hy_conv_w, hy_conv_b, hy_f_w1, hy_f_b1, hy_f_freq, hy_f_w2, hy_f_b2, hy_f_w3,
              hy_skip, hy_w_out, hy_b_out,
              s5_lambda_re, s5_lambda_im, s5_log_dt, s5_b_re, s5_b_im, s5_c_re, s5_c_im, s5_d, s5_w_glu, s5_b_glu,
              fn_w_o, fn_b_o):
    f32 = jnp.float32
    n_lat = x.shape[1]
    ROWS = n_lat // GRID_W
    row = jnp.repeat(jnp.arange(ROWS, dtype=f32), GRID_W)
    col = jnp.tile(jnp.arange(GRID_W, dtype=f32), ROWS)
    axis_dim = MLA_ROPE // 2
    inv_freq = 1.0 / (ROPE_THETA ** (jnp.arange(0, axis_dim, 2, dtype=f32) / axis_dim))
    ang_r = row[:, None] * inv_freq
    ang_c = col[:, None] * inv_freq
    rope = (jnp.cos(ang_r), jnp.sin(ang_r), jnp.cos(ang_c), jnp.sin(ang_c))

    silu_c = jax.nn.silu(c)
    silu_cc = jax.nn.silu(c_ctx)
    reads_ctx = (0, 2)
    xl, xc = x, ctx
    for i in range(DEPTH):
        m, j = i % N_MIXERS, i // N_MIXERS
        ctx_read = m in reads_ctx
        ctx_out = any((k % N_MIXERS) in reads_ctx for k in range(i + 1, DEPTH))
        mod_l = jnp.split((silu_c @ mod_w[i] + mod_b[i])[:, None, :], N_MOD, axis=-1)
        hl = modulate(rmsnorm(xl, mix_pre_g[i]), mod_l[0], mod_l[1])
        hc = None
        if ctx_read or ctx_out:
            mod_c = jnp.split((silu_cc @ mod_w[i] + mod_b[i])[None, None, :], N_MOD, axis=-1)
            hc = modulate(rmsnorm(xc, mix_pre_g[i]), mod_c[0], mod_c[1])

        if m == 0:
            yc, yl = mla_mixer(hc, hl, rope, ctx_out, mla_w_in[j], mla_q_norm_g[j], mla_kv_norm_g[j],
                               mla_w_uq[j], mla_w_ukv[j], mla_w_o[j])
        elif m == 1:
            filt = (hy_f_w1[j], hy_f_b1[j], hy_f_freq[j], hy_f_w2[j], hy_f_b2[j], hy_f_w3[j])
            hy_args = (hy_w_in[j], hy_b_in[j], hy_conv_w[j], hy_conv_b[j], filt, hy_skip[j], hy_w_out[j], hy_b_out[j])
            yl = hyena_seq(hl, *hy_args)
            yc = hyena_seq(hc, *hy_args) if ctx_out else None
        elif m == 2:
            yc, yl = s5_mixer(hc, hl, ctx_out, s5_lambda_re[j], s5_lambda_im[j], s5_log_dt[j], s5_b_re[j],
                              s5_b_im[j], s5_c_re[j], s5_c_im[j], s5_d[j], s5_w_glu[j], s5_b_glu[j])
        else:
            yl = fnet_seq(hl, fn_w_o[j], fn_b_o[j])
            yc = fnet_seq(hc, fn_w_o[j], fn_b_o[j]) if ctx_out else None

        xl = xl + mod_l[2] * rmsnorm(yl, mix_post_g[i])
        hl = modulate(rmsnorm(xl, ffn_pre_g[i]), mod_l[3], mod_l[4])
        xl = xl + mod_l[5] * rmsnorm(swiglu(hl, ffn_w13[i], ffn_w2[i]), ffn_post_g[i])
        if ctx_out:
            xc = xc + mod_c[2] * rmsnorm(yc, mix_post_g[i])
            hc = modulate(rmsnorm(xc, ffn_pre_g[i]), mod_c[3], mod_c[4])
            xc = xc + mod_c[5] * rmsnorm(swiglu(hc, ffn_w13[i], ffn_w2[i]), ffn_post_g[i])
    return xl
```

```python
import functools
import math

import numpy as np
import jax
import jax.numpy as jnp
from jax import lax
from jax.experimental import pallas as pl
from jax.experimental.pallas import tpu as pltpu

F32 = jnp.float32
BF16 = jnp.bfloat16
NORM_EPS = 1e-6
LANE = 128
VMEM_LIMIT = 56 * 1024 * 1024
HIGHEST = lax.Precision.HIGHEST

GRID_W = 64
ROPE_THETA = 10000.0
MLA_HEADS = 8
MLA_NOPE = 128
MLA_ROPE = 64
MLA_V = 128
HYENA_BANDS = 16
HYENA_TARGET = 1e-2
HYENA_FAST = 0.3
HYENA_SLOW = 1.5
S5_GROUP = 16
S5_STATE = 64
S5_T = 16
FNET_GC = 128
FFT_N2 = 128


def _cp(*sem):
    return pltpu.CompilerParams(dimension_semantics=sem, vmem_limit_bytes=VMEM_LIMIT)


def _dot(a, b):
    return jnp.dot(a, b, preferred_element_type=F32)


def _dot_hi(a, b):
    return jnp.dot(a, b, preferred_element_type=F32, precision=HIGHEST)


def _rms(x, g):
    ms = jnp.mean(x * x, axis=-1, keepdims=True)
    return x * lax.rsqrt(ms + NORM_EPS) * g


def _normmod(x, g, shift, scale):
    return _rms(x, g) * (1.0 + scale) + shift


def _const_spec(shape):
    nd = len(shape)
    return pl.BlockSpec(shape, lambda *_: (0,) * nd)


def _mods_kernel(st_ref, w_ref, b_ref, o_ref):
    st = st_ref[...]
    st = st * jax.nn.sigmoid(st)
    w = w_ref[0]
    rows = [jnp.sum(st[:, r:r + 1] * w, axis=0, keepdims=True) for r in range(3)]
    rows.append(jnp.zeros((5, w.shape[1]), F32))
    o_ref[0] = jnp.concatenate(rows, axis=0) + b_ref[0]


def _mods(c, c_ctx, mod_w, mod_b):
    depth, d, n6 = mod_w.shape
    st = jnp.zeros((d, 8), F32).at[:, 0:2].set(c.T).at[:, 2].set(c_ctx)
    tn = 1024
    out = pl.pallas_call(
        _mods_kernel,
        grid=(depth, n6 // tn),
        in_specs=[_const_spec((d, 8)),
                  pl.BlockSpec((1, d, tn), lambda i, j: (i, 0, j)),
                  pl.BlockSpec((1, 1, tn), lambda i, j: (i, 0, j))],
        out_specs=pl.BlockSpec((1, 8, tn), lambda i, j: (i, 0, j)),
        out_shape=jax.ShapeDtypeStruct((depth, 8, n6), F32),
        compiler_params=_cp("parallel", "parallel"),
        name="mods",
    )(st, mod_w, mod_b.reshape(depth, 1, n6))
    m = out[:, :3].reshape(depth, 3, n6 // d, d)
    return jnp.pad(m, ((0, 0), (0, 0), (0, 8 - n6 // d), (0, 0)))


def _row_specs(tm, d, tpb):
    x_spec = pl.BlockSpec((tm, d), lambda i: (i, 0))
    mod_spec = pl.BlockSpec((1, 8, d), lambda i: (i // tpb, 0, 0))
    return x_spec, mod_spec


def _normmod_kernel(x_ref, mod_ref, g_ref, o_ref):
    h = _normmod(x_ref[...], g_ref[...], mod_ref[0, 0:1, :], mod_ref[0, 1:2, :])
    o_ref[...] = h.astype(o_ref.dtype)


def _normmod_call(x, mods, g, tm, rows_per_batch):
    m, d = x.shape
    x_spec, mod_spec = _row_specs(tm, d, rows_per_batch // tm)
    return pl.pallas_call(
        _normmod_kernel, grid=(m // tm,),
        in_specs=[x_spec, mod_spec, _const_spec((1, d))],
        out_specs=x_spec, out_shape=jax.ShapeDtypeStruct((m, d), BF16),
        compiler_params=_cp("parallel"), name="normmod",
    )(x, mods, g.reshape(1, d))


def _post_kernel(x_ref, y_ref, mod_ref, w_ref, b_ref, g_ref, o_ref, *, glu):
    z = _dot(y_ref[...].astype(BF16), w_ref[...]) + b_ref[...]
    if glu:
        d = o_ref.shape[-1]
        z = z[:, :d] * jax.nn.sigmoid(z[:, d:])
    o_ref[...] = x_ref[...] + mod_ref[0, 2:3, :] * _rms(z, g_ref[...])


def _post_call(x, y, mods, w, b, g, tm, rows_per_batch, glu=False):
    m, d = x.shape
    k, n = w.shape
    x_spec, mod_spec = _row_specs(tm, d, rows_per_batch // tm)
    return pl.pallas_call(
        functools.partial(_post_kernel, glu=glu), grid=(m // tm,),
        in_specs=[x_spec, pl.BlockSpec((tm, k), lambda i: (i, 0)), mod_spec,
                  _const_spec((k, n)), _const_spec((1, n)), _const_spec((1, d))],
        out_specs=x_spec, out_shape=jax.ShapeDtypeStruct((m, d), F32),
        compiler_params=_cp("parallel"), name="post",
    )(x, y, mods, w, b.reshape(1, n), g.reshape(1, d))


def _ffn_kernel(x_ref, mod_ref, pre_ref, post_ref, w13_ref, w2_ref, o_ref, *, f, fc):
    x = x_ref[...]
    h = _normmod(x, pre_ref[...], mod_ref[0, 3:4, :], mod_ref[0, 4:5, :]).astype(BF16)
    acc = None
    for c in range(f // fc):
        a = _dot(h, w13_ref[:, c * fc:(c + 1) * fc])
        b = _dot(h, w13_ref[:, f + c * fc:f + (c + 1) * fc])
        gact = (a * jax.nn.sigmoid(a) * b).astype(BF16)
        part = _dot(gact, w2_ref[c * fc:(c + 1) * fc, :])
        acc = part if acc is None else acc + part
    o_ref[...] = x + mod_ref[0, 5:6, :] * _rms(acc, post_ref[...])


def _ffn_call(x, mods, pre_g, post_g, w13, w2, tm, rows_per_batch):
    m, d = x.shape
    f = w2.shape[0]
    fc = f // 2 if (f // 2) % LANE == 0 else f
    x_spec, mod_spec = _row_specs(tm, d, rows_per_batch // tm)
    return pl.pallas_call(
        functools.partial(_ffn_kernel, f=f, fc=fc), grid=(m // tm,),
        in_specs=[x_spec, mod_spec, _const_spec((1, d)), _const_spec((1, d)),
                  pl.BlockSpec((d, 2 * f), lambda i: (0, 0), pipeline_mode=pl.Buffered(1)),
                  pl.BlockSpec((f, d), lambda i: (0, 0), pipeline_mode=pl.Buffered(1))],
        out_specs=x_spec, out_shape=jax.ShapeDtypeStruct((m, d), F32),
        compiler_params=_cp("parallel"), name="ffn",
    )(x, mods, pre_g.reshape(1, d), post_g.reshape(1, d), w13, w2)


def _mla_in_kernel(x_ref, mod_ref, g_ref, w_ref, qg_ref, kvg_ref, t1_ref, t2_ref,
                   qn_ref, cn_ref, kr_ref, *, ql, kvl):
    h = _normmod(x_ref[...], g_ref[...], mod_ref[0, 0:1, :], mod_ref[0, 1:2, :])
    z = _dot(h.astype(BF16), w_ref[...])
    qn_ref[...] = _rms(z[:, :ql], qg_ref[...]).astype(BF16)
    cn_ref[...] = _rms(z[:, ql:ql + kvl], kvg_ref[...]).astype(BF16)
    pair = z[:, ql + kvl:]
    kr_ref[...] = (pair * t1_ref[...] + pltpu.roll(pair, 64, 1) * t2_ref[...]).astype(BF16)


def _mla_q_kernel(qn_ref, w_ref, t1_ref, t2_ref, q_ref, *, qscale):
    z = _dot(qn_ref[...], w_ref[...])
    t1 = t1_ref[...]
    t2 = t2_ref[...]
    for h in range(MLA_HEADS):
        base = h * 256
        q_ref[0, h, :, 0:128] = (z[:, base:base + 128] * qscale).astype(BF16)
        pair = z[:, base + 128:base + 256]
        rp = pair * t1 + pltpu.roll(pair, 64, 1) * t2
        q_ref[0, h, :, 128:256] = (rp * qscale).astype(BF16)


def _mla_kv_kernel(cn_ref, kr_ref, w_ref, kt_ref, v_ref):
    z = _dot(cn_ref[...], w_ref[...])
    krt = kr_ref[...].astype(F32).T.astype(BF16)
    for h in range(MLA_HEADS):
        base = h * 256
        kt_ref[0, h, 0, 0:128, :] = z[:, base:base + 128].T.astype(BF16)
        kt_ref[0, h, 0, 128:256, :] = krt
        v_ref[0, h, 0] = z[:, base + 128:base + 256].astype(BF16)


def _flash_kernel(q_ref, kc_ref, vc_ref, *rest, n_lat):
    if n_lat:
        kl_ref, vl_ref, o_ref = rest
    else:
        (o_ref,) = rest
    q = q_ref[0, 0]
    tq = q.shape[0]

    def step(kt, v, carry):
        m, l, acc = carry
        s = _dot(q, kt)
        m_new = jnp.maximum(m, jnp.max(s, axis=-1, keepdims=True))
        alpha = jnp.exp2(m - m_new)
        p = jnp.exp2(s - m_new)
        l = alpha * l + jnp.sum(p, axis=-1, keepdims=True)
        acc = alpha * acc + _dot(p.astype(BF16), v)
        return m_new, l, acc

    carry = (jnp.full((tq, 1), -jnp.inf, F32), jnp.zeros((tq, 1), F32), jnp.zeros((tq, MLA_V), F32))
    carry = step(kc_ref[0, 0, 0], vc_ref[0, 0, 0], carry)
    if n_lat:
        carry = lax.fori_loop(0, n_lat, lambda c, cr: step(kl_ref[0, 0, c], vl_ref[0, 0, c], cr), carry)
    _, l, acc = carry
    o_ref[0] = (acc / l).astype(o_ref.dtype)


def _rope_tables(n_lat):
    rows = n_lat // GRID_W
    row = jnp.repeat(jnp.arange(rows, dtype=F32), GRID_W)
    col = jnp.tile(jnp.arange(GRID_W, dtype=F32), rows)
    axis_dim = MLA_ROPE // 2
    inv_freq = 1.0 / (ROPE_THETA ** (jnp.arange(0, axis_dim, 2, dtype=F32) / axis_dim))
    ang_r = row[:, None] * inv_freq
    ang_c = col[:, None] * inv_freq
    cr, sr, cc, sc = jnp.cos(ang_r), jnp.sin(ang_r), jnp.cos(ang_c), jnp.sin(ang_c)
    cp = jnp.concatenate([cr, cr, cc, cc], axis=-1)
    sp = jnp.concatenate([-sr, sr, -sc, sc], axis=-1)
    return cp, sp


_ROPE_SWAP = np.concatenate([np.arange(16, 32), np.arange(0, 16), np.arange(48, 64), np.arange(32, 48)])


def _mla_side(x, mods, pre_g, w_in_ext, q_g, kv_g, w_uq_ext, w_ukv, tabs, b, n, tm, tk, want_q):
    m, d = x.shape
    ql, kvl = q_g.shape[-1], kv_g.shape[-1]
    t1k, t2k, t1q, t2q = tabs
    x_spec, mod_spec = _row_specs(tm, d, n // tm)
    row = lambda w: pl.BlockSpec((tm, w), lambda i: (i, 0))
    tab_spec = pl.BlockSpec((tm, 128), lambda i: (i % (n // tm), 0))
    qn, cn, kr = pl.pallas_call(
        functools.partial(_mla_in_kernel, ql=ql, kvl=kvl), grid=(m // tm,),
        in_specs=[x_spec, mod_spec, _const_spec((1, d)), _const_spec(w_in_ext.shape),
                  _const_spec((1, ql)), _const_spec((1, kvl)), tab_spec, tab_spec],
        out_specs=[row(ql), row(kvl), row(128)],
        out_shape=[jax.ShapeDtypeStruct((m, ql), BF16), jax.ShapeDtypeStruct((m, kvl), BF16),
                   jax.ShapeDtypeStruct((m, 128), BF16)],
        compiler_params=_cp("parallel"), name="mla_in",
    )(x, mods, pre_g.reshape(1, d), w_in_ext, q_g.reshape(1, ql), kv_g.reshape(1, kvl), t1k, t2k)

    nc = n // tk
    kt, v = pl.pallas_call(
        _mla_kv_kernel, grid=(b, nc),
        in_specs=[pl.BlockSpec((tk, kvl), lambda bi, c: (bi * nc + c, 0)),
                  pl.BlockSpec((tk, 128), lambda bi, c: (bi * nc + c, 0)),
                  _const_spec(w_ukv.shape)],
        out_specs=[pl.BlockSpec((1, MLA_HEADS, 1, 256, tk), lambda bi, c: (bi, 0, c, 0, 0)),
                   pl.BlockSpec((1, MLA_HEADS, 1, tk, MLA_V), lambda bi, c: (bi, 0, c, 0, 0))],
        out_shape=[jax.ShapeDtypeStruct((b, MLA_HEADS, nc, 256, tk), BF16),
                   jax.ShapeDtypeStruct((b, MLA_HEADS, nc, tk, MLA_V), BF16)],
        compiler_params=_cp("parallel", "parallel"), name="mla_kv",
    )(cn, kr, w_ukv)

    q = None
    if want_q:
        qscale = (MLA_NOPE + MLA_ROPE) ** -0.5 * math.log2(math.e)
        tpb = n // tm
        q = pl.pallas_call(
            functools.partial(_mla_q_kernel, qscale=qscale), grid=(m // tm,),
            in_specs=[row(ql), _const_spec(w_uq_ext.shape), tab_spec, tab_spec],
            out_specs=pl.BlockSpec((1, MLA_HEADS, tm, 256), lambda i: (i // tpb, 0, i % tpb, 0)),
            out_shape=jax.ShapeDtypeStruct((b, MLA_HEADS, n, 256), BF16),
            compiler_params=_cp("parallel"), name="mla_q",
        )(qn, w_uq_ext, t1q, t2q)
    return q, kt, v


def _flash_call(q, ktc, vc, ktl, vl, tq):
    b, hh, n, _ = q.shape
    c = ktc.shape[-1]
    n_lat = 0 if ktl is None else ktl.shape[2]
    in_specs = [pl.BlockSpec((1, 1, tq, 256), lambda bi, h, i: (bi, h, i, 0)),
                pl.BlockSpec((1, 1, 1, 256, c), lambda bi, h, i: (bi, h, 0, 0, 0)),
                pl.BlockSpec((1, 1, 1, c, MLA_V), lambda bi, h, i: (bi, h, 0, 0, 0))]
    args = [q, ktc, vc]
    if n_lat:
        tk = ktl.shape[-1]
        in_specs += [pl.BlockSpec((1, 1, n_lat, 256, tk), lambda bi, h, i: (bi, h, 0, 0, 0)),
                     pl.BlockSpec((1, 1, n_lat, tk, MLA_V), lambda bi, h, i: (bi, h, 0, 0, 0))]
        args += [ktl, vl]
    return pl.pallas_call(
        functools.partial(_flash_kernel, n_lat=n_lat), grid=(b, hh, n // tq),
        in_specs=in_specs,
        out_specs=pl.BlockSpec((1, tq, MLA_V), lambda bi, h, i: (bi, i, h)),
        out_shape=jax.ShapeDtypeStruct((b, n, hh * MLA_V), BF16),
        compiler_params=_cp("parallel", "parallel", "arbitrary"), name="flash",
    )(*args)


def _mla_layer(xl, xc, mods_l, mods_c, pre_g, post_g, w_in, q_g, kv_g, w_uq, w_ukv, w_o, b, n, c):
    d = xl.shape[-1]
    ql, kvl = q_g.shape[-1], kv_g.shape[-1]
    hh = MLA_HEADS
    rope_cols = w_in[:, ql + kvl:]
    w_in_ext = jnp.concatenate([w_in, rope_cols[:, _ROPE_SWAP]], axis=1).astype(BF16)
    wq = w_uq.reshape(ql, hh, MLA_NOPE + MLA_ROPE)
    w_uq_ext = jnp.concatenate([wq, wq[:, :, MLA_NOPE:][:, :, _ROPE_SWAP]], axis=-1)
    w_uq_ext = w_uq_ext.reshape(ql, hh * 256).astype(BF16)
    w_ukv_b = w_ukv.astype(BF16)
    w_o_b = w_o.astype(BF16)

    cp, sp = _rope_tables(n)
    z64l, o64l = jnp.zeros((n, 64), F32), jnp.ones((n, 64), F32)
    z64c, o64c = jnp.zeros((c, 64), F32), jnp.ones((c, 64), F32)
    cat = lambda a, bb: jnp.concatenate([a, bb], axis=1)
    tabs_l = (cat(cp, z64l), cat(sp, z64l), cat(cp, z64l), cat(sp, o64l))
    tabs_c = (cat(z64c, z64c), cat(z64c, o64c), cat(z64c, z64c), cat(z64c, o64c))

    tm_l = min(512, n)
    tk_l = min(1024, n)
    ql_, ktl, vl = _mla_side(xl, mods_l, pre_g, w_in_ext, q_g, kv_g, w_uq_ext, w_ukv_b, tabs_l,
                             b, n, tm_l, tk_l, True)
    qc_, ktc, vc = _mla_side(xc, mods_c, pre_g, w_in_ext, q_g, kv_g, w_uq_ext, w_ukv_b, tabs_c,
                             b, c, c, c, True)
    o_lat = _flash_call(ql_, ktc, vc, ktl, vl, min(512, n)).reshape(b * n, hh * MLA_V)
    o_ctx = _flash_call(qc_, ktc, vc, None, None, c).reshape(b * c, hh * MLA_V)
    zb = jnp.zeros((d,), F32)
    xl = _post_call(xl, o_lat, mods_l, w_o_b, zb, post_g, tm_l, n)
    xc = _post_call(xc, o_ctx, mods_c, w_o_b, zb, post_g, c, c)
    return xl, xc


def _hy_in_kernel(x_ref, xp_ref, xn_ref, mod_ref, g_ref, w_ref, b_ref, cw_ref, cb_ref,
                  g0_ref, vg_ref, *, tpb):
    i = pl.program_id(0)
    g = g_ref[...]
    shift, scale = mod_ref[0, 0:1, :], mod_ref[0, 1:2, :]
    w = w_ref[...]
    bias = b_ref[...]
    u = _dot(_normmod(x_ref[...], g, shift, scale).astype(BF16), w) + bias
    up = _dot(_normmod(xp_ref[...], g, shift, scale).astype(BF16), w) + bias
    un = _dot(_normmod(xn_ref[...], g, shift, scale).astype(BF16), w) + bias
    first = (i % tpb) == 0
    last = (i % tpb) == tpb - 1
    prev_row = jnp.where(first, 0.0, up[7:8, :])
    next_row = jnp.where(last, 0.0, un[0:1, :])
    tm = u.shape[0]
    ridx = lax.broadcasted_iota(jnp.int32, (tm, 1), 0)
    dn = jnp.where(ridx == 0, prev_row, pltpu.roll(u, 1, 0))
    upw = jnp.where(ridx == tm - 1, next_row, pltpu.roll(u, tm - 1, 0))
    conv = cb_ref[...] + dn * cw_ref[0:1, :] + u * cw_ref[1:2, :] + upw * cw_ref[2:3, :]
    d = g0_ref.shape[-1]
    g0_ref[...] = conv[:, :d]
    vg_ref[...] = conv[:, 2 * d:] * conv[:, d:2 * d]


def _hy_in_call(x, mods, pre_g, w_in, b_in, conv_w, conv_b, tm, n):
    m, d = x.shape
    p = w_in.shape[1]
    tpb = n // tm
    x_spec, mod_spec = _row_specs(tm, d, tpb)
    r8 = tm // 8
    nb8 = m // 8
    prev_spec = pl.BlockSpec((8, d), lambda i: (jnp.maximum(i * r8 - 1, 0), 0))
    next_spec = pl.BlockSpec((8, d), lambda i: (jnp.minimum((i + 1) * r8, nb8 - 1), 0))
    cw = jnp.pad(conv_w, ((0, 8 - conv_w.shape[0]), (0, 0)))
    return pl.pallas_call(
        functools.partial(_hy_in_kernel, tpb=tpb), grid=(m // tm,),
        in_specs=[x_spec, prev_spec, next_spec, mod_spec, _const_spec((1, d)), _const_spec((d, p)),
                  _const_spec((1, p)), _const_spec((8, p)), _const_spec((1, p))],
        out_specs=[x_spec, x_spec],
        out_shape=[jax.ShapeDtypeStruct((m, d), F32), jax.ShapeDtypeStruct((m, d), F32)],
        compiler_params=_cp("parallel"), name="hy_in",
    )(x, x, x, mods, pre_g.reshape(1, d), w_in, b_in.reshape(1, p), cw, conv_b.reshape(1, p))


def _hy_filter_kernel(bands_ref, w1_ref, b1_ref, fq_ref, w2_ref, b2_ref, w3_ref, dl_ref,
                      kf_ref, gb_ref, nrm_ref, *, n, tr):
    i = pl.program_id(0)
    j = (lax.broadcasted_iota(jnp.int32, (tr, LANE), 0) + i * tr).astype(F32)
    lane = lax.broadcasted_iota(jnp.int32, (tr, LANE), 1)
    t = j * (1.0 / (n - 1))
    arg = (2.0 * math.pi / n) * j * bands_ref[...]
    z = jnp.where(lane == 0, t,
                  jnp.where(lane <= HYENA_BANDS, jnp.cos(arg),
                            jnp.where(lane <= 2 * HYENA_BANDS, -jnp.sin(arg), 0.0)))
    fq = fq_ref[...]
    a = jnp.sin(fq * (_dot_hi(z, w1_ref[...]) + b1_ref[...]))
    for k in range(w2_ref.shape[0]):
        a = jnp.sin(fq * (_dot_hi(a, w2_ref[k]) + b2_ref[k]))
    h = _dot_hi(a, w3_ref[...])
    d = kf_ref.shape[-1]
    decay = jnp.exp(-t[:, 0:1] * dl_ref[...])
    kf = h[:, :d] * decay
    gb = jnp.where(j[:, 0:1] == 0.0, 0.0, h[:, d:] * decay)
    kf_ref[...] = kf
    gb_ref[...] = gb
    part = jnp.sum(jnp.abs(kf) + jnp.abs(gb), axis=0, keepdims=True)

    @pl.when(i == 0)
    def _():
        nrm_ref[...] = jnp.zeros_like(nrm_ref)

    nrm_ref[...] += jnp.broadcast_to(part, nrm_ref.shape)


def _hy_filter_call(n, d, f_w1, f_b1, f_freq, f_w2, f_b2, f_w3):
    fw = f_w1.shape[1]
    tr = min(512, n)
    bands_np = np.zeros((1, LANE), np.float32)
    bands_np[0, 1:1 + HYENA_BANDS] = np.linspace(1e-4, HYENA_BANDS - 1, HYENA_BANDS, dtype=np.float32)
    bands_np[0, 1 + HYENA_BANDS:1 + 2 * HYENA_BANDS] = bands_np[0, 1:1 + HYENA_BANDS]
    w1p = jnp.zeros((LANE, fw), F32).at[:f_w1.shape[0]].set(f_w1)
    deltas = jnp.abs(jnp.linspace(math.log(HYENA_TARGET) / HYENA_SLOW, math.log(HYENA_TARGET) / HYENA_FAST,
                                  d, dtype=F32)).reshape(1, d)
    row = pl.BlockSpec((tr, d), lambda i: (i, 0))
    return pl.pallas_call(
        functools.partial(_hy_filter_kernel, n=n, tr=tr), grid=(n // tr,),
        in_specs=[_const_spec((1, LANE)), _const_spec((LANE, fw)), _const_spec((1, fw)), _const_spec((1, fw)),
                  _const_spec(f_w2.shape), _const_spec((f_w2.shape[0], 1, fw)), _const_spec(f_w3.shape),
                  _const_spec((1, d))],
        out_specs=[row, row, _const_spec((8, d))],
        out_shape=[jax.ShapeDtypeStruct((n, d), F32), jax.ShapeDtypeStruct((n, d), F32),
                   jax.ShapeDtypeStruct((8, d), F32)],
        compiler_params=_cp("arbitrary"), name="hy_filter",
    )(jnp.asarray(bands_np), w1p, f_b1.reshape(1, fw), f_freq.reshape(1, fw), f_w2,
      f_b2.reshape(f_w2.shape[0], 1, fw), f_w3, deltas)


def _dft_cs(nf, nt, period):
    ft = (np.arange(nf)[:, None] * np.arange(nt)[None, :]) % period
    ang = 2.0 * np.pi * ft / period
    return np.cos(ang), np.sin(ang)


def _twiddle_tables(n1, n2, lead_t2):
    nn = n1 * n2
    f1 = jnp.arange(n1, dtype=jnp.int32)
    t2 = jnp.arange(n2, dtype=jnp.int32)
    idx = (t2[:, None] * f1[None, :]) % nn if lead_t2 else (f1[:, None] * t2[None, :]) % nn
    ang = idx.astype(F32) * (2.0 * math.pi / nn)
    shape = idx.shape + (LANE,)
    return (jnp.broadcast_to(jnp.cos(ang)[..., None], shape),
            jnp.broadcast_to(jnp.sin(ang)[..., None], shape))


def _fft_a_kernel(*refs, n_in, nt, d):
    fm_ref = refs[0]
    in_refs = refs[1:1 + n_in]
    twc_ref, tws_ref, or_ref, oi_ref = refs[1 + n_in:]
    xs = [r[0] for r in in_refs]
    x = (jnp.concatenate(xs, axis=0) if n_in > 1 else xs[0]).astype(BF16)
    out = _dot(fm_ref[...], x)
    n1 = out.shape[0] // 2
    xr, xi = out[:n1], out[n1:]
    for s in range(nt):
        c = jnp.concatenate([twc_ref[s]] * (d // LANE), axis=1)
        sn = jnp.concatenate([tws_ref[s]] * (d // LANE), axis=1)
        a = xr[:, s * d:(s + 1) * d]
        bb = xi[:, s * d:(s + 1) * d]
        or_ref[:, s * d:(s + 1) * d] = (a * c + bb * sn).astype(or_ref.dtype)
        oi_ref[:, s * d:(s + 1) * d] = (bb * c - a * sn).astype(oi_ref.dtype)


def _fft_a_call(fm, ins, twc, tws, n1, n2, d, nt):
    rows = ins[0][0].shape[1]
    in_specs = [_const_spec(fm.shape)]
    args = [fm]
    for arr, bi in ins:
        in_specs.append(pl.BlockSpec((1, rows, nt * d), lambda j, bi=bi: (bi, 0, j)))
        args.append(arr)
    tw_spec = pl.BlockSpec((nt, n1, LANE), lambda j: (j, 0, 0))
    out_spec = pl.BlockSpec((n1, nt * d), lambda j: (0, j))
    return pl.pallas_call(
        functools.partial(_fft_a_kernel, n_in=len(ins), nt=nt, d=d), grid=(n2 // nt,),
        in_specs=in_specs + [tw_spec, tw_spec],
        out_specs=[out_spec, out_spec],
        out_shape=[jax.ShapeDtypeStruct((n1, n2 * d), BF16)] * 2,
        compiler_params=_cp("parallel"), name="fft_a",
    )(*args, twc, tws)


def _hy_kb_kernel(fm_ref, fr_ref, fi_ref, gr_ref, gi_ref, sc_ref, kr_ref, ki_ref, *, nf):
    fm = fm_ref[...]
    n2 = fr_ref.shape[1]
    sc = sc_ref[...]
    for s in range(nf):
        kf = _dot(fm, jnp.concatenate([fr_ref[s], fi_ref[s]], axis=0))
        gb = _dot(fm, jnp.concatenate([gr_ref[s], gi_ref[s]], axis=0))
        kr_ref[s] = (kf[:n2] + gb[:n2]) * sc
        ki_ref[s] = (kf[n2:] - gb[n2:]) * sc


def _hy_b_kernel(fm_ref, fmc_ref, xr_ref, xi_ref, kr_ref, ki_ref, twc_ref, tws_ref,
                 or_ref, oi_ref, *, nf, d):
    fm = fm_ref[...]
    fmc = fmc_ref[...]
    n2 = xr_ref.shape[1]
    for s in range(nf):
        x = _dot(fm, jnp.concatenate([xr_ref[s], xi_ref[s]], axis=0))
        xr, xi = x[:n2], x[n2:]
        kr, ki = kr_ref[s], ki_ref[s]
        yr = (xr * kr - xi * ki).astype(BF16)
        yi = (xr * ki + xi * kr).astype(BF16)
        g = _dot(fmc, jnp.concatenate([yr, yi], axis=0))
        gr, gi = g[:n2], g[n2:]
        c = jnp.concatenate([twc_ref[s]] * (d // LANE), axis=1)
        sn = jnp.concatenate([tws_ref[s]] * (d // LANE), axis=1)
        or_ref[s] = (gr * c - gi * sn).astype(or_ref.dtype)
        oi_ref[s] = (gi * c + gr * sn).astype(oi_ref.dtype)


def _hy_c_kernel(fm_ref, gr_ref, gi_ref, vg_ref, g0_ref, skip_ref, o_ref):
    y = _dot(fm_ref[...], jnp.concatenate([gr_ref[...], gi_ref[...]], axis=0))
    half = y.shape[0] // 2
    skip = skip_ref[...]
    for bi in range(2):
        yb = y[bi * half:(bi + 1) * half]
        o_ref[bi] = ((yb + vg_ref[bi] * skip) * g0_ref[bi]).astype(o_ref.dtype)


def _block_c(cs, sn, sign):
    return np.block([[cs, -sign * sn], [sign * sn, cs]])


def _hy_conv_long(vg, g0, kf, gb, nrm, skip, b, n, d):
    n2 = FFT_N2
    nn = 2 * n
    n1 = nn // n2
    rows = n // n2
    cs1, sn1 = _dft_cs(n1, rows, n1)
    fm_a = jnp.asarray(_block_c(cs1, sn1, -1.0), BF16)
    fm_a_real = jnp.asarray(np.concatenate([cs1, -sn1], axis=0), BF16)
    cs2, sn2 = _dft_cs(n2, n2, n2)
    fm_b = jnp.asarray(_block_c(cs2, sn2, -1.0), BF16)
    fm_bc = jnp.asarray(_block_c(cs2, sn2, 1.0), BF16)
    cs1i, sn1i = _dft_cs(rows, n1, n1)
    fm_c = jnp.asarray(_block_c(cs1i, sn1i, 1.0), BF16)
    twc_a, tws_a = _twiddle_tables(n1, n2, lead_t2=True)
    twc_b, tws_b = _twiddle_tables(n1, n2, lead_t2=False)

    nt = 2 if n2 % 2 == 0 else 1
    nf = 4 if n1 % 4 == 0 else 1
    kf3 = kf.reshape(1, rows, n2 * d)
    gb3 = gb.reshape(1, rows, n2 * d)
    kfr, kfi = _fft_a_call(fm_a_real, [(kf3, 0)], twc_a, tws_a, n1, n2, d, nt)
    gbr, gbi = _fft_a_call(fm_a_real, [(gb3, 0)], twc_a, tws_a, n1, n2, d, nt)
    scale = (1.0 / (nrm[0:1, :] * nn))
    slab = pl.BlockSpec((nf, n2, d), lambda j: (j, 0, 0))
    shp3 = (n1, n2, d)
    khr, khi = pl.pallas_call(
        functools.partial(_hy_kb_kernel, nf=nf), grid=(n1 // nf,),
        in_specs=[_const_spec(fm_b.shape), slab, slab, slab, slab, _const_spec((1, d))],
        out_specs=[slab, slab],
        out_shape=[jax.ShapeDtypeStruct(shp3, F32)] * 2,
        compiler_params=_cp("parallel"), name="hy_kb",
    )(fm_b, kfr.reshape(shp3), kfi.reshape(shp3), gbr.reshape(shp3), gbi.reshape(shp3), scale)

    vg3 = vg.reshape(b, rows, n2 * d)
    g03 = g0.reshape(b, rows, n2 * d)
    x1r, x1i = _fft_a_call(fm_a, [(vg3, 0), (vg3, 1)], twc_a, tws_a, n1, n2, d, nt)
    tw_slab = pl.BlockSpec((nf, n2, LANE), lambda j: (j, 0, 0))
    g1r, g1i = pl.pallas_call(
        functools.partial(_hy_b_kernel, nf=nf, d=d), grid=(n1 // nf,),
        in_specs=[_const_spec(fm_b.shape), _const_spec(fm_bc.shape), slab, slab, slab, slab,
                  tw_slab, tw_slab],
        out_specs=[slab, slab],
        out_shape=[jax.ShapeDtypeStruct(shp3, BF16)] * 2,
        compiler_params=_cp("parallel"), name="hy_b",
    )(fm_b, fm_bc, x1r.reshape(shp3), x1i.reshape(shp3), khr, khi, twc_b, tws_b)

    cb = nt * d
    col = pl.BlockSpec((n1, cb), lambda j: (0, j))
    col3 = pl.BlockSpec((b, rows, cb), lambda j: (0, 0, j))
    out = pl.pallas_call(
        _hy_c_kernel, grid=(n2 * d // cb,),
        in_specs=[_const_spec(fm_c.shape), col, col, col3, col3, _const_spec((1, cb))],
        out_specs=col3,
        out_shape=jax.ShapeDtypeStruct((b, rows, n2 * d), BF16),
        compiler_params=_cp("parallel"), name="hy_c",
    )(fm_c, g1r.reshape(n1, n2 * d), g1i.reshape(n1, n2 * d), vg3, g03, jnp.tile(skip.reshape(1, d), (1, nt)))
    return out.reshape(b * n, d)


def _hy_short_kernel(fa_ref, fk_ref, fi_ref, vg_ref, g0_ref, kf_ref, gb_ref, nrm_ref, skip_ref, o_ref, *, n):
    z = jnp.concatenate([vg_ref[0], vg_ref[1]], axis=0)
    x = _dot_hi(fa_ref[...], z)
    kk = _dot_hi(fk_ref[...], jnp.concatenate([kf_ref[...], gb_ref[...]], axis=0))
    nn = 2 * n
    sc = 1.0 / (nrm_ref[0:1, :] * nn)
    xr, xi = x[:nn], x[nn:]
    kr, ki = kk[:nn] * sc, kk[nn:] * sc
    y = _dot_hi(fi_ref[...], jnp.concatenate([xr * kr - xi * ki, xr * ki + xi * kr], axis=0))
    skip = skip_ref[...]
    for bi in range(2):
        o_ref[bi] = ((y[bi * n:(bi + 1) * n] + vg_ref[bi] * skip) * g0_ref[bi]).astype(o_ref.dtype)


def _hy_conv_short(vg, g0, kf, gb, nrm, skip, b, n, d):
    nn = 2 * n
    cs, sn = _dft_cs(nn, n, nn)
    fa = jnp.asarray(_block_c(cs, sn, -1.0), F32)
    fk = jnp.asarray(np.block([[cs, cs], [-sn, sn]]), F32)
    csi, sni = _dft_cs(n, nn, nn)
    fi = jnp.asarray(_block_c(csi, sni, 1.0), F32)
    cb = 256
    col3 = pl.BlockSpec((b, n, cb), lambda j: (0, 0, j))
    col = pl.BlockSpec((n, cb), lambda j: (0, j))
    vec = pl.BlockSpec((1, cb), lambda j: (0, j))
    out = pl.pallas_call(
        functools.partial(_hy_short_kernel, n=n), grid=(d // cb,),
        in_specs=[_const_spec(fa.shape), _const_spec(fk.shape), _const_spec(fi.shape), col3, col3, col, col,
                  pl.BlockSpec((8, cb), lambda j: (0, j)), vec],
        out_specs=col3, out_shape=jax.ShapeDtypeStruct((b, n, d), BF16),
        compiler_params=_cp("parallel"), name="hy_short",
    )(fa, fk, fi, vg.reshape(b, n, d), g0.reshape(b, n, d), kf, gb, nrm, skip.reshape(1, d))
    return out.reshape(b * n, d)


def _hyena_layer(xl, xc, mods_l, mods_c, pre_g, post_g, w_in, b_in, conv_w, conv_b, filt, skip,
                 w_out, b_out, b, n, c):
    d = xl.shape[-1]
    w_in_b = w_in.astype(BF16)
    w_out_b = w_out.astype(BF16)
    tm = min(512, n)
    g0, vg = _hy_in_call(xl, mods_l, pre_g, w_in_b, b_in, conv_w, conv_b, tm, n)
    kf, gb, nrm = _hy_filter_call(n, d, *filt)
    u_out = _hy_conv_long(vg, g0, kf, gb, nrm, skip[0], b, n, d)
    xl = _post_call(xl, u_out, mods_l, w_out_b, b_out, post_g, tm, n)

    g0c, vgc = _hy_in_call(xc, mods_c, pre_g, w_in_b, b_in, conv_w, conv_b, c, c)
    kfc, gbc, nrmc = _hy_filter_call(c, d, *filt)
    u_out_c = _hy_conv_short(vgc, g0c, kfc, gbc, nrmc, skip[0], b, c, d)
    xc = _post_call(xc, u_out_c, mods_c, w_out_b, b_out, post_g, c, c)
    return xl, xc


def _s5_operators(lam_re, lam_im, log_dt, b_re, b_im, c_re, c_im, d_skip):
    t = S5_T
    g, ns = lam_re.shape[1], lam_re.shape[2]
    gc = b_re.shape[-1]
    gl = LANE // gc
    nblk = g // gl
    lam = lax.complex(lam_re, lam_im)
    dt = jnp.exp(log_dt)[..., None]
    lam_bar = jnp.exp(lam * dt)
    b_bar = ((lam_bar - 1.0) / lam)[..., None] * lax.complex(b_re, b_im)
    c_mat = lax.complex(c_re, c_im)
    pw = jnp.arange(t + 1, dtype=F32)
    lam_pw = jnp.exp((lam * dt)[None] * pw[:, None, None, None])
    hp = HIGHEST
    kern = jnp.einsum('dgcn,tdgn,dgne->dgtce', c_mat, lam_pw[:t], b_bar, precision=hp).real
    s_idx = np.arange(t)[:, None]
    t_idx = np.arange(t)[None, :]
    lag = np.abs(t_idx - s_idx)
    mf = (s_idx <= t_idx).astype(np.float32)
    mb = (s_idx >= t_idx).astype(np.float32)
    kf = kern[0][:, lag] * mf[None, :, :, None, None]
    kb = kern[1][:, lag] * mb[None, :, :, None, None]
    dsk = d_skip.reshape(g, gc)
    diag = (np.eye(t, dtype=np.float32)[None, :, :, None, None]
            * (jnp.eye(gc, dtype=F32)[None] * dsk[:, :, None])[:, None, None, :, :])
    mg = jnp.transpose(kf + kb + diag, (0, 1, 4, 2, 3))
    eye_gl = jnp.eye(gl, dtype=F32)
    mg = mg.reshape(nblk, gl, t, gc, t, gc)
    m_op = jnp.einsum('bgsetc,gh->bsgethc', mg, eye_gl).reshape(nblk, t * LANE, t * LANE)

    pf = lam_pw[:t][::-1][:, 0, :, :, None] * b_bar[0][None]
    pb = lam_pw[:t][:, 1, :, :, None] * b_bar[1][None]
    pcat = jnp.stack([pf, pb], axis=0)
    pri = jnp.stack([pcat.real, pcat.imag], axis=0)
    pri = jnp.transpose(pri, (2, 3, 5, 1, 0, 4))
    pri = pri.reshape(t, nblk, gl, gc, 2, 2, ns)
    p_op = jnp.einsum('jbgedrn,gh->bjgedhrn', pri, eye_gl).reshape(nblk, t * LANE, 2 * gl * 2 * ns)

    qf = c_mat[0][None] * jnp.transpose(lam_pw[1:t + 1, 0], (0, 1, 2))[:, :, None, :]
    qb = c_mat[1][None] * lam_pw[1:t + 1][::-1][:, 1][:, :, None, :]
    qcat = jnp.stack([qf, qb], axis=0)
    qri = jnp.stack([qcat.real, -qcat.imag], axis=0)
    qri = jnp.transpose(qri, (1, 3, 0, 5, 2, 4))
    qri = qri.reshape(2, nblk, gl, 2, ns, t, gc)
    q_op = jnp.einsum('dbgrnjc,gh->bdgrnjhc', qri, eye_gl).reshape(nblk, 2 * gl * 2 * ns, t * LANE)

    a = lam_pw[t]
    a1 = jnp.concatenate([a.real, a.real], axis=-1).reshape(2, g * 2 * ns)
    a2 = jnp.concatenate([-a.imag, a.imag], axis=-1).reshape(2, g * 2 * ns)
    return m_op.astype(BF16), p_op.astype(BF16), q_op.astype(BF16), a1, a2


def _s5_sum_kernel(*refs, t):
    u_refs = refs[:t]
    p_ref, o_ref = refs[t:]
    u = jnp.concatenate([r[...] for r in u_refs], axis=1)
    o_ref[...] = _dot(u, p_ref[0])


def _s5_sum_call(h, p_op, rb):
    rows = h.shape[0]
    t = S5_T
    nblk = p_op.shape[0]
    ns2 = p_op.shape[2]
    u_specs = [pl.BlockSpec((rb, LANE), lambda gb, r, s=s: (r, s * nblk + gb)) for s in range(t)]
    return pl.pallas_call(
        functools.partial(_s5_sum_kernel, t=t), grid=(nblk, rows // rb),
        in_specs=u_specs + [pl.BlockSpec((1,) + p_op.shape[1:], lambda gb, r: (gb, 0, 0))],
        out_specs=pl.BlockSpec((rb, ns2), lambda gb, r: (r, gb)),
        out_shape=jax.ShapeDtypeStruct((rows, nblk * ns2), F32),
        compiler_params=_cp("parallel", "parallel"), name="s5_sum",
    )(*([h] * t), p_op)


def _s5_rec_kernel(sf_ref, sb_ref, a1f_ref, a2f_ref, a1b_ref, a2b_ref, hf0_ref, hb0_ref,
                   hf_ref, hb_ref, ff_ref, fb_ref, *, kb):
    a1f, a2f, a1b, a2b = a1f_ref[...], a2f_ref[...], a1b_ref[...], a2b_ref[...]

    @pl.when(pl.program_id(0) == 0)
    def _():
        ff_ref[...] = hf0_ref[...]
        fb_ref[...] = hb0_ref[...]

    def body(i, carry):
        hf, hb = carry
        k = kb - 1 - i
        hf_ref[i] = hf
        hb_ref[k] = hb
        hf = a1f * hf + a2f * pltpu.roll(hf, 64, 1) + sf_ref[i]
        hb = a1b * hb + a2b * pltpu.roll(hb, 64, 1) + sb_ref[k]
        return hf, hb

    hf, hb = lax.fori_loop(0, kb, body, (ff_ref[...], fb_ref[...]))
    ff_ref[...] = hf
    fb_ref[...] = hb


def _s5_rec_call(sf, sb, a1, a2, hf0, hb0):
    nk, r, _ = sf.shape
    rep = r * LANE // a1.shape[1]
    tile = lambda v: jnp.tile(v.reshape(1, -1), (1, rep)).reshape(r, LANE)
    kb = min(32, nk)
    nb = nk // kb
    fwd3 = pl.BlockSpec((kb, r, LANE), lambda i: (i, 0, 0))
    bwd3 = pl.BlockSpec((kb, r, LANE), lambda i: (nb - 1 - i, 0, 0))
    full2 = _const_spec((r, LANE))
    return pl.pallas_call(
        functools.partial(_s5_rec_kernel, kb=kb), grid=(nb,),
        in_specs=[fwd3, bwd3, full2, full2, full2, full2, full2, full2],
        out_specs=[fwd3, bwd3, full2, full2],
        out_shape=[jax.ShapeDtypeStruct((nk, r, LANE), F32)] * 2 + [jax.ShapeDtypeStruct((r, LANE), F32)] * 2,
        compiler_params=_cp("arbitrary"), name="s5_rec",
    )(sf, sb, tile(a1[0]), tile(a2[0]), tile(a1[1]), tile(a2[1]), hf0, hb0)


def _s5_out_kernel(*refs, t):
    u_refs = refs[:t]
    h_ref, m_ref, q_ref, o_ref = refs[t:]
    u = jnp.concatenate([r[...] for r in u_refs], axis=1)
    y = _dot(u, m_ref[0]) + _dot(h_ref[...].astype(BF16), q_ref[0])
    o_ref[0] = (0.5 * y * (1.0 + lax.erf(y * (2.0 ** -0.5)))).astype(o_ref.dtype)


def _s5_out_call(h, hcat, m_op, q_op, rb):
    rows = h.shape[0]
    t = S5_T
    nblk = m_op.shape[0]
    ns2 = q_op.shape[1]
    u_specs = [pl.BlockSpec((rb, LANE), lambda gb, r, s=s: (r, s * nblk + gb)) for s in range(t)]
    return pl.pallas_call(
        functools.partial(_s5_out_kernel, t=t), grid=(nblk, rows // rb),
        in_specs=u_specs + [pl.BlockSpec((rb, ns2), lambda gb, r: (r, gb)),
                            pl.BlockSpec((1,) + m_op.shape[1:], lambda gb, r: (gb, 0, 0)),
                            pl.BlockSpec((1,) + q_op.shape[1:], lambda gb, r: (gb, 0, 0))],
        out_specs=pl.BlockSpec((1, rb, t * LANE), lambda gb, r: (gb, r, 0)),
        out_shape=jax.ShapeDtypeStruct((nblk, rows, t * LANE), BF16),
        compiler_params=_cp("parallel", "parallel"), name="s5_out",
    )(*([h] * t), hcat, m_op, q_op)


def _s5_glu_kernel(*refs, t, nblk):
    x_ref, mod_ref, w_ref, b_ref, g_ref = refs[nblk:nblk + 5]
    o_ref = refs[nblk + 5]
    d = w_ref.shape[0]
    w = w_ref[...]
    bias = b_ref[...]
    gate = mod_ref[0, 2:3, :]
    pg = g_ref[...]
    for j in range(t):
        lhs = jnp.concatenate([refs[gb][0, :, j * LANE:(j + 1) * LANE] for gb in range(nblk)], axis=1)
        z = _dot(lhs, w) + bias
        z = z[:, :d] * jax.nn.sigmoid(z[:, d:])
        o_ref[:, j * d:(j + 1) * d] = x_ref[:, j * d:(j + 1) * d] + gate * _rms(z, pg)


def _s5_glu_call(x, gact, mods, w, bias, post_g, rb, rows_per_batch):
    rows, td = x.shape
    t = S5_T
    d = td // t
    nblk = gact.shape[0]
    tpb = rows_per_batch // rb
    g_specs = [pl.BlockSpec((1, rb, t * LANE), lambda i, gb=gb: (gb, i, 0)) for gb in range(nblk)]
    x_spec = pl.BlockSpec((rb, td), lambda i: (i, 0))
    return pl.pallas_call(
        functools.partial(_s5_glu_kernel, t=t, nblk=nblk), grid=(rows // rb,),
        in_specs=g_specs + [x_spec, pl.BlockSpec((1, 8, d), lambda i: (i // tpb, 0, 0)),
                            _const_spec(w.shape), _const_spec((1, 2 * d)), _const_spec((1, d))],
        out_specs=x_spec, out_shape=jax.ShapeDtypeStruct((rows, td), F32),
        compiler_params=_cp("parallel"), name="s5_glu",
    )(*([gact] * nblk), x, mods, w, bias.reshape(1, 2 * d), post_g.reshape(1, d))


def _s5_layer(xl, xc, mods_l, mods_c, pre_g, post_g, lam_re, lam_im, log_dt, b_re, b_im, c_re, c_im,
              d_skip, w_glu, b_glu, b, n, c):
    d = xl.shape[-1]
    t = S5_T
    m_op, p_op, q_op, a1, a2 = _s5_operators(lam_re, lam_im, log_dt, b_re, b_im, c_re, c_im, d_skip)
    nblk = m_op.shape[0]
    half = a1.shape[1]
    hl = _normmod_call(xl, mods_l, pre_g, min(512, n), n).reshape(b * n // t, t * d)
    hc = _normmod_call(xc, mods_c, pre_g, c, c).reshape(b * c // t, t * d)

    def summaries(h, nk):
        s = _s5_sum_call(h, p_op, min(512, h.shape[0]))
        s = s.reshape(b, nk, nblk, 2, half // nblk)
        s = jnp.transpose(s, (3, 1, 0, 2, 4)).reshape(2, nk, b * half // LANE, LANE)
        return s[0], s[1]

    zeros = jnp.zeros((b * half // LANE, LANE), F32)
    sfc, sbc = summaries(hc, c // t)
    _, _, hf0, hb0 = _s5_rec_call(sfc, sbc, a1, a2, zeros, zeros)
    nk = n // t
    sfl, sbl = summaries(hl, nk)
    hf, hb, _, _ = _s5_rec_call(sfl, sbl, a1, a2, hf0, hb0)
    hcat = jnp.stack([hf.reshape(nk, b, nblk, half // nblk), hb.reshape(nk, b, nblk, half // nblk)], axis=3)
    hcat = jnp.transpose(hcat, (1, 0, 2, 3, 4)).reshape(b * nk, nblk * 2 * (half // nblk))
    rb = min(512, b * nk)
    gact = _s5_out_call(hl, hcat, m_op, q_op, rb)
    rbg = min(64, nk)
    out = _s5_glu_call(xl.reshape(b * nk, t * d), gact, mods_l, w_glu.astype(BF16), b_glu, post_g, rbg, nk)
    return out.reshape(b * n, d)


def _fn_a_kernel(x_ref, mod_ref, g_ref, cs_ref, fm_ref, twc_ref, tws_ref, or_ref, oi_ref, *, nt, d):
    g = g_ref[...]
    shift, scale = mod_ref[0, 0:1, :], mod_ref[0, 1:2, :]
    cs = cs_ref[...]
    fm = fm_ref[...]
    ng = d // FNET_GC
    for s in range(nt):
        h = _normmod(x_ref[0, :, s * d:(s + 1) * d], g, shift, scale).astype(BF16)
        ab = [_dot(h[:, k * FNET_GC:(k + 1) * FNET_GC], cs) for k in range(ng)]
        a = jnp.concatenate([z[:, :FNET_GC] for z in ab], axis=1)
        bb = jnp.concatenate([z[:, FNET_GC:] for z in ab], axis=1)
        x1 = _dot(fm, jnp.concatenate([a, bb], axis=0).astype(BF16))
        n1 = x1.shape[0] // 2
        xr, xi = x1[:n1], x1[n1:]
        c = jnp.concatenate([twc_ref[s]] * (d // LANE), axis=1)
        sn = jnp.concatenate([tws_ref[s]] * (d // LANE), axis=1)
        or_ref[0, s] = (xr * c + xi * sn).astype(or_ref.dtype)
        oi_ref[0, s] = (xi * c - xr * sn).astype(oi_ref.dtype)


def _fn_c_kernel(fm_ref, xr_ref, xi_ref, o_ref):
    o_ref[0] = _dot(fm_ref[...], jnp.concatenate([xr_ref[0], xi_ref[0]], axis=0)).astype(o_ref.dtype)


def _fnet_layer(xl, mods_l, pre_g, post_g, w_o, b_o, b, n, d):
    n2 = FFT_N2
    n1 = n // n2
    gc = FNET_GC
    cc, sc = _dft_cs(gc, gc, gc)
    cs = jnp.asarray(np.concatenate([cc, sc], axis=1) / np.sqrt(gc), BF16)
    c1, s1 = _dft_cs(n1, n1, n1)
    fm_a = jnp.asarray(np.block([[c1, -s1], [-s1, -c1]]) / np.sqrt(n), BF16)
    c2, s2 = _dft_cs(n2, n2, n2)
    fm_c = jnp.asarray(np.concatenate([c2, s2], axis=1), BF16)
    twc, tws = _twiddle_tables(n1, n2, lead_t2=True)
    nt = 4 if n2 % 4 == 0 else 1
    x3 = xl.reshape(b, n1, n2 * d)
    mid = jax.ShapeDtypeStruct((b, n2, n1, d), BF16)
    mid_spec = pl.BlockSpec((1, nt, n1, d), lambda bi, j: (bi, j, 0, 0))
    tw_spec = pl.BlockSpec((nt, n1, LANE), lambda bi, j: (j, 0, 0))
    xr, xi = pl.pallas_call(
        functools.partial(_fn_a_kernel, nt=nt, d=d), grid=(b, n2 // nt),
        in_specs=[pl.BlockSpec((1, n1, nt * d), lambda bi, j: (bi, 0, j)),
                  pl.BlockSpec((1, 8, d), lambda bi, j: (bi, 0, 0)), _const_spec((1, d)),
                  _const_spec(cs.shape), _const_spec(fm_a.shape), tw_spec, tw_spec],
        out_specs=[mid_spec, mid_spec], out_shape=[mid, mid],
        compiler_params=_cp("parallel", "parallel"), name="fn_a",
    )(x3, mods_l, pre_g.reshape(1, d), cs, fm_a, twc, tws)
    cb = 4 * d if n1 % 4 == 0 else d
    col = pl.BlockSpec((1, n2, cb), lambda bi, j: (bi, 0, j))
    y = pl.pallas_call(
        _fn_c_kernel, grid=(b, n1 * d // cb),
        in_specs=[_const_spec(fm_c.shape), col, col],
        out_specs=col, out_shape=jax.ShapeDtypeStruct((b, n2, n1 * d), BF16),
        compiler_params=_cp("parallel", "parallel"), name="fn_c",
    )(fm_c, xr.reshape(b, n2, n1 * d), xi.reshape(b, n2, n1 * d))
    return _post_call(xl, y.reshape(b * n, d), mods_l, w_o.astype(BF16), b_o, post_g, min(512, n), n)


def kernel(x, c, ctx, c_ctx, mod_w, mod_b, mix_pre_g, mix_post_g, ffn_pre_g, ffn_post_g, ffn_w13, ffn_w2,
           mla_w_in, mla_q_norm_g, mla_kv_norm_g, mla_w_uq, mla_w_ukv, mla_w_o,
           hy_w_in, hy_b_in, hy_conv_w, hy_conv_b, hy_f_w1, hy_f_b1, hy_f_freq, hy_f_w2, hy_f_b2, hy_f_w3,
           hy_skip, hy_w_out, hy_b_out,
           s5_lambda_re, s5_lambda_im, s5_log_dt, s5_b_re, s5_b_im, s5_c_re, s5_c_im, s5_d, s5_w_glu, s5_b_glu,
           fn_w_o, fn_b_o):
    b, n, d = x.shape
    cl = ctx.shape[1]
    depth = mod_w.shape[0]
    assert b == 2 and depth == 4, "two batches ride one complex transform; one layer per mixer"
    mods = _mods(c, c_ctx, mod_w, mod_b)
    xl = x.reshape(b * n, d)
    xc = ctx.reshape(b * cl, d)
    tm = min(512, n)

    def ffn(i, xl, xc, with_ctx):
        w13 = ffn_w13[i].astype(BF16)
        w2 = ffn_w2[i].astype(BF16)
        xl = _ffn_call(xl, mods[i, 0:2], ffn_pre_g[i], ffn_post_g[i], w13, w2, tm, n)
        if with_ctx:
            xc = _ffn_call(xc, mods_c(i), ffn_pre_g[i], ffn_post_g[i], w13, w2, cl, cl)
        return xl, xc

    def mods_c(i):
        return jnp.broadcast_to(mods[i, 2:3], (b, 8, d))

    xl, xc = _mla_layer(xl, xc, mods[0, 0:2], mods_c(0), mix_pre_g[0], mix_post_g[0], mla_w_in[0],
                        mla_q_norm_g[0], mla_kv_norm_g[0], mla_w_uq[0], mla_w_ukv[0], mla_w_o[0], b, n, cl)
    xl, xc = ffn(0, xl, xc, True)
    filt = (hy_f_w1[0], hy_f_b1[0], hy_f_freq[0], hy_f_w2[0], hy_f_b2[0], hy_f_w3[0])
    xl, xc = _hyena_layer(xl, xc, mods[1, 0:2], mods_c(1), mix_pre_g[1], mix_post_g[1], hy_w_in[0], hy_b_in[0],
                          hy_conv_w[0], hy_conv_b[0], filt, hy_skip[0], hy_w_out[0], hy_b_out[0], b, n, cl)
    xl, xc = ffn(1, xl, xc, True)
    xl = _s5_layer(xl, xc, mods[2, 0:2], mods_c(2), mix_pre_g[2], mix_post_g[2], s5_lambda_re[0],
                   s5_lambda_im[0], s5_log_dt[0], s5_b_re[0], s5_b_im[0], s5_c_re[0], s5_c_im[0], s5_d[0],
                   s5_w_glu[0], s5_b_glu[0], b, n, cl)
    xl, _ = ffn(2, xl, xc, False)
    xl = _fnet_layer(xl, mods[3, 0:2], mix_pre_g[3], mix_post_g[3], fn_w_o[0], fn_b_o[0], b, n, d)
    xl, _ = ffn(3, xl, xc, False)
    return xl.reshape(b, n, d)
```

```python
import functools
import math

import numpy as np
import jax
import jax.numpy as jnp
from jax import lax
from jax.experimental import pallas as pl
from jax.experimental.pallas import tpu as pltpu

F32 = jnp.float32
BF16 = jnp.bfloat16
NORM_EPS = 1e-6
LANE = 128
VMEM_LIMIT = 56 * 1024 * 1024
HIGHEST = lax.Precision.HIGHEST

GRID_W = 64
ROPE_THETA = 10000.0
MLA_HEADS = 8
MLA_NOPE = 128
MLA_ROPE = 64
MLA_V = 128
MLA_VT = MLA_V + 16
HYENA_BANDS = 16
HYENA_TARGET = 1e-2
HYENA_FAST = 0.3
HYENA_SLOW = 1.5
S5_GROUP = 16
S5_STATE = 64
S5_T = 16
FNET_GC = 128
FFT_N2 = 128


def _cp(*sem):
    return pltpu.CompilerParams(dimension_semantics=sem, vmem_limit_bytes=VMEM_LIMIT)


def _dot(a, b):
    return jnp.dot(a, b, preferred_element_type=F32)


def _dot_hi(a, b):
    return jnp.dot(a, b, preferred_element_type=F32, precision=HIGHEST)


def _rms(x, g):
    ms = jnp.mean(x * x, axis=-1, keepdims=True)
    return x * lax.rsqrt(ms + NORM_EPS) * g


def _normmod(x, g, shift, scale):
    return _rms(x, g) * (1.0 + scale) + shift


def _const_spec(shape):
    nd = len(shape)
    return pl.BlockSpec(shape, lambda *_: (0,) * nd)


def _mods_kernel(st_ref, w_ref, b_ref, o_ref):
    st = st_ref[...]
    st = st * jax.nn.sigmoid(st)
    w = w_ref[0]
    rows = [jnp.sum(st[:, r:r + 1] * w, axis=0, keepdims=True) for r in range(3)]
    rows.append(jnp.zeros((5, w.shape[1]), F32))
    o_ref[0] = jnp.concatenate(rows, axis=0) + b_ref[0]


def _mods(c, c_ctx, mod_w, mod_b):
    depth, d, n6 = mod_w.shape
    st = jnp.zeros((d, 8), F32).at[:, 0:2].set(c.T).at[:, 2].set(c_ctx)
    tn = 1024
    out = pl.pallas_call(
        _mods_kernel,
        grid=(depth, n6 // tn),
        in_specs=[_const_spec((d, 8)),
                  pl.BlockSpec((1, d, tn), lambda i, j: (i, 0, j)),
                  pl.BlockSpec((1, 1, tn), lambda i, j: (i, 0, j))],
        out_specs=pl.BlockSpec((1, 8, tn), lambda i, j: (i, 0, j)),
        out_shape=jax.ShapeDtypeStruct((depth, 8, n6), F32),
        compiler_params=_cp("parallel", "parallel"),
        name="mods",
    )(st, mod_w, mod_b.reshape(depth, 1, n6))
    m = out[:, :3].reshape(depth, 3, n6 // d, d)
    return jnp.pad(m, ((0, 0), (0, 0), (0, 8 - n6 // d), (0, 0)))


def _row_specs(tm, d, tpb):
    x_spec = pl.BlockSpec((tm, d), lambda i: (i, 0))
    mod_spec = pl.BlockSpec((1, 8, d), lambda i: (i // tpb, 0, 0))
    return x_spec, mod_spec


def _normmod_kernel(x_ref, mod_ref, g_ref, o_ref):
    h = _normmod(x_ref[...], g_ref[...], mod_ref[0, 0:1, :], mod_ref[0, 1:2, :])
    o_ref[...] = h.astype(o_ref.dtype)


def _normmod_call(x, mods, g, tm, rows_per_batch):
    m, d = x.shape
    x_spec, mod_spec = _row_specs(tm, d, rows_per_batch // tm)
    return pl.pallas_call(
        _normmod_kernel, grid=(m // tm,),
        in_specs=[x_spec, mod_spec, _const_spec((1, d))],
        out_specs=x_spec, out_shape=jax.ShapeDtypeStruct((m, d), BF16),
        compiler_params=_cp("parallel"), name="normmod",
    )(x, mods, g.reshape(1, d))


def _post_kernel(x_ref, y_ref, mod_ref, w_ref, b_ref, g_ref, o_ref, *, glu):
    z = _dot(y_ref[...].astype(BF16), w_ref[...]) + b_ref[...]
    if glu:
        d = o_ref.shape[-1]
        z = z[:, :d] * jax.nn.sigmoid(z[:, d:])
    o_ref[...] = x_ref[...] + mod_ref[0, 2:3, :] * _rms(z, g_ref[...])


def _post_call(x, y, mods, w, b, g, tm, rows_per_batch, glu=False):
    m, d = x.shape
    k, n = w.shape
    x_spec, mod_spec = _row_specs(tm, d, rows_per_batch // tm)
    return pl.pallas_call(
        functools.partial(_post_kernel, glu=glu), grid=(m // tm,),
        in_specs=[x_spec, pl.BlockSpec((tm, k), lambda i: (i, 0)), mod_spec,
                  _const_spec((k, n)), _const_spec((1, n)), _const_spec((1, d))],
        out_specs=x_spec, out_shape=jax.ShapeDtypeStruct((m, d), F32),
        compiler_params=_cp("parallel"), name="post",
    )(x, y, mods, w, b.reshape(1, n), g.reshape(1, d))


def _ffn_kernel(x_ref, mod_ref, pre_ref, post_ref, w13_ref, w2_ref, o_ref, *, f, fc):
    x = x_ref[...]
    h = _normmod(x, pre_ref[...], mod_ref[0, 3:4, :], mod_ref[0, 4:5, :]).astype(BF16)
    acc = None
    for c in range(f // fc):
        a = _dot(h, w13_ref[:, c * fc:(c + 1) * fc])
        b = _dot(h, w13_ref[:, f + c * fc:f + (c + 1) * fc])
        gact = (a * jax.nn.sigmoid(a) * b).astype(BF16)
        part = _dot(gact, w2_ref[c * fc:(c + 1) * fc, :])
        acc = part if acc is None else acc + part
    o_ref[...] = x + mod_ref[0, 5:6, :] * _rms(acc, post_ref[...])


def _ffn_call(x, mods, pre_g, post_g, w13, w2, tm, rows_per_batch):
    m, d = x.shape
    f = w2.shape[0]
    fc = f // 2 if (f // 2) % LANE == 0 else f
    x_spec, mod_spec = _row_specs(tm, d, rows_per_batch // tm)
    return pl.pallas_call(
        functools.partial(_ffn_kernel, f=f, fc=fc), grid=(m // tm,),
        in_specs=[x_spec, mod_spec, _const_spec((1, d)), _const_spec((1, d)),
                  pl.BlockSpec((d, 2 * f), lambda i: (0, 0), pipeline_mode=pl.Buffered(1)),
                  pl.BlockSpec((f, d), lambda i: (0, 0), pipeline_mode=pl.Buffered(1))],
        out_specs=x_spec, out_shape=jax.ShapeDtypeStruct((m, d), F32),
        compiler_params=_cp("parallel"), name="ffn",
    )(x, mods, pre_g.reshape(1, d), post_g.reshape(1, d), w13, w2)


def _mla_in_kernel(x_ref, mod_ref, g_ref, w_ref, qg_ref, kvg_ref, t1_ref, t2_ref,
                   qn_ref, cn_ref, kr_ref, *, ql, kvl):
    h = _normmod(x_ref[...], g_ref[...], mod_ref[0, 0:1, :], mod_ref[0, 1:2, :])
    z = _dot(h.astype(BF16), w_ref[...])
    qn_ref[...] = _rms(z[:, :ql], qg_ref[...]).astype(BF16)
    cn_ref[...] = _rms(z[:, ql:ql + kvl], kvg_ref[...]).astype(BF16)
    pair = z[:, ql + kvl:]
    kr_ref[...] = (pair * t1_ref[...] + pltpu.roll(pair, 64, 1) * t2_ref[...]).astype(BF16)


def _mla_q_kernel(qn_ref, w_ref, t1_ref, t2_ref, q_ref, *, qscale):
    z = _dot(qn_ref[...], w_ref[...])
    t1 = t1_ref[...]
    t2 = t2_ref[...]
    for h in range(MLA_HEADS):
        base = h * 256
        pair = z[:, base + 128:base + 256]
        rp = pair * t1 + pltpu.roll(pair, 64, 1) * t2
        qcat = jnp.concatenate([z[:, base:base + 128], rp], axis=1) * qscale
        q_ref[0, h] = qcat.T.astype(BF16)


def _mla_kv_kernel(cn_ref, kr_ref, w_ref, k_ref, vt_ref):
    z = _dot(cn_ref[...], w_ref[...])
    kr = kr_ref[...]
    tk = kr.shape[0]
    ones_blk = (lax.broadcasted_iota(jnp.int32, (MLA_VT - MLA_V, tk), 0) == 0).astype(BF16)
    for h in range(MLA_HEADS):
        base = h * 256
        k_ref[0, h, 0, :, 0:128] = z[:, base:base + 128].astype(BF16)
        k_ref[0, h, 0, :, 128:256] = kr
        vt_ref[0, h, 0, 0:MLA_V, :] = z[:, base + 128:base + 256].T.astype(BF16)
        vt_ref[0, h, 0, MLA_V:MLA_VT, :] = ones_blk


def _flash_kernel(q_ref, kc_ref, vc_ref, *rest, n_lat):
    if n_lat:
        kl_ref, vl_ref, o_ref, s_scr, acc_scr = rest
    else:
        o_ref, acc_scr = rest
    qt = q_ref[0, 0]

    def qk(k, slot):
        s = _dot(k, qt)
        s_scr[slot] = s
        return jnp.max(s, axis=0, keepdims=True)

    def sm_pv(slot, vt, m, mx):
        m_new = jnp.maximum(m, mx)
        alpha = jnp.exp2(m - m_new)
        p = jnp.exp2(s_scr[slot] - m_new).astype(BF16)
        acc_scr[...] = alpha * acc_scr[...] + _dot(vt, p)
        return m_new

    sc = _dot(kc_ref[0, 0, 0], qt)
    m = jnp.max(sc, axis=0, keepdims=True)
    acc_scr[...] = _dot(vc_ref[0, 0, 0], jnp.exp2(sc - m).astype(BF16))
    if n_lat:
        mx = qk(kl_ref[0, 0, 0], 0)

        def body(i, carry):
            m, mx0 = carry
            c = 2 * i
            mx1 = qk(kl_ref[0, 0, c + 1], 1)
            m = sm_pv(0, vl_ref[0, 0, c], m, mx0)
            mx0 = qk(kl_ref[0, 0, jnp.minimum(c + 2, n_lat - 1)], 0)
            m = sm_pv(1, vl_ref[0, 0, c + 1], m, mx1)
            return m, mx0

        lax.fori_loop(0, n_lat // 2, body, (m, mx))
    acc = acc_scr[...]
    o_ref[0] = (acc[0:MLA_V] / acc[MLA_V:MLA_V + 1]).T.astype(o_ref.dtype)


def _rope_tables(n_lat):
    rows = n_lat // GRID_W
    row = jnp.repeat(jnp.arange(rows, dtype=F32), GRID_W)
    col = jnp.tile(jnp.arange(GRID_W, dtype=F32), rows)
    axis_dim = MLA_ROPE // 2
    inv_freq = 1.0 / (ROPE_THETA ** (jnp.arange(0, axis_dim, 2, dtype=F32) / axis_dim))
    ang_r = row[:, None] * inv_freq
    ang_c = col[:, None] * inv_freq
    cr, sr, cc, sc = jnp.cos(ang_r), jnp.sin(ang_r), jnp.cos(ang_c), jnp.sin(ang_c)
    cp = jnp.concatenate([cr, cr, cc, cc], axis=-1)
    sp = jnp.concatenate([-sr, sr, -sc, sc], axis=-1)
    return cp, sp


_ROPE_SWAP = np.concatenate([np.arange(16, 32), np.arange(0, 16), np.arange(48, 64), np.arange(32, 48)])


def _mla_side(x, mods, pre_g, w_in_ext, q_g, kv_g, w_uq_ext, w_ukv, tabs, b, n, tm, tk, want_q):
    m, d = x.shape
    ql, kvl = q_g.shape[-1], kv_g.shape[-1]
    t1k, t2k, t1q, t2q = tabs
    x_spec, mod_spec = _row_specs(tm, d, n // tm)
    row = lambda w: pl.BlockSpec((tm, w), lambda i: (i, 0))
    tab_spec = pl.BlockSpec((tm, 128), lambda i: (i % (n // tm), 0))
    qn, cn, kr = pl.pallas_call(
        functools.partial(_mla_in_kernel, ql=ql, kvl=kvl), grid=(m // tm,),
        in_specs=[x_spec, mod_spec, _const_spec((1, d)), _const_spec(w_in_ext.shape),
                  _const_spec((1, ql)), _const_spec((1, kvl)), tab_spec, tab_spec],
        out_specs=[row(ql), row(kvl), row(128)],
        out_shape=[jax.ShapeDtypeStruct((m, ql), BF16), jax.ShapeDtypeStruct((m, kvl), BF16),
                   jax.ShapeDtypeStruct((m, 128), BF16)],
        compiler_params=_cp("parallel"), name="mla_in",
    )(x, mods, pre_g.reshape(1, d), w_in_ext, q_g.reshape(1, ql), kv_g.reshape(1, kvl), t1k, t2k)

    nc = n // tk
    kt, v = pl.pallas_call(
        _mla_kv_kernel, grid=(b, nc),
        in_specs=[pl.BlockSpec((tk, kvl), lambda bi, c: (bi * nc + c, 0)),
                  pl.BlockSpec((tk, 128), lambda bi, c: (bi * nc + c, 0)),
                  _const_spec(w_ukv.shape)],
        out_specs=[pl.BlockSpec((1, MLA_HEADS, 1, tk, 256), lambda bi, c: (bi, 0, c, 0, 0)),
                   pl.BlockSpec((1, MLA_HEADS, 1, MLA_VT, tk), lambda bi, c: (bi, 0, c, 0, 0))],
        out_shape=[jax.ShapeDtypeStruct((b, MLA_HEADS, nc, tk, 256), BF16),
                   jax.ShapeDtypeStruct((b, MLA_HEADS, nc, MLA_VT, tk), BF16)],
        compiler_params=_cp("parallel", "parallel"), name="mla_kv",
    )(cn, kr, w_ukv)

    q = None
    if want_q:
        qscale = (MLA_NOPE + MLA_ROPE) ** -0.5 * math.log2(math.e)
        tpb = n // tm
        q = pl.pallas_call(
            functools.partial(_mla_q_kernel, qscale=qscale), grid=(m // tm,),
            in_specs=[row(ql), _const_spec(w_uq_ext.shape), tab_spec, tab_spec],
            out_specs=pl.BlockSpec((1, MLA_HEADS, 256, tm), lambda i: (i // tpb, 0, 0, i % tpb)),
            out_shape=jax.ShapeDtypeStruct((b, MLA_HEADS, 256, n), BF16),
            compiler_params=_cp("parallel"), name="mla_q",
        )(qn, w_uq_ext, t1q, t2q)
    return q, kt, v


def _flash_call(qt, kc, vtc, kl, vtl, tq):
    b, hh, _, n = qt.shape
    c = kc.shape[-2]
    n_lat = 0 if kl is None else kl.shape[2]
    in_specs = [pl.BlockSpec((1, 1, 256, tq), lambda bi, h, i: (bi, h, 0, i)),
                pl.BlockSpec((1, 1, 1, c, 256), lambda bi, h, i: (bi, h, 0, 0, 0)),
                pl.BlockSpec((1, 1, 1, MLA_VT, c), lambda bi, h, i: (bi, h, 0, 0, 0))]
    args = [qt, kc, vtc]
    scratch = [pltpu.VMEM((MLA_VT, tq), F32)]
    if n_lat:
        assert n_lat % 2 == 0, "latent key chunks are consumed in pairs"
        tk = kl.shape[-2]
        in_specs += [pl.BlockSpec((1, 1, n_lat, tk, 256), lambda bi, h, i: (bi, h, 0, 0, 0)),
                     pl.BlockSpec((1, 1, n_lat, MLA_VT, tk), lambda bi, h, i: (bi, h, 0, 0, 0))]
        args += [kl, vtl]
        scratch = [pltpu.VMEM((2, tk, tq), F32)] + scratch
    return pl.pallas_call(
        functools.partial(_flash_kernel, n_lat=n_lat), grid=(b, hh, n // tq),
        in_specs=in_specs,
        out_specs=pl.BlockSpec((1, tq, MLA_V), lambda bi, h, i: (bi, i, h)),
        out_shape=jax.ShapeDtypeStruct((b, n, hh * MLA_V), BF16),
        scratch_shapes=scratch,
        compiler_params=_cp("parallel", "parallel", "arbitrary"), name="flash",
    )(*args)


def _mla_layer(xl, xc, mods_l, mods_c, pre_g, post_g, w_in, q_g, kv_g, w_uq, w_ukv, w_o, b, n, c):
    d = xl.shape[-1]
    ql, kvl = q_g.shape[-1], kv_g.shape[-1]
    hh = MLA_HEADS
    rope_cols = w_in[:, ql + kvl:]
    w_in_ext = jnp.concatenate([w_in, rope_cols[:, _ROPE_SWAP]], axis=1).astype(BF16)
    wq = w_uq.reshape(ql, hh, MLA_NOPE + MLA_ROPE)
    w_uq_ext = jnp.concatenate([wq, wq[:, :, MLA_NOPE:][:, :, _ROPE_SWAP]], axis=-1)
    w_uq_ext = w_uq_ext.reshape(ql, hh * 256).astype(BF16)
    w_ukv_b = w_ukv.astype(BF16)
    w_o_b = w_o.astype(BF16)

    cp, sp = _rope_tables(n)
    z64l, o64l = jnp.zeros((n, 64), F32), jnp.ones((n, 64), F32)
    z64c, o64c = jnp.zeros((c, 64), F32), jnp.ones((c, 64), F32)
    cat = lambda a, bb: jnp.concatenate([a, bb], axis=1)
    tabs_l = (cat(cp, z64l), cat(sp, z64l), cat(cp, z64l), cat(sp, o64l))
    tabs_c = (cat(z64c, z64c), cat(z64c, o64c), cat(z64c, z64c), cat(z64c, o64c))

    tm_l = min(512, n)
    tk_l = min(512, n // 2)
    ql_, kl, vtl = _mla_side(xl, mods_l, pre_g, w_in_ext, q_g, kv_g, w_uq_ext, w_ukv_b, tabs_l,
                             b, n, tm_l, tk_l, True)
    qc_, kc, vtc = _mla_side(xc, mods_c, pre_g, w_in_ext, q_g, kv_g, w_uq_ext, w_ukv_b, tabs_c,
                             b, c, c, c, True)
    o_lat = _flash_call(ql_, kc, vtc, kl, vtl, min(1024, n)).reshape(b * n, hh * MLA_V)
    o_ctx = _flash_call(qc_, kc, vtc, None, None, c).reshape(b * c, hh * MLA_V)
    zb = jnp.zeros((d,), F32)
    xl = _post_call(xl, o_lat, mods_l, w_o_b, zb, post_g, tm_l, n)
    xc = _post_call(xc, o_ctx, mods_c, w_o_b, zb, post_g, c, c)
    return xl, xc


def _hy_in_kernel(x_ref, xp_ref, xn_ref, mod_ref, g_ref, w_ref, b_ref, cw_ref, cb_ref,
                  g0_ref, vg_ref, *, tpb):
    i = pl.program_id(0)
    g = g_ref[...]
    shift, scale = mod_ref[0, 0:1, :], mod_ref[0, 1:2, :]
    w = w_ref[...]
    bias = b_ref[...]
    u = _dot(_normmod(x_ref[...], g, shift, scale).astype(BF16), w) + bias
    up = _dot(_normmod(xp_ref[...], g, shift, scale).astype(BF16), w) + bias
    un = _dot(_normmod(xn_ref[...], g, shift, scale).astype(BF16), w) + bias
    first = (i % tpb) == 0
    last = (i % tpb) == tpb - 1
    prev_row = jnp.where(first, 0.0, up[7:8, :])
    next_row = jnp.where(last, 0.0, un[0:1, :])
    tm = u.shape[0]
    ridx = lax.broadcasted_iota(jnp.int32, (tm, 1), 0)
    dn = jnp.where(ridx == 0, prev_row, pltpu.roll(u, 1, 0))
    upw = jnp.where(ridx == tm - 1, next_row, pltpu.roll(u, tm - 1, 0))
    conv = cb_ref[...] + dn * cw_ref[0:1, :] + u * cw_ref[1:2, :] + upw * cw_ref[2:3, :]
    d = g0_ref.shape[-1]
    g0_ref[...] = conv[:, :d]
    vg_ref[...] = conv[:, 2 * d:] * conv[:, d:2 * d]


def _hy_in_call(x, mods, pre_g, w_in, b_in, conv_w, conv_b, tm, n):
    m, d = x.shape
    p = w_in.shape[1]
    tpb = n // tm
    x_spec, mod_spec = _row_specs(tm, d, tpb)
    r8 = tm // 8
    nb8 = m // 8
    prev_spec = pl.BlockSpec((8, d), lambda i: (jnp.maximum(i * r8 - 1, 0), 0))
    next_spec = pl.BlockSpec((8, d), lambda i: (jnp.minimum((i + 1) * r8, nb8 - 1), 0))
    cw = jnp.pad(conv_w, ((0, 8 - conv_w.shape[0]), (0, 0)))
    return pl.pallas_call(
        functools.partial(_hy_in_kernel, tpb=tpb), grid=(m // tm,),
        in_specs=[x_spec, prev_spec, next_spec, mod_spec, _const_spec((1, d)), _const_spec((d, p)),
                  _const_spec((1, p)), _const_spec((8, p)), _const_spec((1, p))],
        out_specs=[x_spec, x_spec],
        out_shape=[jax.ShapeDtypeStruct((m, d), F32), jax.ShapeDtypeStruct((m, d), F32)],
        compiler_params=_cp("parallel"), name="hy_in",
    )(x, x, x, mods, pre_g.reshape(1, d), w_in, b_in.reshape(1, p), cw, conv_b.reshape(1, p))


def _hy_filter_kernel(bands_ref, w1_ref, b1_ref, fq_ref, w2_ref, b2_ref, w3_ref, dl_ref,
                      kf_ref, gb_ref, nrm_ref, *, n, tr):
    i = pl.program_id(0)
    j = (lax.broadcasted_iota(jnp.int32, (tr, LANE), 0) + i * tr).astype(F32)
    lane = lax.broadcasted_iota(jnp.int32, (tr, LANE), 1)
    t = j * (1.0 / (n - 1))
    arg = (2.0 * math.pi / n) * j * bands_ref[...]
    z = jnp.where(lane == 0, t,
                  jnp.where(lane <= HYENA_BANDS, jnp.cos(arg),
                            jnp.where(lane <= 2 * HYENA_BANDS, -jnp.sin(arg), 0.0)))
    fq = fq_ref[...]
    a = jnp.sin(fq * (_dot_hi(z, w1_ref[...]) + b1_ref[...]))
    for k in range(w2_ref.shape[0]):
        a = jnp.sin(fq * (_dot_hi(a, w2_ref[k]) + b2_ref[k]))
    h = _dot_hi(a, w3_ref[...])
    d = kf_ref.shape[-1]
    decay = jnp.exp(-t[:, 0:1] * dl_ref[...])
    kf = h[:, :d] * decay
    gb = jnp.where(j[:, 0:1] == 0.0, 0.0, h[:, d:] * decay)
    kf_ref[...] = kf
    gb_ref[...] = gb
    part = jnp.sum(jnp.abs(kf) + jnp.abs(gb), axis=0, keepdims=True)

    @pl.when(i == 0)
    def _():
        nrm_ref[...] = jnp.zeros_like(nrm_ref)

    nrm_ref[...] += jnp.broadcast_to(part, nrm_ref.shape)


def _hy_filter_call(n, d, f_w1, f_b1, f_freq, f_w2, f_b2, f_w3):
    fw = f_w1.shape[1]
    tr = min(512, n)
    bands_np = np.zeros((1, LANE), np.float32)
    bands_np[0, 1:1 + HYENA_BANDS] = np.linspace(1e-4, HYENA_BANDS - 1, HYENA_BANDS, dtype=np.float32)
    bands_np[0, 1 + HYENA_BANDS:1 + 2 * HYENA_BANDS] = bands_np[0, 1:1 + HYENA_BANDS]
    w1p = jnp.zeros((LANE, fw), F32).at[:f_w1.shape[0]].set(f_w1)
    deltas = jnp.abs(jnp.linspace(math.log(HYENA_TARGET) / HYENA_SLOW, math.log(HYENA_TARGET) / HYENA_FAST,
                                  d, dtype=F32)).reshape(1, d)
    row = pl.BlockSpec((tr, d), lambda i: (i, 0))
    return pl.pallas_call(
        functools.partial(_hy_filter_kernel, n=n, tr=tr), grid=(n // tr,),
        in_specs=[_const_spec((1, LANE)), _const_spec((LANE, fw)), _const_spec((1, fw)), _const_spec((1, fw)),
                  _const_spec(f_w2.shape), _const_spec((f_w2.shape[0], 1, fw)), _const_spec(f_w3.shape),
                  _const_spec((1, d))],
        out_specs=[row, row, _const_spec((8, d))],
        out_shape=[jax.ShapeDtypeStruct((n, d), F32), jax.ShapeDtypeStruct((n, d), F32),
                   jax.ShapeDtypeStruct((8, d), F32)],
        compiler_params=_cp("arbitrary"), name="hy_filter",
    )(jnp.asarray(bands_np), w1p, f_b1.reshape(1, fw), f_freq.reshape(1, fw), f_w2,
      f_b2.reshape(f_w2.shape[0], 1, fw), f_w3, deltas)


def _dft_cs(nf, nt, period):
    ft = (np.arange(nf)[:, None] * np.arange(nt)[None, :]) % period
    ang = 2.0 * np.pi * ft / period
    return np.cos(ang), np.sin(ang)


def _twiddle_tables(n1, n2, lead_t2):
    nn = n1 * n2
    f1 = jnp.arange(n1, dtype=jnp.int32)
    t2 = jnp.arange(n2, dtype=jnp.int32)
    idx = (t2[:, None] * f1[None, :]) % nn if lead_t2 else (f1[:, None] * t2[None, :]) % nn
    ang = idx.astype(F32) * (2.0 * math.pi / nn)
    shape = idx.shape + (LANE,)
    return (jnp.broadcast_to(jnp.cos(ang)[..., None], shape),
            jnp.broadcast_to(jnp.sin(ang)[..., None], shape))


def _fft_a_kernel(*refs, n_in, nt, d):
    fm_ref = refs[0]
    in_refs = refs[1:1 + n_in]
    twc_ref, tws_ref, or_ref, oi_ref = refs[1 + n_in:]
    xs = [r[0] for r in in_refs]
    x = (jnp.concatenate(xs, axis=0) if n_in > 1 else xs[0]).astype(BF16)
    out = _dot(fm_ref[...], x)
    n1 = out.shape[0] // 2
    xr, xi = out[:n1], out[n1:]
    for s in range(nt):
        c = jnp.concatenate([twc_ref[s]] * (d // LANE), axis=1)
        sn = jnp.concatenate([tws_ref[s]] * (d // LANE), axis=1)
        a = xr[:, s * d:(s + 1) * d]
        bb = xi[:, s * d:(s + 1) * d]
        or_ref[:, s * d:(s + 1) * d] = (a * c + bb * sn).astype(or_ref.dtype)
        oi_ref[:, s * d:(s + 1) * d] = (bb * c - a * sn).astype(oi_ref.dtype)


def _fft_a_call(fm, ins, twc, tws, n1, n2, d, nt):
    rows = ins[0][0].shape[1]
    in_specs = [_const_spec(fm.shape)]
    args = [fm]
    for arr, bi in ins:
        in_specs.append(pl.BlockSpec((1, rows, nt * d), lambda j, bi=bi: (bi, 0, j)))
        args.append(arr)
    tw_spec = pl.BlockSpec((nt, n1, LANE), lambda j: (j, 0, 0))
    out_spec = pl.BlockSpec((n1, nt * d), lambda j: (0, j))
    return pl.pallas_call(
        functools.partial(_fft_a_kernel, n_in=len(ins), nt=nt, d=d), grid=(n2 // nt,),
        in_specs=in_specs + [tw_spec, tw_spec],
        out_specs=[out_spec, out_spec],
        out_shape=[jax.ShapeDtypeStruct((n1, n2 * d), BF16)] * 2,
        compiler_params=_cp("parallel"), name="fft_a",
    )(*args, twc, tws)


def _hy_kb_kernel(fm_ref, fr_ref, fi_ref, gr_ref, gi_ref, sc_ref, kr_ref, ki_ref, *, nf):
    fm = fm_ref[...]
    n2 = fr_ref.shape[1]
    sc = sc_ref[...]
    for s in range(nf):
        kf = _dot(fm, jnp.concatenate([fr_ref[s], fi_ref[s]], axis=0))
        gb = _dot(fm, jnp.concatenate([gr_ref[s], gi_ref[s]], axis=0))
        kr_ref[s] = (kf[:n2] + gb[:n2]) * sc
        ki_ref[s] = (kf[n2:] - gb[n2:]) * sc


def _hy_b_kernel(fm_ref, fmc_ref, xr_ref, xi_ref, kr_ref, ki_ref, twc_ref, tws_ref,
                 or_ref, oi_ref, *, nf, d):
    fm = fm_ref[...]
    fmc = fmc_ref[...]
    n2 = xr_ref.shape[1]
    for s in range(nf):
        x = _dot(fm, jnp.concatenate([xr_ref[s], xi_ref[s]], axis=0))
        xr, xi = x[:n2], x[n2:]
        kr, ki = kr_ref[s], ki_ref[s]
        yr = (xr * kr - xi * ki).astype(BF16)
        yi = (xr * ki + xi * kr).astype(BF16)
        g = _dot(fmc, jnp.concatenate([yr, yi], axis=0))
        gr, gi = g[:n2], g[n2:]
        c = jnp.concatenate([twc_ref[s]] * (d // LANE), axis=1)
        sn = jnp.concatenate([tws_ref[s]] * (d // LANE), axis=1)
        or_ref[s] = (gr * c - gi * sn).astype(or_ref.dtype)
        oi_ref[s] = (gi * c + gr * sn).astype(oi_ref.dtype)


def _hy_c_kernel(fm_ref, gr_ref, gi_ref, vg_ref, g0_ref, skip_ref, o_ref):
    y = _dot(fm_ref[...], jnp.concatenate([gr_ref[...], gi_ref[...]], axis=0))
    half = y.shape[0] // 2
    skip = skip_ref[...]
    for bi in range(2):
        yb = y[bi * half:(bi + 1) * half]
        o_ref[bi] = ((yb + vg_ref[bi] * skip) * g0_ref[bi]).astype(o_ref.dtype)


def _block_c(cs, sn, sign):
    return np.block([[cs, -sign * sn], [sign * sn, cs]])


def _hy_conv_long(vg, g0, kf, gb, nrm, skip, b, n, d):
    n2 = FFT_N2
    nn = 2 * n
    n1 = nn // n2
    rows = n // n2
    cs1, sn1 = _dft_cs(n1, rows, n1)
    fm_a = jnp.asarray(_block_c(cs1, sn1, -1.0), BF16)
    fm_a_real = jnp.asarray(np.concatenate([cs1, -sn1], axis=0), BF16)
    cs2, sn2 = _dft_cs(n2, n2, n2)
    fm_b = jnp.asarray(_block_c(cs2, sn2, -1.0), BF16)
    fm_bc = jnp.asarray(_block_c(cs2, sn2, 1.0), BF16)
    cs1i, sn1i = _dft_cs(rows, n1, n1)
    fm_c = jnp.asarray(_block_c(cs1i, sn1i, 1.0), BF16)
    twc_a, tws_a = _twiddle_tables(n1, n2, lead_t2=True)
    twc_b, tws_b = _twiddle_tables(n1, n2, lead_t2=False)

    nt = 2 if n2 % 2 == 0 else 1
    nf = 4 if n1 % 4 == 0 else 1
    kf3 = kf.reshape(1, rows, n2 * d)
    gb3 = gb.reshape(1, rows, n2 * d)
    kfr, kfi = _fft_a_call(fm_a_real, [(kf3, 0)], twc_a, tws_a, n1, n2, d, nt)
    gbr, gbi = _fft_a_call(fm_a_real, [(gb3, 0)], twc_a, tws_a, n1, n2, d, nt)
    scale = (1.0 / (nrm[0:1, :] * nn))
    slab = pl.BlockSpec((nf, n2, d), lambda j: (j, 0, 0))
    shp3 = (n1, n2, d)
    khr, khi = pl.pallas_call(
        functools.partial(_hy_kb_kernel, nf=nf), grid=(n1 // nf,),
        in_specs=[_const_spec(fm_b.shape), slab, slab, slab, slab, _const_spec((1, d))],
        out_specs=[slab, slab],
        out_shape=[jax.ShapeDtypeStruct(shp3, F32)] * 2,
        compiler_params=_cp("parallel"), name="hy_kb",
    )(fm_b, kfr.reshape(shp3), kfi.reshape(shp3), gbr.reshape(shp3), gbi.reshape(shp3), scale)

    vg3 = vg.reshape(b, rows, n2 * d)
    g03 = g0.reshape(b, rows, n2 * d)
    x1r, x1i = _fft_a_call(fm_a, [(vg3, 0), (vg3, 1)], twc_a, tws_a, n1, n2, d, nt)
    tw_slab = pl.BlockSpec((nf, n2, LANE), lambda j: (j, 0, 0))
    g1r, g1i = pl.pallas_call(
        functools.partial(_hy_b_kernel, nf=nf, d=d), grid=(n1 // nf,),
        in_specs=[_const_spec(fm_b.shape), _const_spec(fm_bc.shape), slab, slab, slab, slab,
                  tw_slab, tw_slab],
        out_specs=[slab, slab],
        out_shape=[jax.ShapeDtypeStruct(shp3, BF16)] * 2,
        compiler_params=_cp("parallel"), name="hy_b",
    )(fm_b, fm_bc, x1r.reshape(shp3), x1i.reshape(shp3), khr, khi, twc_b, tws_b)

    cb = nt * d
    col = pl.BlockSpec((n1, cb), lambda j: (0, j))
    col3 = pl.BlockSpec((b, rows, cb), lambda j: (0, 0, j))
    out = pl.pallas_call(
        _hy_c_kernel, grid=(n2 * d // cb,),
        in_specs=[_const_spec(fm_c.shape), col, col, col3, col3, _const_spec((1, cb))],
        out_specs=col3,
        out_shape=jax.ShapeDtypeStruct((b, rows, n2 * d), BF16),
        compiler_params=_cp("parallel"), name="hy_c",
    )(fm_c, g1r.reshape(n1, n2 * d), g1i.reshape(n1, n2 * d), vg3, g03, jnp.tile(skip.reshape(1, d), (1, nt)))
    return out.reshape(b * n, d)


def _hy_short_kernel(fa_ref, fk_ref, fi_ref, vg_ref, g0_ref, kf_ref, gb_ref, nrm_ref, skip_ref, o_ref, *, n):
    z = jnp.concatenate([vg_ref[0], vg_ref[1]], axis=0)
    x = _dot_hi(fa_ref[...], z)
    kk = _dot_hi(fk_ref[...], jnp.concatenate([kf_ref[...], gb_ref[...]], axis=0))
    nn = 2 * n
    sc = 1.0 / (nrm_ref[0:1, :] * nn)
    xr, xi = x[:nn], x[nn:]
    kr, ki = kk[:nn] * sc, kk[nn:] * sc
    y = _dot_hi(fi_ref[...], jnp.concatenate([xr * kr - xi * ki, xr * ki + xi * kr], axis=0))
    skip = skip_ref[...]
    for bi in range(2):
        o_ref[bi] = ((y[bi * n:(bi + 1) * n] + vg_ref[bi] * skip) * g0_ref[bi]).astype(o_ref.dtype)


def _hy_conv_short(vg, g0, kf, gb, nrm, skip, b, n, d):
    nn = 2 * n
    cs, sn = _dft_cs(nn, n, nn)
    fa = jnp.asarray(_block_c(cs, sn, -1.0), F32)
    fk = jnp.asarray(np.block([[cs, cs], [-sn, sn]]), F32)
    csi, sni = _dft_cs(n, nn, nn)
    fi = jnp.asarray(_block_c(csi, sni, 1.0), F32)
    cb = 256
    col3 = pl.BlockSpec((b, n, cb), lambda j: (0, 0, j))
    col = pl.BlockSpec((n, cb), lambda j: (0, j))
    vec = pl.BlockSpec((1, cb), lambda j: (0, j))
    out = pl.pallas_call(
        functools.partial(_hy_short_kernel, n=n), grid=(d // cb,),
        in_specs=[_const_spec(fa.shape), _const_spec(fk.shape), _const_spec(fi.shape), col3, col3, col, col,
                  pl.BlockSpec((8, cb), lambda j: (0, j)), vec],
        out_specs=col3, out_shape=jax.ShapeDtypeStruct((b, n, d), BF16),
        compiler_params=_cp("parallel"), name="hy_short",
    )(fa, fk, fi, vg.reshape(b, n, d), g0.reshape(b, n, d), kf, gb, nrm, skip.reshape(1, d))
    return out.reshape(b * n, d)


def _hyena_layer(xl, xc, mods_l, mods_c, pre_g, post_g, w_in, b_in, conv_w, conv_b, filt, skip,
                 w_out, b_out, b, n, c):
    d = xl.shape[-1]
    w_in_b = w_in.astype(BF16)
    w_out_b = w_out.astype(BF16)
    tm = min(512, n)
    g0, vg = _hy_in_call(xl, mods_l, pre_g, w_in_b, b_in, conv_w, conv_b, tm, n)
    kf, gb, nrm = _hy_filter_call(n, d, *filt)
    u_out = _hy_conv_long(vg, g0, kf, gb, nrm, skip[0], b, n, d)
    xl = _post_call(xl, u_out, mods_l, w_out_b, b_out, post_g, tm, n)

    g0c, vgc = _hy_in_call(xc, mods_c, pre_g, w_in_b, b_in, conv_w, conv_b, c, c)
    kfc, gbc, nrmc = _hy_filter_call(c, d, *filt)
    u_out_c = _hy_conv_short(vgc, g0c, kfc, gbc, nrmc, skip[0], b, c, d)
    xc = _post_call(xc, u_out_c, mods_c, w_out_b, b_out, post_g, c, c)
    return xl, xc


def _s5_operators(lam_re, lam_im, log_dt, b_re, b_im, c_re, c_im, d_skip):
    t = S5_T
    g, ns = lam_re.shape[1], lam_re.shape[2]
    gc = b_re.shape[-1]
    gl = LANE // gc
    nblk = g // gl
    lam = lax.complex(lam_re, lam_im)
    dt = jnp.exp(log_dt)[..., None]
    lam_bar = jnp.exp(lam * dt)
    b_bar = ((lam_bar - 1.0) / lam)[..., None] * lax.complex(b_re, b_im)
    c_mat = lax.complex(c_re, c_im)
    pw = jnp.arange(t + 1, dtype=F32)
    lam_pw = jnp.exp((lam * dt)[None] * pw[:, None, None, None])
    hp = HIGHEST
    kern = jnp.einsum('dgcn,tdgn,dgne->dgtce', c_mat, lam_pw[:t], b_bar, precision=hp).real
    s_idx = np.arange(t)[:, None]
    t_idx = np.arange(t)[None, :]
    lag = np.abs(t_idx - s_idx)
    mf = (s_idx <= t_idx).astype(np.float32)
    mb = (s_idx >= t_idx).astype(np.float32)
    kf = kern[0][:, lag] * mf[None, :, :, None, None]
    kb = kern[1][:, lag] * mb[None, :, :, None, None]
    dsk = d_skip.reshape(g, gc)
    diag = (np.eye(t, dtype=np.float32)[None, :, :, None, None]
            * (jnp.eye(gc, dtype=F32)[None] * dsk[:, :, None])[:, None, None, :, :])
    mg = jnp.transpose(kf + kb + diag, (0, 1, 4, 2, 3))
    eye_gl = jnp.eye(gl, dtype=F32)
    mg = mg.reshape(nblk, gl, t, gc, t, gc)
    m_op = jnp.einsum('bgsetc,gh->bsgethc', mg, eye_gl).reshape(nblk, t * LANE, t * LANE)

    pf = lam_pw[:t][::-1][:, 0, :, :, None] * b_bar[0][None]
    pb = lam_pw[:t][:, 1, :, :, None] * b_bar[1][None]
    pcat = jnp.stack([pf, pb], axis=0)
    pri = jnp.stack([pcat.real, pcat.imag], axis=0)
    pri = jnp.transpose(pri, (2, 3, 5, 1, 0, 4))
    pri = pri.reshape(t, nblk, gl, gc, 2, 2, ns)
    p_op = jnp.einsum('jbgedrn,gh->bjgedhrn', pri, eye_gl).reshape(nblk, t * LANE, 2 * gl * 2 * ns)

    qf = c_mat[0][None] * jnp.transpose(lam_pw[1:t + 1, 0], (0, 1, 2))[:, :, None, :]
    qb = c_mat[1][None] * lam_pw[1:t + 1][::-1][:, 1][:, :, None, :]
    qcat = jnp.stack([qf, qb], axis=0)
    qri = jnp.stack([qcat.real, -qcat.imag], axis=0)
    qri = jnp.transpose(qri, (1, 3, 0, 5, 2, 4))
    qri = qri.reshape(2, nblk, gl, 2, ns, t, gc)
    q_op = jnp.einsum('dbgrnjc,gh->bdgrnjhc', qri, eye_gl).reshape(nblk, 2 * gl * 2 * ns, t * LANE)

    a = lam_pw[t]
    a1 = jnp.concatenate([a.real, a.real], axis=-1).reshape(2, g * 2 * ns)
    a2 = jnp.concatenate([-a.imag, a.imag], axis=-1).reshape(2, g * 2 * ns)
    return m_op.astype(BF16), p_op.astype(BF16), q_op.astype(BF16), a1, a2


def _s5_sum_kernel(*refs, t):
    u_refs = refs[:t]
    p_ref, o_ref = refs[t:]
    u = jnp.concatenate([r[...] for r in u_refs], axis=1)
    o_ref[...] = _dot(u, p_ref[0])


def _s5_sum_call(h, p_op, rb):
    rows = h.shape[0]
    t = S5_T
    nblk = p_op.shape[0]
    ns2 = p_op.shape[2]
    u_specs = [pl.BlockSpec((rb, LANE), lambda gb, r, s=s: (r, s * nblk + gb)) for s in range(t)]
    return pl.pallas_call(
        functools.partial(_s5_sum_kernel, t=t), grid=(nblk, rows // rb),
        in_specs=u_specs + [pl.BlockSpec((1,) + p_op.shape[1:], lambda gb, r: (gb, 0, 0))],
        out_specs=pl.BlockSpec((rb, ns2), lambda gb, r: (r, gb)),
        out_shape=jax.ShapeDtypeStruct((rows, nblk * ns2), F32),
        compiler_params=_cp("parallel", "parallel"), name="s5_sum",
    )(*([h] * t), p_op)


def _s5_rec_kernel(sf_ref, sb_ref, a1f_ref, a2f_ref, a1b_ref, a2b_ref, hf0_ref, hb0_ref,
                   hf_ref, hb_ref, ff_ref, fb_ref, *, kb):
    a1f, a2f, a1b, a2b = a1f_ref[...], a2f_ref[...], a1b_ref[...], a2b_ref[...]

    @pl.when(pl.program_id(0) == 0)
    def _():
        ff_ref[...] = hf0_ref[...]
        fb_ref[...] = hb0_ref[...]

    def body(i, carry):
        hf, hb = carry
        k = kb - 1 - i
        hf_ref[i] = hf
        hb_ref[k] = hb
        hf = a1f * hf + a2f * pltpu.roll(hf, 64, 1) + sf_ref[i]
        hb = a1b * hb + a2b * pltpu.roll(hb, 64, 1) + sb_ref[k]
        return hf, hb

    hf, hb = lax.fori_loop(0, kb, body, (ff_ref[...], fb_ref[...]))
    ff_ref[...] = hf
    fb_ref[...] = hb


def _s5_rec_call(sf, sb, a1, a2, hf0, hb0):
    nk, r, _ = sf.shape
    rep = r * LANE // a1.shape[1]
    tile = lambda v: jnp.tile(v.reshape(1, -1), (1, rep)).reshape(r, LANE)
    kb = min(32, nk)
    nb = nk // kb
    fwd3 = pl.BlockSpec((kb, r, LANE), lambda i: (i, 0, 0))
    bwd3 = pl.BlockSpec((kb, r, LANE), lambda i: (nb - 1 - i, 0, 0))
    full2 = _const_spec((r, LANE))
    return pl.pallas_call(
        functools.partial(_s5_rec_kernel, kb=kb), grid=(nb,),
        in_specs=[fwd3, bwd3, full2, full2, full2, full2, full2, full2],
        out_specs=[fwd3, bwd3, full2, full2],
        out_shape=[jax.ShapeDtypeStruct((nk, r, LANE), F32)] * 2 + [jax.ShapeDtypeStruct((r, LANE), F32)] * 2,
        compiler_params=_cp("arbitrary"), name="s5_rec",
    )(sf, sb, tile(a1[0]), tile(a2[0]), tile(a1[1]), tile(a2[1]), hf0, hb0)


def _s5_out_kernel(*refs, t):
    u_refs = refs[:t]
    h_ref, m_ref, q_ref, o_ref = refs[t:]
    u = jnp.concatenate([r[...] for r in u_refs], axis=1)
    y = _dot(u, m_ref[0]) + _dot(h_ref[...].astype(BF16), q_ref[0])
    o_ref[0] = (0.5 * y * (1.0 + lax.erf(y * (2.0 ** -0.5)))).astype(o_ref.dtype)


def _s5_out_call(h, hcat, m_op, q_op, rb):
    rows = h.shape[0]
    t = S5_T
    nblk = m_op.shape[0]
    ns2 = q_op.shape[1]
    u_specs = [pl.BlockSpec((rb, LANE), lambda gb, r, s=s: (r, s * nblk + gb)) for s in range(t)]
    return pl.pallas_call(
        functools.partial(_s5_out_kernel, t=t), grid=(nblk, rows // rb),
        in_specs=u_specs + [pl.BlockSpec((rb, ns2), lambda gb, r: (r, gb)),
                            pl.BlockSpec((1,) + m_op.shape[1:], lambda gb, r: (gb, 0, 0)),
                            pl.BlockSpec((1,) + q_op.shape[1:], lambda gb, r: (gb, 0, 0))],
        out_specs=pl.BlockSpec((1, rb, t * LANE), lambda gb, r: (gb, r, 0)),
        out_shape=jax.ShapeDtypeStruct((nblk, rows, t * LANE), BF16),
        compiler_params=_cp("parallel", "parallel"), name="s5_out",
    )(*([h] * t), hcat, m_op, q_op)


def _s5_glu_kernel(*refs, t, nblk):
    x_ref, mod_ref, w_ref, b_ref, g_ref = refs[nblk:nblk + 5]
    o_ref = refs[nblk + 5]
    d = w_ref.shape[0]
    w = w_ref[...]
    bias = b_ref[...]
    gate = mod_ref[0, 2:3, :]
    pg = g_ref[...]
    for j in range(t):
        lhs = jnp.concatenate([refs[gb][0, :, j * LANE:(j + 1) * LANE] for gb in range(nblk)], axis=1)
        z = _dot(lhs, w) + bias
        z = z[:, :d] * jax.nn.sigmoid(z[:, d:])
        o_ref[:, j * d:(j + 1) * d] = x_ref[:, j * d:(j + 1) * d] + gate * _rms(z, pg)


def _s5_glu_call(x, gact, mods, w, bias, post_g, rb, rows_per_batch):
    rows, td = x.shape
    t = S5_T
    d = td // t
    nblk = gact.shape[0]
    tpb = rows_per_batch // rb
    g_specs = [pl.BlockSpec((1, rb, t * LANE), lambda i, gb=gb: (gb, i, 0)) for gb in range(nblk)]
    x_spec = pl.BlockSpec((rb, td), lambda i: (i, 0))
    return pl.pallas_call(
        functools.partial(_s5_glu_kernel, t=t, nblk=nblk), grid=(rows // rb,),
        in_specs=g_specs + [x_spec, pl.BlockSpec((1, 8, d), lambda i: (i // tpb, 0, 0)),
                            _const_spec(w.shape), _const_spec((1, 2 * d)), _const_spec((1, d))],
        out_specs=x_spec, out_shape=jax.ShapeDtypeStruct((rows, td), F32),
        compiler_params=_cp("parallel"), name="s5_glu",
    )(*([gact] * nblk), x, mods, w, bias.reshape(1, 2 * d), post_g.reshape(1, d))


def _s5_layer(xl, xc, mods_l, mods_c, pre_g, post_g, lam_re, lam_im, log_dt, b_re, b_im, c_re, c_im,
              d_skip, w_glu, b_glu, b, n, c):
    d = xl.shape[-1]
    t = S5_T
    m_op, p_op, q_op, a1, a2 = _s5_operators(lam_re, lam_im, log_dt, b_re, b_im, c_re, c_im, d_skip)
    nblk = m_op.shape[0]
    half = a1.shape[1]
    hl = _normmod_call(xl, mods_l, pre_g, min(512, n), n).reshape(b * n // t, t * d)
    hc = _normmod_call(xc, mods_c, pre_g, c, c).reshape(b * c // t, t * d)

    def summaries(h, nk):
        s = _s5_sum_call(h, p_op, min(512, h.shape[0]))
        s = s.reshape(b, nk, nblk, 2, half // nblk)
        s = jnp.transpose(s, (3, 1, 0, 2, 4)).reshape(2, nk, b * half // LANE, LANE)
        return s[0], s[1]

    zeros = jnp.zeros((b * half // LANE, LANE), F32)
    sfc, sbc = summaries(hc, c // t)
    _, _, hf0, hb0 = _s5_rec_call(sfc, sbc, a1, a2, zeros, zeros)
    nk = n // t
    sfl, sbl = summaries(hl, nk)
    hf, hb, _, _ = _s5_rec_call(sfl, sbl, a1, a2, hf0, hb0)
    hcat = jnp.stack([hf.reshape(nk, b, nblk, half // nblk), hb.reshape(nk, b, nblk, half // nblk)], axis=3)
    hcat = jnp.transpose(hcat, (1, 0, 2, 3, 4)).reshape(b * nk, nblk * 2 * (half // nblk))
    rb = min(512, b * nk)
    gact = _s5_out_call(hl, hcat, m_op, q_op, rb)
    rbg = min(64, nk)
    out = _s5_glu_call(xl.reshape(b * nk, t * d), gact, mods_l, w_glu.astype(BF16), b_glu, post_g, rbg, nk)
    return out.reshape(b * n, d)


def _fn_a_kernel(x_ref, mod_ref, g_ref, cs_ref, fm_ref, twc_ref, tws_ref, or_ref, oi_ref, *, nt, d):
    g = g_ref[...]
    shift, scale = mod_ref[0, 0:1, :], mod_ref[0, 1:2, :]
    cs = cs_ref[...]
    fm = fm_ref[...]
    ng = d // FNET_GC
    for s in range(nt):
        h = _normmod(x_ref[0, :, s * d:(s + 1) * d], g, shift, scale).astype(BF16)
        ab = [_dot(h[:, k * FNET_GC:(k + 1) * FNET_GC], cs) for k in range(ng)]
        a = jnp.concatenate([z[:, :FNET_GC] for z in ab], axis=1)
        bb = jnp.concatenate([z[:, FNET_GC:] for z in ab], axis=1)
        x1 = _dot(fm, jnp.concatenate([a, bb], axis=0).astype(BF16))
        n1 = x1.shape[0] // 2
        xr, xi = x1[:n1], x1[n1:]
        c = jnp.concatenate([twc_ref[s]] * (d // LANE), axis=1)
        sn = jnp.concatenate([tws_ref[s]] * (d // LANE), axis=1)
        or_ref[0, s] = (xr * c + xi * sn).astype(or_ref.dtype)
        oi_ref[0, s] = (xi * c - xr * sn).astype(oi_ref.dtype)


def _fn_c_kernel(fm_ref, xr_ref, xi_ref, o_ref):
    o_ref[0] = _dot(fm_ref[...], jnp.concatenate([xr_ref[0], xi_ref[0]], axis=0)).astype(o_ref.dtype)


def _fnet_layer(xl, mods_l, pre_g, post_g, w_o, b_o, b, n, d):
    n2 = FFT_N2
    n1 = n // n2
    gc = FNET_GC
    cc, sc = _dft_cs(gc, gc, gc)
    cs = jnp.asarray(np.concatenate([cc, sc], axis=1) / np.sqrt(gc), BF16)
    c1, s1 = _dft_cs(n1, n1, n1)
    fm_a = jnp.asarray(np.block([[c1, -s1], [-s1, -c1]]) / np.sqrt(n), BF16)
    c2, s2 = _dft_cs(n2, n2, n2)
    fm_c = jnp.asarray(np.concatenate([c2, s2], axis=1), BF16)
    twc, tws = _twiddle_tables(n1, n2, lead_t2=True)
    nt = 4 if n2 % 4 == 0 else 1
    x3 = xl.reshape(b, n1, n2 * d)
    mid = jax.ShapeDtypeStruct((b, n2, n1, d), BF16)
    mid_spec = pl.BlockSpec((1, nt, n1, d), lambda bi, j: (bi, j, 0, 0))
    tw_spec = pl.BlockSpec((nt, n1, LANE), lambda bi, j: (j, 0, 0))
    xr, xi = pl.pallas_call(
        functools.partial(_fn_a_kernel, nt=nt, d=d), grid=(b, n2 // nt),
        in_specs=[pl.BlockSpec((1, n1, nt * d), lambda bi, j: (bi, 0, j)),
                  pl.BlockSpec((1, 8, d), lambda bi, j: (bi, 0, 0)), _const_spec((1, d)),
                  _const_spec(cs.shape), _const_spec(fm_a.shape), tw_spec, tw_spec],
        out_specs=[mid_spec, mid_spec], out_shape=[mid, mid],
        compiler_params=_cp("parallel", "parallel"), name="fn_a",
    )(x3, mods_l, pre_g.reshape(1, d), cs, fm_a, twc, tws)
    cb = 4 * d if n1 % 4 == 0 else d
    col = pl.BlockSpec((1, n2, cb), lambda bi, j: (bi, 0, j))
    y = pl.pallas_call(
        _fn_c_kernel, grid=(b, n1 * d // cb),
        in_specs=[_const_spec(fm_c.shape), col, col],
        out_specs=col, out_shape=jax.ShapeDtypeStruct((b, n2, n1 * d), BF16),
        compiler_params=_cp("parallel", "parallel"), name="fn_c",
    )(fm_c, xr.reshape(b, n2, n1 * d), xi.reshape(b, n2, n1 * d))
    return _post_call(xl, y.reshape(b * n, d), mods_l, w_o.astype(BF16), b_o, post_g, min(512, n), n)


def kernel(x, c, ctx, c_ctx, mod_w, mod_b, mix_pre_g, mix_post_g, ffn_pre_g, ffn_post_g, ffn_w13, ffn_w2,
           mla_w_in, mla_q_norm_g, mla_kv_norm_g, mla_w_uq, mla_w_ukv, mla_w_o,
           hy_w_in, hy_b_in, hy_conv_w, hy_conv_b, hy_f_w1, hy_f_b1, hy_f_freq, hy_f_w2, hy_f_b2, hy_f_w3,
           hy_skip, hy_w_out, hy_b_out,
           s5_lambda_re, s5_lambda_im, s5_log_dt, s5_b_re, s5_b_im, s5_c_re, s5_c_im, s5_d, s5_w_glu, s5_b_glu,
           fn_w_o, fn_b_o):
    b, n, d = x.shape
    cl = ctx.shape[1]
    depth = mod_w.shape[0]
    assert b == 2 and depth == 4, "two batches ride one complex transform; one layer per mixer"
    mods = _mods(c, c_ctx, mod_w, mod_b)
    xl = x.reshape(b * n, d)
    xc = ctx.reshape(b * cl, d)
    tm = min(512, n)

    def ffn(i, xl, xc, with_ctx):
        w13 = ffn_w13[i].astype(BF16)
        w2 = ffn_w2[i].astype(BF16)
        xl = _ffn_call(xl, mods[i, 0:2], ffn_pre_g[i], ffn_post_g[i], w13, w2, tm, n)
        if with_ctx:
            xc = _ffn_call(xc, mods_c(i), ffn_pre_g[i], ffn_post_g[i], w13, w2, cl, cl)
        return xl, xc

    def mods_c(i):
        return jnp.broadcast_to(mods[i, 2:3], (b, 8, d))

    xl, xc = _mla_layer(xl, xc, mods[0, 0:2], mods_c(0), mix_pre_g[0], mix_post_g[0], mla_w_in[0],
                        mla_q_norm_g[0], mla_kv_norm_g[0], mla_w_uq[0], mla_w_ukv[0], mla_w_o[0], b, n, cl)
    xl, xc = ffn(0, xl, xc, True)
    filt = (hy_f_w1[0], hy_f_b1[0], hy_f_freq[0], hy_f_w2[0], hy_f_b2[0], hy_f_w3[0])
    xl, xc = _hyena_layer(xl, xc, mods[1, 0:2], mods_c(1), mix_pre_g[1], mix_post_g[1], hy_w_in[0], hy_b_in[0],
                          hy_conv_w[0], hy_conv_b[0], filt, hy_skip[0], hy_w_out[0], hy_b_out[0], b, n, cl)
    xl, xc = ffn(1, xl, xc, True)
    xl = _s5_layer(xl, xc, mods[2, 0:2], mods_c(2), mix_pre_g[2], mix_post_g[2], s5_lambda_re[0],
                   s5_lambda_im[0], s5_log_dt[0], s5_b_re[0], s5_b_im[0], s5_c_re[0], s5_c_im[0], s5_d[0],
                   s5_w_glu[0], s5_b_glu[0], b, n, cl)
    xl, _ = ffn(2, xl, xc, False)
    xl = _fnet_layer(xl, mods[3, 0:2], mix_pre_g[3], mix_post_g[3], fn_w_o[0], fn_b_o[0], b, n, d)
    xl, _ = ffn(3, xl, xc, False)
    return xl.reshape(b, n, d)
```

```python
import functools
import math

import numpy as np
import jax
import jax.numpy as jnp
from jax import lax
from jax.experimental import pallas as pl
from jax.experimental.pallas import tpu as pltpu

F32 = jnp.float32
BF16 = jnp.bfloat16
NORM_EPS = 1e-6
LANE = 128
VMEM_LIMIT = 56 * 1024 * 1024
HIGHEST = lax.Precision.HIGHEST

GRID_W = 64
ROPE_THETA = 10000.0
MLA_HEADS = 8
MLA_NOPE = 128
MLA_ROPE = 64
MLA_V = 128
MLA_VT = MLA_V + 16
HYENA_BANDS = 16
HYENA_TARGET = 1e-2
HYENA_FAST = 0.3
HYENA_SLOW = 1.5
S5_GROUP = 16
S5_STATE = 64
S5_T = 16
FNET_GC = 128
FFT_N2 = 128


def _cp(*sem):
    return pltpu.CompilerParams(dimension_semantics=sem, vmem_limit_bytes=VMEM_LIMIT)


def _dot(a, b):
    return jnp.dot(a, b, preferred_element_type=F32)


def _dot_hi(a, b):
    return jnp.dot(a, b, preferred_element_type=F32, precision=HIGHEST)


def _rms(x, g):
    ms = jnp.mean(x * x, axis=-1, keepdims=True)
    return x * lax.rsqrt(ms + NORM_EPS) * g


def _normmod(x, g, shift, scale):
    return _rms(x, g) * (1.0 + scale) + shift


def _const_spec(shape):
    nd = len(shape)
    return pl.BlockSpec(shape, lambda *_: (0,) * nd)


def _mods_kernel(st_ref, w_ref, b_ref, o_ref):
    st = st_ref[...]
    st = st * jax.nn.sigmoid(st)
    w = w_ref[0]
    rows = [jnp.sum(st[:, r:r + 1] * w, axis=0, keepdims=True) for r in range(3)]
    rows.append(jnp.zeros((5, w.shape[1]), F32))
    o_ref[0] = jnp.concatenate(rows, axis=0) + b_ref[0]


def _mods(c, c_ctx, mod_w, mod_b):
    depth, d, n6 = mod_w.shape
    st = jnp.zeros((d, 8), F32).at[:, 0:2].set(c.T).at[:, 2].set(c_ctx)
    tn = 1024
    out = pl.pallas_call(
        _mods_kernel,
        grid=(depth, n6 // tn),
        in_specs=[_const_spec((d, 8)),
                  pl.BlockSpec((1, d, tn), lambda i, j: (i, 0, j)),
                  pl.BlockSpec((1, 1, tn), lambda i, j: (i, 0, j))],
        out_specs=pl.BlockSpec((1, 8, tn), lambda i, j: (i, 0, j)),
        out_shape=jax.ShapeDtypeStruct((depth, 8, n6), F32),
        compiler_params=_cp("parallel", "parallel"),
        name="mods",
    )(st, mod_w, mod_b.reshape(depth, 1, n6))
    m = out[:, :3].reshape(depth, 3, n6 // d, d)
    return jnp.pad(m, ((0, 0), (0, 0), (0, 8 - n6 // d), (0, 0)))


def _row_specs(tm, d, tpb):
    x_spec = pl.BlockSpec((tm, d), lambda i: (i, 0))
    mod_spec = pl.BlockSpec((1, 8, d), lambda i: (i // tpb, 0, 0))
    return x_spec, mod_spec


def _normmod_kernel(x_ref, mod_ref, g_ref, o_ref):
    h = _normmod(x_ref[...], g_ref[...], mod_ref[0, 0:1, :], mod_ref[0, 1:2, :])
    o_ref[...] = h.astype(o_ref.dtype)


def _normmod_call(x, mods, g, tm, rows_per_batch):
    m, d = x.shape
    x_spec, mod_spec = _row_specs(tm, d, rows_per_batch // tm)
    return pl.pallas_call(
        _normmod_kernel, grid=(m // tm,),
        in_specs=[x_spec, mod_spec, _const_spec((1, d))],
        out_specs=x_spec, out_shape=jax.ShapeDtypeStruct((m, d), BF16),
        compiler_params=_cp("parallel"), name="normmod",
    )(x, mods, g.reshape(1, d))


def _post_kernel(x_ref, y_ref, mod_ref, w_ref, b_ref, g_ref, o_ref, *, glu):
    z = _dot(y_ref[...].astype(BF16), w_ref[...]) + b_ref[...]
    if glu:
        d = o_ref.shape[-1]
        z = z[:, :d] * jax.nn.sigmoid(z[:, d:])
    o_ref[...] = x_ref[...] + mod_ref[0, 2:3, :] * _rms(z, g_ref[...])


def _post_call(x, y, mods, w, b, g, tm, rows_per_batch, glu=False):
    m, d = x.shape
    k, n = w.shape
    x_spec, mod_spec = _row_specs(tm, d, rows_per_batch // tm)
    return pl.pallas_call(
        functools.partial(_post_kernel, glu=glu), grid=(m // tm,),
        in_specs=[x_spec, pl.BlockSpec((tm, k), lambda i: (i, 0)), mod_spec,
                  _const_spec((k, n)), _const_spec((1, n)), _const_spec((1, d))],
        out_specs=x_spec, out_shape=jax.ShapeDtypeStruct((m, d), F32),
        compiler_params=_cp("parallel"), name="post",
    )(x, y, mods, w, b.reshape(1, n), g.reshape(1, d))


def _ffn_kernel(x_ref, mod_ref, pre_ref, post_ref, w13_ref, w2_ref, o_ref, *, f, fc):
    x = x_ref[...]
    h = _normmod(x, pre_ref[...], mod_ref[0, 3:4, :], mod_ref[0, 4:5, :]).astype(BF16)
    acc = None
    for c in range(f // fc):
        a = _dot(h, w13_ref[:, c * fc:(c + 1) * fc])
        b = _dot(h, w13_ref[:, f + c * fc:f + (c + 1) * fc])
        gact = (a * jax.nn.sigmoid(a) * b).astype(BF16)
        part = _dot(gact, w2_ref[c * fc:(c + 1) * fc, :])
        acc = part if acc is None else acc + part
    o_ref[...] = x + mod_ref[0, 5:6, :] * _rms(acc, post_ref[...])


def _ffn_call(x, mods, pre_g, post_g, w13, w2, tm, rows_per_batch):
    m, d = x.shape
    f = w2.shape[0]
    fc = f // 2 if (f // 2) % LANE == 0 else f
    x_spec, mod_spec = _row_specs(tm, d, rows_per_batch // tm)
    return pl.pallas_call(
        functools.partial(_ffn_kernel, f=f, fc=fc), grid=(m // tm,),
        in_specs=[x_spec, mod_spec, _const_spec((1, d)), _const_spec((1, d)),
                  pl.BlockSpec((d, 2 * f), lambda i: (0, 0), pipeline_mode=pl.Buffered(1)),
                  pl.BlockSpec((f, d), lambda i: (0, 0), pipeline_mode=pl.Buffered(1))],
        out_specs=x_spec, out_shape=jax.ShapeDtypeStruct((m, d), F32),
        compiler_params=_cp("parallel"), name="ffn",
    )(x, mods, pre_g.reshape(1, d), post_g.reshape(1, d), w13, w2)


def _mla_in_kernel(x_ref, mod_ref, g_ref, w_ref, qg_ref, kvg_ref, t1_ref, t2_ref,
                   qn_ref, cn_ref, kr_ref, *, ql, kvl):
    h = _normmod(x_ref[...], g_ref[...], mod_ref[0, 0:1, :], mod_ref[0, 1:2, :])
    z = _dot(h.astype(BF16), w_ref[...])
    qn_ref[...] = _rms(z[:, :ql], qg_ref[...]).astype(BF16)
    cn_ref[...] = _rms(z[:, ql:ql + kvl], kvg_ref[...]).astype(BF16)
    pair = z[:, ql + kvl:]
    kr_ref[...] = (pair * t1_ref[...] + pltpu.roll(pair, 64, 1) * t2_ref[...]).astype(BF16)


def _mla_q_kernel(qn_ref, w_ref, t1_ref, t2_ref, q_ref, *, qscale):
    z = _dot(qn_ref[...], w_ref[...])
    t1 = t1_ref[...]
    t2 = t2_ref[...]
    for h in range(MLA_HEADS):
        base = h * 256
        pair = z[:, base + 128:base + 256]
        rp = pair * t1 + pltpu.roll(pair, 64, 1) * t2
        qcat = jnp.concatenate([z[:, base:base + 128], rp], axis=1) * qscale
        q_ref[0, h] = qcat.T.astype(BF16)


def _mla_kv_kernel(cn_ref, kr_ref, w_ref, k_ref, vt_ref):
    z = _dot(cn_ref[...], w_ref[...])
    kr = kr_ref[...]
    tk = kr.shape[0]
    ones_blk = (lax.broadcasted_iota(jnp.int32, (MLA_VT - MLA_V, tk), 0) == 0).astype(BF16)
    for h in range(MLA_HEADS):
        base = h * 256
        k_ref[0, h, 0, :, 0:128] = z[:, base:base + 128].astype(BF16)
        k_ref[0, h, 0, :, 128:256] = kr
        vt_ref[0, h, 0, 0:MLA_V, :] = z[:, base + 128:base + 256].T.astype(BF16)
        vt_ref[0, h, 0, MLA_V:MLA_VT, :] = ones_blk


def _flash_kernel(q_ref, kc_ref, vc_ref, *rest, n_lat):
    if n_lat:
        kl_ref, vl_ref, o_ref, s_scr, acc_scr = rest
    else:
        o_ref, acc_scr = rest
    qt = q_ref[0, 0]

    def qk(k, slot):
        s = _dot(k, qt)
        s_scr[slot] = s
        return jnp.max(s, axis=0, keepdims=True)

    def sm_pv(slot, vt, m, mx):
        m_new = jnp.maximum(m, mx)
        alpha = jnp.exp2(m - m_new)
        p = jnp.exp2(s_scr[slot] - m_new).astype(BF16)
        acc_scr[...] = alpha * acc_scr[...] + _dot(vt, p)
        return m_new

    sc = _dot(kc_ref[0, 0, 0], qt)
    m = jnp.max(sc, axis=0, keepdims=True)
    acc_scr[...] = _dot(vc_ref[0, 0, 0], jnp.exp2(sc - m).astype(BF16))
    if n_lat:
        mx = qk(kl_ref[0, 0, 0], 0)

        def body(i, carry):
            m, mx0 = carry
            c = 2 * i
            mx1 = qk(kl_ref[0, 0, c + 1], 1)
            m = sm_pv(0, vl_ref[0, 0, c], m, mx0)
            mx0 = qk(kl_ref[0, 0, jnp.minimum(c + 2, n_lat - 1)], 0)
            m = sm_pv(1, vl_ref[0, 0, c + 1], m, mx1)
            return m, mx0

        lax.fori_loop(0, n_lat // 2, body, (m, mx))
    acc = acc_scr[...]
    o_ref[0] = (acc[0:MLA_V] / acc[MLA_V:MLA_V + 1]).T.astype(o_ref.dtype)


def _rope_tables(n_lat):
    rows = n_lat // GRID_W
    row = jnp.repeat(jnp.arange(rows, dtype=F32), GRID_W)
    col = jnp.tile(jnp.arange(GRID_W, dtype=F32), rows)
    axis_dim = MLA_ROPE // 2
    inv_freq = 1.0 / (ROPE_THETA ** (jnp.arange(0, axis_dim, 2, dtype=F32) / axis_dim))
    ang_r = row[:, None] * inv_freq
    ang_c = col[:, None] * inv_freq
    cr, sr, cc, sc = jnp.cos(ang_r), jnp.sin(ang_r), jnp.cos(ang_c), jnp.sin(ang_c)
    cp = jnp.concatenate([cr, cr, cc, cc], axis=-1)
    sp = jnp.concatenate([-sr, sr, -sc, sc], axis=-1)
    return cp, sp


_ROPE_SWAP = np.concatenate([np.arange(16, 32), np.arange(0, 16), np.arange(48, 64), np.arange(32, 48)])


def _mla_side(x, mods, pre_g, w_in_ext, q_g, kv_g, w_uq_ext, w_ukv, tabs, b, n, tm, tk, want_q):
    m, d = x.shape
    ql, kvl = q_g.shape[-1], kv_g.shape[-1]
    t1k, t2k, t1q, t2q = tabs
    x_spec, mod_spec = _row_specs(tm, d, n // tm)
    row = lambda w: pl.BlockSpec((tm, w), lambda i: (i, 0))
    tab_spec = pl.BlockSpec((tm, 128), lambda i: (i % (n // tm), 0))
    qn, cn, kr = pl.pallas_call(
        functools.partial(_mla_in_kernel, ql=ql, kvl=kvl), grid=(m // tm,),
        in_specs=[x_spec, mod_spec, _const_spec((1, d)), _const_spec(w_in_ext.shape),
                  _const_spec((1, ql)), _const_spec((1, kvl)), tab_spec, tab_spec],
        out_specs=[row(ql), row(kvl), row(128)],
        out_shape=[jax.ShapeDtypeStruct((m, ql), BF16), jax.ShapeDtypeStruct((m, kvl), BF16),
                   jax.ShapeDtypeStruct((m, 128), BF16)],
        compiler_params=_cp("parallel"), name="mla_in",
    )(x, mods, pre_g.reshape(1, d), w_in_ext, q_g.reshape(1, ql), kv_g.reshape(1, kvl), t1k, t2k)

    nc = n // tk
    kt, v = pl.pallas_call(
        _mla_kv_kernel, grid=(b, nc),
        in_specs=[pl.BlockSpec((tk, kvl), lambda bi, c: (bi * nc + c, 0)),
                  pl.BlockSpec((tk, 128), lambda bi, c: (bi * nc + c, 0)),
                  _const_spec(w_ukv.shape)],
        out_specs=[pl.BlockSpec((1, MLA_HEADS, 1, tk, 256), lambda bi, c: (bi, 0, c, 0, 0)),
                   pl.BlockSpec((1, MLA_HEADS, 1, MLA_VT, tk), lambda bi, c: (bi, 0, c, 0, 0))],
        out_shape=[jax.ShapeDtypeStruct((b, MLA_HEADS, nc, tk, 256), BF16),
                   jax.ShapeDtypeStruct((b, MLA_HEADS, nc, MLA_VT, tk), BF16)],
        compiler_params=_cp("parallel", "parallel"), name="mla_kv",
    )(cn, kr, w_ukv)

    q = None
    if want_q:
        qscale = (MLA_NOPE + MLA_ROPE) ** -0.5 * math.log2(math.e)
        tpb = n // tm
        q = pl.pallas_call(
            functools.partial(_mla_q_kernel, qscale=qscale), grid=(m // tm,),
            in_specs=[row(ql), _const_spec(w_uq_ext.shape), tab_spec, tab_spec],
            out_specs=pl.BlockSpec((1, MLA_HEADS, 256, tm), lambda i: (i // tpb, 0, 0, i % tpb)),
            out_shape=jax.ShapeDtypeStruct((b, MLA_HEADS, 256, n), BF16),
            compiler_params=_cp("parallel"), name="mla_q",
        )(qn, w_uq_ext, t1q, t2q)
    return q, kt, v


def _flash_call(qt, kc, vtc, kl, vtl, tq):
    b, hh, _, n = qt.shape
    c = kc.shape[-2]
    n_lat = 0 if kl is None else kl.shape[2]
    in_specs = [pl.BlockSpec((1, 1, 256, tq), lambda bi, h, i: (bi, h, 0, i)),
                pl.BlockSpec((1, 1, 1, c, 256), lambda bi, h, i: (bi, h, 0, 0, 0)),
                pl.BlockSpec((1, 1, 1, MLA_VT, c), lambda bi, h, i: (bi, h, 0, 0, 0))]
    args = [qt, kc, vtc]
    scratch = [pltpu.VMEM((MLA_VT, tq), F32)]
    if n_lat:
        assert n_lat % 2 == 0, "latent key chunks are consumed in pairs"
        tk = kl.shape[-2]
        in_specs += [pl.BlockSpec((1, 1, n_lat, tk, 256), lambda bi, h, i: (bi, h, 0, 0, 0)),
                     pl.BlockSpec((1, 1, n_lat, MLA_VT, tk), lambda bi, h, i: (bi, h, 0, 0, 0))]
        args += [kl, vtl]
        scratch = [pltpu.VMEM((2, tk, tq), F32)] + scratch
    return pl.pallas_call(
        functools.partial(_flash_kernel, n_lat=n_lat), grid=(b, hh, n // tq),
        in_specs=in_specs,
        out_specs=pl.BlockSpec((1, tq, MLA_V), lambda bi, h, i: (bi, i, h)),
        out_shape=jax.ShapeDtypeStruct((b, n, hh * MLA_V), BF16),
        scratch_shapes=scratch,
        compiler_params=_cp("parallel", "parallel", "arbitrary"), name="flash",
    )(*args)


def _mla_layer(xl, xc, mods_l, mods_c, pre_g, post_g, w_in, q_g, kv_g, w_uq, w_ukv, w_o, b, n, c):
    d = xl.shape[-1]
    ql, kvl = q_g.shape[-1], kv_g.shape[-1]
    hh = MLA_HEADS
    rope_cols = w_in[:, ql + kvl:]
    w_in_ext = jnp.concatenate([w_in, rope_cols[:, _ROPE_SWAP]], axis=1).astype(BF16)
    wq = w_uq.reshape(ql, hh, MLA_NOPE + MLA_ROPE)
    w_uq_ext = jnp.concatenate([wq, wq[:, :, MLA_NOPE:][:, :, _ROPE_SWAP]], axis=-1)
    w_uq_ext = w_uq_ext.reshape(ql, hh * 256).astype(BF16)
    w_ukv_b = w_ukv.astype(BF16)
    w_o_b = w_o.astype(BF16)

    cp, sp = _rope_tables(n)
    z64l, o64l = jnp.zeros((n, 64), F32), jnp.ones((n, 64), F32)
    z64c, o64c = jnp.zeros((c, 64), F32), jnp.ones((c, 64), F32)
    cat = lambda a, bb: jnp.concatenate([a, bb], axis=1)
    tabs_l = (cat(cp, z64l), cat(sp, z64l), cat(cp, z64l), cat(sp, o64l))
    tabs_c = (cat(z64c, z64c), cat(z64c, o64c), cat(z64c, z64c), cat(z64c, o64c))

    tm_l = min(512, n)
    tk_l = min(512, n // 2)
    ql_, kl, vtl = _mla_side(xl, mods_l, pre_g, w_in_ext, q_g, kv_g, w_uq_ext, w_ukv_b, tabs_l,
                             b, n, tm_l, tk_l, True)
    qc_, kc, vtc = _mla_side(xc, mods_c, pre_g, w_in_ext, q_g, kv_g, w_uq_ext, w_ukv_b, tabs_c,
                             b, c, c, c, True)
    o_lat = _flash_call(ql_, kc, vtc, kl, vtl, min(1024, n)).reshape(b * n, hh * MLA_V)
    o_ctx = _flash_call(qc_, kc, vtc, None, None, c).reshape(b * c, hh * MLA_V)
    zb = jnp.zeros((d,), F32)
    xl = _post_call(xl, o_lat, mods_l, w_o_b, zb, post_g, tm_l, n)
    xc = _post_call(xc, o_ctx, mods_c, w_o_b, zb, post_g, c, c)
    return xl, xc


def _hy_in_kernel(x_ref, xp_ref, xn_ref, mod_ref, g_ref, w_ref, b_ref, cw_ref, cb_ref,
                  g0_ref, vg_ref, *, tpb):
    i = pl.program_id(0)
    g = g_ref[...]
    shift, scale = mod_ref[0, 0:1, :], mod_ref[0, 1:2, :]
    xcat = jnp.concatenate([xp_ref[...], x_ref[...], xn_ref[...]], axis=0)
    ucat = _dot(_normmod(xcat, g, shift, scale).astype(BF16), w_ref[...]) + b_ref[...]
    tm = x_ref.shape[0]
    u = ucat[8:tm + 8]
    first = (i % tpb) == 0
    last = (i % tpb) == tpb - 1
    prev_row = jnp.where(first, 0.0, ucat[7:8, :])
    next_row = jnp.where(last, 0.0, ucat[tm + 8:tm + 9, :])
    ridx = lax.broadcasted_iota(jnp.int32, (tm, 1), 0)
    dn = jnp.where(ridx == 0, prev_row, pltpu.roll(u, 1, 0))
    upw = jnp.where(ridx == tm - 1, next_row, pltpu.roll(u, tm - 1, 0))
    conv = cb_ref[...] + dn * cw_ref[0:1, :] + u * cw_ref[1:2, :] + upw * cw_ref[2:3, :]
    d = g0_ref.shape[-1]
    g0_ref[...] = conv[:, :d]
    vg_ref[...] = conv[:, 2 * d:] * conv[:, d:2 * d]


def _hy_in_call(x, mods, pre_g, w_in, b_in, conv_w, conv_b, tm, n):
    m, d = x.shape
    p = w_in.shape[1]
    tpb = n // tm
    x_spec, mod_spec = _row_specs(tm, d, tpb)
    r8 = tm // 8
    nb8 = m // 8
    prev_spec = pl.BlockSpec((8, d), lambda i: (jnp.maximum(i * r8 - 1, 0), 0))
    next_spec = pl.BlockSpec((8, d), lambda i: (jnp.minimum((i + 1) * r8, nb8 - 1), 0))
    cw = jnp.pad(conv_w, ((0, 8 - conv_w.shape[0]), (0, 0)))
    return pl.pallas_call(
        functools.partial(_hy_in_kernel, tpb=tpb), grid=(m // tm,),
        in_specs=[x_spec, prev_spec, next_spec, mod_spec, _const_spec((1, d)), _const_spec((d, p)),
                  _const_spec((1, p)), _const_spec((8, p)), _const_spec((1, p))],
        out_specs=[x_spec, x_spec],
        out_shape=[jax.ShapeDtypeStruct((m, d), F32), jax.ShapeDtypeStruct((m, d), F32)],
        compiler_params=_cp("parallel"), name="hy_in",
    )(x, x, x, mods, pre_g.reshape(1, d), w_in, b_in.reshape(1, p), cw, conv_b.reshape(1, p))


def _hy_filter_kernel(bands_ref, w1_ref, b1_ref, fq_ref, w2_ref, b2_ref, w3_ref, dl_ref,
                      kf_ref, gb_ref, nrm_ref, *, n, tr):
    i = pl.program_id(0)
    j = (lax.broadcasted_iota(jnp.int32, (tr, LANE), 0) + i * tr).astype(F32)
    lane = lax.broadcasted_iota(jnp.int32, (tr, LANE), 1)
    t = j * (1.0 / (n - 1))
    arg = (2.0 * math.pi / n) * j * bands_ref[...]
    z = jnp.where(lane == 0, t,
                  jnp.where(lane <= HYENA_BANDS, jnp.cos(arg),
                            jnp.where(lane <= 2 * HYENA_BANDS, -jnp.sin(arg), 0.0)))
    fq = fq_ref[...]
    a = jnp.sin(fq * (_dot_hi(z, w1_ref[...]) + b1_ref[...]))
    for k in range(w2_ref.shape[0]):
        a = jnp.sin(fq * (_dot_hi(a, w2_ref[k]) + b2_ref[k]))
    h = _dot_hi(a, w3_ref[...])
    d = kf_ref.shape[-1]
    decay = jnp.exp(-t[:, 0:1] * dl_ref[...])
    kf = h[:, :d] * decay
    gb = jnp.where(j[:, 0:1] == 0.0, 0.0, h[:, d:] * decay)
    kf_ref[...] = kf
    gb_ref[...] = gb
    part = jnp.sum(jnp.abs(kf) + jnp.abs(gb), axis=0, keepdims=True)

    @pl.when(i == 0)
    def _():
        nrm_ref[...] = jnp.zeros_like(nrm_ref)

    nrm_ref[...] += jnp.broadcast_to(part, nrm_ref.shape)


def _hy_filter_call(n, d, f_w1, f_b1, f_freq, f_w2, f_b2, f_w3):
    fw = f_w1.shape[1]
    tr = min(512, n)
    bands_np = np.zeros((1, LANE), np.float32)
    bands_np[0, 1:1 + HYENA_BANDS] = np.linspace(1e-4, HYENA_BANDS - 1, HYENA_BANDS, dtype=np.float32)
    bands_np[0, 1 + HYENA_BANDS:1 + 2 * HYENA_BANDS] = bands_np[0, 1:1 + HYENA_BANDS]
    w1p = jnp.zeros((LANE, fw), F32).at[:f_w1.shape[0]].set(f_w1)
    deltas = jnp.abs(jnp.linspace(math.log(HYENA_TARGET) / HYENA_SLOW, math.log(HYENA_TARGET) / HYENA_FAST,
                                  d, dtype=F32)).reshape(1, d)
    row = pl.BlockSpec((tr, d), lambda i: (i, 0))
    return pl.pallas_call(
        functools.partial(_hy_filter_kernel, n=n, tr=tr), grid=(n // tr,),
        in_specs=[_const_spec((1, LANE)), _const_spec((LANE, fw)), _const_spec((1, fw)), _const_spec((1, fw)),
                  _const_spec(f_w2.shape), _const_spec((f_w2.shape[0], 1, fw)), _const_spec(f_w3.shape),
                  _const_spec((1, d))],
        out_specs=[row, row, _const_spec((8, d))],
        out_shape=[jax.ShapeDtypeStruct((n, d), F32), jax.ShapeDtypeStruct((n, d), F32),
                   jax.ShapeDtypeStruct((8, d), F32)],
        compiler_params=_cp("arbitrary"), name="hy_filter",
    )(jnp.asarray(bands_np), w1p, f_b1.reshape(1, fw), f_freq.reshape(1, fw), f_w2,
      f_b2.reshape(f_w2.shape[0], 1, fw), f_w3, deltas)


def _dft_cs(nf, nt, period):
    ft = (np.arange(nf)[:, None] * np.arange(nt)[None, :]) % period
    ang = 2.0 * np.pi * ft / period
    return np.cos(ang), np.sin(ang)


def _twiddle_tables(n1, n2, lead_t2):
    nn = n1 * n2
    f1 = jnp.arange(n1, dtype=jnp.int32)
    t2 = jnp.arange(n2, dtype=jnp.int32)
    idx = (t2[:, None] * f1[None, :]) % nn if lead_t2 else (f1[:, None] * t2[None, :]) % nn
    ang = idx.astype(F32) * (2.0 * math.pi / nn)
    shape = idx.shape + (LANE,)
    return (jnp.broadcast_to(jnp.cos(ang)[..., None], shape),
            jnp.broadcast_to(jnp.sin(ang)[..., None], shape))


def _fft_a_kernel(*refs, n_in, nt, d):
    fm_ref = refs[0]
    in_refs = refs[1:1 + n_in]
    twc_ref, tws_ref, or_ref, oi_ref = refs[1 + n_in:]
    xs = [r[0] for r in in_refs]
    x = (jnp.concatenate(xs, axis=0) if n_in > 1 else xs[0]).astype(BF16)
    out = _dot(fm_ref[...], x)
    n1 = out.shape[0] // 2
    xr, xi = out[:n1], out[n1:]
    for s in range(nt):
        c = jnp.concatenate([twc_ref[s]] * (d // LANE), axis=1)
        sn = jnp.concatenate([tws_ref[s]] * (d // LANE), axis=1)
        a = xr[:, s * d:(s + 1) * d]
        bb = xi[:, s * d:(s + 1) * d]
        or_ref[:, s * d:(s + 1) * d] = (a * c + bb * sn).astype(or_ref.dtype)
        oi_ref[:, s * d:(s + 1) * d] = (bb * c - a * sn).astype(oi_ref.dtype)


def _fft_a_call(fm, ins, twc, tws, n1, n2, d, nt):
    rows = ins[0][0].shape[1]
    in_specs = [_const_spec(fm.shape)]
    args = [fm]
    for arr, bi in ins:
        in_specs.append(pl.BlockSpec((1, rows, nt * d), lambda j, bi=bi: (bi, 0, j)))
        args.append(arr)
    tw_spec = pl.BlockSpec((nt, n1, LANE), lambda j: (j, 0, 0))
    out_spec = pl.BlockSpec((n1, nt * d), lambda j: (0, j))
    return pl.pallas_call(
        functools.partial(_fft_a_kernel, n_in=len(ins), nt=nt, d=d), grid=(n2 // nt,),
        in_specs=in_specs + [tw_spec, tw_spec],
        out_specs=[out_spec, out_spec],
        out_shape=[jax.ShapeDtypeStruct((n1, n2 * d), BF16)] * 2,
        compiler_params=_cp("parallel"), name="fft_a",
    )(*args, twc, tws)


def _hy_kb_kernel(fm_ref, fr_ref, fi_ref, gr_ref, gi_ref, sc_ref, kr_ref, ki_ref, *, nf):
    fm = fm_ref[...]
    n2 = fr_ref.shape[1]
    sc = sc_ref[...]
    for s in range(nf):
        kf = _dot(fm, jnp.concatenate([fr_ref[s], fi_ref[s]], axis=0))
        gb = _dot(fm, jnp.concatenate([gr_ref[s], gi_ref[s]], axis=0))
        kr_ref[s] = (kf[:n2] + gb[:n2]) * sc
        ki_ref[s] = (kf[n2:] - gb[n2:]) * sc


def _hy_b_kernel(fm_ref, fmc_ref, xr_ref, xi_ref, kr_ref, ki_ref, twc_ref, tws_ref,
                 or_ref, oi_ref, *, nf, d):
    fm = fm_ref[...]
    fmc = fmc_ref[...]
    n2 = xr_ref.shape[1]
    for s in range(nf):
        x = _dot(fm, jnp.concatenate([xr_ref[s], xi_ref[s]], axis=0))
        xr, xi = x[:n2], x[n2:]
        kr, ki = kr_ref[s], ki_ref[s]
        yr = (xr * kr - xi * ki).astype(BF16)
        yi = (xr * ki + xi * kr).astype(BF16)
        g = _dot(fmc, jnp.concatenate([yr, yi], axis=0))
        gr, gi = g[:n2], g[n2:]
        c = jnp.concatenate([twc_ref[s]] * (d // LANE), axis=1)
        sn = jnp.concatenate([tws_ref[s]] * (d // LANE), axis=1)
        or_ref[s] = (gr * c - gi * sn).astype(or_ref.dtype)
        oi_ref[s] = (gi * c + gr * sn).astype(oi_ref.dtype)


def _hy_c_kernel(fm_ref, gr_ref, gi_ref, vg_ref, g0_ref, skip_ref, o_ref):
    y = _dot(fm_ref[...], jnp.concatenate([gr_ref[...], gi_ref[...]], axis=0))
    half = y.shape[0] // 2
    skip = skip_ref[...]
    for bi in range(2):
        yb = y[bi * half:(bi + 1) * half]
        o_ref[bi] = ((yb + vg_ref[bi] * skip) * g0_ref[bi]).astype(o_ref.dtype)


def _block_c(cs, sn, sign):
    return np.block([[cs, -sign * sn], [sign * sn, cs]])


def _hy_conv_long(vg, g0, kf, gb, nrm, skip, b, n, d):
    n2 = FFT_N2
    nn = 2 * n
    n1 = nn // n2
    rows = n // n2
    cs1, sn1 = _dft_cs(n1, rows, n1)
    fm_a = jnp.asarray(_block_c(cs1, sn1, -1.0), BF16)
    fm_a_real = jnp.asarray(np.concatenate([cs1, -sn1], axis=0), BF16)
    cs2, sn2 = _dft_cs(n2, n2, n2)
    fm_b = jnp.asarray(_block_c(cs2, sn2, -1.0), BF16)
    fm_bc = jnp.asarray(_block_c(cs2, sn2, 1.0), BF16)
    cs1i, sn1i = _dft_cs(rows, n1, n1)
    fm_c = jnp.asarray(_block_c(cs1i, sn1i, 1.0), BF16)
    twc_a, tws_a = _twiddle_tables(n1, n2, lead_t2=True)
    twc_b, tws_b = _twiddle_tables(n1, n2, lead_t2=False)

    nt = 2 if n2 % 2 == 0 else 1
    nf = 4 if n1 % 4 == 0 else 1
    kf3 = kf.reshape(1, rows, n2 * d)
    gb3 = gb.reshape(1, rows, n2 * d)
    kfr, kfi = _fft_a_call(fm_a_real, [(kf3, 0)], twc_a, tws_a, n1, n2, d, nt)
    gbr, gbi = _fft_a_call(fm_a_real, [(gb3, 0)], twc_a, tws_a, n1, n2, d, nt)
    scale = (1.0 / (nrm[0:1, :] * nn))
    slab = pl.BlockSpec((nf, n2, d), lambda j: (j, 0, 0))
    shp3 = (n1, n2, d)
    khr, khi = pl.pallas_call(
        functools.partial(_hy_kb_kernel, nf=nf), grid=(n1 // nf,),
        in_specs=[_const_spec(fm_b.shape), slab, slab, slab, slab, _const_spec((1, d))],
        out_specs=[slab, slab],
        out_shape=[jax.ShapeDtypeStruct(shp3, F32)] * 2,
        compiler_params=_cp("parallel"), name="hy_kb",
    )(fm_b, kfr.reshape(shp3), kfi.reshape(shp3), gbr.reshape(shp3), gbi.reshape(shp3), scale)

    vg3 = vg.reshape(b, rows, n2 * d)
    g03 = g0.reshape(b, rows, n2 * d)
    x1r, x1i = _fft_a_call(fm_a, [(vg3, 0), (vg3, 1)], twc_a, tws_a, n1, n2, d, nt)
    tw_slab = pl.BlockSpec((nf, n2, LANE), lambda j: (j, 0, 0))
    g1r, g1i = pl.pallas_call(
        functools.partial(_hy_b_kernel, nf=nf, d=d), grid=(n1 // nf,),
        in_specs=[_const_spec(fm_b.shape), _const_spec(fm_bc.shape), slab, slab, slab, slab,
                  tw_slab, tw_slab],
        out_specs=[slab, slab],
        out_shape=[jax.ShapeDtypeStruct(shp3, BF16)] * 2,
        compiler_params=_cp("parallel"), name="hy_b",
    )(fm_b, fm_bc, x1r.reshape(shp3), x1i.reshape(shp3), khr, khi, twc_b, tws_b)

    cb = nt * d
    col = pl.BlockSpec((n1, cb), lambda j: (0, j))
    col3 = pl.BlockSpec((b, rows, cb), lambda j: (0, 0, j))
    out = pl.pallas_call(
        _hy_c_kernel, grid=(n2 * d // cb,),
        in_specs=[_const_spec(fm_c.shape), col, col, col3, col3, _const_spec((1, cb))],
        out_specs=col3,
        out_shape=jax.ShapeDtypeStruct((b, rows, n2 * d), BF16),
        compiler_params=_cp("parallel"), name="hy_c",
    )(fm_c, g1r.reshape(n1, n2 * d), g1i.reshape(n1, n2 * d), vg3, g03, jnp.tile(skip.reshape(1, d), (1, nt)))
    return out.reshape(b * n, d)


def _hy_short_kernel(fa_ref, fk_ref, fi_ref, vg_ref, g0_ref, kf_ref, gb_ref, nrm_ref, skip_ref, o_ref, *, n):
    z = jnp.concatenate([vg_ref[0], vg_ref[1]], axis=0)
    x = _dot_hi(fa_ref[...], z)
    kk = _dot_hi(fk_ref[...], jnp.concatenate([kf_ref[...], gb_ref[...]], axis=0))
    nn = 2 * n
    sc = 1.0 / (nrm_ref[0:1, :] * nn)
    xr, xi = x[:nn], x[nn:]
    kr, ki = kk[:nn] * sc, kk[nn:] * sc
    y = _dot_hi(fi_ref[...], jnp.concatenate([xr * kr - xi * ki, xr * ki + xi * kr], axis=0))
    skip = skip_ref[...]
    for bi in range(2):
        o_ref[bi] = ((y[bi * n:(bi + 1) * n] + vg_ref[bi] * skip) * g0_ref[bi]).astype(o_ref.dtype)


def _hy_conv_short(vg, g0, kf, gb, nrm, skip, b, n, d):
    nn = 2 * n
    cs, sn = _dft_cs(nn, n, nn)
    fa = jnp.asarray(_block_c(cs, sn, -1.0), F32)
    fk = jnp.asarray(np.block([[cs, cs], [-sn, sn]]), F32)
    csi, sni = _dft_cs(n, nn, nn)
    fi = jnp.asarray(_block_c(csi, sni, 1.0), F32)
    cb = 256
    col3 = pl.BlockSpec((b, n, cb), lambda j: (0, 0, j))
    col = pl.BlockSpec((n, cb), lambda j: (0, j))
    vec = pl.BlockSpec((1, cb), lambda j: (0, j))
    out = pl.pallas_call(
        functools.partial(_hy_short_kernel, n=n), grid=(d // cb,),
        in_specs=[_const_spec(fa.shape), _const_spec(fk.shape), _const_spec(fi.shape), col3, col3, col, col,
                  pl.BlockSpec((8, cb), lambda j: (0, j)), vec],
        out_specs=col3, out_shape=jax.ShapeDtypeStruct((b, n, d), BF16),
        compiler_params=_cp("parallel"), name="hy_short",
    )(fa, fk, fi, vg.reshape(b, n, d), g0.reshape(b, n, d), kf, gb, nrm, skip.reshape(1, d))
    return out.reshape(b * n, d)


def _hyena_layer(xl, xc, mods_l, mods_c, pre_g, post_g, w_in, b_in, conv_w, conv_b, filt, skip,
                 w_out, b_out, b, n, c):
    d = xl.shape[-1]
    w_in_b = w_in.astype(BF16)
    w_out_b = w_out.astype(BF16)
    tm = min(512, n)
    g0, vg = _hy_in_call(xl, mods_l, pre_g, w_in_b, b_in, conv_w, conv_b, tm, n)
    kf, gb, nrm = _hy_filter_call(n, d, *filt)
    u_out = _hy_conv_long(vg, g0, kf, gb, nrm, skip[0], b, n, d)
    xl = _post_call(xl, u_out, mods_l, w_out_b, b_out, post_g, tm, n)

    g0c, vgc = _hy_in_call(xc, mods_c, pre_g, w_in_b, b_in, conv_w, conv_b, c, c)
    kfc, gbc, nrmc = _hy_filter_call(c, d, *filt)
    u_out_c = _hy_conv_short(vgc, g0c, kfc, gbc, nrmc, skip[0], b, c, d)
    xc = _post_call(xc, u_out_c, mods_c, w_out_b, b_out, post_g, c, c)
    return xl, xc


def _s5_operators(lam_re, lam_im, log_dt, b_re, b_im, c_re, c_im, d_skip):
    t = S5_T
    g, ns = lam_re.shape[1], lam_re.shape[2]
    gc = b_re.shape[-1]
    gl = LANE // gc
    nblk = g // gl
    lam = lax.complex(lam_re, lam_im)
    dt = jnp.exp(log_dt)[..., None]
    lam_bar = jnp.exp(lam * dt)
    b_bar = ((lam_bar - 1.0) / lam)[..., None] * lax.complex(b_re, b_im)
    c_mat = lax.complex(c_re, c_im)
    pw = jnp.arange(t + 1, dtype=F32)
    lam_pw = jnp.exp((lam * dt)[None] * pw[:, None, None, None])
    hp = HIGHEST
    kern = jnp.einsum('dgcn,tdgn,dgne->dgtce', c_mat, lam_pw[:t], b_bar, precision=hp).real
    dsk = d_skip.reshape(g, gc)
    kt = jnp.swapaxes(kern, -1, -2)
    centre = kt[0][:, 0] + kt[1][:, 0] + jnp.eye(gc, dtype=F32)[None] * dsk[:, :, None]
    ks = jnp.concatenate([kt[1][:, 1:][:, ::-1], centre[:, None], kt[0][:, 1:]], axis=1)
    eye_gl = jnp.eye(gl, dtype=F32)
    ks = ks.reshape(nblk, gl, 2 * t - 1, gc, gc)
    d_tab = jnp.einsum('bglec,gh->blgehc', ks, eye_gl).reshape(nblk, 2 * t - 1, LANE, LANE)

    def compact(z_ri):
        z = jnp.transpose(z_ri, (3, 2, 1, 4, 0, 5)).reshape(nblk, gl, t, 2, gc, 2 * ns)
        return jnp.transpose(z, (0, 2, 1, 3, 4, 5))

    pf = lam_pw[:t][::-1][:, 0, :, :, None] * b_bar[0][None]
    pb = lam_pw[:t][:, 1, :, :, None] * b_bar[1][None]
    pcat = jnp.swapaxes(jnp.stack([pf, pb], axis=0), -1, -2)
    p_tab = compact(jnp.stack([pcat.real, pcat.imag], axis=0))
    qf = c_mat[0][None] * lam_pw[1:t + 1, 0][:, :, None, :]
    qb = c_mat[1][None] * lam_pw[1:t + 1][::-1][:, 1][:, :, None, :]
    qcat = jnp.stack([qf, qb], axis=0)
    q_tab = compact(jnp.stack([qcat.real, -qcat.imag], axis=0))

    a = lam_pw[t]
    a1 = jnp.concatenate([a.real, a.real], axis=-1).reshape(2, g * 2 * ns)
    a2 = jnp.concatenate([-a.imag, a.imag], axis=-1).reshape(2, g * 2 * ns)
    m_op, p_op, q_op = _s5_expand(d_tab, p_tab, q_tab)
    return m_op, p_op, q_op, a1, a2


def _s5_m_kernel(d_ref, o_ref, *, t):
    s = pl.program_id(1)
    for tt in range(t):
        o_ref[0, :, tt * LANE:(tt + 1) * LANE] = d_ref[0, tt - s + t - 1].astype(o_ref.dtype)


def _s5_pq_kernel(c_ref, o_ref, *, transpose):
    gl, nd, gc, w = c_ref.shape[2:]
    rows = [jnp.concatenate([c_ref[0, 0, g, dd] if (g == h) else jnp.zeros((gc, w), F32)
                             for dd in range(nd) for h in range(gl)], axis=1) for g in range(gl)]
    blk = jnp.concatenate(rows, axis=0)
    o_ref[0] = (blk.T if transpose else blk).astype(o_ref.dtype)


def _s5_expand(d_tab, p_tab, q_tab):
    nblk, nlag = d_tab.shape[:2]
    t = (nlag + 1) // 2
    _, _, gl, nd, gc, w = p_tab.shape
    ncol = nd * gl * w
    m_op = pl.pallas_call(
        functools.partial(_s5_m_kernel, t=t), grid=(nblk, t),
        in_specs=[pl.BlockSpec((1, nlag, LANE, LANE), lambda b, s: (b, 0, 0, 0))],
        out_specs=pl.BlockSpec((1, LANE, t * LANE), lambda b, s: (b, s, 0)),
        out_shape=jax.ShapeDtypeStruct((nblk, t * LANE, t * LANE), BF16),
        compiler_params=_cp("parallel", "parallel"), name="s5_m_op",
    )(d_tab)
    tab_spec = pl.BlockSpec((1, 1, gl, nd, gc, w), lambda b, j: (b, j, 0, 0, 0, 0))
    p_op = pl.pallas_call(
        functools.partial(_s5_pq_kernel, transpose=False), grid=(nblk, t),
        in_specs=[tab_spec],
        out_specs=pl.BlockSpec((1, LANE, ncol), lambda b, j: (b, j, 0)),
        out_shape=jax.ShapeDtypeStruct((nblk, t * LANE, ncol), BF16),
        compiler_params=_cp("parallel", "parallel"), name="s5_p_op",
    )(p_tab)
    q_op = pl.pallas_call(
        functools.partial(_s5_pq_kernel, transpose=True), grid=(nblk, t),
        in_specs=[tab_spec],
        out_specs=pl.BlockSpec((1, ncol, LANE), lambda b, j: (b, 0, j)),
        out_shape=jax.ShapeDtypeStruct((nblk, ncol, t * LANE), BF16),
        compiler_params=_cp("parallel", "parallel"), name="s5_q_op",
    )(q_tab)
    return m_op, p_op, q_op


def _s5_sum_kernel(*refs, t):
    u_refs = refs[:t]
    p_ref, o_ref = refs[t:]
    u = jnp.concatenate([r[...] for r in u_refs], axis=1)
    o_ref[...] = _dot(u, p_ref[0])


def _s5_sum_call(h, p_op, rb):
    rows = h.shape[0]
    t = S5_T
    nblk = p_op.shape[0]
    ns2 = p_op.shape[2]
    u_specs = [pl.BlockSpec((rb, LANE), lambda gb, r, s=s: (r, s * nblk + gb)) for s in range(t)]
    return pl.pallas_call(
        functools.partial(_s5_sum_kernel, t=t), grid=(nblk, rows // rb),
        in_specs=u_specs + [pl.BlockSpec((1,) + p_op.shape[1:], lambda gb, r: (gb, 0, 0))],
        out_specs=pl.BlockSpec((rb, ns2), lambda gb, r: (r, gb)),
        out_shape=jax.ShapeDtypeStruct((rows, nblk * ns2), F32),
        compiler_params=_cp("parallel", "parallel"), name="s5_sum",
    )(*([h] * t), p_op)


def _s5_rec_kernel(sf_ref, sb_ref, a1f_ref, a2f_ref, a1b_ref, a2b_ref, hf0_ref, hb0_ref,
                   hf_ref, hb_ref, ff_ref, fb_ref, *, kb):
    a1f, a2f, a1b, a2b = a1f_ref[...], a2f_ref[...], a1b_ref[...], a2b_ref[...]

    @pl.when(pl.program_id(0) == 0)
    def _():
        ff_ref[...] = hf0_ref[...]
        fb_ref[...] = hb0_ref[...]

    def body(i, carry):
        hf, hb = carry
        k = kb - 1 - i
        hf_ref[i] = hf
        hb_ref[k] = hb
        hf = a1f * hf + a2f * pltpu.roll(hf, 64, 1) + sf_ref[i]
        hb = a1b * hb + a2b * pltpu.roll(hb, 64, 1) + sb_ref[k]
        return hf, hb

    hf, hb = lax.fori_loop(0, kb, body, (ff_ref[...], fb_ref[...]))
    ff_ref[...] = hf
    fb_ref[...] = hb


def _s5_rec_call(sf, sb, a1, a2, hf0, hb0):
    nk, r, _ = sf.shape
    rep = r * LANE // a1.shape[1]
    tile = lambda v: jnp.tile(v.reshape(1, -1), (1, rep)).reshape(r, LANE)
    kb = min(32, nk)
    nb = nk // kb
    fwd3 = pl.BlockSpec((kb, r, LANE), lambda i: (i, 0, 0))
    bwd3 = pl.BlockSpec((kb, r, LANE), lambda i: (nb - 1 - i, 0, 0))
    full2 = _const_spec((r, LANE))
    return pl.pallas_call(
        functools.partial(_s5_rec_kernel, kb=kb), grid=(nb,),
        in_specs=[fwd3, bwd3, full2, full2, full2, full2, full2, full2],
        out_specs=[fwd3, bwd3, full2, full2],
        out_shape=[jax.ShapeDtypeStruct((nk, r, LANE), F32)] * 2 + [jax.ShapeDtypeStruct((r, LANE), F32)] * 2,
        compiler_params=_cp("arbitrary"), name="s5_rec",
    )(sf, sb, tile(a1[0]), tile(a2[0]), tile(a1[1]), tile(a2[1]), hf0, hb0)


def _s5_out_kernel(*refs, t):
    u_refs = refs[:t]
    h_ref, m_ref, q_ref, o_ref = refs[t:]
    u = jnp.concatenate([r[...] for r in u_refs], axis=1)
    y = _dot(u, m_ref[0]) + _dot(h_ref[...].astype(BF16), q_ref[0])
    o_ref[0] = (0.5 * y * (1.0 + lax.erf(y * (2.0 ** -0.5)))).astype(o_ref.dtype)


def _s5_out_call(h, hcat, m_op, q_op, rb):
    rows = h.shape[0]
    t = S5_T
    nblk = m_op.shape[0]
    ns2 = q_op.shape[1]
    u_specs = [pl.BlockSpec((rb, LANE), lambda gb, r, s=s: (r, s * nblk + gb)) for s in range(t)]
    return pl.pallas_call(
        functools.partial(_s5_out_kernel, t=t), grid=(nblk, rows // rb),
        in_specs=u_specs + [pl.BlockSpec((rb, ns2), lambda gb, r: (r, gb)),
                            pl.BlockSpec((1,) + m_op.shape[1:], lambda gb, r: (gb, 0, 0)),
                            pl.BlockSpec((1,) + q_op.shape[1:], lambda gb, r: (gb, 0, 0))],
        out_specs=pl.BlockSpec((1, rb, t * LANE), lambda gb, r: (gb, r, 0)),
        out_shape=jax.ShapeDtypeStruct((nblk, rows, t * LANE), BF16),
        compiler_params=_cp("parallel", "parallel"), name="s5_out",
    )(*([h] * t), hcat, m_op, q_op)


def _s5_layer(xl, xc, mods_l, mods_c, pre_g, post_g, lam_re, lam_im, log_dt, b_re, b_im, c_re, c_im,
              d_skip, w_glu, b_glu, b, n, c):
    d = xl.shape[-1]
    t = S5_T
    m_op, p_op, q_op, a1, a2 = _s5_operators(lam_re, lam_im, log_dt, b_re, b_im, c_re, c_im, d_skip)
    nblk = m_op.shape[0]
    half = a1.shape[1]
    hl = _normmod_call(xl, mods_l, pre_g, min(512, n), n).reshape(b * n // t, t * d)
    hc = _normmod_call(xc, mods_c, pre_g, c, c).reshape(b * c // t, t * d)

    def summaries(h, nk):
        s = _s5_sum_call(h, p_op, min(512, h.shape[0]))
        s = s.reshape(b, nk, nblk, 2, half // nblk)
        s = jnp.transpose(s, (3, 1, 0, 2, 4)).reshape(2, nk, b * half // LANE, LANE)
        return s[0], s[1]

    zeros = jnp.zeros((b * half // LANE, LANE), F32)
    sfc, sbc = summaries(hc, c // t)
    _, _, hf0, hb0 = _s5_rec_call(sfc, sbc, a1, a2, zeros, zeros)
    nk = n // t
    sfl, sbl = summaries(hl, nk)
    hf, hb, _, _ = _s5_rec_call(sfl, sbl, a1, a2, hf0, hb0)
    hcat = jnp.stack([hf.reshape(nk, b, nblk, half // nblk), hb.reshape(nk, b, nblk, half // nblk)], axis=3)
    hcat = jnp.transpose(hcat, (1, 0, 2, 3, 4)).reshape(b * nk, nblk * 2 * (half // nblk))
    rb = min(512, b * nk)
    gact = _s5_out_call(hl, hcat, m_op, q_op, rb)
    g_nat = jnp.transpose(gact.reshape(nblk, b * nk, t, LANE), (1, 2, 0, 3)).reshape(b * n, d)
    return _post_call(xl, g_nat, mods_l, w_glu.astype(BF16), b_glu, post_g, min(512, n), n, glu=True)


def _fn_a_kernel(x_ref, mod_ref, g_ref, cs_ref, fm_ref, twc_ref, tws_ref, or_ref, oi_ref, *, nt, d):
    g = g_ref[...]
    shift, scale = mod_ref[0, 0:1, :], mod_ref[0, 1:2, :]
    cs = cs_ref[...]
    fm = fm_ref[...]
    ng = d // FNET_GC
    for s in range(nt):
        h = _normmod(x_ref[0, :, s * d:(s + 1) * d], g, shift, scale).astype(BF16)
        ab = [_dot(h[:, k * FNET_GC:(k + 1) * FNET_GC], cs) for k in range(ng)]
        a = jnp.concatenate([z[:, :FNET_GC] for z in ab], axis=1)
        bb = jnp.concatenate([z[:, FNET_GC:] for z in ab], axis=1)
        x1 = _dot(fm, jnp.concatenate([a, bb], axis=0).astype(BF16))
        n1 = x1.shape[0] // 2
        xr, xi = x1[:n1], x1[n1:]
        c = jnp.concatenate([twc_ref[s]] * (d // LANE), axis=1)
        sn = jnp.concatenate([tws_ref[s]] * (d // LANE), axis=1)
        or_ref[0, s] = (xr * c + xi * sn).astype(or_ref.dtype)
        oi_ref[0, s] = (xi * c - xr * sn).astype(oi_ref.dtype)


def _fn_c_kernel(fm_ref, xr_ref, xi_ref, o_ref):
    o_ref[0] = _dot(fm_ref[...], jnp.concatenate([xr_ref[0], xi_ref[0]], axis=0)).astype(o_ref.dtype)


def _fnet_layer(xl, mods_l, pre_g, post_g, w_o, b_o, b, n, d):
    n2 = FFT_N2
    n1 = n // n2
    gc = FNET_GC
    cc, sc = _dft_cs(gc, gc, gc)
    cs = jnp.asarray(np.concatenate([cc, sc], axis=1) / np.sqrt(gc), BF16)
    c1, s1 = _dft_cs(n1, n1, n1)
    fm_a = jnp.asarray(np.block([[c1, -s1], [-s1, -c1]]) / np.sqrt(n), BF16)
    c2, s2 = _dft_cs(n2, n2, n2)
    fm_c = jnp.asarray(np.concatenate([c2, s2], axis=1), BF16)
    twc, tws = _twiddle_tables(n1, n2, lead_t2=True)
    nt = 4 if n2 % 4 == 0 else 1
    x3 = xl.reshape(b, n1, n2 * d)
    mid = jax.ShapeDtypeStruct((b, n2, n1, d), BF16)
    mid_spec = pl.BlockSpec((1, nt, n1, d), lambda bi, j: (bi, j, 0, 0))
    tw_spec = pl.BlockSpec((nt, n1, LANE), lambda bi, j: (j, 0, 0))
    xr, xi = pl.pallas_call(
        functools.partial(_fn_a_kernel, nt=nt, d=d), grid=(b, n2 // nt),
        in_specs=[pl.BlockSpec((1, n1, nt * d), lambda bi, j: (bi, 0, j)),
                  pl.BlockSpec((1, 8, d), lambda bi, j: (bi, 0, 0)), _const_spec((1, d)),
                  _const_spec(cs.shape), _const_spec(fm_a.shape), tw_spec, tw_spec],
        out_specs=[mid_spec, mid_spec], out_shape=[mid, mid],
        compiler_params=_cp("parallel", "parallel"), name="fn_a",
    )(x3, mods_l, pre_g.reshape(1, d), cs, fm_a, twc, tws)
    cb = 4 * d if n1 % 4 == 0 else d
    col = pl.BlockSpec((1, n2, cb), lambda bi, j: (bi, 0, j))
    y = pl.pallas_call(
        _fn_c_kernel, grid=(b, n1 * d // cb),
        in_specs=[_const_spec(fm_c.shape), col, col],
        out_specs=col, out_shape=jax.ShapeDtypeStruct((b, n2, n1 * d), BF16),
        compiler_params=_cp("parallel", "parallel"), name="fn_c",
    )(fm_c, xr.reshape(b, n2, n1 * d), xi.reshape(b, n2, n1 * d))
    return _post_call(xl, y.reshape(b * n, d), mods_l, w_o.astype(BF16), b_o, post_g, min(512, n), n)


def kernel(x, c, ctx, c_ctx, mod_w, mod_b, mix_pre_g, mix_post_g, ffn_pre_g, ffn_post_g, ffn_w13, ffn_w2,
           mla_w_in, mla_q_norm_g, mla_kv_norm_g, mla_w_uq, mla_w_ukv, mla_w_o,
           hy_w_in, hy_b_in, hy_conv_w, hy_conv_b, hy_f_w1, hy_f_b1, hy_f_freq, hy_f_w2, hy_f_b2, hy_f_w3,
           hy_skip, hy_w_out, hy_b_out,
           s5_lambda_re, s5_lambda_im, s5_log_dt, s5_b_re, s5_b_im, s5_c_re, s5_c_im, s5_d, s5_w_glu, s5_b_glu,
           fn_w_o, fn_b_o):
    b, n, d = x.shape
    cl = ctx.shape[1]
    depth = mod_w.shape[0]
    assert b == 2 and depth == 4, "two batches ride one complex transform; one layer per mixer"
    mods = _mods(c, c_ctx, mod_w, mod_b)
    xl = x.reshape(b * n, d)
    xc = ctx.reshape(b * cl, d)
    tm = min(512, n)

    def ffn(i, xl, xc, with_ctx):
        w13 = ffn_w13[i].astype(BF16)
        w2 = ffn_w2[i].astype(BF16)
        xl = _ffn_call(xl, mods[i, 0:2], ffn_pre_g[i], ffn_post_g[i], w13, w2, tm, n)
        if with_ctx:
            xc = _ffn_call(xc, mods_c(i), ffn_pre_g[i], ffn_post_g[i], w13, w2, cl, cl)
        return xl, xc

    def mods_c(i):
        return jnp.broadcast_to(mods[i, 2:3], (b, 8, d))

    xl, xc = _mla_layer(xl, xc, mods[0, 0:2], mods_c(0), mix_pre_g[0], mix_post_g[0], mla_w_in[0],
                        mla_q_norm_g[0], mla_kv_norm_g[0], mla_w_uq[0], mla_w_ukv[0], mla_w_o[0], b, n, cl)
    xl, xc = ffn(0, xl, xc, True)
    filt = (hy_f_w1[0], hy_f_b1[0], hy_f_freq[0], hy_f_w2[0], hy_f_b2[0], hy_f_w3[0])
    xl, xc = _hyena_layer(xl, xc, mods[1, 0:2], mods_c(1), mix_pre_g[1], mix_post_g[1], hy_w_in[0], hy_b_in[0],
                          hy_conv_w[0], hy_conv_b[0], filt, hy_skip[0], hy_w_out[0], hy_b_out[0], b, n, cl)
    xl, xc = ffn(1, xl, xc, True)
    xl = _s5_layer(xl, xc, mods[2, 0:2], mods_c(2), mix_pre_g[2], mix_post_g[2], s5_lambda_re[0],
                   s5_lambda_im[0], s5_log_dt[0], s5_b_re[0], s5_b_im[0], s5_c_re[0], s5_c_im[0], s5_d[0],
                   s5_w_glu[0], s5_b_glu[0], b, n, cl)
    xl, _ = ffn(2, xl, xc, False)
    xl = _fnet_layer(xl, mods[3, 0:2], mix_pre_g[3], mix_post_g[3], fn_w_o[0], fn_b_o[0], b, n, d)
    xl, _ = ffn(3, xl, xc, False)
    return xl.reshape(b, n, d)
```

```python
import functools
import math

import numpy as np
import jax
import jax.numpy as jnp
from jax import lax
from jax.experimental import pallas as pl
from jax.experimental.pallas import tpu as pltpu

F32 = jnp.float32
BF16 = jnp.bfloat16
NORM_EPS = 1e-6
LANE = 128
VMEM_LIMIT = 56 * 1024 * 1024
HIGHEST = lax.Precision.HIGHEST

GRID_W = 64
ROPE_THETA = 10000.0
MLA_HEADS = 8
MLA_NOPE = 128
MLA_ROPE = 64
MLA_V = 128
MLA_VT = MLA_V + 16
HYENA_BANDS = 16
HYENA_TARGET = 1e-2
HYENA_FAST = 0.3
HYENA_SLOW = 1.5
S5_GROUP = 16
S5_STATE = 64
S5_T = 16
FNET_GC = 128
FFT_N2 = 128


def _cp(*sem):
    return pltpu.CompilerParams(dimension_semantics=sem, vmem_limit_bytes=VMEM_LIMIT)


def _dot(a, b):
    return jnp.dot(a, b, preferred_element_type=F32)


def _dot_hi(a, b):
    return jnp.dot(a, b, preferred_element_type=F32, precision=HIGHEST)


def _rms(x, g):
    ms = jnp.mean(x * x, axis=-1, keepdims=True)
    return x * lax.rsqrt(ms + NORM_EPS) * g


def _normmod(x, g, shift, scale):
    return _rms(x, g) * (1.0 + scale) + shift


def _const_spec(shape):
    nd = len(shape)
    return pl.BlockSpec(shape, lambda *_: (0,) * nd)


def _mods_kernel(st_ref, w_ref, b_ref, o_ref):
    st = st_ref[...]
    st = st * jax.nn.sigmoid(st)
    w = w_ref[0]
    rows = [jnp.sum(st[:, r:r + 1] * w, axis=0, keepdims=True) for r in range(3)]
    rows.append(jnp.zeros((5, w.shape[1]), F32))
    o_ref[0] = jnp.concatenate(rows, axis=0) + b_ref[0]


def _mods(c, c_ctx, mod_w, mod_b):
    depth, d, n6 = mod_w.shape
    st = jnp.zeros((d, 8), F32).at[:, 0:2].set(c.T).at[:, 2].set(c_ctx)
    tn = 1024
    out = pl.pallas_call(
        _mods_kernel,
        grid=(depth, n6 // tn),
        in_specs=[_const_spec((d, 8)),
                  pl.BlockSpec((1, d, tn), lambda i, j: (i, 0, j)),
                  pl.BlockSpec((1, 1, tn), lambda i, j: (i, 0, j))],
        out_specs=pl.BlockSpec((1, 8, tn), lambda i, j: (i, 0, j)),
        out_shape=jax.ShapeDtypeStruct((depth, 8, n6), F32),
        compiler_params=_cp("parallel", "parallel"),
        name="mods",
    )(st, mod_w, mod_b.reshape(depth, 1, n6))
    m = out[:, :3].reshape(depth, 3, n6 // d, d)
    return jnp.pad(m, ((0, 0), (0, 0), (0, 8 - n6 // d), (0, 0)))


def _row_specs(tm, d, tpb):
    x_spec = pl.BlockSpec((tm, d), lambda i: (i, 0))
    mod_spec = pl.BlockSpec((1, 8, d), lambda i: (i // tpb, 0, 0))
    return x_spec, mod_spec


def _normmod_kernel(x_ref, mod_ref, g_ref, o_ref):
    h = _normmod(x_ref[...], g_ref[...], mod_ref[0, 0:1, :], mod_ref[0, 1:2, :])
    o_ref[...] = h.astype(o_ref.dtype)


def _normmod_call(x, mods, g, tm, rows_per_batch):
    m, d = x.shape
    x_spec, mod_spec = _row_specs(tm, d, rows_per_batch // tm)
    return pl.pallas_call(
        _normmod_kernel, grid=(m // tm,),
        in_specs=[x_spec, mod_spec, _const_spec((1, d))],
        out_specs=x_spec, out_shape=jax.ShapeDtypeStruct((m, d), BF16),
        compiler_params=_cp("parallel"), name="normmod",
    )(x, mods, g.reshape(1, d))


def _mix_ffn_kernel(x_ref, y_ref, mod_ref, wm_ref, bm_ref, gm_ref, pre_ref, post_ref, w13_ref, w2_ref, o_ref,
                    *, f, fc, glu):
    z = _dot(y_ref[...].astype(BF16), wm_ref[...]) + bm_ref[...]
    if glu:
        d = o_ref.shape[-1]
        z = z[:, :d] * jax.nn.sigmoid(z[:, d:])
    x = x_ref[...] + mod_ref[0, 2:3, :] * _rms(z, gm_ref[...])
    h = _normmod(x, pre_ref[...], mod_ref[0, 3:4, :], mod_ref[0, 4:5, :]).astype(BF16)
    acc = None
    for c in range(f // fc):
        a = _dot(h, w13_ref[:, c * fc:(c + 1) * fc])
        b = _dot(h, w13_ref[:, f + c * fc:f + (c + 1) * fc])
        gact = (a * jax.nn.sigmoid(a) * b).astype(BF16)
        part = _dot(gact, w2_ref[c * fc:(c + 1) * fc, :])
        acc = part if acc is None else acc + part
    o_ref[...] = x + mod_ref[0, 5:6, :] * _rms(acc, post_ref[...])


def _mix_ffn_call(x, y, mods, wm, bm, mix_post_g, ffn_pre_g, ffn_post_g, w13, w2, tm, rows_per_batch, glu=False):
    m, d = x.shape
    k, nm = wm.shape
    f = w2.shape[0]
    fc = f // 2 if (f // 2) % LANE == 0 else f
    x_spec, mod_spec = _row_specs(tm, d, rows_per_batch // tm)
    once = lambda shape: pl.BlockSpec(shape, lambda i: (0, 0), pipeline_mode=pl.Buffered(1))
    return pl.pallas_call(
        functools.partial(_mix_ffn_kernel, f=f, fc=fc, glu=glu), grid=(m // tm,),
        in_specs=[x_spec, pl.BlockSpec((tm, k), lambda i: (i, 0)), mod_spec,
                  once((k, nm)), _const_spec((1, nm)), _const_spec((1, d)), _const_spec((1, d)),
                  _const_spec((1, d)), once((d, 2 * f)), once((f, d))],
        out_specs=x_spec, out_shape=jax.ShapeDtypeStruct((m, d), F32),
        compiler_params=_cp("parallel"), name="mix_ffn",
    )(x, y, mods, wm, bm.reshape(1, nm), mix_post_g.reshape(1, d), ffn_pre_g.reshape(1, d),
      ffn_post_g.reshape(1, d), w13, w2)


def _mla_in_kernel(x_ref, mod_ref, g_ref, w_ref, qg_ref, kvg_ref, t1_ref, t2_ref,
                   qn_ref, cn_ref, kr_ref, *, ql, kvl):
    h = _normmod(x_ref[...], g_ref[...], mod_ref[0, 0:1, :], mod_ref[0, 1:2, :])
    z = _dot(h.astype(BF16), w_ref[...])
    qn_ref[...] = _rms(z[:, :ql], qg_ref[...]).astype(BF16)
    cn_ref[...] = _rms(z[:, ql:ql + kvl], kvg_ref[...]).astype(BF16)
    pair = z[:, ql + kvl:]
    kr_ref[...] = (pair * t1_ref[...] + pltpu.roll(pair, 64, 1) * t2_ref[...]).astype(BF16)


def _mla_q_kernel(qn_ref, w_ref, t1_ref, t2_ref, q_ref, *, qscale):
    z = _dot(qn_ref[...], w_ref[...])
    t1 = t1_ref[...]
    t2 = t2_ref[...]
    for h in range(MLA_HEADS):
        base = h * 256
        pair = z[:, base + 128:base + 256]
        rp = pair * t1 + pltpu.roll(pair, 64, 1) * t2
        qcat = jnp.concatenate([z[:, base:base + 128], rp], axis=1) * qscale
        q_ref[0, h] = qcat.T.astype(BF16)


def _mla_kv_kernel(cn_ref, kr_ref, w_ref, k_ref, vt_ref):
    z = _dot(cn_ref[...], w_ref[...])
    kr = kr_ref[...]
    tk = kr.shape[0]
    ones_blk = (lax.broadcasted_iota(jnp.int32, (MLA_VT - MLA_V, tk), 0) == 0).astype(BF16)
    for h in range(MLA_HEADS):
        base = h * 256
        k_ref[0, h, 0, :, 0:128] = z[:, base:base + 128].astype(BF16)
        k_ref[0, h, 0, :, 128:256] = kr
        vt_ref[0, h, 0, 0:MLA_V, :] = z[:, base + 128:base + 256].T.astype(BF16)
        vt_ref[0, h, 0, MLA_V:MLA_VT, :] = ones_blk


def _flash_kernel(q_ref, kc_ref, vc_ref, *rest, n_lat):
    if n_lat:
        kl_ref, vl_ref, o_ref, s_scr, acc_scr = rest
    else:
        o_ref, acc_scr = rest
    qt = q_ref[0, 0]

    def qk(k, slot):
        s = _dot(k, qt)
        s_scr[slot] = s
        return jnp.max(s, axis=0, keepdims=True)

    def sm_pv(slot, vt, m, mx):
        m_new = jnp.maximum(m, mx)
        alpha = jnp.exp2(m - m_new)
        p = jnp.exp2(s_scr[slot] - m_new).astype(BF16)
        acc_scr[...] = alpha * acc_scr[...] + _dot(vt, p)
        return m_new

    sc = _dot(kc_ref[0, 0, 0], qt)
    m = jnp.max(sc, axis=0, keepdims=True)
    acc_scr[...] = _dot(vc_ref[0, 0, 0], jnp.exp2(sc - m).astype(BF16))
    if n_lat:
        mx = qk(kl_ref[0, 0, 0], 0)

        def body(i, carry):
            m, mx0 = carry
            c = 2 * i
            mx1 = qk(kl_ref[0, 0, c + 1], 1)
            m = sm_pv(0, vl_ref[0, 0, c], m, mx0)
            mx0 = qk(kl_ref[0, 0, jnp.minimum(c + 2, n_lat - 1)], 0)
            m = sm_pv(1, vl_ref[0, 0, c + 1], m, mx1)
            return m, mx0

        lax.fori_loop(0, n_lat // 2, body, (m, mx))
    acc = acc_scr[...]
    o_ref[0] = (acc[0:MLA_V] / acc[MLA_V:MLA_V + 1]).T.astype(o_ref.dtype)


def _rope_tables(n_lat):
    rows = n_lat // GRID_W
    row = jnp.repeat(jnp.arange(rows, dtype=F32), GRID_W)
    col = jnp.tile(jnp.arange(GRID_W, dtype=F32), rows)
    axis_dim = MLA_ROPE // 2
    inv_freq = 1.0 / (ROPE_THETA ** (jnp.arange(0, axis_dim, 2, dtype=F32) / axis_dim))
    ang_r = row[:, None] * inv_freq
    ang_c = col[:, None] * inv_freq
    cr, sr, cc, sc = jnp.cos(ang_r), jnp.sin(ang_r), jnp.cos(ang_c), jnp.sin(ang_c)
    cp = jnp.concatenate([cr, cr, cc, cc], axis=-1)
    sp = jnp.concatenate([-sr, sr, -sc, sc], axis=-1)
    return cp, sp


_ROPE_SWAP = np.concatenate([np.arange(16, 32), np.arange(0, 16), np.arange(48, 64), np.arange(32, 48)])


def _mla_side(x, mods, pre_g, w_in_ext, q_g, kv_g, w_uq_ext, w_ukv, tabs, b, n, tm, tk, want_q):
    m, d = x.shape
    ql, kvl = q_g.shape[-1], kv_g.shape[-1]
    t1k, t2k, t1q, t2q = tabs
    x_spec, mod_spec = _row_specs(tm, d, n // tm)
    row = lambda w: pl.BlockSpec((tm, w), lambda i: (i, 0))
    tab_spec = pl.BlockSpec((tm, 128), lambda i: (i % (n // tm), 0))
    qn, cn, kr = pl.pallas_call(
        functools.partial(_mla_in_kernel, ql=ql, kvl=kvl), grid=(m // tm,),
        in_specs=[x_spec, mod_spec, _const_spec((1, d)), _const_spec(w_in_ext.shape),
                  _const_spec((1, ql)), _const_spec((1, kvl)), tab_spec, tab_spec],
        out_specs=[row(ql), row(kvl), row(128)],
        out_shape=[jax.ShapeDtypeStruct((m, ql), BF16), jax.ShapeDtypeStruct((m, kvl), BF16),
                   jax.ShapeDtypeStruct((m, 128), BF16)],
        compiler_params=_cp("parallel"), name="mla_in",
    )(x, mods, pre_g.reshape(1, d), w_in_ext, q_g.reshape(1, ql), kv_g.reshape(1, kvl), t1k, t2k)

    nc = n // tk
    kt, v = pl.pallas_call(
        _mla_kv_kernel, grid=(b, nc),
        in_specs=[pl.BlockSpec((tk, kvl), lambda bi, c: (bi * nc + c, 0)),
                  pl.BlockSpec((tk, 128), lambda bi, c: (bi * nc + c, 0)),
                  _const_spec(w_ukv.shape)],
        out_specs=[pl.BlockSpec((1, MLA_HEADS, 1, tk, 256), lambda bi, c: (bi, 0, c, 0, 0)),
                   pl.BlockSpec((1, MLA_HEADS, 1, MLA_VT, tk), lambda bi, c: (bi, 0, c, 0, 0))],
        out_shape=[jax.ShapeDtypeStruct((b, MLA_HEADS, nc, tk, 256), BF16),
                   jax.ShapeDtypeStruct((b, MLA_HEADS, nc, MLA_VT, tk), BF16)],
        compiler_params=_cp("parallel", "parallel"), name="mla_kv",
    )(cn, kr, w_ukv)

    q = None
    if want_q:
        qscale = (MLA_NOPE + MLA_ROPE) ** -0.5 * math.log2(math.e)
        tpb = n // tm
        q = pl.pallas_call(
            functools.partial(_mla_q_kernel, qscale=qscale), grid=(m // tm,),
            in_specs=[row(ql), _const_spec(w_uq_ext.shape), tab_spec, tab_spec],
            out_specs=pl.BlockSpec((1, MLA_HEADS, 256, tm), lambda i: (i // tpb, 0, 0, i % tpb)),
            out_shape=jax.ShapeDtypeStruct((b, MLA_HEADS, 256, n), BF16),
            compiler_params=_cp("parallel"), name="mla_q",
        )(qn, w_uq_ext, t1q, t2q)
    return q, kt, v


def _flash_call(qt, kc, vtc, kl, vtl, tq):
    b, hh, _, n = qt.shape
    c = kc.shape[-2]
    n_lat = 0 if kl is None else kl.shape[2]
    in_specs = [pl.BlockSpec((1, 1, 256, tq), lambda bi, h, i: (bi, h, 0, i)),
                pl.BlockSpec((1, 1, 1, c, 256), lambda bi, h, i: (bi, h, 0, 0, 0)),
                pl.BlockSpec((1, 1, 1, MLA_VT, c), lambda bi, h, i: (bi, h, 0, 0, 0))]
    args = [qt, kc, vtc]
    scratch = [pltpu.VMEM((MLA_VT, tq), F32)]
    if n_lat:
        assert n_lat % 2 == 0, "latent key chunks are consumed in pairs"
        tk = kl.shape[-2]
        in_specs += [pl.BlockSpec((1, 1, n_lat, tk, 256), lambda bi, h, i: (bi, h, 0, 0, 0)),
                     pl.BlockSpec((1, 1, n_lat, MLA_VT, tk), lambda bi, h, i: (bi, h, 0, 0, 0))]
        args += [kl, vtl]
        scratch = [pltpu.VMEM((2, tk, tq), F32)] + scratch
    return pl.pallas_call(
        functools.partial(_flash_kernel, n_lat=n_lat), grid=(b, hh, n // tq),
        in_specs=in_specs,
        out_specs=pl.BlockSpec((1, tq, MLA_V), lambda bi, h, i: (bi, i, h)),
        out_shape=jax.ShapeDtypeStruct((b, n, hh * MLA_V), BF16),
        scratch_shapes=scratch,
        compiler_params=_cp("parallel", "parallel", "arbitrary"), name="flash",
    )(*args)


def _mla_layer(xl, xc, mods_l, mods_c, pre_g, fin, w_in, q_g, kv_g, w_uq, w_ukv, w_o, b, n, c):
    d = xl.shape[-1]
    ql, kvl = q_g.shape[-1], kv_g.shape[-1]
    hh = MLA_HEADS
    rope_cols = w_in[:, ql + kvl:]
    w_in_ext = jnp.concatenate([w_in, rope_cols[:, _ROPE_SWAP]], axis=1).astype(BF16)
    wq = w_uq.reshape(ql, hh, MLA_NOPE + MLA_ROPE)
    w_uq_ext = jnp.concatenate([wq, wq[:, :, MLA_NOPE:][:, :, _ROPE_SWAP]], axis=-1)
    w_uq_ext = w_uq_ext.reshape(ql, hh * 256).astype(BF16)
    w_ukv_b = w_ukv.astype(BF16)
    w_o_b = w_o.astype(BF16)

    cp, sp = _rope_tables(n)
    z64l, o64l = jnp.zeros((n, 64), F32), jnp.ones((n, 64), F32)
    z64c, o64c = jnp.zeros((c, 64), F32), jnp.ones((c, 64), F32)
    cat = lambda a, bb: jnp.concatenate([a, bb], axis=1)
    tabs_l = (cat(cp, z64l), cat(sp, z64l), cat(cp, z64l), cat(sp, o64l))
    tabs_c = (cat(z64c, z64c), cat(z64c, o64c), cat(z64c, z64c), cat(z64c, o64c))

    tm_l = min(512, n)
    tk_l = min(512, n // 2)
    ql_, kl, vtl = _mla_side(xl, mods_l, pre_g, w_in_ext, q_g, kv_g, w_uq_ext, w_ukv_b, tabs_l,
                             b, n, tm_l, tk_l, True)
    qc_, kc, vtc = _mla_side(xc, mods_c, pre_g, w_in_ext, q_g, kv_g, w_uq_ext, w_ukv_b, tabs_c,
                             b, c, c, c, True)
    o_lat = _flash_call(ql_, kc, vtc, kl, vtl, min(1024, n)).reshape(b * n, hh * MLA_V)
    o_ctx = _flash_call(qc_, kc, vtc, None, None, c).reshape(b * c, hh * MLA_V)
    zb = jnp.zeros((d,), F32)
    xl = fin(xl, o_lat, mods_l, w_o_b, zb, tm_l, n)
    xc = fin(xc, o_ctx, mods_c, w_o_b, zb, c, c)
    return xl, xc


def _hy_in_kernel(x_ref, xp_ref, xn_ref, mod_ref, g_ref, w_ref, b_ref, cw_ref, cb_ref,
                  g0_ref, vg_ref, *, tpb):
    i = pl.program_id(0)
    g = g_ref[...]
    shift, scale = mod_ref[0, 0:1, :], mod_ref[0, 1:2, :]
    xcat = jnp.concatenate([xp_ref[...], x_ref[...], xn_ref[...]], axis=0)
    ucat = _dot(_normmod(xcat, g, shift, scale).astype(BF16), w_ref[...]) + b_ref[...]
    tm = x_ref.shape[0]
    u = ucat[8:tm + 8]
    first = (i % tpb) == 0
    last = (i % tpb) == tpb - 1
    prev_row = jnp.where(first, 0.0, ucat[7:8, :])
    next_row = jnp.where(last, 0.0, ucat[tm + 8:tm + 9, :])
    ridx = lax.broadcasted_iota(jnp.int32, (tm, 1), 0)
    dn = jnp.where(ridx == 0, prev_row, pltpu.roll(u, 1, 0))
    upw = jnp.where(ridx == tm - 1, next_row, pltpu.roll(u, tm - 1, 0))
    conv = cb_ref[...] + dn * cw_ref[0:1, :] + u * cw_ref[1:2, :] + upw * cw_ref[2:3, :]
    d = g0_ref.shape[-1]
    g0_ref[...] = conv[:, :d]
    vg_ref[...] = conv[:, 2 * d:] * conv[:, d:2 * d]


def _hy_in_call(x, mods, pre_g, w_in, b_in, conv_w, conv_b, tm, n):
    m, d = x.shape
    p = w_in.shape[1]
    tpb = n // tm
    x_spec, mod_spec = _row_specs(tm, d, tpb)
    r8 = tm // 8
    nb8 = m // 8
    prev_spec = pl.BlockSpec((8, d), lambda i: (jnp.maximum(i * r8 - 1, 0), 0))
    next_spec = pl.BlockSpec((8, d), lambda i: (jnp.minimum((i + 1) * r8, nb8 - 1), 0))
    cw = jnp.pad(conv_w, ((0, 8 - conv_w.shape[0]), (0, 0)))
    return pl.pallas_call(
        functools.partial(_hy_in_kernel, tpb=tpb), grid=(m // tm,),
        in_specs=[x_spec, prev_spec, next_spec, mod_spec, _const_spec((1, d)), _const_spec((d, p)),
                  _const_spec((1, p)), _const_spec((8, p)), _const_spec((1, p))],
        out_specs=[x_spec, x_spec],
        out_shape=[jax.ShapeDtypeStruct((m, d), F32), jax.ShapeDtypeStruct((m, d), F32)],
        compiler_params=_cp("parallel"), name="hy_in",
    )(x, x, x, mods, pre_g.reshape(1, d), w_in, b_in.reshape(1, p), cw, conv_b.reshape(1, p))


def _hy_filter_kernel(bands_ref, w1_ref, b1_ref, fq_ref, w2_ref, b2_ref, w3_ref, dl_ref,
                      kf_ref, gb_ref, nrm_ref, *, n, tr):
    i = pl.program_id(0)
    j = (lax.broadcasted_iota(jnp.int32, (tr, LANE), 0) + i * tr).astype(F32)
    lane = lax.broadcasted_iota(jnp.int32, (tr, LANE), 1)
    t = j * (1.0 / (n - 1))
    arg = (2.0 * math.pi / n) * j * bands_ref[...]
    z = jnp.where(lane == 0, t,
                  jnp.where(lane <= HYENA_BANDS, jnp.cos(arg),
                            jnp.where(lane <= 2 * HYENA_BANDS, -jnp.sin(arg), 0.0)))
    fq = fq_ref[...]
    a = jnp.sin(fq * (_dot_hi(z, w1_ref[...]) + b1_ref[...]))
    for k in range(w2_ref.shape[0]):
        a = jnp.sin(fq * (_dot_hi(a, w2_ref[k]) + b2_ref[k]))
    h = _dot_hi(a, w3_ref[...])
    d = kf_ref.shape[-1]
    decay = jnp.exp(-t[:, 0:1] * dl_ref[...])
    kf = h[:, :d] * decay
    gb = jnp.where(j[:, 0:1] == 0.0, 0.0, h[:, d:] * decay)
    kf_ref[...] = kf
    gb_ref[...] = gb
    part = jnp.sum(jnp.abs(kf) + jnp.abs(gb), axis=0, keepdims=True)

    @pl.when(i == 0)
    def _():
        nrm_ref[...] = jnp.zeros_like(nrm_ref)

    nrm_ref[...] += jnp.broadcast_to(part, nrm_ref.shape)


def _hy_filter_call(n, d, f_w1, f_b1, f_freq, f_w2, f_b2, f_w3):
    fw = f_w1.shape[1]
    tr = min(512, n)
    bands_np = np.zeros((1, LANE), np.float32)
    bands_np[0, 1:1 + HYENA_BANDS] = np.linspace(1e-4, HYENA_BANDS - 1, HYENA_BANDS, dtype=np.float32)
    bands_np[0, 1 + HYENA_BANDS:1 + 2 * HYENA_BANDS] = bands_np[0, 1:1 + HYENA_BANDS]
    w1p = jnp.zeros((LANE, fw), F32).at[:f_w1.shape[0]].set(f_w1)
    deltas = jnp.abs(jnp.linspace(math.log(HYENA_TARGET) / HYENA_SLOW, math.log(HYENA_TARGET) / HYENA_FAST,
                                  d, dtype=F32)).reshape(1, d)
    row = pl.BlockSpec((tr, d), lambda i: (i, 0))
    return pl.pallas_call(
        functools.partial(_hy_filter_kernel, n=n, tr=tr), grid=(n // tr,),
        in_specs=[_const_spec((1, LANE)), _const_spec((LANE, fw)), _const_spec((1, fw)), _const_spec((1, fw)),
                  _const_spec(f_w2.shape), _const_spec((f_w2.shape[0], 1, fw)), _const_spec(f_w3.shape),
                  _const_spec((1, d))],
        out_specs=[row, row, _const_spec((8, d))],
        out_shape=[jax.ShapeDtypeStruct((n, d), F32), jax.ShapeDtypeStruct((n, d), F32),
                   jax.ShapeDtypeStruct((8, d), F32)],
        compiler_params=_cp("arbitrary"), name="hy_filter",
    )(jnp.asarray(bands_np), w1p, f_b1.reshape(1, fw), f_freq.reshape(1, fw), f_w2,
      f_b2.reshape(f_w2.shape[0], 1, fw), f_w3, deltas)


def _dft_cs(nf, nt, period):
    ft = (np.arange(nf)[:, None] * np.arange(nt)[None, :]) % period
    ang = 2.0 * np.pi * ft / period
    return np.cos(ang), np.sin(ang)


def _twiddle_tables(n1, n2, lead_t2):
    nn = n1 * n2
    f1 = jnp.arange(n1, dtype=jnp.int32)
    t2 = jnp.arange(n2, dtype=jnp.int32)
    idx = (t2[:, None] * f1[None, :]) % nn if lead_t2 else (f1[:, None] * t2[None, :]) % nn
    ang = idx.astype(F32) * (2.0 * math.pi / nn)
    shape = idx.shape + (LANE,)
    return (jnp.broadcast_to(jnp.cos(ang)[..., None], shape),
            jnp.broadcast_to(jnp.sin(ang)[..., None], shape))


def _fft_a_kernel(*refs, n_in, nt, d):
    fm_ref = refs[0]
    in_refs = refs[1:1 + n_in]
    twc_ref, tws_ref, or_ref, oi_ref = refs[1 + n_in:]
    xs = [r[0] for r in in_refs]
    x = (jnp.concatenate(xs, axis=0) if n_in > 1 else xs[0]).astype(BF16)
    out = _dot(fm_ref[...], x)
    n1 = out.shape[0] // 2
    xr, xi = out[:n1], out[n1:]
    for s in range(nt):
        c = jnp.concatenate([twc_ref[s]] * (d // LANE), axis=1)
        sn = jnp.concatenate([tws_ref[s]] * (d // LANE), axis=1)
        a = xr[:, s * d:(s + 1) * d]
        bb = xi[:, s * d:(s + 1) * d]
        or_ref[:, s * d:(s + 1) * d] = (a * c + bb * sn).astype(or_ref.dtype)
        oi_ref[:, s * d:(s + 1) * d] = (bb * c - a * sn).astype(oi_ref.dtype)


def _fft_a_call(fm, ins, twc, tws, n1, n2, d, nt):
    rows = ins[0][0].shape[1]
    in_specs = [_const_spec(fm.shape)]
    args = [fm]
    for arr, bi in ins:
        in_specs.append(pl.BlockSpec((1, rows, nt * d), lambda j, bi=bi: (bi, 0, j)))
        args.append(arr)
    tw_spec = pl.BlockSpec((nt, n1, LANE), lambda j: (j, 0, 0))
    out_spec = pl.BlockSpec((n1, nt * d), lambda j: (0, j))
    return pl.pallas_call(
        functools.partial(_fft_a_kernel, n_in=len(ins), nt=nt, d=d), grid=(n2 // nt,),
        in_specs=in_specs + [tw_spec, tw_spec],
        out_specs=[out_spec, out_spec],
        out_shape=[jax.ShapeDtypeStruct((n1, n2 * d), BF16)] * 2,
        compiler_params=_cp("parallel"), name="fft_a",
    )(*args, twc, tws)


def _hy_kb_kernel(fm_ref, fr_ref, fi_ref, gr_ref, gi_ref, sc_ref, kr_ref, ki_ref, *, nf):
    fm = fm_ref[...]
    n2 = fr_ref.shape[1]
    sc = sc_ref[...]
    for s in range(nf):
        kf = _dot(fm, jnp.concatenate([fr_ref[s], fi_ref[s]], axis=0))
        gb = _dot(fm, jnp.concatenate([gr_ref[s], gi_ref[s]], axis=0))
        kr_ref[s] = (kf[:n2] + gb[:n2]) * sc
        ki_ref[s] = (kf[n2:] - gb[n2:]) * sc


def _hy_b_kernel(fm_ref, fmc_ref, xr_ref, xi_ref, kr_ref, ki_ref, twc_ref, tws_ref,
                 or_ref, oi_ref, *, nf, d):
    fm = fm_ref[...]
    fmc = fmc_ref[...]
    n2 = xr_ref.shape[1]
    for s in range(nf):
        x = _dot(fm, jnp.concatenate([xr_ref[s], xi_ref[s]], axis=0))
        xr, xi = x[:n2], x[n2:]
        kr, ki = kr_ref[s], ki_ref[s]
        yr = (xr * kr - xi * ki).astype(BF16)
        yi = (xr * ki + xi * kr).astype(BF16)
        g = _dot(fmc, jnp.concatenate([yr, yi], axis=0))
        gr, gi = g[:n2], g[n2:]
        c = jnp.concatenate([twc_ref[s]] * (d // LANE), axis=1)
        sn = jnp.concatenate([tws_ref[s]] * (d // LANE), axis=1)
        or_ref[s] = (gr * c - gi * sn).astype(or_ref.dtype)
        oi_ref[s] = (gi * c + gr * sn).astype(oi_ref.dtype)


def _hy_c_kernel(fm_ref, gr_ref, gi_ref, vg_ref, g0_ref, skip_ref, o_ref):
    y = _dot(fm_ref[...], jnp.concatenate([gr_ref[...], gi_ref[...]], axis=0))
    half = y.shape[0] // 2
    skip = skip_ref[...]
    for bi in range(2):
        yb = y[bi * half:(bi + 1) * half]
        o_ref[bi] = ((yb + vg_ref[bi] * skip) * g0_ref[bi]).astype(o_ref.dtype)


def _block_c(cs, sn, sign):
    return np.block([[cs, -sign * sn], [sign * sn, cs]])


def _hy_conv_long(vg, g0, kf, gb, nrm, skip, b, n, d):
    n2 = FFT_N2
    nn = 2 * n
    n1 = nn // n2
    rows = n // n2
    cs1, sn1 = _dft_cs(n1, rows, n1)
    fm_a = jnp.asarray(_block_c(cs1, sn1, -1.0), BF16)
    fm_a_real = jnp.asarray(np.concatenate([cs1, -sn1], axis=0), BF16)
    cs2, sn2 = _dft_cs(n2, n2, n2)
    fm_b = jnp.asarray(_block_c(cs2, sn2, -1.0), BF16)
    fm_bc = jnp.asarray(_block_c(cs2, sn2, 1.0), BF16)
    cs1i, sn1i = _dft_cs(rows, n1, n1)
    fm_c = jnp.asarray(_block_c(cs1i, sn1i, 1.0), BF16)
    twc_a, tws_a = _twiddle_tables(n1, n2, lead_t2=True)
    twc_b, tws_b = _twiddle_tables(n1, n2, lead_t2=False)

    nt = 2 if n2 % 2 == 0 else 1
    nf = 4 if n1 % 4 == 0 else 1
    kf3 = kf.reshape(1, rows, n2 * d)
    gb3 = gb.reshape(1, rows, n2 * d)
    kfr, kfi = _fft_a_call(fm_a_real, [(kf3, 0)], twc_a, tws_a, n1, n2, d, nt)
    gbr, gbi = _fft_a_call(fm_a_real, [(gb3, 0)], twc_a, tws_a, n1, n2, d, nt)
    scale = (1.0 / (nrm[0:1, :] * nn))
    slab = pl.BlockSpec((nf, n2, d), lambda j: (j, 0, 0))
    shp3 = (n1, n2, d)
    khr, khi = pl.pallas_call(
        functools.partial(_hy_kb_kernel, nf=nf), grid=(n1 // nf,),
        in_specs=[_const_spec(fm_b.shape), slab, slab, slab, slab, _const_spec((1, d))],
        out_specs=[slab, slab],
        out_shape=[jax.ShapeDtypeStruct(shp3, F32)] * 2,
        compiler_params=_cp("parallel"), name="hy_kb",
    )(fm_b, kfr.reshape(shp3), kfi.reshape(shp3), gbr.reshape(shp3), gbi.reshape(shp3), scale)

    vg3 = vg.reshape(b, rows, n2 * d)
    g03 = g0.reshape(b, rows, n2 * d)
    x1r, x1i = _fft_a_call(fm_a, [(vg3, 0), (vg3, 1)], twc_a, tws_a, n1, n2, d, nt)
    tw_slab = pl.BlockSpec((nf, n2, LANE), lambda j: (j, 0, 0))
    g1r, g1i = pl.pallas_call(
        functools.partial(_hy_b_kernel, nf=nf, d=d), grid=(n1 // nf,),
        in_specs=[_const_spec(fm_b.shape), _const_spec(fm_bc.shape), slab, slab, slab, slab,
                  tw_slab, tw_slab],
        out_specs=[slab, slab],
        out_shape=[jax.ShapeDtypeStruct(shp3, BF16)] * 2,
        compiler_params=_cp("parallel"), name="hy_b",
    )(fm_b, fm_bc, x1r.reshape(shp3), x1i.reshape(shp3), khr, khi, twc_b, tws_b)

    cb = nt * d
    col = pl.BlockSpec((n1, cb), lambda j: (0, j))
    col3 = pl.BlockSpec((b, rows, cb), lambda j: (0, 0, j))
    out = pl.pallas_call(
        _hy_c_kernel, grid=(n2 * d // cb,),
        in_specs=[_const_spec(fm_c.shape), col, col, col3, col3, _const_spec((1, cb))],
        out_specs=col3,
        out_shape=jax.ShapeDtypeStruct((b, rows, n2 * d), BF16),
        compiler_params=_cp("parallel"), name="hy_c",
    )(fm_c, g1r.reshape(n1, n2 * d), g1i.reshape(n1, n2 * d), vg3, g03, jnp.tile(skip.reshape(1, d), (1, nt)))
    return out.reshape(b * n, d)


def _hy_short_kernel(fa_ref, fk_ref, fi_ref, vg_ref, g0_ref, kf_ref, gb_ref, nrm_ref, skip_ref, o_ref, *, n):
    z = jnp.concatenate([vg_ref[0], vg_ref[1]], axis=0)
    x = _dot_hi(fa_ref[...], z)
    kk = _dot_hi(fk_ref[...], jnp.concatenate([kf_ref[...], gb_ref[...]], axis=0))
    nn = 2 * n
    sc = 1.0 / (nrm_ref[0:1, :] * nn)
    xr, xi = x[:nn], x[nn:]
    kr, ki = kk[:nn] * sc, kk[nn:] * sc
    y = _dot_hi(fi_ref[...], jnp.concatenate([xr * kr - xi * ki, xr * ki + xi * kr], axis=0))
    skip = skip_ref[...]
    for bi in range(2):
        o_ref[bi] = ((y[bi * n:(bi + 1) * n] + vg_ref[bi] * skip) * g0_ref[bi]).astype(o_ref.dtype)


def _hy_conv_short(vg, g0, kf, gb, nrm, skip, b, n, d):
    nn = 2 * n
    cs, sn = _dft_cs(nn, n, nn)
    fa = jnp.asarray(_block_c(cs, sn, -1.0), F32)
    fk = jnp.asarray(np.block([[cs, cs], [-sn, sn]]), F32)
    csi, sni = _dft_cs(n, nn, nn)
    fi = jnp.asarray(_block_c(csi, sni, 1.0), F32)
    cb = 256
    col3 = pl.BlockSpec((b, n, cb), lambda j: (0, 0, j))
    col = pl.BlockSpec((n, cb), lambda j: (0, j))
    vec = pl.BlockSpec((1, cb), lambda j: (0, j))
    out = pl.pallas_call(
        functools.partial(_hy_short_kernel, n=n), grid=(d // cb,),
        in_specs=[_const_spec(fa.shape), _const_spec(fk.shape), _const_spec(fi.shape), col3, col3, col, col,
                  pl.BlockSpec((8, cb), lambda j: (0, j)), vec],
        out_specs=col3, out_shape=jax.ShapeDtypeStruct((b, n, d), BF16),
        compiler_params=_cp("parallel"), name="hy_short",
    )(fa, fk, fi, vg.reshape(b, n, d), g0.reshape(b, n, d), kf, gb, nrm, skip.reshape(1, d))
    return out.reshape(b * n, d)


def _hyena_layer(xl, xc, mods_l, mods_c, pre_g, fin, w_in, b_in, conv_w, conv_b, filt, skip,
                 w_out, b_out, b, n, c):
    d = xl.shape[-1]
    w_in_b = w_in.astype(BF16)
    w_out_b = w_out.astype(BF16)
    tm = min(512, n)
    g0, vg = _hy_in_call(xl, mods_l, pre_g, w_in_b, b_in, conv_w, conv_b, tm, n)
    kf, gb, nrm = _hy_filter_call(n, d, *filt)
    u_out = _hy_conv_long(vg, g0, kf, gb, nrm, skip[0], b, n, d)
    xl = fin(xl, u_out, mods_l, w_out_b, b_out, tm, n)

    g0c, vgc = _hy_in_call(xc, mods_c, pre_g, w_in_b, b_in, conv_w, conv_b, c, c)
    kfc, gbc, nrmc = _hy_filter_call(c, d, *filt)
    u_out_c = _hy_conv_short(vgc, g0c, kfc, gbc, nrmc, skip[0], b, c, d)
    xc = fin(xc, u_out_c, mods_c, w_out_b, b_out, c, c)
    return xl, xc


def _s5_operators(lam_re, lam_im, log_dt, b_re, b_im, c_re, c_im, d_skip):
    t = S5_T
    g, ns = lam_re.shape[1], lam_re.shape[2]
    gc = b_re.shape[-1]
    gl = LANE // gc
    nblk = g // gl
    lam = lax.complex(lam_re, lam_im)
    dt = jnp.exp(log_dt)[..., None]
    lam_bar = jnp.exp(lam * dt)
    b_bar = ((lam_bar - 1.0) / lam)[..., None] * lax.complex(b_re, b_im)
    c_mat = lax.complex(c_re, c_im)
    pw = jnp.arange(t + 1, dtype=F32)
    lam_pw = jnp.exp((lam * dt)[None] * pw[:, None, None, None])
    hp = HIGHEST
    kern = jnp.einsum('dgcn,tdgn,dgne->dgtce', c_mat, lam_pw[:t], b_bar, precision=hp).real
    dsk = d_skip.reshape(g, gc)
    kt = jnp.swapaxes(kern, -1, -2)
    centre = kt[0][:, 0] + kt[1][:, 0] + jnp.eye(gc, dtype=F32)[None] * dsk[:, :, None]
    ks = jnp.concatenate([kt[1][:, 1:][:, ::-1], centre[:, None], kt[0][:, 1:]], axis=1)
    eye_gl = jnp.eye(gl, dtype=F32)
    ks = ks.reshape(nblk, gl, 2 * t - 1, gc, gc)
    d_tab = jnp.einsum('bglec,gh->blgehc', ks, eye_gl).reshape(nblk, 2 * t - 1, LANE, LANE)

    def compact(z_ri):
        z = jnp.transpose(z_ri, (3, 2, 1, 4, 0, 5)).reshape(nblk, gl, t, 2, gc, 2 * ns)
        return jnp.transpose(z, (0, 2, 1, 3, 4, 5))

    pf = lam_pw[:t][::-1][:, 0, :, :, None] * b_bar[0][None]
    pb = lam_pw[:t][:, 1, :, :, None] * b_bar[1][None]
    pcat = jnp.swapaxes(jnp.stack([pf, pb], axis=0), -1, -2)
    p_tab = compact(jnp.stack([pcat.real, pcat.imag], axis=0))
    qf = c_mat[0][None] * lam_pw[1:t + 1, 0][:, :, None, :]
    qb = c_mat[1][None] * lam_pw[1:t + 1][::-1][:, 1][:, :, None, :]
    qcat = jnp.stack([qf, qb], axis=0)
    q_tab = compact(jnp.stack([qcat.real, -qcat.imag], axis=0))

    a = lam_pw[t]
    a1 = jnp.concatenate([a.real, a.real], axis=-1).reshape(2, g * 2 * ns)
    a2 = jnp.concatenate([-a.imag, a.imag], axis=-1).reshape(2, g * 2 * ns)
    m_op, p_op, q_op = _s5_expand(d_tab, p_tab, q_tab)
    return m_op, p_op, q_op, a1, a2


def _s5_m_kernel(d_ref, o_ref, *, t):
    s = pl.program_id(1)
    for tt in range(t):
        o_ref[0, :, tt * LANE:(tt + 1) * LANE] = d_ref[0, tt - s + t - 1].astype(o_ref.dtype)


def _s5_pq_kernel(c_ref, o_ref, *, transpose):
    gl, nd, gc, w = c_ref.shape[2:]
    rows = [jnp.concatenate([c_ref[0, 0, g, dd] if (g == h) else jnp.zeros((gc, w), F32)
                             for dd in range(nd) for h in range(gl)], axis=1) for g in range(gl)]
    blk = jnp.concatenate(rows, axis=0)
    o_ref[0] = (blk.T if transpose else blk).astype(o_ref.dtype)


def _s5_expand(d_tab, p_tab, q_tab):
    nblk, nlag = d_tab.shape[:2]
    t = (nlag + 1) // 2
    _, _, gl, nd, gc, w = p_tab.shape
    ncol = nd * gl * w
    m_op = pl.pallas_call(
        functools.partial(_s5_m_kernel, t=t), grid=(nblk, t),
        in_specs=[pl.BlockSpec((1, nlag, LANE, LANE), lambda b, s: (b, 0, 0, 0))],
        out_specs=pl.BlockSpec((1, LANE, t * LANE), lambda b, s: (b, s, 0)),
        out_shape=jax.ShapeDtypeStruct((nblk, t * LANE, t * LANE), BF16),
        compiler_params=_cp("parallel", "parallel"), name="s5_m_op",
    )(d_tab)
    tab_spec = pl.BlockSpec((1, 1, gl, nd, gc, w), lambda b, j: (b, j, 0, 0, 0, 0))
    p_op = pl.pallas_call(
        functools.partial(_s5_pq_kernel, transpose=False), grid=(nblk, t),
        in_specs=[tab_spec],
        out_specs=pl.BlockSpec((1, LANE, ncol), lambda b, j: (b, j, 0)),
        out_shape=jax.ShapeDtypeStruct((nblk, t * LANE, ncol), BF16),
        compiler_params=_cp("parallel", "parallel"), name="s5_p_op",
    )(p_tab)
    q_op = pl.pallas_call(
        functools.partial(_s5_pq_kernel, transpose=True), grid=(nblk, t),
        in_specs=[tab_spec],
        out_specs=pl.BlockSpec((1, ncol, LANE), lambda b, j: (b, 0, j)),
        out_shape=jax.ShapeDtypeStruct((nblk, ncol, t * LANE), BF16),
        compiler_params=_cp("parallel", "parallel"), name="s5_q_op",
    )(q_tab)
    return m_op, p_op, q_op


def _s5_sum_kernel(*refs, t):
    u_refs = refs[:t]
    p_ref, o_ref = refs[t:]
    u = jnp.concatenate([r[...] for r in u_refs], axis=1)
    o_ref[...] = _dot(u, p_ref[0])


def _s5_sum_call(h, p_op, rb):
    rows = h.shape[0]
    t = S5_T
    nblk = p_op.shape[0]
    ns2 = p_op.shape[2]
    u_specs = [pl.BlockSpec((rb, LANE), lambda gb, r, s=s: (r, s * nblk + gb)) for s in range(t)]
    return pl.pallas_call(
        functools.partial(_s5_sum_kernel, t=t), grid=(nblk, rows // rb),
        in_specs=u_specs + [pl.BlockSpec((1,) + p_op.shape[1:], lambda gb, r: (gb, 0, 0))],
        out_specs=pl.BlockSpec((rb, ns2), lambda gb, r: (r, gb)),
        out_shape=jax.ShapeDtypeStruct((rows, nblk * ns2), F32),
        compiler_params=_cp("parallel", "parallel"), name="s5_sum",
    )(*([h] * t), p_op)


def _s5_rec_kernel(sf_ref, sb_ref, a1f_ref, a2f_ref, a1b_ref, a2b_ref, hf0_ref, hb0_ref,
                   hf_ref, hb_ref, ff_ref, fb_ref, *, kb):
    a1f, a2f, a1b, a2b = a1f_ref[...], a2f_ref[...], a1b_ref[...], a2b_ref[...]

    @pl.when(pl.program_id(0) == 0)
    def _():
        ff_ref[...] = hf0_ref[...]
        fb_ref[...] = hb0_ref[...]

    def body(i, carry):
        hf, hb = carry
        k = kb - 1 - i
        hf_ref[i] = hf
        hb_ref[k] = hb
        hf = a1f * hf + a2f * pltpu.roll(hf, 64, 1) + sf_ref[i]
        hb = a1b * hb + a2b * pltpu.roll(hb, 64, 1) + sb_ref[k]
        return hf, hb

    hf, hb = lax.fori_loop(0, kb, body, (ff_ref[...], fb_ref[...]))
    ff_ref[...] = hf
    fb_ref[...] = hb


def _s5_rec_call(sf, sb, a1, a2, hf0, hb0):
    nk, r, _ = sf.shape
    rep = r * LANE // a1.shape[1]
    tile = lambda v: jnp.tile(v.reshape(1, -1), (1, rep)).reshape(r, LANE)
    kb = min(32, nk)
    nb = nk // kb
    fwd3 = pl.BlockSpec((kb, r, LANE), lambda i: (i, 0, 0))
    bwd3 = pl.BlockSpec((kb, r, LANE), lambda i: (nb - 1 - i, 0, 0))
    full2 = _const_spec((r, LANE))
    return pl.pallas_call(
        functools.partial(_s5_rec_kernel, kb=kb), grid=(nb,),
        in_specs=[fwd3, bwd3, full2, full2, full2, full2, full2, full2],
        out_specs=[fwd3, bwd3, full2, full2],
        out_shape=[jax.ShapeDtypeStruct((nk, r, LANE), F32)] * 2 + [jax.ShapeDtypeStruct((r, LANE), F32)] * 2,
        compiler_params=_cp("arbitrary"), name="s5_rec",
    )(sf, sb, tile(a1[0]), tile(a2[0]), tile(a1[1]), tile(a2[1]), hf0, hb0)


def _s5_out_kernel(*refs, t):
    u_refs = refs[:t]
    h_ref, m_ref, q_ref, o_ref = refs[t:]
    u = jnp.concatenate([r[...] for r in u_refs], axis=1)
    y = _dot(u, m_ref[0]) + _dot(h_ref[...].astype(BF16), q_ref[0])
    o_ref[0] = (0.5 * y * (1.0 + lax.erf(y * (2.0 ** -0.5)))).astype(o_ref.dtype)


def _s5_out_call(h, hcat, m_op, q_op, rb):
    rows = h.shape[0]
    t = S5_T
    nblk = m_op.shape[0]
    ns2 = q_op.shape[1]
    u_specs = [pl.BlockSpec((rb, LANE), lambda gb, r, s=s: (r, s * nblk + gb)) for s in range(t)]
    return pl.pallas_call(
        functools.partial(_s5_out_kernel, t=t), grid=(nblk, rows // rb),
        in_specs=u_specs + [pl.BlockSpec((rb, ns2), lambda gb, r: (r, gb)),
                            pl.BlockSpec((1,) + m_op.shape[1:], lambda gb, r: (gb, 0, 0)),
                            pl.BlockSpec((1,) + q_op.shape[1:], lambda gb, r: (gb, 0, 0))],
        out_specs=pl.BlockSpec((1, rb, t * LANE), lambda gb, r: (gb, r, 0)),
        out_shape=jax.ShapeDtypeStruct((nblk, rows, t * LANE), BF16),
        compiler_params=_cp("parallel", "parallel"), name="s5_out",
    )(*([h] * t), hcat, m_op, q_op)


def _s5_layer(xl, xc, mods_l, mods_c, pre_g, fin, lam_re, lam_im, log_dt, b_re, b_im, c_re, c_im,
              d_skip, w_glu, b_glu, b, n, c):
    d = xl.shape[-1]
    t = S5_T
    m_op, p_op, q_op, a1, a2 = _s5_operators(lam_re, lam_im, log_dt, b_re, b_im, c_re, c_im, d_skip)
    nblk = m_op.shape[0]
    half = a1.shape[1]
    hl = _normmod_call(xl, mods_l, pre_g, min(512, n), n).reshape(b * n // t, t * d)
    hc = _normmod_call(xc, mods_c, pre_g, c, c).reshape(b * c // t, t * d)

    def summaries(h, nk):
        s = _s5_sum_call(h, p_op, min(512, h.shape[0]))
        s = s.reshape(b, nk, nblk, 2, half // nblk)
        s = jnp.transpose(s, (3, 1, 0, 2, 4)).reshape(2, nk, b * half // LANE, LANE)
        return s[0], s[1]

    zeros = jnp.zeros((b * half // LANE, LANE), F32)
    sfc, sbc = summaries(hc, c // t)
    _, _, hf0, hb0 = _s5_rec_call(sfc, sbc, a1, a2, zeros, zeros)
    nk = n // t
    sfl, sbl = summaries(hl, nk)
    hf, hb, _, _ = _s5_rec_call(sfl, sbl, a1, a2, hf0, hb0)
    hcat = jnp.stack([hf.reshape(nk, b, nblk, half // nblk), hb.reshape(nk, b, nblk, half // nblk)], axis=3)
    hcat = jnp.transpose(hcat, (1, 0, 2, 3, 4)).reshape(b * nk, nblk * 2 * (half // nblk))
    rb = min(512, b * nk)
    gact = _s5_out_call(hl, hcat, m_op, q_op, rb)
    g_nat = jnp.transpose(gact.reshape(nblk, b * nk, t, LANE), (1, 2, 0, 3)).reshape(b * n, d)
    return fin(xl, g_nat, mods_l, w_glu.astype(BF16), b_glu, min(512, n), n, glu=True)


def _fn_a_kernel(x_ref, mod_ref, g_ref, cs_ref, fm_ref, twc_ref, tws_ref, or_ref, oi_ref, *, nt, d):
    g = g_ref[...]
    shift, scale = mod_ref[0, 0:1, :], mod_ref[0, 1:2, :]
    cs = cs_ref[...]
    fm = fm_ref[...]
    ng = d // FNET_GC
    for s in range(nt):
        h = _normmod(x_ref[0, :, s * d:(s + 1) * d], g, shift, scale).astype(BF16)
        ab = [_dot(h[:, k * FNET_GC:(k + 1) * FNET_GC], cs) for k in range(ng)]
        a = jnp.concatenate([z[:, :FNET_GC] for z in ab], axis=1)
        bb = jnp.concatenate([z[:, FNET_GC:] for z in ab], axis=1)
        x1 = _dot(fm, jnp.concatenate([a, bb], axis=0).astype(BF16))
        n1 = x1.shape[0] // 2
        xr, xi = x1[:n1], x1[n1:]
        c = jnp.concatenate([twc_ref[s]] * (d // LANE), axis=1)
        sn = jnp.concatenate([tws_ref[s]] * (d // LANE), axis=1)
        or_ref[0, s] = (xr * c + xi * sn).astype(or_ref.dtype)
        oi_ref[0, s] = (xi * c - xr * sn).astype(oi_ref.dtype)


def _fn_c_kernel(fm_ref, xr_ref, xi_ref, o_ref):
    o_ref[0] = _dot(fm_ref[...], jnp.concatenate([xr_ref[0], xi_ref[0]], axis=0)).astype(o_ref.dtype)


def _fnet_layer(xl, mods_l, pre_g, fin, w_o, b_o, b, n, d):
    n2 = FFT_N2
    n1 = n // n2
    gc = FNET_GC
    cc, sc = _dft_cs(gc, gc, gc)
    cs = jnp.asarray(np.concatenate([cc, sc], axis=1) / np.sqrt(gc), BF16)
    c1, s1 = _dft_cs(n1, n1, n1)
    fm_a = jnp.asarray(np.block([[c1, -s1], [-s1, -c1]]) / np.sqrt(n), BF16)
    c2, s2 = _dft_cs(n2, n2, n2)
    fm_c = jnp.asarray(np.concatenate([c2, s2], axis=1), BF16)
    twc, tws = _twiddle_tables(n1, n2, lead_t2=True)
    nt = 4 if n2 % 4 == 0 else 1
    x3 = xl.reshape(b, n1, n2 * d)
    mid = jax.ShapeDtypeStruct((b, n2, n1, d), BF16)
    mid_spec = pl.BlockSpec((1, nt, n1, d), lambda bi, j: (bi, j, 0, 0))
    tw_spec = pl.BlockSpec((nt, n1, LANE), lambda bi, j: (j, 0, 0))
    xr, xi = pl.pallas_call(
        functools.partial(_fn_a_kernel, nt=nt, d=d), grid=(b, n2 // nt),
        in_specs=[pl.BlockSpec((1, n1, nt * d), lambda bi, j: (bi, 0, j)),
                  pl.BlockSpec((1, 8, d), lambda bi, j: (bi, 0, 0)), _const_spec((1, d)),
                  _const_spec(cs.shape), _const_spec(fm_a.shape), tw_spec, tw_spec],
        out_specs=[mid_spec, mid_spec], out_shape=[mid, mid],
        compiler_params=_cp("parallel", "parallel"), name="fn_a",
    )(x3, mods_l, pre_g.reshape(1, d), cs, fm_a, twc, tws)
    cb = 4 * d if n1 % 4 == 0 else d
    col = pl.BlockSpec((1, n2, cb), lambda bi, j: (bi, 0, j))
    y = pl.pallas_call(
        _fn_c_kernel, grid=(b, n1 * d // cb),
        in_specs=[_const_spec(fm_c.shape), col, col],
        out_specs=col, out_shape=jax.ShapeDtypeStruct((b, n2, n1 * d), BF16),
        compiler_params=_cp("parallel", "parallel"), name="fn_c",
    )(fm_c, xr.reshape(b, n2, n1 * d), xi.reshape(b, n2, n1 * d))
    return fin(xl, y.reshape(b * n, d), mods_l, w_o.astype(BF16), b_o, min(512, n), n)


def kernel(x, c, ctx, c_ctx, mod_w, mod_b, mix_pre_g, mix_post_g, ffn_pre_g, ffn_post_g, ffn_w13, ffn_w2,
           mla_w_in, mla_q_norm_g, mla_kv_norm_g, mla_w_uq, mla_w_ukv, mla_w_o,
           hy_w_in, hy_b_in, hy_conv_w, hy_conv_b, hy_f_w1, hy_f_b1, hy_f_freq, hy_f_w2, hy_f_b2, hy_f_w3,
           hy_skip, hy_w_out, hy_b_out,
           s5_lambda_re, s5_lambda_im, s5_log_dt, s5_b_re, s5_b_im, s5_c_re, s5_c_im, s5_d, s5_w_glu, s5_b_glu,
           fn_w_o, fn_b_o):
    b, n, d = x.shape
    cl = ctx.shape[1]
    depth = mod_w.shape[0]
    assert b == 2 and depth == 4, "two batches ride one complex transform; one layer per mixer"
    mods = _mods(c, c_ctx, mod_w, mod_b)
    xl = x.reshape(b * n, d)
    xc = ctx.reshape(b * cl, d)
    tm = min(512, n)

    def finisher(i):
        w13 = ffn_w13[i].astype(BF16)
        w2 = ffn_w2[i].astype(BF16)

        def fin(x_, y_, mods_, wm, bm, tm_, rows_per_batch, glu=False):
            return _mix_ffn_call(x_, y_, mods_, wm, bm, mix_post_g[i], ffn_pre_g[i], ffn_post_g[i], w13, w2,
                                 tm_, rows_per_batch, glu)
        return fin

    def mods_c(i):
        return jnp.broadcast_to(mods[i, 2:3], (b, 8, d))

    xl, xc = _mla_layer(xl, xc, mods[0, 0:2], mods_c(0), mix_pre_g[0], finisher(0), mla_w_in[0],
                        mla_q_norm_g[0], mla_kv_norm_g[0], mla_w_uq[0], mla_w_ukv[0], mla_w_o[0], b, n, cl)
    filt = (hy_f_w1[0], hy_f_b1[0], hy_f_freq[0], hy_f_w2[0], hy_f_b2[0], hy_f_w3[0])
    xl, xc = _hyena_layer(xl, xc, mods[1, 0:2], mods_c(1), mix_pre_g[1], finisher(1), hy_w_in[0], hy_b_in[0],
                          hy_conv_w[0], hy_conv_b[0], filt, hy_skip[0], hy_w_out[0], hy_b_out[0], b, n, cl)
    xl = _s5_layer(xl, xc, mods[2, 0:2], mods_c(2), mix_pre_g[2], finisher(2), s5_lambda_re[0],
                   s5_lambda_im[0], s5_log_dt[0], s5_b_re[0], s5_b_im[0], s5_c_re[0], s5_c_im[0], s5_d[0],
                   s5_w_glu[0], s5_b_glu[0], b, n, cl)
    xl = _fnet_layer(xl, mods[3, 0:2], mix_pre_g[3], finisher(3), fn_w_o[0], fn_b_o[0], b, n, d)
    return xl.reshape(b, n, d)
```

```python
import functools
import math

import numpy as np
import jax
import jax.numpy as jnp
from jax import lax
from jax.experimental import pallas as pl
from jax.experimental.pallas import tpu as pltpu

F32 = jnp.float32
BF16 = jnp.bfloat16
NORM_EPS = 1e-6
LANE = 128
VMEM_LIMIT = 56 * 1024 * 1024
HIGHEST = lax.Precision.HIGHEST

GRID_W = 64
ROPE_THETA = 10000.0
MLA_HEADS = 8
MLA_NOPE = 128
MLA_ROPE = 64
MLA_V = 128
MLA_VT = MLA_V + 16
HYENA_BANDS = 16
HYENA_TARGET = 1e-2
HYENA_FAST = 0.3
HYENA_SLOW = 1.5
S5_GROUP = 16
S5_STATE = 64
S5_T = 16
FNET_GC = 128
FFT_N2 = 128


def _cp(*sem):
    return pltpu.CompilerParams(dimension_semantics=sem, vmem_limit_bytes=VMEM_LIMIT)


def _dot(a, b):
    return jnp.dot(a, b, preferred_element_type=F32)


def _dot_hi(a, b):
    return jnp.dot(a, b, preferred_element_type=F32, precision=HIGHEST)


def _rms(x, g):
    ms = jnp.mean(x * x, axis=-1, keepdims=True)
    return x * lax.rsqrt(ms + NORM_EPS) * g


def _normmod(x, g, shift, scale):
    return _rms(x, g) * (1.0 + scale) + shift


def _const_spec(shape):
    nd = len(shape)
    return pl.BlockSpec(shape, lambda *_: (0,) * nd)


def _mods_kernel(st_ref, w_ref, b_ref, o_ref):
    st = st_ref[...]
    st = st * jax.nn.sigmoid(st)
    w = w_ref[0]
    rows = [jnp.sum(st[:, r:r + 1] * w, axis=0, keepdims=True) for r in range(3)]
    rows.append(jnp.zeros((5, w.shape[1]), F32))
    o_ref[0] = jnp.concatenate(rows, axis=0) + b_ref[0]


def _mods(c, c_ctx, mod_w, mod_b):
    depth, d, n6 = mod_w.shape
    st = jnp.zeros((d, 8), F32).at[:, 0:2].set(c.T).at[:, 2].set(c_ctx)
    tn = 1024
    out = pl.pallas_call(
        _mods_kernel,
        grid=(depth, n6 // tn),
        in_specs=[_const_spec((d, 8)),
                  pl.BlockSpec((1, d, tn), lambda i, j: (i, 0, j)),
                  pl.BlockSpec((1, 1, tn), lambda i, j: (i, 0, j))],
        out_specs=pl.BlockSpec((1, 8, tn), lambda i, j: (i, 0, j)),
        out_shape=jax.ShapeDtypeStruct((depth, 8, n6), F32),
        compiler_params=_cp("parallel", "parallel"),
        name="mods",
    )(st, mod_w, mod_b.reshape(depth, 1, n6))
    m = out[:, :3].reshape(depth, 3, n6 // d, d)
    return jnp.pad(m, ((0, 0), (0, 0), (0, 8 - n6 // d), (0, 0)))


def _row_specs(tm, d, tpb):
    x_spec = pl.BlockSpec((tm, d), lambda i: (i, 0))
    mod_spec = pl.BlockSpec((1, 8, d), lambda i: (i // tpb, 0, 0))
    return x_spec, mod_spec


def _normmod_kernel(x_ref, mod_ref, g_ref, o_ref):
    h = _normmod(x_ref[...], g_ref[...], mod_ref[0, 0:1, :], mod_ref[0, 1:2, :])
    o_ref[...] = h.astype(o_ref.dtype)


def _normmod_call(x, mods, g, tm, rows_per_batch):
    m, d = x.shape
    x_spec, mod_spec = _row_specs(tm, d, rows_per_batch // tm)
    return pl.pallas_call(
        _normmod_kernel, grid=(m // tm,),
        in_specs=[x_spec, mod_spec, _const_spec((1, d))],
        out_specs=x_spec, out_shape=jax.ShapeDtypeStruct((m, d), BF16),
        compiler_params=_cp("parallel"), name="normmod",
    )(x, mods, g.reshape(1, d))


def _mix_ffn_kernel(x_ref, y_ref, mod_ref, wm_ref, bm_ref, gm_ref, pre_ref, post_ref, w13_ref, w2_ref, o_ref,
                    *, f, fc, glu):
    z = _dot(y_ref[...].astype(BF16), wm_ref[...]) + bm_ref[...]
    if glu:
        d = o_ref.shape[-1]
        z = z[:, :d] * jax.nn.sigmoid(z[:, d:])
    x = x_ref[...] + mod_ref[0, 2:3, :] * _rms(z, gm_ref[...])
    h = _normmod(x, pre_ref[...], mod_ref[0, 3:4, :], mod_ref[0, 4:5, :]).astype(BF16)
    acc = None
    for c in range(f // fc):
        a = _dot(h, w13_ref[:, c * fc:(c + 1) * fc])
        b = _dot(h, w13_ref[:, f + c * fc:f + (c + 1) * fc])
        gact = (a * jax.nn.sigmoid(a) * b).astype(BF16)
        part = _dot(gact, w2_ref[c * fc:(c + 1) * fc, :])
        acc = part if acc is None else acc + part
    o_ref[...] = x + mod_ref[0, 5:6, :] * _rms(acc, post_ref[...])


def _mix_ffn_call(x, y, mods, wm, bm, mix_post_g, ffn_pre_g, ffn_post_g, w13, w2, tm, rows_per_batch, glu=False):
    m, d = x.shape
    k, nm = wm.shape
    f = w2.shape[0]
    fc = f // 2 if (f // 2) % LANE == 0 else f
    x_spec, mod_spec = _row_specs(tm, d, rows_per_batch // tm)
    once = lambda shape: pl.BlockSpec(shape, lambda i: (0, 0), pipeline_mode=pl.Buffered(1))
    return pl.pallas_call(
        functools.partial(_mix_ffn_kernel, f=f, fc=fc, glu=glu), grid=(m // tm,),
        in_specs=[x_spec, pl.BlockSpec((tm, k), lambda i: (i, 0)), mod_spec,
                  once((k, nm)), _const_spec((1, nm)), _const_spec((1, d)), _const_spec((1, d)),
                  _const_spec((1, d)), once((d, 2 * f)), once((f, d))],
        out_specs=x_spec, out_shape=jax.ShapeDtypeStruct((m, d), F32),
        compiler_params=_cp("parallel"), name="mix_ffn",
    )(x, y, mods, wm, bm.reshape(1, nm), mix_post_g.reshape(1, d), ffn_pre_g.reshape(1, d),
      ffn_post_g.reshape(1, d), w13, w2)


def _mla_in_kernel(x_ref, mod_ref, g_ref, w_ref, qg_ref, kvg_ref, t1_ref, t2_ref,
                   qn_ref, cn_ref, kr_ref, *, ql, kvl):
    h = _normmod(x_ref[...], g_ref[...], mod_ref[0, 0:1, :], mod_ref[0, 1:2, :])
    z = _dot(h.astype(BF16), w_ref[...])
    qn_ref[...] = _rms(z[:, :ql], qg_ref[...]).astype(BF16)
    cn_ref[...] = _rms(z[:, ql:ql + kvl], kvg_ref[...]).astype(BF16)
    pair = z[:, ql + kvl:]
    kr_ref[...] = (pair * t1_ref[...] + pltpu.roll(pair, 64, 1) * t2_ref[...]).astype(BF16)


def _mla_q_kernel(qn_ref, w_ref, t1_ref, t2_ref, q_ref, *, qscale):
    z = _dot(qn_ref[...], w_ref[...])
    t1 = t1_ref[...]
    t2 = t2_ref[...]
    for h in range(MLA_HEADS):
        base = h * 256
        pair = z[:, base + 128:base + 256]
        rp = pair * t1 + pltpu.roll(pair, 64, 1) * t2
        qcat = jnp.concatenate([z[:, base:base + 128], rp], axis=1) * qscale
        q_ref[0, h] = qcat.T.astype(BF16)


def _mla_kv_kernel(cn_ref, kr_ref, w_ref, k_ref, vt_ref):
    z = _dot(cn_ref[...], w_ref[...])
    kr = kr_ref[...]
    tk = kr.shape[0]
    ones_blk = (lax.broadcasted_iota(jnp.int32, (MLA_VT - MLA_V, tk), 0) == 0).astype(BF16)
    for h in range(MLA_HEADS):
        base = h * 256
        k_ref[0, h, 0, :, 0:128] = z[:, base:base + 128].astype(BF16)
        k_ref[0, h, 0, :, 128:256] = kr
        vt_ref[0, h, 0, 0:MLA_V, :] = z[:, base + 128:base + 256].T.astype(BF16)
        vt_ref[0, h, 0, MLA_V:MLA_VT, :] = ones_blk


def _flash_kernel(q_ref, kc_ref, vc_ref, *rest, n_lat):
    if n_lat:
        kl_ref, vl_ref, o_ref, s_scr, acc_scr = rest
    else:
        o_ref, acc_scr = rest
    qt = q_ref[0, 0]

    def qk(k, slot):
        s = _dot(k, qt)
        s_scr[slot] = s
        return jnp.max(s, axis=0, keepdims=True)

    def sm_pv(slot, vt, m, mx):
        m_new = jnp.maximum(m, mx)
        alpha = jnp.exp2(m - m_new)
        p = jnp.exp2(s_scr[slot] - m_new).astype(BF16)
        acc_scr[...] = alpha * acc_scr[...] + _dot(vt, p)
        return m_new

    sc = _dot(kc_ref[0, 0, 0], qt)
    m = jnp.max(sc, axis=0, keepdims=True)
    acc_scr[...] = _dot(vc_ref[0, 0, 0], jnp.exp2(sc - m).astype(BF16))
    if n_lat:
        mx = qk(kl_ref[0, 0, 0], 0)

        def body(i, carry):
            m, mx0 = carry
            c = 2 * i
            mx1 = qk(kl_ref[0, 0, c + 1], 1)
            m = sm_pv(0, vl_ref[0, 0, c], m, mx0)
            mx0 = qk(kl_ref[0, 0, jnp.minimum(c + 2, n_lat - 1)], 0)
            m = sm_pv(1, vl_ref[0, 0, c + 1], m, mx1)
            return m, mx0

        lax.fori_loop(0, n_lat // 2, body, (m, mx))
    acc = acc_scr[...]
    o_ref[0] = (acc[0:MLA_V] / acc[MLA_V:MLA_V + 1]).T.astype(o_ref.dtype)


def _rope_tables(n_lat):
    rows = n_lat // GRID_W
    row = jnp.repeat(jnp.arange(rows, dtype=F32), GRID_W)
    col = jnp.tile(jnp.arange(GRID_W, dtype=F32), rows)
    axis_dim = MLA_ROPE // 2
    inv_freq = 1.0 / (ROPE_THETA ** (jnp.arange(0, axis_dim, 2, dtype=F32) / axis_dim))
    ang_r = row[:, None] * inv_freq
    ang_c = col[:, None] * inv_freq
    cr, sr, cc, sc = jnp.cos(ang_r), jnp.sin(ang_r), jnp.cos(ang_c), jnp.sin(ang_c)
    cp = jnp.concatenate([cr, cr, cc, cc], axis=-1)
    sp = jnp.concatenate([-sr, sr, -sc, sc], axis=-1)
    return cp, sp


_ROPE_SWAP = np.concatenate([np.arange(16, 32), np.arange(0, 16), np.arange(48, 64), np.arange(32, 48)])


def _mla_side(x, mods, pre_g, w_in_ext, q_g, kv_g, w_uq_ext, w_ukv, tabs, b, n, tm, tk, want_q):
    m, d = x.shape
    ql, kvl = q_g.shape[-1], kv_g.shape[-1]
    t1k, t2k, t1q, t2q = tabs
    x_spec, mod_spec = _row_specs(tm, d, n // tm)
    row = lambda w: pl.BlockSpec((tm, w), lambda i: (i, 0))
    tab_spec = pl.BlockSpec((tm, 128), lambda i: (i % (n // tm), 0))
    qn, cn, kr = pl.pallas_call(
        functools.partial(_mla_in_kernel, ql=ql, kvl=kvl), grid=(m // tm,),
        in_specs=[x_spec, mod_spec, _const_spec((1, d)), _const_spec(w_in_ext.shape),
                  _const_spec((1, ql)), _const_spec((1, kvl)), tab_spec, tab_spec],
        out_specs=[row(ql), row(kvl), row(128)],
        out_shape=[jax.ShapeDtypeStruct((m, ql), BF16), jax.ShapeDtypeStruct((m, kvl), BF16),
                   jax.ShapeDtypeStruct((m, 128), BF16)],
        compiler_params=_cp("parallel"), name="mla_in",
    )(x, mods, pre_g.reshape(1, d), w_in_ext, q_g.reshape(1, ql), kv_g.reshape(1, kvl), t1k, t2k)

    nc = n // tk
    kt, v = pl.pallas_call(
        _mla_kv_kernel, grid=(b, nc),
        in_specs=[pl.BlockSpec((tk, kvl), lambda bi, c: (bi * nc + c, 0)),
                  pl.BlockSpec((tk, 128), lambda bi, c: (bi * nc + c, 0)),
                  _const_spec(w_ukv.shape)],
        out_specs=[pl.BlockSpec((1, MLA_HEADS, 1, tk, 256), lambda bi, c: (bi, 0, c, 0, 0)),
                   pl.BlockSpec((1, MLA_HEADS, 1, MLA_VT, tk), lambda bi, c: (bi, 0, c, 0, 0))],
        out_shape=[jax.ShapeDtypeStruct((b, MLA_HEADS, nc, tk, 256), BF16),
                   jax.ShapeDtypeStruct((b, MLA_HEADS, nc, MLA_VT, tk), BF16)],
        compiler_params=_cp("parallel", "parallel"), name="mla_kv",
    )(cn, kr, w_ukv)

    q = None
    if want_q:
        qscale = (MLA_NOPE + MLA_ROPE) ** -0.5 * math.log2(math.e)
        tpb = n // tm
        q = pl.pallas_call(
            functools.partial(_mla_q_kernel, qscale=qscale), grid=(m // tm,),
            in_specs=[row(ql), _const_spec(w_uq_ext.shape), tab_spec, tab_spec],
            out_specs=pl.BlockSpec((1, MLA_HEADS, 256, tm), lambda i: (i // tpb, 0, 0, i % tpb)),
            out_shape=jax.ShapeDtypeStruct((b, MLA_HEADS, 256, n), BF16),
            compiler_params=_cp("parallel"), name="mla_q",
        )(qn, w_uq_ext, t1q, t2q)
    return q, kt, v


def _flash_call(qt, kc, vtc, kl, vtl, tq):
    b, hh, _, n = qt.shape
    c = kc.shape[-2]
    n_lat = 0 if kl is None else kl.shape[2]
    in_specs = [pl.BlockSpec((1, 1, 256, tq), lambda bi, h, i: (bi, h, 0, i)),
                pl.BlockSpec((1, 1, 1, c, 256), lambda bi, h, i: (bi, h, 0, 0, 0)),
                pl.BlockSpec((1, 1, 1, MLA_VT, c), lambda bi, h, i: (bi, h, 0, 0, 0))]
    args = [qt, kc, vtc]
    scratch = [pltpu.VMEM((MLA_VT, tq), F32)]
    if n_lat:
        assert n_lat % 2 == 0, "latent key chunks are consumed in pairs"
        tk = kl.shape[-2]
        in_specs += [pl.BlockSpec((1, 1, n_lat, tk, 256), lambda bi, h, i: (bi, h, 0, 0, 0)),
                     pl.BlockSpec((1, 1, n_lat, MLA_VT, tk), lambda bi, h, i: (bi, h, 0, 0, 0))]
        args += [kl, vtl]
        scratch = [pltpu.VMEM((2, tk, tq), F32)] + scratch
    return pl.pallas_call(
        functools.partial(_flash_kernel, n_lat=n_lat), grid=(b, hh, n // tq),
        in_specs=in_specs,
        out_specs=pl.BlockSpec((1, tq, MLA_V), lambda bi, h, i: (bi, i, h)),
        out_shape=jax.ShapeDtypeStruct((b, n, hh * MLA_V), BF16),
        scratch_shapes=scratch,
        compiler_params=_cp("parallel", "parallel", "arbitrary"), name="flash",
    )(*args)


def _mla_layer(xl, xc, mods_l, mods_c, pre_g, fin, w_in, q_g, kv_g, w_uq, w_ukv, w_o, b, n, c):
    d = xl.shape[-1]
    ql, kvl = q_g.shape[-1], kv_g.shape[-1]
    hh = MLA_HEADS
    rope_cols = w_in[:, ql + kvl:]
    w_in_ext = jnp.concatenate([w_in, rope_cols[:, _ROPE_SWAP]], axis=1).astype(BF16)
    wq = w_uq.reshape(ql, hh, MLA_NOPE + MLA_ROPE)
    w_uq_ext = jnp.concatenate([wq, wq[:, :, MLA_NOPE:][:, :, _ROPE_SWAP]], axis=-1)
    w_uq_ext = w_uq_ext.reshape(ql, hh * 256).astype(BF16)
    w_ukv_b = w_ukv.astype(BF16)
    w_o_b = w_o.astype(BF16)

    cp, sp = _rope_tables(n)
    z64l, o64l = jnp.zeros((n, 64), F32), jnp.ones((n, 64), F32)
    z64c, o64c = jnp.zeros((c, 64), F32), jnp.ones((c, 64), F32)
    cat = lambda a, bb: jnp.concatenate([a, bb], axis=1)
    tabs_l = (cat(cp, z64l), cat(sp, z64l), cat(cp, z64l), cat(sp, o64l))
    tabs_c = (cat(z64c, z64c), cat(z64c, o64c), cat(z64c, z64c), cat(z64c, o64c))

    tm_l = min(512, n)
    tk_l = min(512, n // 2)
    ql_, kl, vtl = _mla_side(xl, mods_l, pre_g, w_in_ext, q_g, kv_g, w_uq_ext, w_ukv_b, tabs_l,
                             b, n, tm_l, tk_l, True)
    qc_, kc, vtc = _mla_side(xc, mods_c, pre_g, w_in_ext, q_g, kv_g, w_uq_ext, w_ukv_b, tabs_c,
                             b, c, c, c, True)
    o_lat = _flash_call(ql_, kc, vtc, kl, vtl, min(1024, n)).reshape(b * n, hh * MLA_V)
    o_ctx = _flash_call(qc_, kc, vtc, None, None, c).reshape(b * c, hh * MLA_V)
    zb = jnp.zeros((d,), F32)
    xl = fin(xl, o_lat, mods_l, w_o_b, zb, tm_l, n)
    xc = fin(xc, o_ctx, mods_c, w_o_b, zb, c, c)
    return xl, xc


def _hy_in_kernel(x_ref, xp_ref, xn_ref, mod_ref, g_ref, w_ref, b_ref, cw_ref, cb_ref,
                  g0_ref, vg_ref, *, tpb):
    i = pl.program_id(0)
    g = g_ref[...]
    shift, scale = mod_ref[0, 0:1, :], mod_ref[0, 1:2, :]
    xcat = jnp.concatenate([xp_ref[...], x_ref[...], xn_ref[...]], axis=0)
    ucat = _dot(_normmod(xcat, g, shift, scale).astype(BF16), w_ref[...]) + b_ref[...]
    tm = x_ref.shape[0]
    u = ucat[8:tm + 8]
    first = (i % tpb) == 0
    last = (i % tpb) == tpb - 1
    prev_row = jnp.where(first, 0.0, ucat[7:8, :])
    next_row = jnp.where(last, 0.0, ucat[tm + 8:tm + 9, :])
    ridx = lax.broadcasted_iota(jnp.int32, (tm, 1), 0)
    dn = jnp.where(ridx == 0, prev_row, pltpu.roll(u, 1, 0))
    upw = jnp.where(ridx == tm - 1, next_row, pltpu.roll(u, tm - 1, 0))
    conv = cb_ref[...] + dn * cw_ref[0:1, :] + u * cw_ref[1:2, :] + upw * cw_ref[2:3, :]
    d = g0_ref.shape[-1]
    g0_ref[...] = conv[:, :d]
    vg_ref[...] = conv[:, 2 * d:] * conv[:, d:2 * d]


def _hy_in_call(x, mods, pre_g, w_in, b_in, conv_w, conv_b, tm, n):
    m, d = x.shape
    p = w_in.shape[1]
    tpb = n // tm
    x_spec, mod_spec = _row_specs(tm, d, tpb)
    r8 = tm // 8
    nb8 = m // 8
    prev_spec = pl.BlockSpec((8, d), lambda i: (jnp.maximum(i * r8 - 1, 0), 0))
    next_spec = pl.BlockSpec((8, d), lambda i: (jnp.minimum((i + 1) * r8, nb8 - 1), 0))
    cw = jnp.pad(conv_w, ((0, 8 - conv_w.shape[0]), (0, 0)))
    return pl.pallas_call(
        functools.partial(_hy_in_kernel, tpb=tpb), grid=(m // tm,),
        in_specs=[x_spec, prev_spec, next_spec, mod_spec, _const_spec((1, d)), _const_spec((d, p)),
                  _const_spec((1, p)), _const_spec((8, p)), _const_spec((1, p))],
        out_specs=[x_spec, x_spec],
        out_shape=[jax.ShapeDtypeStruct((m, d), F32), jax.ShapeDtypeStruct((m, d), F32)],
        compiler_params=_cp("parallel"), name="hy_in",
    )(x, x, x, mods, pre_g.reshape(1, d), w_in, b_in.reshape(1, p), cw, conv_b.reshape(1, p))


def _hy_filter_kernel(bands_ref, w1_ref, b1_ref, fq_ref, w2_ref, b2_ref, w3_ref, dl_ref,
                      k_ref, nrm_ref, *, n, tr):
    i = pl.program_id(0)
    bwd = i >= n // tr
    row = lax.broadcasted_iota(jnp.int32, (tr, LANE), 0) + i * tr
    j = jnp.where(bwd, 2 * n - row, row).astype(F32)
    lane = lax.broadcasted_iota(jnp.int32, (tr, LANE), 1)
    t = j * (1.0 / (n - 1))
    arg = (2.0 * math.pi / n) * j * bands_ref[...]
    z = jnp.where(lane == 0, t,
                  jnp.where(lane <= HYENA_BANDS, jnp.cos(arg),
                            jnp.where(lane <= 2 * HYENA_BANDS, -jnp.sin(arg), 0.0)))
    fq = fq_ref[...]
    a = jnp.sin(fq * (_dot_hi(z, w1_ref[...]) + b1_ref[...]))
    for k in range(w2_ref.shape[0]):
        a = jnp.sin(fq * (_dot_hi(a, w2_ref[k]) + b2_ref[k]))
    h = _dot_hi(a, w3_ref[...])
    d = k_ref.shape[-1]
    decay = jnp.exp(-t[:, 0:1] * dl_ref[...])
    k = jnp.where(bwd, h[:, d:], h[:, :d]) * decay
    k = jnp.where(row[:, 0:1] == n, 0.0, k)
    k_ref[...] = k
    part = jnp.sum(jnp.abs(k), axis=0, keepdims=True)

    @pl.when(i == 0)
    def _():
        nrm_ref[...] = jnp.zeros_like(nrm_ref)

    nrm_ref[...] += jnp.broadcast_to(part, nrm_ref.shape)


def _hy_filter_call(n, d, f_w1, f_b1, f_freq, f_w2, f_b2, f_w3):
    fw = f_w1.shape[1]
    tr = min(512, n)
    bands_np = np.zeros((1, LANE), np.float32)
    bands_np[0, 1:1 + HYENA_BANDS] = np.linspace(1e-4, HYENA_BANDS - 1, HYENA_BANDS, dtype=np.float32)
    bands_np[0, 1 + HYENA_BANDS:1 + 2 * HYENA_BANDS] = bands_np[0, 1:1 + HYENA_BANDS]
    w1p = jnp.zeros((LANE, fw), F32).at[:f_w1.shape[0]].set(f_w1)
    deltas = jnp.abs(jnp.linspace(math.log(HYENA_TARGET) / HYENA_SLOW, math.log(HYENA_TARGET) / HYENA_FAST,
                                  d, dtype=F32)).reshape(1, d)
    row = pl.BlockSpec((tr, d), lambda i: (i, 0))
    return pl.pallas_call(
        functools.partial(_hy_filter_kernel, n=n, tr=tr), grid=(2 * n // tr,),
        in_specs=[_const_spec((1, LANE)), _const_spec((LANE, fw)), _const_spec((1, fw)), _const_spec((1, fw)),
                  _const_spec(f_w2.shape), _const_spec((f_w2.shape[0], 1, fw)), _const_spec(f_w3.shape),
                  _const_spec((1, d))],
        out_specs=[row, _const_spec((8, d))],
        out_shape=[jax.ShapeDtypeStruct((2 * n, d), F32), jax.ShapeDtypeStruct((8, d), F32)],
        compiler_params=_cp("arbitrary"), name="hy_filter",
    )(jnp.asarray(bands_np), w1p, f_b1.reshape(1, fw), f_freq.reshape(1, fw), f_w2,
      f_b2.reshape(f_w2.shape[0], 1, fw), f_w3, deltas)


def _dft_cs(nf, nt, period):
    ft = (np.arange(nf)[:, None] * np.arange(nt)[None, :]) % period
    ang = 2.0 * np.pi * ft / period
    return np.cos(ang), np.sin(ang)


def _twiddle_tables(n1, n2, lead_t2):
    nn = n1 * n2
    f1 = jnp.arange(n1, dtype=jnp.int32)
    t2 = jnp.arange(n2, dtype=jnp.int32)
    idx = (t2[:, None] * f1[None, :]) % nn if lead_t2 else (f1[:, None] * t2[None, :]) % nn
    ang = idx.astype(F32) * (2.0 * math.pi / nn)
    shape = idx.shape + (LANE,)
    return (jnp.broadcast_to(jnp.cos(ang)[..., None], shape),
            jnp.broadcast_to(jnp.sin(ang)[..., None], shape))


KRON_R = 16
KRON_W = 256


def _cblock(mc):
    return np.block([[mc.real, -mc.imag], [mc.imag, mc.real]])


def _split_radix(n1):
    b = 16 if (n1 % 16 == 0 and n1 >= 64) else 4
    assert n1 % b == 0 and (n1 // b) % 2 == 0
    return n1 // b, b


def _slab_dft_mats(n1, a_in, a_out, sign, neg_im=False, real_in=False, scale=1.0):
    a, b = _split_radix(n1)
    r = KRON_R
    eye = np.eye(r)
    w = lambda num, den: np.exp(sign * 2j * np.pi * num / den)
    ua = np.arange(a)[:, None] * np.arange(a)[None, :]
    vb = np.arange(b)[:, None] * np.arange(b)[None, :]
    if sign < 0:
        m1 = np.kron(w(ua[:, :a_in], a) * scale, eye)
        l1 = _cblock(m1)
        if neg_im:
            l1[:, a_in * r:] *= -1.0
        if real_in:
            l1 = l1[:, :a_in * r]
        l2 = np.stack([_cblock(np.kron(w(vb, b) * w(u * np.arange(b)[None, :], n1), eye)) for u in range(a)])
    else:
        l1 = np.stack([_cblock(np.kron(w(vb, b) * w(u * np.arange(b)[:, None], n1), eye)) for u in range(a)])
        l2 = _cblock(np.kron(w(ua[:a_out, :], a) * scale, eye))
    return jnp.asarray(l1, BF16), jnp.asarray(l2, BF16)


def _kfa_kernel(*refs, nparts, a_in, a, b):
    l1_ref, l2_ref = refs[0], refs[1]
    parts = refs[2:2 + nparts]
    twc_ref, tws_ref, or_ref, oi_ref = refs[2 + nparts:]
    r = KRON_R
    reps = or_ref.shape[-1] // LANE
    l1 = l1_ref[...]
    y1 = []
    for bb in range(b):
        x = jnp.concatenate([p[0, aa * b + bb] for p in parts for aa in range(a_in)], axis=0).astype(BF16)
        y1.append(_dot(l1, x).astype(BF16))
    for u in range(a):
        x = jnp.concatenate([y1[bb][u * r:(u + 1) * r] for bb in range(b)]
                            + [y1[bb][(a + u) * r:(a + u + 1) * r] for bb in range(b)], axis=0)
        z = _dot(l2_ref[u], x)
        for v in range(b):
            f1 = u + a * v
            zr = z[v * r:(v + 1) * r]
            zi = z[(b + v) * r:(b + v + 1) * r]
            c = jnp.concatenate([twc_ref[f1]] * reps, axis=1)
            sn = jnp.concatenate([tws_ref[f1]] * reps, axis=1)
            or_ref[0, f1] = (zr * c + zi * sn).astype(or_ref.dtype)
            oi_ref[0, f1] = (zi * c - zr * sn).astype(oi_ref.dtype)


def _kfa_call(parts, nb_out, n1, n2, d, l1, l2, twc, tws):
    a, b = _split_radix(n1)
    a_in = parts[0][0].shape[1] // b
    r, w = KRON_R, min(KRON_W, d)
    in_specs = [_const_spec(l1.shape), pl.BlockSpec(l2.shape, lambda bi, j, k: (0, 0, 0), pipeline_mode=pl.Buffered(1))]
    args = [l1, l2]
    for arr, bi_fixed in parts:
        t1_in = arr.shape[1]
        if bi_fixed is None:
            in_specs.append(pl.BlockSpec((1, t1_in, r, w), lambda bi, j, k: (bi, 0, j, k)))
        else:
            in_specs.append(pl.BlockSpec((1, t1_in, r, w), lambda bi, j, k, f=bi_fixed: (f, 0, j, k)))
        args.append(arr)
    tw_spec = pl.BlockSpec((n1, r, LANE), lambda bi, j, k: (0, j, 0))
    out_spec = pl.BlockSpec((1, n1, r, w), lambda bi, j, k: (bi, 0, j, k))
    out = jax.ShapeDtypeStruct((nb_out, n1, n2, d), BF16)
    return pl.pallas_call(
        functools.partial(_kfa_kernel, nparts=len(parts), a_in=a_in, a=a, b=b),
        grid=(nb_out, n2 // r, d // w),
        in_specs=in_specs + [tw_spec, tw_spec], out_specs=[out_spec, out_spec], out_shape=[out, out],
        compiler_params=_cp("parallel", "parallel", "parallel"), name="kfa",
    )(*args, twc, tws)


def _kfc_kernel(l3_ref, l4_ref, gr_ref, gi_ref, vg_ref, g0_ref, skip_ref, o_ref, *, a, b, a_out):
    r = KRON_R
    y3 = []
    for u in range(a):
        x = jnp.concatenate([gr_ref[0, u + a * v] for v in range(b)]
                            + [gi_ref[0, u + a * v] for v in range(b)], axis=0)
        y3.append(_dot(l3_ref[u], x).astype(BF16))
    l4 = l4_ref[...]
    skip = skip_ref[...]
    for bb in range(b):
        x = jnp.concatenate([y3[u][bb * r:(bb + 1) * r] for u in range(a)]
                            + [y3[u][(b + bb) * r:(b + bb + 1) * r] for u in range(a)], axis=0)
        y = _dot(l4, x)
        for sg in range(2):
            for aa in range(a_out):
                t1 = aa * b + bb
                yb = y[(sg * a_out + aa) * r:(sg * a_out + aa + 1) * r]
                o_ref[sg, t1] = ((yb + vg_ref[sg, t1] * skip) * g0_ref[sg, t1]).astype(o_ref.dtype)


def _kfc_call(g1r, g1i, vg4, g04, skip, n1, n2, d, l3, l4):
    a, b = _split_radix(n1)
    nb, t1_out = vg4.shape[:2]
    r, w = KRON_R, min(KRON_W, d)
    slab = pl.BlockSpec((1, n1, r, w), lambda j, k: (0, 0, j, k))
    nat = pl.BlockSpec((nb, t1_out, r, w), lambda j, k: (0, 0, j, k))
    return pl.pallas_call(
        functools.partial(_kfc_kernel, a=a, b=b, a_out=t1_out // b), grid=(n2 // r, d // w),
        in_specs=[pl.BlockSpec(l3.shape, lambda j, k: (0, 0, 0), pipeline_mode=pl.Buffered(1)),
                  _const_spec(l4.shape), slab, slab, nat, nat, pl.BlockSpec((1, w), lambda j, k: (0, k))],
        out_specs=nat, out_shape=jax.ShapeDtypeStruct(vg4.shape, BF16),
        compiler_params=_cp("parallel", "parallel"), name="kfc",
    )(l3, l4, g1r, g1i, vg4, g04, skip.reshape(1, d))


def _hy_kb_kernel(fm_ref, fr_ref, fi_ref, sc_ref, kr_ref, ki_ref, *, nf):
    fm = fm_ref[...]
    n2 = fr_ref.shape[1]
    sc = sc_ref[...]
    for s in range(nf):
        kk = _dot(fm, jnp.concatenate([fr_ref[s], fi_ref[s]], axis=0))
        kr_ref[s] = kk[:n2] * sc
        ki_ref[s] = kk[n2:] * sc


def _hy_b_kernel(fm_ref, fmc_ref, xr_ref, xi_ref, kr_ref, ki_ref, twc_ref, tws_ref,
                 or_ref, oi_ref, *, nf, d):
    fm = fm_ref[...]
    fmc = fmc_ref[...]
    n2 = xr_ref.shape[1]
    for s in range(nf):
        x = _dot(fm, jnp.concatenate([xr_ref[s], xi_ref[s]], axis=0))
        xr, xi = x[:n2], x[n2:]
        kr, ki = kr_ref[s], ki_ref[s]
        yr = (xr * kr - xi * ki).astype(BF16)
        yi = (xr * ki + xi * kr).astype(BF16)
        g = _dot(fmc, jnp.concatenate([yr, yi], axis=0))
        gr, gi = g[:n2], g[n2:]
        c = jnp.concatenate([twc_ref[s]] * (d // LANE), axis=1)
        sn = jnp.concatenate([tws_ref[s]] * (d // LANE), axis=1)
        or_ref[s] = (gr * c - gi * sn).astype(or_ref.dtype)
        oi_ref[s] = (gi * c + gr * sn).astype(oi_ref.dtype)


def _block_c(cs, sn, sign):
    return np.block([[cs, -sign * sn], [sign * sn, cs]])


def _hy_conv_long(vg, g0, k, nrm, skip, b, n, d):
    n2 = FFT_N2
    nn = 2 * n
    n1 = nn // n2
    rows = n // n2
    a, _ = _split_radix(n1)
    la1, la2 = _slab_dft_mats(n1, a // 2, 0, -1)
    lk1, lk2 = _slab_dft_mats(n1, a, 0, -1, real_in=True)
    lc3, lc4 = _slab_dft_mats(n1, 0, a // 2, +1)
    cs2, sn2 = _dft_cs(n2, n2, n2)
    fm_b = jnp.asarray(_block_c(cs2, sn2, -1.0), BF16)
    fm_bc = jnp.asarray(_block_c(cs2, sn2, 1.0), BF16)
    twc, tws = _twiddle_tables(n1, n2, lead_t2=False)

    kfr, kfi = _kfa_call([(k.reshape(1, n1, n2, d), 0)], 1, n1, n2, d, lk1, lk2, twc, tws)
    nf = 4 if n1 % 4 == 0 else 1
    scale = (1.0 / (nrm[0:1, :] * nn))
    slab = pl.BlockSpec((nf, n2, d), lambda j: (j, 0, 0))
    shp3 = (n1, n2, d)
    khr, khi = pl.pallas_call(
        functools.partial(_hy_kb_kernel, nf=nf), grid=(n1 // nf,),
        in_specs=[_const_spec(fm_b.shape), slab, slab, _const_spec((1, d))],
        out_specs=[slab, slab],
        out_shape=[jax.ShapeDtypeStruct(shp3, F32)] * 2,
        compiler_params=_cp("parallel"), name="hy_kb",
    )(fm_b, kfr.reshape(shp3), kfi.reshape(shp3), scale)

    vg4 = vg.reshape(b, rows, n2, d)
    g04 = g0.reshape(b, rows, n2, d)
    x1r, x1i = _kfa_call([(vg4, 0), (vg4, 1)], 1, n1, n2, d, la1, la2, twc, tws)
    tw_slab = pl.BlockSpec((nf, n2, LANE), lambda j: (j, 0, 0))
    g1r, g1i = pl.pallas_call(
        functools.partial(_hy_b_kernel, nf=nf, d=d), grid=(n1 // nf,),
        in_specs=[_const_spec(fm_b.shape), _const_spec(fm_bc.shape), slab, slab, slab, slab,
                  tw_slab, tw_slab],
        out_specs=[slab, slab],
        out_shape=[jax.ShapeDtypeStruct(shp3, BF16)] * 2,
        compiler_params=_cp("parallel"), name="hy_b",
    )(fm_b, fm_bc, x1r.reshape(shp3), x1i.reshape(shp3), khr, khi, twc, tws)
    out = _kfc_call(g1r.reshape(1, n1, n2, d), g1i.reshape(1, n1, n2, d), vg4, g04, skip, n1, n2, d, lc3, lc4)
    return out.reshape(b * n, d)


def _hy_short_kernel(fa_ref, fk_ref, fi_ref, vg_ref, g0_ref, k_ref, nrm_ref, skip_ref, o_ref, *, n):
    z = jnp.concatenate([vg_ref[0], vg_ref[1]], axis=0)
    x = _dot_hi(fa_ref[...], z)
    kk = _dot_hi(fk_ref[...], k_ref[...])
    nn = 2 * n
    sc = 1.0 / (nrm_ref[0:1, :] * nn)
    xr, xi = x[:nn], x[nn:]
    kr, ki = kk[:nn] * sc, kk[nn:] * sc
    y = _dot_hi(fi_ref[...], jnp.concatenate([xr * kr - xi * ki, xr * ki + xi * kr], axis=0))
    skip = skip_ref[...]
    for bi in range(2):
        o_ref[bi] = ((y[bi * n:(bi + 1) * n] + vg_ref[bi] * skip) * g0_ref[bi]).astype(o_ref.dtype)


def _hy_conv_short(vg, g0, k, nrm, skip, b, n, d):
    nn = 2 * n
    cs, sn = _dft_cs(nn, n, nn)
    fa = jnp.asarray(_block_c(cs, sn, -1.0), F32)
    csk, snk = _dft_cs(nn, nn, nn)
    fk = jnp.asarray(np.concatenate([csk, -snk], axis=0), F32)
    csi, sni = _dft_cs(n, nn, nn)
    fi = jnp.asarray(_block_c(csi, sni, 1.0), F32)
    cb = 256
    col3 = pl.BlockSpec((b, n, cb), lambda j: (0, 0, j))
    vec = pl.BlockSpec((1, cb), lambda j: (0, j))
    out = pl.pallas_call(
        functools.partial(_hy_short_kernel, n=n), grid=(d // cb,),
        in_specs=[_const_spec(fa.shape), _const_spec(fk.shape), _const_spec(fi.shape), col3, col3,
                  pl.BlockSpec((nn, cb), lambda j: (0, j)), pl.BlockSpec((8, cb), lambda j: (0, j)), vec],
        out_specs=col3, out_shape=jax.ShapeDtypeStruct((b, n, d), BF16),
        compiler_params=_cp("parallel"), name="hy_short",
    )(fa, fk, fi, vg.reshape(b, n, d), g0.reshape(b, n, d), k, nrm, skip.reshape(1, d))
    return out.reshape(b * n, d)


def _hyena_layer(xl, xc, mods_l, mods_c, pre_g, fin, w_in, b_in, conv_w, conv_b, filt, skip,
                 w_out, b_out, b, n, c):
    d = xl.shape[-1]
    w_in_b = w_in.astype(BF16)
    w_out_b = w_out.astype(BF16)
    tm = min(512, n)
    g0, vg = _hy_in_call(xl, mods_l, pre_g, w_in_b, b_in, conv_w, conv_b, tm, n)
    k, nrm = _hy_filter_call(n, d, *filt)
    u_out = _hy_conv_long(vg, g0, k, nrm, skip[0], b, n, d)
    xl = fin(xl, u_out, mods_l, w_out_b, b_out, tm, n)

    g0c, vgc = _hy_in_call(xc, mods_c, pre_g, w_in_b, b_in, conv_w, conv_b, c, c)
    kc, nrmc = _hy_filter_call(c, d, *filt)
    u_out_c = _hy_conv_short(vgc, g0c, kc, nrmc, skip[0], b, c, d)
    xc = fin(xc, u_out_c, mods_c, w_out_b, b_out, c, c)
    return xl, xc


def _s5_operators(lam_re, lam_im, log_dt, b_re, b_im, c_re, c_im, d_skip):
    t = S5_T
    g, ns = lam_re.shape[1], lam_re.shape[2]
    gc = b_re.shape[-1]
    gl = LANE // gc
    nblk = g // gl
    lam = lax.complex(lam_re, lam_im)
    dt = jnp.exp(log_dt)[..., None]
    lam_bar = jnp.exp(lam * dt)
    b_bar = ((lam_bar - 1.0) / lam)[..., None] * lax.complex(b_re, b_im)
    c_mat = lax.complex(c_re, c_im)
    pw = jnp.arange(t + 1, dtype=F32)
    lam_pw = jnp.exp((lam * dt)[None] * pw[:, None, None, None])
    hp = HIGHEST
    kern = jnp.einsum('dgcn,tdgn,dgne->dgtce', c_mat, lam_pw[:t], b_bar, precision=hp).real
    dsk = d_skip.reshape(g, gc)
    kt = jnp.swapaxes(kern, -1, -2)
    centre = kt[0][:, 0] + kt[1][:, 0] + jnp.eye(gc, dtype=F32)[None] * dsk[:, :, None]
    ks = jnp.concatenate([kt[1][:, 1:][:, ::-1], centre[:, None], kt[0][:, 1:]], axis=1)
    eye_gl = jnp.eye(gl, dtype=F32)
    ks = ks.reshape(nblk, gl, 2 * t - 1, gc, gc)
    d_tab = jnp.einsum('bglec,gh->blgehc', ks, eye_gl).reshape(nblk, 2 * t - 1, LANE, LANE)

    def compact(z_ri):
        z = jnp.transpose(z_ri, (3, 2, 1, 4, 0, 5)).reshape(nblk, gl, t, 2, gc, 2 * ns)
        return jnp.transpose(z, (0, 2, 1, 3, 4, 5))

    pf = lam_pw[:t][::-1][:, 0, :, :, None] * b_bar[0][None]
    pb = lam_pw[:t][:, 1, :, :, None] * b_bar[1][None]
    pcat = jnp.swapaxes(jnp.stack([pf, pb], axis=0), -1, -2)
    p_tab = compact(jnp.stack([pcat.real, pcat.imag], axis=0))
    qf = c_mat[0][None] * lam_pw[1:t + 1, 0][:, :, None, :]
    qb = c_mat[1][None] * lam_pw[1:t + 1][::-1][:, 1][:, :, None, :]
    qcat = jnp.stack([qf, qb], axis=0)
    q_tab = compact(jnp.stack([qcat.real, -qcat.imag], axis=0))

    a = lam_pw[t]
    a1 = jnp.concatenate([a.real, a.real], axis=-1).reshape(2, g * 2 * ns)
    a2 = jnp.concatenate([-a.imag, a.imag], axis=-1).reshape(2, g * 2 * ns)
    m_op, p_op, q_op = _s5_expand(d_tab, p_tab, q_tab)
    return m_op, p_op, q_op, a1, a2


def _s5_m_kernel(d_ref, o_ref, *, t):
    s = pl.program_id(1)
    for tt in range(t):
        o_ref[0, :, tt * LANE:(tt + 1) * LANE] = d_ref[0, tt - s + t - 1].astype(o_ref.dtype)


def _s5_pq_kernel(c_ref, o_ref, *, transpose):
    gl, nd, gc, w = c_ref.shape[2:]
    rows = [jnp.concatenate([c_ref[0, 0, g, dd] if (g == h) else jnp.zeros((gc, w), F32)
                             for dd in range(nd) for h in range(gl)], axis=1) for g in range(gl)]
    blk = jnp.concatenate(rows, axis=0)
    o_ref[0] = (blk.T if transpose else blk).astype(o_ref.dtype)


def _s5_expand(d_tab, p_tab, q_tab):
    nblk, nlag = d_tab.shape[:2]
    t = (nlag + 1) // 2
    _, _, gl, nd, gc, w = p_tab.shape
    ncol = nd * gl * w
    m_op = pl.pallas_call(
        functools.partial(_s5_m_kernel, t=t), grid=(nblk, t),
        in_specs=[pl.BlockSpec((1, nlag, LANE, LANE), lambda b, s: (b, 0, 0, 0))],
        out_specs=pl.BlockSpec((1, LANE, t * LANE), lambda b, s: (b, s, 0)),
        out_shape=jax.ShapeDtypeStruct((nblk, t * LANE, t * LANE), BF16),
        compiler_params=_cp("parallel", "parallel"), name="s5_m_op",
    )(d_tab)
    tab_spec = pl.BlockSpec((1, 1, gl, nd, gc, w), lambda b, j: (b, j, 0, 0, 0, 0))
    p_op = pl.pallas_call(
        functools.partial(_s5_pq_kernel, transpose=False), grid=(nblk, t),
        in_specs=[tab_spec],
        out_specs=pl.BlockSpec((1, LANE, ncol), lambda b, j: (b, j, 0)),
        out_shape=jax.ShapeDtypeStruct((nblk, t * LANE, ncol), BF16),
        compiler_params=_cp("parallel", "parallel"), name="s5_p_op",
    )(p_tab)
    q_op = pl.pallas_call(
        functools.partial(_s5_pq_kernel, transpose=True), grid=(nblk, t),
        in_specs=[tab_spec],
        out_specs=pl.BlockSpec((1, ncol, LANE), lambda b, j: (b, 0, j)),
        out_shape=jax.ShapeDtypeStruct((nblk, ncol, t * LANE), BF16),
        compiler_params=_cp("parallel", "parallel"), name="s5_q_op",
    )(q_tab)
    return m_op, p_op, q_op


def _s5_sum_kernel(*refs, t):
    u_refs = refs[:t]
    p_ref, o_ref = refs[t:]
    u = jnp.concatenate([r[...] for r in u_refs], axis=1)
    o_ref[...] = _dot(u, p_ref[0])


def _s5_sum_call(h, p_op, rb):
    rows = h.shape[0]
    t = S5_T
    nblk = p_op.shape[0]
    ns2 = p_op.shape[2]
    u_specs = [pl.BlockSpec((rb, LANE), lambda gb, r, s=s: (r, s * nblk + gb)) for s in range(t)]
    return pl.pallas_call(
        functools.partial(_s5_sum_kernel, t=t), grid=(nblk, rows // rb),
        in_specs=u_specs + [pl.BlockSpec((1,) + p_op.shape[1:], lambda gb, r: (gb, 0, 0))],
        out_specs=pl.BlockSpec((rb, ns2), lambda gb, r: (r, gb)),
        out_shape=jax.ShapeDtypeStruct((rows, nblk * ns2), F32),
        compiler_params=_cp("parallel", "parallel"), name="s5_sum",
    )(*([h] * t), p_op)


def _s5_rec_kernel(sf_ref, sb_ref, a1f_ref, a2f_ref, a1b_ref, a2b_ref, hf0_ref, hb0_ref,
                   hf_ref, hb_ref, ff_ref, fb_ref, *, kb):
    a1f, a2f, a1b, a2b = a1f_ref[...], a2f_ref[...], a1b_ref[...], a2b_ref[...]

    @pl.when(pl.program_id(0) == 0)
    def _():
        ff_ref[...] = hf0_ref[...]
        fb_ref[...] = hb0_ref[...]

    def body(i, carry):
        hf, hb = carry
        k = kb - 1 - i
        hf_ref[i] = hf
        hb_ref[k] = hb
        hf = a1f * hf + a2f * pltpu.roll(hf, 64, 1) + sf_ref[i]
        hb = a1b * hb + a2b * pltpu.roll(hb, 64, 1) + sb_ref[k]
        return hf, hb

    hf, hb = lax.fori_loop(0, kb, body, (ff_ref[...], fb_ref[...]))
    ff_ref[...] = hf
    fb_ref[...] = hb


def _s5_rec_call(sf, sb, a1, a2, hf0, hb0):
    nk, r, _ = sf.shape
    rep = r * LANE // a1.shape[1]
    tile = lambda v: jnp.tile(v.reshape(1, -1), (1, rep)).reshape(r, LANE)
    kb = min(32, nk)
    nb = nk // kb
    fwd3 = pl.BlockSpec((kb, r, LANE), lambda i: (i, 0, 0))
    bwd3 = pl.BlockSpec((kb, r, LANE), lambda i: (nb - 1 - i, 0, 0))
    full2 = _const_spec((r, LANE))
    return pl.pallas_call(
        functools.partial(_s5_rec_kernel, kb=kb), grid=(nb,),
        in_specs=[fwd3, bwd3, full2, full2, full2, full2, full2, full2],
        out_specs=[fwd3, bwd3, full2, full2],
        out_shape=[jax.ShapeDtypeStruct((nk, r, LANE), F32)] * 2 + [jax.ShapeDtypeStruct((r, LANE), F32)] * 2,
        compiler_params=_cp("arbitrary"), name="s5_rec",
    )(sf, sb, tile(a1[0]), tile(a2[0]), tile(a1[1]), tile(a2[1]), hf0, hb0)


def _s5_out_kernel(*refs, t):
    u_refs = refs[:t]
    h_ref, m_ref, q_ref, o_ref = refs[t:]
    u = jnp.concatenate([r[...] for r in u_refs], axis=1)
    y = _dot(u, m_ref[0]) + _dot(h_ref[...].astype(BF16), q_ref[0])
    o_ref[0] = (0.5 * y * (1.0 + lax.erf(y * (2.0 ** -0.5)))).astype(o_ref.dtype)


def _s5_out_call(h, hcat, m_op, q_op, rb):
    rows = h.shape[0]
    t = S5_T
    nblk = m_op.shape[0]
    ns2 = q_op.shape[1]
    u_specs = [pl.BlockSpec((rb, LANE), lambda gb, r, s=s: (r, s * nblk + gb)) for s in range(t)]
    return pl.pallas_call(
        functools.partial(_s5_out_kernel, t=t), grid=(nblk, rows // rb),
        in_specs=u_specs + [pl.BlockSpec((rb, ns2), lambda gb, r: (r, gb)),
                            pl.BlockSpec((1,) + m_op.shape[1:], lambda gb, r: (gb, 0, 0)),
                            pl.BlockSpec((1,) + q_op.shape[1:], lambda gb, r: (gb, 0, 0))],
        out_specs=pl.BlockSpec((1, rb, t * LANE), lambda gb, r: (gb, r, 0)),
        out_shape=jax.ShapeDtypeStruct((nblk, rows, t * LANE), BF16),
        compiler_params=_cp("parallel", "parallel"), name="s5_out",
    )(*([h] * t), hcat, m_op, q_op)


def _s5_layer(xl, xc, mods_l, mods_c, pre_g, fin, lam_re, lam_im, log_dt, b_re, b_im, c_re, c_im,
              d_skip, w_glu, b_glu, b, n, c):
    d = xl.shape[-1]
    t = S5_T
    m_op, p_op, q_op, a1, a2 = _s5_operators(lam_re, lam_im, log_dt, b_re, b_im, c_re, c_im, d_skip)
    nblk = m_op.shape[0]
    half = a1.shape[1]
    hl = _normmod_call(xl, mods_l, pre_g, min(512, n), n).reshape(b * n // t, t * d)
    hc = _normmod_call(xc, mods_c, pre_g, c, c).reshape(b * c // t, t * d)

    def summaries(h, nk):
        s = _s5_sum_call(h, p_op, min(512, h.shape[0]))
        s = s.reshape(b, nk, nblk, 2, half // nblk)
        s = jnp.transpose(s, (3, 1, 0, 2, 4)).reshape(2, nk, b * half // LANE, LANE)
        return s[0], s[1]

    zeros = jnp.zeros((b * half // LANE, LANE), F32)
    sfc, sbc = summaries(hc, c // t)
    _, _, hf0, hb0 = _s5_rec_call(sfc, sbc, a1, a2, zeros, zeros)
    nk = n // t
    sfl, sbl = summaries(hl, nk)
    hf, hb, _, _ = _s5_rec_call(sfl, sbl, a1, a2, hf0, hb0)
    hcat = jnp.stack([hf.reshape(nk, b, nblk, half // nblk), hb.reshape(nk, b, nblk, half // nblk)], axis=3)
    hcat = jnp.transpose(hcat, (1, 0, 2, 3, 4)).reshape(b * nk, nblk * 2 * (half // nblk))
    rb = min(512, b * nk)
    gact = _s5_out_call(hl, hcat, m_op, q_op, rb)
    g_nat = jnp.transpose(gact.reshape(nblk, b * nk, t, LANE), (1, 2, 0, 3)).reshape(b * n, d)
    return fin(xl, g_nat, mods_l, w_glu.astype(BF16), b_glu, min(512, n), n, glu=True)


def _fn_pre_kernel(x_ref, mod_ref, g_ref, cs_ref, a_ref, b_ref):
    h = _normmod(x_ref[...], g_ref[...], mod_ref[0, 0:1, :], mod_ref[0, 1:2, :]).astype(BF16)
    cs = cs_ref[...]
    gc = FNET_GC
    ab = [_dot(h[:, k * gc:(k + 1) * gc], cs) for k in range(h.shape[1] // gc)]
    a_ref[...] = jnp.concatenate([z[:, :gc] for z in ab], axis=1).astype(a_ref.dtype)
    b_ref[...] = jnp.concatenate([z[:, gc:] for z in ab], axis=1).astype(b_ref.dtype)


def _fn_c_kernel(l5_ref, l6_ref, xr_ref, xi_ref, o_ref, *, nh):
    r = KRON_R
    n2 = nh * r
    l5 = l5_ref[...]
    y5 = [_dot(l5, jnp.concatenate([xr_ref[0, f], xi_ref[0, f]], axis=0)).astype(BF16) for f in range(r)]
    l6 = l6_ref[...]
    for p in range(nh):
        x = jnp.concatenate([y5[f][p * r:(p + 1) * r] for f in range(r)]
                            + [y5[f][n2 + p * r:n2 + (p + 1) * r] for f in range(r)], axis=0)
        out = _dot(l6, x)
        for q in range(r):
            o_ref[0, p + nh * q] = out[q * r:(q + 1) * r].astype(o_ref.dtype)


def _fnet_layer(xl, mods_l, pre_g, fin, w_o, b_o, b, n, d):
    n2 = FFT_N2
    n1 = n // n2
    gc = FNET_GC
    cc, sc = _dft_cs(gc, gc, gc)
    cs = jnp.asarray(np.concatenate([cc, sc], axis=1) / np.sqrt(gc), BF16)
    tm = min(512, n)
    x_spec, mod_spec = _row_specs(tm, d, n // tm)
    ab = jax.ShapeDtypeStruct((b * n, d), BF16)
    am, bm = pl.pallas_call(
        _fn_pre_kernel, grid=(b * n // tm,),
        in_specs=[x_spec, mod_spec, _const_spec((1, d)), _const_spec(cs.shape)],
        out_specs=[x_spec, x_spec], out_shape=[ab, ab],
        compiler_params=_cp("parallel"), name="fn_pre",
    )(xl, mods_l, pre_g.reshape(1, d), cs)

    a, _ = _split_radix(n1)
    l1, l2 = _slab_dft_mats(n1, a, 0, -1, neg_im=True, scale=1.0 / np.sqrt(n))
    twc, tws = _twiddle_tables(n1, n2, lead_t2=False)
    a4, b4 = am.reshape(b, n1, n2, d), bm.reshape(b, n1, n2, d)
    xr, xi = _kfa_call([(a4, None), (b4, None)], b, n1, n2, d, l1, l2, twc, tws)

    r, w = KRON_R, min(KRON_W, d)
    nh = n2 // r
    assert n1 % r == 0 and n2 % r == 0
    m5 = np.zeros((n2, n2), np.complex128)
    m6 = np.zeros((r * r, r * r), np.complex128)
    for s in range(r):
        for p in range(nh):
            for h in range(nh):
                m5[p * r + s, h * r + s] = np.exp(-2j * np.pi * (p * h / nh + p * s / n2))
        for q in range(r):
            for f in range(r):
                m6[q * r + f, f * r + s] = np.exp(-2j * np.pi * q * s / r)
    l5 = jnp.asarray(_cblock(m5), BF16)
    l6 = jnp.asarray(np.concatenate([m6.real, -m6.imag], axis=1), BF16)
    grp = pl.BlockSpec((1, r, n2, w), lambda bi, fh, k: (bi, fh, 0, k))
    y = pl.pallas_call(
        functools.partial(_fn_c_kernel, nh=nh), grid=(b, n1 // r, d // w),
        in_specs=[_const_spec(l5.shape), _const_spec(l6.shape), grp, grp],
        out_specs=pl.BlockSpec((1, n2, None, r, w), lambda bi, fh, k: (bi, 0, fh, 0, k)),
        out_shape=jax.ShapeDtypeStruct((b, n2, n1 // r, r, d), BF16),
        compiler_params=_cp("parallel", "parallel", "parallel"), name="fn_c",
    )(l5, l6, xr, xi)
    return fin(xl, y.reshape(b * n, d), mods_l, w_o.astype(BF16), b_o, tm, n)


def kernel(x, c, ctx, c_ctx, mod_w, mod_b, mix_pre_g, mix_post_g, ffn_pre_g, ffn_post_g, ffn_w13, ffn_w2,
           mla_w_in, mla_q_norm_g, mla_kv_norm_g, mla_w_uq, mla_w_ukv, mla_w_o,
           hy_w_in, hy_b_in, hy_conv_w, hy_conv_b, hy_f_w1, hy_f_b1, hy_f_freq, hy_f_w2, hy_f_b2, hy_f_w3,
           hy_skip, hy_w_out, hy_b_out,
           s5_lambda_re, s5_lambda_im, s5_log_dt, s5_b_re, s5_b_im, s5_c_re, s5_c_im, s5_d, s5_w_glu, s5_b_glu,
           fn_w_o, fn_b_o):
    b, n, d = x.shape
    cl = ctx.shape[1]
    depth = mod_w.shape[0]
    assert b == 2 and depth == 4, "two batches ride one complex transform; one layer per mixer"
    mods = _mods(c, c_ctx, mod_w, mod_b)
    xl = x.reshape(b * n, d)
    xc = ctx.reshape(b * cl, d)
    tm = min(512, n)

    def finisher(i):
        w13 = ffn_w13[i].astype(BF16)
        w2 = ffn_w2[i].astype(BF16)

        def fin(x_, y_, mods_, wm, bm, tm_, rows_per_batch, glu=False):
            return _mix_ffn_call(x_, y_, mods_, wm, bm, mix_post_g[i], ffn_pre_g[i], ffn_post_g[i], w13, w2,
                                 tm_, rows_per_batch, glu)
        return fin

    def mods_c(i):
        return jnp.broadcast_to(mods[i, 2:3], (b, 8, d))

    xl, xc = _mla_layer(xl, xc, mods[0, 0:2], mods_c(0), mix_pre_g[0], finisher(0), mla_w_in[0],
                        mla_q_norm_g[0], mla_kv_norm_g[0], mla_w_uq[0], mla_w_ukv[0], mla_w_o[0], b, n, cl)
    filt = (hy_f_w1[0], hy_f_b1[0], hy_f_freq[0], hy_f_w2[0], hy_f_b2[0], hy_f_w3[0])
    xl, xc = _hyena_layer(xl, xc, mods[1, 0:2], mods_c(1), mix_pre_g[1], finisher(1), hy_w_in[0], hy_b_in[0],
                          hy_conv_w[0], hy_conv_b[0], filt, hy_skip[0], hy_w_out[0], hy_b_out[0], b, n, cl)
    xl = _s5_layer(xl, xc, mods[2, 0:2], mods_c(2), mix_pre_g[2], finisher(2), s5_lambda_re[0],
                   s5_lambda_im[0], s5_log_dt[0], s5_b_re[0], s5_b_im[0], s5_c_re[0], s5_c_im[0], s5_d[0],
                   s5_w_glu[0], s5_b_glu[0], b, n, cl)
    xl = _fnet_layer(xl, mods[3, 0:2], mix_pre_g[3], finisher(3), fn_w_o[0], fn_b_o[0], b, n, d)
    return xl.reshape(b, n, d)
```

```python
import functools
import math

import numpy as np
import jax
import jax.numpy as jnp
from jax import lax
from jax.experimental import pallas as pl
from jax.experimental.pallas import tpu as pltpu

F32 = jnp.float32
BF16 = jnp.bfloat16
NORM_EPS = 1e-6
LANE = 128
VMEM_LIMIT = 56 * 1024 * 1024
HIGHEST = lax.Precision.HIGHEST

GRID_W = 64
ROPE_THETA = 10000.0
MLA_HEADS = 8
MLA_NOPE = 128
MLA_ROPE = 64
MLA_V = 128
MLA_VT = MLA_V + 16
HYENA_BANDS = 16
HYENA_TARGET = 1e-2
HYENA_FAST = 0.3
HYENA_SLOW = 1.5
S5_GROUP = 16
S5_STATE = 64
S5_T = 16
FNET_GC = 128
FFT_N2 = 128


def _cp(*sem):
    return pltpu.CompilerParams(dimension_semantics=sem, vmem_limit_bytes=VMEM_LIMIT)


def _dot(a, b):
    return jnp.dot(a, b, preferred_element_type=F32)


def _dot_hi(a, b):
    return jnp.dot(a, b, preferred_element_type=F32, precision=HIGHEST)


def _rms(x, g):
    ms = jnp.mean(x * x, axis=-1, keepdims=True)
    return x * lax.rsqrt(ms + NORM_EPS) * g


def _normmod(x, g, shift, scale):
    return _rms(x, g) * (1.0 + scale) + shift


def _const_spec(shape):
    nd = len(shape)
    return pl.BlockSpec(shape, lambda *_: (0,) * nd)


def _mods_kernel(st_ref, w_ref, b_ref, o_ref):
    st = st_ref[...]
    st = st * jax.nn.sigmoid(st)
    w = w_ref[0]
    rows = [jnp.sum(st[:, r:r + 1] * w, axis=0, keepdims=True) for r in range(3)]
    rows.append(jnp.zeros((5, w.shape[1]), F32))
    o_ref[0] = jnp.concatenate(rows, axis=0) + b_ref[0]


def _mods(c, c_ctx, mod_w, mod_b):
    depth, d, n6 = mod_w.shape
    st = jnp.zeros((d, 8), F32).at[:, 0:2].set(c.T).at[:, 2].set(c_ctx)
    tn = 1024
    out = pl.pallas_call(
        _mods_kernel,
        grid=(depth, n6 // tn),
        in_specs=[_const_spec((d, 8)),
                  pl.BlockSpec((1, d, tn), lambda i, j: (i, 0, j)),
                  pl.BlockSpec((1, 1, tn), lambda i, j: (i, 0, j))],
        out_specs=pl.BlockSpec((1, 8, tn), lambda i, j: (i, 0, j)),
        out_shape=jax.ShapeDtypeStruct((depth, 8, n6), F32),
        compiler_params=_cp("parallel", "parallel"),
        name="mods",
    )(st, mod_w, mod_b.reshape(depth, 1, n6))
    m = out[:, :3].reshape(depth, 3, n6 // d, d)
    return jnp.pad(m, ((0, 0), (0, 0), (0, 8 - n6 // d), (0, 0)))


def _row_specs(tm, d, tpb):
    x_spec = pl.BlockSpec((tm, d), lambda i: (i, 0))
    mod_spec = pl.BlockSpec((1, 8, d), lambda i: (i // tpb, 0, 0))
    return x_spec, mod_spec


def _normmod_kernel(x_ref, mod_ref, g_ref, o_ref):
    h = _normmod(x_ref[...], g_ref[...], mod_ref[0, 0:1, :], mod_ref[0, 1:2, :])
    o_ref[...] = h.astype(o_ref.dtype)


def _normmod_call(x, mods, g, tm, rows_per_batch):
    m, d = x.shape
    x_spec, mod_spec = _row_specs(tm, d, rows_per_batch // tm)
    return pl.pallas_call(
        _normmod_kernel, grid=(m // tm,),
        in_specs=[x_spec, mod_spec, _const_spec((1, d))],
        out_specs=x_spec, out_shape=jax.ShapeDtypeStruct((m, d), BF16),
        compiler_params=_cp("parallel"), name="normmod",
    )(x, mods, g.reshape(1, d))


def _mix_ffn_kernel(x_ref, y_ref, mod_ref, wm_ref, bm_ref, gm_ref, pre_ref, post_ref, w13_ref, w2_ref, o_ref,
                    *, f, fc, glu):
    z = _dot(y_ref[...].astype(BF16), wm_ref[...]) + bm_ref[...]
    if glu:
        d = o_ref.shape[-1]
        z = z[:, :d] * jax.nn.sigmoid(z[:, d:])
    x = x_ref[...] + mod_ref[0, 2:3, :] * _rms(z, gm_ref[...])
    h = _normmod(x, pre_ref[...], mod_ref[0, 3:4, :], mod_ref[0, 4:5, :]).astype(BF16)
    acc = None
    for c in range(f // fc):
        a = _dot(h, w13_ref[:, c * fc:(c + 1) * fc])
        b = _dot(h, w13_ref[:, f + c * fc:f + (c + 1) * fc])
        gact = (a * jax.nn.sigmoid(a) * b).astype(BF16)
        part = _dot(gact, w2_ref[c * fc:(c + 1) * fc, :])
        acc = part if acc is None else acc + part
    o_ref[...] = x + mod_ref[0, 5:6, :] * _rms(acc, post_ref[...])


def _mix_ffn_call(x, y, mods, wm, bm, mix_post_g, ffn_pre_g, ffn_post_g, w13, w2, tm, rows_per_batch, glu=False):
    m, d = x.shape
    k, nm = wm.shape
    f = w2.shape[0]
    fc = f // 2 if (f // 2) % LANE == 0 else f
    x_spec, mod_spec = _row_specs(tm, d, rows_per_batch // tm)
    once = lambda shape: pl.BlockSpec(shape, lambda i: (0, 0), pipeline_mode=pl.Buffered(1))
    return pl.pallas_call(
        functools.partial(_mix_ffn_kernel, f=f, fc=fc, glu=glu), grid=(m // tm,),
        in_specs=[x_spec, pl.BlockSpec((tm, k), lambda i: (i, 0)), mod_spec,
                  once((k, nm)), _const_spec((1, nm)), _const_spec((1, d)), _const_spec((1, d)),
                  _const_spec((1, d)), once((d, 2 * f)), once((f, d))],
        out_specs=x_spec, out_shape=jax.ShapeDtypeStruct((m, d), F32),
        compiler_params=_cp("parallel"), name="mix_ffn",
    )(x, y, mods, wm, bm.reshape(1, nm), mix_post_g.reshape(1, d), ffn_pre_g.reshape(1, d),
      ffn_post_g.reshape(1, d), w13, w2)


def _mla_in_kernel(x_ref, mod_ref, g_ref, w_ref, qg_ref, kvg_ref, t1_ref, t2_ref,
                   qn_ref, cn_ref, kr_ref, *, ql, kvl):
    h = _normmod(x_ref[...], g_ref[...], mod_ref[0, 0:1, :], mod_ref[0, 1:2, :])
    z = _dot(h.astype(BF16), w_ref[...])
    qn_ref[...] = _rms(z[:, :ql], qg_ref[...]).astype(BF16)
    cn_ref[...] = _rms(z[:, ql:ql + kvl], kvg_ref[...]).astype(BF16)
    pair = z[:, ql + kvl:]
    kr_ref[...] = (pair * t1_ref[...] + pltpu.roll(pair, 64, 1) * t2_ref[...]).astype(BF16)


def _mla_q_kernel(qn_ref, w_ref, t1_ref, t2_ref, q_ref, *, qscale):
    z = _dot(qn_ref[...], w_ref[...])
    t1 = t1_ref[...]
    t2 = t2_ref[...]
    for h in range(MLA_HEADS):
        base = h * 256
        pair = z[:, base + 128:base + 256]
        rp = pair * t1 + pltpu.roll(pair, 64, 1) * t2
        qcat = jnp.concatenate([z[:, base:base + 128], rp], axis=1) * qscale
        q_ref[0, h] = qcat.T.astype(BF16)


def _mla_kv_kernel(cn_ref, kr_ref, w_ref, k_ref, vt_ref):
    z = _dot(cn_ref[...], w_ref[...])
    kr = kr_ref[...]
    tk = kr.shape[0]
    ones_blk = (lax.broadcasted_iota(jnp.int32, (MLA_VT - MLA_V, tk), 0) == 0).astype(BF16)
    for h in range(MLA_HEADS):
        base = h * 256
        k_ref[0, h, 0, :, 0:128] = z[:, base:base + 128].astype(BF16)
        k_ref[0, h, 0, :, 128:256] = kr
        vt_ref[0, h, 0, 0:MLA_V, :] = z[:, base + 128:base + 256].T.astype(BF16)
        vt_ref[0, h, 0, MLA_V:MLA_VT, :] = ones_blk


def _flash_kernel(q_ref, kc_ref, vc_ref, *rest, n_lat):
    if n_lat:
        kl_ref, vl_ref, o_ref, s_scr, acc_scr = rest
    else:
        o_ref, acc_scr = rest
    qt = q_ref[0, 0]

    def qk(k, slot):
        s = _dot(k, qt)
        s_scr[slot] = s
        return jnp.max(s, axis=0, keepdims=True)

    def sm_pv(slot, vt, m, mx):
        m_new = jnp.maximum(m, mx)
        alpha = jnp.exp2(m - m_new)
        p = jnp.exp2(s_scr[slot] - m_new).astype(BF16)
        acc_scr[...] = alpha * acc_scr[...] + _dot(vt, p)
        return m_new

    sc = _dot(kc_ref[0, 0, 0], qt)
    m = jnp.max(sc, axis=0, keepdims=True)
    acc_scr[...] = _dot(vc_ref[0, 0, 0], jnp.exp2(sc - m).astype(BF16))
    if n_lat:
        mx = qk(kl_ref[0, 0, 0], 0)

        def body(i, carry):
            m, mx0 = carry
            c = 2 * i
            mx1 = qk(kl_ref[0, 0, c + 1], 1)
            m = sm_pv(0, vl_ref[0, 0, c], m, mx0)
            mx0 = qk(kl_ref[0, 0, jnp.minimum(c + 2, n_lat - 1)], 0)
            m = sm_pv(1, vl_ref[0, 0, c + 1], m, mx1)
            return m, mx0

        lax.fori_loop(0, n_lat // 2, body, (m, mx))
    acc = acc_scr[...]
    o_ref[0] = (acc[0:MLA_V] / acc[MLA_V:MLA_V + 1]).T.astype(o_ref.dtype)


def _rope_tables(n_lat):
    rows = n_lat // GRID_W
    row = jnp.repeat(jnp.arange(rows, dtype=F32), GRID_W)
    col = jnp.tile(jnp.arange(GRID_W, dtype=F32), rows)
    axis_dim = MLA_ROPE // 2
    inv_freq = 1.0 / (ROPE_THETA ** (jnp.arange(0, axis_dim, 2, dtype=F32) / axis_dim))
    ang_r = row[:, None] * inv_freq
    ang_c = col[:, None] * inv_freq
    cr, sr, cc, sc = jnp.cos(ang_r), jnp.sin(ang_r), jnp.cos(ang_c), jnp.sin(ang_c)
    cp = jnp.concatenate([cr, cr, cc, cc], axis=-1)
    sp = jnp.concatenate([-sr, sr, -sc, sc], axis=-1)
    return cp, sp


_ROPE_SWAP = np.concatenate([np.arange(16, 32), np.arange(0, 16), np.arange(48, 64), np.arange(32, 48)])


def _mla_side(x, mods, pre_g, w_in_ext, q_g, kv_g, w_uq_ext, w_ukv, tabs, b, n, tm, tk, want_q):
    m, d = x.shape
    ql, kvl = q_g.shape[-1], kv_g.shape[-1]
    t1k, t2k, t1q, t2q = tabs
    x_spec, mod_spec = _row_specs(tm, d, n // tm)
    row = lambda w: pl.BlockSpec((tm, w), lambda i: (i, 0))
    tab_spec = pl.BlockSpec((tm, 128), lambda i: (i % (n // tm), 0))
    qn, cn, kr = pl.pallas_call(
        functools.partial(_mla_in_kernel, ql=ql, kvl=kvl), grid=(m // tm,),
        in_specs=[x_spec, mod_spec, _const_spec((1, d)), _const_spec(w_in_ext.shape),
                  _const_spec((1, ql)), _const_spec((1, kvl)), tab_spec, tab_spec],
        out_specs=[row(ql), row(kvl), row(128)],
        out_shape=[jax.ShapeDtypeStruct((m, ql), BF16), jax.ShapeDtypeStruct((m, kvl), BF16),
                   jax.ShapeDtypeStruct((m, 128), BF16)],
        compiler_params=_cp("parallel"), name="mla_in",
    )(x, mods, pre_g.reshape(1, d), w_in_ext, q_g.reshape(1, ql), kv_g.reshape(1, kvl), t1k, t2k)

    nc = n // tk
    kt, v = pl.pallas_call(
        _mla_kv_kernel, grid=(b, nc),
        in_specs=[pl.BlockSpec((tk, kvl), lambda bi, c: (bi * nc + c, 0)),
                  pl.BlockSpec((tk, 128), lambda bi, c: (bi * nc + c, 0)),
                  _const_spec(w_ukv.shape)],
        out_specs=[pl.BlockSpec((1, MLA_HEADS, 1, tk, 256), lambda bi, c: (bi, 0, c, 0, 0)),
                   pl.BlockSpec((1, MLA_HEADS, 1, MLA_VT, tk), lambda bi, c: (bi, 0, c, 0, 0))],
        out_shape=[jax.ShapeDtypeStruct((b, MLA_HEADS, nc, tk, 256), BF16),
                   jax.ShapeDtypeStruct((b, MLA_HEADS, nc, MLA_VT, tk), BF16)],
        compiler_params=_cp("parallel", "parallel"), name="mla_kv",
    )(cn, kr, w_ukv)

    q = None
    if want_q:
        qscale = (MLA_NOPE + MLA_ROPE) ** -0.5 * math.log2(math.e)
        tpb = n // tm
        q = pl.pallas_call(
            functools.partial(_mla_q_kernel, qscale=qscale), grid=(m // tm,),
            in_specs=[row(ql), _const_spec(w_uq_ext.shape), tab_spec, tab_spec],
            out_specs=pl.BlockSpec((1, MLA_HEADS, 256, tm), lambda i: (i // tpb, 0, 0, i % tpb)),
            out_shape=jax.ShapeDtypeStruct((b, MLA_HEADS, 256, n), BF16),
            compiler_params=_cp("parallel"), name="mla_q",
        )(qn, w_uq_ext, t1q, t2q)
    return q, kt, v


def _flash_call(qt, kc, vtc, kl, vtl, tq):
    b, hh, _, n = qt.shape
    c = kc.shape[-2]
    n_lat = 0 if kl is None else kl.shape[2]
    in_specs = [pl.BlockSpec((1, 1, 256, tq), lambda bi, h, i: (bi, h, 0, i)),
                pl.BlockSpec((1, 1, 1, c, 256), lambda bi, h, i: (bi, h, 0, 0, 0)),
                pl.BlockSpec((1, 1, 1, MLA_VT, c), lambda bi, h, i: (bi, h, 0, 0, 0))]
    args = [qt, kc, vtc]
    scratch = [pltpu.VMEM((MLA_VT, tq), F32)]
    if n_lat:
        assert n_lat % 2 == 0, "latent key chunks are consumed in pairs"
        tk = kl.shape[-2]
        in_specs += [pl.BlockSpec((1, 1, n_lat, tk, 256), lambda bi, h, i: (bi, h, 0, 0, 0)),
                     pl.BlockSpec((1, 1, n_lat, MLA_VT, tk), lambda bi, h, i: (bi, h, 0, 0, 0))]
        args += [kl, vtl]
        scratch = [pltpu.VMEM((2, tk, tq), F32)] + scratch
    return pl.pallas_call(
        functools.partial(_flash_kernel, n_lat=n_lat), grid=(b, hh, n // tq),
        in_specs=in_specs,
        out_specs=pl.BlockSpec((1, tq, MLA_V), lambda bi, h, i: (bi, i, h)),
        out_shape=jax.ShapeDtypeStruct((b, n, hh * MLA_V), BF16),
        scratch_shapes=scratch,
        compiler_params=_cp("parallel", "parallel", "arbitrary"), name="flash",
    )(*args)


def _mla_layer(xl, xc, mods_l, mods_c, pre_g, fin, w_in, q_g, kv_g, w_uq, w_ukv, w_o, b, n, c):
    d = xl.shape[-1]
    ql, kvl = q_g.shape[-1], kv_g.shape[-1]
    hh = MLA_HEADS
    rope_cols = w_in[:, ql + kvl:]
    w_in_ext = jnp.concatenate([w_in, rope_cols[:, _ROPE_SWAP]], axis=1).astype(BF16)
    wq = w_uq.reshape(ql, hh, MLA_NOPE + MLA_ROPE)
    w_uq_ext = jnp.concatenate([wq, wq[:, :, MLA_NOPE:][:, :, _ROPE_SWAP]], axis=-1)
    w_uq_ext = w_uq_ext.reshape(ql, hh * 256).astype(BF16)
    w_ukv_b = w_ukv.astype(BF16)
    w_o_b = w_o.astype(BF16)

    cp, sp = _rope_tables(n)
    z64l, o64l = jnp.zeros((n, 64), F32), jnp.ones((n, 64), F32)
    z64c, o64c = jnp.zeros((c, 64), F32), jnp.ones((c, 64), F32)
    cat = lambda a, bb: jnp.concatenate([a, bb], axis=1)
    tabs_l = (cat(cp, z64l), cat(sp, z64l), cat(cp, z64l), cat(sp, o64l))
    tabs_c = (cat(z64c, z64c), cat(z64c, o64c), cat(z64c, z64c), cat(z64c, o64c))

    tm_l = min(512, n)
    tk_l = min(512, n // 2)
    ql_, kl, vtl = _mla_side(xl, mods_l, pre_g, w_in_ext, q_g, kv_g, w_uq_ext, w_ukv_b, tabs_l,
                             b, n, tm_l, tk_l, True)
    qc_, kc, vtc = _mla_side(xc, mods_c, pre_g, w_in_ext, q_g, kv_g, w_uq_ext, w_ukv_b, tabs_c,
                             b, c, c, c, True)
    o_lat = _flash_call(ql_, kc, vtc, kl, vtl, min(1024, n)).reshape(b * n, hh * MLA_V)
    o_ctx = _flash_call(qc_, kc, vtc, None, None, c).reshape(b * c, hh * MLA_V)
    zb = jnp.zeros((d,), F32)
    xl = fin(xl, o_lat, mods_l, w_o_b, zb, tm_l, n)
    xc = fin(xc, o_ctx, mods_c, w_o_b, zb, c, c)
    return xl, xc


def _hy_in_kernel(x_ref, xp_ref, xn_ref, mod_ref, g_ref, w_ref, b_ref, cw_ref, cb_ref,
                  g0_ref, vg_ref, *, tpb):
    i = pl.program_id(0)
    g = g_ref[...]
    shift, scale = mod_ref[0, 0:1, :], mod_ref[0, 1:2, :]
    xcat = jnp.concatenate([xp_ref[...], x_ref[...], xn_ref[...]], axis=0)
    ucat = _dot(_normmod(xcat, g, shift, scale).astype(BF16), w_ref[...]) + b_ref[...]
    tm = x_ref.shape[0]
    u = ucat[8:tm + 8]
    first = (i % tpb) == 0
    last = (i % tpb) == tpb - 1
    prev_row = jnp.where(first, 0.0, ucat[7:8, :])
    next_row = jnp.where(last, 0.0, ucat[tm + 8:tm + 9, :])
    ridx = lax.broadcasted_iota(jnp.int32, (tm, 1), 0)
    dn = jnp.where(ridx == 0, prev_row, pltpu.roll(u, 1, 0))
    upw = jnp.where(ridx == tm - 1, next_row, pltpu.roll(u, tm - 1, 0))
    conv = cb_ref[...] + dn * cw_ref[0:1, :] + u * cw_ref[1:2, :] + upw * cw_ref[2:3, :]
    d = g0_ref.shape[-1]
    g0_ref[...] = conv[:, :d]
    vg_ref[...] = conv[:, 2 * d:] * conv[:, d:2 * d]


def _hy_in_call(x, mods, pre_g, w_in, b_in, conv_w, conv_b, tm, n):
    m, d = x.shape
    p = w_in.shape[1]
    tpb = n // tm
    x_spec, mod_spec = _row_specs(tm, d, tpb)
    r8 = tm // 8
    nb8 = m // 8
    prev_spec = pl.BlockSpec((8, d), lambda i: (jnp.maximum(i * r8 - 1, 0), 0))
    next_spec = pl.BlockSpec((8, d), lambda i: (jnp.minimum((i + 1) * r8, nb8 - 1), 0))
    cw = jnp.pad(conv_w, ((0, 8 - conv_w.shape[0]), (0, 0)))
    return pl.pallas_call(
        functools.partial(_hy_in_kernel, tpb=tpb), grid=(m // tm,),
        in_specs=[x_spec, prev_spec, next_spec, mod_spec, _const_spec((1, d)), _const_spec((d, p)),
                  _const_spec((1, p)), _const_spec((8, p)), _const_spec((1, p))],
        out_specs=[x_spec, x_spec],
        out_shape=[jax.ShapeDtypeStruct((m, d), F32), jax.ShapeDtypeStruct((m, d), F32)],
        compiler_params=_cp("parallel"), name="hy_in",
    )(x, x, x, mods, pre_g.reshape(1, d), w_in, b_in.reshape(1, p), cw, conv_b.reshape(1, p))


def _hy_filter_kernel(bands_ref, w1_ref, b1_ref, fq_ref, w2_ref, b2_ref, w3_ref, dl_ref,
                      k_ref, nrm_ref, *, n, tr):
    i = pl.program_id(0)
    bwd = i >= n // tr
    row = lax.broadcasted_iota(jnp.int32, (tr, LANE), 0) + i * tr
    j = jnp.where(bwd, 2 * n - row, row).astype(F32)
    lane = lax.broadcasted_iota(jnp.int32, (tr, LANE), 1)
    t = j * (1.0 / (n - 1))
    arg = (2.0 * math.pi / n) * j * bands_ref[0:1, :] + bands_ref[1:2, :]
    z = jnp.where(lane == 0, t, jnp.where(lane <= 2 * HYENA_BANDS, jnp.sin(arg), 0.0))
    fq = fq_ref[...]
    a = jnp.sin(fq * (_dot_hi(z, w1_ref[...]) + b1_ref[...]))
    for k in range(w2_ref.shape[0]):
        a = jnp.sin(fq * (_dot_hi(a, w2_ref[k]) + b2_ref[k]))
    h = _dot_hi(a, w3_ref[jnp.where(bwd, 1, 0)])
    decay = jnp.exp(-t[:, 0:1] * dl_ref[...])
    k = h * decay
    k = jnp.where(row[:, 0:1] == n, 0.0, k)
    k_ref[...] = k
    part = jnp.sum(jnp.abs(k), axis=0, keepdims=True)

    @pl.when(i == 0)
    def _():
        nrm_ref[...] = jnp.zeros_like(nrm_ref)

    nrm_ref[...] += jnp.broadcast_to(part, nrm_ref.shape)


def _hy_filter_call(n, d, f_w1, f_b1, f_freq, f_w2, f_b2, f_w3):
    fw = f_w1.shape[1]
    tr = min(512, n)
    bands_np = np.zeros((8, LANE), np.float32)
    bands_np[0, 1:1 + HYENA_BANDS] = np.linspace(1e-4, HYENA_BANDS - 1, HYENA_BANDS, dtype=np.float32)
    bands_np[0, 1 + HYENA_BANDS:1 + 2 * HYENA_BANDS] = bands_np[0, 1:1 + HYENA_BANDS]
    bands_np[1, 1:1 + HYENA_BANDS] = 0.5 * np.pi
    bands_np[1, 1 + HYENA_BANDS:1 + 2 * HYENA_BANDS] = np.pi
    w1p = jnp.zeros((LANE, fw), F32).at[:f_w1.shape[0]].set(f_w1)
    deltas = jnp.abs(jnp.linspace(math.log(HYENA_TARGET) / HYENA_SLOW, math.log(HYENA_TARGET) / HYENA_FAST,
                                  d, dtype=F32)).reshape(1, d)
    row = pl.BlockSpec((tr, d), lambda i: (i, 0))
    return pl.pallas_call(
        functools.partial(_hy_filter_kernel, n=n, tr=tr), grid=(2 * n // tr,),
        in_specs=[_const_spec((8, LANE)), _const_spec((LANE, fw)), _const_spec((1, fw)), _const_spec((1, fw)),
                  _const_spec(f_w2.shape), _const_spec((f_w2.shape[0], 1, fw)), _const_spec((2, fw, d)),
                  _const_spec((1, d))],
        out_specs=[row, _const_spec((8, d))],
        out_shape=[jax.ShapeDtypeStruct((2 * n, d), F32), jax.ShapeDtypeStruct((8, d), F32)],
        compiler_params=_cp("arbitrary"), name="hy_filter",
    )(jnp.asarray(bands_np), w1p, f_b1.reshape(1, fw), f_freq.reshape(1, fw), f_w2,
      f_b2.reshape(f_w2.shape[0], 1, fw), jnp.transpose(f_w3.reshape(fw, 2, d), (1, 0, 2)), deltas)


def _dft_cs(nf, nt, period):
    ft = (np.arange(nf)[:, None] * np.arange(nt)[None, :]) % period
    ang = 2.0 * np.pi * ft / period
    return np.cos(ang), np.sin(ang)


def _twiddle_tables(n1, n2, lead_t2):
    nn = n1 * n2
    f1 = jnp.arange(n1, dtype=jnp.int32)
    t2 = jnp.arange(n2, dtype=jnp.int32)
    idx = (t2[:, None] * f1[None, :]) % nn if lead_t2 else (f1[:, None] * t2[None, :]) % nn
    ang = idx.astype(F32) * (2.0 * math.pi / nn)
    shape = idx.shape + (LANE,)
    return (jnp.broadcast_to(jnp.cos(ang)[..., None], shape),
            jnp.broadcast_to(jnp.sin(ang)[..., None], shape))


KRON_R = 16
KRON_W = 256


def _cblock(mc):
    return np.block([[mc.real, -mc.imag], [mc.imag, mc.real]])


def _split_radix(n1):
    b = 16 if (n1 % 16 == 0 and n1 >= 64) else 4
    assert n1 % b == 0 and (n1 // b) % 2 == 0
    return n1 // b, b


def _slab_dft_mats(n1, a_in, a_out, sign, neg_im=False, real_in=False, scale=1.0):
    a, b = _split_radix(n1)
    r = KRON_R
    eye = np.eye(r)
    w = lambda num, den: np.exp(sign * 2j * np.pi * num / den)
    ua = np.arange(a)[:, None] * np.arange(a)[None, :]
    vb = np.arange(b)[:, None] * np.arange(b)[None, :]
    if sign < 0:
        m1 = np.kron(w(ua[:, :a_in], a) * scale, eye)
        l1 = _cblock(m1)
        if neg_im:
            l1[:, a_in * r:] *= -1.0
        if real_in:
            l1 = l1[:, :a_in * r]
        l2 = np.stack([_cblock(np.kron(w(vb, b) * w(u * np.arange(b)[None, :], n1), eye)) for u in range(a)])
    else:
        l1 = np.stack([_cblock(np.kron(w(vb, b) * w(u * np.arange(b)[:, None], n1), eye)) for u in range(a)])
        l2 = _cblock(np.kron(w(ua[:a_out, :], a) * scale, eye))
    return jnp.asarray(l1, BF16), jnp.asarray(l2, BF16)


def _kfa_kernel(*refs, nparts, a_in, a, b):
    l1_ref, l2_ref = refs[0], refs[1]
    parts = refs[2:2 + nparts]
    twc_ref, tws_ref, or_ref, oi_ref = refs[2 + nparts:]
    r = KRON_R
    reps = or_ref.shape[-1] // LANE
    l1 = l1_ref[...]
    y1 = []
    for bb in range(b):
        x = jnp.concatenate([p[0, aa * b + bb] for p in parts for aa in range(a_in)], axis=0).astype(BF16)
        y1.append(_dot(l1, x).astype(BF16))
    for u in range(a):
        x = jnp.concatenate([y1[bb][u * r:(u + 1) * r] for bb in range(b)]
                            + [y1[bb][(a + u) * r:(a + u + 1) * r] for bb in range(b)], axis=0)
        z = _dot(l2_ref[u], x)
        for v in range(b):
            f1 = u + a * v
            zr = z[v * r:(v + 1) * r]
            zi = z[(b + v) * r:(b + v + 1) * r]
            c = jnp.concatenate([twc_ref[f1]] * reps, axis=1)
            sn = jnp.concatenate([tws_ref[f1]] * reps, axis=1)
            or_ref[0, f1] = (zr * c + zi * sn).astype(or_ref.dtype)
            oi_ref[0, f1] = (zi * c - zr * sn).astype(oi_ref.dtype)


def _kfa_call(parts, nb_out, n1, n2, d, l1, l2, twc, tws):
    a, b = _split_radix(n1)
    a_in = parts[0][0].shape[1] // b
    r, w = KRON_R, min(KRON_W, d)
    in_specs = [_const_spec(l1.shape), pl.BlockSpec(l2.shape, lambda bi, j, k: (0, 0, 0), pipeline_mode=pl.Buffered(1))]
    args = [l1, l2]
    for arr, bi_fixed in parts:
        t1_in = arr.shape[1]
        if bi_fixed is None:
            in_specs.append(pl.BlockSpec((1, t1_in, r, w), lambda bi, j, k: (bi, 0, j, k)))
        else:
            in_specs.append(pl.BlockSpec((1, t1_in, r, w), lambda bi, j, k, f=bi_fixed: (f, 0, j, k)))
        args.append(arr)
    tw_spec = pl.BlockSpec((n1, r, LANE), lambda bi, j, k: (0, j, 0))
    out_spec = pl.BlockSpec((1, n1, r, w), lambda bi, j, k: (bi, 0, j, k))
    out = jax.ShapeDtypeStruct((nb_out, n1, n2, d), BF16)
    return pl.pallas_call(
        functools.partial(_kfa_kernel, nparts=len(parts), a_in=a_in, a=a, b=b),
        grid=(nb_out, n2 // r, d // w),
        in_specs=in_specs + [tw_spec, tw_spec], out_specs=[out_spec, out_spec], out_shape=[out, out],
        compiler_params=_cp("parallel", "parallel", "parallel"), name="kfa",
    )(*args, twc, tws)


def _kfc_kernel(l3_ref, l4_ref, gr_ref, gi_ref, vg_ref, g0_ref, skip_ref, o_ref, *, a, b, a_out):
    r = KRON_R
    y3 = []
    for u in range(a):
        x = jnp.concatenate([gr_ref[0, u + a * v] for v in range(b)]
                            + [gi_ref[0, u + a * v] for v in range(b)], axis=0)
        y3.append(_dot(l3_ref[u], x).astype(BF16))
    l4 = l4_ref[...]
    skip = skip_ref[...]
    for bb in range(b):
        x = jnp.concatenate([y3[u][bb * r:(bb + 1) * r] for u in range(a)]
                            + [y3[u][(b + bb) * r:(b + bb + 1) * r] for u in range(a)], axis=0)
        y = _dot(l4, x)
        for sg in range(2):
            for aa in range(a_out):
                t1 = aa * b + bb
                yb = y[(sg * a_out + aa) * r:(sg * a_out + aa + 1) * r]
                o_ref[sg, t1] = ((yb + vg_ref[sg, t1] * skip) * g0_ref[sg, t1]).astype(o_ref.dtype)


def _kfc_call(g1r, g1i, vg4, g04, skip, n1, n2, d, l3, l4):
    a, b = _split_radix(n1)
    nb, t1_out = vg4.shape[:2]
    r, w = KRON_R, min(KRON_W, d)
    slab = pl.BlockSpec((1, n1, r, w), lambda j, k: (0, 0, j, k))
    nat = pl.BlockSpec((nb, t1_out, r, w), lambda j, k: (0, 0, j, k))
    return pl.pallas_call(
        functools.partial(_kfc_kernel, a=a, b=b, a_out=t1_out // b), grid=(n2 // r, d // w),
        in_specs=[pl.BlockSpec(l3.shape, lambda j, k: (0, 0, 0), pipeline_mode=pl.Buffered(1)),
                  _const_spec(l4.shape), slab, slab, nat, nat, pl.BlockSpec((1, w), lambda j, k: (0, k))],
        out_specs=nat, out_shape=jax.ShapeDtypeStruct(vg4.shape, BF16),
        compiler_params=_cp("parallel", "parallel"), name="kfc",
    )(l3, l4, g1r, g1i, vg4, g04, skip.reshape(1, d))


def _hy_kb_kernel(fm_ref, fr_ref, fi_ref, sc_ref, kr_ref, ki_ref, *, nf):
    fm = fm_ref[...]
    n2 = fr_ref.shape[1]
    sc = sc_ref[...]
    for s in range(nf):
        kk = _dot(fm, jnp.concatenate([fr_ref[s], fi_ref[s]], axis=0))
        kr_ref[s] = kk[:n2] * sc
        ki_ref[s] = kk[n2:] * sc


def _hy_b_kernel(fm_ref, fmc_ref, xr_ref, xi_ref, kr_ref, ki_ref, twc_ref, tws_ref,
                 or_ref, oi_ref, *, nf, d):
    fm = fm_ref[...]
    fmc = fmc_ref[...]
    n2 = xr_ref.shape[1]
    for s in range(nf):
        x = _dot(fm, jnp.concatenate([xr_ref[s], xi_ref[s]], axis=0))
        xr, xi = x[:n2], x[n2:]
        kr, ki = kr_ref[s], ki_ref[s]
        yr = (xr * kr - xi * ki).astype(BF16)
        yi = (xr * ki + xi * kr).astype(BF16)
        g = _dot(fmc, jnp.concatenate([yr, yi], axis=0))
        gr, gi = g[:n2], g[n2:]
        c = jnp.concatenate([twc_ref[s]] * (d // LANE), axis=1)
        sn = jnp.concatenate([tws_ref[s]] * (d // LANE), axis=1)
        or_ref[s] = (gr * c - gi * sn).astype(or_ref.dtype)
        oi_ref[s] = (gi * c + gr * sn).astype(oi_ref.dtype)


def _block_c(cs, sn, sign):
    return np.block([[cs, -sign * sn], [sign * sn, cs]])


def _hy_conv_long(vg, g0, k, nrm, skip, b, n, d):
    n2 = FFT_N2
    nn = 2 * n
    n1 = nn // n2
    rows = n // n2
    a, _ = _split_radix(n1)
    la1, la2 = _slab_dft_mats(n1, a // 2, 0, -1)
    lk1, lk2 = _slab_dft_mats(n1, a, 0, -1, real_in=True)
    lc3, lc4 = _slab_dft_mats(n1, 0, a // 2, +1)
    cs2, sn2 = _dft_cs(n2, n2, n2)
    fm_b = jnp.asarray(_block_c(cs2, sn2, -1.0), BF16)
    fm_bc = jnp.asarray(_block_c(cs2, sn2, 1.0), BF16)
    twc, tws = _twiddle_tables(n1, n2, lead_t2=False)

    kfr, kfi = _kfa_call([(k.reshape(1, n1, n2, d), 0)], 1, n1, n2, d, lk1, lk2, twc, tws)
    nf = 4 if n1 % 4 == 0 else 1
    scale = (1.0 / (nrm[0:1, :] * nn))
    slab = pl.BlockSpec((nf, n2, d), lambda j: (j, 0, 0))
    shp3 = (n1, n2, d)
    khr, khi = pl.pallas_call(
        functools.partial(_hy_kb_kernel, nf=nf), grid=(n1 // nf,),
        in_specs=[_const_spec(fm_b.shape), slab, slab, _const_spec((1, d))],
        out_specs=[slab, slab],
        out_shape=[jax.ShapeDtypeStruct(shp3, F32)] * 2,
        compiler_params=_cp("parallel"), name="hy_kb",
    )(fm_b, kfr.reshape(shp3), kfi.reshape(shp3), scale)

    vg4 = vg.reshape(b, rows, n2, d)
    g04 = g0.reshape(b, rows, n2, d)
    x1r, x1i = _kfa_call([(vg4, 0), (vg4, 1)], 1, n1, n2, d, la1, la2, twc, tws)
    tw_slab = pl.BlockSpec((nf, n2, LANE), lambda j: (j, 0, 0))
    g1r, g1i = pl.pallas_call(
        functools.partial(_hy_b_kernel, nf=nf, d=d), grid=(n1 // nf,),
        in_specs=[_const_spec(fm_b.shape), _const_spec(fm_bc.shape), slab, slab, slab, slab,
                  tw_slab, tw_slab],
        out_specs=[slab, slab],
        out_shape=[jax.ShapeDtypeStruct(shp3, BF16)] * 2,
        compiler_params=_cp("parallel"), name="hy_b",
    )(fm_b, fm_bc, x1r.reshape(shp3), x1i.reshape(shp3), khr, khi, twc, tws)
    out = _kfc_call(g1r.reshape(1, n1, n2, d), g1i.reshape(1, n1, n2, d), vg4, g04, skip, n1, n2, d, lc3, lc4)
    return out.reshape(b * n, d)


def _hy_short_kernel(fa_ref, fk_ref, fi_ref, vg_ref, g0_ref, k_ref, nrm_ref, skip_ref, o_ref, *, n):
    z = jnp.concatenate([vg_ref[0], vg_ref[1]], axis=0)
    x = _dot_hi(fa_ref[...], z)
    kk = _dot_hi(fk_ref[...], k_ref[...])
    nn = 2 * n
    sc = 1.0 / (nrm_ref[0:1, :] * nn)
    xr, xi = x[:nn], x[nn:]
    kr, ki = kk[:nn] * sc, kk[nn:] * sc
    y = _dot_hi(fi_ref[...], jnp.concatenate([xr * kr - xi * ki, xr * ki + xi * kr], axis=0))
    skip = skip_ref[...]
    for bi in range(2):
        o_ref[bi] = ((y[bi * n:(bi + 1) * n] + vg_ref[bi] * skip) * g0_ref[bi]).astype(o_ref.dtype)


def _hy_conv_short(vg, g0, k, nrm, skip, b, n, d):
    nn = 2 * n
    cs, sn = _dft_cs(nn, n, nn)
    fa = jnp.asarray(_block_c(cs, sn, -1.0), F32)
    csk, snk = _dft_cs(nn, nn, nn)
    fk = jnp.asarray(np.concatenate([csk, -snk], axis=0), F32)
    csi, sni = _dft_cs(n, nn, nn)
    fi = jnp.asarray(_block_c(csi, sni, 1.0), F32)
    cb = 256
    col3 = pl.BlockSpec((b, n, cb), lambda j: (0, 0, j))
    vec = pl.BlockSpec((1, cb), lambda j: (0, j))
    out = pl.pallas_call(
        functools.partial(_hy_short_kernel, n=n), grid=(d // cb,),
        in_specs=[_const_spec(fa.shape), _const_spec(fk.shape), _const_spec(fi.shape), col3, col3,
                  pl.BlockSpec((nn, cb), lambda j: (0, j)), pl.BlockSpec((8, cb), lambda j: (0, j)), vec],
        out_specs=col3, out_shape=jax.ShapeDtypeStruct((b, n, d), BF16),
        compiler_params=_cp("parallel"), name="hy_short",
    )(fa, fk, fi, vg.reshape(b, n, d), g0.reshape(b, n, d), k, nrm, skip.reshape(1, d))
    return out.reshape(b * n, d)


def _hyena_layer(xl, xc, mods_l, mods_c, pre_g, fin, w_in, b_in, conv_w, conv_b, filt, skip,
                 w_out, b_out, b, n, c):
    d = xl.shape[-1]
    w_in_b = w_in.astype(BF16)
    w_out_b = w_out.astype(BF16)
    tm = min(512, n)
    g0, vg = _hy_in_call(xl, mods_l, pre_g, w_in_b, b_in, conv_w, conv_b, tm, n)
    k, nrm = _hy_filter_call(n, d, *filt)
    u_out = _hy_conv_long(vg, g0, k, nrm, skip[0], b, n, d)
    xl = fin(xl, u_out, mods_l, w_out_b, b_out, tm, n)

    g0c, vgc = _hy_in_call(xc, mods_c, pre_g, w_in_b, b_in, conv_w, conv_b, c, c)
    kc, nrmc = _hy_filter_call(c, d, *filt)
    u_out_c = _hy_conv_short(vgc, g0c, kc, nrmc, skip[0], b, c, d)
    xc = fin(xc, u_out_c, mods_c, w_out_b, b_out, c, c)
    return xl, xc


def _s5_operators(lam_re, lam_im, log_dt, b_re, b_im, c_re, c_im, d_skip):
    t = S5_T
    g, ns = lam_re.shape[1], lam_re.shape[2]
    gc = b_re.shape[-1]
    gl = LANE // gc
    nblk = g // gl
    lam = lax.complex(lam_re, lam_im)
    dt = jnp.exp(log_dt)[..., None]
    lam_bar = jnp.exp(lam * dt)
    b_bar = ((lam_bar - 1.0) / lam)[..., None] * lax.complex(b_re, b_im)
    c_mat = lax.complex(c_re, c_im)
    pw = jnp.arange(t + 1, dtype=F32)
    lam_pw = jnp.exp((lam * dt)[None] * pw[:, None, None, None])
    hp = HIGHEST
    kern = jnp.einsum('dgcn,tdgn,dgne->dgtce', c_mat, lam_pw[:t], b_bar, precision=hp).real
    dsk = d_skip.reshape(g, gc)
    kt = jnp.swapaxes(kern, -1, -2)
    centre = kt[0][:, 0] + kt[1][:, 0] + jnp.eye(gc, dtype=F32)[None] * dsk[:, :, None]
    ks = jnp.concatenate([kt[1][:, 1:][:, ::-1], centre[:, None], kt[0][:, 1:]], axis=1)
    eye_gl = jnp.eye(gl, dtype=F32)
    ks = ks.reshape(nblk, gl, 2 * t - 1, gc, gc)
    d_tab = jnp.einsum('bglec,gh->blgehc', ks, eye_gl).reshape(nblk, 2 * t - 1, LANE, LANE)

    def compact(z_ri):
        z = jnp.transpose(z_ri, (3, 2, 1, 4, 0, 5)).reshape(nblk, gl, t, 2, gc, 2 * ns)
        return jnp.transpose(z, (0, 2, 1, 3, 4, 5))

    pf = lam_pw[:t][::-1][:, 0, :, :, None] * b_bar[0][None]
    pb = lam_pw[:t][:, 1, :, :, None] * b_bar[1][None]
    pcat = jnp.swapaxes(jnp.stack([pf, pb], axis=0), -1, -2)
    p_tab = compact(jnp.stack([pcat.real, pcat.imag], axis=0))
    qf = c_mat[0][None] * lam_pw[1:t + 1, 0][:, :, None, :]
    qb = c_mat[1][None] * lam_pw[1:t + 1][::-1][:, 1][:, :, None, :]
    qcat = jnp.stack([qf, qb], axis=0)
    q_tab = compact(jnp.stack([qcat.real, -qcat.imag], axis=0))

    a = lam_pw[t]
    m_op, p_op, q_op = _s5_expand(d_tab, p_tab, q_tab)
    return m_op, p_op, q_op, a.real.reshape(2, g * ns), a.imag.reshape(2, g * ns)


def _s5_m_kernel(d_ref, o_ref, *, t):
    s = pl.program_id(1)
    for tt in range(t):
        o_ref[0, :, tt * LANE:(tt + 1) * LANE] = d_ref[0, tt - s + t - 1].astype(o_ref.dtype)


def _s5_pq_kernel(c_ref, o_ref, *, transpose):
    gl, nd, gc, w = c_ref.shape[2:]
    ns = w // 2
    lane_grp = lax.broadcasted_iota(jnp.int32, (gc, gl * ns), 1) // ns
    rows = []
    for g in range(gl):
        cols = []
        for dd in range(nd):
            piece = c_ref[0, 0, g, dd]
            for ri in range(2):
                tiled = jnp.concatenate([piece[:, ri * ns:(ri + 1) * ns]] * gl, axis=1)
                cols.append(jnp.where(lane_grp == g, tiled, 0.0))
        rows.append(jnp.concatenate(cols, axis=1))
    blk = jnp.concatenate(rows, axis=0)
    o_ref[0] = (blk.T if transpose else blk).astype(o_ref.dtype)


def _s5_expand(d_tab, p_tab, q_tab):
    nblk, nlag = d_tab.shape[:2]
    t = (nlag + 1) // 2
    _, _, gl, nd, gc, w = p_tab.shape
    ncol = nd * gl * w
    m_op = pl.pallas_call(
        functools.partial(_s5_m_kernel, t=t), grid=(nblk, t),
        in_specs=[pl.BlockSpec((1, nlag, LANE, LANE), lambda b, s: (b, 0, 0, 0))],
        out_specs=pl.BlockSpec((1, LANE, t * LANE), lambda b, s: (b, s, 0)),
        out_shape=jax.ShapeDtypeStruct((nblk, t * LANE, t * LANE), BF16),
        compiler_params=_cp("parallel", "parallel"), name="s5_m_op",
    )(d_tab)
    tab_spec = pl.BlockSpec((1, 1, gl, nd, gc, w), lambda b, j: (b, j, 0, 0, 0, 0))
    p_op = pl.pallas_call(
        functools.partial(_s5_pq_kernel, transpose=False), grid=(nblk, t),
        in_specs=[tab_spec],
        out_specs=pl.BlockSpec((1, LANE, ncol), lambda b, j: (b, j, 0)),
        out_shape=jax.ShapeDtypeStruct((nblk, t * LANE, ncol), BF16),
        compiler_params=_cp("parallel", "parallel"), name="s5_p_op",
    )(p_tab)
    q_op = pl.pallas_call(
        functools.partial(_s5_pq_kernel, transpose=True), grid=(nblk, t),
        in_specs=[tab_spec],
        out_specs=pl.BlockSpec((1, ncol, LANE), lambda b, j: (b, 0, j)),
        out_shape=jax.ShapeDtypeStruct((nblk, ncol, t * LANE), BF16),
        compiler_params=_cp("parallel", "parallel"), name="s5_q_op",
    )(q_tab)
    return m_op, p_op, q_op


def _s5_sum_kernel(*refs, t):
    u_refs = refs[:t]
    p_ref = refs[t]
    outs = refs[t + 1:]
    u = jnp.concatenate([r[...] for r in u_refs], axis=1)
    res = _dot(u, p_ref[0])
    w = res.shape[1] // len(outs)
    for i, o in enumerate(outs):
        o[...] = res[:, i * w:(i + 1) * w]


def _s5_sum_call(h, p_op, rb):
    rows = h.shape[0]
    t = S5_T
    nblk = p_op.shape[0]
    w = p_op.shape[2] // 4
    u_specs = [pl.BlockSpec((rb, LANE), lambda gb, r, s=s: (r, s * nblk + gb)) for s in range(t)]
    out_spec = pl.BlockSpec((rb, w), lambda gb, r: (r, gb))
    return pl.pallas_call(
        functools.partial(_s5_sum_kernel, t=t), grid=(nblk, rows // rb),
        in_specs=u_specs + [pl.BlockSpec((1,) + p_op.shape[1:], lambda gb, r: (gb, 0, 0))],
        out_specs=[out_spec] * 4,
        out_shape=[jax.ShapeDtypeStruct((rows, nblk * w), F32)] * 4,
        compiler_params=_cp("parallel", "parallel"), name="s5_sum",
    )(*([h] * t), p_op)


def _s5_rec_kernel(sr_ref, si_ref, ar_ref, ai_ref, h0r_ref, h0i_ref, hr_ref, hi_ref, fr_ref, fi_ref,
                   cr, ci, *, kb, reverse):
    @pl.when(pl.program_id(1) == 0)
    def _():
        cr[...] = h0r_ref[0]
        ci[...] = h0i_ref[0]

    ar, ai = ar_ref[...], ai_ref[...]

    def body(i, carry):
        hr, hi = carry
        k = kb - 1 - i if reverse else i
        hr_ref[pl.ds(k, 1), :] = hr
        hi_ref[pl.ds(k, 1), :] = hi
        nr = ar * hr - ai * hi + sr_ref[pl.ds(k, 1), :]
        ni = ar * hi + ai * hr + si_ref[pl.ds(k, 1), :]
        return nr, ni

    hr, hi = lax.fori_loop(0, kb, body, (cr[...], ci[...]))
    cr[...] = hr
    ci[...] = hi
    fr_ref[0] = hr
    fi_ref[0] = hi


def _s5_rec_call(sr, si, ar, ai, h0r, h0i, nb_batch, reverse):
    rows, w = sr.shape
    nk = rows // nb_batch
    kb = min(64, nk)
    nb = nk // kb
    blk = (lambda bi, i: (bi * nb + nb - 1 - i, 0)) if reverse else (lambda bi, i: (bi * nb + i, 0))
    row_spec = pl.BlockSpec((kb, w), blk)
    vec = _const_spec((1, w))
    st = pl.BlockSpec((1, 1, w), lambda bi, i: (bi, 0, 0))
    return pl.pallas_call(
        functools.partial(_s5_rec_kernel, kb=kb, reverse=reverse), grid=(nb_batch, nb),
        in_specs=[row_spec, row_spec, vec, vec, st, st],
        out_specs=[row_spec, row_spec, st, st],
        out_shape=[jax.ShapeDtypeStruct((rows, w), F32)] * 2 + [jax.ShapeDtypeStruct((nb_batch, 1, w), F32)] * 2,
        scratch_shapes=[pltpu.VMEM((1, w), F32), pltpu.VMEM((1, w), F32)],
        compiler_params=_cp("parallel", "arbitrary"), name="s5_rec",
    )(sr, si, ar, ai, h0r, h0i)


def _s5_out_kernel(*refs, t):
    u_refs = refs[:t]
    h_refs = refs[t:t + 4]
    m_ref, q_ref, o_ref = refs[t + 4:]
    u = jnp.concatenate([r[...] for r in u_refs], axis=1)
    hcat = jnp.concatenate([r[...].astype(BF16) for r in h_refs], axis=1)
    y = _dot(u, m_ref[0]) + _dot(hcat, q_ref[0])
    o_ref[0] = (0.5 * y * (1.0 + lax.erf(y * (2.0 ** -0.5)))).astype(o_ref.dtype)


def _s5_out_call(h, states, m_op, q_op, rb):
    rows = h.shape[0]
    t = S5_T
    nblk = m_op.shape[0]
    w = q_op.shape[1] // 4
    u_specs = [pl.BlockSpec((rb, LANE), lambda gb, r, s=s: (r, s * nblk + gb)) for s in range(t)]
    return pl.pallas_call(
        functools.partial(_s5_out_kernel, t=t), grid=(nblk, rows // rb),
        in_specs=u_specs + [pl.BlockSpec((rb, w), lambda gb, r: (r, gb))] * 4
        + [pl.BlockSpec((1,) + m_op.shape[1:], lambda gb, r: (gb, 0, 0)),
           pl.BlockSpec((1,) + q_op.shape[1:], lambda gb, r: (gb, 0, 0))],
        out_specs=pl.BlockSpec((1, rb, t * LANE), lambda gb, r: (gb, r, 0)),
        out_shape=jax.ShapeDtypeStruct((nblk, rows, t * LANE), BF16),
        compiler_params=_cp("parallel", "parallel"), name="s5_out",
    )(*([h] * t), *states, m_op, q_op)


def _s5_layer(xl, xc, mods_l, mods_c, pre_g, fin, lam_re, lam_im, log_dt, b_re, b_im, c_re, c_im,
              d_skip, w_glu, b_glu, b, n, c):
    d = xl.shape[-1]
    t = S5_T
    m_op, p_op, q_op, a_re, a_im = _s5_operators(lam_re, lam_im, log_dt, b_re, b_im, c_re, c_im, d_skip)
    nblk = m_op.shape[0]
    w = a_re.shape[-1]
    hl = _normmod_call(xl, mods_l, pre_g, min(512, n), n).reshape(b * n // t, t * d)
    hc = _normmod_call(xc, mods_c, pre_g, c, c).reshape(b * c // t, t * d)

    def scan(h, init):
        sfr, sfi, sbr, sbi = _s5_sum_call(h, p_op, min(512, h.shape[0]))
        hfr, hfi, ffr, ffi = _s5_rec_call(sfr, sfi, a_re[0:1], a_im[0:1], init[0], init[1], b, False)
        hbr, hbi, fbr, fbi = _s5_rec_call(sbr, sbi, a_re[1:2], a_im[1:2], init[2], init[3], b, True)
        return (hfr, hfi, hbr, hbi), (ffr, ffi, fbr, fbi)

    zeros = jnp.zeros((b, 1, w), F32)
    _, ctx_final = scan(hc, (zeros,) * 4)
    states, _ = scan(hl, ctx_final)
    nk = n // t
    gact = _s5_out_call(hl, states, m_op, q_op, min(512, b * nk))
    g_nat = jnp.transpose(gact.reshape(nblk, b * nk, t, LANE), (1, 2, 0, 3)).reshape(b * n, d)
    return fin(xl, g_nat, mods_l, w_glu.astype(BF16), b_glu, min(512, n), n, glu=True)


def _fn_pre_kernel(x_ref, mod_ref, g_ref, cs_ref, a_ref, b_ref):
    h = _normmod(x_ref[...], g_ref[...], mod_ref[0, 0:1, :], mod_ref[0, 1:2, :]).astype(BF16)
    cs = cs_ref[...]
    gc = FNET_GC
    ab = [_dot(h[:, k * gc:(k + 1) * gc], cs) for k in range(h.shape[1] // gc)]
    a_ref[...] = jnp.concatenate([z[:, :gc] for z in ab], axis=1).astype(a_ref.dtype)
    b_ref[...] = jnp.concatenate([z[:, gc:] for z in ab], axis=1).astype(b_ref.dtype)


def _fn_c_kernel(l5_ref, l6_ref, xr_ref, xi_ref, o_ref, *, nh):
    r = KRON_R
    n2 = nh * r
    l5 = l5_ref[...]
    y5 = [_dot(l5, jnp.concatenate([xr_ref[0, f], xi_ref[0, f]], axis=0)).astype(BF16) for f in range(r)]
    l6 = l6_ref[...]
    for p in range(nh):
        x = jnp.concatenate([y5[f][p * r:(p + 1) * r] for f in range(r)]
                            + [y5[f][n2 + p * r:n2 + (p + 1) * r] for f in range(r)], axis=0)
        out = _dot(l6, x)
        for q in range(r):
            o_ref[0, p + nh * q] = out[q * r:(q + 1) * r].astype(o_ref.dtype)


def _fnet_layer(xl, mods_l, pre_g, fin, w_o, b_o, b, n, d):
    n2 = FFT_N2
    n1 = n // n2
    gc = FNET_GC
    cc, sc = _dft_cs(gc, gc, gc)
    cs = jnp.asarray(np.concatenate([cc, sc], axis=1) / np.sqrt(gc), BF16)
    tm = min(512, n)
    x_spec, mod_spec = _row_specs(tm, d, n // tm)
    ab = jax.ShapeDtypeStruct((b * n, d), BF16)
    am, bm = pl.pallas_call(
        _fn_pre_kernel, grid=(b * n // tm,),
        in_specs=[x_spec, mod_spec, _const_spec((1, d)), _const_spec(cs.shape)],
        out_specs=[x_spec, x_spec], out_shape=[ab, ab],
        compiler_params=_cp("parallel"), name="fn_pre",
    )(xl, mods_l, pre_g.reshape(1, d), cs)

    a, _ = _split_radix(n1)
    l1, l2 = _slab_dft_mats(n1, a, 0, -1, neg_im=True, scale=1.0 / np.sqrt(n))
    twc, tws = _twiddle_tables(n1, n2, lead_t2=False)
    a4, b4 = am.reshape(b, n1, n2, d), bm.reshape(b, n1, n2, d)
    xr, xi = _kfa_call([(a4, None), (b4, None)], b, n1, n2, d, l1, l2, twc, tws)

    r, w = KRON_R, min(KRON_W, d)
    nh = n2 // r
    assert n1 % r == 0 and n2 % r == 0
    m5 = np.zeros((n2, n2), np.complex128)
    m6 = np.zeros((r * r, r * r), np.complex128)
    for s in range(r):
        for p in range(nh):
            for h in range(nh):
                m5[p * r + s, h * r + s] = np.exp(-2j * np.pi * (p * h / nh + p * s / n2))
        for q in range(r):
            for f in range(r):
                m6[q * r + f, f * r + s] = np.exp(-2j * np.pi * q * s / r)
    l5 = jnp.asarray(_cblock(m5), BF16)
    l6 = jnp.asarray(np.concatenate([m6.real, -m6.imag], axis=1), BF16)
    grp = pl.BlockSpec((1, r, n2, w), lambda bi, fh, k: (bi, fh, 0, k))
    y = pl.pallas_call(
        functools.partial(_fn_c_kernel, nh=nh), grid=(b, n1 // r, d // w),
        in_specs=[_const_spec(l5.shape), _const_spec(l6.shape), grp, grp],
        out_specs=pl.BlockSpec((1, n2, None, r, w), lambda bi, fh, k: (bi, 0, fh, 0, k)),
        out_shape=jax.ShapeDtypeStruct((b, n2, n1 // r, r, d), BF16),
        compiler_params=_cp("parallel", "parallel", "parallel"), name="fn_c",
    )(l5, l6, xr, xi)
    return fin(xl, y.reshape(b * n, d), mods_l, w_o.astype(BF16), b_o, tm, n)


def kernel(x, c, ctx, c_ctx, mod_w, mod_b, mix_pre_g, mix_post_g, ffn_pre_g, ffn_post_g, ffn_w13, ffn_w2,
           mla_w_in, mla_q_norm_g, mla_kv_norm_g, mla_w_uq, mla_w_ukv, mla_w_o,
           hy_w_in, hy_b_in, hy_conv_w, hy_conv_b, hy_f_w1, hy_f_b1, hy_f_freq, hy_f_w2, hy_f_b2, hy_f_w3,
           hy_skip, hy_w_out, hy_b_out,
           s5_lambda_re, s5_lambda_im, s5_log_dt, s5_b_re, s5_b_im, s5_c_re, s5_c_im, s5_d, s5_w_glu, s5_b_glu,
           fn_w_o, fn_b_o):
    b, n, d = x.shape
    cl = ctx.shape[1]
    depth = mod_w.shape[0]
    assert b == 2 and depth == 4, "two batches ride one complex transform; one layer per mixer"
    mods = _mods(c, c_ctx, mod_w, mod_b)
    xl = x.reshape(b * n, d)
    xc = ctx.reshape(b * cl, d)
    tm = min(512, n)

    def finisher(i):
        w13 = ffn_w13[i].astype(BF16)
        w2 = ffn_w2[i].astype(BF16)

        def fin(x_, y_, mods_, wm, bm, tm_, rows_per_batch, glu=False):
            return _mix_ffn_call(x_, y_, mods_, wm, bm, mix_post_g[i], ffn_pre_g[i], ffn_post_g[i], w13, w2,
                                 tm_, rows_per_batch, glu)
        return fin

    def mods_c(i):
        return jnp.broadcast_to(mods[i, 2:3], (b, 8, d))

    xl, xc = _mla_layer(xl, xc, mods[0, 0:2], mods_c(0), mix_pre_g[0], finisher(0), mla_w_in[0],
                        mla_q_norm_g[0], mla_kv_norm_g[0], mla_w_uq[0], mla_w_ukv[0], mla_w_o[0], b, n, cl)
    filt = (hy_f_w1[0], hy_f_b1[0], hy_f_freq[0], hy_f_w2[0], hy_f_b2[0], hy_f_w3[0])
    xl, xc = _hyena_layer(xl, xc, mods[1, 0:2], mods_c(1), mix_pre_g[1], finisher(1), hy_w_in[0], hy_b_in[0],
                          hy_conv_w[0], hy_conv_b[0], filt, hy_skip[0], hy_w_out[0], hy_b_out[0], b, n, cl)
    xl = _s5_layer(xl, xc, mods[2, 0:2], mods_c(2), mix_pre_g[2], finisher(2), s5_lambda_re[0],
                   s5_lambda_im[0], s5_log_dt[0], s5_b_re[0], s5_b_im[0], s5_c_re[0], s5_c_im[0], s5_d[0],
                   s5_w_glu[0], s5_b_glu[0], b, n, cl)
    xl = _fnet_layer(xl, mods[3, 0:2], mix_pre_g[3], finisher(3), fn_w_o[0], fn_b_o[0], b, n, d)
    return xl.reshape(b, n, d)
```

```python
import functools
import math

import numpy as np
import jax
import jax.numpy as jnp
from jax import lax
from jax.experimental import pallas as pl
from jax.experimental.pallas import tpu as pltpu

F32 = jnp.float32
BF16 = jnp.bfloat16
NORM_EPS = 1e-6
LANE = 128
MXU_COLS = 256
VMEM_LIMIT = 56 * 1024 * 1024
HIGHEST = lax.Precision.HIGHEST

GRID_W = 64
ROPE_THETA = 10000.0
MLA_HEADS = 8
MLA_NOPE = 128
MLA_ROPE = 64
MLA_V = 128
MLA_VT = MLA_V + 16
HYENA_BANDS = 16
HYENA_TARGET = 1e-2
HYENA_FAST = 0.3
HYENA_SLOW = 1.5
S5_GROUP = 16
S5_STATE = 64
S5_T = 16
FNET_GC = 128
FFT_N2 = 128


def _cp(*sem):
    return pltpu.CompilerParams(dimension_semantics=sem, vmem_limit_bytes=VMEM_LIMIT)


def _dot(a, b):
    return jnp.dot(a, b, preferred_element_type=F32)


def _dot_hi(a, b):
    return jnp.dot(a, b, preferred_element_type=F32, precision=HIGHEST)


def _rms(x, g):
    ms = jnp.mean(x * x, axis=-1, keepdims=True)
    return x * lax.rsqrt(ms + NORM_EPS) * g


def _normmod(x, g, shift, scale):
    return _rms(x, g) * (1.0 + scale) + shift


def _const_spec(shape):
    nd = len(shape)
    return pl.BlockSpec(shape, lambda *_: (0,) * nd)


def _mods_kernel(st_ref, w_ref, b_ref, o_ref):
    st = st_ref[...]
    st = st * jax.nn.sigmoid(st)
    w = w_ref[0]
    rows = [jnp.sum(st[:, r:r + 1] * w, axis=0, keepdims=True) for r in range(3)]
    rows.append(jnp.zeros((5, w.shape[1]), F32))
    o_ref[0] = jnp.concatenate(rows, axis=0) + b_ref[0]


def _mods(c, c_ctx, mod_w, mod_b):
    depth, d, n6 = mod_w.shape
    st = jnp.zeros((d, 8), F32).at[:, 0:2].set(c.T).at[:, 2].set(c_ctx)
    tn = 1024
    out = pl.pallas_call(
        _mods_kernel,
        grid=(depth, n6 // tn),
        in_specs=[_const_spec((d, 8)),
                  pl.BlockSpec((1, d, tn), lambda i, j: (i, 0, j)),
                  pl.BlockSpec((1, 1, tn), lambda i, j: (i, 0, j))],
        out_specs=pl.BlockSpec((1, 8, tn), lambda i, j: (i, 0, j)),
        out_shape=jax.ShapeDtypeStruct((depth, 8, n6), F32),
        compiler_params=_cp("parallel", "parallel"),
        name="mods",
    )(st, mod_w, mod_b.reshape(depth, 1, n6))
    m = out[:, :3].reshape(depth, 3, n6 // d, d)
    return jnp.pad(m, ((0, 0), (0, 0), (0, 8 - n6 // d), (0, 0)))


def _row_specs(tm, d, tpb):
    x_spec = pl.BlockSpec((tm, d), lambda i: (i, 0))
    mod_spec = pl.BlockSpec((1, 8, d), lambda i: (i // tpb, 0, 0))
    return x_spec, mod_spec


def _normmod_kernel(x_ref, mod_ref, g_ref, o_ref):
    h = _normmod(x_ref[...], g_ref[...], mod_ref[0, 0:1, :], mod_ref[0, 1:2, :])
    o_ref[...] = h.astype(o_ref.dtype)


def _normmod_call(x, mods, g, tm, rows_per_batch):
    m, d = x.shape
    x_spec, mod_spec = _row_specs(tm, d, rows_per_batch // tm)
    return pl.pallas_call(
        _normmod_kernel, grid=(m // tm,),
        in_specs=[x_spec, mod_spec, _const_spec((1, d))],
        out_specs=x_spec, out_shape=jax.ShapeDtypeStruct((m, d), BF16),
        compiler_params=_cp("parallel"), name="normmod",
    )(x, mods, g.reshape(1, d))


def _mix_ffn_kernel(x_ref, y_ref, mod_ref, wm_ref, bm_ref, gm_ref, pre_ref, post_ref, w13_ref, w2_ref, o_ref,
                    *, f, cuts, glu):
    z = _dot(y_ref[...].astype(BF16), wm_ref[...]) + bm_ref[...]
    if glu:
        d = o_ref.shape[-1]
        z = z[:, :d] * jax.nn.sigmoid(z[:, d:])
    x = x_ref[...] + mod_ref[0, 2:3, :] * _rms(z, gm_ref[...])
    h = _normmod(x, pre_ref[...], mod_ref[0, 3:4, :], mod_ref[0, 4:5, :]).astype(BF16)
    acc = None
    for lo, hi in zip(cuts[:-1], cuts[1:]):
        a = _dot(h, w13_ref[:, lo:hi])
        b = _dot(h, w13_ref[:, f + lo:f + hi])
        gact = (a * jax.nn.sigmoid(a) * b).astype(BF16)
        part = _dot(gact, w2_ref[lo:hi, :])
        acc = part if acc is None else acc + part
    o_ref[...] = x + mod_ref[0, 5:6, :] * _rms(acc, post_ref[...])


def _mix_ffn_call(x, y, mods, wm, bm, mix_post_g, ffn_pre_g, ffn_post_g, w13, w2, tm, rows_per_batch, glu=False):
    m, d = x.shape
    k, nm = wm.shape
    f = w2.shape[0]
    cuts = tuple(range(0, f, 6 * MXU_COLS)) + (f,)
    x_spec, mod_spec = _row_specs(tm, d, rows_per_batch // tm)
    once = lambda shape: pl.BlockSpec(shape, lambda i: (0, 0), pipeline_mode=pl.Buffered(1))
    return pl.pallas_call(
        functools.partial(_mix_ffn_kernel, f=f, cuts=cuts, glu=glu), grid=(m // tm,),
        in_specs=[x_spec, pl.BlockSpec((tm, k), lambda i: (i, 0)), mod_spec,
                  once((k, nm)), _const_spec((1, nm)), _const_spec((1, d)), _const_spec((1, d)),
                  _const_spec((1, d)), once((d, 2 * f)), once((f, d))],
        out_specs=x_spec, out_shape=jax.ShapeDtypeStruct((m, d), F32),
        compiler_params=_cp("parallel"), name="mix_ffn",
    )(x, y, mods, wm, bm.reshape(1, nm), mix_post_g.reshape(1, d), ffn_pre_g.reshape(1, d),
      ffn_post_g.reshape(1, d), w13, w2)


def _mla_in_kernel(x_ref, mod_ref, g_ref, w_ref, qg_ref, kvg_ref, t1_ref, t2_ref,
                   qn_ref, cn_ref, kr_ref, *, ql, kvl):
    h = _normmod(x_ref[...], g_ref[...], mod_ref[0, 0:1, :], mod_ref[0, 1:2, :])
    z = _dot(h.astype(BF16), w_ref[...])
    qn_ref[...] = _rms(z[:, :ql], qg_ref[...]).astype(BF16)
    cn_ref[...] = _rms(z[:, ql:ql + kvl], kvg_ref[...]).astype(BF16)
    pair = z[:, ql + kvl:]
    kr_ref[...] = (pair * t1_ref[...] + pltpu.roll(pair, 64, 1) * t2_ref[...]).astype(BF16)


def _mla_q_kernel(qn_ref, w_ref, t1_ref, t2_ref, q_ref, *, qscale):
    z = _dot(qn_ref[...], w_ref[...])
    t1 = t1_ref[...]
    t2 = t2_ref[...]
    for h in range(MLA_HEADS):
        base = h * 256
        pair = z[:, base + 128:base + 256]
        rp = pair * t1 + pltpu.roll(pair, 64, 1) * t2
        qcat = jnp.concatenate([z[:, base:base + 128], rp], axis=1) * qscale
        q_ref[0, h] = qcat.T.astype(BF16)


def _mla_kv_kernel(cn_ref, kr_ref, w_ref, k_ref, vt_ref):
    z = _dot(cn_ref[...], w_ref[...])
    kr = kr_ref[...]
    tk = kr.shape[0]
    ones_blk = (lax.broadcasted_iota(jnp.int32, (MLA_VT - MLA_V, tk), 0) == 0).astype(BF16)
    for h in range(MLA_HEADS):
        base = h * 256
        k_ref[0, h, 0, :, 0:128] = z[:, base:base + 128].astype(BF16)
        k_ref[0, h, 0, :, 128:256] = kr
        vt_ref[0, h, 0, 0:MLA_V, :] = z[:, base + 128:base + 256].T.astype(BF16)
        vt_ref[0, h, 0, MLA_V:MLA_VT, :] = ones_blk


def _flash_kernel(q_ref, kc_ref, vc_ref, *rest, n_lat):
    if n_lat:
        kl_ref, vl_ref, o_ref, s_scr, acc_scr = rest
    else:
        o_ref, acc_scr = rest
    qt = q_ref[0, 0]

    def qk(k, slot):
        s = _dot(k, qt)
        s_scr[slot] = s
        return jnp.max(s, axis=0, keepdims=True)

    def sm_pv(slot, vt, m, mx):
        m_new = jnp.maximum(m, mx)
        alpha = jnp.exp2(m - m_new)
        p = jnp.exp2(s_scr[slot] - m_new).astype(BF16)
        acc_scr[...] = alpha * acc_scr[...] + _dot(vt, p)
        return m_new

    sc = _dot(kc_ref[0, 0, 0], qt)
    m = jnp.max(sc, axis=0, keepdims=True)
    acc_scr[...] = _dot(vc_ref[0, 0, 0], jnp.exp2(sc - m).astype(BF16))
    if n_lat:
        mx = qk(kl_ref[0, 0, 0], 0)

        per = 8 if n_lat % 8 == 0 else (4 if n_lat % 4 == 0 else 2)

        def body(i, carry):
            m, mx_cur = carry
            c = per * i
            for u in range(per):
                mx_next = qk(kl_ref[0, 0, jnp.minimum(c + u + 1, n_lat - 1)], (u + 1) % 2)
                m = sm_pv(u % 2, vl_ref[0, 0, c + u], m, mx_cur)
                mx_cur = mx_next
            return m, mx_cur

        lax.fori_loop(0, n_lat // per, body, (m, mx))
    acc = acc_scr[...]
    o_ref[0] = (acc[0:MLA_V] / acc[MLA_V:MLA_V + 1]).T.astype(o_ref.dtype)


def _rope_tables(n_lat):
    rows = n_lat // GRID_W
    row = jnp.repeat(jnp.arange(rows, dtype=F32), GRID_W)
    col = jnp.tile(jnp.arange(GRID_W, dtype=F32), rows)
    axis_dim = MLA_ROPE // 2
    inv_freq = 1.0 / (ROPE_THETA ** (jnp.arange(0, axis_dim, 2, dtype=F32) / axis_dim))
    ang_r = row[:, None] * inv_freq
    ang_c = col[:, None] * inv_freq
    cr, sr, cc, sc = jnp.cos(ang_r), jnp.sin(ang_r), jnp.cos(ang_c), jnp.sin(ang_c)
    cp = jnp.concatenate([cr, cr, cc, cc], axis=-1)
    sp = jnp.concatenate([-sr, sr, -sc, sc], axis=-1)
    return cp, sp


_ROPE_SWAP = np.concatenate([np.arange(16, 32), np.arange(0, 16), np.arange(48, 64), np.arange(32, 48)])


def _mla_side(x, mods, pre_g, w_in_ext, q_g, kv_g, w_uq_ext, w_ukv, tabs, b, n, tm, tk, want_q):
    m, d = x.shape
    ql, kvl = q_g.shape[-1], kv_g.shape[-1]
    t1k, t2k, t1q, t2q = tabs
    x_spec, mod_spec = _row_specs(tm, d, n // tm)
    row = lambda w: pl.BlockSpec((tm, w), lambda i: (i, 0))
    tab_spec = pl.BlockSpec((tm, 128), lambda i: (i % (n // tm), 0))
    qn, cn, kr = pl.pallas_call(
        functools.partial(_mla_in_kernel, ql=ql, kvl=kvl), grid=(m // tm,),
        in_specs=[x_spec, mod_spec, _const_spec((1, d)), _const_spec(w_in_ext.shape),
                  _const_spec((1, ql)), _const_spec((1, kvl)), tab_spec, tab_spec],
        out_specs=[row(ql), row(kvl), row(128)],
        out_shape=[jax.ShapeDtypeStruct((m, ql), BF16), jax.ShapeDtypeStruct((m, kvl), BF16),
                   jax.ShapeDtypeStruct((m, 128), BF16)],
        compiler_params=_cp("parallel"), name="mla_in",
    )(x, mods, pre_g.reshape(1, d), w_in_ext, q_g.reshape(1, ql), kv_g.reshape(1, kvl), t1k, t2k)

    nc = n // tk
    kt, v = pl.pallas_call(
        _mla_kv_kernel, grid=(b, nc),
        in_specs=[pl.BlockSpec((tk, kvl), lambda bi, c: (bi * nc + c, 0)),
                  pl.BlockSpec((tk, 128), lambda bi, c: (bi * nc + c, 0)),
                  _const_spec(w_ukv.shape)],
        out_specs=[pl.BlockSpec((1, MLA_HEADS, 1, tk, 256), lambda bi, c: (bi, 0, c, 0, 0)),
                   pl.BlockSpec((1, MLA_HEADS, 1, MLA_VT, tk), lambda bi, c: (bi, 0, c, 0, 0))],
        out_shape=[jax.ShapeDtypeStruct((b, MLA_HEADS, nc, tk, 256), BF16),
                   jax.ShapeDtypeStruct((b, MLA_HEADS, nc, MLA_VT, tk), BF16)],
        compiler_params=_cp("parallel", "parallel"), name="mla_kv",
    )(cn, kr, w_ukv)

    q = None
    if want_q:
        qscale = (MLA_NOPE + MLA_ROPE) ** -0.5 * math.log2(math.e)
        tpb = n // tm
        q = pl.pallas_call(
            functools.partial(_mla_q_kernel, qscale=qscale), grid=(m // tm,),
            in_specs=[row(ql), _const_spec(w_uq_ext.shape), tab_spec, tab_spec],
            out_specs=pl.BlockSpec((1, MLA_HEADS, 256, tm), lambda i: (i // tpb, 0, 0, i % tpb)),
            out_shape=jax.ShapeDtypeStruct((b, MLA_HEADS, 256, n), BF16),
            compiler_params=_cp("parallel"), name="mla_q",
        )(qn, w_uq_ext, t1q, t2q)
    return q, kt, v


def _flash_call(qt, kc, vtc, kl, vtl, tq):
    b, hh, _, n = qt.shape
    c = kc.shape[-2]
    n_lat = 0 if kl is None else kl.shape[2]
    in_specs = [pl.BlockSpec((1, 1, 256, tq), lambda bi, h, i: (bi, h, 0, i)),
                pl.BlockSpec((1, 1, 1, c, 256), lambda bi, h, i: (bi, h, 0, 0, 0)),
                pl.BlockSpec((1, 1, 1, MLA_VT, c), lambda bi, h, i: (bi, h, 0, 0, 0))]
    args = [qt, kc, vtc]
    scratch = [pltpu.VMEM((MLA_VT, tq), F32)]
    if n_lat:
        assert n_lat % 2 == 0, "latent key chunks are consumed in pairs"
        tk = kl.shape[-2]
        in_specs += [pl.BlockSpec((1, 1, n_lat, tk, 256), lambda bi, h, i: (bi, h, 0, 0, 0)),
                     pl.BlockSpec((1, 1, n_lat, MLA_VT, tk), lambda bi, h, i: (bi, h, 0, 0, 0))]
        args += [kl, vtl]
        scratch = [pltpu.VMEM((2, tk, tq), F32)] + scratch
    return pl.pallas_call(
        functools.partial(_flash_kernel, n_lat=n_lat), grid=(b, hh, n // tq),
        in_specs=in_specs,
        out_specs=pl.BlockSpec((1, tq, MLA_V), lambda bi, h, i: (bi, i, h)),
        out_shape=jax.ShapeDtypeStruct((b, n, hh * MLA_V), BF16),
        scratch_shapes=scratch,
        compiler_params=_cp("parallel", "parallel", "arbitrary"), name="flash",
    )(*args)


def _mla_layer(xl, xc, mods_l, mods_c, pre_g, fin, w_in, q_g, kv_g, w_uq, w_ukv, w_o, b, n, c):
    d = xl.shape[-1]
    ql, kvl = q_g.shape[-1], kv_g.shape[-1]
    hh = MLA_HEADS
    rope_cols = w_in[:, ql + kvl:]
    w_in_ext = jnp.concatenate([w_in, rope_cols[:, _ROPE_SWAP]], axis=1).astype(BF16)
    wq = w_uq.reshape(ql, hh, MLA_NOPE + MLA_ROPE)
    w_uq_ext = jnp.concatenate([wq, wq[:, :, MLA_NOPE:][:, :, _ROPE_SWAP]], axis=-1)
    w_uq_ext = w_uq_ext.reshape(ql, hh * 256).astype(BF16)
    w_ukv_b = w_ukv.astype(BF16)
    w_o_b = w_o.astype(BF16)

    cp, sp = _rope_tables(n)
    z64l, o64l = jnp.zeros((n, 64), F32), jnp.ones((n, 64), F32)
    z64c, o64c = jnp.zeros((c, 64), F32), jnp.ones((c, 64), F32)
    cat = lambda a, bb: jnp.concatenate([a, bb], axis=1)
    tabs_l = (cat(cp, z64l), cat(sp, z64l), cat(cp, z64l), cat(sp, o64l))
    tabs_c = (cat(z64c, z64c), cat(z64c, o64c), cat(z64c, z64c), cat(z64c, o64c))

    tm_l = min(512, n)
    tk_l = min(512, n // 2)
    ql_, kl, vtl = _mla_side(xl, mods_l, pre_g, w_in_ext, q_g, kv_g, w_uq_ext, w_ukv_b, tabs_l,
                             b, n, tm_l, tk_l, True)
    qc_, kc, vtc = _mla_side(xc, mods_c, pre_g, w_in_ext, q_g, kv_g, w_uq_ext, w_ukv_b, tabs_c,
                             b, c, c, c, True)
    o_lat = _flash_call(ql_, kc, vtc, kl, vtl, min(1024, n)).reshape(b * n, hh * MLA_V)
    o_ctx = _flash_call(qc_, kc, vtc, None, None, c).reshape(b * c, hh * MLA_V)
    zb = jnp.zeros((d,), F32)
    xl = fin(xl, o_lat, mods_l, w_o_b, zb, tm_l, n)
    xc = fin(xc, o_ctx, mods_c, w_o_b, zb, c, c)
    return xl, xc


def _hy_in_kernel(x_ref, xp_ref, xn_ref, mod_ref, g_ref, w_ref, b_ref, cw_ref, cb_ref,
                  g0_ref, vg_ref, *, tpb):
    i = pl.program_id(0)
    g = g_ref[...]
    shift, scale = mod_ref[0, 0:1, :], mod_ref[0, 1:2, :]
    xcat = jnp.concatenate([xp_ref[...], x_ref[...], xn_ref[...]], axis=0)
    ucat = _dot(_normmod(xcat, g, shift, scale).astype(BF16), w_ref[...]) + b_ref[...]
    tm = x_ref.shape[0]
    u = ucat[8:tm + 8]
    first = (i % tpb) == 0
    last = (i % tpb) == tpb - 1
    prev_row = jnp.where(first, 0.0, ucat[7:8, :])
    next_row = jnp.where(last, 0.0, ucat[tm + 8:tm + 9, :])
    ridx = lax.broadcasted_iota(jnp.int32, (tm, 1), 0)
    dn = jnp.where(ridx == 0, prev_row, pltpu.roll(u, 1, 0))
    upw = jnp.where(ridx == tm - 1, next_row, pltpu.roll(u, tm - 1, 0))
    conv = cb_ref[...] + dn * cw_ref[0:1, :] + u * cw_ref[1:2, :] + upw * cw_ref[2:3, :]
    d = g0_ref.shape[-1]
    g0_ref[...] = conv[:, :d]
    vg_ref[...] = conv[:, 2 * d:] * conv[:, d:2 * d]


def _hy_in_call(x, mods, pre_g, w_in, b_in, conv_w, conv_b, tm, n):
    m, d = x.shape
    p = w_in.shape[1]
    tpb = n // tm
    x_spec, mod_spec = _row_specs(tm, d, tpb)
    r8 = tm // 8
    nb8 = m // 8
    prev_spec = pl.BlockSpec((8, d), lambda i: (jnp.maximum(i * r8 - 1, 0), 0))
    next_spec = pl.BlockSpec((8, d), lambda i: (jnp.minimum((i + 1) * r8, nb8 - 1), 0))
    cw = jnp.pad(conv_w, ((0, 8 - conv_w.shape[0]), (0, 0)))
    return pl.pallas_call(
        functools.partial(_hy_in_kernel, tpb=tpb), grid=(m // tm,),
        in_specs=[x_spec, prev_spec, next_spec, mod_spec, _const_spec((1, d)), _const_spec((d, p)),
                  _const_spec((1, p)), _const_spec((8, p)), _const_spec((1, p))],
        out_specs=[x_spec, x_spec],
        out_shape=[jax.ShapeDtypeStruct((m, d), F32), jax.ShapeDtypeStruct((m, d), F32)],
        compiler_params=_cp("parallel"), name="hy_in",
    )(x, x, x, mods, pre_g.reshape(1, d), w_in, b_in.reshape(1, p), cw, conv_b.reshape(1, p))


_PI_SPLIT = (3.140625, 9.67502593994140625e-4, 1.509957990978376432e-7)
_SIN_TAYLOR = (-1.0 / 6, 1.0 / 120, -1.0 / 5040, 1.0 / 362880, -1.0 / 39916800)


def _sin(x):
    kf = jnp.round(x * (1.0 / math.pi))
    r = ((x - kf * _PI_SPLIT[0]) - kf * _PI_SPLIT[1]) - kf * _PI_SPLIT[2]
    r2 = r * r
    p = _SIN_TAYLOR[4]
    for c in _SIN_TAYLOR[3::-1]:
        p = p * r2 + c
    s = r + r * r2 * p
    odd = (kf.astype(jnp.int32) & 1) == 1
    return jnp.where(odd, -s, s)


def _hy_filter_kernel(bands_ref, w1_ref, b1_ref, fq_ref, w2_ref, b2_ref, w3_ref, dl_ref,
                      k_ref, nrm_ref, *, n, tr):
    i = pl.program_id(0)
    bwd = i >= n // tr
    row = lax.broadcasted_iota(jnp.int32, (tr, LANE), 0) + i * tr
    j = jnp.where(bwd, 2 * n - row, row).astype(F32)
    lane = lax.broadcasted_iota(jnp.int32, (tr, LANE), 1)
    t = j * (1.0 / (n - 1))
    arg = (2.0 * math.pi / n) * j * bands_ref[0:1, :] + bands_ref[1:2, :]
    z = jnp.where(lane == 0, t, jnp.where(lane <= 2 * HYENA_BANDS, _sin(arg), 0.0))
    fq = fq_ref[...]
    a = _sin(fq * (_dot_hi(z, w1_ref[...]) + b1_ref[...]))
    for k in range(w2_ref.shape[0]):
        a = _sin(fq * (_dot_hi(a, w2_ref[k]) + b2_ref[k]))
    h = _dot_hi(a, w3_ref[jnp.where(bwd, 1, 0)])
    decay = jnp.exp(-t[:, 0:1] * dl_ref[...])
    k = h * decay
    k = jnp.where(row[:, 0:1] == n, 0.0, k)
    k_ref[...] = k
    part = jnp.sum(jnp.abs(k), axis=0, keepdims=True)

    @pl.when(i == 0)
    def _():
        nrm_ref[...] = jnp.zeros_like(nrm_ref)

    nrm_ref[...] += jnp.broadcast_to(part, nrm_ref.shape)


def _hy_filter_call(n, d, f_w1, f_b1, f_freq, f_w2, f_b2, f_w3):
    fw = f_w1.shape[1]
    tr = min(512, n)
    bands_np = np.zeros((8, LANE), np.float32)
    bands_np[0, 1:1 + HYENA_BANDS] = np.linspace(1e-4, HYENA_BANDS - 1, HYENA_BANDS, dtype=np.float32)
    bands_np[0, 1 + HYENA_BANDS:1 + 2 * HYENA_BANDS] = bands_np[0, 1:1 + HYENA_BANDS]
    bands_np[1, 1:1 + HYENA_BANDS] = 0.5 * np.pi
    bands_np[1, 1 + HYENA_BANDS:1 + 2 * HYENA_BANDS] = np.pi
    w1p = jnp.zeros((LANE, fw), F32).at[:f_w1.shape[0]].set(f_w1)
    deltas = jnp.abs(jnp.linspace(math.log(HYENA_TARGET) / HYENA_SLOW, math.log(HYENA_TARGET) / HYENA_FAST,
                                  d, dtype=F32)).reshape(1, d)
    row = pl.BlockSpec((tr, d), lambda i: (i, 0))
    return pl.pallas_call(
        functools.partial(_hy_filter_kernel, n=n, tr=tr), grid=(2 * n // tr,),
        in_specs=[_const_spec((8, LANE)), _const_spec((LANE, fw)), _const_spec((1, fw)), _const_spec((1, fw)),
                  _const_spec(f_w2.shape), _const_spec((f_w2.shape[0], 1, fw)), _const_spec((2, fw, d)),
                  _const_spec((1, d))],
        out_specs=[row, _const_spec((8, d))],
        out_shape=[jax.ShapeDtypeStruct((2 * n, d), F32), jax.ShapeDtypeStruct((8, d), F32)],
        compiler_params=_cp("arbitrary"), name="hy_filter",
    )(jnp.asarray(bands_np), w1p, f_b1.reshape(1, fw), f_freq.reshape(1, fw), f_w2,
      f_b2.reshape(f_w2.shape[0], 1, fw), jnp.transpose(f_w3.reshape(fw, 2, d), (1, 0, 2)), deltas)


def _dft_cs(nf, nt, period):
    ft = (np.arange(nf)[:, None] * np.arange(nt)[None, :]) % period
    ang = 2.0 * np.pi * ft / period
    return np.cos(ang), np.sin(ang)


def _twiddle_tables(n1, n2, lead_t2):
    nn = n1 * n2
    f1 = jnp.arange(n1, dtype=jnp.int32)
    t2 = jnp.arange(n2, dtype=jnp.int32)
    idx = (t2[:, None] * f1[None, :]) % nn if lead_t2 else (f1[:, None] * t2[None, :]) % nn
    ang = idx.astype(F32) * (2.0 * math.pi / nn)
    shape = idx.shape + (LANE,)
    return (jnp.broadcast_to(jnp.cos(ang)[..., None], shape),
            jnp.broadcast_to(jnp.sin(ang)[..., None], shape))


KRON_R = 16
KRON_W = 256


def _cblock(mc):
    return np.block([[mc.real, -mc.imag], [mc.imag, mc.real]])


def _split_radix(n1):
    b = 16 if (n1 % 16 == 0 and n1 >= 64) else 4
    assert n1 % b == 0 and (n1 // b) % 2 == 0
    return n1 // b, b


def _slab_dft_mats(n1, a_in, a_out, sign, neg_im=False, real_in=False, scale=1.0):
    a, b = _split_radix(n1)
    r = KRON_R
    eye = np.eye(r)
    w = lambda num, den: np.exp(sign * 2j * np.pi * num / den)
    ua = np.arange(a)[:, None] * np.arange(a)[None, :]
    vb = np.arange(b)[:, None] * np.arange(b)[None, :]
    if sign < 0:
        m1 = np.kron(w(ua[:, :a_in], a) * scale, eye)
        l1 = _cblock(m1)
        if neg_im:
            l1[:, a_in * r:] *= -1.0
        if real_in:
            l1 = l1[:, :a_in * r]
        l2 = np.stack([_cblock(np.kron(w(vb, b) * w(u * np.arange(b)[None, :], n1), eye)) for u in range(a)])
    else:
        l1 = np.stack([_cblock(np.kron(w(vb, b) * w(u * np.arange(b)[:, None], n1), eye)) for u in range(a)])
        l2 = _cblock(np.kron(w(ua[:a_out, :], a) * scale, eye))
    return jnp.asarray(l1, BF16), jnp.asarray(l2, BF16)


def _kfa_kernel(*refs, nparts, a_in, a, b):
    l1_ref, l2_ref = refs[0], refs[1]
    parts = refs[2:2 + nparts]
    twc_ref, tws_ref, or_ref, oi_ref = refs[2 + nparts:]
    r = KRON_R
    reps = or_ref.shape[-1] // LANE
    l1 = l1_ref[...]
    y1 = []
    for bb in range(b):
        x = jnp.concatenate([p[0, aa * b + bb] for p in parts for aa in range(a_in)], axis=0).astype(BF16)
        y1.append(_dot(l1, x).astype(BF16))
    for u in range(a):
        x = jnp.concatenate([y1[bb][u * r:(u + 1) * r] for bb in range(b)]
                            + [y1[bb][(a + u) * r:(a + u + 1) * r] for bb in range(b)], axis=0)
        z = _dot(l2_ref[u], x)
        for v in range(b):
            f1 = u + a * v
            zr = z[v * r:(v + 1) * r]
            zi = z[(b + v) * r:(b + v + 1) * r]
            c = jnp.concatenate([twc_ref[f1]] * reps, axis=1)
            sn = jnp.concatenate([tws_ref[f1]] * reps, axis=1)
            or_ref[0, f1] = (zr * c + zi * sn).astype(or_ref.dtype)
            oi_ref[0, f1] = (zi * c - zr * sn).astype(oi_ref.dtype)


def _kfa_call(parts, nb_out, n1, n2, d, l1, l2, twc, tws):
    a, b = _split_radix(n1)
    a_in = parts[0][0].shape[1] // b
    r, w = KRON_R, min(KRON_W, d)
    in_specs = [_const_spec(l1.shape), pl.BlockSpec(l2.shape, lambda bi, j, k: (0, 0, 0), pipeline_mode=pl.Buffered(1))]
    args = [l1, l2]
    for arr, bi_fixed in parts:
        t1_in = arr.shape[1]
        if bi_fixed is None:
            in_specs.append(pl.BlockSpec((1, t1_in, r, w), lambda bi, j, k: (bi, 0, j, k)))
        else:
            in_specs.append(pl.BlockSpec((1, t1_in, r, w), lambda bi, j, k, f=bi_fixed: (f, 0, j, k)))
        args.append(arr)
    tw_spec = pl.BlockSpec((n1, r, LANE), lambda bi, j, k: (0, j, 0))
    out_spec = pl.BlockSpec((1, n1, r, w), lambda bi, j, k: (bi, 0, j, k))
    out = jax.ShapeDtypeStruct((nb_out, n1, n2, d), BF16)
    return pl.pallas_call(
        functools.partial(_kfa_kernel, nparts=len(parts), a_in=a_in, a=a, b=b),
        grid=(nb_out, n2 // r, d // w),
        in_specs=in_specs + [tw_spec, tw_spec], out_specs=[out_spec, out_spec], out_shape=[out, out],
        compiler_params=_cp("parallel", "parallel", "parallel"), name="kfa",
    )(*args, twc, tws)


def _kfc_kernel(l3_ref, l4_ref, gr_ref, gi_ref, vg_ref, g0_ref, skip_ref, o_ref, *, a, b, a_out):
    r = KRON_R
    y3 = []
    for u in range(a):
        x = jnp.concatenate([gr_ref[0, u + a * v] for v in range(b)]
                            + [gi_ref[0, u + a * v] for v in range(b)], axis=0)
        y3.append(_dot(l3_ref[u], x).astype(BF16))
    l4 = l4_ref[...]
    skip = skip_ref[...]
    for bb in range(b):
        x = jnp.concatenate([y3[u][bb * r:(bb + 1) * r] for u in range(a)]
                            + [y3[u][(b + bb) * r:(b + bb + 1) * r] for u in range(a)], axis=0)
        y = _dot(l4, x)
        for sg in range(2):
            for aa in range(a_out):
                t1 = aa * b + bb
                yb = y[(sg * a_out + aa) * r:(sg * a_out + aa + 1) * r]
                o_ref[sg, t1] = ((yb + vg_ref[sg, t1] * skip) * g0_ref[sg, t1]).astype(o_ref.dtype)


def _kfc_call(g1r, g1i, vg4, g04, skip, n1, n2, d, l3, l4):
    a, b = _split_radix(n1)
    nb, t1_out = vg4.shape[:2]
    r, w = KRON_R, min(KRON_W, d)
    slab = pl.BlockSpec((1, n1, r, w), lambda j, k: (0, 0, j, k))
    nat = pl.BlockSpec((nb, t1_out, r, w), lambda j, k: (0, 0, j, k))
    return pl.pallas_call(
        functools.partial(_kfc_kernel, a=a, b=b, a_out=t1_out // b), grid=(n2 // r, d // w),
        in_specs=[pl.BlockSpec(l3.shape, lambda j, k: (0, 0, 0), pipeline_mode=pl.Buffered(1)),
                  _const_spec(l4.shape), slab, slab, nat, nat, pl.BlockSpec((1, w), lambda j, k: (0, k))],
        out_specs=nat, out_shape=jax.ShapeDtypeStruct(vg4.shape, BF16),
        compiler_params=_cp("parallel", "parallel"), name="kfc",
    )(l3, l4, g1r, g1i, vg4, g04, skip.reshape(1, d))


def _hy_kb_kernel(fm_ref, fr_ref, fi_ref, sc_ref, kr_ref, ki_ref, *, nf):
    fm = fm_ref[...]
    n2 = fr_ref.shape[1]
    sc = sc_ref[...]
    for s in range(nf):
        kk = _dot(fm, jnp.concatenate([fr_ref[s], fi_ref[s]], axis=0))
        kr_ref[s] = kk[:n2] * sc
        ki_ref[s] = kk[n2:] * sc


def _hy_b_kernel(fm_ref, fmc_ref, xr_ref, xi_ref, kr_ref, ki_ref, twc_ref, tws_ref,
                 or_ref, oi_ref, *, nf, d):
    fm = fm_ref[...]
    fmc = fmc_ref[...]
    n2 = xr_ref.shape[1]
    for s in range(nf):
        x = _dot(fm, jnp.concatenate([xr_ref[s], xi_ref[s]], axis=0))
        xr, xi = x[:n2], x[n2:]
        kr, ki = kr_ref[s], ki_ref[s]
        yr = (xr * kr - xi * ki).astype(BF16)
        yi = (xr * ki + xi * kr).astype(BF16)
        g = _dot(fmc, jnp.concatenate([yr, yi], axis=0))
        gr, gi = g[:n2], g[n2:]
        c = jnp.concatenate([twc_ref[s]] * (d // LANE), axis=1)
        sn = jnp.concatenate([tws_ref[s]] * (d // LANE), axis=1)
        or_ref[s] = (gr * c - gi * sn).astype(or_ref.dtype)
        oi_ref[s] = (gi * c + gr * sn).astype(oi_ref.dtype)


def _block_c(cs, sn, sign):
    return np.block([[cs, -sign * sn], [sign * sn, cs]])


def _hy_conv_long(vg, g0, k, nrm, skip, b, n, d):
    n2 = FFT_N2
    nn = 2 * n
    n1 = nn // n2
    rows = n // n2
    a, _ = _split_radix(n1)
    la1, la2 = _slab_dft_mats(n1, a // 2, 0, -1)
    lk1, lk2 = _slab_dft_mats(n1, a, 0, -1, real_in=True)
    lc3, lc4 = _slab_dft_mats(n1, 0, a // 2, +1)
    cs2, sn2 = _dft_cs(n2, n2, n2)
    fm_b = jnp.asarray(_block_c(cs2, sn2, -1.0), BF16)
    fm_bc = jnp.asarray(_block_c(cs2, sn2, 1.0), BF16)
    twc, tws = _twiddle_tables(n1, n2, lead_t2=False)

    kfr, kfi = _kfa_call([(k.reshape(1, n1, n2, d), 0)], 1, n1, n2, d, lk1, lk2, twc, tws)
    nf = 4 if n1 % 4 == 0 else 1
    scale = (1.0 / (nrm[0:1, :] * nn))
    slab = pl.BlockSpec((nf, n2, d), lambda j: (j, 0, 0))
    shp3 = (n1, n2, d)
    khr, khi = pl.pallas_call(
        functools.partial(_hy_kb_kernel, nf=nf), grid=(n1 // nf,),
        in_specs=[_const_spec(fm_b.shape), slab, slab, _const_spec((1, d))],
        out_specs=[slab, slab],
        out_shape=[jax.ShapeDtypeStruct(shp3, F32)] * 2,
        compiler_params=_cp("parallel"), name="hy_kb",
    )(fm_b, kfr.reshape(shp3), kfi.reshape(shp3), scale)

    vg4 = vg.reshape(b, rows, n2, d)
    g04 = g0.reshape(b, rows, n2, d)
    x1r, x1i = _kfa_call([(vg4, 0), (vg4, 1)], 1, n1, n2, d, la1, la2, twc, tws)
    tw_slab = pl.BlockSpec((nf, n2, LANE), lambda j: (j, 0, 0))
    g1r, g1i = pl.pallas_call(
        functools.partial(_hy_b_kernel, nf=nf, d=d), grid=(n1 // nf,),
        in_specs=[_const_spec(fm_b.shape), _const_spec(fm_bc.shape), slab, slab, slab, slab,
                  tw_slab, tw_slab],
        out_specs=[slab, slab],
        out_shape=[jax.ShapeDtypeStruct(shp3, BF16)] * 2,
        compiler_params=_cp("parallel"), name="hy_b",
    )(fm_b, fm_bc, x1r.reshape(shp3), x1i.reshape(shp3), khr, khi, twc, tws)
    out = _kfc_call(g1r.reshape(1, n1, n2, d), g1i.reshape(1, n1, n2, d), vg4, g04, skip, n1, n2, d, lc3, lc4)
    return out.reshape(b * n, d)


def _hy_short_kernel(fa_ref, fk_ref, fi_ref, vg_ref, g0_ref, k_ref, nrm_ref, skip_ref, o_ref, *, n):
    z = jnp.concatenate([vg_ref[0], vg_ref[1]], axis=0)
    x = _dot_hi(fa_ref[...], z)
    kk = _dot_hi(fk_ref[...], k_ref[...])
    nn = 2 * n
    sc = 1.0 / (nrm_ref[0:1, :] * nn)
    xr, xi = x[:nn], x[nn:]
    kr, ki = kk[:nn] * sc, kk[nn:] * sc
    y = _dot_hi(fi_ref[...], jnp.concatenate([xr * kr - xi * ki, xr * ki + xi * kr], axis=0))
    skip = skip_ref[...]
    for bi in range(2):
        o_ref[bi] = ((y[bi * n:(bi + 1) * n] + vg_ref[bi] * skip) * g0_ref[bi]).astype(o_ref.dtype)


def _hy_conv_short(vg, g0, k, nrm, skip, b, n, d):
    nn = 2 * n
    cs, sn = _dft_cs(nn, n, nn)
    fa = jnp.asarray(_block_c(cs, sn, -1.0), F32)
    csk, snk = _dft_cs(nn, nn, nn)
    fk = jnp.asarray(np.concatenate([csk, -snk], axis=0), F32)
    csi, sni = _dft_cs(n, nn, nn)
    fi = jnp.asarray(_block_c(csi, sni, 1.0), F32)
    cb = 256
    col3 = pl.BlockSpec((b, n, cb), lambda j: (0, 0, j))
    vec = pl.BlockSpec((1, cb), lambda j: (0, j))
    out = pl.pallas_call(
        functools.partial(_hy_short_kernel, n=n), grid=(d // cb,),
        in_specs=[_const_spec(fa.shape), _const_spec(fk.shape), _const_spec(fi.shape), col3, col3,
                  pl.BlockSpec((nn, cb), lambda j: (0, j)), pl.BlockSpec((8, cb), lambda j: (0, j)), vec],
        out_specs=col3, out_shape=jax.ShapeDtypeStruct((b, n, d), BF16),
        compiler_params=_cp("parallel"), name="hy_short",
    )(fa, fk, fi, vg.reshape(b, n, d), g0.reshape(b, n, d), k, nrm, skip.reshape(1, d))
    return out.reshape(b * n, d)


def _hyena_layer(xl, xc, mods_l, mods_c, pre_g, fin, w_in, b_in, conv_w, conv_b, filt, skip,
                 w_out, b_out, b, n, c):
    d = xl.shape[-1]
    w_in_b = w_in.astype(BF16)
    w_out_b = w_out.astype(BF16)
    tm = min(512, n)
    g0, vg = _hy_in_call(xl, mods_l, pre_g, w_in_b, b_in, conv_w, conv_b, tm, n)
    k, nrm = _hy_filter_call(n, d, *filt)
    u_out = _hy_conv_long(vg, g0, k, nrm, skip[0], b, n, d)
    xl = fin(xl, u_out, mods_l, w_out_b, b_out, tm, n)

    g0c, vgc = _hy_in_call(xc, mods_c, pre_g, w_in_b, b_in, conv_w, conv_b, c, c)
    kc, nrmc = _hy_filter_call(c, d, *filt)
    u_out_c = _hy_conv_short(vgc, g0c, kc, nrmc, skip[0], b, c, d)
    xc = fin(xc, u_out_c, mods_c, w_out_b, b_out, c, c)
    return xl, xc


def _s5_operators(lam_re, lam_im, log_dt, b_re, b_im, c_re, c_im, d_skip):
    t = S5_T
    g, ns = lam_re.shape[1], lam_re.shape[2]
    gc = b_re.shape[-1]
    gl = LANE // gc
    nblk = g // gl
    lam = lax.complex(lam_re, lam_im)
    dt = jnp.exp(log_dt)[..., None]
    lam_bar = jnp.exp(lam * dt)
    b_bar = ((lam_bar - 1.0) / lam)[..., None] * lax.complex(b_re, b_im)
    c_mat = lax.complex(c_re, c_im)
    pw = jnp.arange(t + 1, dtype=F32)
    lam_pw = jnp.exp((lam * dt)[None] * pw[:, None, None, None])
    hp = HIGHEST
    kern = jnp.einsum('dgcn,tdgn,dgne->dgtce', c_mat, lam_pw[:t], b_bar, precision=hp).real
    dsk = d_skip.reshape(g, gc)
    kt = jnp.swapaxes(kern, -1, -2)
    centre = kt[0][:, 0] + kt[1][:, 0] + jnp.eye(gc, dtype=F32)[None] * dsk[:, :, None]
    ks = jnp.concatenate([kt[1][:, 1:][:, ::-1], centre[:, None], kt[0][:, 1:]], axis=1)
    eye_gl = jnp.eye(gl, dtype=F32)
    ks = ks.reshape(nblk, gl, 2 * t - 1, gc, gc)
    d_tab = jnp.einsum('bglec,gh->blgehc', ks, eye_gl).reshape(nblk, 2 * t - 1, LANE, LANE)

    def compact(zc, im_sign):
        z = jnp.concatenate([zc.real, im_sign * zc.imag], axis=-1)
        z = jnp.transpose(z, (2, 1, 0, 3, 4)).reshape(nblk, gl, t, 2, gc, 2 * ns)
        return jnp.transpose(z, (0, 2, 1, 3, 4, 5))

    b_t = jnp.swapaxes(b_bar, -1, -2)
    pf = lam_pw[:t][::-1][:, 0, :, None, :] * b_t[0][None]
    pb = lam_pw[:t][:, 1, :, None, :] * b_t[1][None]
    p_tab = compact(jnp.stack([pf, pb], axis=0), 1.0)
    qf = c_mat[0][None] * lam_pw[1:t + 1, 0][:, :, None, :]
    qb = c_mat[1][None] * lam_pw[1:t + 1][::-1][:, 1][:, :, None, :]
    q_tab = compact(jnp.stack([qf, qb], axis=0), -1.0)

    a = lam_pw[t]
    m_op, p_op, q_op = _s5_expand(d_tab, p_tab, q_tab)
    return m_op, p_op, q_op, a.real.reshape(2, g * ns), a.imag.reshape(2, g * ns)


def _s5_m_kernel(d_ref, o_ref, *, t):
    s = pl.program_id(1)
    for tt in range(t):
        o_ref[0, :, tt * LANE:(tt + 1) * LANE] = d_ref[0, tt - s + t - 1].astype(o_ref.dtype)


def _s5_pq_kernel(c_ref, o_ref, *, transpose):
    gl, nd, gc, w = c_ref.shape[2:]
    ns = w // 2
    lane_grp = lax.broadcasted_iota(jnp.int32, (gc, gl * ns), 1) // ns
    rows = []
    for g in range(gl):
        cols = []
        for dd in range(nd):
            piece = c_ref[0, 0, g, dd]
            for ri in range(2):
                tiled = jnp.concatenate([piece[:, ri * ns:(ri + 1) * ns]] * gl, axis=1)
                cols.append(jnp.where(lane_grp == g, tiled, 0.0))
        rows.append(jnp.concatenate(cols, axis=1))
    blk = jnp.concatenate(rows, axis=0)
    o_ref[0] = (blk.T if transpose else blk).astype(o_ref.dtype)


def _s5_expand(d_tab, p_tab, q_tab):
    nblk, nlag = d_tab.shape[:2]
    t = (nlag + 1) // 2
    _, _, gl, nd, gc, w = p_tab.shape
    ncol = nd * gl * w
    m_op = pl.pallas_call(
        functools.partial(_s5_m_kernel, t=t), grid=(nblk, t),
        in_specs=[pl.BlockSpec((1, nlag, LANE, LANE), lambda b, s: (b, 0, 0, 0))],
        out_specs=pl.BlockSpec((1, LANE, t * LANE), lambda b, s: (b, s, 0)),
        out_shape=jax.ShapeDtypeStruct((nblk, t * LANE, t * LANE), BF16),
        compiler_params=_cp("parallel", "parallel"), name="s5_m_op",
    )(d_tab)
    tab_spec = pl.BlockSpec((1, 1, gl, nd, gc, w), lambda b, j: (b, j, 0, 0, 0, 0))
    p_op = pl.pallas_call(
        functools.partial(_s5_pq_kernel, transpose=False), grid=(nblk, t),
        in_specs=[tab_spec],
        out_specs=pl.BlockSpec((1, LANE, ncol), lambda b, j: (b, j, 0)),
        out_shape=jax.ShapeDtypeStruct((nblk, t * LANE, ncol), BF16),
        compiler_params=_cp("parallel", "parallel"), name="s5_p_op",
    )(p_tab)
    q_op = pl.pallas_call(
        functools.partial(_s5_pq_kernel, transpose=True), grid=(nblk, t),
        in_specs=[tab_spec],
        out_specs=pl.BlockSpec((1, ncol, LANE), lambda b, j: (b, 0, j)),
        out_shape=jax.ShapeDtypeStruct((nblk, ncol, t * LANE), BF16),
        compiler_params=_cp("parallel", "parallel"), name="s5_q_op",
    )(q_tab)
    return m_op, p_op, q_op


def _s5_sum_kernel(*refs, t):
    u_refs = refs[:t]
    p_ref = refs[t]
    outs = refs[t + 1:]
    u = jnp.concatenate([r[...] for r in u_refs], axis=1)
    res = _dot(u, p_ref[0])
    w = res.shape[1] // len(outs)
    for i, o in enumerate(outs):
        o[...] = res[:, i * w:(i + 1) * w]


def _s5_sum_call(h, p_op, rb):
    rows = h.shape[0]
    t = S5_T
    nblk = p_op.shape[0]
    w = p_op.shape[2] // 4
    u_specs = [pl.BlockSpec((rb, LANE), lambda gb, r, s=s: (r, s * nblk + gb)) for s in range(t)]
    out_spec = pl.BlockSpec((rb, w), lambda gb, r: (r, gb))
    return pl.pallas_call(
        functools.partial(_s5_sum_kernel, t=t), grid=(nblk, rows // rb),
        in_specs=u_specs + [pl.BlockSpec((1,) + p_op.shape[1:], lambda gb, r: (gb, 0, 0))],
        out_specs=[out_spec] * 4,
        out_shape=[jax.ShapeDtypeStruct((rows, nblk * w), F32)] * 4,
        compiler_params=_cp("parallel", "parallel"), name="s5_sum",
    )(*([h] * t), p_op)


def _s5_rec_kernel(sr_ref, si_ref, ar_ref, ai_ref, h0r_ref, h0i_ref, hr_ref, hi_ref, fr_ref, fi_ref,
                   cr, ci, *, kb, reverse):
    @pl.when(pl.program_id(1) == 0)
    def _():
        cr[...] = h0r_ref[0]
        ci[...] = h0i_ref[0]

    ar, ai = ar_ref[...], ai_ref[...]

    def body(i, carry):
        hr, hi = carry
        k = kb - 1 - i if reverse else i
        hr_ref[pl.ds(k, 1), :] = hr
        hi_ref[pl.ds(k, 1), :] = hi
        nr = ar * hr - ai * hi + sr_ref[pl.ds(k, 1), :]
        ni = ar * hi + ai * hr + si_ref[pl.ds(k, 1), :]
        return nr, ni

    hr, hi = lax.fori_loop(0, kb, body, (cr[...], ci[...]))
    cr[...] = hr
    ci[...] = hi
    fr_ref[0] = hr
    fi_ref[0] = hi


def _s5_rec_call(sr, si, ar, ai, h0r, h0i, nb_batch, reverse):
    rows, w = sr.shape
    nk = rows // nb_batch
    kb = min(64, nk)
    nb = nk // kb
    blk = (lambda bi, i: (bi * nb + nb - 1 - i, 0)) if reverse else (lambda bi, i: (bi * nb + i, 0))
    row_spec = pl.BlockSpec((kb, w), blk)
    vec = _const_spec((1, w))
    st = pl.BlockSpec((1, 1, w), lambda bi, i: (bi, 0, 0))
    return pl.pallas_call(
        functools.partial(_s5_rec_kernel, kb=kb, reverse=reverse), grid=(nb_batch, nb),
        in_specs=[row_spec, row_spec, vec, vec, st, st],
        out_specs=[row_spec, row_spec, st, st],
        out_shape=[jax.ShapeDtypeStruct((rows, w), F32)] * 2 + [jax.ShapeDtypeStruct((nb_batch, 1, w), F32)] * 2,
        scratch_shapes=[pltpu.VMEM((1, w), F32), pltpu.VMEM((1, w), F32)],
        compiler_params=_cp("parallel", "arbitrary"), name="s5_rec",
    )(sr, si, ar, ai, h0r, h0i)


def _s5_out_kernel(*refs, t):
    u_refs = refs[:t]
    h_refs = refs[t:t + 4]
    m_ref, q_ref, o_ref = refs[t + 4:]
    u = jnp.concatenate([r[...] for r in u_refs], axis=1)
    hcat = jnp.concatenate([r[...].astype(BF16) for r in h_refs], axis=1)
    y = _dot(u, m_ref[0]) + _dot(hcat, q_ref[0])
    o_ref[0] = (0.5 * y * (1.0 + lax.erf(y * (2.0 ** -0.5)))).astype(o_ref.dtype)


def _s5_out_call(h, states, m_op, q_op, rb):
    rows = h.shape[0]
    t = S5_T
    nblk = m_op.shape[0]
    w = q_op.shape[1] // 4
    u_specs = [pl.BlockSpec((rb, LANE), lambda gb, r, s=s: (r, s * nblk + gb)) for s in range(t)]
    return pl.pallas_call(
        functools.partial(_s5_out_kernel, t=t), grid=(nblk, rows // rb),
        in_specs=u_specs + [pl.BlockSpec((rb, w), lambda gb, r: (r, gb))] * 4
        + [pl.BlockSpec((1,) + m_op.shape[1:], lambda gb, r: (gb, 0, 0)),
           pl.BlockSpec((1,) + q_op.shape[1:], lambda gb, r: (gb, 0, 0))],
        out_specs=pl.BlockSpec((1, rb, t * LANE), lambda gb, r: (gb, r, 0)),
        out_shape=jax.ShapeDtypeStruct((nblk, rows, t * LANE), BF16),
        compiler_params=_cp("parallel", "parallel"), name="s5_out",
    )(*([h] * t), *states, m_op, q_op)


def _s5_layer(xl, xc, mods_l, mods_c, pre_g, fin, lam_re, lam_im, log_dt, b_re, b_im, c_re, c_im,
              d_skip, w_glu, b_glu, b, n, c):
    d = xl.shape[-1]
    t = S5_T
    m_op, p_op, q_op, a_re, a_im = _s5_operators(lam_re, lam_im, log_dt, b_re, b_im, c_re, c_im, d_skip)
    nblk = m_op.shape[0]
    w = a_re.shape[-1]
    hl = _normmod_call(xl, mods_l, pre_g, min(512, n), n).reshape(b * n // t, t * d)
    hc = _normmod_call(xc, mods_c, pre_g, c, c).reshape(b * c // t, t * d)

    def scan(h, init):
        sfr, sfi, sbr, sbi = _s5_sum_call(h, p_op, min(512, h.shape[0]))
        hfr, hfi, ffr, ffi = _s5_rec_call(sfr, sfi, a_re[0:1], a_im[0:1], init[0], init[1], b, False)
        hbr, hbi, fbr, fbi = _s5_rec_call(sbr, sbi, a_re[1:2], a_im[1:2], init[2], init[3], b, True)
        return (hfr, hfi, hbr, hbi), (ffr, ffi, fbr, fbi)

    zeros = jnp.zeros((b, 1, w), F32)
    _, ctx_final = scan(hc, (zeros,) * 4)
    states, _ = scan(hl, ctx_final)
    nk = n // t
    gact = _s5_out_call(hl, states, m_op, q_op, min(512, b * nk))
    g_nat = jnp.transpose(gact.reshape(nblk, b * nk, t, LANE), (1, 2, 0, 3)).reshape(b * n, d)
    return fin(xl, g_nat, mods_l, w_glu.astype(BF16), b_glu, min(512, n), n, glu=True)


def _fn_pre_kernel(x_ref, mod_ref, g_ref, cs_ref, a_ref, b_ref):
    h = _normmod(x_ref[...], g_ref[...], mod_ref[0, 0:1, :], mod_ref[0, 1:2, :]).astype(BF16)
    cs = cs_ref[...]
    gc = FNET_GC
    ab = [_dot(h[:, k * gc:(k + 1) * gc], cs) for k in range(h.shape[1] // gc)]
    a_ref[...] = jnp.concatenate([z[:, :gc] for z in ab], axis=1).astype(a_ref.dtype)
    b_ref[...] = jnp.concatenate([z[:, gc:] for z in ab], axis=1).astype(b_ref.dtype)


def _fn_c_kernel(l5_ref, l6_ref, xr_ref, xi_ref, o_ref, *, nh):
    r = KRON_R
    n2 = nh * r
    l5 = l5_ref[...]
    y5 = [_dot(l5, jnp.concatenate([xr_ref[0, f], xi_ref[0, f]], axis=0)).astype(BF16) for f in range(r)]
    l6 = l6_ref[...]
    for p in range(nh):
        x = jnp.concatenate([y5[f][p * r:(p + 1) * r] for f in range(r)]
                            + [y5[f][n2 + p * r:n2 + (p + 1) * r] for f in range(r)], axis=0)
        out = _dot(l6, x)
        for q in range(r):
            o_ref[0, p + nh * q] = out[q * r:(q + 1) * r].astype(o_ref.dtype)


def _fnet_layer(xl, mods_l, pre_g, fin, w_o, b_o, b, n, d):
    n2 = FFT_N2
    n1 = n // n2
    gc = FNET_GC
    cc, sc = _dft_cs(gc, gc, gc)
    cs = jnp.asarray(np.concatenate([cc, sc], axis=1) / np.sqrt(gc), BF16)
    tm = min(512, n)
    x_spec, mod_spec = _row_specs(tm, d, n // tm)
    ab = jax.ShapeDtypeStruct((b * n, d), BF16)
    am, bm = pl.pallas_call(
        _fn_pre_kernel, grid=(b * n // tm,),
        in_specs=[x_spec, mod_spec, _const_spec((1, d)), _const_spec(cs.shape)],
        out_specs=[x_spec, x_spec], out_shape=[ab, ab],
        compiler_params=_cp("parallel"), name="fn_pre",
    )(xl, mods_l, pre_g.reshape(1, d), cs)

    a, _ = _split_radix(n1)
    l1, l2 = _slab_dft_mats(n1, a, 0, -1, neg_im=True, scale=1.0 / np.sqrt(n))
    twc, tws = _twiddle_tables(n1, n2, lead_t2=False)
    a4, b4 = am.reshape(b, n1, n2, d), bm.reshape(b, n1, n2, d)
    xr, xi = _kfa_call([(a4, None), (b4, None)], b, n1, n2, d, l1, l2, twc, tws)

    r, w = KRON_R, min(KRON_W, d)
    nh = n2 // r
    assert n1 % r == 0 and n2 % r == 0
    m5 = np.zeros((n2, n2), np.complex128)
    m6 = np.zeros((r * r, r * r), np.complex128)
    for s in range(r):
        for p in range(nh):
            for h in range(nh):
                m5[p * r + s, h * r + s] = np.exp(-2j * np.pi * (p * h / nh + p * s / n2))
        for q in range(r):
            for f in range(r):
                m6[q * r + f, f * r + s] = np.exp(-2j * np.pi * q * s / r)
    l5 = jnp.asarray(_cblock(m5), BF16)
    l6 = jnp.asarray(np.concatenate([m6.real, -m6.imag], axis=1), BF16)
    grp = pl.BlockSpec((1, r, n2, w), lambda bi, fh, k: (bi, fh, 0, k))
    y = pl.pallas_call(
        functools.partial(_fn_c_kernel, nh=nh), grid=(b, n1 // r, d // w),
        in_specs=[_const_spec(l5.shape), _const_spec(l6.shape), grp, grp],
        out_specs=pl.BlockSpec((1, n2, None, r, w), lambda bi, fh, k: (bi, 0, fh, 0, k)),
        out_shape=jax.ShapeDtypeStruct((b, n2, n1 // r, r, d), BF16),
        compiler_params=_cp("parallel", "parallel", "parallel"), name="fn_c",
    )(l5, l6, xr, xi)
    return fin(xl, y.reshape(b * n, d), mods_l, w_o.astype(BF16), b_o, tm, n)


def kernel(x, c, ctx, c_ctx, mod_w, mod_b, mix_pre_g, mix_post_g, ffn_pre_g, ffn_post_g, ffn_w13, ffn_w2,
           mla_w_in, mla_q_norm_g, mla_kv_norm_g, mla_w_uq, mla_w_ukv, mla_w_o,
           hy_w_in, hy_b_in, hy_conv_w, hy_conv_b, hy_f_w1, hy_f_b1, hy_f_freq, hy_f_w2, hy_f_b2, hy_f_w3,
           hy_skip, hy_w_out, hy_b_out,
           s5_lambda_re, s5_lambda_im, s5_log_dt, s5_b_re, s5_b_im, s5_c_re, s5_c_im, s5_d, s5_w_glu, s5_b_glu,
           fn_w_o, fn_b_o):
    b, n, d = x.shape
    cl = ctx.shape[1]
    depth = mod_w.shape[0]
    assert b == 2 and depth == 4, "two batches ride one complex transform; one layer per mixer"
    mods = _mods(c, c_ctx, mod_w, mod_b)
    xl = x.reshape(b * n, d)
    xc = ctx.reshape(b * cl, d)
    tm = min(512, n)

    def finisher(i):
        w13 = ffn_w13[i].astype(BF16)
        w2 = ffn_w2[i].astype(BF16)

        def fin(x_, y_, mods_, wm, bm, tm_, rows_per_batch, glu=False):
            return _mix_ffn_call(x_, y_, mods_, wm, bm, mix_post_g[i], ffn_pre_g[i], ffn_post_g[i], w13, w2,
                                 tm_, rows_per_batch, glu)
        return fin

    def mods_c(i):
        return jnp.broadcast_to(mods[i, 2:3], (b, 8, d))

    xl, xc = _mla_layer(xl, xc, mods[0, 0:2], mods_c(0), mix_pre_g[0], finisher(0), mla_w_in[0],
                        mla_q_norm_g[0], mla_kv_norm_g[0], mla_w_uq[0], mla_w_ukv[0], mla_w_o[0], b, n, cl)
    filt = (hy_f_w1[0], hy_f_b1[0], hy_f_freq[0], hy_f_w2[0], hy_f_b2[0], hy_f_w3[0])
    xl, xc = _hyena_layer(xl, xc, mods[1, 0:2], mods_c(1), mix_pre_g[1], finisher(1), hy_w_in[0], hy_b_in[0],
                          hy_conv_w[0], hy_conv_b[0], filt, hy_skip[0], hy_w_out[0], hy_b_out[0], b, n, cl)
    xl = _s5_layer(xl, xc, mods[2, 0:2], mods_c(2), mix_pre_g[2], finisher(2), s5_lambda_re[0],
                   s5_lambda_im[0], s5_log_dt[0], s5_b_re[0], s5_b_im[0], s5_c_re[0], s5_c_im[0], s5_d[0],
                   s5_w_glu[0], s5_b_glu[0], b, n, cl)
    xl = _fnet_layer(xl, mods[3, 0:2], mix_pre_g[3], finisher(3), fn_w_o[0], fn_b_o[0], b, n, d)
    return xl.reshape(b, n, d)
```

```python
import functools
import math

import numpy as np
import jax
import jax.numpy as jnp
from jax import lax
from jax.experimental import pallas as pl
from jax.experimental.pallas import tpu as pltpu

F32 = jnp.float32
BF16 = jnp.bfloat16
NORM_EPS = 1e-6
LANE = 128
MXU_COLS = 256
VMEM_LIMIT = 56 * 1024 * 1024
HIGHEST = lax.Precision.HIGHEST

GRID_W = 64
ROPE_THETA = 10000.0
MLA_HEADS = 8
MLA_NOPE = 128
MLA_ROPE = 64
MLA_V = 128
MLA_VT = MLA_V + 16
HYENA_BANDS = 16
HYENA_TARGET = 1e-2
HYENA_FAST = 0.3
HYENA_SLOW = 1.5
S5_GROUP = 16
S5_STATE = 64
S5_T = 16
FNET_GC = 128
FFT_N2 = 128


def _cp(*sem):
    return pltpu.CompilerParams(dimension_semantics=sem, vmem_limit_bytes=VMEM_LIMIT)


def _dot(a, b):
    return jnp.dot(a, b, preferred_element_type=F32)


def _dot_hi(a, b):
    return jnp.dot(a, b, preferred_element_type=F32, precision=HIGHEST)


def _dot_x3(a, b):
    a_hi = a.astype(BF16)
    b_hi = b.astype(BF16)
    a_lo = (a - a_hi.astype(F32)).astype(BF16)
    b_lo = (b - b_hi.astype(F32)).astype(BF16)
    return _dot(a_hi, b_hi) + (_dot(a_hi, b_lo) + _dot(a_lo, b_hi))


def _rms(x, g):
    ms = jnp.mean(x * x, axis=-1, keepdims=True)
    return x * lax.rsqrt(ms + NORM_EPS) * g


def _normmod(x, g, shift, scale):
    return _rms(x, g) * (1.0 + scale) + shift


def _const_spec(shape):
    nd = len(shape)
    return pl.BlockSpec(shape, lambda *_: (0,) * nd)


def _mods_kernel(st_ref, w_ref, b_ref, o_ref):
    st = st_ref[...]
    st = st * jax.nn.sigmoid(st)
    w = w_ref[0]
    rows = [jnp.sum(st[:, r:r + 1] * w, axis=0, keepdims=True) for r in range(3)]
    rows.append(jnp.zeros((5, w.shape[1]), F32))
    o_ref[0] = jnp.concatenate(rows, axis=0) + b_ref[0]


def _mods(c, c_ctx, mod_w, mod_b):
    depth, d, n6 = mod_w.shape
    st = jnp.zeros((d, 8), F32).at[:, 0:2].set(c.T).at[:, 2].set(c_ctx)
    tn = 1024
    out = pl.pallas_call(
        _mods_kernel,
        grid=(depth, n6 // tn),
        in_specs=[_const_spec((d, 8)),
                  pl.BlockSpec((1, d, tn), lambda i, j: (i, 0, j)),
                  pl.BlockSpec((1, 1, tn), lambda i, j: (i, 0, j))],
        out_specs=pl.BlockSpec((1, 8, tn), lambda i, j: (i, 0, j)),
        out_shape=jax.ShapeDtypeStruct((depth, 8, n6), F32),
        compiler_params=_cp("parallel", "parallel"),
        name="mods",
    )(st, mod_w, mod_b.reshape(depth, 1, n6))
    m = out[:, :3].reshape(depth, 3, n6 // d, d)
    return jnp.pad(m, ((0, 0), (0, 0), (0, 8 - n6 // d), (0, 0)))


def _row_specs(tm, d, tpb):
    x_spec = pl.BlockSpec((tm, d), lambda i: (i, 0))
    mod_spec = pl.BlockSpec((1, 8, d), lambda i: (i // tpb, 0, 0))
    return x_spec, mod_spec


def _normmod_kernel(x_ref, mod_ref, g_ref, o_ref):
    h = _normmod(x_ref[...], g_ref[...], mod_ref[0, 0:1, :], mod_ref[0, 1:2, :])
    o_ref[...] = h.astype(o_ref.dtype)


def _normmod_call(x, mods, g, tm, rows_per_batch):
    m, d = x.shape
    x_spec, mod_spec = _row_specs(tm, d, rows_per_batch // tm)
    return pl.pallas_call(
        _normmod_kernel, grid=(m // tm,),
        in_specs=[x_spec, mod_spec, _const_spec((1, d))],
        out_specs=x_spec, out_shape=jax.ShapeDtypeStruct((m, d), BF16),
        compiler_params=_cp("parallel"), name="normmod",
    )(x, mods, g.reshape(1, d))


def _mix_ffn_kernel(x_ref, y_ref, mod_ref, wm_ref, bm_ref, gm_ref, pre_ref, post_ref, w13_ref, w2_ref, o_ref,
                    *, f, cuts, glu):
    z = _dot(y_ref[...].astype(BF16), wm_ref[...]) + bm_ref[...]
    if glu:
        d = o_ref.shape[-1]
        z = z[:, :d] * jax.nn.sigmoid(z[:, d:])
    x = x_ref[...] + mod_ref[0, 2:3, :] * _rms(z, gm_ref[...])
    h = _normmod(x, pre_ref[...], mod_ref[0, 3:4, :], mod_ref[0, 4:5, :]).astype(BF16)
    acc = None
    for lo, hi in zip(cuts[:-1], cuts[1:]):
        a = _dot(h, w13_ref[:, lo:hi])
        b = _dot(h, w13_ref[:, f + lo:f + hi])
        gact = (a * jax.nn.sigmoid(a) * b).astype(BF16)
        part = _dot(gact, w2_ref[lo:hi, :])
        acc = part if acc is None else acc + part
    o_ref[...] = x + mod_ref[0, 5:6, :] * _rms(acc, post_ref[...])


def _mix_ffn_call(x, y, mods, wm, bm, mix_post_g, ffn_pre_g, ffn_post_g, w13, w2, tm, rows_per_batch, glu=False):
    m, d = x.shape
    k, nm = wm.shape
    f = w2.shape[0]
    cuts = tuple(range(0, f, 6 * MXU_COLS)) + (f,)
    x_spec, mod_spec = _row_specs(tm, d, rows_per_batch // tm)
    once = lambda shape: pl.BlockSpec(shape, lambda i: (0, 0), pipeline_mode=pl.Buffered(1))
    return pl.pallas_call(
        functools.partial(_mix_ffn_kernel, f=f, cuts=cuts, glu=glu), grid=(m // tm,),
        in_specs=[x_spec, pl.BlockSpec((tm, k), lambda i: (i, 0)), mod_spec,
                  once((k, nm)), _const_spec((1, nm)), _const_spec((1, d)), _const_spec((1, d)),
                  _const_spec((1, d)), once((d, 2 * f)), once((f, d))],
        out_specs=x_spec, out_shape=jax.ShapeDtypeStruct((m, d), F32),
        compiler_params=_cp("parallel"), name="mix_ffn",
    )(x, y, mods, wm, bm.reshape(1, nm), mix_post_g.reshape(1, d), ffn_pre_g.reshape(1, d),
      ffn_post_g.reshape(1, d), w13, w2)


def _mla_proj_kernel(x_ref, mod_ref, g_ref, w_ref, qg_ref, kvg_ref, wq_ref, wkv_ref, t1k_ref, t2k_ref,
                     t1q_ref, t2q_ref, q_ref, k_ref, vt_ref, *, ql, kvl, qscale):
    h = _normmod(x_ref[...], g_ref[...], mod_ref[0, 0:1, :], mod_ref[0, 1:2, :])
    z = _dot(h.astype(BF16), w_ref[...])
    qn = _rms(z[:, :ql], qg_ref[...]).astype(BF16)
    cn = _rms(z[:, ql:ql + kvl], kvg_ref[...]).astype(BF16)
    pair = z[:, ql + kvl:]
    kr = (pair * t1k_ref[...] + pltpu.roll(pair, 64, 1) * t2k_ref[...]).astype(BF16)
    tk = kr.shape[0]

    zkv = _dot(cn, wkv_ref[...])
    ones_blk = (lax.broadcasted_iota(jnp.int32, (MLA_VT - MLA_V, tk), 0) == 0).astype(BF16)
    for hd in range(MLA_HEADS):
        base = hd * 256
        k_ref[0, hd, 0, :, 0:128] = zkv[:, base:base + 128].astype(BF16)
        k_ref[0, hd, 0, :, 128:256] = kr
        vt_ref[0, hd, 0, 0:MLA_V, :] = zkv[:, base + 128:base + 256].T.astype(BF16)
        vt_ref[0, hd, 0, MLA_V:MLA_VT, :] = ones_blk

    zq = _dot(qn, wq_ref[...])
    t1 = t1q_ref[...]
    t2 = t2q_ref[...]
    for hd in range(MLA_HEADS):
        base = hd * 256
        pair = zq[:, base + 128:base + 256]
        rp = pair * t1 + pltpu.roll(pair, 64, 1) * t2
        qcat = jnp.concatenate([zq[:, base:base + 128], rp], axis=1) * qscale
        q_ref[0, hd] = qcat.T.astype(BF16)


def _flash_kernel(q_ref, kc_ref, vc_ref, *rest, n_lat):
    if n_lat:
        kl_ref, vl_ref, o_ref, s_scr, acc_scr = rest
    else:
        o_ref, acc_scr = rest
    qt = q_ref[0, 0]

    def qk(k, slot):
        s = _dot(k, qt)
        s_scr[slot] = s
        return jnp.max(s, axis=0, keepdims=True)

    def sm_pv(slot, vt, m, mx):
        m_new = jnp.maximum(m, mx)
        alpha = jnp.exp2(m - m_new)
        p = jnp.exp2(s_scr[slot] - m_new).astype(BF16)
        acc_scr[...] = alpha * acc_scr[...] + _dot(vt, p)
        return m_new

    sc = _dot(kc_ref[0, 0, 0], qt)
    if n_lat:
        mx = qk(kl_ref[0, 0, 0], 0)
    m = jnp.max(sc, axis=0, keepdims=True)
    acc_scr[...] = _dot(vc_ref[0, 0, 0], jnp.exp2(sc - m).astype(BF16))
    if n_lat:

        per = 8 if n_lat % 8 == 0 else (4 if n_lat % 4 == 0 else 2)

        def body(i, carry):
            m, mx_cur = carry
            c = per * i
            for u in range(per):
                mx_next = qk(kl_ref[0, 0, jnp.minimum(c + u + 1, n_lat - 1)], (u + 1) % 2)
                m = sm_pv(u % 2, vl_ref[0, 0, c + u], m, mx_cur)
                mx_cur = mx_next
            return m, mx_cur

        lax.fori_loop(0, n_lat // per, body, (m, mx))
    acc = acc_scr[...]
    o_ref[0] = (acc[0:MLA_V] / acc[MLA_V:MLA_V + 1]).T.astype(o_ref.dtype)


def _rope_tables(n_lat):
    rows = n_lat // GRID_W
    row = jnp.repeat(jnp.arange(rows, dtype=F32), GRID_W)
    col = jnp.tile(jnp.arange(GRID_W, dtype=F32), rows)
    axis_dim = MLA_ROPE // 2
    inv_freq = 1.0 / (ROPE_THETA ** (jnp.arange(0, axis_dim, 2, dtype=F32) / axis_dim))
    ang_r = row[:, None] * inv_freq
    ang_c = col[:, None] * inv_freq
    cr, sr, cc, sc = jnp.cos(ang_r), jnp.sin(ang_r), jnp.cos(ang_c), jnp.sin(ang_c)
    cp = jnp.concatenate([cr, cr, cc, cc], axis=-1)
    sp = jnp.concatenate([-sr, sr, -sc, sc], axis=-1)
    return cp, sp


_ROPE_SWAP = np.concatenate([np.arange(16, 32), np.arange(0, 16), np.arange(48, 64), np.arange(32, 48)])


def _mla_side(x, mods, pre_g, w_in_ext, q_g, kv_g, w_uq_ext, w_ukv, tabs, b, n, tk):
    m, d = x.shape
    ql, kvl = q_g.shape[-1], kv_g.shape[-1]
    nc = n // tk
    x_spec, mod_spec = _row_specs(tk, d, nc)
    tab_spec = pl.BlockSpec((tk, 128), lambda i: (i % nc, 0))
    qscale = (MLA_NOPE + MLA_ROPE) ** -0.5 * math.log2(math.e)
    return pl.pallas_call(
        functools.partial(_mla_proj_kernel, ql=ql, kvl=kvl, qscale=qscale), grid=(m // tk,),
        in_specs=[x_spec, mod_spec, _const_spec((1, d)), _const_spec(w_in_ext.shape),
                  _const_spec((1, ql)), _const_spec((1, kvl)), _const_spec(w_uq_ext.shape),
                  _const_spec(w_ukv.shape), tab_spec, tab_spec, tab_spec, tab_spec],
        out_specs=[pl.BlockSpec((1, MLA_HEADS, 256, tk), lambda i: (i // nc, 0, 0, i % nc)),
                   pl.BlockSpec((1, MLA_HEADS, 1, tk, 256), lambda i: (i // nc, 0, i % nc, 0, 0)),
                   pl.BlockSpec((1, MLA_HEADS, 1, MLA_VT, tk), lambda i: (i // nc, 0, i % nc, 0, 0))],
        out_shape=[jax.ShapeDtypeStruct((b, MLA_HEADS, 256, n), BF16),
                   jax.ShapeDtypeStruct((b, MLA_HEADS, nc, tk, 256), BF16),
                   jax.ShapeDtypeStruct((b, MLA_HEADS, nc, MLA_VT, tk), BF16)],
        compiler_params=_cp("parallel"), name="mla_proj",
    )(x, mods, pre_g.reshape(1, d), w_in_ext, q_g.reshape(1, ql), kv_g.reshape(1, kvl), w_uq_ext, w_ukv, *tabs)


def _flash_call(qt, kc, vtc, kl, vtl, tq):
    b, hh, _, n = qt.shape
    c = kc.shape[-2]
    n_lat = 0 if kl is None else kl.shape[2]
    in_specs = [pl.BlockSpec((1, 1, 256, tq), lambda bi, h, i: (bi, h, 0, i)),
                pl.BlockSpec((1, 1, 1, c, 256), lambda bi, h, i: (bi, h, 0, 0, 0)),
                pl.BlockSpec((1, 1, 1, MLA_VT, c), lambda bi, h, i: (bi, h, 0, 0, 0))]
    args = [qt, kc, vtc]
    scratch = [pltpu.VMEM((MLA_VT, tq), F32)]
    if n_lat:
        assert n_lat % 2 == 0, "latent key chunks are consumed in pairs"
        tk = kl.shape[-2]
        in_specs += [pl.BlockSpec((1, 1, n_lat, tk, 256), lambda bi, h, i: (bi, h, 0, 0, 0)),
                     pl.BlockSpec((1, 1, n_lat, MLA_VT, tk), lambda bi, h, i: (bi, h, 0, 0, 0))]
        args += [kl, vtl]
        scratch = [pltpu.VMEM((2, tk, tq), F32)] + scratch
    return pl.pallas_call(
        functools.partial(_flash_kernel, n_lat=n_lat), grid=(b, hh, n // tq),
        in_specs=in_specs,
        out_specs=pl.BlockSpec((1, tq, MLA_V), lambda bi, h, i: (bi, i, h)),
        out_shape=jax.ShapeDtypeStruct((b, n, hh * MLA_V), BF16),
        scratch_shapes=scratch,
        compiler_params=_cp("parallel", "parallel", "arbitrary"), name="flash",
    )(*args)


def _mla_layer(xl, xc, mods_l, mods_c, pre_g, fin, w_in, q_g, kv_g, w_uq, w_ukv, w_o, b, n, c):
    d = xl.shape[-1]
    ql, kvl = q_g.shape[-1], kv_g.shape[-1]
    hh = MLA_HEADS
    rope_cols = w_in[:, ql + kvl:]
    w_in_ext = jnp.concatenate([w_in, rope_cols[:, _ROPE_SWAP]], axis=1).astype(BF16)
    wq = w_uq.reshape(ql, hh, MLA_NOPE + MLA_ROPE)
    w_uq_ext = jnp.concatenate([wq, wq[:, :, MLA_NOPE:][:, :, _ROPE_SWAP]], axis=-1)
    w_uq_ext = w_uq_ext.reshape(ql, hh * 256).astype(BF16)
    w_ukv_b = w_ukv.astype(BF16)
    w_o_b = w_o.astype(BF16)

    cp, sp = _rope_tables(n)
    z64l, o64l = jnp.zeros((n, 64), F32), jnp.ones((n, 64), F32)
    z64c, o64c = jnp.zeros((c, 64), F32), jnp.ones((c, 64), F32)
    cat = lambda a, bb: jnp.concatenate([a, bb], axis=1)
    tabs_l = (cat(cp, z64l), cat(sp, z64l), cat(cp, z64l), cat(sp, o64l))
    tabs_c = (cat(z64c, z64c), cat(z64c, o64c), cat(z64c, z64c), cat(z64c, o64c))

    tm_l = min(512, n)
    tk_l = min(512, n // 2)
    ql_, kl, vtl = _mla_side(xl, mods_l, pre_g, w_in_ext, q_g, kv_g, w_uq_ext, w_ukv_b, tabs_l, b, n, tk_l)
    qc_, kc, vtc = _mla_side(xc, mods_c, pre_g, w_in_ext, q_g, kv_g, w_uq_ext, w_ukv_b, tabs_c, b, c, c)
    o_lat = _flash_call(ql_, kc, vtc, kl, vtl, min(1024, n)).reshape(b * n, hh * MLA_V)
    o_ctx = _flash_call(qc_, kc, vtc, None, None, c).reshape(b * c, hh * MLA_V)
    zb = jnp.zeros((d,), F32)
    xl = fin(xl, o_lat, mods_l, w_o_b, zb, tm_l, n)
    xc = fin(xc, o_ctx, mods_c, w_o_b, zb, c, c)
    return xl, xc


def _hy_in_kernel(x_ref, xp_ref, xn_ref, mod_ref, g_ref, w_ref, b_ref, cw_ref, cb_ref,
                  g0_ref, vg_ref, *, tpb):
    i = pl.program_id(0)
    g = g_ref[...]
    shift, scale = mod_ref[0, 0:1, :], mod_ref[0, 1:2, :]
    xcat = jnp.concatenate([xp_ref[...], x_ref[...], xn_ref[...]], axis=0)
    ucat = _dot(_normmod(xcat, g, shift, scale).astype(BF16), w_ref[...]) + b_ref[...]
    tm = x_ref.shape[0]
    u = ucat[8:tm + 8]
    first = (i % tpb) == 0
    last = (i % tpb) == tpb - 1
    prev_row = jnp.where(first, 0.0, ucat[7:8, :])
    next_row = jnp.where(last, 0.0, ucat[tm + 8:tm + 9, :])
    ridx = lax.broadcasted_iota(jnp.int32, (tm, 1), 0)
    dn = jnp.where(ridx == 0, prev_row, pltpu.roll(u, 1, 0))
    upw = jnp.where(ridx == tm - 1, next_row, pltpu.roll(u, tm - 1, 0))
    conv = cb_ref[...] + dn * cw_ref[0:1, :] + u * cw_ref[1:2, :] + upw * cw_ref[2:3, :]
    d = g0_ref.shape[-1]
    g0_ref[...] = conv[:, :d]
    vg_ref[...] = conv[:, 2 * d:] * conv[:, d:2 * d]


def _hy_in_call(x, mods, pre_g, w_in, b_in, conv_w, conv_b, tm, n):
    m, d = x.shape
    p = w_in.shape[1]
    tpb = n // tm
    x_spec, mod_spec = _row_specs(tm, d, tpb)
    r8 = tm // 8
    nb8 = m // 8
    prev_spec = pl.BlockSpec((8, d), lambda i: (jnp.maximum(i * r8 - 1, 0), 0))
    next_spec = pl.BlockSpec((8, d), lambda i: (jnp.minimum((i + 1) * r8, nb8 - 1), 0))
    cw = jnp.pad(conv_w, ((0, 8 - conv_w.shape[0]), (0, 0)))
    return pl.pallas_call(
        functools.partial(_hy_in_kernel, tpb=tpb), grid=(m // tm,),
        in_specs=[x_spec, prev_spec, next_spec, mod_spec, _const_spec((1, d)), _const_spec((d, p)),
                  _const_spec((1, p)), _const_spec((8, p)), _const_spec((1, p))],
        out_specs=[x_spec, x_spec],
        out_shape=[jax.ShapeDtypeStruct((m, d), F32), jax.ShapeDtypeStruct((m, d), F32)],
        compiler_params=_cp("parallel"), name="hy_in",
    )(x, x, x, mods, pre_g.reshape(1, d), w_in, b_in.reshape(1, p), cw, conv_b.reshape(1, p))


_PI_SPLIT = (3.140625, 9.67502593994140625e-4, 1.509957990978376432e-7)
_SIN_TAYLOR = (-1.0 / 6, 1.0 / 120, -1.0 / 5040, 1.0 / 362880, -1.0 / 39916800)


def _sin(x):
    kf = jnp.round(x * (1.0 / math.pi))
    r = ((x - kf * _PI_SPLIT[0]) - kf * _PI_SPLIT[1]) - kf * _PI_SPLIT[2]
    r2 = r * r
    p = _SIN_TAYLOR[4]
    for c in _SIN_TAYLOR[3::-1]:
        p = p * r2 + c
    s = r + r * r2 * p
    odd = (kf.astype(jnp.int32) & 1) == 1
    return jnp.where(odd, -s, s)


def _hy_filter_kernel(bands_ref, w1_ref, b1_ref, fq_ref, w2_ref, b2_ref, w3_ref, dl_ref,
                      k_ref, nrm_ref, *, n, tr):
    i = pl.program_id(0)
    bwd = i >= n // tr
    row = lax.broadcasted_iota(jnp.int32, (tr, LANE), 0) + i * tr
    j = jnp.where(bwd, 2 * n - row, row).astype(F32)
    lane = lax.broadcasted_iota(jnp.int32, (tr, LANE), 1)
    t = j * (1.0 / (n - 1))
    arg = (2.0 * math.pi / n) * j * bands_ref[0:1, :] + bands_ref[1:2, :]
    z = jnp.where(lane == 0, t, jnp.where(lane <= 2 * HYENA_BANDS, _sin(arg), 0.0))
    fq = fq_ref[...]
    a = _sin(fq * (_dot_hi(z, w1_ref[...]) + b1_ref[...]))
    for k in range(w2_ref.shape[0]):
        a = _sin(fq * (_dot_hi(a, w2_ref[k]) + b2_ref[k]))
    h = _dot_x3(a, w3_ref[jnp.where(bwd, 1, 0)])
    decay = jnp.exp(-t[:, 0:1] * dl_ref[...])
    k = h * decay
    k = jnp.where(row[:, 0:1] == n, 0.0, k)
    k_ref[...] = k
    part = jnp.sum(jnp.abs(k), axis=0, keepdims=True)

    @pl.when(i == 0)
    def _():
        nrm_ref[...] = jnp.zeros_like(nrm_ref)

    nrm_ref[...] += jnp.broadcast_to(part, nrm_ref.shape)


def _hy_filter_call(n, d, f_w1, f_b1, f_freq, f_w2, f_b2, f_w3):
    fw = f_w1.shape[1]
    tr = min(512, n)
    bands_np = np.zeros((8, LANE), np.float32)
    bands_np[0, 1:1 + HYENA_BANDS] = np.linspace(1e-4, HYENA_BANDS - 1, HYENA_BANDS, dtype=np.float32)
    bands_np[0, 1 + HYENA_BANDS:1 + 2 * HYENA_BANDS] = bands_np[0, 1:1 + HYENA_BANDS]
    bands_np[1, 1:1 + HYENA_BANDS] = 0.5 * np.pi
    bands_np[1, 1 + HYENA_BANDS:1 + 2 * HYENA_BANDS] = np.pi
    w1p = jnp.zeros((LANE, fw), F32).at[:f_w1.shape[0]].set(f_w1)
    deltas = jnp.abs(jnp.linspace(math.log(HYENA_TARGET) / HYENA_SLOW, math.log(HYENA_TARGET) / HYENA_FAST,
                                  d, dtype=F32)).reshape(1, d)
    row = pl.BlockSpec((tr, d), lambda i: (i, 0))
    return pl.pallas_call(
        functools.partial(_hy_filter_kernel, n=n, tr=tr), grid=(2 * n // tr,),
        in_specs=[_const_spec((8, LANE)), _const_spec((LANE, fw)), _const_spec((1, fw)), _const_spec((1, fw)),
                  _const_spec(f_w2.shape), _const_spec((f_w2.shape[0], 1, fw)), _const_spec((2, fw, d)),
                  _const_spec((1, d))],
        out_specs=[row, _const_spec((8, d))],
        out_shape=[jax.ShapeDtypeStruct((2 * n, d), F32), jax.ShapeDtypeStruct((8, d), F32)],
        compiler_params=_cp("arbitrary"), name="hy_filter",
    )(jnp.asarray(bands_np), w1p, f_b1.reshape(1, fw), f_freq.reshape(1, fw), f_w2,
      f_b2.reshape(f_w2.shape[0], 1, fw), jnp.transpose(f_w3.reshape(fw, 2, d), (1, 0, 2)), deltas)


def _dft_cs(nf, nt, period):
    ft = (np.arange(nf)[:, None] * np.arange(nt)[None, :]) % period
    ang = 2.0 * np.pi * ft / period
    return np.cos(ang), np.sin(ang)


def _twiddle_tables(n1, n2, lead_t2):
    nn = n1 * n2
    f1 = jnp.arange(n1, dtype=jnp.int32)
    t2 = jnp.arange(n2, dtype=jnp.int32)
    idx = (t2[:, None] * f1[None, :]) % nn if lead_t2 else (f1[:, None] * t2[None, :]) % nn
    ang = idx.astype(F32) * (2.0 * math.pi / nn)
    shape = idx.shape + (LANE,)
    return (jnp.broadcast_to(jnp.cos(ang)[..., None], shape),
            jnp.broadcast_to(jnp.sin(ang)[..., None], shape))


KRON_R = 16
KRON_W = 256


def _cblock(mc):
    return np.block([[mc.real, -mc.imag], [mc.imag, mc.real]])


def _split_radix(n1):
    b = 16 if (n1 % 16 == 0 and n1 >= 64) else 4
    assert n1 % b == 0 and (n1 // b) % 2 == 0
    return n1 // b, b


def _slab_dft_mats(n1, a_in, a_out, sign, neg_im=False, real_in=False, scale=1.0):
    a, b = _split_radix(n1)
    r = KRON_R
    eye = np.eye(r)
    w = lambda num, den: np.exp(sign * 2j * np.pi * num / den)
    ua = np.arange(a)[:, None] * np.arange(a)[None, :]
    vb = np.arange(b)[:, None] * np.arange(b)[None, :]
    if sign < 0:
        m1 = np.kron(w(ua[:, :a_in], a) * scale, eye)
        l1 = _cblock(m1)
        if neg_im:
            l1[:, a_in * r:] *= -1.0
        if real_in:
            l1 = l1[:, :a_in * r]
        l2 = np.stack([_cblock(np.kron(w(vb, b) * w(u * np.arange(b)[None, :], n1), eye)) for u in range(a)])
    else:
        l1 = np.stack([_cblock(np.kron(w(vb, b) * w(u * np.arange(b)[:, None], n1), eye)) for u in range(a)])
        l2 = _cblock(np.kron(w(ua[:a_out, :], a) * scale, eye))
    return jnp.asarray(l1, BF16), jnp.asarray(l2, BF16)


def _kfa_kernel(*refs, nparts, a_in, a, b):
    l1_ref, l2_ref = refs[0], refs[1]
    parts = refs[2:2 + nparts]
    twc_ref, tws_ref, or_ref, oi_ref = refs[2 + nparts:]
    r = KRON_R
    reps = or_ref.shape[-1] // LANE
    l1 = l1_ref[...]
    y1 = []
    for bb in range(b):
        x = jnp.concatenate([p[0, aa * b + bb] for p in parts for aa in range(a_in)], axis=0).astype(BF16)
        y1.append(_dot(l1, x).astype(BF16))
    for u in range(a):
        x = jnp.concatenate([y1[bb][u * r:(u + 1) * r] for bb in range(b)]
                            + [y1[bb][(a + u) * r:(a + u + 1) * r] for bb in range(b)], axis=0)
        z = _dot(l2_ref[u], x)
        for v in range(b):
            f1 = u + a * v
            zr = z[v * r:(v + 1) * r]
            zi = z[(b + v) * r:(b + v + 1) * r]
            c = jnp.concatenate([twc_ref[f1]] * reps, axis=1)
            sn = jnp.concatenate([tws_ref[f1]] * reps, axis=1)
            or_ref[0, f1] = (zr * c + zi * sn).astype(or_ref.dtype)
            oi_ref[0, f1] = (zi * c - zr * sn).astype(oi_ref.dtype)


def _kfa_call(parts, nb_out, n1, n2, d, l1, l2, twc, tws):
    a, b = _split_radix(n1)
    a_in = parts[0][0].shape[1] // b
    r, w = KRON_R, min(KRON_W, d)
    in_specs = [_const_spec(l1.shape), pl.BlockSpec(l2.shape, lambda bi, j, k: (0, 0, 0), pipeline_mode=pl.Buffered(1))]
    args = [l1, l2]
    for arr, bi_fixed in parts:
        t1_in = arr.shape[1]
        if bi_fixed is None:
            in_specs.append(pl.BlockSpec((1, t1_in, r, w), lambda bi, j, k: (bi, 0, j, k)))
        else:
            in_specs.append(pl.BlockSpec((1, t1_in, r, w), lambda bi, j, k, f=bi_fixed: (f, 0, j, k)))
        args.append(arr)
    tw_spec = pl.BlockSpec((n1, r, LANE), lambda bi, j, k: (0, j, 0))
    out_spec = pl.BlockSpec((1, n1, r, w), lambda bi, j, k: (bi, 0, j, k))
    out = jax.ShapeDtypeStruct((nb_out, n1, n2, d), BF16)
    return pl.pallas_call(
        functools.partial(_kfa_kernel, nparts=len(parts), a_in=a_in, a=a, b=b),
        grid=(nb_out, n2 // r, d // w),
        in_specs=in_specs + [tw_spec, tw_spec], out_specs=[out_spec, out_spec], out_shape=[out, out],
        compiler_params=_cp("parallel", "parallel", "parallel"), name="kfa",
    )(*args, twc, tws)


def _kfc_kernel(l3_ref, l4_ref, gr_ref, gi_ref, vg_ref, g0_ref, skip_ref, o_ref, *, a, b, a_out):
    r = KRON_R
    y3 = []
    for u in range(a):
        x = jnp.concatenate([gr_ref[0, u + a * v] for v in range(b)]
                            + [gi_ref[0, u + a * v] for v in range(b)], axis=0)
        y3.append(_dot(l3_ref[u], x).astype(BF16))
    l4 = l4_ref[...]
    skip = skip_ref[...]
    for bb in range(b):
        x = jnp.concatenate([y3[u][bb * r:(bb + 1) * r] for u in range(a)]
                            + [y3[u][(b + bb) * r:(b + bb + 1) * r] for u in range(a)], axis=0)
        y = _dot(l4, x)
        for sg in range(2):
            for aa in range(a_out):
                t1 = aa * b + bb
                yb = y[(sg * a_out + aa) * r:(sg * a_out + aa + 1) * r]
                o_ref[sg, t1] = ((yb + vg_ref[sg, t1] * skip) * g0_ref[sg, t1]).astype(o_ref.dtype)


def _kfc_call(g1r, g1i, vg4, g04, skip, n1, n2, d, l3, l4):
    a, b = _split_radix(n1)
    nb, t1_out = vg4.shape[:2]
    r, w = KRON_R, min(KRON_W, d)
    slab = pl.BlockSpec((1, n1, r, w), lambda j, k: (0, 0, j, k))
    nat = pl.BlockSpec((nb, t1_out, r, w), lambda j, k: (0, 0, j, k))
    return pl.pallas_call(
        functools.partial(_kfc_kernel, a=a, b=b, a_out=t1_out // b), grid=(n2 // r, d // w),
        in_specs=[pl.BlockSpec(l3.shape, lambda j, k: (0, 0, 0), pipeline_mode=pl.Buffered(1)),
                  _const_spec(l4.shape), slab, slab, nat, nat, pl.BlockSpec((1, w), lambda j, k: (0, k))],
        out_specs=nat, out_shape=jax.ShapeDtypeStruct(vg4.shape, BF16),
        compiler_params=_cp("parallel", "parallel"), name="kfc",
    )(l3, l4, g1r, g1i, vg4, g04, skip.reshape(1, d))


def _hy_kb_kernel(fm_ref, fr_ref, fi_ref, sc_ref, kr_ref, ki_ref, *, nf):
    fm = fm_ref[...]
    n2 = fr_ref.shape[1]
    sc = sc_ref[...]
    for s in range(nf):
        kk = _dot(fm, jnp.concatenate([fr_ref[s], fi_ref[s]], axis=0))
        kr_ref[s] = kk[:n2] * sc
        ki_ref[s] = kk[n2:] * sc


def _hy_b_kernel(fm_ref, fmc_ref, xr_ref, xi_ref, kr_ref, ki_ref, twc_ref, tws_ref,
                 or_ref, oi_ref, *, nf, d):
    fm = fm_ref[...]
    fmc = fmc_ref[...]
    n2 = xr_ref.shape[1]
    for s in range(nf):
        x = _dot(fm, jnp.concatenate([xr_ref[s], xi_ref[s]], axis=0))
        xr, xi = x[:n2], x[n2:]
        kr, ki = kr_ref[s], ki_ref[s]
        yr = (xr * kr - xi * ki).astype(BF16)
        yi = (xr * ki + xi * kr).astype(BF16)
        g = _dot(fmc, jnp.concatenate([yr, yi], axis=0))
        gr, gi = g[:n2], g[n2:]
        c = jnp.concatenate([twc_ref[s]] * (d // LANE), axis=1)
        sn = jnp.concatenate([tws_ref[s]] * (d // LANE), axis=1)
        or_ref[s] = (gr * c - gi * sn).astype(or_ref.dtype)
        oi_ref[s] = (gi * c + gr * sn).astype(oi_ref.dtype)


def _block_c(cs, sn, sign):
    return np.block([[cs, -sign * sn], [sign * sn, cs]])


def _hy_conv_long(vg, g0, k, nrm, skip, b, n, d):
    n2 = FFT_N2
    nn = 2 * n
    n1 = nn // n2
    rows = n // n2
    a, _ = _split_radix(n1)
    la1, la2 = _slab_dft_mats(n1, a // 2, 0, -1)
    lk1, lk2 = _slab_dft_mats(n1, a, 0, -1, real_in=True)
    lc3, lc4 = _slab_dft_mats(n1, 0, a // 2, +1)
    cs2, sn2 = _dft_cs(n2, n2, n2)
    fm_b = jnp.asarray(_block_c(cs2, sn2, -1.0), BF16)
    fm_bc = jnp.asarray(_block_c(cs2, sn2, 1.0), BF16)
    twc, tws = _twiddle_tables(n1, n2, lead_t2=False)

    kfr, kfi = _kfa_call([(k.reshape(1, n1, n2, d), 0)], 1, n1, n2, d, lk1, lk2, twc, tws)
    nf = 4 if n1 % 4 == 0 else 1
    scale = (1.0 / (nrm[0:1, :] * nn))
    slab = pl.BlockSpec((nf, n2, d), lambda j: (j, 0, 0))
    shp3 = (n1, n2, d)
    khr, khi = pl.pallas_call(
        functools.partial(_hy_kb_kernel, nf=nf), grid=(n1 // nf,),
        in_specs=[_const_spec(fm_b.shape), slab, slab, _const_spec((1, d))],
        out_specs=[slab, slab],
        out_shape=[jax.ShapeDtypeStruct(shp3, F32)] * 2,
        compiler_params=_cp("parallel"), name="hy_kb",
    )(fm_b, kfr.reshape(shp3), kfi.reshape(shp3), scale)

    vg4 = vg.reshape(b, rows, n2, d)
    g04 = g0.reshape(b, rows, n2, d)
    x1r, x1i = _kfa_call([(vg4, 0), (vg4, 1)], 1, n1, n2, d, la1, la2, twc, tws)
    tw_slab = pl.BlockSpec((nf, n2, LANE), lambda j: (j, 0, 0))
    g1r, g1i = pl.pallas_call(
        functools.partial(_hy_b_kernel, nf=nf, d=d), grid=(n1 // nf,),
        in_specs=[_const_spec(fm_b.shape), _const_spec(fm_bc.shape), slab, slab, slab, slab,
                  tw_slab, tw_slab],
        out_specs=[slab, slab],
        out_shape=[jax.ShapeDtypeStruct(shp3, BF16)] * 2,
        compiler_params=_cp("parallel"), name="hy_b",
    )(fm_b, fm_bc, x1r.reshape(shp3), x1i.reshape(shp3), khr, khi, twc, tws)
    out = _kfc_call(g1r.reshape(1, n1, n2, d), g1i.reshape(1, n1, n2, d), vg4, g04, skip, n1, n2, d, lc3, lc4)
    return out.reshape(b * n, d)


def _hy_short_kernel(fa_ref, fk_ref, fi_ref, vg_ref, g0_ref, k_ref, nrm_ref, skip_ref, o_ref, *, n):
    z = jnp.concatenate([vg_ref[0], vg_ref[1]], axis=0)
    x = _dot_hi(fa_ref[...], z)
    kk = _dot_hi(fk_ref[...], k_ref[...])
    nn = 2 * n
    sc = 1.0 / (nrm_ref[0:1, :] * nn)
    xr, xi = x[:nn], x[nn:]
    kr, ki = kk[:nn] * sc, kk[nn:] * sc
    y = _dot_hi(fi_ref[...], jnp.concatenate([xr * kr - xi * ki, xr * ki + xi * kr], axis=0))
    skip = skip_ref[...]
    for bi in range(2):
        o_ref[bi] = ((y[bi * n:(bi + 1) * n] + vg_ref[bi] * skip) * g0_ref[bi]).astype(o_ref.dtype)


def _hy_conv_short(vg, g0, k, nrm, skip, b, n, d):
    nn = 2 * n
    cs, sn = _dft_cs(nn, n, nn)
    fa = jnp.asarray(_block_c(cs, sn, -1.0), F32)
    csk, snk = _dft_cs(nn, nn, nn)
    fk = jnp.asarray(np.concatenate([csk, -snk], axis=0), F32)
    csi, sni = _dft_cs(n, nn, nn)
    fi = jnp.asarray(_block_c(csi, sni, 1.0), F32)
    cb = 256
    col3 = pl.BlockSpec((b, n, cb), lambda j: (0, 0, j))
    vec = pl.BlockSpec((1, cb), lambda j: (0, j))
    out = pl.pallas_call(
        functools.partial(_hy_short_kernel, n=n), grid=(d // cb,),
        in_specs=[_const_spec(fa.shape), _const_spec(fk.shape), _const_spec(fi.shape), col3, col3,
                  pl.BlockSpec((nn, cb), lambda j: (0, j)), pl.BlockSpec((8, cb), lambda j: (0, j)), vec],
        out_specs=col3, out_shape=jax.ShapeDtypeStruct((b, n, d), BF16),
        compiler_params=_cp("parallel"), name="hy_short",
    )(fa, fk, fi, vg.reshape(b, n, d), g0.reshape(b, n, d), k, nrm, skip.reshape(1, d))
    return out.reshape(b * n, d)


def _hyena_layer(xl, xc, mods_l, mods_c, pre_g, fin, w_in, b_in, conv_w, conv_b, filt, skip,
                 w_out, b_out, b, n, c):
    d = xl.shape[-1]
    w_in_b = w_in.astype(BF16)
    w_out_b = w_out.astype(BF16)
    tm = min(512, n)
    g0, vg = _hy_in_call(xl, mods_l, pre_g, w_in_b, b_in, conv_w, conv_b, tm, n)
    k, nrm = _hy_filter_call(n, d, *filt)
    u_out = _hy_conv_long(vg, g0, k, nrm, skip[0], b, n, d)
    xl = fin(xl, u_out, mods_l, w_out_b, b_out, tm, n)

    g0c, vgc = _hy_in_call(xc, mods_c, pre_g, w_in_b, b_in, conv_w, conv_b, c, c)
    kc, nrmc = _hy_filter_call(c, d, *filt)
    u_out_c = _hy_conv_short(vgc, g0c, kc, nrmc, skip[0], b, c, d)
    xc = fin(xc, u_out_c, mods_c, w_out_b, b_out, c, c)
    return xl, xc


def _s5_operators(lam_re, lam_im, log_dt, b_re, b_im, c_re, c_im, d_skip):
    t = S5_T
    g, ns = lam_re.shape[1], lam_re.shape[2]
    gc = b_re.shape[-1]
    gl = LANE // gc
    nblk = g // gl
    lam = lax.complex(lam_re, lam_im)
    dt = jnp.exp(log_dt)[..., None]
    lam_bar = jnp.exp(lam * dt)
    b_bar = ((lam_bar - 1.0) / lam)[..., None] * lax.complex(b_re, b_im)
    c_mat = lax.complex(c_re, c_im)
    pw = jnp.arange(t + 1, dtype=F32)
    lam_pw = jnp.exp((lam * dt)[None] * pw[:, None, None, None])
    hp = HIGHEST
    kern = jnp.einsum('dgcn,tdgn,dgne->dgtce', c_mat, lam_pw[:t], b_bar, precision=hp).real
    dsk = d_skip.reshape(g, gc)
    kt = jnp.swapaxes(kern, -1, -2)
    centre = kt[0][:, 0] + kt[1][:, 0] + jnp.eye(gc, dtype=F32)[None] * dsk[:, :, None]
    ks = jnp.concatenate([kt[1][:, 1:][:, ::-1], centre[:, None], kt[0][:, 1:]], axis=1)
    eye_gl = jnp.eye(gl, dtype=F32)
    ks = ks.reshape(nblk, gl, 2 * t - 1, gc, gc)
    d_tab = jnp.einsum('bglec,gh->blgehc', ks, eye_gl).reshape(nblk, 2 * t - 1, LANE, LANE)

    def compact(zc, im_sign):
        z = jnp.concatenate([zc.real, im_sign * zc.imag], axis=-1)
        z = jnp.transpose(z, (2, 1, 0, 3, 4)).reshape(nblk, gl, t, 2, gc, 2 * ns)
        return jnp.transpose(z, (0, 2, 1, 3, 4, 5))

    b_t = jnp.swapaxes(b_bar, -1, -2)
    pf = lam_pw[:t][::-1][:, 0, :, None, :] * b_t[0][None]
    pb = lam_pw[:t][:, 1, :, None, :] * b_t[1][None]
    p_tab = compact(jnp.stack([pf, pb], axis=0), 1.0)
    qf = c_mat[0][None] * lam_pw[1:t + 1, 0][:, :, None, :]
    qb = c_mat[1][None] * lam_pw[1:t + 1][::-1][:, 1][:, :, None, :]
    q_tab = compact(jnp.stack([qf, qb], axis=0), -1.0)

    a = lam_pw[t]
    m_op, p_op, q_op = _s5_expand(d_tab, p_tab, q_tab)
    return m_op, p_op, q_op, a.real.reshape(2, g * ns), a.imag.reshape(2, g * ns)


def _s5_m_kernel(d_ref, o_ref, *, t):
    for s in range(t):
        for tt in range(t):
            o_ref[0, s * LANE:(s + 1) * LANE, tt * LANE:(tt + 1) * LANE] = d_ref[0, tt - s + t - 1].astype(o_ref.dtype)


def _s5_pq_kernel(c_ref, o_ref, *, transpose):
    t, gl, nd, gc, w = c_ref.shape[1:]
    ns = w // 2
    lane_grp = lax.broadcasted_iota(jnp.int32, (gc, gl * ns), 1) // ns
    for j in range(t):
        rows = []
        for g in range(gl):
            cols = []
            for dd in range(nd):
                piece = c_ref[0, j, g, dd]
                for ri in range(2):
                    tiled = jnp.concatenate([piece[:, ri * ns:(ri + 1) * ns]] * gl, axis=1)
                    cols.append(jnp.where(lane_grp == g, tiled, 0.0))
            rows.append(jnp.concatenate(cols, axis=1))
        blk = jnp.concatenate(rows, axis=0)
        if transpose:
            o_ref[0, :, j * LANE:(j + 1) * LANE] = blk.T.astype(o_ref.dtype)
        else:
            o_ref[0, j * LANE:(j + 1) * LANE, :] = blk.astype(o_ref.dtype)


def _s5_expand(d_tab, p_tab, q_tab):
    nblk, nlag = d_tab.shape[:2]
    t = (nlag + 1) // 2
    _, _, gl, nd, gc, w = p_tab.shape
    ncol = nd * gl * w
    whole = lambda shape: pl.BlockSpec((1,) + shape, lambda b: (b,) + (0,) * len(shape))
    m_op = pl.pallas_call(
        functools.partial(_s5_m_kernel, t=t), grid=(nblk,),
        in_specs=[whole((nlag, LANE, LANE))], out_specs=whole((t * LANE, t * LANE)),
        out_shape=jax.ShapeDtypeStruct((nblk, t * LANE, t * LANE), BF16),
        compiler_params=_cp("parallel"), name="s5_m_op",
    )(d_tab)
    tab_spec = whole((t, gl, nd, gc, w))
    p_op = pl.pallas_call(
        functools.partial(_s5_pq_kernel, transpose=False), grid=(nblk,),
        in_specs=[tab_spec], out_specs=whole((t * LANE, ncol)),
        out_shape=jax.ShapeDtypeStruct((nblk, t * LANE, ncol), BF16),
        compiler_params=_cp("parallel"), name="s5_p_op",
    )(p_tab)
    q_op = pl.pallas_call(
        functools.partial(_s5_pq_kernel, transpose=True), grid=(nblk,),
        in_specs=[tab_spec], out_specs=whole((ncol, t * LANE)),
        out_shape=jax.ShapeDtypeStruct((nblk, ncol, t * LANE), BF16),
        compiler_params=_cp("parallel"), name="s5_q_op",
    )(q_tab)
    return m_op, p_op, q_op


def _s5_sum_kernel(*refs, t):
    u_refs = refs[:t]
    p_ref = refs[t]
    outs = refs[t + 1:]
    u = jnp.concatenate([r[...] for r in u_refs], axis=1)
    res = _dot(u, p_ref[0])
    w = res.shape[1] // len(outs)
    for i, o in enumerate(outs):
        o[...] = res[:, i * w:(i + 1) * w]


def _s5_sum_call(h, p_op, rb):
    rows = h.shape[0]
    t = S5_T
    nblk = p_op.shape[0]
    w = p_op.shape[2] // 4
    u_specs = [pl.BlockSpec((rb, LANE), lambda gb, r, s=s: (r, s * nblk + gb)) for s in range(t)]
    out_spec = pl.BlockSpec((rb, w), lambda gb, r: (r, gb))
    return pl.pallas_call(
        functools.partial(_s5_sum_kernel, t=t), grid=(nblk, rows // rb),
        in_specs=u_specs + [pl.BlockSpec((1,) + p_op.shape[1:], lambda gb, r: (gb, 0, 0))],
        out_specs=[out_spec] * 4,
        out_shape=[jax.ShapeDtypeStruct((rows, nblk * w), F32)] * 4,
        compiler_params=_cp("parallel", "parallel"), name="s5_sum",
    )(*([h] * t), p_op)


def _s5_rec_kernel(sr_ref, si_ref, ar_ref, ai_ref, h0r_ref, h0i_ref, hr_ref, hi_ref, fr_ref, fi_ref,
                   cr, ci, *, kb, reverse):
    @pl.when(pl.program_id(1) == 0)
    def _():
        cr[...] = h0r_ref[0]
        ci[...] = h0i_ref[0]

    ar, ai = ar_ref[...], ai_ref[...]

    def body(i, carry):
        hr, hi = carry
        k = kb - 1 - i if reverse else i
        hr_ref[pl.ds(k, 1), :] = hr
        hi_ref[pl.ds(k, 1), :] = hi
        nr = ar * hr - ai * hi + sr_ref[pl.ds(k, 1), :]
        ni = ar * hi + ai * hr + si_ref[pl.ds(k, 1), :]
        return nr, ni

    hr, hi = lax.fori_loop(0, kb, body, (cr[...], ci[...]))
    cr[...] = hr
    ci[...] = hi
    fr_ref[0] = hr
    fi_ref[0] = hi


def _s5_rec_call(sr, si, ar, ai, h0r, h0i, nb_batch, reverse):
    rows, w = sr.shape
    nk = rows // nb_batch
    kb = min(64, nk)
    nb = nk // kb
    blk = (lambda bi, i: (bi * nb + nb - 1 - i, 0)) if reverse else (lambda bi, i: (bi * nb + i, 0))
    row_spec = pl.BlockSpec((kb, w), blk)
    vec = _const_spec((1, w))
    st = pl.BlockSpec((1, 1, w), lambda bi, i: (bi, 0, 0))
    return pl.pallas_call(
        functools.partial(_s5_rec_kernel, kb=kb, reverse=reverse), grid=(nb_batch, nb),
        in_specs=[row_spec, row_spec, vec, vec, st, st],
        out_specs=[row_spec, row_spec, st, st],
        out_shape=[jax.ShapeDtypeStruct((rows, w), F32)] * 2 + [jax.ShapeDtypeStruct((nb_batch, 1, w), F32)] * 2,
        scratch_shapes=[pltpu.VMEM((1, w), F32), pltpu.VMEM((1, w), F32)],
        compiler_params=_cp("parallel", "arbitrary"), name="s5_rec",
    )(sr, si, ar, ai, h0r, h0i)


def _s5_out_kernel(*refs, t):
    u_refs = refs[:t]
    h_refs = refs[t:t + 4]
    m_ref, q_ref, o_ref = refs[t + 4:]
    u = jnp.concatenate([r[...] for r in u_refs], axis=1)
    hcat = jnp.concatenate([r[...].astype(BF16) for r in h_refs], axis=1)
    y = _dot(u, m_ref[0]) + _dot(hcat, q_ref[0])
    o_ref[0] = (0.5 * y * (1.0 + lax.erf(y * (2.0 ** -0.5)))).astype(o_ref.dtype)


def _s5_out_call(h, states, m_op, q_op, rb):
    rows = h.shape[0]
    t = S5_T
    nblk = m_op.shape[0]
    w = q_op.shape[1] // 4
    u_specs = [pl.BlockSpec((rb, LANE), lambda gb, r, s=s: (r, s * nblk + gb)) for s in range(t)]
    return pl.pallas_call(
        functools.partial(_s5_out_kernel, t=t), grid=(nblk, rows // rb),
        in_specs=u_specs + [pl.BlockSpec((rb, w), lambda gb, r: (r, gb))] * 4
        + [pl.BlockSpec((1,) + m_op.shape[1:], lambda gb, r: (gb, 0, 0)),
           pl.BlockSpec((1,) + q_op.shape[1:], lambda gb, r: (gb, 0, 0))],
        out_specs=pl.BlockSpec((1, rb, t * LANE), lambda gb, r: (gb, r, 0)),
        out_shape=jax.ShapeDtypeStruct((nblk, rows, t * LANE), BF16),
        compiler_params=_cp("parallel", "parallel"), name="s5_out",
    )(*([h] * t), *states, m_op, q_op)


def _s5_layer(xl, xc, mods_l, mods_c, pre_g, fin, lam_re, lam_im, log_dt, b_re, b_im, c_re, c_im,
              d_skip, w_glu, b_glu, b, n, c):
    d = xl.shape[-1]
    t = S5_T
    m_op, p_op, q_op, a_re, a_im = _s5_operators(lam_re, lam_im, log_dt, b_re, b_im, c_re, c_im, d_skip)
    nblk = m_op.shape[0]
    w = a_re.shape[-1]
    hl = _normmod_call(xl, mods_l, pre_g, min(512, n), n).reshape(b * n // t, t * d)
    hc = _normmod_call(xc, mods_c, pre_g, c, c).reshape(b * c // t, t * d)

    def scan(h, init):
        sfr, sfi, sbr, sbi = _s5_sum_call(h, p_op, min(512, h.shape[0]))
        hfr, hfi, ffr, ffi = _s5_rec_call(sfr, sfi, a_re[0:1], a_im[0:1], init[0], init[1], b, False)
        hbr, hbi, fbr, fbi = _s5_rec_call(sbr, sbi, a_re[1:2], a_im[1:2], init[2], init[3], b, True)
        return (hfr, hfi, hbr, hbi), (ffr, ffi, fbr, fbi)

    zeros = jnp.zeros((b, 1, w), F32)
    _, ctx_final = scan(hc, (zeros,) * 4)
    states, _ = scan(hl, ctx_final)
    nk = n // t
    gact = _s5_out_call(hl, states, m_op, q_op, min(512, b * nk))
    g_nat = jnp.transpose(gact.reshape(nblk, b * nk, t, LANE), (1, 2, 0, 3)).reshape(b * n, d)
    return fin(xl, g_nat, mods_l, w_glu.astype(BF16), b_glu, min(512, n), n, glu=True)


def _fn_pre_kernel(x_ref, mod_ref, g_ref, cs_ref, a_ref, b_ref):
    h = _normmod(x_ref[...], g_ref[...], mod_ref[0, 0:1, :], mod_ref[0, 1:2, :]).astype(BF16)
    cs = cs_ref[...]
    gc = FNET_GC
    ab = [_dot(h[:, k * gc:(k + 1) * gc], cs) for k in range(h.shape[1] // gc)]
    a_ref[...] = jnp.concatenate([z[:, :gc] for z in ab], axis=1).astype(a_ref.dtype)
    b_ref[...] = jnp.concatenate([z[:, gc:] for z in ab], axis=1).astype(b_ref.dtype)


def _fn_c_kernel(l5_ref, l6_ref, xr_ref, xi_ref, o_ref, *, nh):
    r = KRON_R
    n2 = nh * r
    l5 = l5_ref[...]
    y5 = [_dot(l5, jnp.concatenate([xr_ref[0, f], xi_ref[0, f]], axis=0)).astype(BF16) for f in range(r)]
    l6 = l6_ref[...]
    for p in range(nh):
        x = jnp.concatenate([y5[f][p * r:(p + 1) * r] for f in range(r)]
                            + [y5[f][n2 + p * r:n2 + (p + 1) * r] for f in range(r)], axis=0)
        out = _dot(l6, x)
        for q in range(r):
            o_ref[0, p + nh * q] = out[q * r:(q + 1) * r].astype(o_ref.dtype)


def _fnet_layer(xl, mods_l, pre_g, fin, w_o, b_o, b, n, d):
    n2 = FFT_N2
    n1 = n // n2
    gc = FNET_GC
    cc, sc = _dft_cs(gc, gc, gc)
    cs = jnp.asarray(np.concatenate([cc, sc], axis=1) / np.sqrt(gc), BF16)
    tm = min(512, n)
    x_spec, mod_spec = _row_specs(tm, d, n // tm)
    ab = jax.ShapeDtypeStruct((b * n, d), BF16)
    am, bm = pl.pallas_call(
        _fn_pre_kernel, grid=(b * n // tm,),
        in_specs=[x_spec, mod_spec, _const_spec((1, d)), _const_spec(cs.shape)],
        out_specs=[x_spec, x_spec], out_shape=[ab, ab],
        compiler_params=_cp("parallel"), name="fn_pre",
    )(xl, mods_l, pre_g.reshape(1, d), cs)

    a, _ = _split_radix(n1)
    l1, l2 = _slab_dft_mats(n1, a, 0, -1, neg_im=True, scale=1.0 / np.sqrt(n))
    twc, tws = _twiddle_tables(n1, n2, lead_t2=False)
    a4, b4 = am.reshape(b, n1, n2, d), bm.reshape(b, n1, n2, d)
    xr, xi = _kfa_call([(a4, None), (b4, None)], b, n1, n2, d, l1, l2, twc, tws)

    r, w = KRON_R, min(KRON_W, d)
    nh = n2 // r
    assert n1 % r == 0 and n2 % r == 0
    m5 = np.zeros((n2, n2), np.complex128)
    m6 = np.zeros((r * r, r * r), np.complex128)
    for s in range(r):
        for p in range(nh):
            for h in range(nh):
                m5[p * r + s, h * r + s] = np.exp(-2j * np.pi * (p * h / nh + p * s / n2))
        for q in range(r):
            for f in range(r):
                m6[q * r + f, f * r + s] = np.exp(-2j * np.pi * q * s / r)
    l5 = jnp.asarray(_cblock(m5), BF16)
    l6 = jnp.asarray(np.concatenate([m6.real, -m6.imag], axis=1), BF16)
    grp = pl.BlockSpec((1, r, n2, w), lambda bi, fh, k: (bi, fh, 0, k))
    y = pl.pallas_call(
        functools.partial(_fn_c_kernel, nh=nh), grid=(b, n1 // r, d // w),
        in_specs=[_const_spec(l5.shape), _const_spec(l6.shape), grp, grp],
        out_specs=pl.BlockSpec((1, n2, None, r, w), lambda bi, fh, k: (bi, 0, fh, 0, k)),
        out_shape=jax.ShapeDtypeStruct((b, n2, n1 // r, r, d), BF16),
        compiler_params=_cp("parallel", "parallel", "parallel"), name="fn_c",
    )(l5, l6, xr, xi)
    return fin(xl, y.reshape(b * n, d), mods_l, w_o.astype(BF16), b_o, tm, n)


def kernel(x, c, ctx, c_ctx, mod_w, mod_b, mix_pre_g, mix_post_g, ffn_pre_g, ffn_post_g, ffn_w13, ffn_w2,
           mla_w_in, mla_q_norm_g, mla_kv_norm_g, mla_w_uq, mla_w_ukv, mla_w_o,
           hy_w_in, hy_b_in, hy_conv_w, hy_conv_b, hy_f_w1, hy_f_b1, hy_f_freq, hy_f_w2, hy_f_b2, hy_f_w3,
           hy_skip, hy_w_out, hy_b_out,
           s5_lambda_re, s5_lambda_im, s5_log_dt, s5_b_re, s5_b_im, s5_c_re, s5_c_im, s5_d, s5_w_glu, s5_b_glu,
           fn_w_o, fn_b_o):
    b, n, d = x.shape
    cl = ctx.shape[1]
    depth = mod_w.shape[0]
    assert b == 2 and depth == 4, "two batches ride one complex transform; one layer per mixer"
    mods = _mods(c, c_ctx, mod_w, mod_b)
    xl = x.reshape(b * n, d)
    xc = ctx.reshape(b * cl, d)
    tm = min(512, n)

    def finisher(i):
        w13 = ffn_w13[i].astype(BF16)
        w2 = ffn_w2[i].astype(BF16)

        def fin(x_, y_, mods_, wm, bm, tm_, rows_per_batch, glu=False):
            return _mix_ffn_call(x_, y_, mods_, wm, bm, mix_post_g[i], ffn_pre_g[i], ffn_post_g[i], w13, w2,
                                 tm_, rows_per_batch, glu)
        return fin

    def mods_c(i):
        return jnp.broadcast_to(mods[i, 2:3], (b, 8, d))

    xl, xc = _mla_layer(xl, xc, mods[0, 0:2], mods_c(0), mix_pre_g[0], finisher(0), mla_w_in[0],
                        mla_q_norm_g[0], mla_kv_norm_g[0], mla_w_uq[0], mla_w_ukv[0], mla_w_o[0], b, n, cl)
    filt = (hy_f_w1[0], hy_f_b1[0], hy_f_freq[0], hy_f_w2[0], hy_f_b2[0], hy_f_w3[0])
    xl, xc = _hyena_layer(xl, xc, mods[1, 0:2], mods_c(1), mix_pre_g[1], finisher(1), hy_w_in[0], hy_b_in[0],
                          hy_conv_w[0], hy_conv_b[0], filt, hy_skip[0], hy_w_out[0], hy_b_out[0], b, n, cl)
    xl = _s5_layer(xl, xc, mods[2, 0:2], mods_c(2), mix_pre_g[2], finisher(2), s5_lambda_re[0],
                   s5_lambda_im[0], s5_log_dt[0], s5_b_re[0], s5_b_im[0], s5_c_re[0], s5_c_im[0], s5_d[0],
                   s5_w_glu[0], s5_b_glu[0], b, n, cl)
    xl = _fnet_layer(xl, mods[3, 0:2], mix_pre_g[3], finisher(3), fn_w_o[0], fn_b_o[0], b, n, d)
    return xl.reshape(b, n, d)
```

```python
import functools
import math

import numpy as np
import jax
import jax.numpy as jnp
from jax import lax
from jax.experimental import pallas as pl
from jax.experimental.pallas import tpu as pltpu

F32 = jnp.float32
BF16 = jnp.bfloat16
NORM_EPS = 1e-6
LANE = 128
MXU_COLS = 256
VMEM_LIMIT = 56 * 1024 * 1024
HIGHEST = lax.Precision.HIGHEST

GRID_W = 64
ROPE_THETA = 10000.0
MLA_HEADS = 8
MLA_NOPE = 128
MLA_ROPE = 64
MLA_V = 128
MLA_VT = MLA_V + 16
HYENA_BANDS = 16
HYENA_TARGET = 1e-2
HYENA_FAST = 0.3
HYENA_SLOW = 1.5
S5_GROUP = 16
S5_STATE = 64
S5_T = 16
FNET_GC = 128
FFT_N2 = 128


def _cp(*sem):
    return pltpu.CompilerParams(dimension_semantics=sem, vmem_limit_bytes=VMEM_LIMIT)


def _dot(a, b):
    return jnp.dot(a, b, preferred_element_type=F32)


def _dot_hi(a, b):
    return jnp.dot(a, b, preferred_element_type=F32, precision=HIGHEST)


def _dot_x3(a, b):
    a_hi = a.astype(BF16)
    b_hi = b.astype(BF16)
    a_lo = (a - a_hi.astype(F32)).astype(BF16)
    b_lo = (b - b_hi.astype(F32)).astype(BF16)
    return _dot(a_hi, b_hi) + (_dot(a_hi, b_lo) + _dot(a_lo, b_hi))


def _rms(x, g):
    ms = jnp.mean(x * x, axis=-1, keepdims=True)
    return x * lax.rsqrt(ms + NORM_EPS) * g


def _normmod(x, g, shift, scale):
    return _rms(x, g) * (1.0 + scale) + shift


def _const_spec(shape):
    nd = len(shape)
    return pl.BlockSpec(shape, lambda *_: (0,) * nd)


def _mods_kernel(st_ref, w_ref, b_ref, o_ref):
    st = st_ref[...]
    st = st * jax.nn.sigmoid(st)
    w = w_ref[0]
    rows = [jnp.sum(st[:, r:r + 1] * w, axis=0, keepdims=True) for r in range(3)]
    rows.append(jnp.zeros((5, w.shape[1]), F32))
    o_ref[0] = jnp.concatenate(rows, axis=0) + b_ref[0]


def _mods(c, c_ctx, mod_w, mod_b):
    depth, d, n6 = mod_w.shape
    st = jnp.zeros((d, 8), F32).at[:, 0:2].set(c.T).at[:, 2].set(c_ctx)
    tn = 1024
    out = pl.pallas_call(
        _mods_kernel,
        grid=(depth, n6 // tn),
        in_specs=[_const_spec((d, 8)),
                  pl.BlockSpec((1, d, tn), lambda i, j: (i, 0, j)),
                  pl.BlockSpec((1, 1, tn), lambda i, j: (i, 0, j))],
        out_specs=pl.BlockSpec((1, 8, tn), lambda i, j: (i, 0, j)),
        out_shape=jax.ShapeDtypeStruct((depth, 8, n6), F32),
        compiler_params=_cp("parallel", "parallel"),
        name="mods",
    )(st, mod_w, mod_b.reshape(depth, 1, n6))
    m = out[:, :3].reshape(depth, 3, n6 // d, d)
    return jnp.pad(m, ((0, 0), (0, 0), (0, 8 - n6 // d), (0, 0)))


def _row_specs(tm, d, tpb):
    x_spec = pl.BlockSpec((tm, d), lambda i: (i, 0))
    mod_spec = pl.BlockSpec((1, 8, d), lambda i: (i // tpb, 0, 0))
    return x_spec, mod_spec


def _normmod_kernel(x_ref, mod_ref, g_ref, o_ref):
    h = _normmod(x_ref[...], g_ref[...], mod_ref[0, 0:1, :], mod_ref[0, 1:2, :])
    o_ref[...] = h.astype(o_ref.dtype)


def _normmod_call(x, mods, g, tm, rows_per_batch):
    m, d = x.shape
    x_spec, mod_spec = _row_specs(tm, d, rows_per_batch // tm)
    return pl.pallas_call(
        _normmod_kernel, grid=(m // tm,),
        in_specs=[x_spec, mod_spec, _const_spec((1, d))],
        out_specs=x_spec, out_shape=jax.ShapeDtypeStruct((m, d), BF16),
        compiler_params=_cp("parallel"), name="normmod",
    )(x, mods, g.reshape(1, d))


def _mix_ffn_kernel(x_ref, y_ref, mod_ref, wm_ref, bm_ref, gm_ref, pre_ref, post_ref, w13_ref, w2_ref, o_ref,
                    *, f, cuts, glu):
    z = _dot(y_ref[...].astype(BF16), wm_ref[...]) + bm_ref[...]
    if glu:
        d = o_ref.shape[-1]
        z = z[:, :d] * jax.nn.sigmoid(z[:, d:])
    x = x_ref[...] + mod_ref[0, 2:3, :] * _rms(z, gm_ref[...])
    h = _normmod(x, pre_ref[...], mod_ref[0, 3:4, :], mod_ref[0, 4:5, :]).astype(BF16)
    acc = None
    for lo, hi in zip(cuts[:-1], cuts[1:]):
        a = _dot(h, w13_ref[:, lo:hi])
        b = _dot(h, w13_ref[:, f + lo:f + hi])
        gact = (a * jax.nn.sigmoid(a) * b).astype(BF16)
        part = _dot(gact, w2_ref[lo:hi, :])
        acc = part if acc is None else acc + part
    o_ref[...] = x + mod_ref[0, 5:6, :] * _rms(acc, post_ref[...])


def _mix_ffn_call(x, y, mods, wm, bm, mix_post_g, ffn_pre_g, ffn_post_g, w13, w2, layer, tm, rows_per_batch,
                  glu=False):
    m, d = x.shape
    k, nm = wm.shape
    f = w2.shape[1]
    cuts = tuple(range(0, f, 6 * MXU_COLS)) + (f,)
    x_spec, mod_spec = _row_specs(tm, d, rows_per_batch // tm)
    once = lambda shape: pl.BlockSpec(shape, lambda i: (0, 0), pipeline_mode=pl.Buffered(1))
    of_layer = lambda shape: pl.BlockSpec((None,) + shape, lambda i: (layer, 0, 0), pipeline_mode=pl.Buffered(1))
    return pl.pallas_call(
        functools.partial(_mix_ffn_kernel, f=f, cuts=cuts, glu=glu), grid=(m // tm,),
        in_specs=[x_spec, pl.BlockSpec((tm, k), lambda i: (i, 0)), mod_spec,
                  once((k, nm)), _const_spec((1, nm)), _const_spec((1, d)), _const_spec((1, d)),
                  _const_spec((1, d)), of_layer((d, 2 * f)), of_layer((f, d))],
        out_specs=x_spec, out_shape=jax.ShapeDtypeStruct((m, d), F32),
        compiler_params=_cp("parallel"), name="mix_ffn",
    )(x, y, mods, wm, bm.reshape(1, nm), mix_post_g.reshape(1, d), ffn_pre_g.reshape(1, d),
      ffn_post_g.reshape(1, d), w13, w2)


def _mla_proj_kernel(x_ref, mod_ref, g_ref, w_ref, qg_ref, kvg_ref, wq_ref, wkv_ref, cp_ref, sp_ref,
                     q_ref, k_ref, vt_ref, *, ql, kvl, qscale, positional):
    h = _normmod(x_ref[...], g_ref[...], mod_ref[0, 0:1, :], mod_ref[0, 1:2, :])
    z = _dot(h.astype(BF16), w_ref[...])
    qn = _rms(z[:, :ql], qg_ref[...]).astype(BF16)
    cn = _rms(z[:, ql:ql + kvl], kvg_ref[...]).astype(BF16)
    tk = z.shape[0]
    low = lax.broadcasted_iota(jnp.int32, (tk, 2 * MLA_ROPE), 1) < MLA_ROPE
    if positional:
        cpf = jnp.concatenate([cp_ref[...]] * 2, axis=1)
        spf = jnp.concatenate([sp_ref[...]] * 2, axis=1)

    def rope_slots(pair, keep_raw):
        swapped = pltpu.roll(pair, 64, 1)
        raw = jnp.where(low, 0.0, swapped) if keep_raw else None
        if not positional:
            return raw
        rot = jnp.where(low, pair * cpf + swapped * spf, 0.0)
        return rot + raw if keep_raw else rot

    kr = rope_slots(z[:, ql + kvl:], keep_raw=not positional).astype(BF16)

    zkv = _dot(cn, wkv_ref[...])
    ones_blk = (lax.broadcasted_iota(jnp.int32, (MLA_VT - MLA_V, tk), 0) == 0).astype(BF16)
    for hd in range(MLA_HEADS):
        base = hd * 256
        k_ref[0, hd, 0, :, 0:128] = zkv[:, base:base + 128].astype(BF16)
        k_ref[0, hd, 0, :, 128:256] = kr
        vt_ref[0, hd, 0, 0:MLA_V, :] = zkv[:, base + 128:base + 256].T.astype(BF16)
        vt_ref[0, hd, 0, MLA_V:MLA_VT, :] = ones_blk

    zq = _dot(qn, wq_ref[...])
    for hd in range(MLA_HEADS):
        base = hd * 256
        rp = rope_slots(zq[:, base + 128:base + 256], keep_raw=True)
        qcat = jnp.concatenate([zq[:, base:base + 128], rp], axis=1) * qscale
        q_ref[0, hd] = qcat.T.astype(BF16)


def _flash_kernel(q_ref, kc_ref, vc_ref, *rest, n_lat):
    if n_lat:
        kl_ref, vl_ref, o_ref, s_scr, acc_scr = rest
    else:
        o_ref, acc_scr = rest
    qt = q_ref[0, 0]

    def qk(k, slot):
        s = _dot(k, qt)
        s_scr[slot] = s
        return jnp.max(s, axis=0, keepdims=True)

    def sm_pv(slot, vt, m, mx):
        m_new = jnp.maximum(m, mx)
        alpha = jnp.exp2(m - m_new)
        p = jnp.exp2(s_scr[slot] - m_new).astype(BF16)
        acc_scr[...] = alpha * acc_scr[...] + _dot(vt, p)
        return m_new

    sc = _dot(kc_ref[0, 0, 0], qt)
    if n_lat:
        mx = qk(kl_ref[0, 0, 0], 0)
    m = jnp.max(sc, axis=0, keepdims=True)
    acc_scr[...] = _dot(vc_ref[0, 0, 0], jnp.exp2(sc - m).astype(BF16))
    if n_lat:

        per = 8 if n_lat % 8 == 0 else (4 if n_lat % 4 == 0 else 2)

        def body(i, carry):
            m, mx_cur = carry
            c = per * i
            for u in range(per):
                mx_next = qk(kl_ref[0, 0, jnp.minimum(c + u + 1, n_lat - 1)], (u + 1) % 2)
                m = sm_pv(u % 2, vl_ref[0, 0, c + u], m, mx_cur)
                mx_cur = mx_next
            return m, mx_cur

        lax.fori_loop(0, n_lat // per, body, (m, mx))
    acc = acc_scr[...]
    o_ref[0] = (acc[0:MLA_V] / acc[MLA_V:MLA_V + 1]).T.astype(o_ref.dtype)


def _rope_tables(n_lat):
    rows = n_lat // GRID_W
    row = jnp.repeat(jnp.arange(rows, dtype=F32), GRID_W)
    col = jnp.tile(jnp.arange(GRID_W, dtype=F32), rows)
    axis_dim = MLA_ROPE // 2
    inv_freq = 1.0 / (ROPE_THETA ** (jnp.arange(0, axis_dim, 2, dtype=F32) / axis_dim))
    ang_r = row[:, None] * inv_freq
    ang_c = col[:, None] * inv_freq
    cr, sr, cc, sc = jnp.cos(ang_r), jnp.sin(ang_r), jnp.cos(ang_c), jnp.sin(ang_c)
    cp = jnp.concatenate([cr, cr, cc, cc], axis=-1)
    sp = jnp.concatenate([-sr, sr, -sc, sc], axis=-1)
    return cp, sp


_ROPE_SWAP = np.concatenate([np.arange(16, 32), np.arange(0, 16), np.arange(48, 64), np.arange(32, 48)])


def _mla_side(x, mods, pre_g, w_in_ext, q_g, kv_g, w_uq_ext, w_ukv, tabs, b, n, tk, positional):
    m, d = x.shape
    ql, kvl = q_g.shape[-1], kv_g.shape[-1]
    nc = n // tk
    x_spec, mod_spec = _row_specs(tk, d, nc)
    tab_spec = pl.BlockSpec((tk, MLA_ROPE), lambda i: (i % nc, 0))
    qscale = (MLA_NOPE + MLA_ROPE) ** -0.5 * math.log2(math.e)
    return pl.pallas_call(
        functools.partial(_mla_proj_kernel, ql=ql, kvl=kvl, qscale=qscale, positional=positional),
        grid=(m // tk,),
        in_specs=[x_spec, mod_spec, _const_spec((1, d)), _const_spec(w_in_ext.shape),
                  _const_spec((1, ql)), _const_spec((1, kvl)), _const_spec(w_uq_ext.shape),
                  _const_spec(w_ukv.shape), tab_spec, tab_spec],
        out_specs=[pl.BlockSpec((1, MLA_HEADS, 256, tk), lambda i: (i // nc, 0, 0, i % nc)),
                   pl.BlockSpec((1, MLA_HEADS, 1, tk, 256), lambda i: (i // nc, 0, i % nc, 0, 0)),
                   pl.BlockSpec((1, MLA_HEADS, 1, MLA_VT, tk), lambda i: (i // nc, 0, i % nc, 0, 0))],
        out_shape=[jax.ShapeDtypeStruct((b, MLA_HEADS, 256, n), BF16),
                   jax.ShapeDtypeStruct((b, MLA_HEADS, nc, tk, 256), BF16),
                   jax.ShapeDtypeStruct((b, MLA_HEADS, nc, MLA_VT, tk), BF16)],
        compiler_params=_cp("parallel"), name="mla_proj",
    )(x, mods, pre_g.reshape(1, d), w_in_ext, q_g.reshape(1, ql), kv_g.reshape(1, kvl), w_uq_ext, w_ukv, *tabs)


def _flash_call(qt, kc, vtc, kl, vtl, tq):
    b, hh, _, n = qt.shape
    c = kc.shape[-2]
    n_lat = 0 if kl is None else kl.shape[2]
    in_specs = [pl.BlockSpec((1, 1, 256, tq), lambda bi, h, i: (bi, h, 0, i)),
                pl.BlockSpec((1, 1, 1, c, 256), lambda bi, h, i: (bi, h, 0, 0, 0)),
                pl.BlockSpec((1, 1, 1, MLA_VT, c), lambda bi, h, i: (bi, h, 0, 0, 0))]
    args = [qt, kc, vtc]
    scratch = [pltpu.VMEM((MLA_VT, tq), F32)]
    if n_lat:
        assert n_lat % 2 == 0, "latent key chunks are consumed in pairs"
        tk = kl.shape[-2]
        in_specs += [pl.BlockSpec((1, 1, n_lat, tk, 256), lambda bi, h, i: (bi, h, 0, 0, 0)),
                     pl.BlockSpec((1, 1, n_lat, MLA_VT, tk), lambda bi, h, i: (bi, h, 0, 0, 0))]
        args += [kl, vtl]
        scratch = [pltpu.VMEM((2, tk, tq), F32)] + scratch
    return pl.pallas_call(
        functools.partial(_flash_kernel, n_lat=n_lat), grid=(b, hh, n // tq),
        in_specs=in_specs,
        out_specs=pl.BlockSpec((1, tq, MLA_V), lambda bi, h, i: (bi, i, h)),
        out_shape=jax.ShapeDtypeStruct((b, n, hh * MLA_V), BF16),
        scratch_shapes=scratch,
        compiler_params=_cp("parallel", "parallel", "arbitrary"), name="flash",
    )(*args)


def _mla_layer(xl, xc, mods_l, mods_c, pre_g, fin, w_in, q_g, kv_g, w_uq, w_ukv, w_o, b, n, c):
    d = xl.shape[-1]
    ql, kvl = q_g.shape[-1], kv_g.shape[-1]
    hh = MLA_HEADS
    rope_cols = w_in[:, ql + kvl:]
    w_in_ext = jnp.concatenate([w_in, rope_cols[:, _ROPE_SWAP]], axis=1).astype(BF16)
    wq = w_uq.reshape(ql, hh, MLA_NOPE + MLA_ROPE)
    w_uq_ext = jnp.concatenate([wq, wq[:, :, MLA_NOPE:][:, :, _ROPE_SWAP]], axis=-1)
    w_uq_ext = w_uq_ext.reshape(ql, hh * 256).astype(BF16)
    w_ukv_b = w_ukv.astype(BF16)
    w_o_b = w_o.astype(BF16)

    tabs = _rope_tables(n)

    tm_l = min(512, n)
    tk_l = min(512, n // 2)
    ql_, kl, vtl = _mla_side(xl, mods_l, pre_g, w_in_ext, q_g, kv_g, w_uq_ext, w_ukv_b, tabs, b, n, tk_l, True)
    qc_, kc, vtc = _mla_side(xc, mods_c, pre_g, w_in_ext, q_g, kv_g, w_uq_ext, w_ukv_b, tabs, b, c, c, False)
    o_lat = _flash_call(ql_, kc, vtc, kl, vtl, min(1024, n)).reshape(b * n, hh * MLA_V)
    o_ctx = _flash_call(qc_, kc, vtc, None, None, c).reshape(b * c, hh * MLA_V)
    zb = jnp.zeros((d,), F32)
    xl = fin(xl, o_lat, mods_l, w_o_b, zb, tm_l, n)
    xc = fin(xc, o_ctx, mods_c, w_o_b, zb, c, c)
    return xl, xc


def _hy_in_kernel(x_ref, xp_ref, xn_ref, mod_ref, g_ref, w_ref, b_ref, cw_ref, cb_ref,
                  g0_ref, vg_ref, *, tpb):
    i = pl.program_id(0)
    g = g_ref[...]
    shift, scale = mod_ref[0, 0:1, :], mod_ref[0, 1:2, :]
    xcat = jnp.concatenate([xp_ref[...], x_ref[...], xn_ref[...]], axis=0)
    hcat = _normmod(xcat, g, shift, scale).astype(BF16)
    tm = x_ref.shape[0]
    d = g0_ref.shape[-1]
    first = (i % tpb) == 0
    last = (i % tpb) == tpb - 1
    ridx = lax.broadcasted_iota(jnp.int32, (tm, 1), 0)

    def conv_part(c):
        cols = slice(c * d, (c + 1) * d)
        ucat = _dot(hcat, w_ref[:, cols]) + b_ref[:, cols]
        u = ucat[8:tm + 8]
        prev_row = jnp.where(first, 0.0, ucat[7:8, :])
        next_row = jnp.where(last, 0.0, ucat[tm + 8:tm + 9, :])
        dn = jnp.where(ridx == 0, prev_row, pltpu.roll(u, 1, 0))
        upw = jnp.where(ridx == tm - 1, next_row, pltpu.roll(u, tm - 1, 0))
        return cb_ref[:, cols] + dn * cw_ref[0:1, cols] + u * cw_ref[1:2, cols] + upw * cw_ref[2:3, cols]

    g0_ref[...] = conv_part(0)
    vg_ref[...] = conv_part(2) * conv_part(1)


def _hy_in_call(x, mods, pre_g, w_in, b_in, conv_w, conv_b, tm, n):
    m, d = x.shape
    p = w_in.shape[1]
    tpb = n // tm
    x_spec, mod_spec = _row_specs(tm, d, tpb)
    r8 = tm // 8
    nb8 = m // 8
    prev_spec = pl.BlockSpec((8, d), lambda i: (jnp.maximum(i * r8 - 1, 0), 0))
    next_spec = pl.BlockSpec((8, d), lambda i: (jnp.minimum((i + 1) * r8, nb8 - 1), 0))
    cw = jnp.pad(conv_w, ((0, 8 - conv_w.shape[0]), (0, 0)))
    return pl.pallas_call(
        functools.partial(_hy_in_kernel, tpb=tpb), grid=(m // tm,),
        in_specs=[x_spec, prev_spec, next_spec, mod_spec, _const_spec((1, d)), _const_spec((d, p)),
                  _const_spec((1, p)), _const_spec((8, p)), _const_spec((1, p))],
        out_specs=[x_spec, x_spec],
        out_shape=[jax.ShapeDtypeStruct((m, d), F32), jax.ShapeDtypeStruct((m, d), F32)],
        compiler_params=_cp("parallel"), name="hy_in",
    )(x, x, x, mods, pre_g.reshape(1, d), w_in, b_in.reshape(1, p), cw, conv_b.reshape(1, p))


_PI_SPLIT = (3.140625, 9.67502593994140625e-4, 1.509957990978376432e-7)
_SIN_TAYLOR = (-1.0 / 6, 1.0 / 120, -1.0 / 5040, 1.0 / 362880, -1.0 / 39916800)


def _sin(x):
    kf = jnp.round(x * (1.0 / math.pi))
    r = ((x - kf * _PI_SPLIT[0]) - kf * _PI_SPLIT[1]) - kf * _PI_SPLIT[2]
    r2 = r * r
    p = _SIN_TAYLOR[4]
    for c in _SIN_TAYLOR[3::-1]:
        p = p * r2 + c
    s = r + r * r2 * p
    odd = (kf.astype(jnp.int32) & 1) == 1
    return jnp.where(odd, -s, s)


def _hy_filter_kernel(bands_ref, w1_ref, b1_ref, fq_ref, w2_ref, b2_ref, w3_ref, dl_ref,
                      k_ref, nrm_ref, *, n, tr):
    i = pl.program_id(0)
    bwd = i >= n // tr
    row = lax.broadcasted_iota(jnp.int32, (tr, LANE), 0) + i * tr
    j = jnp.where(bwd, 2 * n - row, row).astype(F32)
    lane = lax.broadcasted_iota(jnp.int32, (tr, LANE), 1)
    t = j * (1.0 / (n - 1))
    arg = (2.0 * math.pi / n) * j * bands_ref[0:1, :] + bands_ref[1:2, :]
    z = jnp.where(lane == 0, t, jnp.where(lane <= 2 * HYENA_BANDS, _sin(arg), 0.0))
    fq = fq_ref[...]
    a = _sin(fq * (_dot_hi(z, w1_ref[...]) + b1_ref[...]))
    for k in range(w2_ref.shape[0]):
        a = _sin(fq * (_dot_hi(a, w2_ref[k]) + b2_ref[k]))
    h = _dot_x3(a, w3_ref[jnp.where(bwd, 1, 0)])
    decay = jnp.exp(-t[:, 0:1] * dl_ref[...])
    k = h * decay
    k = jnp.where(row[:, 0:1] == n, 0.0, k)
    k_ref[...] = k
    part = jnp.sum(jnp.abs(k), axis=0, keepdims=True)

    @pl.when(i == 0)
    def _():
        nrm_ref[...] = jnp.zeros_like(nrm_ref)

    nrm_ref[...] += jnp.broadcast_to(part, nrm_ref.shape)


def _hy_filter_call(n, d, f_w1, f_b1, f_freq, f_w2, f_b2, f_w3):
    fw = f_w1.shape[1]
    tr = min(512, n)
    bands_np = np.zeros((8, LANE), np.float32)
    bands_np[0, 1:1 + HYENA_BANDS] = np.linspace(1e-4, HYENA_BANDS - 1, HYENA_BANDS, dtype=np.float32)
    bands_np[0, 1 + HYENA_BANDS:1 + 2 * HYENA_BANDS] = bands_np[0, 1:1 + HYENA_BANDS]
    bands_np[1, 1:1 + HYENA_BANDS] = 0.5 * np.pi
    bands_np[1, 1 + HYENA_BANDS:1 + 2 * HYENA_BANDS] = np.pi
    w1p = jnp.zeros((LANE, fw), F32).at[:f_w1.shape[0]].set(f_w1)
    deltas = jnp.abs(jnp.linspace(math.log(HYENA_TARGET) / HYENA_SLOW, math.log(HYENA_TARGET) / HYENA_FAST,
                                  d, dtype=F32)).reshape(1, d)
    row = pl.BlockSpec((tr, d), lambda i: (i, 0))
    return pl.pallas_call(
        functools.partial(_hy_filter_kernel, n=n, tr=tr), grid=(2 * n // tr,),
        in_specs=[_const_spec((8, LANE)), _const_spec((LANE, fw)), _const_spec((1, fw)), _const_spec((1, fw)),
                  _const_spec(f_w2.shape), _const_spec((f_w2.shape[0], 1, fw)), _const_spec((2, fw, d)),
                  _const_spec((1, d))],
        out_specs=[row, _const_spec((8, d))],
        out_shape=[jax.ShapeDtypeStruct((2 * n, d), F32), jax.ShapeDtypeStruct((8, d), F32)],
        compiler_params=_cp("arbitrary"), name="hy_filter",
    )(jnp.asarray(bands_np), w1p, f_b1.reshape(1, fw), f_freq.reshape(1, fw), f_w2,
      f_b2.reshape(f_w2.shape[0], 1, fw), jnp.transpose(f_w3.reshape(fw, 2, d), (1, 0, 2)), deltas)


def _dft_cs(nf, nt, period):
    ft = (np.arange(nf)[:, None] * np.arange(nt)[None, :]) % period
    ang = 2.0 * np.pi * ft / period
    return np.cos(ang), np.sin(ang)


def _twiddle_tables(n1, n2):
    nn = n1 * n2
    f1 = jnp.arange(n1, dtype=jnp.int32)
    t2 = jnp.arange(n2, dtype=jnp.int32)
    idx = (f1[:, None] * t2[None, :]) % nn
    ang = idx.astype(F32) * (2.0 * math.pi / nn)
    shape = idx.shape + (LANE,)
    return (jnp.broadcast_to(jnp.cos(ang)[..., None], shape),
            jnp.broadcast_to(jnp.sin(ang)[..., None], shape))


KRON_R = 16
KRON_W = 256


def _cblock(mc):
    return np.block([[mc.real, -mc.imag], [mc.imag, mc.real]])


def _split_radix(n1):
    b = 16 if (n1 % 16 == 0 and n1 >= 64) else 4
    assert n1 % b == 0 and (n1 // b) % 2 == 0
    return n1 // b, b


def _slab_dft_mats(n1, a_in, a_out, sign, neg_im=False, real_in=False, scale=1.0):
    a, b = _split_radix(n1)
    r = KRON_R
    eye = np.eye(r)
    w = lambda num, den: np.exp(sign * 2j * np.pi * num / den)
    ua = np.arange(a)[:, None] * np.arange(a)[None, :]
    vb = np.arange(b)[:, None] * np.arange(b)[None, :]
    if sign < 0:
        m1 = np.kron(w(ua[:, :a_in], a) * scale, eye)
        l1 = _cblock(m1)
        if neg_im:
            l1[:, a_in * r:] *= -1.0
        if real_in:
            l1 = l1[:, :a_in * r]
        l2 = np.stack([_cblock(np.kron(w(vb, b) * w(u * np.arange(b)[None, :], n1), eye)) for u in range(a)])
    else:
        l1 = np.stack([_cblock(np.kron(w(vb, b) * w(u * np.arange(b)[:, None], n1), eye)) for u in range(a)])
        l2 = _cblock(np.kron(w(ua[:a_out, :], a) * scale, eye))
    return jnp.asarray(l1, BF16), jnp.asarray(l2, BF16)


def _kfa_kernel(*refs, nparts, a_in, a, b):
    l1_ref, l2_ref = refs[0], refs[1]
    parts = refs[2:2 + nparts]
    twc_ref, tws_ref, or_ref, oi_ref = refs[2 + nparts:]
    r = KRON_R
    reps = or_ref.shape[-1] // LANE
    l1 = l1_ref[...]
    y1 = []
    for bb in range(b):
        x = jnp.concatenate([p[0, aa * b + bb] for p in parts for aa in range(a_in)], axis=0).astype(BF16)
        y1.append(_dot(l1, x).astype(BF16))
    for u in range(a):
        x = jnp.concatenate([y1[bb][u * r:(u + 1) * r] for bb in range(b)]
                            + [y1[bb][(a + u) * r:(a + u + 1) * r] for bb in range(b)], axis=0)
        z = _dot(l2_ref[u], x)
        for v in range(b):
            f1 = u + a * v
            zr = z[v * r:(v + 1) * r]
            zi = z[(b + v) * r:(b + v + 1) * r]
            c = jnp.concatenate([twc_ref[f1]] * reps, axis=1)
            sn = jnp.concatenate([tws_ref[f1]] * reps, axis=1)
            or_ref[0, f1] = (zr * c + zi * sn).astype(or_ref.dtype)
            oi_ref[0, f1] = (zi * c - zr * sn).astype(oi_ref.dtype)


def _kfa_call(parts, nb_out, n1, n2, d, l1, l2, twc, tws):
    a, b = _split_radix(n1)
    a_in = parts[0][0].shape[1] // b
    r, w = KRON_R, min(KRON_W, d)
    in_specs = [_const_spec(l1.shape), pl.BlockSpec(l2.shape, lambda bi, j, k: (0, 0, 0), pipeline_mode=pl.Buffered(1))]
    args = [l1, l2]
    for arr, bi_fixed in parts:
        t1_in = arr.shape[1]
        if bi_fixed is None:
            in_specs.append(pl.BlockSpec((1, t1_in, r, w), lambda bi, j, k: (bi, 0, j, k)))
        else:
            in_specs.append(pl.BlockSpec((1, t1_in, r, w), lambda bi, j, k, f=bi_fixed: (f, 0, j, k)))
        args.append(arr)
    tw_spec = pl.BlockSpec((n1, r, LANE), lambda bi, j, k: (0, j, 0))
    out_spec = pl.BlockSpec((1, n1, r, w), lambda bi, j, k: (bi, 0, j, k))
    out = jax.ShapeDtypeStruct((nb_out, n1, n2, d), BF16)
    return pl.pallas_call(
        functools.partial(_kfa_kernel, nparts=len(parts), a_in=a_in, a=a, b=b),
        grid=(nb_out, n2 // r, d // w),
        in_specs=in_specs + [tw_spec, tw_spec], out_specs=[out_spec, out_spec], out_shape=[out, out],
        compiler_params=_cp("parallel", "parallel", "parallel"), name="kfa",
    )(*args, twc, tws)


def _kfc_kernel(l3_ref, l4_ref, gr_ref, gi_ref, vg_ref, g0_ref, skip_ref, o_ref, *, a, b, a_out):
    r = KRON_R
    y3 = []
    for u in range(a):
        x = jnp.concatenate([gr_ref[0, u + a * v] for v in range(b)]
                            + [gi_ref[0, u + a * v] for v in range(b)], axis=0)
        y3.append(_dot(l3_ref[u], x).astype(BF16))
    l4 = l4_ref[...]
    skip = skip_ref[...]
    for bb in range(b):
        x = jnp.concatenate([y3[u][bb * r:(bb + 1) * r] for u in range(a)]
                            + [y3[u][(b + bb) * r:(b + bb + 1) * r] for u in range(a)], axis=0)
        y = _dot(l4, x)
        for sg in range(2):
            for aa in range(a_out):
                t1 = aa * b + bb
                yb = y[(sg * a_out + aa) * r:(sg * a_out + aa + 1) * r]
                o_ref[sg, t1] = ((yb + vg_ref[sg, t1] * skip) * g0_ref[sg, t1]).astype(o_ref.dtype)


def _kfc_call(g1r, g1i, vg4, g04, skip, n1, n2, d, l3, l4):
    a, b = _split_radix(n1)
    nb, t1_out = vg4.shape[:2]
    r, w = KRON_R, min(KRON_W, d)
    slab = pl.BlockSpec((1, n1, r, w), lambda j, k: (0, 0, j, k))
    nat = pl.BlockSpec((nb, t1_out, r, w), lambda j, k: (0, 0, j, k))
    return pl.pallas_call(
        functools.partial(_kfc_kernel, a=a, b=b, a_out=t1_out // b), grid=(n2 // r, d // w),
        in_specs=[pl.BlockSpec(l3.shape, lambda j, k: (0, 0, 0), pipeline_mode=pl.Buffered(1)),
                  _const_spec(l4.shape), slab, slab, nat, nat, pl.BlockSpec((1, w), lambda j, k: (0, k))],
        out_specs=nat, out_shape=jax.ShapeDtypeStruct(vg4.shape, BF16),
        compiler_params=_cp("parallel", "parallel"), name="kfc",
    )(l3, l4, g1r, g1i, vg4, g04, skip.reshape(1, d))


def _hy_kb_kernel(fm_ref, fr_ref, fi_ref, sc_ref, kr_ref, ki_ref, *, nf):
    fm = fm_ref[...]
    n2 = fr_ref.shape[1]
    sc = sc_ref[...]
    for s in range(nf):
        kk = _dot(fm, jnp.concatenate([fr_ref[s], fi_ref[s]], axis=0))
        kr_ref[s] = kk[:n2] * sc
        ki_ref[s] = kk[n2:] * sc


def _hy_b_kernel(fm_ref, fmc_ref, xr_ref, xi_ref, kr_ref, ki_ref, twc_ref, tws_ref,
                 or_ref, oi_ref, *, nf, d):
    fm = fm_ref[...]
    fmc = fmc_ref[...]
    n2 = xr_ref.shape[1]
    for s in range(nf):
        x = _dot(fm, jnp.concatenate([xr_ref[s], xi_ref[s]], axis=0))
        xr, xi = x[:n2], x[n2:]
        kr, ki = kr_ref[s], ki_ref[s]
        yr = (xr * kr - xi * ki).astype(BF16)
        yi = (xr * ki + xi * kr).astype(BF16)
        g = _dot(fmc, jnp.concatenate([yr, yi], axis=0))
        gr, gi = g[:n2], g[n2:]
        c = jnp.concatenate([twc_ref[s]] * (d // LANE), axis=1)
        sn = jnp.concatenate([tws_ref[s]] * (d // LANE), axis=1)
        or_ref[s] = (gr * c - gi * sn).astype(or_ref.dtype)
        oi_ref[s] = (gi * c + gr * sn).astype(oi_ref.dtype)


def _block_c(cs, sn, sign):
    return np.block([[cs, -sign * sn], [sign * sn, cs]])


def _hy_conv_long(vg, g0, k, nrm, skip, b, n, d):
    n2 = FFT_N2
    nn = 2 * n
    n1 = nn // n2
    rows = n // n2
    a, _ = _split_radix(n1)
    la1, la2 = _slab_dft_mats(n1, a // 2, 0, -1)
    lk1, lk2 = _slab_dft_mats(n1, a, 0, -1, real_in=True)
    lc3, lc4 = _slab_dft_mats(n1, 0, a // 2, +1)
    cs2, sn2 = _dft_cs(n2, n2, n2)
    fm_b = jnp.asarray(_block_c(cs2, sn2, -1.0), BF16)
    fm_bc = jnp.asarray(_block_c(cs2, sn2, 1.0), BF16)
    twc, tws = _twiddle_tables(n1, n2)

    kfr, kfi = _kfa_call([(k.reshape(1, n1, n2, d), 0)], 1, n1, n2, d, lk1, lk2, twc, tws)
    nf = 4 if n1 % 4 == 0 else 1
    scale = (1.0 / (nrm[0:1, :] * nn))
    slab = pl.BlockSpec((nf, n2, d), lambda j: (j, 0, 0))
    shp3 = (n1, n2, d)
    khr, khi = pl.pallas_call(
        functools.partial(_hy_kb_kernel, nf=nf), grid=(n1 // nf,),
        in_specs=[_const_spec(fm_b.shape), slab, slab, _const_spec((1, d))],
        out_specs=[slab, slab],
        out_shape=[jax.ShapeDtypeStruct(shp3, F32)] * 2,
        compiler_params=_cp("parallel"), name="hy_kb",
    )(fm_b, kfr.reshape(shp3), kfi.reshape(shp3), scale)

    vg4 = vg.reshape(b, rows, n2, d)
    g04 = g0.reshape(b, rows, n2, d)
    x1r, x1i = _kfa_call([(vg4, 0), (vg4, 1)], 1, n1, n2, d, la1, la2, twc, tws)
    tw_slab = pl.BlockSpec((nf, n2, LANE), lambda j: (j, 0, 0))
    g1r, g1i = pl.pallas_call(
        functools.partial(_hy_b_kernel, nf=nf, d=d), grid=(n1 // nf,),
        in_specs=[_const_spec(fm_b.shape), _const_spec(fm_bc.shape), slab, slab, slab, slab,
                  tw_slab, tw_slab],
        out_specs=[slab, slab],
        out_shape=[jax.ShapeDtypeStruct(shp3, BF16)] * 2,
        compiler_params=_cp("parallel"), name="hy_b",
    )(fm_b, fm_bc, x1r.reshape(shp3), x1i.reshape(shp3), khr, khi, twc, tws)
    out = _kfc_call(g1r.reshape(1, n1, n2, d), g1i.reshape(1, n1, n2, d), vg4, g04, skip, n1, n2, d, lc3, lc4)
    return out.reshape(b * n, d)


def _hy_short_kernel(fa_ref, fk_ref, fi_ref, vg_ref, g0_ref, k_ref, nrm_ref, skip_ref, o_ref, *, n):
    z = jnp.concatenate([vg_ref[0], vg_ref[1]], axis=0)
    x = _dot_hi(fa_ref[...], z)
    kk = _dot_hi(fk_ref[...], k_ref[...])
    nn = 2 * n
    sc = 1.0 / (nrm_ref[0:1, :] * nn)
    xr, xi = x[:nn], x[nn:]
    kr, ki = kk[:nn] * sc, kk[nn:] * sc
    y = _dot_hi(fi_ref[...], jnp.concatenate([xr * kr - xi * ki, xr * ki + xi * kr], axis=0))
    skip = skip_ref[...]
    for bi in range(2):
        o_ref[bi] = ((y[bi * n:(bi + 1) * n] + vg_ref[bi] * skip) * g0_ref[bi]).astype(o_ref.dtype)


def _hy_conv_short(vg, g0, k, nrm, skip, b, n, d):
    nn = 2 * n
    cs, sn = _dft_cs(nn, n, nn)
    fa = jnp.asarray(_block_c(cs, sn, -1.0), F32)
    csk, snk = _dft_cs(nn, nn, nn)
    fk = jnp.asarray(np.concatenate([csk, -snk], axis=0), F32)
    csi, sni = _dft_cs(n, nn, nn)
    fi = jnp.asarray(_block_c(csi, sni, 1.0), F32)
    cb = 256
    col3 = pl.BlockSpec((b, n, cb), lambda j: (0, 0, j))
    vec = pl.BlockSpec((1, cb), lambda j: (0, j))
    out = pl.pallas_call(
        functools.partial(_hy_short_kernel, n=n), grid=(d // cb,),
        in_specs=[_const_spec(fa.shape), _const_spec(fk.shape), _const_spec(fi.shape), col3, col3,
                  pl.BlockSpec((nn, cb), lambda j: (0, j)), pl.BlockSpec((8, cb), lambda j: (0, j)), vec],
        out_specs=col3, out_shape=jax.ShapeDtypeStruct((b, n, d), BF16),
        compiler_params=_cp("parallel"), name="hy_short",
    )(fa, fk, fi, vg.reshape(b, n, d), g0.reshape(b, n, d), k, nrm, skip.reshape(1, d))
    return out.reshape(b * n, d)


def _hyena_layer(xl, xc, mods_l, mods_c, pre_g, fin, w_in, b_in, conv_w, conv_b, filt, skip,
                 w_out, b_out, b, n, c):
    d = xl.shape[-1]
    w_in_b = w_in.astype(BF16)
    w_out_b = w_out.astype(BF16)
    tm = min(512, n)
    g0, vg = _hy_in_call(xl, mods_l, pre_g, w_in_b, b_in, conv_w, conv_b, tm, n)
    k, nrm = _hy_filter_call(n, d, *filt)
    u_out = _hy_conv_long(vg, g0, k, nrm, skip[0], b, n, d)
    xl = fin(xl, u_out, mods_l, w_out_b, b_out, tm, n)

    g0c, vgc = _hy_in_call(xc, mods_c, pre_g, w_in_b, b_in, conv_w, conv_b, c, c)
    kc, nrmc = _hy_filter_call(c, d, *filt)
    u_out_c = _hy_conv_short(vgc, g0c, kc, nrmc, skip[0], b, c, d)
    xc = fin(xc, u_out_c, mods_c, w_out_b, b_out, c, c)
    return xl, xc


def _s5_operators(lam_re, lam_im, log_dt, b_re, b_im, c_re, c_im, d_skip):
    t = S5_T
    g, ns = lam_re.shape[1], lam_re.shape[2]
    gc = b_re.shape[-1]
    gl = LANE // gc
    nblk = g // gl
    lam = lax.complex(lam_re, lam_im)
    dt = jnp.exp(log_dt)[..., None]
    lam_bar = jnp.exp(lam * dt)
    b_bar = ((lam_bar - 1.0) / lam)[..., None] * lax.complex(b_re, b_im)
    c_mat = lax.complex(c_re, c_im)
    pw = jnp.arange(t + 1, dtype=F32)
    lam_pw = jnp.exp((lam * dt)[None] * pw[:, None, None, None])
    hp = HIGHEST
    kern = jnp.einsum('dgcn,tdgn,dgne->dgtce', c_mat, lam_pw[:t], b_bar, precision=hp).real
    dsk = d_skip.reshape(g, gc)
    kt = jnp.swapaxes(kern, -1, -2)
    centre = kt[0][:, 0] + kt[1][:, 0] + jnp.eye(gc, dtype=F32)[None] * dsk[:, :, None]
    ks = jnp.concatenate([kt[1][:, 1:][:, ::-1], centre[:, None], kt[0][:, 1:]], axis=1)
    ks = jnp.transpose(ks.reshape(nblk, gl, 2 * t - 1, gc, gc), (0, 2, 1, 3, 4)).reshape(nblk, 2 * t - 1, LANE, gc)
    same_group = (np.arange(LANE)[:, None] // gc == np.arange(LANE)[None, :] // gc).astype(np.float32)
    d_tab = jnp.tile(ks, (1, 1, 1, gl)) * same_group

    def compact(zc, im_sign):
        z = jnp.concatenate([zc.real, im_sign * zc.imag], axis=-1)
        z = jnp.transpose(z, (2, 1, 0, 3, 4)).reshape(nblk, gl, t, 2, gc, 2 * ns)
        return jnp.transpose(z, (0, 2, 1, 3, 4, 5))

    b_t = jnp.swapaxes(b_bar, -1, -2)
    pf = lam_pw[:t][::-1][:, 0, :, None, :] * b_t[0][None]
    pb = lam_pw[:t][:, 1, :, None, :] * b_t[1][None]
    p_tab = compact(jnp.stack([pf, pb], axis=0), 1.0)
    qf = c_mat[0][None] * lam_pw[1:t + 1, 0][:, :, None, :]
    qb = c_mat[1][None] * lam_pw[1:t + 1][::-1][:, 1][:, :, None, :]
    q_tab = compact(jnp.stack([qf, qb], axis=0), -1.0)

    a = lam_pw[t]
    m_op, p_op, q_op = _s5_expand(d_tab, p_tab, q_tab)
    return m_op, p_op, q_op, a.real.reshape(2, g * ns), a.imag.reshape(2, g * ns)


def _s5_m_kernel(d_ref, o_ref, *, t):
    for s in range(t):
        for tt in range(t):
            o_ref[0, s * LANE:(s + 1) * LANE, tt * LANE:(tt + 1) * LANE] = d_ref[0, tt - s + t - 1].astype(o_ref.dtype)


def _s5_pq_kernel(c_ref, o_ref, *, transpose):
    t, gl, nd, gc, w = c_ref.shape[1:]
    ns = w // 2
    lane_grp = lax.broadcasted_iota(jnp.int32, (gc, gl * ns), 1) // ns
    for j in range(t):
        rows = []
        for g in range(gl):
            cols = []
            for dd in range(nd):
                piece = c_ref[0, j, g, dd]
                for ri in range(2):
                    tiled = jnp.concatenate([piece[:, ri * ns:(ri + 1) * ns]] * gl, axis=1)
                    cols.append(jnp.where(lane_grp == g, tiled, 0.0))
            rows.append(jnp.concatenate(cols, axis=1))
        blk = jnp.concatenate(rows, axis=0)
        if transpose:
            o_ref[0, :, j * LANE:(j + 1) * LANE] = blk.T.astype(o_ref.dtype)
        else:
            o_ref[0, j * LANE:(j + 1) * LANE, :] = blk.astype(o_ref.dtype)


def _s5_expand(d_tab, p_tab, q_tab):
    nblk, nlag = d_tab.shape[:2]
    t = (nlag + 1) // 2
    _, _, gl, nd, gc, w = p_tab.shape
    ncol = nd * gl * w
    whole = lambda shape: pl.BlockSpec((1,) + shape, lambda b: (b,) + (0,) * len(shape))
    m_op = pl.pallas_call(
        functools.partial(_s5_m_kernel, t=t), grid=(nblk,),
        in_specs=[whole((nlag, LANE, LANE))], out_specs=whole((t * LANE, t * LANE)),
        out_shape=jax.ShapeDtypeStruct((nblk, t * LANE, t * LANE), BF16),
        compiler_params=_cp("parallel"), name="s5_m_op",
    )(d_tab)
    tab_spec = whole((t, gl, nd, gc, w))
    p_op = pl.pallas_call(
        functools.partial(_s5_pq_kernel, transpose=False), grid=(nblk,),
        in_specs=[tab_spec], out_specs=whole((t * LANE, ncol)),
        out_shape=jax.ShapeDtypeStruct((nblk, t * LANE, ncol), BF16),
        compiler_params=_cp("parallel"), name="s5_p_op",
    )(p_tab)
    q_op = pl.pallas_call(
        functools.partial(_s5_pq_kernel, transpose=True), grid=(nblk,),
        in_specs=[tab_spec], out_specs=whole((ncol, t * LANE)),
        out_shape=jax.ShapeDtypeStruct((nblk, ncol, t * LANE), BF16),
        compiler_params=_cp("parallel"), name="s5_q_op",
    )(q_tab)
    return m_op, p_op, q_op


def _s5_sum_kernel(*refs, t):
    u_refs = refs[:t]
    p_ref = refs[t]
    outs = refs[t + 1:]
    u = jnp.concatenate([r[...] for r in u_refs], axis=1)
    res = _dot(u, p_ref[0])
    w = res.shape[1] // len(outs)
    for i, o in enumerate(outs):
        o[...] = res[:, i * w:(i + 1) * w]


def _s5_sum_call(h, p_op, rb):
    rows = h.shape[0]
    t = S5_T
    nblk = p_op.shape[0]
    w = p_op.shape[2] // 4
    u_specs = [pl.BlockSpec((rb, LANE), lambda gb, r, s=s: (r, s * nblk + gb)) for s in range(t)]
    out_spec = pl.BlockSpec((rb, w), lambda gb, r: (r, gb))
    return pl.pallas_call(
        functools.partial(_s5_sum_kernel, t=t), grid=(nblk, rows // rb),
        in_specs=u_specs + [pl.BlockSpec((1,) + p_op.shape[1:], lambda gb, r: (gb, 0, 0))],
        out_specs=[out_spec] * 4,
        out_shape=[jax.ShapeDtypeStruct((rows, nblk * w), F32)] * 4,
        compiler_params=_cp("parallel", "parallel"), name="s5_sum",
    )(*([h] * t), p_op)


def _s5_rec_kernel(sr_ref, si_ref, ar_ref, ai_ref, h0r_ref, h0i_ref, hr_ref, hi_ref, fr_ref, fi_ref,
                   cr, ci, *, kb, reverse):
    @pl.when(pl.program_id(1) == 0)
    def _():
        cr[...] = h0r_ref[0]
        ci[...] = h0i_ref[0]

    ar, ai = ar_ref[...], ai_ref[...]

    def body(i, carry):
        hr, hi = carry
        k = kb - 1 - i if reverse else i
        hr_ref[pl.ds(k, 1), :] = hr
        hi_ref[pl.ds(k, 1), :] = hi
        nr = ar * hr - ai * hi + sr_ref[pl.ds(k, 1), :]
        ni = ar * hi + ai * hr + si_ref[pl.ds(k, 1), :]
        return nr, ni

    hr, hi = lax.fori_loop(0, kb, body, (cr[...], ci[...]))
    cr[...] = hr
    ci[...] = hi
    fr_ref[0] = hr
    fi_ref[0] = hi


def _s5_rec_call(sr, si, ar, ai, h0r, h0i, nb_batch, reverse):
    rows, w = sr.shape
    nk = rows // nb_batch
    kb = min(64, nk)
    nb = nk // kb
    blk = (lambda bi, i: (bi * nb + nb - 1 - i, 0)) if reverse else (lambda bi, i: (bi * nb + i, 0))
    row_spec = pl.BlockSpec((kb, w), blk)
    vec = _const_spec((1, w))
    st = pl.BlockSpec((1, 1, w), lambda bi, i: (bi, 0, 0))
    return pl.pallas_call(
        functools.partial(_s5_rec_kernel, kb=kb, reverse=reverse), grid=(nb_batch, nb),
        in_specs=[row_spec, row_spec, vec, vec, st, st],
        out_specs=[row_spec, row_spec, st, st],
        out_shape=[jax.ShapeDtypeStruct((rows, w), F32)] * 2 + [jax.ShapeDtypeStruct((nb_batch, 1, w), F32)] * 2,
        scratch_shapes=[pltpu.VMEM((1, w), F32), pltpu.VMEM((1, w), F32)],
        compiler_params=_cp("parallel", "arbitrary"), name="s5_rec",
    )(sr, si, ar, ai, h0r, h0i)


def _s5_out_kernel(*refs, t):
    u_refs = refs[:t]
    h_refs = refs[t:t + 4]
    m_ref, q_ref, o_ref = refs[t + 4:]
    u = jnp.concatenate([r[...] for r in u_refs], axis=1)
    hcat = jnp.concatenate([r[...].astype(BF16) for r in h_refs], axis=1)
    y = _dot(u, m_ref[0]) + _dot(hcat, q_ref[0])
    o_ref[0] = (0.5 * y * (1.0 + lax.erf(y * (2.0 ** -0.5)))).astype(o_ref.dtype)


def _s5_out_call(h, states, m_op, q_op, rb):
    rows = h.shape[0]
    t = S5_T
    nblk = m_op.shape[0]
    w = q_op.shape[1] // 4
    u_specs = [pl.BlockSpec((rb, LANE), lambda gb, r, s=s: (r, s * nblk + gb)) for s in range(t)]
    return pl.pallas_call(
        functools.partial(_s5_out_kernel, t=t), grid=(nblk, rows // rb),
        in_specs=u_specs + [pl.BlockSpec((rb, w), lambda gb, r: (r, gb))] * 4
        + [pl.BlockSpec((1,) + m_op.shape[1:], lambda gb, r: (gb, 0, 0)),
           pl.BlockSpec((1,) + q_op.shape[1:], lambda gb, r: (gb, 0, 0))],
        out_specs=pl.BlockSpec((1, rb, t * LANE), lambda gb, r: (gb, r, 0)),
        out_shape=jax.ShapeDtypeStruct((nblk, rows, t * LANE), BF16),
        compiler_params=_cp("parallel", "parallel"), name="s5_out",
    )(*([h] * t), *states, m_op, q_op)


def _s5_layer(xl, xc, mods_l, mods_c, pre_g, fin, lam_re, lam_im, log_dt, b_re, b_im, c_re, c_im,
              d_skip, w_glu, b_glu, b, n, c):
    d = xl.shape[-1]
    t = S5_T
    m_op, p_op, q_op, a_re, a_im = _s5_operators(lam_re, lam_im, log_dt, b_re, b_im, c_re, c_im, d_skip)
    nblk = m_op.shape[0]
    w = a_re.shape[-1]
    hl = _normmod_call(xl, mods_l, pre_g, min(512, n), n).reshape(b * n // t, t * d)
    hc = _normmod_call(xc, mods_c, pre_g, c, c).reshape(b * c // t, t * d)

    def scan(h, init):
        sfr, sfi, sbr, sbi = _s5_sum_call(h, p_op, min(512, h.shape[0]))
        hfr, hfi, ffr, ffi = _s5_rec_call(sfr, sfi, a_re[0:1], a_im[0:1], init[0], init[1], b, False)
        hbr, hbi, fbr, fbi = _s5_rec_call(sbr, sbi, a_re[1:2], a_im[1:2], init[2], init[3], b, True)
        return (hfr, hfi, hbr, hbi), (ffr, ffi, fbr, fbi)

    zeros = jnp.zeros((b, 1, w), F32)
    _, ctx_final = scan(hc, (zeros,) * 4)
    states, _ = scan(hl, ctx_final)
    nk = n // t
    gact = _s5_out_call(hl, states, m_op, q_op, min(512, b * nk))
    g_nat = jnp.transpose(gact.reshape(nblk, b * nk, t, LANE), (1, 2, 0, 3)).reshape(b * n, d)
    return fin(xl, g_nat, mods_l, w_glu.astype(BF16), b_glu, min(512, n), n, glu=True)


def _fn_pre_kernel(x_ref, mod_ref, g_ref, cs_ref, a_ref, b_ref):
    h = _normmod(x_ref[...], g_ref[...], mod_ref[0, 0:1, :], mod_ref[0, 1:2, :]).astype(BF16)
    cs = cs_ref[...]
    gc = FNET_GC
    ab = [_dot(h[:, k * gc:(k + 1) * gc], cs) for k in range(h.shape[1] // gc)]
    a_ref[...] = jnp.concatenate([z[:, :gc] for z in ab], axis=1).astype(a_ref.dtype)
    b_ref[...] = jnp.concatenate([z[:, gc:] for z in ab], axis=1).astype(b_ref.dtype)


def _fn_c_kernel(l5_ref, l6_ref, xr_ref, xi_ref, o_ref, *, nh):
    r = KRON_R
    n2 = nh * r
    l5 = l5_ref[...]
    y5 = [_dot(l5, jnp.concatenate([xr_ref[0, f], xi_ref[0, f]], axis=0)).astype(BF16) for f in range(r)]
    l6 = l6_ref[...]
    for p in range(nh):
        x = jnp.concatenate([y5[f][p * r:(p + 1) * r] for f in range(r)]
                            + [y5[f][n2 + p * r:n2 + (p + 1) * r] for f in range(r)], axis=0)
        out = _dot(l6, x)
        for q in range(r):
            o_ref[0, p + nh * q] = out[q * r:(q + 1) * r].astype(o_ref.dtype)


def _fnet_layer(xl, mods_l, pre_g, fin, w_o, b_o, b, n, d):
    n2 = FFT_N2
    n1 = n // n2
    gc = FNET_GC
    cc, sc = _dft_cs(gc, gc, gc)
    cs = jnp.asarray(np.concatenate([cc, sc], axis=1) / np.sqrt(gc), BF16)
    tm = min(512, n)
    x_spec, mod_spec = _row_specs(tm, d, n // tm)
    ab = jax.ShapeDtypeStruct((b * n, d), BF16)
    am, bm = pl.pallas_call(
        _fn_pre_kernel, grid=(b * n // tm,),
        in_specs=[x_spec, mod_spec, _const_spec((1, d)), _const_spec(cs.shape)],
        out_specs=[x_spec, x_spec], out_shape=[ab, ab],
        compiler_params=_cp("parallel"), name="fn_pre",
    )(xl, mods_l, pre_g.reshape(1, d), cs)

    a, _ = _split_radix(n1)
    l1, l2 = _slab_dft_mats(n1, a, 0, -1, neg_im=True, scale=1.0 / np.sqrt(n))
    twc, tws = _twiddle_tables(n1, n2)
    a4, b4 = am.reshape(b, n1, n2, d), bm.reshape(b, n1, n2, d)
    xr, xi = _kfa_call([(a4, None), (b4, None)], b, n1, n2, d, l1, l2, twc, tws)

    r, w = KRON_R, min(KRON_W, d)
    nh = n2 // r
    assert n1 % r == 0 and n2 % r == 0
    m5 = np.zeros((n2, n2), np.complex128)
    m6 = np.zeros((r * r, r * r), np.complex128)
    for s in range(r):
        for p in range(nh):
            for h in range(nh):
                m5[p * r + s, h * r + s] = np.exp(-2j * np.pi * (p * h / nh + p * s / n2))
        for q in range(r):
            for f in range(r):
                m6[q * r + f, f * r + s] = np.exp(-2j * np.pi * q * s / r)
    l5 = jnp.asarray(_cblock(m5), BF16)
    l6 = jnp.asarray(np.concatenate([m6.real, -m6.imag], axis=1), BF16)
    grp = pl.BlockSpec((1, r, n2, w), lambda bi, fh, k: (bi, fh, 0, k))
    y = pl.pallas_call(
        functools.partial(_fn_c_kernel, nh=nh), grid=(b, n1 // r, d // w),
        in_specs=[_const_spec(l5.shape), _const_spec(l6.shape), grp, grp],
        out_specs=pl.BlockSpec((1, n2, None, r, w), lambda bi, fh, k: (bi, 0, fh, 0, k)),
        out_shape=jax.ShapeDtypeStruct((b, n2, n1 // r, r, d), BF16),
        compiler_params=_cp("parallel", "parallel", "parallel"), name="fn_c",
    )(l5, l6, xr, xi)
    return fin(xl, y.reshape(b * n, d), mods_l, w_o.astype(BF16), b_o, tm, n)


def kernel(x, c, ctx, c_ctx, mod_w, mod_b, mix_pre_g, mix_post_g, ffn_pre_g, ffn_post_g, ffn_w13, ffn_w2,
           mla_w_in, mla_q_norm_g, mla_kv_norm_g, mla_w_uq, mla_w_ukv, mla_w_o,
           hy_w_in, hy_b_in, hy_conv_w, hy_conv_b, hy_f_w1, hy_f_b1, hy_f_freq, hy_f_w2, hy_f_b2, hy_f_w3,
           hy_skip, hy_w_out, hy_b_out,
           s5_lambda_re, s5_lambda_im, s5_log_dt, s5_b_re, s5_b_im, s5_c_re, s5_c_im, s5_d, s5_w_glu, s5_b_glu,
           fn_w_o, fn_b_o):
    b, n, d = x.shape
    cl = ctx.shape[1]
    depth = mod_w.shape[0]
    assert b == 2 and depth == 4, "two batches ride one complex transform; one layer per mixer"
    mods = _mods(c, c_ctx, mod_w, mod_b)
    xl = x.reshape(b * n, d)
    xc = ctx.reshape(b * cl, d)

    w13_all = ffn_w13.astype(BF16)
    w2_all = ffn_w2.astype(BF16)

    def finisher(i):
        def fin(x_, y_, mods_, wm, bm, tm_, rows_per_batch, glu=False):
            return _mix_ffn_call(x_, y_, mods_, wm, bm, mix_post_g[i], ffn_pre_g[i], ffn_post_g[i],
                                 w13_all, w2_all, i, tm_, rows_per_batch, glu)
        return fin

    def mods_c(i):
        return jnp.broadcast_to(mods[i, 2:3], (b, 8, d))

    xl, xc = _mla_layer(xl, xc, mods[0, 0:2], mods_c(0), mix_pre_g[0], finisher(0), mla_w_in[0],
                        mla_q_norm_g[0], mla_kv_norm_g[0], mla_w_uq[0], mla_w_ukv[0], mla_w_o[0], b, n, cl)
    filt = (hy_f_w1[0], hy_f_b1[0], hy_f_freq[0], hy_f_w2[0], hy_f_b2[0], hy_f_w3[0])
    xl, xc = _hyena_layer(xl, xc, mods[1, 0:2], mods_c(1), mix_pre_g[1], finisher(1), hy_w_in[0], hy_b_in[0],
                          hy_conv_w[0], hy_conv_b[0], filt, hy_skip[0], hy_w_out[0], hy_b_out[0], b, n, cl)
    xl = _s5_layer(xl, xc, mods[2, 0:2], mods_c(2), mix_pre_g[2], finisher(2), s5_lambda_re[0],
                   s5_lambda_im[0], s5_log_dt[0], s5_b_re[0], s5_b_im[0], s5_c_re[0], s5_c_im[0], s5_d[0],
                   s5_w_glu[0], s5_b_glu[0], b, n, cl)
    xl = _fnet_layer(xl, mods[3, 0:2], mix_pre_g[3], finisher(3), fn_w_o[0], fn_b_o[0], b, n, d)
    return xl.reshape(b, n, d)
```

```python
import functools
import math

import numpy as np
import jax
import jax.numpy as jnp
from jax import lax
from jax.experimental import pallas as pl
from jax.experimental.pallas import tpu as pltpu

F32 = jnp.float32
BF16 = jnp.bfloat16
NORM_EPS = 1e-6
LANE = 128
MXU_COLS = 256
VMEM_LIMIT = 56 * 1024 * 1024
HIGHEST = lax.Precision.HIGHEST

GRID_W = 64
ROPE_THETA = 10000.0
MLA_HEADS = 8
MLA_NOPE = 128
MLA_ROPE = 64
MLA_V = 128
MLA_VT = MLA_V + 16
HYENA_BANDS = 16
HYENA_TARGET = 1e-2
HYENA_FAST = 0.3
HYENA_SLOW = 1.5
S5_GROUP = 16
S5_STATE = 64
S5_T = 16
FNET_GC = 128
FFT_N2 = 128


def _cp(*sem):
    return pltpu.CompilerParams(dimension_semantics=sem, vmem_limit_bytes=VMEM_LIMIT)


def _dot(a, b):
    return jnp.dot(a, b, preferred_element_type=F32)


def _dot_hi(a, b):
    return jnp.dot(a, b, preferred_element_type=F32, precision=HIGHEST)


def _dot_x3(a, b):
    a_hi = a.astype(BF16)
    b_hi = b.astype(BF16)
    a_lo = (a - a_hi.astype(F32)).astype(BF16)
    b_lo = (b - b_hi.astype(F32)).astype(BF16)
    return _dot(a_hi, b_hi) + (_dot(a_hi, b_lo) + _dot(a_lo, b_hi))


def _rms(x, g):
    ms = jnp.mean(x * x, axis=-1, keepdims=True)
    return x * lax.rsqrt(ms + NORM_EPS) * g


def _normmod(x, g, shift, scale):
    return _rms(x, g) * (1.0 + scale) + shift


def _const_spec(shape):
    nd = len(shape)
    return pl.BlockSpec(shape, lambda *_: (0,) * nd)


def _mods_kernel(st_ref, w_ref, b_ref, o_ref):
    st = st_ref[...]
    st = st * jax.nn.sigmoid(st)
    w = w_ref[0]
    rows = [jnp.sum(st[:, r:r + 1] * w, axis=0, keepdims=True) for r in range(3)]
    rows.append(jnp.zeros((5, w.shape[1]), F32))
    o_ref[0] = jnp.concatenate(rows, axis=0) + b_ref[0]


def _mods(c, c_ctx, mod_w, mod_b):
    depth, d, n6 = mod_w.shape
    st = jnp.zeros((d, 8), F32).at[:, 0:2].set(c.T).at[:, 2].set(c_ctx)
    tn = 1024
    out = pl.pallas_call(
        _mods_kernel,
        grid=(depth, n6 // tn),
        in_specs=[_const_spec((d, 8)),
                  pl.BlockSpec((1, d, tn), lambda i, j: (i, 0, j)),
                  pl.BlockSpec((1, 1, tn), lambda i, j: (i, 0, j))],
        out_specs=pl.BlockSpec((1, 8, tn), lambda i, j: (i, 0, j)),
        out_shape=jax.ShapeDtypeStruct((depth, 8, n6), F32),
        compiler_params=_cp("parallel", "parallel"),
        name="mods",
    )(st, mod_w, mod_b.reshape(depth, 1, n6))
    m = out[:, :3].reshape(depth, 3, n6 // d, d)
    return jnp.pad(m, ((0, 0), (0, 0), (0, 8 - n6 // d), (0, 0)))


def _row_specs(tm, d, tpb):
    x_spec = pl.BlockSpec((tm, d), lambda i: (i, 0))
    mod_spec = pl.BlockSpec((1, 8, d), lambda i: (i // tpb, 0, 0))
    return x_spec, mod_spec


def _channel_dft(h, cs):
    gc = FNET_GC
    ab = [_dot(h[:, k * gc:(k + 1) * gc], cs) for k in range(h.shape[1] // gc)]
    return (jnp.concatenate([z[:, :gc] for z in ab], axis=1), jnp.concatenate([z[:, gc:] for z in ab], axis=1))


def _mix_ffn_kernel(x_ref, y_ref, mod_ref, wm_ref, bm_ref, gm_ref, pre_ref, post_ref, w13_ref, w2_ref, *rest,
                    f, cuts, glu, emit):
    if emit == "normmod":
        nmod_ref, ng_ref, o_ref, h_ref = rest
    elif emit == "fnet":
        nmod_ref, ng_ref, cs_ref, o_ref, a_ref, b_ref = rest
    else:
        (o_ref,) = rest
    z = _dot(y_ref[...].astype(BF16), wm_ref[...]) + bm_ref[...]
    if glu:
        d = o_ref.shape[-1]
        z = z[:, :d] * jax.nn.sigmoid(z[:, d:])
    x = x_ref[...] + mod_ref[0, 2:3, :] * _rms(z, gm_ref[...])
    h = _normmod(x, pre_ref[...], mod_ref[0, 3:4, :], mod_ref[0, 4:5, :]).astype(BF16)
    acc = None
    for lo, hi in zip(cuts[:-1], cuts[1:]):
        a = _dot(h, w13_ref[:, lo:hi])
        b = _dot(h, w13_ref[:, f + lo:f + hi])
        gact = (a * jax.nn.sigmoid(a) * b).astype(BF16)
        part = _dot(gact, w2_ref[lo:hi, :])
        acc = part if acc is None else acc + part
    xo = x + mod_ref[0, 5:6, :] * _rms(acc, post_ref[...])
    o_ref[...] = xo
    if emit:
        hn = _normmod(xo, ng_ref[...], nmod_ref[0, 0:1, :], nmod_ref[0, 1:2, :]).astype(BF16)
        if emit == "normmod":
            h_ref[...] = hn
        else:
            a, b = _channel_dft(hn, cs_ref[...])
            a_ref[...] = a.astype(a_ref.dtype)
            b_ref[...] = b.astype(b_ref.dtype)


def _mix_ffn_call(x, y, mods, wm, bm, mix_post_g, ffn_pre_g, ffn_post_g, w13, w2, layer, tm, rows_per_batch,
                  glu=False, emit=None):
    m, d = x.shape
    k, nm = wm.shape
    f = w2.shape[1]
    cuts = tuple(range(0, f, 6 * MXU_COLS)) + (f,)
    x_spec, mod_spec = _row_specs(tm, d, rows_per_batch // tm)
    once = lambda shape: pl.BlockSpec(shape, lambda i: (0, 0), pipeline_mode=pl.Buffered(1))
    of_layer = lambda shape: pl.BlockSpec((None,) + shape, lambda i: (layer, 0, 0), pipeline_mode=pl.Buffered(1))
    in_specs = [x_spec, pl.BlockSpec((tm, k), lambda i: (i, 0)), mod_spec,
                once((k, nm)), _const_spec((1, nm)), _const_spec((1, d)), _const_spec((1, d)),
                _const_spec((1, d)), of_layer((d, 2 * f)), of_layer((f, d))]
    args = [x, y, mods, wm, bm.reshape(1, nm), mix_post_g.reshape(1, d), ffn_pre_g.reshape(1, d),
            ffn_post_g.reshape(1, d), w13, w2]
    out_specs, out_shape = [x_spec], [jax.ShapeDtypeStruct((m, d), F32)]
    kind = None
    if emit is not None:
        kind, next_mods, next_g, cs = emit
        in_specs += [mod_spec, _const_spec((1, d))]
        args += [next_mods, next_g.reshape(1, d)]
        n_extra = 1
        if kind == "fnet":
            in_specs.append(_const_spec(cs.shape))
            args.append(cs)
            n_extra = 2
        out_specs += [x_spec] * n_extra
        out_shape += [jax.ShapeDtypeStruct((m, d), BF16)] * n_extra
    outs = pl.pallas_call(
        functools.partial(_mix_ffn_kernel, f=f, cuts=cuts, glu=glu, emit=kind), grid=(m // tm,),
        in_specs=in_specs, out_specs=out_specs, out_shape=out_shape,
        compiler_params=_cp("parallel"), name="mix_ffn",
    )(*args)
    return outs[0] if emit is None else (outs[0], tuple(outs[1:]))


def _mla_proj_kernel(x_ref, mod_ref, g_ref, w_ref, qg_ref, kvg_ref, wq_ref, wkv_ref, cp_ref, sp_ref,
                     q_ref, k_ref, vt_ref, *, ql, kvl, qscale, positional):
    h = _normmod(x_ref[...], g_ref[...], mod_ref[0, 0:1, :], mod_ref[0, 1:2, :])
    z = _dot(h.astype(BF16), w_ref[...])
    qn = _rms(z[:, :ql], qg_ref[...]).astype(BF16)
    cn = _rms(z[:, ql:ql + kvl], kvg_ref[...]).astype(BF16)
    tk = z.shape[0]
    low = lax.broadcasted_iota(jnp.int32, (tk, 2 * MLA_ROPE), 1) < MLA_ROPE
    if positional:
        cpf = jnp.concatenate([cp_ref[...]] * 2, axis=1)
        spf = jnp.concatenate([sp_ref[...]] * 2, axis=1)

    def rope_slots(pair, keep_raw):
        swapped = pltpu.roll(pair, 64, 1)
        raw = jnp.where(low, 0.0, swapped) if keep_raw else None
        if not positional:
            return raw
        rot = jnp.where(low, pair * cpf + swapped * spf, 0.0)
        return rot + raw if keep_raw else rot

    kr = rope_slots(z[:, ql + kvl:], keep_raw=not positional).astype(BF16)

    zkv = _dot(cn, wkv_ref[...])
    ones_blk = (lax.broadcasted_iota(jnp.int32, (MLA_VT - MLA_V, tk), 0) == 0).astype(BF16)
    for hd in range(MLA_HEADS):
        base = hd * 256
        k_ref[0, hd, 0, :, 0:128] = zkv[:, base:base + 128].astype(BF16)
        k_ref[0, hd, 0, :, 128:256] = kr
        vt_ref[0, hd, 0, 0:MLA_V, :] = zkv[:, base + 128:base + 256].T.astype(BF16)
        vt_ref[0, hd, 0, MLA_V:MLA_VT, :] = ones_blk

    zq = _dot(qn, wq_ref[...])
    for hd in range(MLA_HEADS):
        base = hd * 256
        rp = rope_slots(zq[:, base + 128:base + 256], keep_raw=True)
        qcat = jnp.concatenate([zq[:, base:base + 128], rp], axis=1) * qscale
        q_ref[0, hd] = qcat.T.astype(BF16)


def _flash_kernel(q_ref, kc_ref, vc_ref, *rest, n_lat):
    if n_lat:
        kl_ref, vl_ref, o_ref, s_scr, acc_scr = rest
    else:
        o_ref, acc_scr = rest
    qt = q_ref[0, 0]

    def qk(k, slot):
        s = _dot(k, qt)
        s_scr[slot] = s
        return jnp.max(s, axis=0, keepdims=True)

    def sm_pv(slot, vt, m, mx):
        m_new = jnp.maximum(m, mx)
        alpha = jnp.exp2(m - m_new)
        p = jnp.exp2(s_scr[slot] - m_new).astype(BF16)
        acc_scr[...] = alpha * acc_scr[...] + _dot(vt, p)
        return m_new

    sc = _dot(kc_ref[0, 0, 0], qt)
    if n_lat:
        mx = qk(kl_ref[0, 0, 0], 0)
    m = jnp.max(sc, axis=0, keepdims=True)
    acc_scr[...] = _dot(vc_ref[0, 0, 0], jnp.exp2(sc - m).astype(BF16))
    if n_lat:

        per = 8 if n_lat % 8 == 0 else (4 if n_lat % 4 == 0 else 2)

        def body(i, carry):
            m, mx_cur = carry
            c = per * i
            for u in range(per):
                mx_next = qk(kl_ref[0, 0, jnp.minimum(c + u + 1, n_lat - 1)], (u + 1) % 2)
                m = sm_pv(u % 2, vl_ref[0, 0, c + u], m, mx_cur)
                mx_cur = mx_next
            return m, mx_cur

        lax.fori_loop(0, n_lat // per, body, (m, mx))
    acc = acc_scr[...]
    o_ref[0] = (acc[0:MLA_V] / acc[MLA_V:MLA_V + 1]).T.astype(o_ref.dtype)


def _rope_tables(n_lat):
    rows = n_lat // GRID_W
    row = jnp.repeat(jnp.arange(rows, dtype=F32), GRID_W)
    col = jnp.tile(jnp.arange(GRID_W, dtype=F32), rows)
    axis_dim = MLA_ROPE // 2
    inv_freq = 1.0 / (ROPE_THETA ** (jnp.arange(0, axis_dim, 2, dtype=F32) / axis_dim))
    ang_r = row[:, None] * inv_freq
    ang_c = col[:, None] * inv_freq
    cr, sr, cc, sc = jnp.cos(ang_r), jnp.sin(ang_r), jnp.cos(ang_c), jnp.sin(ang_c)
    cp = jnp.concatenate([cr, cr, cc, cc], axis=-1)
    sp = jnp.concatenate([-sr, sr, -sc, sc], axis=-1)
    return cp, sp


_ROPE_SWAP = np.concatenate([np.arange(16, 32), np.arange(0, 16), np.arange(48, 64), np.arange(32, 48)])


def _mla_side(x, mods, pre_g, w_in_ext, q_g, kv_g, w_uq_ext, w_ukv, tabs, b, n, tk, positional):
    m, d = x.shape
    ql, kvl = q_g.shape[-1], kv_g.shape[-1]
    nc = n // tk
    x_spec, mod_spec = _row_specs(tk, d, nc)
    tab_spec = pl.BlockSpec((tk, MLA_ROPE), lambda i: (i % nc, 0))
    qscale = (MLA_NOPE + MLA_ROPE) ** -0.5 * math.log2(math.e)
    return pl.pallas_call(
        functools.partial(_mla_proj_kernel, ql=ql, kvl=kvl, qscale=qscale, positional=positional),
        grid=(m // tk,),
        in_specs=[x_spec, mod_spec, _const_spec((1, d)), _const_spec(w_in_ext.shape),
                  _const_spec((1, ql)), _const_spec((1, kvl)), _const_spec(w_uq_ext.shape),
                  _const_spec(w_ukv.shape), tab_spec, tab_spec],
        out_specs=[pl.BlockSpec((1, MLA_HEADS, 256, tk), lambda i: (i // nc, 0, 0, i % nc)),
                   pl.BlockSpec((1, MLA_HEADS, 1, tk, 256), lambda i: (i // nc, 0, i % nc, 0, 0)),
                   pl.BlockSpec((1, MLA_HEADS, 1, MLA_VT, tk), lambda i: (i // nc, 0, i % nc, 0, 0))],
        out_shape=[jax.ShapeDtypeStruct((b, MLA_HEADS, 256, n), BF16),
                   jax.ShapeDtypeStruct((b, MLA_HEADS, nc, tk, 256), BF16),
                   jax.ShapeDtypeStruct((b, MLA_HEADS, nc, MLA_VT, tk), BF16)],
        compiler_params=_cp("parallel"), name="mla_proj",
    )(x, mods, pre_g.reshape(1, d), w_in_ext, q_g.reshape(1, ql), kv_g.reshape(1, kvl), w_uq_ext, w_ukv, *tabs)


def _flash_call(qt, kc, vtc, kl, vtl, tq):
    b, hh, _, n = qt.shape
    c = kc.shape[-2]
    n_lat = 0 if kl is None else kl.shape[2]
    in_specs = [pl.BlockSpec((1, 1, 256, tq), lambda bi, h, i: (bi, h, 0, i)),
                pl.BlockSpec((1, 1, 1, c, 256), lambda bi, h, i: (bi, h, 0, 0, 0)),
                pl.BlockSpec((1, 1, 1, MLA_VT, c), lambda bi, h, i: (bi, h, 0, 0, 0))]
    args = [qt, kc, vtc]
    scratch = [pltpu.VMEM((MLA_VT, tq), F32)]
    if n_lat:
        assert n_lat % 2 == 0, "latent key chunks are consumed in pairs"
        tk = kl.shape[-2]
        in_specs += [pl.BlockSpec((1, 1, n_lat, tk, 256), lambda bi, h, i: (bi, h, 0, 0, 0)),
                     pl.BlockSpec((1, 1, n_lat, MLA_VT, tk), lambda bi, h, i: (bi, h, 0, 0, 0))]
        args += [kl, vtl]
        scratch = [pltpu.VMEM((2, tk, tq), F32)] + scratch
    return pl.pallas_call(
        functools.partial(_flash_kernel, n_lat=n_lat), grid=(b, hh, n // tq),
        in_specs=in_specs,
        out_specs=pl.BlockSpec((1, tq, MLA_V), lambda bi, h, i: (bi, i, h)),
        out_shape=jax.ShapeDtypeStruct((b, n, hh * MLA_V), BF16),
        scratch_shapes=scratch,
        compiler_params=_cp("parallel", "parallel", "arbitrary"), name="flash",
    )(*args)


def _mla_layer(xl, xc, mods_l, mods_c, pre_g, fin, w_in, q_g, kv_g, w_uq, w_ukv, w_o, b, n, c):
    d = xl.shape[-1]
    ql, kvl = q_g.shape[-1], kv_g.shape[-1]
    hh = MLA_HEADS
    rope_cols = w_in[:, ql + kvl:]
    w_in_ext = jnp.concatenate([w_in, rope_cols[:, _ROPE_SWAP]], axis=1).astype(BF16)
    wq = w_uq.reshape(ql, hh, MLA_NOPE + MLA_ROPE)
    w_uq_ext = jnp.concatenate([wq, wq[:, :, MLA_NOPE:][:, :, _ROPE_SWAP]], axis=-1)
    w_uq_ext = w_uq_ext.reshape(ql, hh * 256).astype(BF16)
    w_ukv_b = w_ukv.astype(BF16)
    w_o_b = w_o.astype(BF16)

    tabs = _rope_tables(n)

    tm_l = min(512, n)
    tk_l = min(512, n // 2)
    ql_, kl, vtl = _mla_side(xl, mods_l, pre_g, w_in_ext, q_g, kv_g, w_uq_ext, w_ukv_b, tabs, b, n, tk_l, True)
    qc_, kc, vtc = _mla_side(xc, mods_c, pre_g, w_in_ext, q_g, kv_g, w_uq_ext, w_ukv_b, tabs, b, c, c, False)
    o_lat = _flash_call(ql_, kc, vtc, kl, vtl, min(1024, n)).reshape(b * n, hh * MLA_V)
    o_ctx = _flash_call(qc_, kc, vtc, None, None, c).reshape(b * c, hh * MLA_V)
    zb = jnp.zeros((d,), F32)
    xl = fin(xl, o_lat, mods_l, w_o_b, zb, tm_l, n)
    xc = fin(xc, o_ctx, mods_c, w_o_b, zb, c, c)
    return xl, xc


def _hy_in_kernel(x_ref, xp_ref, xn_ref, mod_ref, g_ref, w_ref, b_ref, cw_ref, cb_ref,
                  g0_ref, vg_ref, *, tpb):
    i = pl.program_id(0)
    g = g_ref[...]
    shift, scale = mod_ref[0, 0:1, :], mod_ref[0, 1:2, :]
    xcat = jnp.concatenate([xp_ref[...], x_ref[...], xn_ref[...]], axis=0)
    hcat = _normmod(xcat, g, shift, scale).astype(BF16)
    tm = x_ref.shape[0]
    d = g0_ref.shape[-1]
    first = (i % tpb) == 0
    last = (i % tpb) == tpb - 1
    ridx = lax.broadcasted_iota(jnp.int32, (tm, 1), 0)

    def conv_part(c):
        cols = slice(c * d, (c + 1) * d)
        ucat = _dot(hcat, w_ref[:, cols]) + b_ref[:, cols]
        u = ucat[8:tm + 8]
        prev_row = jnp.where(first, 0.0, ucat[7:8, :])
        next_row = jnp.where(last, 0.0, ucat[tm + 8:tm + 9, :])
        dn = jnp.where(ridx == 0, prev_row, pltpu.roll(u, 1, 0))
        upw = jnp.where(ridx == tm - 1, next_row, pltpu.roll(u, tm - 1, 0))
        return cb_ref[:, cols] + dn * cw_ref[0:1, cols] + u * cw_ref[1:2, cols] + upw * cw_ref[2:3, cols]

    g0_ref[...] = conv_part(0)
    vg_ref[...] = conv_part(2) * conv_part(1)


def _hy_in_call(x, mods, pre_g, w_in, b_in, conv_w, conv_b, tm, n):
    m, d = x.shape
    p = w_in.shape[1]
    tpb = n // tm
    x_spec, mod_spec = _row_specs(tm, d, tpb)
    r8 = tm // 8
    nb8 = m // 8
    prev_spec = pl.BlockSpec((8, d), lambda i: (jnp.maximum(i * r8 - 1, 0), 0))
    next_spec = pl.BlockSpec((8, d), lambda i: (jnp.minimum((i + 1) * r8, nb8 - 1), 0))
    cw = jnp.pad(conv_w, ((0, 8 - conv_w.shape[0]), (0, 0)))
    return pl.pallas_call(
        functools.partial(_hy_in_kernel, tpb=tpb), grid=(m // tm,),
        in_specs=[x_spec, prev_spec, next_spec, mod_spec, _const_spec((1, d)), _const_spec((d, p)),
                  _const_spec((1, p)), _const_spec((8, p)), _const_spec((1, p))],
        out_specs=[x_spec, x_spec],
        out_shape=[jax.ShapeDtypeStruct((m, d), F32), jax.ShapeDtypeStruct((m, d), F32)],
        compiler_params=_cp("parallel"), name="hy_in",
    )(x, x, x, mods, pre_g.reshape(1, d), w_in, b_in.reshape(1, p), cw, conv_b.reshape(1, p))


_PI_SPLIT = (3.140625, 9.67502593994140625e-4, 1.509957990978376432e-7)
_SIN_TAYLOR = (-1.0 / 6, 1.0 / 120, -1.0 / 5040, 1.0 / 362880, -1.0 / 39916800)


def _sin(x):
    kf = jnp.round(x * (1.0 / math.pi))
    r = ((x - kf * _PI_SPLIT[0]) - kf * _PI_SPLIT[1]) - kf * _PI_SPLIT[2]
    r2 = r * r
    p = _SIN_TAYLOR[4]
    for c in _SIN_TAYLOR[3::-1]:
        p = p * r2 + c
    s = r + r * r2 * p
    odd = (kf.astype(jnp.int32) & 1) == 1
    return jnp.where(odd, -s, s)


def _hy_filter_kernel(bands_ref, w1_ref, b1_ref, fq_ref, w2_ref, b2_ref, w3_ref, dl_ref,
                      k_ref, nrm_ref, *, n, tr):
    i = pl.program_id(0)
    bwd = i >= n // tr
    row = lax.broadcasted_iota(jnp.int32, (tr, LANE), 0) + i * tr
    j = jnp.where(bwd, 2 * n - row, row).astype(F32)
    lane = lax.broadcasted_iota(jnp.int32, (tr, LANE), 1)
    t = j * (1.0 / (n - 1))
    arg = (2.0 * math.pi / n) * j * bands_ref[0:1, :] + bands_ref[1:2, :]
    z = jnp.where(lane == 0, t, jnp.where(lane <= 2 * HYENA_BANDS, _sin(arg), 0.0))
    fq = fq_ref[...]
    a = _sin(fq * (_dot_hi(z, w1_ref[...]) + b1_ref[...]))
    for k in range(w2_ref.shape[0]):
        a = _sin(fq * (_dot_hi(a, w2_ref[k]) + b2_ref[k]))
    h = _dot_x3(a, w3_ref[jnp.where(bwd, 1, 0)])
    decay = jnp.exp(-t[:, 0:1] * dl_ref[...])
    k = h * decay
    k = jnp.where(row[:, 0:1] == n, 0.0, k)
    k_ref[...] = k
    part = jnp.sum(jnp.abs(k), axis=0, keepdims=True)

    @pl.when(i == 0)
    def _():
        nrm_ref[...] = jnp.zeros_like(nrm_ref)

    nrm_ref[...] += jnp.broadcast_to(part, nrm_ref.shape)


def _hy_filter_call(n, d, f_w1, f_b1, f_freq, f_w2, f_b2, f_w3):
    fw = f_w1.shape[1]
    tr = min(512, n)
    bands_np = np.zeros((8, LANE), np.float32)
    bands_np[0, 1:1 + HYENA_BANDS] = np.linspace(1e-4, HYENA_BANDS - 1, HYENA_BANDS, dtype=np.float32)
    bands_np[0, 1 + HYENA_BANDS:1 + 2 * HYENA_BANDS] = bands_np[0, 1:1 + HYENA_BANDS]
    bands_np[1, 1:1 + HYENA_BANDS] = 0.5 * np.pi
    bands_np[1, 1 + HYENA_BANDS:1 + 2 * HYENA_BANDS] = np.pi
    w1p = jnp.zeros((LANE, fw), F32).at[:f_w1.shape[0]].set(f_w1)
    deltas = jnp.abs(jnp.linspace(math.log(HYENA_TARGET) / HYENA_SLOW, math.log(HYENA_TARGET) / HYENA_FAST,
                                  d, dtype=F32)).reshape(1, d)
    row = pl.BlockSpec((tr, d), lambda i: (i, 0))
    return pl.pallas_call(
        functools.partial(_hy_filter_kernel, n=n, tr=tr), grid=(2 * n // tr,),
        in_specs=[_const_spec((8, LANE)), _const_spec((LANE, fw)), _const_spec((1, fw)), _const_spec((1, fw)),
                  _const_spec(f_w2.shape), _const_spec((f_w2.shape[0], 1, fw)), _const_spec((2, fw, d)),
                  _const_spec((1, d))],
        out_specs=[row, _const_spec((8, d))],
        out_shape=[jax.ShapeDtypeStruct((2 * n, d), F32), jax.ShapeDtypeStruct((8, d), F32)],
        compiler_params=_cp("arbitrary"), name="hy_filter",
    )(jnp.asarray(bands_np), w1p, f_b1.reshape(1, fw), f_freq.reshape(1, fw), f_w2,
      f_b2.reshape(f_w2.shape[0], 1, fw), jnp.transpose(f_w3.reshape(fw, 2, d), (1, 0, 2)), deltas)


def _dft_cs(nf, nt, period):
    ft = (np.arange(nf)[:, None] * np.arange(nt)[None, :]) % period
    ang = 2.0 * np.pi * ft / period
    return np.cos(ang), np.sin(ang)


def _twiddle_tables(n1, n2):
    nn = n1 * n2
    f1 = jnp.arange(n1, dtype=jnp.int32)
    t2 = jnp.arange(n2, dtype=jnp.int32)
    idx = (f1[:, None] * t2[None, :]) % nn
    ang = idx.astype(F32) * (2.0 * math.pi / nn)
    shape = idx.shape + (LANE,)
    return (jnp.broadcast_to(jnp.cos(ang)[..., None], shape),
            jnp.broadcast_to(jnp.sin(ang)[..., None], shape))


KRON_R = 16
KRON_W = 256


def _cblock(mc):
    return np.block([[mc.real, -mc.imag], [mc.imag, mc.real]])


def _split_radix(n1):
    b = 16 if (n1 % 16 == 0 and n1 >= 64) else 4
    assert n1 % b == 0 and (n1 // b) % 2 == 0
    return n1 // b, b


def _slab_dft_mats(n1, a_in, a_out, sign, neg_im=False, real_in=False, scale=1.0):
    a, b = _split_radix(n1)
    r = KRON_R
    eye = np.eye(r)
    w = lambda num, den: np.exp(sign * 2j * np.pi * num / den)
    ua = np.arange(a)[:, None] * np.arange(a)[None, :]
    vb = np.arange(b)[:, None] * np.arange(b)[None, :]
    if sign < 0:
        m1 = np.kron(w(ua[:, :a_in], a) * scale, eye)
        l1 = _cblock(m1)
        if neg_im:
            l1[:, a_in * r:] *= -1.0
        if real_in:
            l1 = l1[:, :a_in * r]
        l2 = np.stack([_cblock(np.kron(w(vb, b) * w(u * np.arange(b)[None, :], n1), eye)) for u in range(a)])
    else:
        l1 = np.stack([_cblock(np.kron(w(vb, b) * w(u * np.arange(b)[:, None], n1), eye)) for u in range(a)])
        l2 = _cblock(np.kron(w(ua[:a_out, :], a) * scale, eye))
    return jnp.asarray(l1, BF16), jnp.asarray(l2, BF16)


def _kfa_kernel(*refs, nparts, a_in, a, b):
    l1_ref, l2_ref = refs[0], refs[1]
    parts = refs[2:2 + nparts]
    twc_ref, tws_ref, or_ref, oi_ref = refs[2 + nparts:]
    r = KRON_R
    reps = or_ref.shape[-1] // LANE
    l1 = l1_ref[...]
    y1 = []
    for bb in range(b):
        x = jnp.concatenate([p[0, aa * b + bb] for p in parts for aa in range(a_in)], axis=0).astype(BF16)
        y1.append(_dot(l1, x).astype(BF16))
    for u in range(a):
        x = jnp.concatenate([y1[bb][u * r:(u + 1) * r] for bb in range(b)]
                            + [y1[bb][(a + u) * r:(a + u + 1) * r] for bb in range(b)], axis=0)
        z = _dot(l2_ref[u], x)
        for v in range(b):
            f1 = u + a * v
            zr = z[v * r:(v + 1) * r]
            zi = z[(b + v) * r:(b + v + 1) * r]
            c = jnp.concatenate([twc_ref[f1]] * reps, axis=1)
            sn = jnp.concatenate([tws_ref[f1]] * reps, axis=1)
            or_ref[0, f1] = (zr * c + zi * sn).astype(or_ref.dtype)
            oi_ref[0, f1] = (zi * c - zr * sn).astype(oi_ref.dtype)


def _kfa_call(parts, nb_out, n1, n2, d, l1, l2, twc, tws):
    a, b = _split_radix(n1)
    a_in = parts[0][0].shape[1] // b
    r, w = KRON_R, min(KRON_W, d)
    in_specs = [_const_spec(l1.shape), pl.BlockSpec(l2.shape, lambda bi, j, k: (0, 0, 0), pipeline_mode=pl.Buffered(1))]
    args = [l1, l2]
    for arr, bi_fixed in parts:
        t1_in = arr.shape[1]
        if bi_fixed is None:
            in_specs.append(pl.BlockSpec((1, t1_in, r, w), lambda bi, j, k: (bi, 0, j, k)))
        else:
            in_specs.append(pl.BlockSpec((1, t1_in, r, w), lambda bi, j, k, f=bi_fixed: (f, 0, j, k)))
        args.append(arr)
    tw_spec = pl.BlockSpec((n1, r, LANE), lambda bi, j, k: (0, j, 0))
    out_spec = pl.BlockSpec((1, n1, r, w), lambda bi, j, k: (bi, 0, j, k))
    out = jax.ShapeDtypeStruct((nb_out, n1, n2, d), BF16)
    return pl.pallas_call(
        functools.partial(_kfa_kernel, nparts=len(parts), a_in=a_in, a=a, b=b),
        grid=(nb_out, n2 // r, d // w),
        in_specs=in_specs + [tw_spec, tw_spec], out_specs=[out_spec, out_spec], out_shape=[out, out],
        compiler_params=_cp("parallel", "parallel", "parallel"), name="kfa",
    )(*args, twc, tws)


def _kfc_kernel(l3_ref, l4_ref, gr_ref, gi_ref, vg_ref, g0_ref, skip_ref, o_ref, *, a, b, a_out):
    r = KRON_R
    y3 = []
    for u in range(a):
        x = jnp.concatenate([gr_ref[0, u + a * v] for v in range(b)]
                            + [gi_ref[0, u + a * v] for v in range(b)], axis=0)
        y3.append(_dot(l3_ref[u], x).astype(BF16))
    l4 = l4_ref[...]
    skip = skip_ref[...]
    for bb in range(b):
        x = jnp.concatenate([y3[u][bb * r:(bb + 1) * r] for u in range(a)]
                            + [y3[u][(b + bb) * r:(b + bb + 1) * r] for u in range(a)], axis=0)
        y = _dot(l4, x)
        for sg in range(2):
            for aa in range(a_out):
                t1 = aa * b + bb
                yb = y[(sg * a_out + aa) * r:(sg * a_out + aa + 1) * r]
                o_ref[sg, t1] = ((yb + vg_ref[sg, t1] * skip) * g0_ref[sg, t1]).astype(o_ref.dtype)


def _kfc_call(g1r, g1i, vg4, g04, skip, n1, n2, d, l3, l4):
    a, b = _split_radix(n1)
    nb, t1_out = vg4.shape[:2]
    r, w = KRON_R, min(KRON_W, d)
    slab = pl.BlockSpec((1, n1, r, w), lambda j, k: (0, 0, j, k))
    nat = pl.BlockSpec((nb, t1_out, r, w), lambda j, k: (0, 0, j, k))
    return pl.pallas_call(
        functools.partial(_kfc_kernel, a=a, b=b, a_out=t1_out // b), grid=(n2 // r, d // w),
        in_specs=[pl.BlockSpec(l3.shape, lambda j, k: (0, 0, 0), pipeline_mode=pl.Buffered(1)),
                  _const_spec(l4.shape), slab, slab, nat, nat, pl.BlockSpec((1, w), lambda j, k: (0, k))],
        out_specs=nat, out_shape=jax.ShapeDtypeStruct(vg4.shape, BF16),
        compiler_params=_cp("parallel", "parallel"), name="kfc",
    )(l3, l4, g1r, g1i, vg4, g04, skip.reshape(1, d))


def _hy_b_kernel(fm_ref, fmc_ref, xr_ref, xi_ref, fr_ref, fi_ref, sc_ref, twc_ref, tws_ref,
                 or_ref, oi_ref, *, nf, d):
    fm = fm_ref[...]
    fmc = fmc_ref[...]
    n2 = xr_ref.shape[1]
    sc = sc_ref[...]
    for s in range(nf):
        x = _dot(fm, jnp.concatenate([xr_ref[s], xi_ref[s]], axis=0))
        kk = _dot(fm, jnp.concatenate([fr_ref[s], fi_ref[s]], axis=0))
        xr, xi = x[:n2], x[n2:]
        kr, ki = kk[:n2] * sc, kk[n2:] * sc
        yr = (xr * kr - xi * ki).astype(BF16)
        yi = (xr * ki + xi * kr).astype(BF16)
        g = _dot(fmc, jnp.concatenate([yr, yi], axis=0))
        gr, gi = g[:n2], g[n2:]
        c = jnp.concatenate([twc_ref[s]] * (d // LANE), axis=1)
        sn = jnp.concatenate([tws_ref[s]] * (d // LANE), axis=1)
        or_ref[s] = (gr * c - gi * sn).astype(or_ref.dtype)
        oi_ref[s] = (gi * c + gr * sn).astype(oi_ref.dtype)


def _block_c(cs, sn, sign):
    return np.block([[cs, -sign * sn], [sign * sn, cs]])


def _hy_conv_long(vg, g0, k, nrm, skip, b, n, d):
    n2 = FFT_N2
    nn = 2 * n
    n1 = nn // n2
    rows = n // n2
    a, _ = _split_radix(n1)
    la1, la2 = _slab_dft_mats(n1, a // 2, 0, -1)
    lk1, lk2 = _slab_dft_mats(n1, a, 0, -1, real_in=True)
    lc3, lc4 = _slab_dft_mats(n1, 0, a // 2, +1)
    cs2, sn2 = _dft_cs(n2, n2, n2)
    fm_b = jnp.asarray(_block_c(cs2, sn2, -1.0), BF16)
    fm_bc = jnp.asarray(_block_c(cs2, sn2, 1.0), BF16)
    twc, tws = _twiddle_tables(n1, n2)

    kfr, kfi = _kfa_call([(k.reshape(1, n1, n2, d), 0)], 1, n1, n2, d, lk1, lk2, twc, tws)
    nf = 4 if n1 % 4 == 0 else 1
    scale = (1.0 / (nrm[0:1, :] * nn))
    slab = pl.BlockSpec((nf, n2, d), lambda j: (j, 0, 0))
    shp3 = (n1, n2, d)
    vg4 = vg.reshape(b, rows, n2, d)
    g04 = g0.reshape(b, rows, n2, d)
    x1r, x1i = _kfa_call([(vg4, 0), (vg4, 1)], 1, n1, n2, d, la1, la2, twc, tws)
    tw_slab = pl.BlockSpec((nf, n2, LANE), lambda j: (j, 0, 0))
    g1r, g1i = pl.pallas_call(
        functools.partial(_hy_b_kernel, nf=nf, d=d), grid=(n1 // nf,),
        in_specs=[_const_spec(fm_b.shape), _const_spec(fm_bc.shape), slab, slab, slab, slab, _const_spec((1, d)),
                  tw_slab, tw_slab],
        out_specs=[slab, slab],
        out_shape=[jax.ShapeDtypeStruct(shp3, BF16)] * 2,
        compiler_params=_cp("parallel"), name="hy_b",
    )(fm_b, fm_bc, x1r.reshape(shp3), x1i.reshape(shp3), kfr.reshape(shp3), kfi.reshape(shp3), scale, twc, tws)
    out = _kfc_call(g1r.reshape(1, n1, n2, d), g1i.reshape(1, n1, n2, d), vg4, g04, skip, n1, n2, d, lc3, lc4)
    return out.reshape(b * n, d)


def _hy_short_kernel(fa_ref, fk_ref, fi_ref, vg_ref, g0_ref, k_ref, nrm_ref, skip_ref, o_ref, *, n):
    z = jnp.concatenate([vg_ref[0], vg_ref[1]], axis=0)
    x = _dot_hi(fa_ref[...], z)
    kk = _dot_hi(fk_ref[...], k_ref[...])
    nn = 2 * n
    sc = 1.0 / (nrm_ref[0:1, :] * nn)
    xr, xi = x[:nn], x[nn:]
    kr, ki = kk[:nn] * sc, kk[nn:] * sc
    y = _dot_hi(fi_ref[...], jnp.concatenate([xr * kr - xi * ki, xr * ki + xi * kr], axis=0))
    skip = skip_ref[...]
    for bi in range(2):
        o_ref[bi] = ((y[bi * n:(bi + 1) * n] + vg_ref[bi] * skip) * g0_ref[bi]).astype(o_ref.dtype)


def _hy_conv_short(vg, g0, k, nrm, skip, b, n, d):
    nn = 2 * n
    cs, sn = _dft_cs(nn, n, nn)
    fa = jnp.asarray(_block_c(cs, sn, -1.0), F32)
    csk, snk = _dft_cs(nn, nn, nn)
    fk = jnp.asarray(np.concatenate([csk, -snk], axis=0), F32)
    csi, sni = _dft_cs(n, nn, nn)
    fi = jnp.asarray(_block_c(csi, sni, 1.0), F32)
    cb = 256
    col3 = pl.BlockSpec((b, n, cb), lambda j: (0, 0, j))
    vec = pl.BlockSpec((1, cb), lambda j: (0, j))
    out = pl.pallas_call(
        functools.partial(_hy_short_kernel, n=n), grid=(d // cb,),
        in_specs=[_const_spec(fa.shape), _const_spec(fk.shape), _const_spec(fi.shape), col3, col3,
                  pl.BlockSpec((nn, cb), lambda j: (0, j)), pl.BlockSpec((8, cb), lambda j: (0, j)), vec],
        out_specs=col3, out_shape=jax.ShapeDtypeStruct((b, n, d), BF16),
        compiler_params=_cp("parallel"), name="hy_short",
    )(fa, fk, fi, vg.reshape(b, n, d), g0.reshape(b, n, d), k, nrm, skip.reshape(1, d))
    return out.reshape(b * n, d)


def _hyena_layer(xl, xc, mods_l, mods_c, pre_g, fin, w_in, b_in, conv_w, conv_b, filt, skip,
                 w_out, b_out, b, n, c):
    d = xl.shape[-1]
    w_in_b = w_in.astype(BF16)
    w_out_b = w_out.astype(BF16)
    tm = min(512, n)
    g0, vg = _hy_in_call(xl, mods_l, pre_g, w_in_b, b_in, conv_w, conv_b, tm, n)
    k, nrm = _hy_filter_call(n, d, *filt)
    u_out = _hy_conv_long(vg, g0, k, nrm, skip[0], b, n, d)
    xl, (hl,) = fin(xl, u_out, mods_l, w_out_b, b_out, tm, n)

    g0c, vgc = _hy_in_call(xc, mods_c, pre_g, w_in_b, b_in, conv_w, conv_b, c, c)
    kc, nrmc = _hy_filter_call(c, d, *filt)
    u_out_c = _hy_conv_short(vgc, g0c, kc, nrmc, skip[0], b, c, d)
    xc, (hc,) = fin(xc, u_out_c, mods_c, w_out_b, b_out, c, c, is_ctx=True)
    return xl, hl, hc


def _s5_operators(lam_re, lam_im, log_dt, b_re, b_im, c_re, c_im, d_skip):
    t = S5_T
    g, ns = lam_re.shape[1], lam_re.shape[2]
    gc = b_re.shape[-1]
    gl = LANE // gc
    nblk = g // gl
    lam = lax.complex(lam_re, lam_im)
    dt = jnp.exp(log_dt)[..., None]
    lam_bar = jnp.exp(lam * dt)
    b_bar = ((lam_bar - 1.0) / lam)[..., None] * lax.complex(b_re, b_im)
    c_mat = lax.complex(c_re, c_im)
    pw = jnp.arange(t + 1, dtype=F32)
    lam_pw = jnp.exp((lam * dt)[None] * pw[:, None, None, None])
    hp = HIGHEST
    kern = jnp.einsum('dgcn,tdgn,dgne->dgtce', c_mat, lam_pw[:t], b_bar, precision=hp).real
    dsk = d_skip.reshape(g, gc)
    kt = jnp.swapaxes(kern, -1, -2)
    centre = kt[0][:, 0] + kt[1][:, 0] + jnp.eye(gc, dtype=F32)[None] * dsk[:, :, None]
    ks = jnp.concatenate([kt[1][:, 1:][:, ::-1], centre[:, None], kt[0][:, 1:]], axis=1)
    ks = jnp.transpose(ks.reshape(nblk, gl, 2 * t - 1, gc, gc), (0, 2, 1, 3, 4)).reshape(nblk, 2 * t - 1, LANE, gc)
    same_group = (np.arange(LANE)[:, None] // gc == np.arange(LANE)[None, :] // gc).astype(np.float32)
    d_tab = jnp.tile(ks, (1, 1, 1, gl)) * same_group

    def compact(zc, im_sign):
        z = jnp.concatenate([zc.real, im_sign * zc.imag], axis=-1)
        z = jnp.transpose(z, (2, 1, 0, 3, 4)).reshape(nblk, gl, t, 2, gc, 2 * ns)
        return jnp.transpose(z, (0, 2, 1, 3, 4, 5))

    b_t = jnp.swapaxes(b_bar, -1, -2)
    pf = lam_pw[:t][::-1][:, 0, :, None, :] * b_t[0][None]
    pb = lam_pw[:t][:, 1, :, None, :] * b_t[1][None]
    p_tab = compact(jnp.stack([pf, pb], axis=0), 1.0)
    qf = c_mat[0][None] * lam_pw[1:t + 1, 0][:, :, None, :]
    qb = c_mat[1][None] * lam_pw[1:t + 1][::-1][:, 1][:, :, None, :]
    q_tab = compact(jnp.stack([qf, qb], axis=0), -1.0)

    a = lam_pw[t]
    m_op, p_op, q_op = _s5_expand(d_tab, p_tab, q_tab)
    return m_op, p_op, q_op, a.real.reshape(2, g * ns), a.imag.reshape(2, g * ns)


def _s5_m_kernel(d_ref, o_ref, *, t):
    for s in range(t):
        for tt in range(t):
            o_ref[0, s * LANE:(s + 1) * LANE, tt * LANE:(tt + 1) * LANE] = d_ref[0, tt - s + t - 1].astype(o_ref.dtype)


def _s5_pq_kernel(c_ref, o_ref, *, transpose):
    t, gl, nd, gc, w = c_ref.shape[1:]
    ns = w // 2
    lane_grp = lax.broadcasted_iota(jnp.int32, (gc, gl * ns), 1) // ns
    for j in range(t):
        rows = []
        for g in range(gl):
            cols = []
            for dd in range(nd):
                piece = c_ref[0, j, g, dd]
                for ri in range(2):
                    tiled = jnp.concatenate([piece[:, ri * ns:(ri + 1) * ns]] * gl, axis=1)
                    cols.append(jnp.where(lane_grp == g, tiled, 0.0))
            rows.append(jnp.concatenate(cols, axis=1))
        blk = jnp.concatenate(rows, axis=0)
        if transpose:
            o_ref[0, :, j * LANE:(j + 1) * LANE] = blk.T.astype(o_ref.dtype)
        else:
            o_ref[0, j * LANE:(j + 1) * LANE, :] = blk.astype(o_ref.dtype)


def _s5_expand(d_tab, p_tab, q_tab):
    nblk, nlag = d_tab.shape[:2]
    t = (nlag + 1) // 2
    _, _, gl, nd, gc, w = p_tab.shape
    ncol = nd * gl * w
    whole = lambda shape: pl.BlockSpec((1,) + shape, lambda b: (b,) + (0,) * len(shape))
    m_op = pl.pallas_call(
        functools.partial(_s5_m_kernel, t=t), grid=(nblk,),
        in_specs=[whole((nlag, LANE, LANE))], out_specs=whole((t * LANE, t * LANE)),
        out_shape=jax.ShapeDtypeStruct((nblk, t * LANE, t * LANE), BF16),
        compiler_params=_cp("parallel"), name="s5_m_op",
    )(d_tab)
    tab_spec = whole((t, gl, nd, gc, w))
    p_op = pl.pallas_call(
        functools.partial(_s5_pq_kernel, transpose=False), grid=(nblk,),
        in_specs=[tab_spec], out_specs=whole((t * LANE, ncol)),
        out_shape=jax.ShapeDtypeStruct((nblk, t * LANE, ncol), BF16),
        compiler_params=_cp("parallel"), name="s5_p_op",
    )(p_tab)
    q_op = pl.pallas_call(
        functools.partial(_s5_pq_kernel, transpose=True), grid=(nblk,),
        in_specs=[tab_spec], out_specs=whole((ncol, t * LANE)),
        out_shape=jax.ShapeDtypeStruct((nblk, ncol, t * LANE), BF16),
        compiler_params=_cp("parallel"), name="s5_q_op",
    )(q_tab)
    return m_op, p_op, q_op


def _s5_sum_kernel(*refs, t):
    u_refs = refs[:t]
    p_ref = refs[t]
    outs = refs[t + 1:]
    u = jnp.concatenate([r[...] for r in u_refs], axis=1)
    res = _dot(u, p_ref[0])
    w = res.shape[1] // len(outs)
    for i, o in enumerate(outs):
        o[...] = res[:, i * w:(i + 1) * w]


def _s5_sum_call(h, p_op, rb):
    rows = h.shape[0]
    t = S5_T
    nblk = p_op.shape[0]
    w = p_op.shape[2] // 4
    u_specs = [pl.BlockSpec((rb, LANE), lambda gb, r, s=s: (r, s * nblk + gb)) for s in range(t)]
    out_spec = pl.BlockSpec((rb, w), lambda gb, r: (r, gb))
    return pl.pallas_call(
        functools.partial(_s5_sum_kernel, t=t), grid=(nblk, rows // rb),
        in_specs=u_specs + [pl.BlockSpec((1,) + p_op.shape[1:], lambda gb, r: (gb, 0, 0))],
        out_specs=[out_spec] * 4,
        out_shape=[jax.ShapeDtypeStruct((rows, nblk * w), F32)] * 4,
        compiler_params=_cp("parallel", "parallel"), name="s5_sum",
    )(*([h] * t), p_op)


def _s5_rec_kernel(sr_ref, si_ref, ar_ref, ai_ref, h0r_ref, h0i_ref, hr_ref, hi_ref, fr_ref, fi_ref,
                   cr, ci, *, kb, reverse):
    @pl.when(pl.program_id(1) == 0)
    def _():
        cr[...] = h0r_ref[0]
        ci[...] = h0i_ref[0]

    ar, ai = ar_ref[...], ai_ref[...]

    def body(i, carry):
        hr, hi = carry
        k = kb - 1 - i if reverse else i
        hr_ref[pl.ds(k, 1), :] = hr
        hi_ref[pl.ds(k, 1), :] = hi
        nr = ar * hr - ai * hi + sr_ref[pl.ds(k, 1), :]
        ni = ar * hi + ai * hr + si_ref[pl.ds(k, 1), :]
        return nr, ni

    hr, hi = lax.fori_loop(0, kb, body, (cr[...], ci[...]))
    cr[...] = hr
    ci[...] = hi
    fr_ref[0] = hr
    fi_ref[0] = hi


def _s5_rec_call(sr, si, ar, ai, h0r, h0i, nb_batch, reverse):
    rows, w = sr.shape
    nk = rows // nb_batch
    kb = min(64, nk)
    nb = nk // kb
    blk = (lambda bi, i: (bi * nb + nb - 1 - i, 0)) if reverse else (lambda bi, i: (bi * nb + i, 0))
    row_spec = pl.BlockSpec((kb, w), blk)
    vec = _const_spec((1, w))
    st = pl.BlockSpec((1, 1, w), lambda bi, i: (bi, 0, 0))
    return pl.pallas_call(
        functools.partial(_s5_rec_kernel, kb=kb, reverse=reverse), grid=(nb_batch, nb),
        in_specs=[row_spec, row_spec, vec, vec, st, st],
        out_specs=[row_spec, row_spec, st, st],
        out_shape=[jax.ShapeDtypeStruct((rows, w), F32)] * 2 + [jax.ShapeDtypeStruct((nb_batch, 1, w), F32)] * 2,
        scratch_shapes=[pltpu.VMEM((1, w), F32), pltpu.VMEM((1, w), F32)],
        compiler_params=_cp("parallel", "arbitrary"), name="s5_rec",
    )(sr, si, ar, ai, h0r, h0i)


def _s5_out_kernel(*refs, t):
    u_refs = refs[:t]
    h_refs = refs[t:t + 4]
    m_ref, q_ref, o_ref = refs[t + 4:]
    u = jnp.concatenate([r[...] for r in u_refs], axis=1)
    hcat = jnp.concatenate([r[...].astype(BF16) for r in h_refs], axis=1)
    y = _dot(u, m_ref[0]) + _dot(hcat, q_ref[0])
    o_ref[0] = (0.5 * y * (1.0 + lax.erf(y * (2.0 ** -0.5)))).astype(o_ref.dtype)


def _s5_out_call(h, states, m_op, q_op, rb):
    rows = h.shape[0]
    t = S5_T
    nblk = m_op.shape[0]
    w = q_op.shape[1] // 4
    u_specs = [pl.BlockSpec((rb, LANE), lambda gb, r, s=s: (r, s * nblk + gb)) for s in range(t)]
    return pl.pallas_call(
        functools.partial(_s5_out_kernel, t=t), grid=(nblk, rows // rb),
        in_specs=u_specs + [pl.BlockSpec((rb, w), lambda gb, r: (r, gb))] * 4
        + [pl.BlockSpec((1,) + m_op.shape[1:], lambda gb, r: (gb, 0, 0)),
           pl.BlockSpec((1,) + q_op.shape[1:], lambda gb, r: (gb, 0, 0))],
        out_specs=pl.BlockSpec((1, rb, t * LANE), lambda gb, r: (gb, r, 0)),
        out_shape=jax.ShapeDtypeStruct((nblk, rows, t * LANE), BF16),
        compiler_params=_cp("parallel", "parallel"), name="s5_out",
    )(*([h] * t), *states, m_op, q_op)


def _s5_layer(xl, hl, hc, mods_l, fin, lam_re, lam_im, log_dt, b_re, b_im, c_re, c_im,
              d_skip, w_glu, b_glu, b, n, c):
    d = xl.shape[-1]
    t = S5_T
    m_op, p_op, q_op, a_re, a_im = _s5_operators(lam_re, lam_im, log_dt, b_re, b_im, c_re, c_im, d_skip)
    nblk = m_op.shape[0]
    w = a_re.shape[-1]
    hl = hl.reshape(b * n // t, t * d)
    hc = hc.reshape(b * c // t, t * d)

    def scan(h, init):
        sfr, sfi, sbr, sbi = _s5_sum_call(h, p_op, min(512, h.shape[0]))
        hfr, hfi, ffr, ffi = _s5_rec_call(sfr, sfi, a_re[0:1], a_im[0:1], init[0], init[1], b, False)
        hbr, hbi, fbr, fbi = _s5_rec_call(sbr, sbi, a_re[1:2], a_im[1:2], init[2], init[3], b, True)
        return (hfr, hfi, hbr, hbi), (ffr, ffi, fbr, fbi)

    zeros = jnp.zeros((b, 1, w), F32)
    _, ctx_final = scan(hc, (zeros,) * 4)
    states, _ = scan(hl, ctx_final)
    nk = n // t
    gact = _s5_out_call(hl, states, m_op, q_op, min(512, b * nk))
    g_nat = jnp.transpose(gact.reshape(nblk, b * nk, t, LANE), (1, 2, 0, 3)).reshape(b * n, d)
    return fin(xl, g_nat, mods_l, w_glu.astype(BF16), b_glu, min(512, n), n, glu=True)


def _fnet_channel_mats():
    cc, sc = _dft_cs(FNET_GC, FNET_GC, FNET_GC)
    return jnp.asarray(np.concatenate([cc, sc], axis=1) / np.sqrt(FNET_GC), BF16)


def _fn_c_kernel(l5_ref, l6_ref, xr_ref, xi_ref, o_ref, *, nh):
    r = KRON_R
    n2 = nh * r
    l5 = l5_ref[...]
    y5 = [_dot(l5, jnp.concatenate([xr_ref[0, f], xi_ref[0, f]], axis=0)).astype(BF16) for f in range(r)]
    l6 = l6_ref[...]
    for p in range(nh):
        x = jnp.concatenate([y5[f][p * r:(p + 1) * r] for f in range(r)]
                            + [y5[f][n2 + p * r:n2 + (p + 1) * r] for f in range(r)], axis=0)
        out = _dot(l6, x)
        for q in range(r):
            o_ref[0, p + nh * q] = out[q * r:(q + 1) * r].astype(o_ref.dtype)


def _fnet_layer(xl, am, bm, mods_l, fin, w_o, b_o, b, n, d):
    n2 = FFT_N2
    n1 = n // n2
    tm = min(512, n)
    a, _ = _split_radix(n1)
    l1, l2 = _slab_dft_mats(n1, a, 0, -1, neg_im=True, scale=1.0 / np.sqrt(n))
    twc, tws = _twiddle_tables(n1, n2)
    a4, b4 = am.reshape(b, n1, n2, d), bm.reshape(b, n1, n2, d)
    xr, xi = _kfa_call([(a4, None), (b4, None)], b, n1, n2, d, l1, l2, twc, tws)

    r, w = KRON_R, min(KRON_W, d)
    nh = n2 // r
    assert n1 % r == 0 and n2 % r == 0
    m5 = np.zeros((n2, n2), np.complex128)
    m6 = np.zeros((r * r, r * r), np.complex128)
    for s in range(r):
        for p in range(nh):
            for h in range(nh):
                m5[p * r + s, h * r + s] = np.exp(-2j * np.pi * (p * h / nh + p * s / n2))
        for q in range(r):
            for f in range(r):
                m6[q * r + f, f * r + s] = np.exp(-2j * np.pi * q * s / r)
    l5 = jnp.asarray(_cblock(m5), BF16)
    l6 = jnp.asarray(np.concatenate([m6.real, -m6.imag], axis=1), BF16)
    grp = pl.BlockSpec((1, r, n2, w), lambda bi, fh, k: (bi, fh, 0, k))
    y = pl.pallas_call(
        functools.partial(_fn_c_kernel, nh=nh), grid=(b, n1 // r, d // w),
        in_specs=[_const_spec(l5.shape), _const_spec(l6.shape), grp, grp],
        out_specs=pl.BlockSpec((1, n2, None, r, w), lambda bi, fh, k: (bi, 0, fh, 0, k)),
        out_shape=jax.ShapeDtypeStruct((b, n2, n1 // r, r, d), BF16),
        compiler_params=_cp("parallel", "parallel", "parallel"), name="fn_c",
    )(l5, l6, xr, xi)
    return fin(xl, y.reshape(b * n, d), mods_l, w_o.astype(BF16), b_o, tm, n)


def kernel(x, c, ctx, c_ctx, mod_w, mod_b, mix_pre_g, mix_post_g, ffn_pre_g, ffn_post_g, ffn_w13, ffn_w2,
           mla_w_in, mla_q_norm_g, mla_kv_norm_g, mla_w_uq, mla_w_ukv, mla_w_o,
           hy_w_in, hy_b_in, hy_conv_w, hy_conv_b, hy_f_w1, hy_f_b1, hy_f_freq, hy_f_w2, hy_f_b2, hy_f_w3,
           hy_skip, hy_w_out, hy_b_out,
           s5_lambda_re, s5_lambda_im, s5_log_dt, s5_b_re, s5_b_im, s5_c_re, s5_c_im, s5_d, s5_w_glu, s5_b_glu,
           fn_w_o, fn_b_o):
    b, n, d = x.shape
    cl = ctx.shape[1]
    depth = mod_w.shape[0]
    assert b == 2 and depth == 4, "two batches ride one complex transform; one layer per mixer"
    mods = _mods(c, c_ctx, mod_w, mod_b)
    xl = x.reshape(b * n, d)
    xc = ctx.reshape(b * cl, d)

    w13_all = ffn_w13.astype(BF16)
    w2_all = ffn_w2.astype(BF16)

    def mods_c(i):
        return jnp.broadcast_to(mods[i, 2:3], (b, 8, d))

    def finisher(i, emit_kind=None, cs=None):
        def fin(x_, y_, mods_, wm, bm, tm_, rows_per_batch, glu=False, is_ctx=False):
            emit = None
            if emit_kind is not None:
                emit = (emit_kind, mods_c(i + 1) if is_ctx else mods[i + 1, 0:2], mix_pre_g[i + 1], cs)
            return _mix_ffn_call(x_, y_, mods_, wm, bm, mix_post_g[i], ffn_pre_g[i], ffn_post_g[i],
                                 w13_all, w2_all, i, tm_, rows_per_batch, glu, emit)
        return fin

    xl, xc = _mla_layer(xl, xc, mods[0, 0:2], mods_c(0), mix_pre_g[0], finisher(0), mla_w_in[0],
                        mla_q_norm_g[0], mla_kv_norm_g[0], mla_w_uq[0], mla_w_ukv[0], mla_w_o[0], b, n, cl)
    filt = (hy_f_w1[0], hy_f_b1[0], hy_f_freq[0], hy_f_w2[0], hy_f_b2[0], hy_f_w3[0])
    xl, hl, hc = _hyena_layer(xl, xc, mods[1, 0:2], mods_c(1), mix_pre_g[1], finisher(1, "normmod"), hy_w_in[0],
                              hy_b_in[0], hy_conv_w[0], hy_conv_b[0], filt, hy_skip[0], hy_w_out[0], hy_b_out[0],
                              b, n, cl)
    xl, (am, bm) = _s5_layer(xl, hl, hc, mods[2, 0:2], finisher(2, "fnet", _fnet_channel_mats()), s5_lambda_re[0],
                             s5_lambda_im[0], s5_log_dt[0], s5_b_re[0], s5_b_im[0], s5_c_re[0], s5_c_im[0],
                             s5_d[0], s5_w_glu[0], s5_b_glu[0], b, n, cl)
    xl = _fnet_layer(xl, am, bm, mods[3, 0:2], finisher(3), fn_w_o[0], fn_b_o[0], b, n, d)
    return xl.reshape(b, n, d)
```

```python
import functools
import math

import numpy as np
import jax
import jax.numpy as jnp
from jax import lax
from jax.experimental import pallas as pl
from jax.experimental.pallas import tpu as pltpu

F32 = jnp.float32
BF16 = jnp.bfloat16
NORM_EPS = 1e-6
LANE = 128
MXU_COLS = 256
VMEM_LIMIT = 56 * 1024 * 1024
HIGHEST = lax.Precision.HIGHEST

GRID_W = 64
ROPE_THETA = 10000.0
MLA_HEADS = 8
MLA_NOPE = 128
MLA_ROPE = 64
MLA_V = 128
MLA_VT = MLA_V + 16
HYENA_BANDS = 16
HYENA_TARGET = 1e-2
HYENA_FAST = 0.3
HYENA_SLOW = 1.5
S5_GROUP = 16
S5_STATE = 64
S5_T = 16
FNET_GC = 128
FFT_N2 = 128


def _cp(*sem):
    return pltpu.CompilerParams(dimension_semantics=sem, vmem_limit_bytes=VMEM_LIMIT)


def _dot(a, b):
    return jnp.dot(a, b, preferred_element_type=F32)


def _dot_hi(a, b):
    return jnp.dot(a, b, preferred_element_type=F32, precision=HIGHEST)


def _dot_x3(a, b):
    a_hi = a.astype(BF16)
    b_hi = b.astype(BF16)
    a_lo = (a - a_hi.astype(F32)).astype(BF16)
    b_lo = (b - b_hi.astype(F32)).astype(BF16)
    return _dot(a_hi, b_hi) + (_dot(a_hi, b_lo) + _dot(a_lo, b_hi))


def _rms(x, g):
    ms = jnp.mean(x * x, axis=-1, keepdims=True)
    return x * lax.rsqrt(ms + NORM_EPS) * g


def _normmod(x, g, shift, scale):
    return _rms(x, g) * (1.0 + scale) + shift


def _const_spec(shape):
    nd = len(shape)
    return pl.BlockSpec(shape, lambda *_: (0,) * nd)


def _mods_kernel(st_ref, w_ref, b_ref, o_ref):
    st = st_ref[...]
    st = st * jax.nn.sigmoid(st)
    w = w_ref[0]
    rows = [jnp.sum(st[:, r:r + 1] * w, axis=0, keepdims=True) for r in range(3)]
    rows.append(jnp.zeros((5, w.shape[1]), F32))
    o_ref[0] = jnp.concatenate(rows, axis=0) + b_ref[0]


def _mods(c, c_ctx, mod_w, mod_b):
    depth, d, n6 = mod_w.shape
    st = jnp.zeros((d, 8), F32).at[:, 0:2].set(c.T).at[:, 2].set(c_ctx)
    tn = 1024
    out = pl.pallas_call(
        _mods_kernel,
        grid=(depth, n6 // tn),
        in_specs=[_const_spec((d, 8)),
                  pl.BlockSpec((1, d, tn), lambda i, j: (i, 0, j)),
                  pl.BlockSpec((1, 1, tn), lambda i, j: (i, 0, j))],
        out_specs=pl.BlockSpec((1, 8, tn), lambda i, j: (i, 0, j)),
        out_shape=jax.ShapeDtypeStruct((depth, 8, n6), F32),
        compiler_params=_cp("parallel", "parallel"),
        name="mods",
    )(st, mod_w, mod_b.reshape(depth, 1, n6))
    m = out[:, :3].reshape(depth, 3, n6 // d, d)
    return jnp.pad(m, ((0, 0), (0, 0), (0, 8 - n6 // d), (0, 0)))


def _row_specs(tm, d, tpb):
    x_spec = pl.BlockSpec((tm, d), lambda i: (i, 0))
    mod_spec = pl.BlockSpec((1, 8, d), lambda i: (i // tpb, 0, 0))
    return x_spec, mod_spec


def _channel_dft(h, cs):
    gc = FNET_GC
    ab = [_dot(h[:, k * gc:(k + 1) * gc], cs) for k in range(h.shape[1] // gc)]
    return (jnp.concatenate([z[:, :gc] for z in ab], axis=1), jnp.concatenate([z[:, gc:] for z in ab], axis=1))


def _mix_ffn_kernel(x_ref, y_ref, mod_ref, wm_ref, bm_ref, gm_ref, pre_ref, post_ref, w13_ref, w2_ref, *rest,
                    f, cuts, glu, emit):
    if emit == "chunks":
        nmod_ref, ng_ref, o_ref, h_ref, tok_scr = rest
    elif emit == "fnet":
        nmod_ref, ng_ref, cs_ref, o_ref, a_ref, b_ref = rest
    else:
        (o_ref,) = rest
    z = _dot(y_ref[...].astype(BF16), wm_ref[...]) + bm_ref[...]
    if glu:
        d = o_ref.shape[-1]
        z = z[:, :d] * jax.nn.sigmoid(z[:, d:])
    x = x_ref[...] + mod_ref[0, 2:3, :] * _rms(z, gm_ref[...])
    h = _normmod(x, pre_ref[...], mod_ref[0, 3:4, :], mod_ref[0, 4:5, :]).astype(BF16)
    acc = None
    for lo, hi in zip(cuts[:-1], cuts[1:]):
        a = _dot(h, w13_ref[:, lo:hi])
        b = _dot(h, w13_ref[:, f + lo:f + hi])
        gact = (a * jax.nn.sigmoid(a) * b).astype(BF16)
        part = _dot(gact, w2_ref[lo:hi, :])
        acc = part if acc is None else acc + part
    xo = x + mod_ref[0, 5:6, :] * _rms(acc, post_ref[...])
    o_ref[...] = xo
    if emit:
        hn = _normmod(xo, ng_ref[...], nmod_ref[0, 0:1, :], nmod_ref[0, 1:2, :])
        if emit == "chunks":
            tm, d = hn.shape
            for lt in range(d // LANE):
                tok_scr[lt * tm:(lt + 1) * tm, :] = hn[:, lt * LANE:(lt + 1) * LANE]
            for j in range(S5_T):
                for lt in range(d // LANE):
                    rows = tok_scr[pl.ds(lt * tm + j, tm // S5_T, stride=S5_T), :]
                    h_ref[:, j * d + lt * LANE:j * d + (lt + 1) * LANE] = rows.astype(h_ref.dtype)
        else:
            a, b = _channel_dft(hn.astype(BF16), cs_ref[...])
            a_ref[...] = a.astype(a_ref.dtype)
            b_ref[...] = b.astype(b_ref.dtype)


def _mix_ffn_call(x, y, mods, wm, bm, mix_post_g, ffn_pre_g, ffn_post_g, w13, w2, layer, tm, rows_per_batch,
                  glu=False, emit=None):
    m, d = x.shape
    k, nm = wm.shape
    f = w2.shape[1]
    cuts = tuple(range(0, f, 6 * MXU_COLS)) + (f,)
    x_spec, mod_spec = _row_specs(tm, d, rows_per_batch // tm)
    once = lambda shape: pl.BlockSpec(shape, lambda i: (0, 0), pipeline_mode=pl.Buffered(1))
    of_layer = lambda shape: pl.BlockSpec((None,) + shape, lambda i: (layer, 0, 0), pipeline_mode=pl.Buffered(1))
    in_specs = [x_spec, pl.BlockSpec((tm, k), lambda i: (i, 0)), mod_spec,
                once((k, nm)), _const_spec((1, nm)), _const_spec((1, d)), _const_spec((1, d)),
                _const_spec((1, d)), of_layer((d, 2 * f)), of_layer((f, d))]
    args = [x, y, mods, wm, bm.reshape(1, nm), mix_post_g.reshape(1, d), ffn_pre_g.reshape(1, d),
            ffn_post_g.reshape(1, d), w13, w2]
    out_specs, out_shape = [x_spec], [jax.ShapeDtypeStruct((m, d), F32)]
    kind = None
    scratch = []
    if emit is not None:
        kind, next_mods, next_g, cs = emit
        in_specs += [mod_spec, _const_spec((1, d))]
        args += [next_mods, next_g.reshape(1, d)]
        if kind == "fnet":
            in_specs.append(_const_spec(cs.shape))
            args.append(cs)
            out_specs += [x_spec] * 2
            out_shape += [jax.ShapeDtypeStruct((m, d), BF16)] * 2
        else:
            out_specs.append(pl.BlockSpec((tm // S5_T, S5_T * d), lambda i: (i, 0)))
            out_shape.append(jax.ShapeDtypeStruct((m // S5_T, S5_T * d), BF16))
            scratch.append(pltpu.VMEM((tm * (d // LANE), LANE), F32))
    outs = pl.pallas_call(
        functools.partial(_mix_ffn_kernel, f=f, cuts=cuts, glu=glu, emit=kind), grid=(m // tm,),
        in_specs=in_specs, out_specs=out_specs, out_shape=out_shape, scratch_shapes=scratch,
        compiler_params=_cp("parallel"), name="mix_ffn",
    )(*args)
    return outs[0] if emit is None else (outs[0], tuple(outs[1:]))


def _mla_proj_kernel(x_ref, mod_ref, g_ref, w_ref, qg_ref, kvg_ref, wq_ref, wkv_ref, cp_ref, sp_ref,
                     q_ref, k_ref, vt_ref, *, ql, kvl, qscale, positional):
    h = _normmod(x_ref[...], g_ref[...], mod_ref[0, 0:1, :], mod_ref[0, 1:2, :])
    z = _dot(h.astype(BF16), w_ref[...])
    qn = _rms(z[:, :ql], qg_ref[...]).astype(BF16)
    cn = _rms(z[:, ql:ql + kvl], kvg_ref[...]).astype(BF16)
    tk = z.shape[0]
    low = lax.broadcasted_iota(jnp.int32, (tk, 2 * MLA_ROPE), 1) < MLA_ROPE
    if positional:
        cpf = jnp.concatenate([cp_ref[...]] * 2, axis=1)
        spf = jnp.concatenate([sp_ref[...]] * 2, axis=1)

    def rope_slots(pair, keep_raw):
        swapped = pltpu.roll(pair, 64, 1)
        raw = jnp.where(low, 0.0, swapped) if keep_raw else None
        if not positional:
            return raw
        rot = jnp.where(low, pair * cpf + swapped * spf, 0.0)
        return rot + raw if keep_raw else rot

    kr = rope_slots(z[:, ql + kvl:], keep_raw=not positional).astype(BF16)

    zkv = _dot(cn, wkv_ref[...])
    ones_blk = (lax.broadcasted_iota(jnp.int32, (MLA_VT - MLA_V, tk), 0) == 0).astype(BF16)
    for hd in range(MLA_HEADS):
        base = hd * 256
        k_ref[0, hd, 0, :, 0:128] = zkv[:, base:base + 128].astype(BF16)
        k_ref[0, hd, 0, :, 128:256] = kr
        vt_ref[0, hd, 0, 0:MLA_V, :] = zkv[:, base + 128:base + 256].T.astype(BF16)
        vt_ref[0, hd, 0, MLA_V:MLA_VT, :] = ones_blk

    zq = _dot(qn, wq_ref[...])
    for hd in range(MLA_HEADS):
        base = hd * 256
        rp = rope_slots(zq[:, base + 128:base + 256], keep_raw=True)
        qcat = jnp.concatenate([zq[:, base:base + 128], rp], axis=1) * qscale
        q_ref[0, hd] = qcat.T.astype(BF16)


def _flash_kernel(q_ref, kc_ref, vc_ref, *rest, n_lat):
    if n_lat:
        kl_ref, vl_ref, o_ref, s_scr, acc_scr = rest
    else:
        o_ref, acc_scr = rest
    qt = q_ref[0, 0]

    def qk(k, slot):
        s = _dot(k, qt)
        s_scr[slot] = s
        return jnp.max(s, axis=0, keepdims=True)

    def sm_pv(slot, vt, m, mx):
        m_new = jnp.maximum(m, mx)
        alpha = jnp.exp2(m - m_new)
        p = jnp.exp2(s_scr[slot] - m_new).astype(BF16)
        acc_scr[...] = alpha * acc_scr[...] + _dot(vt, p)
        return m_new

    sc = _dot(kc_ref[0, 0, 0], qt)
    if n_lat:
        mx = qk(kl_ref[0, 0, 0], 0)
    m = jnp.max(sc, axis=0, keepdims=True)
    acc_scr[...] = _dot(vc_ref[0, 0, 0], jnp.exp2(sc - m).astype(BF16))
    if n_lat:

        per = 8 if n_lat % 8 == 0 else (4 if n_lat % 4 == 0 else 2)

        def body(i, carry):
            m, mx_cur = carry
            c = per * i
            for u in range(per):
                mx_next = qk(kl_ref[0, 0, jnp.minimum(c + u + 1, n_lat - 1)], (u + 1) % 2)
                m = sm_pv(u % 2, vl_ref[0, 0, c + u], m, mx_cur)
                mx_cur = mx_next
            return m, mx_cur

        lax.fori_loop(0, n_lat // per, body, (m, mx))
    acc = acc_scr[...]
    o_ref[0] = (acc[0:MLA_V] / acc[MLA_V:MLA_V + 1]).T.astype(o_ref.dtype)


def _rope_tables(n_lat):
    rows = n_lat // GRID_W
    row = jnp.repeat(jnp.arange(rows, dtype=F32), GRID_W)
    col = jnp.tile(jnp.arange(GRID_W, dtype=F32), rows)
    axis_dim = MLA_ROPE // 2
    inv_freq = 1.0 / (ROPE_THETA ** (jnp.arange(0, axis_dim, 2, dtype=F32) / axis_dim))
    ang_r = row[:, None] * inv_freq
    ang_c = col[:, None] * inv_freq
    cr, sr, cc, sc = jnp.cos(ang_r), jnp.sin(ang_r), jnp.cos(ang_c), jnp.sin(ang_c)
    cp = jnp.concatenate([cr, cr, cc, cc], axis=-1)
    sp = jnp.concatenate([-sr, sr, -sc, sc], axis=-1)
    return cp, sp


_ROPE_SWAP = np.concatenate([np.arange(16, 32), np.arange(0, 16), np.arange(48, 64), np.arange(32, 48)])


def _mla_side(x, mods, pre_g, w_in_ext, q_g, kv_g, w_uq_ext, w_ukv, tabs, b, n, tk, positional):
    m, d = x.shape
    ql, kvl = q_g.shape[-1], kv_g.shape[-1]
    nc = n // tk
    x_spec, mod_spec = _row_specs(tk, d, nc)
    tab_spec = pl.BlockSpec((tk, MLA_ROPE), lambda i: (i % nc, 0))
    qscale = (MLA_NOPE + MLA_ROPE) ** -0.5 * math.log2(math.e)
    return pl.pallas_call(
        functools.partial(_mla_proj_kernel, ql=ql, kvl=kvl, qscale=qscale, positional=positional),
        grid=(m // tk,),
        in_specs=[x_spec, mod_spec, _const_spec((1, d)), _const_spec(w_in_ext.shape),
                  _const_spec((1, ql)), _const_spec((1, kvl)), _const_spec(w_uq_ext.shape),
                  _const_spec(w_ukv.shape), tab_spec, tab_spec],
        out_specs=[pl.BlockSpec((1, MLA_HEADS, 256, tk), lambda i: (i // nc, 0, 0, i % nc)),
                   pl.BlockSpec((1, MLA_HEADS, 1, tk, 256), lambda i: (i // nc, 0, i % nc, 0, 0)),
                   pl.BlockSpec((1, MLA_HEADS, 1, MLA_VT, tk), lambda i: (i // nc, 0, i % nc, 0, 0))],
        out_shape=[jax.ShapeDtypeStruct((b, MLA_HEADS, 256, n), BF16),
                   jax.ShapeDtypeStruct((b, MLA_HEADS, nc, tk, 256), BF16),
                   jax.ShapeDtypeStruct((b, MLA_HEADS, nc, MLA_VT, tk), BF16)],
        compiler_params=_cp("parallel"), name="mla_proj",
    )(x, mods, pre_g.reshape(1, d), w_in_ext, q_g.reshape(1, ql), kv_g.reshape(1, kvl), w_uq_ext, w_ukv, *tabs)


def _flash_call(qt, kc, vtc, kl, vtl, tq):
    b, hh, _, n = qt.shape
    c = kc.shape[-2]
    n_lat = 0 if kl is None else kl.shape[2]
    in_specs = [pl.BlockSpec((1, 1, 256, tq), lambda bi, h, i: (bi, h, 0, i)),
                pl.BlockSpec((1, 1, 1, c, 256), lambda bi, h, i: (bi, h, 0, 0, 0)),
                pl.BlockSpec((1, 1, 1, MLA_VT, c), lambda bi, h, i: (bi, h, 0, 0, 0))]
    args = [qt, kc, vtc]
    scratch = [pltpu.VMEM((MLA_VT, tq), F32)]
    if n_lat:
        assert n_lat % 2 == 0, "latent key chunks are consumed in pairs"
        tk = kl.shape[-2]
        in_specs += [pl.BlockSpec((1, 1, n_lat, tk, 256), lambda bi, h, i: (bi, h, 0, 0, 0)),
                     pl.BlockSpec((1, 1, n_lat, MLA_VT, tk), lambda bi, h, i: (bi, h, 0, 0, 0))]
        args += [kl, vtl]
        scratch = [pltpu.VMEM((2, tk, tq), F32)] + scratch
    return pl.pallas_call(
        functools.partial(_flash_kernel, n_lat=n_lat), grid=(b, hh, n // tq),
        in_specs=in_specs,
        out_specs=pl.BlockSpec((1, tq, MLA_V), lambda bi, h, i: (bi, i, h)),
        out_shape=jax.ShapeDtypeStruct((b, n, hh * MLA_V), BF16),
        scratch_shapes=scratch,
        compiler_params=_cp("parallel", "parallel", "arbitrary"), name="flash",
    )(*args)


def _mla_layer(xl, xc, mods_l, mods_c, pre_g, fin, w_in, q_g, kv_g, w_uq, w_ukv, w_o, b, n, c):
    d = xl.shape[-1]
    ql, kvl = q_g.shape[-1], kv_g.shape[-1]
    hh = MLA_HEADS
    rope_cols = w_in[:, ql + kvl:]
    w_in_ext = jnp.concatenate([w_in, rope_cols[:, _ROPE_SWAP]], axis=1).astype(BF16)
    wq = w_uq.reshape(ql, hh, MLA_NOPE + MLA_ROPE)
    w_uq_ext = jnp.concatenate([wq, wq[:, :, MLA_NOPE:][:, :, _ROPE_SWAP]], axis=-1)
    w_uq_ext = w_uq_ext.reshape(ql, hh * 256).astype(BF16)
    w_ukv_b = w_ukv.astype(BF16)
    w_o_b = w_o.astype(BF16)

    tabs = _rope_tables(n)

    tm_l = min(512, n)
    tk_l = min(512, n // 2)
    ql_, kl, vtl = _mla_side(xl, mods_l, pre_g, w_in_ext, q_g, kv_g, w_uq_ext, w_ukv_b, tabs, b, n, tk_l, True)
    qc_, kc, vtc = _mla_side(xc, mods_c, pre_g, w_in_ext, q_g, kv_g, w_uq_ext, w_ukv_b, tabs, b, c, c, False)
    o_lat = _flash_call(ql_, kc, vtc, kl, vtl, min(1024, n)).reshape(b * n, hh * MLA_V)
    o_ctx = _flash_call(qc_, kc, vtc, None, None, c).reshape(b * c, hh * MLA_V)
    zb = jnp.zeros((d,), F32)
    xl = fin(xl, o_lat, mods_l, w_o_b, zb, tm_l, n)
    xc = fin(xc, o_ctx, mods_c, w_o_b, zb, c, c)
    return xl, xc


def _hy_in_kernel(x_ref, xp_ref, xn_ref, mod_ref, g_ref, w_ref, b_ref, cw_ref, cb_ref,
                  g0_ref, vg_ref, *, tpb):
    i = pl.program_id(0)
    g = g_ref[...]
    shift, scale = mod_ref[0, 0:1, :], mod_ref[0, 1:2, :]
    xcat = jnp.concatenate([xp_ref[...], x_ref[...], xn_ref[...]], axis=0)
    hcat = _normmod(xcat, g, shift, scale).astype(BF16)
    tm = x_ref.shape[0]
    d = g0_ref.shape[-1]
    first = (i % tpb) == 0
    last = (i % tpb) == tpb - 1
    ridx = lax.broadcasted_iota(jnp.int32, (tm, 1), 0)

    def conv_part(c):
        cols = slice(c * d, (c + 1) * d)
        ucat = _dot(hcat, w_ref[:, cols]) + b_ref[:, cols]
        u = ucat[8:tm + 8]
        prev_row = jnp.where(first, 0.0, ucat[7:8, :])
        next_row = jnp.where(last, 0.0, ucat[tm + 8:tm + 9, :])
        dn = jnp.where(ridx == 0, prev_row, pltpu.roll(u, 1, 0))
        upw = jnp.where(ridx == tm - 1, next_row, pltpu.roll(u, tm - 1, 0))
        return cb_ref[:, cols] + dn * cw_ref[0:1, cols] + u * cw_ref[1:2, cols] + upw * cw_ref[2:3, cols]

    g0_ref[...] = conv_part(0)
    vg_ref[...] = conv_part(2) * conv_part(1)


def _hy_in_call(x, mods, pre_g, w_in, b_in, conv_w, conv_b, tm, n):
    m, d = x.shape
    p = w_in.shape[1]
    tpb = n // tm
    x_spec, mod_spec = _row_specs(tm, d, tpb)
    r8 = tm // 8
    nb8 = m // 8
    prev_spec = pl.BlockSpec((8, d), lambda i: (jnp.maximum(i * r8 - 1, 0), 0))
    next_spec = pl.BlockSpec((8, d), lambda i: (jnp.minimum((i + 1) * r8, nb8 - 1), 0))
    cw = jnp.pad(conv_w, ((0, 8 - conv_w.shape[0]), (0, 0)))
    return pl.pallas_call(
        functools.partial(_hy_in_kernel, tpb=tpb), grid=(m // tm,),
        in_specs=[x_spec, prev_spec, next_spec, mod_spec, _const_spec((1, d)), _const_spec((d, p)),
                  _const_spec((1, p)), _const_spec((8, p)), _const_spec((1, p))],
        out_specs=[x_spec, x_spec],
        out_shape=[jax.ShapeDtypeStruct((m, d), F32), jax.ShapeDtypeStruct((m, d), F32)],
        compiler_params=_cp("parallel"), name="hy_in",
    )(x, x, x, mods, pre_g.reshape(1, d), w_in, b_in.reshape(1, p), cw, conv_b.reshape(1, p))


_PI_SPLIT = (3.140625, 9.67502593994140625e-4, 1.509957990978376432e-7)
_SIN_TAYLOR = (-1.0 / 6, 1.0 / 120, -1.0 / 5040, 1.0 / 362880, -1.0 / 39916800)


def _sin(x):
    kf = jnp.round(x * (1.0 / math.pi))
    r = ((x - kf * _PI_SPLIT[0]) - kf * _PI_SPLIT[1]) - kf * _PI_SPLIT[2]
    r2 = r * r
    p = _SIN_TAYLOR[4]
    for c in _SIN_TAYLOR[3::-1]:
        p = p * r2 + c
    s = r + r * r2 * p
    odd = (kf.astype(jnp.int32) & 1) == 1
    return jnp.where(odd, -s, s)


def _hy_filter_kernel(bands_ref, w1_ref, b1_ref, fq_ref, w2_ref, b2_ref, w3_ref, dl_ref,
                      k_ref, nrm_ref, *, n, tr):
    i = pl.program_id(0)
    bwd = i >= n // tr
    row = lax.broadcasted_iota(jnp.int32, (tr, LANE), 0) + i * tr
    j = jnp.where(bwd, 2 * n - row, row).astype(F32)
    lane = lax.broadcasted_iota(jnp.int32, (tr, LANE), 1)
    t = j * (1.0 / (n - 1))
    arg = (2.0 * math.pi / n) * j * bands_ref[0:1, :] + bands_ref[1:2, :]
    z = jnp.where(lane == 0, t, jnp.where(lane <= 2 * HYENA_BANDS, _sin(arg), 0.0))
    fq = fq_ref[...]
    a = _sin(fq * (_dot_hi(z, w1_ref[...]) + b1_ref[...]))
    for k in range(w2_ref.shape[0]):
        a = _sin(fq * (_dot_hi(a, w2_ref[k]) + b2_ref[k]))
    h = _dot_x3(a, w3_ref[jnp.where(bwd, 1, 0)])
    decay = jnp.exp(-t[:, 0:1] * dl_ref[...])
    k = h * decay
    k = jnp.where(row[:, 0:1] == n, 0.0, k)
    k_ref[...] = k
    part = jnp.sum(jnp.abs(k), axis=0, keepdims=True)

    @pl.when(i == 0)
    def _():
        nrm_ref[...] = jnp.zeros_like(nrm_ref)

    nrm_ref[...] += jnp.broadcast_to(part, nrm_ref.shape)


def _hy_filter_call(n, d, f_w1, f_b1, f_freq, f_w2, f_b2, f_w3):
    fw = f_w1.shape[1]
    tr = min(512, n)
    bands_np = np.zeros((8, LANE), np.float32)
    bands_np[0, 1:1 + HYENA_BANDS] = np.linspace(1e-4, HYENA_BANDS - 1, HYENA_BANDS, dtype=np.float32)
    bands_np[0, 1 + HYENA_BANDS:1 + 2 * HYENA_BANDS] = bands_np[0, 1:1 + HYENA_BANDS]
    bands_np[1, 1:1 + HYENA_BANDS] = 0.5 * np.pi
    bands_np[1, 1 + HYENA_BANDS:1 + 2 * HYENA_BANDS] = np.pi
    w1p = jnp.zeros((LANE, fw), F32).at[:f_w1.shape[0]].set(f_w1)
    deltas = jnp.abs(jnp.linspace(math.log(HYENA_TARGET) / HYENA_SLOW, math.log(HYENA_TARGET) / HYENA_FAST,
                                  d, dtype=F32)).reshape(1, d)
    row = pl.BlockSpec((tr, d), lambda i: (i, 0))
    return pl.pallas_call(
        functools.partial(_hy_filter_kernel, n=n, tr=tr), grid=(2 * n // tr,),
        in_specs=[_const_spec((8, LANE)), _const_spec((LANE, fw)), _const_spec((1, fw)), _const_spec((1, fw)),
                  _const_spec(f_w2.shape), _const_spec((f_w2.shape[0], 1, fw)), _const_spec((2, fw, d)),
                  _const_spec((1, d))],
        out_specs=[row, _const_spec((8, d))],
        out_shape=[jax.ShapeDtypeStruct((2 * n, d), F32), jax.ShapeDtypeStruct((8, d), F32)],
        compiler_params=_cp("arbitrary"), name="hy_filter",
    )(jnp.asarray(bands_np), w1p, f_b1.reshape(1, fw), f_freq.reshape(1, fw), f_w2,
      f_b2.reshape(f_w2.shape[0], 1, fw), jnp.transpose(f_w3.reshape(fw, 2, d), (1, 0, 2)), deltas)


def _dft_cs(nf, nt, period):
    ft = (np.arange(nf)[:, None] * np.arange(nt)[None, :]) % period
    ang = 2.0 * np.pi * ft / period
    return np.cos(ang), np.sin(ang)


def _twiddle_tables(n1, n2):
    nn = n1 * n2
    f1 = jnp.arange(n1, dtype=jnp.int32)
    t2 = jnp.arange(n2, dtype=jnp.int32)
    idx = (f1[:, None] * t2[None, :]) % nn
    ang = idx.astype(F32) * (2.0 * math.pi / nn)
    shape = idx.shape + (LANE,)
    return (jnp.broadcast_to(jnp.cos(ang)[..., None], shape),
            jnp.broadcast_to(jnp.sin(ang)[..., None], shape))


KRON_R = 16
KRON_W = 256


def _cblock(mc):
    return np.block([[mc.real, -mc.imag], [mc.imag, mc.real]])


def _split_radix(n1):
    b = 16 if (n1 % 16 == 0 and n1 >= 64) else 4
    assert n1 % b == 0 and (n1 // b) % 2 == 0
    return n1 // b, b


def _slab_dft_mats(n1, a_in, a_out, sign, neg_im=False, real_in=False, scale=1.0):
    a, b = _split_radix(n1)
    r = KRON_R
    eye = np.eye(r)
    w = lambda num, den: np.exp(sign * 2j * np.pi * num / den)
    ua = np.arange(a)[:, None] * np.arange(a)[None, :]
    vb = np.arange(b)[:, None] * np.arange(b)[None, :]
    if sign < 0:
        m1 = np.kron(w(ua[:, :a_in], a) * scale, eye)
        l1 = _cblock(m1)
        if neg_im:
            l1[:, a_in * r:] *= -1.0
        if real_in:
            l1 = l1[:, :a_in * r]
        l2 = np.stack([_cblock(np.kron(w(vb, b) * w(u * np.arange(b)[None, :], n1), eye)) for u in range(a)])
    else:
        l1 = np.stack([_cblock(np.kron(w(vb, b) * w(u * np.arange(b)[:, None], n1), eye)) for u in range(a)])
        l2 = _cblock(np.kron(w(ua[:a_out, :], a) * scale, eye))
    return jnp.asarray(l1, BF16), jnp.asarray(l2, BF16)


def _kfa_kernel(*refs, nparts, a_in, a, b):
    l1_ref, l2_ref = refs[0], refs[1]
    parts = refs[2:2 + nparts]
    twc_ref, tws_ref, or_ref, oi_ref = refs[2 + nparts:]
    r = KRON_R
    reps = or_ref.shape[-1] // LANE
    l1 = l1_ref[...]
    y1 = []
    for bb in range(b):
        x = jnp.concatenate([p[0, aa * b + bb] for p in parts for aa in range(a_in)], axis=0).astype(BF16)
        y1.append(_dot(l1, x).astype(BF16))
    for u in range(a):
        x = jnp.concatenate([y1[bb][u * r:(u + 1) * r] for bb in range(b)]
                            + [y1[bb][(a + u) * r:(a + u + 1) * r] for bb in range(b)], axis=0)
        z = _dot(l2_ref[u], x)
        for v in range(b):
            f1 = u + a * v
            zr = z[v * r:(v + 1) * r]
            zi = z[(b + v) * r:(b + v + 1) * r]
            c = jnp.concatenate([twc_ref[f1]] * reps, axis=1)
            sn = jnp.concatenate([tws_ref[f1]] * reps, axis=1)
            or_ref[0, f1] = (zr * c + zi * sn).astype(or_ref.dtype)
            oi_ref[0, f1] = (zi * c - zr * sn).astype(oi_ref.dtype)


def _kfa_call(parts, nb_out, n1, n2, d, l1, l2, twc, tws):
    a, b = _split_radix(n1)
    a_in = parts[0][0].shape[1] // b
    r, w = KRON_R, min(KRON_W, d)
    in_specs = [_const_spec(l1.shape), pl.BlockSpec(l2.shape, lambda bi, j, k: (0, 0, 0), pipeline_mode=pl.Buffered(1))]
    args = [l1, l2]
    for arr, bi_fixed in parts:
        t1_in = arr.shape[1]
        if bi_fixed is None:
            in_specs.append(pl.BlockSpec((1, t1_in, r, w), lambda bi, j, k: (bi, 0, j, k)))
        else:
            in_specs.append(pl.BlockSpec((1, t1_in, r, w), lambda bi, j, k, f=bi_fixed: (f, 0, j, k)))
        args.append(arr)
    tw_spec = pl.BlockSpec((n1, r, LANE), lambda bi, j, k: (0, j, 0))
    out_spec = pl.BlockSpec((1, n1, r, w), lambda bi, j, k: (bi, 0, j, k))
    out = jax.ShapeDtypeStruct((nb_out, n1, n2, d), BF16)
    return pl.pallas_call(
        functools.partial(_kfa_kernel, nparts=len(parts), a_in=a_in, a=a, b=b),
        grid=(nb_out, n2 // r, d // w),
        in_specs=in_specs + [tw_spec, tw_spec], out_specs=[out_spec, out_spec], out_shape=[out, out],
        compiler_params=_cp("parallel", "parallel", "parallel"), name="kfa",
    )(*args, twc, tws)


def _kfc_kernel(l3_ref, l4_ref, gr_ref, gi_ref, vg_ref, g0_ref, skip_ref, o_ref, *, a, b, a_out):
    r = KRON_R
    y3 = []
    for u in range(a):
        x = jnp.concatenate([gr_ref[0, u + a * v] for v in range(b)]
                            + [gi_ref[0, u + a * v] for v in range(b)], axis=0)
        y3.append(_dot(l3_ref[u], x).astype(BF16))
    l4 = l4_ref[...]
    skip = skip_ref[...]
    for bb in range(b):
        x = jnp.concatenate([y3[u][bb * r:(bb + 1) * r] for u in range(a)]
                            + [y3[u][(b + bb) * r:(b + bb + 1) * r] for u in range(a)], axis=0)
        y = _dot(l4, x)
        for sg in range(2):
            for aa in range(a_out):
                t1 = aa * b + bb
                yb = y[(sg * a_out + aa) * r:(sg * a_out + aa + 1) * r]
                o_ref[sg, t1] = ((yb + vg_ref[sg, t1] * skip) * g0_ref[sg, t1]).astype(o_ref.dtype)


def _kfc_call(g1r, g1i, vg4, g04, skip, n1, n2, d, l3, l4):
    a, b = _split_radix(n1)
    nb, t1_out = vg4.shape[:2]
    r, w = KRON_R, min(KRON_W, d)
    slab = pl.BlockSpec((1, n1, r, w), lambda j, k: (0, 0, j, k))
    nat = pl.BlockSpec((nb, t1_out, r, w), lambda j, k: (0, 0, j, k))
    return pl.pallas_call(
        functools.partial(_kfc_kernel, a=a, b=b, a_out=t1_out // b), grid=(n2 // r, d // w),
        in_specs=[pl.BlockSpec(l3.shape, lambda j, k: (0, 0, 0), pipeline_mode=pl.Buffered(1)),
                  _const_spec(l4.shape), slab, slab, nat, nat, pl.BlockSpec((1, w), lambda j, k: (0, k))],
        out_specs=nat, out_shape=jax.ShapeDtypeStruct(vg4.shape, BF16),
        compiler_params=_cp("parallel", "parallel"), name="kfc",
    )(l3, l4, g1r, g1i, vg4, g04, skip.reshape(1, d))


def _hy_b_kernel(fm_ref, fmc_ref, xr_ref, xi_ref, fr_ref, fi_ref, sc_ref, twc_ref, tws_ref,
                 or_ref, oi_ref, *, nf, d):
    fm = fm_ref[...]
    fmc = fmc_ref[...]
    n2 = xr_ref.shape[1]
    sc = sc_ref[...]
    for s in range(nf):
        x = _dot(fm, jnp.concatenate([xr_ref[s], xi_ref[s]], axis=0))
        kk = _dot(fm, jnp.concatenate([fr_ref[s], fi_ref[s]], axis=0))
        xr, xi = x[:n2], x[n2:]
        kr, ki = kk[:n2] * sc, kk[n2:] * sc
        yr = (xr * kr - xi * ki).astype(BF16)
        yi = (xr * ki + xi * kr).astype(BF16)
        g = _dot(fmc, jnp.concatenate([yr, yi], axis=0))
        gr, gi = g[:n2], g[n2:]
        c = jnp.concatenate([twc_ref[s]] * (d // LANE), axis=1)
        sn = jnp.concatenate([tws_ref[s]] * (d // LANE), axis=1)
        or_ref[s] = (gr * c - gi * sn).astype(or_ref.dtype)
        oi_ref[s] = (gi * c + gr * sn).astype(oi_ref.dtype)


def _block_c(cs, sn, sign):
    return np.block([[cs, -sign * sn], [sign * sn, cs]])


def _hy_conv_long(vg, g0, k, nrm, skip, b, n, d):
    n2 = FFT_N2
    nn = 2 * n
    n1 = nn // n2
    rows = n // n2
    a, _ = _split_radix(n1)
    la1, la2 = _slab_dft_mats(n1, a // 2, 0, -1)
    lk1, lk2 = _slab_dft_mats(n1, a, 0, -1, real_in=True)
    lc3, lc4 = _slab_dft_mats(n1, 0, a // 2, +1)
    cs2, sn2 = _dft_cs(n2, n2, n2)
    fm_b = jnp.asarray(_block_c(cs2, sn2, -1.0), BF16)
    fm_bc = jnp.asarray(_block_c(cs2, sn2, 1.0), BF16)
    twc, tws = _twiddle_tables(n1, n2)

    kfr, kfi = _kfa_call([(k.reshape(1, n1, n2, d), 0)], 1, n1, n2, d, lk1, lk2, twc, tws)
    nf = 4 if n1 % 4 == 0 else 1
    scale = (1.0 / (nrm[0:1, :] * nn))
    slab = pl.BlockSpec((nf, n2, d), lambda j: (j, 0, 0))
    shp3 = (n1, n2, d)
    vg4 = vg.reshape(b, rows, n2, d)
    g04 = g0.reshape(b, rows, n2, d)
    x1r, x1i = _kfa_call([(vg4, 0), (vg4, 1)], 1, n1, n2, d, la1, la2, twc, tws)
    tw_slab = pl.BlockSpec((nf, n2, LANE), lambda j: (j, 0, 0))
    g1r, g1i = pl.pallas_call(
        functools.partial(_hy_b_kernel, nf=nf, d=d), grid=(n1 // nf,),
        in_specs=[_const_spec(fm_b.shape), _const_spec(fm_bc.shape), slab, slab, slab, slab, _const_spec((1, d)),
                  tw_slab, tw_slab],
        out_specs=[slab, slab],
        out_shape=[jax.ShapeDtypeStruct(shp3, BF16)] * 2,
        compiler_params=_cp("parallel"), name="hy_b",
    )(fm_b, fm_bc, x1r.reshape(shp3), x1i.reshape(shp3), kfr.reshape(shp3), kfi.reshape(shp3), scale, twc, tws)
    out = _kfc_call(g1r.reshape(1, n1, n2, d), g1i.reshape(1, n1, n2, d), vg4, g04, skip, n1, n2, d, lc3, lc4)
    return out.reshape(b * n, d)


def _hy_short_kernel(fa_ref, fk_ref, fi_ref, vg_ref, g0_ref, k_ref, nrm_ref, skip_ref, o_ref, *, n):
    z = jnp.concatenate([vg_ref[0], vg_ref[1]], axis=0)
    x = _dot_hi(fa_ref[...], z)
    kk = _dot_hi(fk_ref[...], k_ref[...])
    nn = 2 * n
    sc = 1.0 / (nrm_ref[0:1, :] * nn)
    xr, xi = x[:nn], x[nn:]
    kr, ki = kk[:nn] * sc, kk[nn:] * sc
    y = _dot_hi(fi_ref[...], jnp.concatenate([xr * kr - xi * ki, xr * ki + xi * kr], axis=0))
    skip = skip_ref[...]
    for bi in range(2):
        o_ref[bi] = ((y[bi * n:(bi + 1) * n] + vg_ref[bi] * skip) * g0_ref[bi]).astype(o_ref.dtype)


def _hy_conv_short(vg, g0, k, nrm, skip, b, n, d):
    nn = 2 * n
    cs, sn = _dft_cs(nn, n, nn)
    fa = jnp.asarray(_block_c(cs, sn, -1.0), F32)
    csk, snk = _dft_cs(nn, nn, nn)
    fk = jnp.asarray(np.concatenate([csk, -snk], axis=0), F32)
    csi, sni = _dft_cs(n, nn, nn)
    fi = jnp.asarray(_block_c(csi, sni, 1.0), F32)
    cb = 256
    col3 = pl.BlockSpec((b, n, cb), lambda j: (0, 0, j))
    vec = pl.BlockSpec((1, cb), lambda j: (0, j))
    out = pl.pallas_call(
        functools.partial(_hy_short_kernel, n=n), grid=(d // cb,),
        in_specs=[_const_spec(fa.shape), _const_spec(fk.shape), _const_spec(fi.shape), col3, col3,
                  pl.BlockSpec((nn, cb), lambda j: (0, j)), pl.BlockSpec((8, cb), lambda j: (0, j)), vec],
        out_specs=col3, out_shape=jax.ShapeDtypeStruct((b, n, d), BF16),
        compiler_params=_cp("parallel"), name="hy_short",
    )(fa, fk, fi, vg.reshape(b, n, d), g0.reshape(b, n, d), k, nrm, skip.reshape(1, d))
    return out.reshape(b * n, d)


def _hyena_layer(xl, xc, mods_l, mods_c, pre_g, fin, w_in, b_in, conv_w, conv_b, filt, skip,
                 w_out, b_out, b, n, c):
    d = xl.shape[-1]
    w_in_b = w_in.astype(BF16)
    w_out_b = w_out.astype(BF16)
    tm = min(512, n)
    g0, vg = _hy_in_call(xl, mods_l, pre_g, w_in_b, b_in, conv_w, conv_b, tm, n)
    k, nrm = _hy_filter_call(n, d, *filt)
    u_out = _hy_conv_long(vg, g0, k, nrm, skip[0], b, n, d)
    xl, (hl,) = fin(xl, u_out, mods_l, w_out_b, b_out, tm, n)

    g0c, vgc = _hy_in_call(xc, mods_c, pre_g, w_in_b, b_in, conv_w, conv_b, c, c)
    kc, nrmc = _hy_filter_call(c, d, *filt)
    u_out_c = _hy_conv_short(vgc, g0c, kc, nrmc, skip[0], b, c, d)
    xc, (hc,) = fin(xc, u_out_c, mods_c, w_out_b, b_out, c, c, is_ctx=True)
    return xl, hl, hc


def _s5_operators(lam_re, lam_im, log_dt, b_re, b_im, c_re, c_im, d_skip):
    t = S5_T
    g, ns = lam_re.shape[1], lam_re.shape[2]
    gc = b_re.shape[-1]
    gl = LANE // gc
    nblk = g // gl
    lam = lax.complex(lam_re, lam_im)
    dt = jnp.exp(log_dt)[..., None]
    lam_bar = jnp.exp(lam * dt)
    b_bar = ((lam_bar - 1.0) / lam)[..., None] * lax.complex(b_re, b_im)
    c_mat = lax.complex(c_re, c_im)
    pw = jnp.arange(t + 1, dtype=F32)
    lam_pw = jnp.exp((lam * dt)[None] * pw[:, None, None, None])
    hp = HIGHEST
    kern = jnp.einsum('dgcn,tdgn,dgne->dgtce', c_mat, lam_pw[:t], b_bar, precision=hp).real
    dsk = d_skip.reshape(g, gc)
    kt = jnp.swapaxes(kern, -1, -2)
    centre = kt[0][:, 0] + kt[1][:, 0] + jnp.eye(gc, dtype=F32)[None] * dsk[:, :, None]
    ks = jnp.concatenate([kt[1][:, 1:][:, ::-1], centre[:, None], kt[0][:, 1:]], axis=1)
    ks = jnp.transpose(ks.reshape(nblk, gl, 2 * t - 1, gc, gc), (0, 2, 1, 3, 4)).reshape(nblk, 2 * t - 1, LANE, gc)
    same_group = (np.arange(LANE)[:, None] // gc == np.arange(LANE)[None, :] // gc).astype(np.float32)
    d_tab = jnp.tile(ks, (1, 1, 1, gl)) * same_group

    def compact(zc, im_sign):
        z = jnp.concatenate([zc.real, im_sign * zc.imag], axis=-1)
        z = jnp.transpose(z, (2, 1, 0, 3, 4)).reshape(nblk, gl, t, 2, gc, 2 * ns)
        return jnp.transpose(z, (0, 2, 1, 3, 4, 5))

    b_t = jnp.swapaxes(b_bar, -1, -2)
    pf = lam_pw[:t][::-1][:, 0, :, None, :] * b_t[0][None]
    pb = lam_pw[:t][:, 1, :, None, :] * b_t[1][None]
    p_tab = compact(jnp.stack([pf, pb], axis=0), 1.0)
    qf = c_mat[0][None] * lam_pw[1:t + 1, 0][:, :, None, :]
    qb = c_mat[1][None] * lam_pw[1:t + 1][::-1][:, 1][:, :, None, :]
    q_tab = compact(jnp.stack([qf, qb], axis=0), -1.0)

    a = lam_pw[t]
    m_op, p_op, q_op = _s5_expand(d_tab, p_tab, q_tab)
    return m_op, p_op, q_op, a.real.reshape(2, g * ns), a.imag.reshape(2, g * ns)


def _s5_m_kernel(d_ref, o_ref, *, t):
    for s in range(t):
        for tt in range(t):
            o_ref[0, s * LANE:(s + 1) * LANE, tt * LANE:(tt + 1) * LANE] = d_ref[0, tt - s + t - 1].astype(o_ref.dtype)


def _s5_pq_kernel(c_ref, o_ref, *, transpose):
    t, gl, nd, gc, w = c_ref.shape[1:]
    ns = w // 2
    lane_grp = lax.broadcasted_iota(jnp.int32, (gc, gl * ns), 1) // ns
    for j in range(t):
        rows = []
        for g in range(gl):
            cols = []
            for dd in range(nd):
                piece = c_ref[0, j, g, dd]
                for ri in range(2):
                    tiled = jnp.concatenate([piece[:, ri * ns:(ri + 1) * ns]] * gl, axis=1)
                    cols.append(jnp.where(lane_grp == g, tiled, 0.0))
            rows.append(jnp.concatenate(cols, axis=1))
        blk = jnp.concatenate(rows, axis=0)
        if transpose:
            o_ref[0, :, j * LANE:(j + 1) * LANE] = blk.T.astype(o_ref.dtype)
        else:
            o_ref[0, j * LANE:(j + 1) * LANE, :] = blk.astype(o_ref.dtype)


def _s5_expand(d_tab, p_tab, q_tab):
    nblk, nlag = d_tab.shape[:2]
    t = (nlag + 1) // 2
    _, _, gl, nd, gc, w = p_tab.shape
    ncol = nd * gl * w
    whole = lambda shape: pl.BlockSpec((1,) + shape, lambda b: (b,) + (0,) * len(shape))
    m_op = pl.pallas_call(
        functools.partial(_s5_m_kernel, t=t), grid=(nblk,),
        in_specs=[whole((nlag, LANE, LANE))], out_specs=whole((t * LANE, t * LANE)),
        out_shape=jax.ShapeDtypeStruct((nblk, t * LANE, t * LANE), BF16),
        compiler_params=_cp("parallel"), name="s5_m_op",
    )(d_tab)
    tab_spec = whole((t, gl, nd, gc, w))
    p_op = pl.pallas_call(
        functools.partial(_s5_pq_kernel, transpose=False), grid=(nblk,),
        in_specs=[tab_spec], out_specs=whole((t * LANE, ncol)),
        out_shape=jax.ShapeDtypeStruct((nblk, t * LANE, ncol), BF16),
        compiler_params=_cp("parallel"), name="s5_p_op",
    )(p_tab)
    q_op = pl.pallas_call(
        functools.partial(_s5_pq_kernel, transpose=True), grid=(nblk,),
        in_specs=[tab_spec], out_specs=whole((ncol, t * LANE)),
        out_shape=jax.ShapeDtypeStruct((nblk, ncol, t * LANE), BF16),
        compiler_params=_cp("parallel"), name="s5_q_op",
    )(q_tab)
    return m_op, p_op, q_op


def _s5_sum_kernel(*refs, t):
    u_refs = refs[:t]
    p_ref = refs[t]
    outs = refs[t + 1:]
    u = jnp.concatenate([r[...] for r in u_refs], axis=1)
    res = _dot(u, p_ref[0])
    w = res.shape[1] // len(outs)
    for i, o in enumerate(outs):
        o[...] = res[:, i * w:(i + 1) * w]


def _s5_sum_call(h, p_op, rb):
    rows = h.shape[0]
    t = S5_T
    nblk = p_op.shape[0]
    w = p_op.shape[2] // 4
    u_specs = [pl.BlockSpec((rb, LANE), lambda gb, r, s=s: (r, s * nblk + gb)) for s in range(t)]
    out_spec = pl.BlockSpec((rb, w), lambda gb, r: (r, gb))
    return pl.pallas_call(
        functools.partial(_s5_sum_kernel, t=t), grid=(nblk, rows // rb),
        in_specs=u_specs + [pl.BlockSpec((1,) + p_op.shape[1:], lambda gb, r: (gb, 0, 0))],
        out_specs=[out_spec] * 4,
        out_shape=[jax.ShapeDtypeStruct((rows, nblk * w), F32)] * 4,
        compiler_params=_cp("parallel", "parallel"), name="s5_sum",
    )(*([h] * t), p_op)


def _s5_rec_kernel(sr_ref, si_ref, ar_ref, ai_ref, h0r_ref, h0i_ref, hr_ref, hi_ref, fr_ref, fi_ref,
                   cr, ci, *, kb, reverse):
    @pl.when(pl.program_id(1) == 0)
    def _():
        cr[...] = h0r_ref[0]
        ci[...] = h0i_ref[0]

    ar, ai = ar_ref[...], ai_ref[...]

    def body(i, carry):
        hr, hi = carry
        k = kb - 1 - i if reverse else i
        hr_ref[pl.ds(k, 1), :] = hr
        hi_ref[pl.ds(k, 1), :] = hi
        nr = ar * hr - ai * hi + sr_ref[pl.ds(k, 1), :]
        ni = ar * hi + ai * hr + si_ref[pl.ds(k, 1), :]
        return nr, ni

    hr, hi = lax.fori_loop(0, kb, body, (cr[...], ci[...]))
    cr[...] = hr
    ci[...] = hi
    fr_ref[0] = hr
    fi_ref[0] = hi


def _s5_rec_call(sr, si, ar, ai, h0r, h0i, nb_batch, reverse):
    rows, w = sr.shape
    nk = rows // nb_batch
    kb = min(64, nk)
    nb = nk // kb
    blk = (lambda bi, i: (bi * nb + nb - 1 - i, 0)) if reverse else (lambda bi, i: (bi * nb + i, 0))
    row_spec = pl.BlockSpec((kb, w), blk)
    vec = _const_spec((1, w))
    st = pl.BlockSpec((1, 1, w), lambda bi, i: (bi, 0, 0))
    return pl.pallas_call(
        functools.partial(_s5_rec_kernel, kb=kb, reverse=reverse), grid=(nb_batch, nb),
        in_specs=[row_spec, row_spec, vec, vec, st, st],
        out_specs=[row_spec, row_spec, st, st],
        out_shape=[jax.ShapeDtypeStruct((rows, w), F32)] * 2 + [jax.ShapeDtypeStruct((nb_batch, 1, w), F32)] * 2,
        scratch_shapes=[pltpu.VMEM((1, w), F32), pltpu.VMEM((1, w), F32)],
        compiler_params=_cp("parallel", "arbitrary"), name="s5_rec",
    )(sr, si, ar, ai, h0r, h0i)


def _s5_out_kernel(*refs, t):
    u_refs = refs[:t]
    h_refs = refs[t:t + 4]
    m_ref, q_ref, o_ref, tok_scr = refs[t + 4:]
    u = jnp.concatenate([r[...] for r in u_refs], axis=1)
    hcat = jnp.concatenate([r[...].astype(BF16) for r in h_refs], axis=1)
    y = _dot(u, m_ref[0]) + _dot(hcat, q_ref[0])
    g = 0.5 * y * (1.0 + lax.erf(y * (2.0 ** -0.5)))
    rb = g.shape[0]
    for j in range(t):
        tok_scr[pl.ds(j, rb, stride=t), :] = g[:, j * LANE:(j + 1) * LANE]
    o_ref[...] = tok_scr[...].astype(o_ref.dtype)


def _s5_out_call(h, states, m_op, q_op, rb):
    rows = h.shape[0]
    t = S5_T
    nblk = m_op.shape[0]
    w = q_op.shape[1] // 4
    u_specs = [pl.BlockSpec((rb, LANE), lambda gb, r, s=s: (r, s * nblk + gb)) for s in range(t)]
    return pl.pallas_call(
        functools.partial(_s5_out_kernel, t=t), grid=(nblk, rows // rb),
        in_specs=u_specs + [pl.BlockSpec((rb, w), lambda gb, r: (r, gb))] * 4
        + [pl.BlockSpec((1,) + m_op.shape[1:], lambda gb, r: (gb, 0, 0)),
           pl.BlockSpec((1,) + q_op.shape[1:], lambda gb, r: (gb, 0, 0))],
        out_specs=pl.BlockSpec((rb * t, LANE), lambda gb, r: (r, gb)),
        out_shape=jax.ShapeDtypeStruct((rows * t, nblk * LANE), BF16),
        scratch_shapes=[pltpu.VMEM((rb * t, LANE), F32)],
        compiler_params=_cp("parallel", "parallel"), name="s5_out",
    )(*([h] * t), *states, m_op, q_op)


def _s5_layer(xl, hl, hc, mods_l, fin, lam_re, lam_im, log_dt, b_re, b_im, c_re, c_im,
              d_skip, w_glu, b_glu, b, n, c):
    t = S5_T
    m_op, p_op, q_op, a_re, a_im = _s5_operators(lam_re, lam_im, log_dt, b_re, b_im, c_re, c_im, d_skip)
    w = a_re.shape[-1]

    def scan(h, init):
        sfr, sfi, sbr, sbi = _s5_sum_call(h, p_op, min(512, h.shape[0]))
        hfr, hfi, ffr, ffi = _s5_rec_call(sfr, sfi, a_re[0:1], a_im[0:1], init[0], init[1], b, False)
        hbr, hbi, fbr, fbi = _s5_rec_call(sbr, sbi, a_re[1:2], a_im[1:2], init[2], init[3], b, True)
        return (hfr, hfi, hbr, hbi), (ffr, ffi, fbr, fbi)

    zeros = jnp.zeros((b, 1, w), F32)
    _, ctx_final = scan(hc, (zeros,) * 4)
    states, _ = scan(hl, ctx_final)
    nk = n // t
    g_nat = _s5_out_call(hl, states, m_op, q_op, min(512, b * nk))
    return fin(xl, g_nat, mods_l, w_glu.astype(BF16), b_glu, min(512, n), n, glu=True)


def _fnet_channel_mats():
    cc, sc = _dft_cs(FNET_GC, FNET_GC, FNET_GC)
    return jnp.asarray(np.concatenate([cc, sc], axis=1) / np.sqrt(FNET_GC), BF16)


def _fn_c_kernel(l5_ref, l6_ref, xr_ref, xi_ref, o_ref, *, nh):
    r = KRON_R
    n2 = nh * r
    l5 = l5_ref[...]
    y5 = [_dot(l5, jnp.concatenate([xr_ref[0, f], xi_ref[0, f]], axis=0)).astype(BF16) for f in range(r)]
    l6 = l6_ref[...]
    for p in range(nh):
        x = jnp.concatenate([y5[f][p * r:(p + 1) * r] for f in range(r)]
                            + [y5[f][n2 + p * r:n2 + (p + 1) * r] for f in range(r)], axis=0)
        out = _dot(l6, x)
        for q in range(r):
            o_ref[0, p + nh * q] = out[q * r:(q + 1) * r].astype(o_ref.dtype)


def _fnet_layer(xl, am, bm, mods_l, fin, w_o, b_o, b, n, d):
    n2 = FFT_N2
    n1 = n // n2
    tm = min(512, n)
    a, _ = _split_radix(n1)
    l1, l2 = _slab_dft_mats(n1, a, 0, -1, neg_im=True, scale=1.0 / np.sqrt(n))
    twc, tws = _twiddle_tables(n1, n2)
    a4, b4 = am.reshape(b, n1, n2, d), bm.reshape(b, n1, n2, d)
    xr, xi = _kfa_call([(a4, None), (b4, None)], b, n1, n2, d, l1, l2, twc, tws)

    r, w = KRON_R, min(KRON_W, d)
    nh = n2 // r
    assert n1 % r == 0 and n2 % r == 0
    m5 = np.zeros((n2, n2), np.complex128)
    m6 = np.zeros((r * r, r * r), np.complex128)
    for s in range(r):
        for p in range(nh):
            for h in range(nh):
                m5[p * r + s, h * r + s] = np.exp(-2j * np.pi * (p * h / nh + p * s / n2))
        for q in range(r):
            for f in range(r):
                m6[q * r + f, f * r + s] = np.exp(-2j * np.pi * q * s / r)
    l5 = jnp.asarray(_cblock(m5), BF16)
    l6 = jnp.asarray(np.concatenate([m6.real, -m6.imag], axis=1), BF16)
    grp = pl.BlockSpec((1, r, n2, w), lambda bi, fh, k: (bi, fh, 0, k))
    y = pl.pallas_call(
        functools.partial(_fn_c_kernel, nh=nh), grid=(b, n1 // r, d // w),
        in_specs=[_const_spec(l5.shape), _const_spec(l6.shape), grp, grp],
        out_specs=pl.BlockSpec((1, n2, None, r, w), lambda bi, fh, k: (bi, 0, fh, 0, k)),
        out_shape=jax.ShapeDtypeStruct((b, n2, n1 // r, r, d), BF16),
        compiler_params=_cp("parallel", "parallel", "parallel"), name="fn_c",
    )(l5, l6, xr, xi)
    return fin(xl, y.reshape(b * n, d), mods_l, w_o.astype(BF16), b_o, tm, n)


def kernel(x, c, ctx, c_ctx, mod_w, mod_b, mix_pre_g, mix_post_g, ffn_pre_g, ffn_post_g, ffn_w13, ffn_w2,
           mla_w_in, mla_q_norm_g, mla_kv_norm_g, mla_w_uq, mla_w_ukv, mla_w_o,
           hy_w_in, hy_b_in, hy_conv_w, hy_conv_b, hy_f_w1, hy_f_b1, hy_f_freq, hy_f_w2, hy_f_b2, hy_f_w3,
           hy_skip, hy_w_out, hy_b_out,
           s5_lambda_re, s5_lambda_im, s5_log_dt, s5_b_re, s5_b_im, s5_c_re, s5_c_im, s5_d, s5_w_glu, s5_b_glu,
           fn_w_o, fn_b_o):
    b, n, d = x.shape
    cl = ctx.shape[1]
    depth = mod_w.shape[0]
    assert b == 2 and depth == 4, "two batches ride one complex transform; one layer per mixer"
    mods = _mods(c, c_ctx, mod_w, mod_b)
    xl = x.reshape(b * n, d)
    xc = ctx.reshape(b * cl, d)

    w13_all = ffn_w13.astype(BF16)
    w2_all = ffn_w2.astype(BF16)

    def mods_c(i):
        return jnp.broadcast_to(mods[i, 2:3], (b, 8, d))

    def finisher(i, emit_kind=None, cs=None):
        def fin(x_, y_, mods_, wm, bm, tm_, rows_per_batch, glu=False, is_ctx=False):
            emit = None
            if emit_kind is not None:
                emit = (emit_kind, mods_c(i + 1) if is_ctx else mods[i + 1, 0:2], mix_pre_g[i + 1], cs)
            return _mix_ffn_call(x_, y_, mods_, wm, bm, mix_post_g[i], ffn_pre_g[i], ffn_post_g[i],
                                 w13_all, w2_all, i, tm_, rows_per_batch, glu, emit)
        return fin

    xl, xc = _mla_layer(xl, xc, mods[0, 0:2], mods_c(0), mix_pre_g[0], finisher(0), mla_w_in[0],
                        mla_q_norm_g[0], mla_kv_norm_g[0], mla_w_uq[0], mla_w_ukv[0], mla_w_o[0], b, n, cl)
    filt = (hy_f_w1[0], hy_f_b1[0], hy_f_freq[0], hy_f_w2[0], hy_f_b2[0], hy_f_w3[0])
    xl, hl, hc = _hyena_layer(xl, xc, mods[1, 0:2], mods_c(1), mix_pre_g[1], finisher(1, "chunks"), hy_w_in[0],
                              hy_b_in[0], hy_conv_w[0], hy_conv_b[0], filt, hy_skip[0], hy_w_out[0], hy_b_out[0],
                              b, n, cl)
    xl, (am, bm) = _s5_layer(xl, hl, hc, mods[2, 0:2], finisher(2, "fnet", _fnet_channel_mats()), s5_lambda_re[0],
                             s5_lambda_im[0], s5_log_dt[0], s5_b_re[0], s5_b_im[0], s5_c_re[0], s5_c_im[0],
                             s5_d[0], s5_w_glu[0], s5_b_glu[0], b, n, cl)
    xl = _fnet_layer(xl, am, bm, mods[3, 0:2], finisher(3), fn_w_o[0], fn_b_o[0], b, n, d)
    return xl.reshape(b, n, d)
```

```python
import functools
import math

import numpy as np
import jax
import jax.numpy as jnp
from jax import lax
from jax.experimental import pallas as pl
from jax.experimental.pallas import tpu as pltpu

F32 = jnp.float32
BF16 = jnp.bfloat16
NORM_EPS = 1e-6
LANE = 128
MXU_COLS = 256
VMEM_LIMIT = 56 * 1024 * 1024
HIGHEST = lax.Precision.HIGHEST

GRID_W = 64
ROPE_THETA = 10000.0
MLA_HEADS = 8
MLA_NOPE = 128
MLA_ROPE = 64
MLA_V = 128
MLA_VT = MLA_V + 16
HYENA_BANDS = 16
HYENA_TARGET = 1e-2
HYENA_FAST = 0.3
HYENA_SLOW = 1.5
S5_T = 16
FNET_GC = 128
FFT_N2 = 128


def _cp(*sem):
    return pltpu.CompilerParams(dimension_semantics=sem, vmem_limit_bytes=VMEM_LIMIT)


def _dot(a, b):
    return jnp.dot(a, b, preferred_element_type=F32)


def _dot_hi(a, b):
    return jnp.dot(a, b, preferred_element_type=F32, precision=HIGHEST)


def _dot_x3(a, b):
    a_hi = a.astype(BF16)
    b_hi = b.astype(BF16)
    a_lo = (a - a_hi.astype(F32)).astype(BF16)
    b_lo = (b - b_hi.astype(F32)).astype(BF16)
    return _dot(a_hi, b_hi) + (_dot(a_hi, b_lo) + _dot(a_lo, b_hi))


def _rms(x, g):
    ms = jnp.mean(x * x, axis=-1, keepdims=True)
    return x * lax.rsqrt(ms + NORM_EPS) * g


def _normmod(x, g, shift, scale):
    return _rms(x, g) * (1.0 + scale) + shift


def _const_spec(shape):
    nd = len(shape)
    return pl.BlockSpec(shape, lambda *_: (0,) * nd)


def _mods_kernel(st_ref, w_ref, b_ref, o_ref):
    st = st_ref[...]
    st = st * jax.nn.sigmoid(st)
    w = w_ref[0]
    rows = [jnp.sum(st[:, r:r + 1] * w, axis=0, keepdims=True) for r in range(3)]
    rows.append(jnp.zeros((5, w.shape[1]), F32))
    o_ref[0] = jnp.concatenate(rows, axis=0) + b_ref[0]


def _mods(c, c_ctx, mod_w, mod_b):
    depth, d, n6 = mod_w.shape
    st = jnp.zeros((d, 8), F32).at[:, 0:2].set(c.T).at[:, 2].set(c_ctx)
    tn = 1024
    out = pl.pallas_call(
        _mods_kernel,
        grid=(depth, n6 // tn),
        in_specs=[_const_spec((d, 8)),
                  pl.BlockSpec((1, d, tn), lambda i, j: (i, 0, j)),
                  pl.BlockSpec((1, 1, tn), lambda i, j: (i, 0, j))],
        out_specs=pl.BlockSpec((1, 8, tn), lambda i, j: (i, 0, j)),
        out_shape=jax.ShapeDtypeStruct((depth, 8, n6), F32),
        compiler_params=_cp("parallel", "parallel"),
        name="mods",
    )(st, mod_w, mod_b.reshape(depth, 1, n6))
    m = out[:, :3].reshape(depth, 3, n6 // d, d)
    return jnp.pad(m, ((0, 0), (0, 0), (0, 8 - n6 // d), (0, 0)))


def _row_specs(tm, d, tpb):
    x_spec = pl.BlockSpec((tm, d), lambda i: (i, 0))
    mod_spec = pl.BlockSpec((1, 8, d), lambda i: (i // tpb, 0, 0))
    return x_spec, mod_spec


def _channel_dft(h, cs):
    gc = FNET_GC
    ab = [_dot(h[:, k * gc:(k + 1) * gc], cs) for k in range(h.shape[1] // gc)]
    return (jnp.concatenate([z[:, :gc] for z in ab], axis=1), jnp.concatenate([z[:, gc:] for z in ab], axis=1))


def _mix_ffn_kernel(x_ref, y_ref, mod_ref, wm_ref, bm_ref, gm_ref, pre_ref, post_ref, w13_ref, w2_ref, *rest,
                    f, cuts, glu, emit):
    if emit == "chunks":
        nmod_ref, ng_ref, o_ref, h_ref, tok_scr = rest
    elif emit == "fnet":
        nmod_ref, ng_ref, cs_ref, o_ref, a_ref, b_ref = rest
    else:
        (o_ref,) = rest
    z = _dot(y_ref[...].astype(BF16), wm_ref[...]) + bm_ref[...]
    if glu:
        d = o_ref.shape[-1]
        z = z[:, :d] * jax.nn.sigmoid(z[:, d:])
    x = x_ref[...] + mod_ref[0, 2:3, :] * _rms(z, gm_ref[...])
    h = _normmod(x, pre_ref[...], mod_ref[0, 3:4, :], mod_ref[0, 4:5, :]).astype(BF16)
    acc = None
    for lo, hi in zip(cuts[:-1], cuts[1:]):
        a = _dot(h, w13_ref[:, lo:hi])
        b = _dot(h, w13_ref[:, f + lo:f + hi])
        gact = (a * jax.nn.sigmoid(a) * b).astype(BF16)
        part = _dot(gact, w2_ref[lo:hi, :])
        acc = part if acc is None else acc + part
    xo = x + mod_ref[0, 5:6, :] * _rms(acc, post_ref[...])
    o_ref[...] = xo
    if emit:
        hn = _normmod(xo, ng_ref[...], nmod_ref[0, 0:1, :], nmod_ref[0, 1:2, :])
        if emit == "chunks":
            tm, d = hn.shape
            for lt in range(d // LANE):
                tok_scr[lt * tm:(lt + 1) * tm, :] = hn[:, lt * LANE:(lt + 1) * LANE]
            for j in range(S5_T):
                for lt in range(d // LANE):
                    rows = tok_scr[pl.ds(lt * tm + j, tm // S5_T, stride=S5_T), :]
                    h_ref[:, j * d + lt * LANE:j * d + (lt + 1) * LANE] = rows.astype(h_ref.dtype)
        else:
            a, b = _channel_dft(hn.astype(BF16), cs_ref[...])
            a_ref[...] = a.astype(a_ref.dtype)
            b_ref[...] = b.astype(b_ref.dtype)


def _mix_ffn_call(x, y, mods, wm, bm, mix_post_g, ffn_pre_g, ffn_post_g, w13, w2, layer, tm, rows_per_batch,
                  glu=False, emit=None):
    m, d = x.shape
    k, nm = wm.shape
    f = w2.shape[1]
    cuts = tuple(range(0, f, 6 * MXU_COLS)) + (f,)
    x_spec, mod_spec = _row_specs(tm, d, rows_per_batch // tm)
    once = lambda shape: pl.BlockSpec(shape, lambda i: (0, 0), pipeline_mode=pl.Buffered(1))
    of_layer = lambda shape: pl.BlockSpec((None,) + shape, lambda i: (layer, 0, 0), pipeline_mode=pl.Buffered(1))
    in_specs = [x_spec, pl.BlockSpec((tm, k), lambda i: (i, 0)), mod_spec,
                once((k, nm)), _const_spec((1, nm)), _const_spec((1, d)), _const_spec((1, d)),
                _const_spec((1, d)), of_layer((d, 2 * f)), of_layer((f, d))]
    args = [x, y, mods, wm, bm.reshape(1, nm), mix_post_g.reshape(1, d), ffn_pre_g.reshape(1, d),
            ffn_post_g.reshape(1, d), w13, w2]
    out_specs, out_shape = [x_spec], [jax.ShapeDtypeStruct((m, d), F32)]
    kind = None
    scratch = []
    if emit is not None:
        kind, next_mods, next_g, cs = emit
        in_specs += [mod_spec, _const_spec((1, d))]
        args += [next_mods, next_g.reshape(1, d)]
        if kind == "fnet":
            in_specs.append(_const_spec(cs.shape))
            args.append(cs)
            out_specs += [x_spec] * 2
            out_shape += [jax.ShapeDtypeStruct((m, d), BF16)] * 2
        else:
            out_specs.append(pl.BlockSpec((tm // S5_T, S5_T * d), lambda i: (i, 0)))
            out_shape.append(jax.ShapeDtypeStruct((m // S5_T, S5_T * d), BF16))
            scratch.append(pltpu.VMEM((tm * (d // LANE), LANE), F32))
    outs = pl.pallas_call(
        functools.partial(_mix_ffn_kernel, f=f, cuts=cuts, glu=glu, emit=kind), grid=(m // tm,),
        in_specs=in_specs, out_specs=out_specs, out_shape=out_shape, scratch_shapes=scratch,
        compiler_params=_cp("parallel"), name="mix_ffn",
    )(*args)
    return outs[0] if emit is None else (outs[0], tuple(outs[1:]))


def _mla_proj_kernel(x_ref, mod_ref, g_ref, w_ref, qg_ref, kvg_ref, wq_ref, wkv_ref, cp_ref, sp_ref,
                     q_ref, k_ref, vt_ref, *, ql, kvl, qscale, positional):
    h = _normmod(x_ref[...], g_ref[...], mod_ref[0, 0:1, :], mod_ref[0, 1:2, :])
    z = _dot(h.astype(BF16), w_ref[...])
    qn = _rms(z[:, :ql], qg_ref[...]).astype(BF16)
    cn = _rms(z[:, ql:ql + kvl], kvg_ref[...]).astype(BF16)
    tk = z.shape[0]
    low = lax.broadcasted_iota(jnp.int32, (tk, 2 * MLA_ROPE), 1) < MLA_ROPE
    if positional:
        cpf = jnp.concatenate([cp_ref[...]] * 2, axis=1)
        spf = jnp.concatenate([sp_ref[...]] * 2, axis=1)

    def rope_slots(pair, keep_raw):
        swapped = pltpu.roll(pair, 64, 1)
        raw = jnp.where(low, 0.0, swapped) if keep_raw else None
        if not positional:
            return raw
        rot = jnp.where(low, pair * cpf + swapped * spf, 0.0)
        return rot + raw if keep_raw else rot

    kr = rope_slots(z[:, ql + kvl:], keep_raw=not positional).astype(BF16)

    zkv = _dot(cn, wkv_ref[...])
    ones_blk = (lax.broadcasted_iota(jnp.int32, (MLA_VT - MLA_V, tk), 0) == 0).astype(BF16)
    for hd in range(MLA_HEADS):
        base = hd * 256
        k_ref[0, hd, 0, :, 0:128] = zkv[:, base:base + 128].astype(BF16)
        k_ref[0, hd, 0, :, 128:256] = kr
        vt_ref[0, hd, 0, 0:MLA_V, :] = zkv[:, base + 128:base + 256].T.astype(BF16)
        vt_ref[0, hd, 0, MLA_V:MLA_VT, :] = ones_blk

    zq = _dot(qn, wq_ref[...])
    for hd in range(MLA_HEADS):
        base = hd * 256
        rp = rope_slots(zq[:, base + 128:base + 256], keep_raw=True)
        qcat = jnp.concatenate([zq[:, base:base + 128], rp], axis=1) * qscale
        q_ref[0, hd] = qcat.T.astype(BF16)


def _flash_kernel(q_ref, kc_ref, vc_ref, *rest, n_lat):
    if n_lat:
        kl_ref, vl_ref, o_ref, s_scr, acc_scr = rest
    else:
        o_ref, acc_scr = rest
    qt = q_ref[0, 0]

    def qk(k, slot):
        s = _dot(k, qt)
        s_scr[slot] = s
        return jnp.max(s, axis=0, keepdims=True)

    def sm_pv(slot, vt, m, mx):
        m_new = jnp.maximum(m, mx)
        alpha = jnp.exp2(m - m_new)
        p = jnp.exp2(s_scr[slot] - m_new).astype(BF16)
        acc_scr[...] = alpha * acc_scr[...] + _dot(vt, p)
        return m_new

    sc = _dot(kc_ref[0, 0, 0], qt)
    if n_lat:
        mx = qk(kl_ref[0, 0, 0], 0)
    m = jnp.max(sc, axis=0, keepdims=True)
    acc_scr[...] = _dot(vc_ref[0, 0, 0], jnp.exp2(sc - m).astype(BF16))
    if n_lat:

        per = next(p for p in (16, 8, 4, 2) if n_lat % p == 0)

        def body(i, carry):
            m, mx_cur = carry
            c = per * i
            for u in range(per):
                mx_next = qk(kl_ref[0, 0, jnp.minimum(c + u + 1, n_lat - 1)], (u + 1) % 2)
                m = sm_pv(u % 2, vl_ref[0, 0, c + u], m, mx_cur)
                mx_cur = mx_next
            return m, mx_cur

        lax.fori_loop(0, n_lat // per, body, (m, mx))
    acc = acc_scr[...]
    o_ref[0] = (acc[0:MLA_V] / acc[MLA_V:MLA_V + 1]).T.astype(o_ref.dtype)


def _rope_tables(n_lat):
    rows = n_lat // GRID_W
    row = jnp.repeat(jnp.arange(rows, dtype=F32), GRID_W)
    col = jnp.tile(jnp.arange(GRID_W, dtype=F32), rows)
    axis_dim = MLA_ROPE // 2
    inv_freq = 1.0 / (ROPE_THETA ** (jnp.arange(0, axis_dim, 2, dtype=F32) / axis_dim))
    ang_r = row[:, None] * inv_freq
    ang_c = col[:, None] * inv_freq
    cr, sr, cc, sc = jnp.cos(ang_r), jnp.sin(ang_r), jnp.cos(ang_c), jnp.sin(ang_c)
    cp = jnp.concatenate([cr, cr, cc, cc], axis=-1)
    sp = jnp.concatenate([-sr, sr, -sc, sc], axis=-1)
    return cp, sp


_ROPE_SWAP = np.concatenate([np.arange(16, 32), np.arange(0, 16), np.arange(48, 64), np.arange(32, 48)])


def _mla_side(x, mods, pre_g, w_in_ext, q_g, kv_g, w_uq_ext, w_ukv, tabs, b, n, tk, positional):
    m, d = x.shape
    ql, kvl = q_g.shape[-1], kv_g.shape[-1]
    nc = n // tk
    x_spec, mod_spec = _row_specs(tk, d, nc)
    tab_spec = pl.BlockSpec((tk, MLA_ROPE), lambda i: (i % nc, 0))
    qscale = (MLA_NOPE + MLA_ROPE) ** -0.5 * math.log2(math.e)
    return pl.pallas_call(
        functools.partial(_mla_proj_kernel, ql=ql, kvl=kvl, qscale=qscale, positional=positional),
        grid=(m // tk,),
        in_specs=[x_spec, mod_spec, _const_spec((1, d)), _const_spec(w_in_ext.shape),
                  _const_spec((1, ql)), _const_spec((1, kvl)), _const_spec(w_uq_ext.shape),
                  _const_spec(w_ukv.shape), tab_spec, tab_spec],
        out_specs=[pl.BlockSpec((1, MLA_HEADS, 256, tk), lambda i: (i // nc, 0, 0, i % nc)),
                   pl.BlockSpec((1, MLA_HEADS, 1, tk, 256), lambda i: (i // nc, 0, i % nc, 0, 0)),
                   pl.BlockSpec((1, MLA_HEADS, 1, MLA_VT, tk), lambda i: (i // nc, 0, i % nc, 0, 0))],
        out_shape=[jax.ShapeDtypeStruct((b, MLA_HEADS, 256, n), BF16),
                   jax.ShapeDtypeStruct((b, MLA_HEADS, nc, tk, 256), BF16),
                   jax.ShapeDtypeStruct((b, MLA_HEADS, nc, MLA_VT, tk), BF16)],
        compiler_params=_cp("parallel"), name="mla_proj",
    )(x, mods, pre_g.reshape(1, d), w_in_ext, q_g.reshape(1, ql), kv_g.reshape(1, kvl), w_uq_ext, w_ukv, *tabs)


def _flash_call(qt, kc, vtc, kl, vtl, tq):
    b, hh, _, n = qt.shape
    c = kc.shape[-2]
    n_lat = 0 if kl is None else kl.shape[2]
    in_specs = [pl.BlockSpec((1, 1, 256, tq), lambda bi, h, i: (bi, h, 0, i)),
                pl.BlockSpec((1, 1, 1, c, 256), lambda bi, h, i: (bi, h, 0, 0, 0)),
                pl.BlockSpec((1, 1, 1, MLA_VT, c), lambda bi, h, i: (bi, h, 0, 0, 0))]
    args = [qt, kc, vtc]
    scratch = [pltpu.VMEM((MLA_VT, tq), F32)]
    if n_lat:
        assert n_lat % 2 == 0, "latent key chunks are consumed in pairs"
        tk = kl.shape[-2]
        in_specs += [pl.BlockSpec((1, 1, n_lat, tk, 256), lambda bi, h, i: (bi, h, 0, 0, 0)),
                     pl.BlockSpec((1, 1, n_lat, MLA_VT, tk), lambda bi, h, i: (bi, h, 0, 0, 0))]
        args += [kl, vtl]
        scratch = [pltpu.VMEM((2, tk, tq), F32)] + scratch
    return pl.pallas_call(
        functools.partial(_flash_kernel, n_lat=n_lat), grid=(b, hh, n // tq),
        in_specs=in_specs,
        out_specs=pl.BlockSpec((1, tq, MLA_V), lambda bi, h, i: (bi, i, h)),
        out_shape=jax.ShapeDtypeStruct((b, n, hh * MLA_V), BF16),
        scratch_shapes=scratch,
        compiler_params=_cp("parallel", "parallel", "arbitrary"), name="flash",
    )(*args)


def _mla_layer(xl, xc, mods_l, mods_c, pre_g, fin, w_in, q_g, kv_g, w_uq, w_ukv, w_o, b, n, c):
    d = xl.shape[-1]
    ql, kvl = q_g.shape[-1], kv_g.shape[-1]
    hh = MLA_HEADS
    rope_cols = w_in[:, ql + kvl:]
    w_in_ext = jnp.concatenate([w_in, rope_cols[:, _ROPE_SWAP]], axis=1).astype(BF16)
    wq = w_uq.reshape(ql, hh, MLA_NOPE + MLA_ROPE)
    w_uq_ext = jnp.concatenate([wq, wq[:, :, MLA_NOPE:][:, :, _ROPE_SWAP]], axis=-1)
    w_uq_ext = w_uq_ext.reshape(ql, hh * 256).astype(BF16)
    w_ukv_b = w_ukv.astype(BF16)
    w_o_b = w_o.astype(BF16)

    tabs = _rope_tables(n)

    tm_l = min(512, n)
    tk_l = min(512, n // 2)
    ql_, kl, vtl = _mla_side(xl, mods_l, pre_g, w_in_ext, q_g, kv_g, w_uq_ext, w_ukv_b, tabs, b, n, tk_l, True)
    qc_, kc, vtc = _mla_side(xc, mods_c, pre_g, w_in_ext, q_g, kv_g, w_uq_ext, w_ukv_b, tabs, b, c, c, False)
    o_lat = _flash_call(ql_, kc, vtc, kl, vtl, min(1024, n)).reshape(b * n, hh * MLA_V)
    o_ctx = _flash_call(qc_, kc, vtc, None, None, c).reshape(b * c, hh * MLA_V)
    zb = jnp.zeros((d,), F32)
    xl = fin(xl, o_lat, mods_l, w_o_b, zb, tm_l, n)
    xc = fin(xc, o_ctx, mods_c, w_o_b, zb, c, c)
    return xl, xc


def _hy_in_kernel(x_ref, xp_ref, xn_ref, mod_ref, g_ref, w_ref, b_ref, cw_ref, cb_ref,
                  g0_ref, vg_ref, *, tpb):
    i = pl.program_id(0)
    g = g_ref[...]
    shift, scale = mod_ref[0, 0:1, :], mod_ref[0, 1:2, :]
    xcat = jnp.concatenate([xp_ref[...], x_ref[...], xn_ref[...]], axis=0)
    hcat = _normmod(xcat, g, shift, scale).astype(BF16)
    tm = x_ref.shape[0]
    d = g0_ref.shape[-1]
    first = (i % tpb) == 0
    last = (i % tpb) == tpb - 1
    ridx = lax.broadcasted_iota(jnp.int32, (tm, 1), 0)

    def conv_part(c):
        cols = slice(c * d, (c + 1) * d)
        ucat = _dot(hcat, w_ref[:, cols]) + b_ref[:, cols]
        u = ucat[8:tm + 8]
        prev_row = jnp.where(first, 0.0, ucat[7:8, :])
        next_row = jnp.where(last, 0.0, ucat[tm + 8:tm + 9, :])
        dn = jnp.where(ridx == 0, prev_row, pltpu.roll(u, 1, 0))
        upw = jnp.where(ridx == tm - 1, next_row, pltpu.roll(u, tm - 1, 0))
        return cb_ref[:, cols] + dn * cw_ref[0:1, cols] + u * cw_ref[1:2, cols] + upw * cw_ref[2:3, cols]

    g0_ref[...] = conv_part(0)
    vg_ref[...] = conv_part(2) * conv_part(1)


def _hy_in_call(x, mods, pre_g, w_in, b_in, conv_w, conv_b, tm, n):
    m, d = x.shape
    p = w_in.shape[1]
    tpb = n // tm
    x_spec, mod_spec = _row_specs(tm, d, tpb)
    r8 = tm // 8
    nb8 = m // 8
    prev_spec = pl.BlockSpec((8, d), lambda i: (jnp.maximum(i * r8 - 1, 0), 0))
    next_spec = pl.BlockSpec((8, d), lambda i: (jnp.minimum((i + 1) * r8, nb8 - 1), 0))
    cw = jnp.pad(conv_w, ((0, 8 - conv_w.shape[0]), (0, 0)))
    return pl.pallas_call(
        functools.partial(_hy_in_kernel, tpb=tpb), grid=(m // tm,),
        in_specs=[x_spec, prev_spec, next_spec, mod_spec, _const_spec((1, d)), _const_spec((d, p)),
                  _const_spec((1, p)), _const_spec((8, p)), _const_spec((1, p))],
        out_specs=[x_spec, x_spec],
        out_shape=[jax.ShapeDtypeStruct((m, d), F32), jax.ShapeDtypeStruct((m, d), F32)],
        compiler_params=_cp("parallel"), name="hy_in",
    )(x, x, x, mods, pre_g.reshape(1, d), w_in, b_in.reshape(1, p), cw, conv_b.reshape(1, p))


_PI_SPLIT = (3.140625, 9.67502593994140625e-4, 1.509957990978376432e-7)
_SIN_TAYLOR = (-1.0 / 6, 1.0 / 120, -1.0 / 5040, 1.0 / 362880, -1.0 / 39916800)


def _sin(x):
    kf = jnp.round(x * (1.0 / math.pi))
    r = ((x - kf * _PI_SPLIT[0]) - kf * _PI_SPLIT[1]) - kf * _PI_SPLIT[2]
    r2 = r * r
    p = _SIN_TAYLOR[4]
    for c in _SIN_TAYLOR[3::-1]:
        p = p * r2 + c
    s = r + r * r2 * p
    odd = (kf.astype(jnp.int32) & 1) == 1
    return jnp.where(odd, -s, s)


def _hy_filter_kernel(bands_ref, w1_ref, b1_ref, fq_ref, w2_ref, b2_ref, w3_ref, dl_ref,
                      k_ref, nrm_ref, *, n, tr):
    i = pl.program_id(0)
    bwd = i >= n // tr
    row = lax.broadcasted_iota(jnp.int32, (tr, LANE), 0) + i * tr
    j = jnp.where(bwd, 2 * n - row, row).astype(F32)
    lane = lax.broadcasted_iota(jnp.int32, (tr, LANE), 1)
    t = j * (1.0 / (n - 1))
    arg = (2.0 * math.pi / n) * j * bands_ref[0:1, :] + bands_ref[1:2, :]
    z = jnp.where(lane == 0, t, jnp.where(lane <= 2 * HYENA_BANDS, _sin(arg), 0.0))
    fq = fq_ref[...]
    a = _sin(fq * (_dot_hi(z, w1_ref[...]) + b1_ref[...]))
    for k in range(w2_ref.shape[0]):
        a = _sin(fq * (_dot_hi(a, w2_ref[k]) + b2_ref[k]))
    h = _dot_x3(a, w3_ref[jnp.where(bwd, 1, 0)])
    decay = jnp.exp(-t[:, 0:1] * dl_ref[...])
    k = h * decay
    k = jnp.where(row[:, 0:1] == n, 0.0, k)
    k_ref[...] = k
    part = jnp.sum(jnp.abs(k), axis=0, keepdims=True)

    @pl.when(i == 0)
    def _():
        nrm_ref[...] = jnp.zeros_like(nrm_ref)

    nrm_ref[...] += jnp.broadcast_to(part, nrm_ref.shape)


def _hy_filter_call(n, d, f_w1, f_b1, f_freq, f_w2, f_b2, f_w3):
    fw = f_w1.shape[1]
    tr = min(512, n)
    bands_np = np.zeros((8, LANE), np.float32)
    bands_np[0, 1:1 + HYENA_BANDS] = np.linspace(1e-4, HYENA_BANDS - 1, HYENA_BANDS, dtype=np.float32)
    bands_np[0, 1 + HYENA_BANDS:1 + 2 * HYENA_BANDS] = bands_np[0, 1:1 + HYENA_BANDS]
    bands_np[1, 1:1 + HYENA_BANDS] = 0.5 * np.pi
    bands_np[1, 1 + HYENA_BANDS:1 + 2 * HYENA_BANDS] = np.pi
    w1p = jnp.zeros((LANE, fw), F32).at[:f_w1.shape[0]].set(f_w1)
    deltas = jnp.abs(jnp.linspace(math.log(HYENA_TARGET) / HYENA_SLOW, math.log(HYENA_TARGET) / HYENA_FAST,
                                  d, dtype=F32)).reshape(1, d)
    row = pl.BlockSpec((tr, d), lambda i: (i, 0))
    return pl.pallas_call(
        functools.partial(_hy_filter_kernel, n=n, tr=tr), grid=(2 * n // tr,),
        in_specs=[_const_spec((8, LANE)), _const_spec((LANE, fw)), _const_spec((1, fw)), _const_spec((1, fw)),
                  _const_spec(f_w2.shape), _const_spec((f_w2.shape[0], 1, fw)), _const_spec((2, fw, d)),
                  _const_spec((1, d))],
        out_specs=[row, _const_spec((8, d))],
        out_shape=[jax.ShapeDtypeStruct((2 * n, d), F32), jax.ShapeDtypeStruct((8, d), F32)],
        compiler_params=_cp("arbitrary"), name="hy_filter",
    )(jnp.asarray(bands_np), w1p, f_b1.reshape(1, fw), f_freq.reshape(1, fw), f_w2,
      f_b2.reshape(f_w2.shape[0], 1, fw), jnp.transpose(f_w3.reshape(fw, 2, d), (1, 0, 2)), deltas)


def _dft_cs(nf, nt, period):
    ft = (np.arange(nf)[:, None] * np.arange(nt)[None, :]) % period
    ang = 2.0 * np.pi * ft / period
    return np.cos(ang), np.sin(ang)


def _twiddle_tables(n1, n2):
    nn = n1 * n2
    f1 = jnp.arange(n1, dtype=jnp.int32)
    t2 = jnp.arange(n2, dtype=jnp.int32)
    idx = (f1[:, None] * t2[None, :]) % nn
    ang = idx.astype(F32) * (2.0 * math.pi / nn)
    shape = idx.shape + (LANE,)
    return (jnp.broadcast_to(jnp.cos(ang)[..., None], shape),
            jnp.broadcast_to(jnp.sin(ang)[..., None], shape))


KRON_R = 16
KRON_W = 256


def _cblock(mc):
    return np.block([[mc.real, -mc.imag], [mc.imag, mc.real]])


def _split_radix(n1):
    b = 16 if (n1 % 16 == 0 and n1 >= 64) else 4
    assert n1 % b == 0 and (n1 // b) % 2 == 0
    return n1 // b, b


def _slab_dft_mats(n1, a_in, a_out, sign, neg_im=False, real_in=False, scale=1.0):
    a, b = _split_radix(n1)
    r = KRON_R
    eye = np.eye(r)
    w = lambda num, den: np.exp(sign * 2j * np.pi * num / den)
    ua = np.arange(a)[:, None] * np.arange(a)[None, :]
    vb = np.arange(b)[:, None] * np.arange(b)[None, :]
    if sign < 0:
        m1 = np.kron(w(ua[:, :a_in], a) * scale, eye)
        l1 = _cblock(m1)
        if neg_im:
            l1[:, a_in * r:] *= -1.0
        if real_in:
            l1 = l1[:, :a_in * r]
        l2 = np.stack([_cblock(np.kron(w(vb, b) * w(u * np.arange(b)[None, :], n1), eye)) for u in range(a)])
    else:
        l1 = np.stack([_cblock(np.kron(w(vb, b) * w(u * np.arange(b)[:, None], n1), eye)) for u in range(a)])
        l2 = _cblock(np.kron(w(ua[:a_out, :], a) * scale, eye))
    return jnp.asarray(l1, BF16), jnp.asarray(l2, BF16)


def _kfa_kernel(*refs, nparts, a_in, a, b):
    l1_ref, l2_ref = refs[0], refs[1]
    parts = refs[2:2 + nparts]
    twc_ref, tws_ref, or_ref, oi_ref = refs[2 + nparts:]
    r = KRON_R
    reps = or_ref.shape[-1] // LANE
    l1 = l1_ref[...]
    y1 = []
    for bb in range(b):
        x = jnp.concatenate([p[0, aa * b + bb] for p in parts for aa in range(a_in)], axis=0).astype(BF16)
        y1.append(_dot(l1, x).astype(BF16))
    for u in range(a):
        x = jnp.concatenate([y1[bb][u * r:(u + 1) * r] for bb in range(b)]
                            + [y1[bb][(a + u) * r:(a + u + 1) * r] for bb in range(b)], axis=0)
        z = _dot(l2_ref[u], x)
        for v in range(b):
            f1 = u + a * v
            zr = z[v * r:(v + 1) * r]
            zi = z[(b + v) * r:(b + v + 1) * r]
            c = jnp.concatenate([twc_ref[f1]] * reps, axis=1)
            sn = jnp.concatenate([tws_ref[f1]] * reps, axis=1)
            or_ref[0, f1] = (zr * c + zi * sn).astype(or_ref.dtype)
            oi_ref[0, f1] = (zi * c - zr * sn).astype(oi_ref.dtype)


def _kfa_call(parts, nb_out, n1, n2, d, l1, l2, twc, tws):
    a, b = _split_radix(n1)
    a_in = parts[0][0].shape[1] // b
    r, w = KRON_R, min(KRON_W, d)
    in_specs = [_const_spec(l1.shape), pl.BlockSpec(l2.shape, lambda bi, j, k: (0, 0, 0), pipeline_mode=pl.Buffered(1))]
    args = [l1, l2]
    for arr, bi_fixed in parts:
        t1_in = arr.shape[1]
        if bi_fixed is None:
            in_specs.append(pl.BlockSpec((1, t1_in, r, w), lambda bi, j, k: (bi, 0, j, k)))
        else:
            in_specs.append(pl.BlockSpec((1, t1_in, r, w), lambda bi, j, k, f=bi_fixed: (f, 0, j, k)))
        args.append(arr)
    tw_spec = pl.BlockSpec((n1, r, LANE), lambda bi, j, k: (0, j, 0))
    out_spec = pl.BlockSpec((1, n1, r, w), lambda bi, j, k: (bi, 0, j, k))
    out = jax.ShapeDtypeStruct((nb_out, n1, n2, d), BF16)
    return pl.pallas_call(
        functools.partial(_kfa_kernel, nparts=len(parts), a_in=a_in, a=a, b=b),
        grid=(nb_out, n2 // r, d // w),
        in_specs=in_specs + [tw_spec, tw_spec], out_specs=[out_spec, out_spec], out_shape=[out, out],
        compiler_params=_cp("parallel", "parallel", "parallel"), name="kfa",
    )(*args, twc, tws)


def _kfc_kernel(l3_ref, l4_ref, gr_ref, gi_ref, vg_ref, g0_ref, skip_ref, o_ref, *, a, b, a_out):
    r = KRON_R
    y3 = []
    for u in range(a):
        x = jnp.concatenate([gr_ref[0, u + a * v] for v in range(b)]
                            + [gi_ref[0, u + a * v] for v in range(b)], axis=0)
        y3.append(_dot(l3_ref[u], x).astype(BF16))
    l4 = l4_ref[...]
    skip = skip_ref[...]
    for bb in range(b):
        x = jnp.concatenate([y3[u][bb * r:(bb + 1) * r] for u in range(a)]
                            + [y3[u][(b + bb) * r:(b + bb + 1) * r] for u in range(a)], axis=0)
        y = _dot(l4, x)
        for sg in range(2):
            for aa in range(a_out):
                t1 = aa * b + bb
                yb = y[(sg * a_out + aa) * r:(sg * a_out + aa + 1) * r]
                o_ref[sg, t1] = ((yb + vg_ref[sg, t1] * skip) * g0_ref[sg, t1]).astype(o_ref.dtype)


def _kfc_call(g1r, g1i, vg4, g04, skip, n1, n2, d, l3, l4):
    a, b = _split_radix(n1)
    nb, t1_out = vg4.shape[:2]
    r, w = KRON_R, min(KRON_W, d)
    slab = pl.BlockSpec((1, n1, r, w), lambda j, k: (0, 0, j, k))
    nat = pl.BlockSpec((nb, t1_out, r, w), lambda j, k: (0, 0, j, k))
    return pl.pallas_call(
        functools.partial(_kfc_kernel, a=a, b=b, a_out=t1_out // b), grid=(n2 // r, d // w),
        in_specs=[pl.BlockSpec(l3.shape, lambda j, k: (0, 0, 0), pipeline_mode=pl.Buffered(1)),
                  _const_spec(l4.shape), slab, slab, nat, nat, pl.BlockSpec((1, w), lambda j, k: (0, k))],
        out_specs=nat, out_shape=jax.ShapeDtypeStruct(vg4.shape, BF16),
        compiler_params=_cp("parallel", "parallel"), name="kfc",
    )(l3, l4, g1r, g1i, vg4, g04, skip.reshape(1, d))


def _hy_b_kernel(fm_ref, fmc_ref, xr_ref, xi_ref, fr_ref, fi_ref, sc_ref, twc_ref, tws_ref,
                 or_ref, oi_ref, *, nf, d):
    fm = fm_ref[...]
    fmc = fmc_ref[...]
    n2 = xr_ref.shape[1]
    sc = sc_ref[...]
    for s in range(nf):
        x = _dot(fm, jnp.concatenate([xr_ref[s], xi_ref[s]], axis=0))
        kk = _dot(fm, jnp.concatenate([fr_ref[s], fi_ref[s]], axis=0))
        xr, xi = x[:n2], x[n2:]
        kr, ki = kk[:n2] * sc, kk[n2:] * sc
        yr = (xr * kr - xi * ki).astype(BF16)
        yi = (xr * ki + xi * kr).astype(BF16)
        g = _dot(fmc, jnp.concatenate([yr, yi], axis=0))
        gr, gi = g[:n2], g[n2:]
        c = jnp.concatenate([twc_ref[s]] * (d // LANE), axis=1)
        sn = jnp.concatenate([tws_ref[s]] * (d // LANE), axis=1)
        or_ref[s] = (gr * c - gi * sn).astype(or_ref.dtype)
        oi_ref[s] = (gi * c + gr * sn).astype(oi_ref.dtype)


def _block_c(cs, sn, sign):
    return np.block([[cs, -sign * sn], [sign * sn, cs]])


def _hy_conv_long(vg, g0, k, nrm, skip, b, n, d):
    n2 = FFT_N2
    nn = 2 * n
    n1 = nn // n2
    rows = n // n2
    a, _ = _split_radix(n1)
    la1, la2 = _slab_dft_mats(n1, a // 2, 0, -1)
    lk1, lk2 = _slab_dft_mats(n1, a, 0, -1, real_in=True)
    lc3, lc4 = _slab_dft_mats(n1, 0, a // 2, +1)
    cs2, sn2 = _dft_cs(n2, n2, n2)
    fm_b = jnp.asarray(_block_c(cs2, sn2, -1.0), BF16)
    fm_bc = jnp.asarray(_block_c(cs2, sn2, 1.0), BF16)
    twc, tws = _twiddle_tables(n1, n2)

    kfr, kfi = _kfa_call([(k.reshape(1, n1, n2, d), 0)], 1, n1, n2, d, lk1, lk2, twc, tws)
    nf = 4 if n1 % 4 == 0 else 1
    scale = (1.0 / (nrm[0:1, :] * nn))
    slab = pl.BlockSpec((nf, n2, d), lambda j: (j, 0, 0))
    shp3 = (n1, n2, d)
    vg4 = vg.reshape(b, rows, n2, d)
    g04 = g0.reshape(b, rows, n2, d)
    x1r, x1i = _kfa_call([(vg4, 0), (vg4, 1)], 1, n1, n2, d, la1, la2, twc, tws)
    tw_slab = pl.BlockSpec((nf, n2, LANE), lambda j: (j, 0, 0))
    g1r, g1i = pl.pallas_call(
        functools.partial(_hy_b_kernel, nf=nf, d=d), grid=(n1 // nf,),
        in_specs=[_const_spec(fm_b.shape), _const_spec(fm_bc.shape), slab, slab, slab, slab, _const_spec((1, d)),
                  tw_slab, tw_slab],
        out_specs=[slab, slab],
        out_shape=[jax.ShapeDtypeStruct(shp3, BF16)] * 2,
        compiler_params=_cp("parallel"), name="hy_b",
    )(fm_b, fm_bc, x1r.reshape(shp3), x1i.reshape(shp3), kfr.reshape(shp3), kfi.reshape(shp3), scale, twc, tws)
    out = _kfc_call(g1r.reshape(1, n1, n2, d), g1i.reshape(1, n1, n2, d), vg4, g04, skip, n1, n2, d, lc3, lc4)
    return out.reshape(b * n, d)


def _hy_short_kernel(fa_ref, fk_ref, fi_ref, vg_ref, g0_ref, k_ref, nrm_ref, skip_ref, o_ref, *, n):
    z = jnp.concatenate([vg_ref[0], vg_ref[1]], axis=0)
    x = _dot_hi(fa_ref[...], z)
    kk = _dot_hi(fk_ref[...], k_ref[...])
    nn = 2 * n
    sc = 1.0 / (nrm_ref[0:1, :] * nn)
    xr, xi = x[:nn], x[nn:]
    kr, ki = kk[:nn] * sc, kk[nn:] * sc
    y = _dot_hi(fi_ref[...], jnp.concatenate([xr * kr - xi * ki, xr * ki + xi * kr], axis=0))
    skip = skip_ref[...]
    for bi in range(2):
        o_ref[bi] = ((y[bi * n:(bi + 1) * n] + vg_ref[bi] * skip) * g0_ref[bi]).astype(o_ref.dtype)


def _hy_conv_short(vg, g0, k, nrm, skip, b, n, d):
    nn = 2 * n
    cs, sn = _dft_cs(nn, n, nn)
    fa = jnp.asarray(_block_c(cs, sn, -1.0), F32)
    csk, snk = _dft_cs(nn, nn, nn)
    fk = jnp.asarray(np.concatenate([csk, -snk], axis=0), F32)
    csi, sni = _dft_cs(n, nn, nn)
    fi = jnp.asarray(_block_c(csi, sni, 1.0), F32)
    cb = 256
    col3 = pl.BlockSpec((b, n, cb), lambda j: (0, 0, j))
    vec = pl.BlockSpec((1, cb), lambda j: (0, j))
    out = pl.pallas_call(
        functools.partial(_hy_short_kernel, n=n), grid=(d // cb,),
        in_specs=[_const_spec(fa.shape), _const_spec(fk.shape), _const_spec(fi.shape), col3, col3,
                  pl.BlockSpec((nn, cb), lambda j: (0, j)), pl.BlockSpec((8, cb), lambda j: (0, j)), vec],
        out_specs=col3, out_shape=jax.ShapeDtypeStruct((b, n, d), BF16),
        compiler_params=_cp("parallel"), name="hy_short",
    )(fa, fk, fi, vg.reshape(b, n, d), g0.reshape(b, n, d), k, nrm, skip.reshape(1, d))
    return out.reshape(b * n, d)


def _hyena_layer(xl, xc, mods_l, mods_c, pre_g, fin, w_in, b_in, conv_w, conv_b, filt, skip,
                 w_out, b_out, b, n, c):
    d = xl.shape[-1]
    w_in_b = w_in.astype(BF16)
    w_out_b = w_out.astype(BF16)
    tm = min(512, n)
    g0, vg = _hy_in_call(xl, mods_l, pre_g, w_in_b, b_in, conv_w, conv_b, tm, n)
    k, nrm = _hy_filter_call(n, d, *filt)
    u_out = _hy_conv_long(vg, g0, k, nrm, skip[0], b, n, d)
    xl, (hl,) = fin(xl, u_out, mods_l, w_out_b, b_out, tm, n)

    g0c, vgc = _hy_in_call(xc, mods_c, pre_g, w_in_b, b_in, conv_w, conv_b, c, c)
    kc, nrmc = _hy_filter_call(c, d, *filt)
    u_out_c = _hy_conv_short(vgc, g0c, kc, nrmc, skip[0], b, c, d)
    xc, (hc,) = fin(xc, u_out_c, mods_c, w_out_b, b_out, c, c, is_ctx=True)
    return xl, hl, hc


def _s5_operators(lam_re, lam_im, log_dt, b_re, b_im, c_re, c_im, d_skip):
    t = S5_T
    g, ns = lam_re.shape[1], lam_re.shape[2]
    gc = b_re.shape[-1]
    gl = LANE // gc
    nblk = g // gl
    lam = lax.complex(lam_re, lam_im)
    dt = jnp.exp(log_dt)[..., None]
    lam_bar = jnp.exp(lam * dt)
    b_bar = ((lam_bar - 1.0) / lam)[..., None] * lax.complex(b_re, b_im)
    c_mat = lax.complex(c_re, c_im)
    pw = jnp.arange(t + 1, dtype=F32)
    lam_pw = jnp.exp((lam * dt)[None] * pw[:, None, None, None])
    hp = HIGHEST
    kern = jnp.einsum('dgcn,tdgn,dgne->dgtce', c_mat, lam_pw[:t], b_bar, precision=hp).real
    dsk = d_skip.reshape(g, gc)
    kt = jnp.swapaxes(kern, -1, -2)
    centre = kt[0][:, 0] + kt[1][:, 0] + jnp.eye(gc, dtype=F32)[None] * dsk[:, :, None]
    ks = jnp.concatenate([kt[1][:, 1:][:, ::-1], centre[:, None], kt[0][:, 1:]], axis=1)
    ks = jnp.transpose(ks.reshape(nblk, gl, 2 * t - 1, gc, gc), (0, 2, 1, 3, 4)).reshape(nblk, 2 * t - 1, LANE, gc)
    same_group = (np.arange(LANE)[:, None] // gc == np.arange(LANE)[None, :] // gc).astype(np.float32)
    d_tab = jnp.tile(ks, (1, 1, 1, gl)) * same_group

    def compact(zc, im_sign):
        z = jnp.concatenate([zc.real, im_sign * zc.imag], axis=-1)
        z = jnp.transpose(z, (2, 1, 0, 3, 4)).reshape(nblk, gl, t, 2, gc, 2 * ns)
        return jnp.transpose(z, (0, 2, 1, 3, 4, 5))

    b_t = jnp.swapaxes(b_bar, -1, -2)
    pf = lam_pw[:t][::-1][:, 0, :, None, :] * b_t[0][None]
    pb = lam_pw[:t][:, 1, :, None, :] * b_t[1][None]
    p_tab = compact(jnp.stack([pf, pb], axis=0), 1.0)
    qf = c_mat[0][None] * lam_pw[1:t + 1, 0][:, :, None, :]
    qb = c_mat[1][None] * lam_pw[1:t + 1][::-1][:, 1][:, :, None, :]
    q_tab = compact(jnp.stack([qf, qb], axis=0), -1.0)

    a = lam_pw[t]
    m_op, p_op, q_op = _s5_expand(d_tab, p_tab, q_tab)
    return m_op, p_op, q_op, a.real.reshape(2, g * ns), a.imag.reshape(2, g * ns)


def _s5_m_kernel(d_ref, o_ref, *, t):
    for s in range(t):
        for tt in range(t):
            o_ref[0, s * LANE:(s + 1) * LANE, tt * LANE:(tt + 1) * LANE] = d_ref[0, tt - s + t - 1].astype(o_ref.dtype)


def _s5_pq_kernel(c_ref, o_ref, *, transpose):
    t, gl, nd, gc, w = c_ref.shape[1:]
    ns = w // 2
    lane_grp = lax.broadcasted_iota(jnp.int32, (gc, gl * ns), 1) // ns
    for j in range(t):
        rows = []
        for g in range(gl):
            cols = []
            for dd in range(nd):
                piece = c_ref[0, j, g, dd]
                for ri in range(2):
                    tiled = jnp.concatenate([piece[:, ri * ns:(ri + 1) * ns]] * gl, axis=1)
                    cols.append(jnp.where(lane_grp == g, tiled, 0.0))
            rows.append(jnp.concatenate(cols, axis=1))
        blk = jnp.concatenate(rows, axis=0)
        if transpose:
            o_ref[0, :, j * LANE:(j + 1) * LANE] = blk.T.astype(o_ref.dtype)
        else:
            o_ref[0, j * LANE:(j + 1) * LANE, :] = blk.astype(o_ref.dtype)


def _s5_expand(d_tab, p_tab, q_tab):
    nblk, nlag = d_tab.shape[:2]
    t = (nlag + 1) // 2
    _, _, gl, nd, gc, w = p_tab.shape
    ncol = nd * gl * w
    whole = lambda shape: pl.BlockSpec((1,) + shape, lambda b: (b,) + (0,) * len(shape))
    m_op = pl.pallas_call(
        functools.partial(_s5_m_kernel, t=t), grid=(nblk,),
        in_specs=[whole((nlag, LANE, LANE))], out_specs=whole((t * LANE, t * LANE)),
        out_shape=jax.ShapeDtypeStruct((nblk, t * LANE, t * LANE), BF16),
        compiler_params=_cp("parallel"), name="s5_m_op",
    )(d_tab)
    tab_spec = whole((t, gl, nd, gc, w))
    p_op = pl.pallas_call(
        functools.partial(_s5_pq_kernel, transpose=False), grid=(nblk,),
        in_specs=[tab_spec], out_specs=whole((t * LANE, ncol)),
        out_shape=jax.ShapeDtypeStruct((nblk, t * LANE, ncol), BF16),
        compiler_params=_cp("parallel"), name="s5_p_op",
    )(p_tab)
    q_op = pl.pallas_call(
        functools.partial(_s5_pq_kernel, transpose=True), grid=(nblk,),
        in_specs=[tab_spec], out_specs=whole((ncol, t * LANE)),
        out_shape=jax.ShapeDtypeStruct((nblk, ncol, t * LANE), BF16),
        compiler_params=_cp("parallel"), name="s5_q_op",
    )(q_tab)
    return m_op, p_op, q_op


def _s5_sum_kernel(*refs, t):
    u_refs = refs[:t]
    p_ref = refs[t]
    outs = refs[t + 1:]
    u = jnp.concatenate([r[...] for r in u_refs], axis=1)
    res = _dot(u, p_ref[0])
    w = res.shape[1] // len(outs)
    for i, o in enumerate(outs):
        o[...] = res[:, i * w:(i + 1) * w]


def _s5_sum_call(h, p_op, rb):
    rows = h.shape[0]
    t = S5_T
    nblk = p_op.shape[0]
    w = p_op.shape[2] // 4
    u_specs = [pl.BlockSpec((rb, LANE), lambda gb, r, s=s: (r, s * nblk + gb)) for s in range(t)]
    out_spec = pl.BlockSpec((rb, w), lambda gb, r: (r, gb))
    return pl.pallas_call(
        functools.partial(_s5_sum_kernel, t=t), grid=(nblk, rows // rb),
        in_specs=u_specs + [pl.BlockSpec((1,) + p_op.shape[1:], lambda gb, r: (gb, 0, 0))],
        out_specs=[out_spec] * 4,
        out_shape=[jax.ShapeDtypeStruct((rows, nblk * w), F32)] * 4,
        compiler_params=_cp("parallel", "parallel"), name="s5_sum",
    )(*([h] * t), p_op)


def _s5_rec_kernel(sr_ref, si_ref, ar_ref, ai_ref, h0r_ref, h0i_ref, hr_ref, hi_ref, fr_ref, fi_ref,
                   cr, ci, *, kb, reverse):
    @pl.when(pl.program_id(1) == 0)
    def _():
        cr[...] = h0r_ref[0]
        ci[...] = h0i_ref[0]

    ar, ai = ar_ref[...], ai_ref[...]

    def body(i, carry):
        hr, hi = carry
        k = kb - 1 - i if reverse else i
        hr_ref[pl.ds(k, 1), :] = hr
        hi_ref[pl.ds(k, 1), :] = hi
        nr = ar * hr - ai * hi + sr_ref[pl.ds(k, 1), :]
        ni = ar * hi + ai * hr + si_ref[pl.ds(k, 1), :]
        return nr, ni

    hr, hi = lax.fori_loop(0, kb, body, (cr[...], ci[...]))
    cr[...] = hr
    ci[...] = hi
    fr_ref[0] = hr
    fi_ref[0] = hi


def _s5_rec_call(sr, si, ar, ai, h0r, h0i, nb_batch, reverse):
    rows, w = sr.shape
    nk = rows // nb_batch
    kb = min(64, nk)
    nb = nk // kb
    blk = (lambda bi, i: (bi * nb + nb - 1 - i, 0)) if reverse else (lambda bi, i: (bi * nb + i, 0))
    row_spec = pl.BlockSpec((kb, w), blk)
    vec = _const_spec((1, w))
    st = pl.BlockSpec((1, 1, w), lambda bi, i: (bi, 0, 0))
    return pl.pallas_call(
        functools.partial(_s5_rec_kernel, kb=kb, reverse=reverse), grid=(nb_batch, nb),
        in_specs=[row_spec, row_spec, vec, vec, st, st],
        out_specs=[row_spec, row_spec, st, st],
        out_shape=[jax.ShapeDtypeStruct((rows, w), F32)] * 2 + [jax.ShapeDtypeStruct((nb_batch, 1, w), F32)] * 2,
        scratch_shapes=[pltpu.VMEM((1, w), F32), pltpu.VMEM((1, w), F32)],
        compiler_params=_cp("parallel", "arbitrary"), name="s5_rec",
    )(sr, si, ar, ai, h0r, h0i)


def _s5_out_kernel(*refs, t):
    u_refs = refs[:t]
    h_refs = refs[t:t + 4]
    m_ref, q_ref, o_ref, tok_scr = refs[t + 4:]
    u = jnp.concatenate([r[...] for r in u_refs], axis=1)
    hcat = jnp.concatenate([r[...].astype(BF16) for r in h_refs], axis=1)
    y = _dot(u, m_ref[0]) + _dot(hcat, q_ref[0])
    g = 0.5 * y * (1.0 + lax.erf(y * (2.0 ** -0.5)))
    rb = g.shape[0]
    for j in range(t):
        tok_scr[pl.ds(j, rb, stride=t), :] = g[:, j * LANE:(j + 1) * LANE]
    o_ref[...] = tok_scr[...].astype(o_ref.dtype)


def _s5_out_call(h, states, m_op, q_op, rb):
    rows = h.shape[0]
    t = S5_T
    nblk = m_op.shape[0]
    w = q_op.shape[1] // 4
    u_specs = [pl.BlockSpec((rb, LANE), lambda gb, r, s=s: (r, s * nblk + gb)) for s in range(t)]
    return pl.pallas_call(
        functools.partial(_s5_out_kernel, t=t), grid=(nblk, rows // rb),
        in_specs=u_specs + [pl.BlockSpec((rb, w), lambda gb, r: (r, gb))] * 4
        + [pl.BlockSpec((1,) + m_op.shape[1:], lambda gb, r: (gb, 0, 0)),
           pl.BlockSpec((1,) + q_op.shape[1:], lambda gb, r: (gb, 0, 0))],
        out_specs=pl.BlockSpec((rb * t, LANE), lambda gb, r: (r, gb)),
        out_shape=jax.ShapeDtypeStruct((rows * t, nblk * LANE), BF16),
        scratch_shapes=[pltpu.VMEM((rb * t, LANE), F32)],
        compiler_params=_cp("parallel", "parallel"), name="s5_out",
    )(*([h] * t), *states, m_op, q_op)


def _s5_layer(xl, hl, hc, mods_l, fin, lam_re, lam_im, log_dt, b_re, b_im, c_re, c_im,
              d_skip, w_glu, b_glu, b, n, c):
    t = S5_T
    m_op, p_op, q_op, a_re, a_im = _s5_operators(lam_re, lam_im, log_dt, b_re, b_im, c_re, c_im, d_skip)
    w = a_re.shape[-1]

    def scan(h, init):
        sfr, sfi, sbr, sbi = _s5_sum_call(h, p_op, min(512, h.shape[0]))
        hfr, hfi, ffr, ffi = _s5_rec_call(sfr, sfi, a_re[0:1], a_im[0:1], init[0], init[1], b, False)
        hbr, hbi, fbr, fbi = _s5_rec_call(sbr, sbi, a_re[1:2], a_im[1:2], init[2], init[3], b, True)
        return (hfr, hfi, hbr, hbi), (ffr, ffi, fbr, fbi)

    zeros = jnp.zeros((b, 1, w), F32)
    _, ctx_final = scan(hc, (zeros,) * 4)
    states, _ = scan(hl, ctx_final)
    nk = n // t
    g_nat = _s5_out_call(hl, states, m_op, q_op, min(512, b * nk))
    return fin(xl, g_nat, mods_l, w_glu.astype(BF16), b_glu, min(512, n), n, glu=True)


def _fnet_channel_mats():
    cc, sc = _dft_cs(FNET_GC, FNET_GC, FNET_GC)
    return jnp.asarray(np.concatenate([cc, sc], axis=1) / np.sqrt(FNET_GC), BF16)


def _fn_c_kernel(l5_ref, l6_ref, xr_ref, xi_ref, o_ref, *, nh):
    r = KRON_R
    n2 = nh * r
    l5 = l5_ref[...]
    y5 = [_dot(l5, jnp.concatenate([xr_ref[0, f], xi_ref[0, f]], axis=0)).astype(BF16) for f in range(r)]
    l6 = l6_ref[...]
    for p in range(nh):
        x = jnp.concatenate([y5[f][p * r:(p + 1) * r] for f in range(r)]
                            + [y5[f][n2 + p * r:n2 + (p + 1) * r] for f in range(r)], axis=0)
        out = _dot(l6, x)
        for q in range(r):
            o_ref[0, p + nh * q] = out[q * r:(q + 1) * r].astype(o_ref.dtype)


def _fnet_layer(xl, am, bm, mods_l, fin, w_o, b_o, b, n, d):
    n2 = FFT_N2
    n1 = n // n2
    tm = min(512, n)
    a, _ = _split_radix(n1)
    l1, l2 = _slab_dft_mats(n1, a, 0, -1, neg_im=True, scale=1.0 / np.sqrt(n))
    twc, tws = _twiddle_tables(n1, n2)
    a4, b4 = am.reshape(b, n1, n2, d), bm.reshape(b, n1, n2, d)
    xr, xi = _kfa_call([(a4, None), (b4, None)], b, n1, n2, d, l1, l2, twc, tws)

    r, w = KRON_R, min(KRON_W, d)
    nh = n2 // r
    assert n1 % r == 0 and n2 % r == 0
    m5 = np.zeros((n2, n2), np.complex128)
    m6 = np.zeros((r * r, r * r), np.complex128)
    for s in range(r):
        for p in range(nh):
            for h in range(nh):
                m5[p * r + s, h * r + s] = np.exp(-2j * np.pi * (p * h / nh + p * s / n2))
        for q in range(r):
            for f in range(r):
                m6[q * r + f, f * r + s] = np.exp(-2j * np.pi * q * s / r)
    l5 = jnp.asarray(_cblock(m5), BF16)
    l6 = jnp.asarray(np.concatenate([m6.real, -m6.imag], axis=1), BF16)
    grp = pl.BlockSpec((1, r, n2, w), lambda bi, fh, k: (bi, fh, 0, k))
    y = pl.pallas_call(
        functools.partial(_fn_c_kernel, nh=nh), grid=(b, n1 // r, d // w),
        in_specs=[_const_spec(l5.shape), _const_spec(l6.shape), grp, grp],
        out_specs=pl.BlockSpec((1, n2, None, r, w), lambda bi, fh, k: (bi, 0, fh, 0, k)),
        out_shape=jax.ShapeDtypeStruct((b, n2, n1 // r, r, d), BF16),
        compiler_params=_cp("parallel", "parallel", "parallel"), name="fn_c",
    )(l5, l6, xr, xi)
    return fin(xl, y.reshape(b * n, d), mods_l, w_o.astype(BF16), b_o, tm, n)


def kernel(x, c, ctx, c_ctx, mod_w, mod_b, mix_pre_g, mix_post_g, ffn_pre_g, ffn_post_g, ffn_w13, ffn_w2,
           mla_w_in, mla_q_norm_g, mla_kv_norm_g, mla_w_uq, mla_w_ukv, mla_w_o,
           hy_w_in, hy_b_in, hy_conv_w, hy_conv_b, hy_f_w1, hy_f_b1, hy_f_freq, hy_f_w2, hy_f_b2, hy_f_w3,
           hy_skip, hy_w_out, hy_b_out,
           s5_lambda_re, s5_lambda_im, s5_log_dt, s5_b_re, s5_b_im, s5_c_re, s5_c_im, s5_d, s5_w_glu, s5_b_glu,
           fn_w_o, fn_b_o):
    b, n, d = x.shape
    cl = ctx.shape[1]
    depth = mod_w.shape[0]
    assert b == 2 and depth == 4, "two batches ride one complex transform; one layer per mixer"
    mods = _mods(c, c_ctx, mod_w, mod_b)
    xl = x.reshape(b * n, d)
    xc = ctx.reshape(b * cl, d)

    w13_all = ffn_w13.astype(BF16)
    w2_all = ffn_w2.astype(BF16)

    def mods_c(i):
        return jnp.broadcast_to(mods[i, 2:3], (b, 8, d))

    def finisher(i, emit_kind=None, cs=None):
        def fin(x_, y_, mods_, wm, bm, tm_, rows_per_batch, glu=False, is_ctx=False):
            emit = None
            if emit_kind is not None:
                emit = (emit_kind, mods_c(i + 1) if is_ctx else mods[i + 1, 0:2], mix_pre_g[i + 1], cs)
            return _mix_ffn_call(x_, y_, mods_, wm, bm, mix_post_g[i], ffn_pre_g[i], ffn_post_g[i],
                                 w13_all, w2_all, i, tm_, rows_per_batch, glu, emit)
        return fin

    xl, xc = _mla_layer(xl, xc, mods[0, 0:2], mods_c(0), mix_pre_g[0], finisher(0), mla_w_in[0],
                        mla_q_norm_g[0], mla_kv_norm_g[0], mla_w_uq[0], mla_w_ukv[0], mla_w_o[0], b, n, cl)
    filt = (hy_f_w1[0], hy_f_b1[0], hy_f_freq[0], hy_f_w2[0], hy_f_b2[0], hy_f_w3[0])
    xl, hl, hc = _hyena_layer(xl, xc, mods[1, 0:2], mods_c(1), mix_pre_g[1], finisher(1, "chunks"), hy_w_in[0],
                              hy_b_in[0], hy_conv_w[0], hy_conv_b[0], filt, hy_skip[0], hy_w_out[0], hy_b_out[0],
                              b, n, cl)
    xl, (am, bm) = _s5_layer(xl, hl, hc, mods[2, 0:2], finisher(2, "fnet", _fnet_channel_mats()), s5_lambda_re[0],
                             s5_lambda_im[0], s5_log_dt[0], s5_b_re[0], s5_b_im[0], s5_c_re[0], s5_c_im[0],
                             s5_d[0], s5_w_glu[0], s5_b_glu[0], b, n, cl)
    xl = _fnet_layer(xl, am, bm, mods[3, 0:2], finisher(3), fn_w_o[0], fn_b_o[0], b, n, d)
    return xl.reshape(b, n, d)
```

```python
import functools
import math

import numpy as np
import jax
import jax.numpy as jnp
from jax import lax
from jax.experimental import pallas as pl
from jax.experimental.pallas import tpu as pltpu

F32 = jnp.float32
BF16 = jnp.bfloat16
NORM_EPS = 1e-6
LANE = 128
MXU_COLS = 256
ROW_TILE = 512
VMEM_LIMIT = 56 * 1024 * 1024
HIGHEST = lax.Precision.HIGHEST

GRID_W = 64
ROPE_THETA = 10000.0
MLA_HEADS = 8
MLA_NOPE = 128
MLA_ROPE = 64
MLA_V = 128
MLA_VT = MLA_V + 16
MLA_QK = MLA_NOPE + 2 * MLA_ROPE
HYENA_BANDS = 16
HYENA_TARGET = 1e-2
HYENA_FAST = 0.3
HYENA_SLOW = 1.5
S5_T = 16
FNET_GC = 128
FFT_N2 = 128


def _cp(*sem):
    return pltpu.CompilerParams(dimension_semantics=sem, vmem_limit_bytes=VMEM_LIMIT)


def _dot(a, b):
    return jnp.dot(a, b, preferred_element_type=F32)


def _dot_hi(a, b):
    return jnp.dot(a, b, preferred_element_type=F32, precision=HIGHEST)


def _dot_x3(a, b):
    a_hi = a.astype(BF16)
    b_hi = b.astype(BF16)
    a_lo = (a - a_hi.astype(F32)).astype(BF16)
    b_lo = (b - b_hi.astype(F32)).astype(BF16)
    return _dot(a_hi, b_hi) + (_dot(a_hi, b_lo) + _dot(a_lo, b_hi))


def _rms(x, g):
    ms = jnp.mean(x * x, axis=-1, keepdims=True)
    return x * lax.rsqrt(ms + NORM_EPS) * g


def _normmod(x, g, shift, scale):
    return _rms(x, g) * (1.0 + scale) + shift


def _const_spec(shape):
    nd = len(shape)
    return pl.BlockSpec(shape, lambda *_: (0,) * nd)


def _mods_kernel(st_ref, w_ref, b_ref, o_ref):
    st = st_ref[...]
    st = st * jax.nn.sigmoid(st)
    w = w_ref[0]
    rows = [jnp.sum(st[:, r:r + 1] * w, axis=0, keepdims=True) for r in range(3)]
    rows.append(jnp.zeros((5, w.shape[1]), F32))
    o_ref[0] = jnp.concatenate(rows, axis=0) + b_ref[0]


def _mods(c, c_ctx, mod_w, mod_b):
    depth, d, n6 = mod_w.shape
    st = jnp.zeros((d, 8), F32).at[:, 0:2].set(c.T).at[:, 2].set(c_ctx)
    tn = 1024
    out = pl.pallas_call(
        _mods_kernel,
        grid=(depth, n6 // tn),
        in_specs=[_const_spec((d, 8)),
                  pl.BlockSpec((1, d, tn), lambda i, j: (i, 0, j)),
                  pl.BlockSpec((1, 1, tn), lambda i, j: (i, 0, j))],
        out_specs=pl.BlockSpec((1, 8, tn), lambda i, j: (i, 0, j)),
        out_shape=jax.ShapeDtypeStruct((depth, 8, n6), F32),
        compiler_params=_cp("parallel", "parallel"),
        name="mods",
    )(st, mod_w, mod_b.reshape(depth, 1, n6))
    m = out[:, :3].reshape(depth, 3, n6 // d, d)
    return jnp.pad(m, ((0, 0), (0, 0), (0, 8 - n6 // d), (0, 0)))


def _row_specs(tm, d, tpb):
    x_spec = pl.BlockSpec((tm, d), lambda i: (i, 0))
    mod_spec = pl.BlockSpec((1, 8, d), lambda i: (i // tpb, 0, 0))
    return x_spec, mod_spec


def _channel_dft(h, cs):
    gc = FNET_GC
    ab = [_dot(h[:, k * gc:(k + 1) * gc], cs) for k in range(h.shape[1] // gc)]
    return (jnp.concatenate([z[:, :gc] for z in ab], axis=1), jnp.concatenate([z[:, gc:] for z in ab], axis=1))


def _mix_ffn_kernel(x_ref, y_ref, mod_ref, wm_ref, bm_ref, gm_ref, pre_ref, post_ref, w13_ref, w2_ref, *rest,
                    f, cuts, glu, emit):
    if emit == "chunks":
        nmod_ref, ng_ref, o_ref, h_ref, tok_scr = rest
    elif emit == "fnet":
        nmod_ref, ng_ref, cs_ref, o_ref, a_ref, b_ref = rest
    else:
        (o_ref,) = rest
    z = _dot(y_ref[...].astype(BF16), wm_ref[...]) + bm_ref[...]
    if glu:
        d = o_ref.shape[-1]
        z = z[:, :d] * jax.nn.sigmoid(z[:, d:])
    x = x_ref[...] + mod_ref[0, 2:3, :] * _rms(z, gm_ref[...])
    h = _normmod(x, pre_ref[...], mod_ref[0, 3:4, :], mod_ref[0, 4:5, :]).astype(BF16)
    acc = None
    for lo, hi in zip(cuts[:-1], cuts[1:]):
        a = _dot(h, w13_ref[:, lo:hi])
        b = _dot(h, w13_ref[:, f + lo:f + hi])
        gact = (a * jax.nn.sigmoid(a) * b).astype(BF16)
        part = _dot(gact, w2_ref[lo:hi, :])
        acc = part if acc is None else acc + part
    xo = x + mod_ref[0, 5:6, :] * _rms(acc, post_ref[...])
    o_ref[...] = xo
    if emit:
        hn = _normmod(xo, ng_ref[...], nmod_ref[0, 0:1, :], nmod_ref[0, 1:2, :])
        if emit == "chunks":
            tm, d = hn.shape
            for lt in range(d // LANE):
                tok_scr[lt * tm:(lt + 1) * tm, :] = hn[:, lt * LANE:(lt + 1) * LANE]
            for j in range(S5_T):
                for lt in range(d // LANE):
                    rows = tok_scr[pl.ds(lt * tm + j, tm // S5_T, stride=S5_T), :]
                    h_ref[:, j * d + lt * LANE:j * d + (lt + 1) * LANE] = rows.astype(h_ref.dtype)
        else:
            a, b = _channel_dft(hn.astype(BF16), cs_ref[...])
            a_ref[...] = a.astype(a_ref.dtype)
            b_ref[...] = b.astype(b_ref.dtype)


def _mix_ffn_call(x, y, mods, wm, bm, mix_post_g, ffn_pre_g, ffn_post_g, w13, w2, layer, tm, rows_per_batch,
                  glu=False, emit=None):
    m, d = x.shape
    k, nm = wm.shape
    f = w2.shape[1]
    cuts = tuple(range(0, f, 6 * MXU_COLS)) + (f,)
    x_spec, mod_spec = _row_specs(tm, d, rows_per_batch // tm)
    once = lambda shape: pl.BlockSpec(shape, lambda i: (0, 0), pipeline_mode=pl.Buffered(1))
    of_layer = lambda shape: pl.BlockSpec((None,) + shape, lambda i: (layer, 0, 0), pipeline_mode=pl.Buffered(1))
    in_specs = [x_spec, pl.BlockSpec((tm, k), lambda i: (i, 0)), mod_spec,
                once((k, nm)), _const_spec((1, nm)), _const_spec((1, d)), _const_spec((1, d)),
                _const_spec((1, d)), of_layer((d, 2 * f)), of_layer((f, d))]
    args = [x, y, mods, wm, bm.reshape(1, nm), mix_post_g.reshape(1, d), ffn_pre_g.reshape(1, d),
            ffn_post_g.reshape(1, d), w13, w2]
    out_specs, out_shape = [x_spec], [jax.ShapeDtypeStruct((m, d), F32)]
    kind = None
    scratch = []
    if emit is not None:
        kind, next_mods, next_g, cs = emit
        in_specs += [mod_spec, _const_spec((1, d))]
        args += [next_mods, next_g.reshape(1, d)]
        if kind == "fnet":
            in_specs.append(_const_spec(cs.shape))
            args.append(cs)
            out_specs += [x_spec] * 2
            out_shape += [jax.ShapeDtypeStruct((m, d), BF16)] * 2
        else:
            out_specs.append(pl.BlockSpec((tm // S5_T, S5_T * d), lambda i: (i, 0)))
            out_shape.append(jax.ShapeDtypeStruct((m // S5_T, S5_T * d), BF16))
            scratch.append(pltpu.VMEM((tm * (d // LANE), LANE), F32))
    outs = pl.pallas_call(
        functools.partial(_mix_ffn_kernel, f=f, cuts=cuts, glu=glu, emit=kind), grid=(m // tm,),
        in_specs=in_specs, out_specs=out_specs, out_shape=out_shape, scratch_shapes=scratch,
        compiler_params=_cp("parallel"), name="mix_ffn",
    )(*args)
    return outs[0] if emit is None else (outs[0], tuple(outs[1:]))


def _mla_proj_kernel(x_ref, mod_ref, g_ref, w_ref, qg_ref, kvg_ref, wq_ref, wkv_ref, cp_ref, sp_ref,
                     q_ref, k_ref, vt_ref, *, ql, kvl, qscale, positional):
    h = _normmod(x_ref[...], g_ref[...], mod_ref[0, 0:1, :], mod_ref[0, 1:2, :])
    z = _dot(h.astype(BF16), w_ref[...])
    qn = _rms(z[:, :ql], qg_ref[...]).astype(BF16)
    cn = _rms(z[:, ql:ql + kvl], kvg_ref[...]).astype(BF16)
    tk = z.shape[0]
    low = lax.broadcasted_iota(jnp.int32, (tk, 2 * MLA_ROPE), 1) < MLA_ROPE
    if positional:
        cpf = jnp.concatenate([cp_ref[...]] * 2, axis=1)
        spf = jnp.concatenate([sp_ref[...]] * 2, axis=1)

    def rope_slots(pair, keep_raw):
        swapped = pltpu.roll(pair, MLA_ROPE, 1)
        raw = jnp.where(low, 0.0, swapped) if keep_raw else None
        if not positional:
            return raw
        rot = jnp.where(low, pair * cpf + swapped * spf, 0.0)
        return rot + raw if keep_raw else rot

    kr = rope_slots(z[:, ql + kvl:], keep_raw=not positional).astype(BF16)

    zkv = _dot(cn, wkv_ref[...])
    ones_blk = (lax.broadcasted_iota(jnp.int32, (MLA_VT - MLA_V, tk), 0) == 0).astype(BF16)
    for hd in range(MLA_HEADS):
        base = hd * (MLA_NOPE + MLA_V)
        k_ref[0, hd, 0, :, 0:MLA_NOPE] = zkv[:, base:base + MLA_NOPE].astype(BF16)
        k_ref[0, hd, 0, :, MLA_NOPE:MLA_QK] = kr
        vt_ref[0, hd, 0, 0:MLA_V, :] = zkv[:, base + MLA_NOPE:base + MLA_NOPE + MLA_V].T.astype(BF16)
        vt_ref[0, hd, 0, MLA_V:MLA_VT, :] = ones_blk

    zq = _dot(qn, wq_ref[...])
    for hd in range(MLA_HEADS):
        base = hd * MLA_QK
        rp = rope_slots(zq[:, base + MLA_NOPE:base + MLA_QK], keep_raw=True)
        qcat = jnp.concatenate([zq[:, base:base + MLA_NOPE], rp], axis=1) * qscale
        q_ref[0, hd] = qcat.T.astype(BF16)


def _flash_kernel(q_ref, kc_ref, vc_ref, *rest, n_lat):
    if n_lat:
        kl_ref, vl_ref, o_ref, s_scr, acc_scr = rest
    else:
        o_ref, acc_scr = rest
    qt = q_ref[0, 0]

    def qk(k, slot):
        s = _dot(k, qt)
        s_scr[slot] = s
        return jnp.max(s, axis=0, keepdims=True)

    def sm_pv(slot, vt, m, mx):
        m_new = jnp.maximum(m, mx)
        alpha = jnp.exp2(m - m_new)
        p = jnp.exp2(s_scr[slot] - m_new).astype(BF16)
        acc_scr[...] = alpha * acc_scr[...] + _dot(vt, p)
        return m_new

    sc = _dot(kc_ref[0, 0, 0], qt)
    if n_lat:
        mx = qk(kl_ref[0, 0, 0], 0)
    m = jnp.max(sc, axis=0, keepdims=True)
    acc_scr[...] = _dot(vc_ref[0, 0, 0], jnp.exp2(sc - m).astype(BF16))
    if n_lat:

        per = next(p for p in (16, 8, 4, 2) if n_lat % p == 0)

        def body(i, carry):
            m, mx_cur = carry
            c = per * i
            for u in range(per):
                mx_next = qk(kl_ref[0, 0, jnp.minimum(c + u + 1, n_lat - 1)], (u + 1) % 2)
                m = sm_pv(u % 2, vl_ref[0, 0, c + u], m, mx_cur)
                mx_cur = mx_next
            return m, mx_cur

        lax.fori_loop(0, n_lat // per, body, (m, mx))
    acc = acc_scr[...]
    o_ref[0] = (acc[0:MLA_V] / acc[MLA_V:MLA_V + 1]).T.astype(o_ref.dtype)


def _rope_tables(n_lat):
    rows = n_lat // GRID_W
    row = jnp.repeat(jnp.arange(rows, dtype=F32), GRID_W)
    col = jnp.tile(jnp.arange(GRID_W, dtype=F32), rows)
    axis_dim = MLA_ROPE // 2
    inv_freq = 1.0 / (ROPE_THETA ** (jnp.arange(0, axis_dim, 2, dtype=F32) / axis_dim))
    ang_r = row[:, None] * inv_freq
    ang_c = col[:, None] * inv_freq
    cr, sr, cc, sc = jnp.cos(ang_r), jnp.sin(ang_r), jnp.cos(ang_c), jnp.sin(ang_c)
    cp = jnp.concatenate([cr, cr, cc, cc], axis=-1)
    sp = jnp.concatenate([-sr, sr, -sc, sc], axis=-1)
    return cp, sp


_ROPE_SWAP = np.concatenate([np.arange(16, 32), np.arange(0, 16), np.arange(48, 64), np.arange(32, 48)])


def _mla_side(x, mods, pre_g, w_in_ext, q_g, kv_g, w_uq_ext, w_ukv, tabs, b, n, tk, positional):
    m, d = x.shape
    ql, kvl = q_g.shape[-1], kv_g.shape[-1]
    nc = n // tk
    x_spec, mod_spec = _row_specs(tk, d, nc)
    tab_spec = pl.BlockSpec((tk, MLA_ROPE), lambda i: (i % nc, 0))
    qscale = (MLA_NOPE + MLA_ROPE) ** -0.5 * math.log2(math.e)
    return pl.pallas_call(
        functools.partial(_mla_proj_kernel, ql=ql, kvl=kvl, qscale=qscale, positional=positional),
        grid=(m // tk,),
        in_specs=[x_spec, mod_spec, _const_spec((1, d)), _const_spec(w_in_ext.shape),
                  _const_spec((1, ql)), _const_spec((1, kvl)), _const_spec(w_uq_ext.shape),
                  _const_spec(w_ukv.shape), tab_spec, tab_spec],
        out_specs=[pl.BlockSpec((1, MLA_HEADS, MLA_QK, tk), lambda i: (i // nc, 0, 0, i % nc)),
                   pl.BlockSpec((1, MLA_HEADS, 1, tk, MLA_QK), lambda i: (i // nc, 0, i % nc, 0, 0)),
                   pl.BlockSpec((1, MLA_HEADS, 1, MLA_VT, tk), lambda i: (i // nc, 0, i % nc, 0, 0))],
        out_shape=[jax.ShapeDtypeStruct((b, MLA_HEADS, MLA_QK, n), BF16),
                   jax.ShapeDtypeStruct((b, MLA_HEADS, nc, tk, MLA_QK), BF16),
                   jax.ShapeDtypeStruct((b, MLA_HEADS, nc, MLA_VT, tk), BF16)],
        compiler_params=_cp("parallel"), name="mla_proj",
    )(x, mods, pre_g.reshape(1, d), w_in_ext, q_g.reshape(1, ql), kv_g.reshape(1, kvl), w_uq_ext, w_ukv, *tabs)


def _flash_call(qt, kc, vtc, kl, vtl, tq):
    b, hh, _, n = qt.shape
    c = kc.shape[-2]
    n_lat = 0 if kl is None else kl.shape[2]
    in_specs = [pl.BlockSpec((1, 1, MLA_QK, tq), lambda bi, h, i: (bi, h, 0, i)),
                pl.BlockSpec((1, 1, 1, c, MLA_QK), lambda bi, h, i: (bi, h, 0, 0, 0)),
                pl.BlockSpec((1, 1, 1, MLA_VT, c), lambda bi, h, i: (bi, h, 0, 0, 0))]
    args = [qt, kc, vtc]
    scratch = [pltpu.VMEM((MLA_VT, tq), F32)]
    if n_lat:
        assert n_lat % 2 == 0, "latent key chunks are consumed in pairs"
        tk = kl.shape[-2]
        in_specs += [pl.BlockSpec((1, 1, n_lat, tk, MLA_QK), lambda bi, h, i: (bi, h, 0, 0, 0)),
                     pl.BlockSpec((1, 1, n_lat, MLA_VT, tk), lambda bi, h, i: (bi, h, 0, 0, 0))]
        args += [kl, vtl]
        scratch = [pltpu.VMEM((2, tk, tq), F32)] + scratch
    return pl.pallas_call(
        functools.partial(_flash_kernel, n_lat=n_lat), grid=(b, hh, n // tq),
        in_specs=in_specs,
        out_specs=pl.BlockSpec((1, tq, MLA_V), lambda bi, h, i: (bi, i, h)),
        out_shape=jax.ShapeDtypeStruct((b, n, hh * MLA_V), BF16),
        scratch_shapes=scratch,
        compiler_params=_cp("parallel", "parallel", "arbitrary"), name="flash",
    )(*args)


def _mla_layer(xl, xc, mods_l, mods_c, pre_g, fin, w_in, q_g, kv_g, w_uq, w_ukv, w_o, b, n, c):
    d = xl.shape[-1]
    ql, kvl = q_g.shape[-1], kv_g.shape[-1]
    hh = MLA_HEADS
    rope_cols = w_in[:, ql + kvl:]
    w_in_ext = jnp.concatenate([w_in, rope_cols[:, _ROPE_SWAP]], axis=1).astype(BF16)
    wq = w_uq.reshape(ql, hh, MLA_NOPE + MLA_ROPE)
    w_uq_ext = jnp.concatenate([wq, wq[:, :, MLA_NOPE:][:, :, _ROPE_SWAP]], axis=-1)
    w_uq_ext = w_uq_ext.reshape(ql, hh * MLA_QK).astype(BF16)
    w_ukv_b = w_ukv.astype(BF16)
    w_o_b = w_o.astype(BF16)

    tabs = _rope_tables(n)

    tm_l = min(ROW_TILE, n)
    tk_l = min(ROW_TILE, n // 2)
    ql_, kl, vtl = _mla_side(xl, mods_l, pre_g, w_in_ext, q_g, kv_g, w_uq_ext, w_ukv_b, tabs, b, n, tk_l, True)
    qc_, kc, vtc = _mla_side(xc, mods_c, pre_g, w_in_ext, q_g, kv_g, w_uq_ext, w_ukv_b, tabs, b, c, c, False)
    o_lat = _flash_call(ql_, kc, vtc, kl, vtl, min(2 * ROW_TILE, n)).reshape(b * n, hh * MLA_V)
    o_ctx = _flash_call(qc_, kc, vtc, None, None, c).reshape(b * c, hh * MLA_V)
    zb = jnp.zeros((d,), F32)
    xl = fin(xl, o_lat, mods_l, w_o_b, zb, tm_l, n)
    xc = fin(xc, o_ctx, mods_c, w_o_b, zb, c, c)
    return xl, xc


def _hy_in_kernel(x_ref, xp_ref, xn_ref, mod_ref, g_ref, w_ref, b_ref, cw_ref, cb_ref,
                  g0_ref, vg_ref, *, tpb):
    i = pl.program_id(0)
    g = g_ref[...]
    shift, scale = mod_ref[0, 0:1, :], mod_ref[0, 1:2, :]
    xcat = jnp.concatenate([xp_ref[...], x_ref[...], xn_ref[...]], axis=0)
    hcat = _normmod(xcat, g, shift, scale).astype(BF16)
    tm = x_ref.shape[0]
    d = g0_ref.shape[-1]
    first = (i % tpb) == 0
    last = (i % tpb) == tpb - 1
    ridx = lax.broadcasted_iota(jnp.int32, (tm, 1), 0)

    def conv_part(c):
        cols = slice(c * d, (c + 1) * d)
        ucat = _dot(hcat, w_ref[:, cols]) + b_ref[:, cols]
        u = ucat[8:tm + 8]
        prev_row = jnp.where(first, 0.0, ucat[7:8, :])
        next_row = jnp.where(last, 0.0, ucat[tm + 8:tm + 9, :])
        dn = jnp.where(ridx == 0, prev_row, pltpu.roll(u, 1, 0))
        upw = jnp.where(ridx == tm - 1, next_row, pltpu.roll(u, tm - 1, 0))
        return cb_ref[:, cols] + dn * cw_ref[0:1, cols] + u * cw_ref[1:2, cols] + upw * cw_ref[2:3, cols]

    g0_ref[...] = conv_part(0)
    vg_ref[...] = conv_part(2) * conv_part(1)


def _hy_in_call(x, mods, pre_g, w_in, b_in, conv_w, conv_b, tm, n):
    m, d = x.shape
    p = w_in.shape[1]
    tpb = n // tm
    x_spec, mod_spec = _row_specs(tm, d, tpb)
    r8 = tm // 8
    nb8 = m // 8
    prev_spec = pl.BlockSpec((8, d), lambda i: (jnp.maximum(i * r8 - 1, 0), 0))
    next_spec = pl.BlockSpec((8, d), lambda i: (jnp.minimum((i + 1) * r8, nb8 - 1), 0))
    cw = jnp.pad(conv_w, ((0, 8 - conv_w.shape[0]), (0, 0)))
    return pl.pallas_call(
        functools.partial(_hy_in_kernel, tpb=tpb), grid=(m // tm,),
        in_specs=[x_spec, prev_spec, next_spec, mod_spec, _const_spec((1, d)), _const_spec((d, p)),
                  _const_spec((1, p)), _const_spec((8, p)), _const_spec((1, p))],
        out_specs=[x_spec, x_spec],
        out_shape=[jax.ShapeDtypeStruct((m, d), F32), jax.ShapeDtypeStruct((m, d), F32)],
        compiler_params=_cp("parallel"), name="hy_in",
    )(x, x, x, mods, pre_g.reshape(1, d), w_in, b_in.reshape(1, p), cw, conv_b.reshape(1, p))


_PI_SPLIT = (3.140625, 9.67502593994140625e-4, 1.509957990978376432e-7)
_SIN_TAYLOR = (-1.0 / 6, 1.0 / 120, -1.0 / 5040, 1.0 / 362880, -1.0 / 39916800)


def _sin(x):
    kf = jnp.round(x * (1.0 / math.pi))
    r = ((x - kf * _PI_SPLIT[0]) - kf * _PI_SPLIT[1]) - kf * _PI_SPLIT[2]
    r2 = r * r
    p = _SIN_TAYLOR[4]
    for c in _SIN_TAYLOR[3::-1]:
        p = p * r2 + c
    s = r + r * r2 * p
    odd = (kf.astype(jnp.int32) & 1) == 1
    return jnp.where(odd, -s, s)


def _hy_filter_kernel(bands_ref, w1_ref, b1_ref, fq_ref, w2_ref, b2_ref, w3_ref, dl_ref,
                      k_ref, nrm_ref, *, n, tr):
    i = pl.program_id(0)
    bwd = i >= n // tr
    row = lax.broadcasted_iota(jnp.int32, (tr, LANE), 0) + i * tr
    j = jnp.where(bwd, 2 * n - row, row).astype(F32)
    lane = lax.broadcasted_iota(jnp.int32, (tr, LANE), 1)
    t = j * (1.0 / (n - 1))
    arg = (2.0 * math.pi / n) * j * bands_ref[0:1, :] + bands_ref[1:2, :]
    z = jnp.where(lane == 0, t, jnp.where(lane <= 2 * HYENA_BANDS, _sin(arg), 0.0))
    fq = fq_ref[...]
    a = _sin(fq * (_dot_x3(z, w1_ref[...]) + b1_ref[...]))
    for k in range(w2_ref.shape[0]):
        a = _sin(fq * (_dot_x3(a, w2_ref[k]) + b2_ref[k]))
    h = _dot_x3(a, w3_ref[jnp.where(bwd, 1, 0)])
    decay = jnp.exp(-t[:, 0:1] * dl_ref[...])
    k = h * decay
    k = jnp.where(row[:, 0:1] == n, 0.0, k)
    k_ref[...] = k
    part = jnp.sum(jnp.abs(k), axis=0, keepdims=True)

    @pl.when(i == 0)
    def _():
        nrm_ref[...] = jnp.zeros_like(nrm_ref)

    nrm_ref[...] += jnp.broadcast_to(part, nrm_ref.shape)


def _hy_filter_call(n, d, f_w1, f_b1, f_freq, f_w2, f_b2, f_w3):
    fw = f_w1.shape[1]
    tr = min(ROW_TILE, n)
    bands_np = np.zeros((8, LANE), np.float32)
    bands_np[0, 1:1 + HYENA_BANDS] = np.linspace(1e-4, HYENA_BANDS - 1, HYENA_BANDS, dtype=np.float32)
    bands_np[0, 1 + HYENA_BANDS:1 + 2 * HYENA_BANDS] = bands_np[0, 1:1 + HYENA_BANDS]
    bands_np[1, 1:1 + HYENA_BANDS] = 0.5 * np.pi
    bands_np[1, 1 + HYENA_BANDS:1 + 2 * HYENA_BANDS] = np.pi
    w1p = jnp.zeros((LANE, fw), F32).at[:f_w1.shape[0]].set(f_w1)
    deltas = jnp.abs(jnp.linspace(math.log(HYENA_TARGET) / HYENA_SLOW, math.log(HYENA_TARGET) / HYENA_FAST,
                                  d, dtype=F32)).reshape(1, d)
    row = pl.BlockSpec((tr, d), lambda i: (i, 0))
    return pl.pallas_call(
        functools.partial(_hy_filter_kernel, n=n, tr=tr), grid=(2 * n // tr,),
        in_specs=[_const_spec((8, LANE)), _const_spec((LANE, fw)), _const_spec((1, fw)), _const_spec((1, fw)),
                  _const_spec(f_w2.shape), _const_spec((f_w2.shape[0], 1, fw)), _const_spec((2, fw, d)),
                  _const_spec((1, d))],
        out_specs=[row, _const_spec((8, d))],
        out_shape=[jax.ShapeDtypeStruct((2 * n, d), F32), jax.ShapeDtypeStruct((8, d), F32)],
        compiler_params=_cp("arbitrary"), name="hy_filter",
    )(jnp.asarray(bands_np), w1p, f_b1.reshape(1, fw), f_freq.reshape(1, fw), f_w2,
      f_b2.reshape(f_w2.shape[0], 1, fw), jnp.transpose(f_w3.reshape(fw, 2, d), (1, 0, 2)), deltas)


def _dft_cs(nf, nt, period):
    ft = (np.arange(nf)[:, None] * np.arange(nt)[None, :]) % period
    ang = 2.0 * np.pi * ft / period
    return np.cos(ang), np.sin(ang)


def _twiddle_tables(n1, n2):
    nn = n1 * n2
    f1 = jnp.arange(n1, dtype=jnp.int32)
    t2 = jnp.arange(n2, dtype=jnp.int32)
    idx = (f1[:, None] * t2[None, :]) % nn
    ang = idx.astype(F32) * (2.0 * math.pi / nn)
    shape = idx.shape + (LANE,)
    return (jnp.broadcast_to(jnp.cos(ang)[..., None], shape),
            jnp.broadcast_to(jnp.sin(ang)[..., None], shape))


KRON_R = 16
KRON_W = 256


def _cblock(mc):
    return np.block([[mc.real, -mc.imag], [mc.imag, mc.real]])


def _split_radix(n1):
    b = 16 if (n1 % 16 == 0 and n1 >= 64) else 4
    assert n1 % b == 0 and (n1 // b) % 2 == 0
    return n1 // b, b


def _slab_dft_mats(n1, a_in, a_out, sign, neg_im=False, real_in=False, scale=1.0):
    a, b = _split_radix(n1)
    r = KRON_R
    eye = np.eye(r)
    w = lambda num, den: np.exp(sign * 2j * np.pi * num / den)
    ua = np.arange(a)[:, None] * np.arange(a)[None, :]
    vb = np.arange(b)[:, None] * np.arange(b)[None, :]
    if sign < 0:
        m1 = np.kron(w(ua[:, :a_in], a) * scale, eye)
        l1 = _cblock(m1)
        if neg_im:
            l1[:, a_in * r:] *= -1.0
        if real_in:
            l1 = l1[:, :a_in * r]
        l2 = np.stack([_cblock(np.kron(w(vb, b) * w(u * np.arange(b)[None, :], n1), eye)) for u in range(a)])
    else:
        l1 = np.stack([_cblock(np.kron(w(vb, b) * w(u * np.arange(b)[:, None], n1), eye)) for u in range(a)])
        l2 = _cblock(np.kron(w(ua[:a_out, :], a) * scale, eye))
    return jnp.asarray(l1, BF16), jnp.asarray(l2, BF16)


def _kfa_kernel(*refs, nparts, a_in, a, b):
    l1_ref, l2_ref = refs[0], refs[1]
    parts = refs[2:2 + nparts]
    twc_ref, tws_ref, or_ref, oi_ref = refs[2 + nparts:]
    r = KRON_R
    reps = or_ref.shape[-1] // LANE
    l1 = l1_ref[...]
    y1 = []
    for bb in range(b):
        x = jnp.concatenate([p[0, aa * b + bb] for p in parts for aa in range(a_in)], axis=0).astype(BF16)
        y1.append(_dot(l1, x).astype(BF16))
    for u in range(a):
        x = jnp.concatenate([y1[bb][u * r:(u + 1) * r] for bb in range(b)]
                            + [y1[bb][(a + u) * r:(a + u + 1) * r] for bb in range(b)], axis=0)
        z = _dot(l2_ref[u], x)
        for v in range(b):
            f1 = u + a * v
            zr = z[v * r:(v + 1) * r]
            zi = z[(b + v) * r:(b + v + 1) * r]
            c = jnp.concatenate([twc_ref[f1]] * reps, axis=1)
            sn = jnp.concatenate([tws_ref[f1]] * reps, axis=1)
            or_ref[0, f1] = (zr * c + zi * sn).astype(or_ref.dtype)
            oi_ref[0, f1] = (zi * c - zr * sn).astype(oi_ref.dtype)


def _kfa_call(parts, nb_out, n1, n2, d, l1, l2, twc, tws):
    a, b = _split_radix(n1)
    a_in = parts[0][0].shape[1] // b
    r, w = KRON_R, min(KRON_W, d)
    in_specs = [_const_spec(l1.shape), pl.BlockSpec(l2.shape, lambda bi, j, k: (0, 0, 0), pipeline_mode=pl.Buffered(1))]
    args = [l1, l2]
    for arr, bi_fixed in parts:
        t1_in = arr.shape[1]
        if bi_fixed is None:
            in_specs.append(pl.BlockSpec((1, t1_in, r, w), lambda bi, j, k: (bi, 0, j, k)))
        else:
            in_specs.append(pl.BlockSpec((1, t1_in, r, w), lambda bi, j, k, f=bi_fixed: (f, 0, j, k)))
        args.append(arr)
    tw_spec = pl.BlockSpec((n1, r, LANE), lambda bi, j, k: (0, j, 0))
    out_spec = pl.BlockSpec((1, n1, r, w), lambda bi, j, k: (bi, 0, j, k))
    out = jax.ShapeDtypeStruct((nb_out, n1, n2, d), BF16)
    return pl.pallas_call(
        functools.partial(_kfa_kernel, nparts=len(parts), a_in=a_in, a=a, b=b),
        grid=(nb_out, n2 // r, d // w),
        in_specs=in_specs + [tw_spec, tw_spec], out_specs=[out_spec, out_spec], out_shape=[out, out],
        compiler_params=_cp("parallel", "parallel", "parallel"), name="kfa",
    )(*args, twc, tws)


def _kfc_kernel(l3_ref, l4_ref, gr_ref, gi_ref, vg_ref, g0_ref, skip_ref, o_ref, *, a, b, a_out):
    r = KRON_R
    y3 = []
    for u in range(a):
        x = jnp.concatenate([gr_ref[0, u + a * v] for v in range(b)]
                            + [gi_ref[0, u + a * v] for v in range(b)], axis=0)
        y3.append(_dot(l3_ref[u], x).astype(BF16))
    l4 = l4_ref[...]
    skip = skip_ref[...]
    for bb in range(b):
        x = jnp.concatenate([y3[u][bb * r:(bb + 1) * r] for u in range(a)]
                            + [y3[u][(b + bb) * r:(b + bb + 1) * r] for u in range(a)], axis=0)
        y = _dot(l4, x)
        for sg in range(2):
            for aa in range(a_out):
                t1 = aa * b + bb
                yb = y[(sg * a_out + aa) * r:(sg * a_out + aa + 1) * r]
                o_ref[sg, t1] = ((yb + vg_ref[sg, t1] * skip) * g0_ref[sg, t1]).astype(o_ref.dtype)


def _kfc_call(g1r, g1i, vg4, g04, skip, n1, n2, d, l3, l4):
    a, b = _split_radix(n1)
    nb, t1_out = vg4.shape[:2]
    r, w = KRON_R, min(KRON_W, d)
    slab = pl.BlockSpec((1, n1, r, w), lambda j, k: (0, 0, j, k))
    nat = pl.BlockSpec((nb, t1_out, r, w), lambda j, k: (0, 0, j, k))
    return pl.pallas_call(
        functools.partial(_kfc_kernel, a=a, b=b, a_out=t1_out // b), grid=(n2 // r, d // w),
        in_specs=[pl.BlockSpec(l3.shape, lambda j, k: (0, 0, 0), pipeline_mode=pl.Buffered(1)),
                  _const_spec(l4.shape), slab, slab, nat, nat, pl.BlockSpec((1, w), lambda j, k: (0, k))],
        out_specs=nat, out_shape=jax.ShapeDtypeStruct(vg4.shape, BF16),
        compiler_params=_cp("parallel", "parallel"), name="kfc",
    )(l3, l4, g1r, g1i, vg4, g04, skip.reshape(1, d))


def _hy_b_kernel(fm_ref, fmc_ref, xr_ref, xi_ref, fr_ref, fi_ref, sc_ref, twc_ref, tws_ref,
                 or_ref, oi_ref, *, nf, d):
    fm = fm_ref[...]
    fmc = fmc_ref[...]
    n2 = xr_ref.shape[1]
    sc = sc_ref[...]
    for s in range(nf):
        x = _dot(fm, jnp.concatenate([xr_ref[s], xi_ref[s]], axis=0))
        kk = _dot(fm, jnp.concatenate([fr_ref[s], fi_ref[s]], axis=0))
        xr, xi = x[:n2], x[n2:]
        kr, ki = kk[:n2] * sc, kk[n2:] * sc
        yr = (xr * kr - xi * ki).astype(BF16)
        yi = (xr * ki + xi * kr).astype(BF16)
        g = _dot(fmc, jnp.concatenate([yr, yi], axis=0))
        gr, gi = g[:n2], g[n2:]
        c = jnp.concatenate([twc_ref[s]] * (d // LANE), axis=1)
        sn = jnp.concatenate([tws_ref[s]] * (d // LANE), axis=1)
        or_ref[s] = (gr * c - gi * sn).astype(or_ref.dtype)
        oi_ref[s] = (gi * c + gr * sn).astype(oi_ref.dtype)


def _block_c(cs, sn, sign):
    return np.block([[cs, -sign * sn], [sign * sn, cs]])


def _hy_conv_long(vg, g0, k, nrm, skip, b, n, d):
    n2 = FFT_N2
    nn = 2 * n
    n1 = nn // n2
    rows = n // n2
    a, _ = _split_radix(n1)
    la1, la2 = _slab_dft_mats(n1, a // 2, 0, -1)
    lk1, lk2 = _slab_dft_mats(n1, a, 0, -1, real_in=True)
    lc3, lc4 = _slab_dft_mats(n1, 0, a // 2, +1)
    cs2, sn2 = _dft_cs(n2, n2, n2)
    fm_b = jnp.asarray(_block_c(cs2, sn2, -1.0), BF16)
    fm_bc = jnp.asarray(_block_c(cs2, sn2, 1.0), BF16)
    twc, tws = _twiddle_tables(n1, n2)

    kfr, kfi = _kfa_call([(k.reshape(1, n1, n2, d), 0)], 1, n1, n2, d, lk1, lk2, twc, tws)
    nf = 4 if n1 % 4 == 0 else 1
    scale = (1.0 / (nrm[0:1, :] * nn))
    slab = pl.BlockSpec((nf, n2, d), lambda j: (j, 0, 0))
    shp3 = (n1, n2, d)
    vg4 = vg.reshape(b, rows, n2, d)
    g04 = g0.reshape(b, rows, n2, d)
    x1r, x1i = _kfa_call([(vg4, 0), (vg4, 1)], 1, n1, n2, d, la1, la2, twc, tws)
    tw_slab = pl.BlockSpec((nf, n2, LANE), lambda j: (j, 0, 0))
    g1r, g1i = pl.pallas_call(
        functools.partial(_hy_b_kernel, nf=nf, d=d), grid=(n1 // nf,),
        in_specs=[_const_spec(fm_b.shape), _const_spec(fm_bc.shape), slab, slab, slab, slab, _const_spec((1, d)),
                  tw_slab, tw_slab],
        out_specs=[slab, slab],
        out_shape=[jax.ShapeDtypeStruct(shp3, BF16)] * 2,
        compiler_params=_cp("parallel"), name="hy_b",
    )(fm_b, fm_bc, x1r.reshape(shp3), x1i.reshape(shp3), kfr.reshape(shp3), kfi.reshape(shp3), scale, twc, tws)
    out = _kfc_call(g1r.reshape(1, n1, n2, d), g1i.reshape(1, n1, n2, d), vg4, g04, skip, n1, n2, d, lc3, lc4)
    return out.reshape(b * n, d)


def _hy_short_kernel(fa_ref, fk_ref, fi_ref, vg_ref, g0_ref, k_ref, nrm_ref, skip_ref, o_ref, *, n):
    z = jnp.concatenate([vg_ref[0], vg_ref[1]], axis=0)
    x = _dot_hi(fa_ref[...], z)
    kk = _dot_hi(fk_ref[...], k_ref[...])
    nn = 2 * n
    sc = 1.0 / (nrm_ref[0:1, :] * nn)
    xr, xi = x[:nn], x[nn:]
    kr, ki = kk[:nn] * sc, kk[nn:] * sc
    y = _dot_hi(fi_ref[...], jnp.concatenate([xr * kr - xi * ki, xr * ki + xi * kr], axis=0))
    skip = skip_ref[...]
    for bi in range(2):
        o_ref[bi] = ((y[bi * n:(bi + 1) * n] + vg_ref[bi] * skip) * g0_ref[bi]).astype(o_ref.dtype)


def _hy_conv_short(vg, g0, k, nrm, skip, b, n, d):
    nn = 2 * n
    cs, sn = _dft_cs(nn, n, nn)
    fa = jnp.asarray(_block_c(cs, sn, -1.0), F32)
    csk, snk = _dft_cs(nn, nn, nn)
    fk = jnp.asarray(np.concatenate([csk, -snk], axis=0), F32)
    csi, sni = _dft_cs(n, nn, nn)
    fi = jnp.asarray(_block_c(csi, sni, 1.0), F32)
    cb = 256
    col3 = pl.BlockSpec((b, n, cb), lambda j: (0, 0, j))
    vec = pl.BlockSpec((1, cb), lambda j: (0, j))
    out = pl.pallas_call(
        functools.partial(_hy_short_kernel, n=n), grid=(d // cb,),
        in_specs=[_const_spec(fa.shape), _const_spec(fk.shape), _const_spec(fi.shape), col3, col3,
                  pl.BlockSpec((nn, cb), lambda j: (0, j)), pl.BlockSpec((8, cb), lambda j: (0, j)), vec],
        out_specs=col3, out_shape=jax.ShapeDtypeStruct((b, n, d), BF16),
        compiler_params=_cp("parallel"), name="hy_short",
    )(fa, fk, fi, vg.reshape(b, n, d), g0.reshape(b, n, d), k, nrm, skip.reshape(1, d))
    return out.reshape(b * n, d)


def _hyena_layer(xl, xc, mods_l, mods_c, pre_g, fin, w_in, b_in, conv_w, conv_b, filt, skip,
                 w_out, b_out, b, n, c):
    d = xl.shape[-1]
    w_in_b = w_in.astype(BF16)
    w_out_b = w_out.astype(BF16)
    tm = min(ROW_TILE, n)
    g0, vg = _hy_in_call(xl, mods_l, pre_g, w_in_b, b_in, conv_w, conv_b, tm, n)
    k, nrm = _hy_filter_call(n, d, *filt)
    u_out = _hy_conv_long(vg, g0, k, nrm, skip[0], b, n, d)
    xl, (hl,) = fin(xl, u_out, mods_l, w_out_b, b_out, tm, n)

    g0c, vgc = _hy_in_call(xc, mods_c, pre_g, w_in_b, b_in, conv_w, conv_b, c, c)
    kc, nrmc = _hy_filter_call(c, d, *filt)
    u_out_c = _hy_conv_short(vgc, g0c, kc, nrmc, skip[0], b, c, d)
    xc, (hc,) = fin(xc, u_out_c, mods_c, w_out_b, b_out, c, c, is_ctx=True)
    return xl, hl, hc


def _s5_operators(lam_re, lam_im, log_dt, b_re, b_im, c_re, c_im, d_skip):
    t = S5_T
    g, ns = lam_re.shape[1], lam_re.shape[2]
    gc = b_re.shape[-1]
    gl = LANE // gc
    nblk = g // gl
    lam = lax.complex(lam_re, lam_im)
    dt = jnp.exp(log_dt)[..., None]
    lam_bar = jnp.exp(lam * dt)
    b_bar = ((lam_bar - 1.0) / lam)[..., None] * lax.complex(b_re, b_im)
    c_mat = lax.complex(c_re, c_im)
    pw = jnp.arange(t + 1, dtype=F32)
    lam_pw = jnp.exp((lam * dt)[None] * pw[:, None, None, None])
    hp = HIGHEST
    kern = jnp.einsum('dgcn,tdgn,dgne->dgtce', c_mat, lam_pw[:t], b_bar, precision=hp).real
    dsk = d_skip.reshape(g, gc)
    kt = jnp.swapaxes(kern, -1, -2)
    centre = kt[0][:, 0] + kt[1][:, 0] + jnp.eye(gc, dtype=F32)[None] * dsk[:, :, None]
    ks = jnp.concatenate([kt[1][:, 1:][:, ::-1], centre[:, None], kt[0][:, 1:]], axis=1)
    ks = jnp.transpose(ks.reshape(nblk, gl, 2 * t - 1, gc, gc), (0, 2, 1, 3, 4)).reshape(nblk, 2 * t - 1, LANE, gc)
    same_group = (np.arange(LANE)[:, None] // gc == np.arange(LANE)[None, :] // gc).astype(np.float32)
    d_tab = jnp.tile(ks, (1, 1, 1, gl)) * same_group

    def compact(zc, im_sign):
        z = jnp.concatenate([zc.real, im_sign * zc.imag], axis=-1)
        z = jnp.transpose(z, (2, 1, 0, 3, 4)).reshape(nblk, gl, t, 2, gc, 2 * ns)
        return jnp.transpose(z, (0, 2, 1, 3, 4, 5))

    b_t = jnp.swapaxes(b_bar, -1, -2)
    pf = lam_pw[:t][::-1][:, 0, :, None, :] * b_t[0][None]
    pb = lam_pw[:t][:, 1, :, None, :] * b_t[1][None]
    p_tab = compact(jnp.stack([pf, pb], axis=0), 1.0)
    qf = c_mat[0][None] * lam_pw[1:t + 1, 0][:, :, None, :]
    qb = c_mat[1][None] * lam_pw[1:t + 1][::-1][:, 1][:, :, None, :]
    q_tab = compact(jnp.stack([qf, qb], axis=0), -1.0)

    a = lam_pw[t]
    m_op, p_op, q_op = _s5_expand(d_tab, p_tab, q_tab)
    return m_op, p_op, q_op, a.real.reshape(2, g * ns), a.imag.reshape(2, g * ns)


def _s5_m_kernel(d_ref, o_ref, *, t):
    for s in range(t):
        for tt in range(t):
            o_ref[0, s * LANE:(s + 1) * LANE, tt * LANE:(tt + 1) * LANE] = d_ref[0, tt - s + t - 1].astype(o_ref.dtype)


def _s5_pq_kernel(c_ref, o_ref, *, transpose):
    t, gl, nd, gc, w = c_ref.shape[1:]
    ns = w // 2
    lane_grp = lax.broadcasted_iota(jnp.int32, (gc, gl * ns), 1) // ns
    for j in range(t):
        rows = []
        for g in range(gl):
            cols = []
            for dd in range(nd):
                piece = c_ref[0, j, g, dd]
                for ri in range(2):
                    tiled = jnp.concatenate([piece[:, ri * ns:(ri + 1) * ns]] * gl, axis=1)
                    cols.append(jnp.where(lane_grp == g, tiled, 0.0))
            rows.append(jnp.concatenate(cols, axis=1))
        blk = jnp.concatenate(rows, axis=0)
        if transpose:
            o_ref[0, :, j * LANE:(j + 1) * LANE] = blk.T.astype(o_ref.dtype)
        else:
            o_ref[0, j * LANE:(j + 1) * LANE, :] = blk.astype(o_ref.dtype)


def _s5_expand(d_tab, p_tab, q_tab):
    nblk, nlag = d_tab.shape[:2]
    t = (nlag + 1) // 2
    _, _, gl, nd, gc, w = p_tab.shape
    ncol = nd * gl * w
    whole = lambda shape: pl.BlockSpec((1,) + shape, lambda b: (b,) + (0,) * len(shape))
    m_op = pl.pallas_call(
        functools.partial(_s5_m_kernel, t=t), grid=(nblk,),
        in_specs=[whole((nlag, LANE, LANE))], out_specs=whole((t * LANE, t * LANE)),
        out_shape=jax.ShapeDtypeStruct((nblk, t * LANE, t * LANE), BF16),
        compiler_params=_cp("parallel"), name="s5_m_op",
    )(d_tab)
    tab_spec = whole((t, gl, nd, gc, w))
    p_op = pl.pallas_call(
        functools.partial(_s5_pq_kernel, transpose=False), grid=(nblk,),
        in_specs=[tab_spec], out_specs=whole((t * LANE, ncol)),
        out_shape=jax.ShapeDtypeStruct((nblk, t * LANE, ncol), BF16),
        compiler_params=_cp("parallel"), name="s5_p_op",
    )(p_tab)
    q_op = pl.pallas_call(
        functools.partial(_s5_pq_kernel, transpose=True), grid=(nblk,),
        in_specs=[tab_spec], out_specs=whole((ncol, t * LANE)),
        out_shape=jax.ShapeDtypeStruct((nblk, ncol, t * LANE), BF16),
        compiler_params=_cp("parallel"), name="s5_q_op",
    )(q_tab)
    return m_op, p_op, q_op


def _s5_sum_kernel(*refs, t):
    u_refs = refs[:t]
    p_ref = refs[t]
    outs = refs[t + 1:]
    u = jnp.concatenate([r[...] for r in u_refs], axis=1)
    res = _dot(u, p_ref[0])
    w = res.shape[1] // len(outs)
    for i, o in enumerate(outs):
        o[...] = res[:, i * w:(i + 1) * w]


def _s5_sum_call(h, p_op, rb):
    rows = h.shape[0]
    t = S5_T
    nblk = p_op.shape[0]
    w = p_op.shape[2] // 4
    u_specs = [pl.BlockSpec((rb, LANE), lambda gb, r, s=s: (r, s * nblk + gb)) for s in range(t)]
    out_spec = pl.BlockSpec((rb, w), lambda gb, r: (r, gb))
    return pl.pallas_call(
        functools.partial(_s5_sum_kernel, t=t), grid=(nblk, rows // rb),
        in_specs=u_specs + [pl.BlockSpec((1,) + p_op.shape[1:], lambda gb, r: (gb, 0, 0))],
        out_specs=[out_spec] * 4,
        out_shape=[jax.ShapeDtypeStruct((rows, nblk * w), F32)] * 4,
        compiler_params=_cp("parallel", "parallel"), name="s5_sum",
    )(*([h] * t), p_op)


def _s5_rec_kernel(sr_ref, si_ref, ar_ref, ai_ref, h0r_ref, h0i_ref, hr_ref, hi_ref, fr_ref, fi_ref,
                   cr, ci, *, kb, reverse):
    @pl.when(pl.program_id(1) == 0)
    def _():
        cr[...] = h0r_ref[0]
        ci[...] = h0i_ref[0]

    ar, ai = ar_ref[...], ai_ref[...]

    def body(i, carry):
        hr, hi = carry
        k = kb - 1 - i if reverse else i
        hr_ref[pl.ds(k, 1), :] = hr
        hi_ref[pl.ds(k, 1), :] = hi
        nr = ar * hr - ai * hi + sr_ref[pl.ds(k, 1), :]
        ni = ar * hi + ai * hr + si_ref[pl.ds(k, 1), :]
        return nr, ni

    hr, hi = lax.fori_loop(0, kb, body, (cr[...], ci[...]))
    cr[...] = hr
    ci[...] = hi
    fr_ref[0] = hr
    fi_ref[0] = hi


def _s5_rec_call(sr, si, ar, ai, h0r, h0i, nb_batch, reverse):
    rows, w = sr.shape
    nk = rows // nb_batch
    kb = min(64, nk)
    nb = nk // kb
    blk = (lambda bi, i: (bi * nb + nb - 1 - i, 0)) if reverse else (lambda bi, i: (bi * nb + i, 0))
    row_spec = pl.BlockSpec((kb, w), blk)
    vec = _const_spec((1, w))
    st = pl.BlockSpec((1, 1, w), lambda bi, i: (bi, 0, 0))
    return pl.pallas_call(
        functools.partial(_s5_rec_kernel, kb=kb, reverse=reverse), grid=(nb_batch, nb),
        in_specs=[row_spec, row_spec, vec, vec, st, st],
        out_specs=[row_spec, row_spec, st, st],
        out_shape=[jax.ShapeDtypeStruct((rows, w), F32)] * 2 + [jax.ShapeDtypeStruct((nb_batch, 1, w), F32)] * 2,
        scratch_shapes=[pltpu.VMEM((1, w), F32), pltpu.VMEM((1, w), F32)],
        compiler_params=_cp("parallel", "arbitrary"), name="s5_rec",
    )(sr, si, ar, ai, h0r, h0i)


def _s5_out_kernel(*refs, t):
    u_refs = refs[:t]
    h_refs = refs[t:t + 4]
    m_ref, q_ref, o_ref, tok_scr = refs[t + 4:]
    u = jnp.concatenate([r[...] for r in u_refs], axis=1)
    hcat = jnp.concatenate([r[...].astype(BF16) for r in h_refs], axis=1)
    y = _dot(u, m_ref[0]) + _dot(hcat, q_ref[0])
    g = 0.5 * y * (1.0 + lax.erf(y * (2.0 ** -0.5)))
    rb = g.shape[0]
    for j in range(t):
        tok_scr[pl.ds(j, rb, stride=t), :] = g[:, j * LANE:(j + 1) * LANE]
    o_ref[...] = tok_scr[...].astype(o_ref.dtype)


def _s5_out_call(h, states, m_op, q_op, rb):
    rows = h.shape[0]
    t = S5_T
    nblk = m_op.shape[0]
    w = q_op.shape[1] // 4
    u_specs = [pl.BlockSpec((rb, LANE), lambda gb, r, s=s: (r, s * nblk + gb)) for s in range(t)]
    return pl.pallas_call(
        functools.partial(_s5_out_kernel, t=t), grid=(nblk, rows // rb),
        in_specs=u_specs + [pl.BlockSpec((rb, w), lambda gb, r: (r, gb))] * 4
        + [pl.BlockSpec((1,) + m_op.shape[1:], lambda gb, r: (gb, 0, 0)),
           pl.BlockSpec((1,) + q_op.shape[1:], lambda gb, r: (gb, 0, 0))],
        out_specs=pl.BlockSpec((rb * t, LANE), lambda gb, r: (r, gb)),
        out_shape=jax.ShapeDtypeStruct((rows * t, nblk * LANE), BF16),
        scratch_shapes=[pltpu.VMEM((rb * t, LANE), F32)],
        compiler_params=_cp("parallel", "parallel"), name="s5_out",
    )(*([h] * t), *states, m_op, q_op)


def _s5_layer(xl, hl, hc, mods_l, fin, lam_re, lam_im, log_dt, b_re, b_im, c_re, c_im,
              d_skip, w_glu, b_glu, b, n, c):
    t = S5_T
    m_op, p_op, q_op, a_re, a_im = _s5_operators(lam_re, lam_im, log_dt, b_re, b_im, c_re, c_im, d_skip)
    w = a_re.shape[-1]

    def scan(h, init):
        sfr, sfi, sbr, sbi = _s5_sum_call(h, p_op, min(ROW_TILE, h.shape[0]))
        hfr, hfi, ffr, ffi = _s5_rec_call(sfr, sfi, a_re[0:1], a_im[0:1], init[0], init[1], b, False)
        hbr, hbi, fbr, fbi = _s5_rec_call(sbr, sbi, a_re[1:2], a_im[1:2], init[2], init[3], b, True)
        return (hfr, hfi, hbr, hbi), (ffr, ffi, fbr, fbi)

    zeros = jnp.zeros((b, 1, w), F32)
    _, ctx_final = scan(hc, (zeros,) * 4)
    states, _ = scan(hl, ctx_final)
    nk = n // t
    g_nat = _s5_out_call(hl, states, m_op, q_op, min(ROW_TILE, b * nk))
    return fin(xl, g_nat, mods_l, w_glu.astype(BF16), b_glu, min(ROW_TILE, n), n, glu=True)


def _fnet_channel_mats():
    cc, sc = _dft_cs(FNET_GC, FNET_GC, FNET_GC)
    return jnp.asarray(np.concatenate([cc, sc], axis=1) / np.sqrt(FNET_GC), BF16)


def _fn_c_kernel(l5_ref, l6_ref, xr_ref, xi_ref, o_ref, *, nh):
    r = KRON_R
    n2 = nh * r
    l5 = l5_ref[...]
    y5 = [_dot(l5, jnp.concatenate([xr_ref[0, f], xi_ref[0, f]], axis=0)).astype(BF16) for f in range(r)]
    l6 = l6_ref[...]
    for p in range(nh):
        x = jnp.concatenate([y5[f][p * r:(p + 1) * r] for f in range(r)]
                            + [y5[f][n2 + p * r:n2 + (p + 1) * r] for f in range(r)], axis=0)
        out = _dot(l6, x)
        for q in range(r):
            o_ref[0, p + nh * q] = out[q * r:(q + 1) * r].astype(o_ref.dtype)


def _fnet_layer(xl, am, bm, mods_l, fin, w_o, b_o, b, n, d):
    n2 = FFT_N2
    n1 = n // n2
    tm = min(ROW_TILE, n)
    a, _ = _split_radix(n1)
    l1, l2 = _slab_dft_mats(n1, a, 0, -1, neg_im=True, scale=1.0 / np.sqrt(n))
    twc, tws = _twiddle_tables(n1, n2)
    a4, b4 = am.reshape(b, n1, n2, d), bm.reshape(b, n1, n2, d)
    xr, xi = _kfa_call([(a4, None), (b4, None)], b, n1, n2, d, l1, l2, twc, tws)

    r, w = KRON_R, min(KRON_W, d)
    nh = n2 // r
    assert n1 % r == 0 and n2 % r == 0
    m5 = np.zeros((n2, n2), np.complex128)
    m6 = np.zeros((r * r, r * r), np.complex128)
    for s in range(r):
        for p in range(nh):
            for h in range(nh):
                m5[p * r + s, h * r + s] = np.exp(-2j * np.pi * (p * h / nh + p * s / n2))
        for q in range(r):
            for f in range(r):
                m6[q * r + f, f * r + s] = np.exp(-2j * np.pi * q * s / r)
    l5 = jnp.asarray(_cblock(m5), BF16)
    l6 = jnp.asarray(np.concatenate([m6.real, -m6.imag], axis=1), BF16)
    grp = pl.BlockSpec((1, r, n2, w), lambda bi, fh, k: (bi, fh, 0, k))
    y = pl.pallas_call(
        functools.partial(_fn_c_kernel, nh=nh), grid=(b, n1 // r, d // w),
        in_specs=[_const_spec(l5.shape), _const_spec(l6.shape), grp, grp],
        out_specs=pl.BlockSpec((1, n2, None, r, w), lambda bi, fh, k: (bi, 0, fh, 0, k)),
        out_shape=jax.ShapeDtypeStruct((b, n2, n1 // r, r, d), BF16),
        compiler_params=_cp("parallel", "parallel", "parallel"), name="fn_c",
    )(l5, l6, xr, xi)
    return fin(xl, y.reshape(b * n, d), mods_l, w_o.astype(BF16), b_o, tm, n)


def kernel(x, c, ctx, c_ctx, mod_w, mod_b, mix_pre_g, mix_post_g, ffn_pre_g, ffn_post_g, ffn_w13, ffn_w2,
           mla_w_in, mla_q_norm_g, mla_kv_norm_g, mla_w_uq, mla_w_ukv, mla_w_o,
           hy_w_in, hy_b_in, hy_conv_w, hy_conv_b, hy_f_w1, hy_f_b1, hy_f_freq, hy_f_w2, hy_f_b2, hy_f_w3,
           hy_skip, hy_w_out, hy_b_out,
           s5_lambda_re, s5_lambda_im, s5_log_dt, s5_b_re, s5_b_im, s5_c_re, s5_c_im, s5_d, s5_w_glu, s5_b_glu,
           fn_w_o, fn_b_o):
    b, n, d = x.shape
    cl = ctx.shape[1]
    depth = mod_w.shape[0]
    assert b == 2 and depth == 4, "two batches ride one complex transform; one layer per mixer"
    mods = _mods(c, c_ctx, mod_w, mod_b)
    xl = x.reshape(b * n, d)
    xc = ctx.reshape(b * cl, d)

    w13_all = ffn_w13.astype(BF16)
    w2_all = ffn_w2.astype(BF16)

    def mods_c(i):
        return jnp.broadcast_to(mods[i, 2:3], (b, 8, d))

    def finisher(i, emit_kind=None, cs=None):
        def fin(x_, y_, mods_, wm, bm, tm_, rows_per_batch, glu=False, is_ctx=False):
            emit = None
            if emit_kind is not None:
                emit = (emit_kind, mods_c(i + 1) if is_ctx else mods[i + 1, 0:2], mix_pre_g[i + 1], cs)
            return _mix_ffn_call(x_, y_, mods_, wm, bm, mix_post_g[i], ffn_pre_g[i], ffn_post_g[i],
                                 w13_all, w2_all, i, tm_, rows_per_batch, glu, emit)
        return fin

    xl, xc = _mla_layer(xl, xc, mods[0, 0:2], mods_c(0), mix_pre_g[0], finisher(0), mla_w_in[0],
                        mla_q_norm_g[0], mla_kv_norm_g[0], mla_w_uq[0], mla_w_ukv[0], mla_w_o[0], b, n, cl)
    filt = (hy_f_w1[0], hy_f_b1[0], hy_f_freq[0], hy_f_w2[0], hy_f_b2[0], hy_f_w3[0])
    xl, hl, hc = _hyena_layer(xl, xc, mods[1, 0:2], mods_c(1), mix_pre_g[1], finisher(1, "chunks"), hy_w_in[0],
                              hy_b_in[0], hy_conv_w[0], hy_conv_b[0], filt, hy_skip[0], hy_w_out[0], hy_b_out[0],
                              b, n, cl)
    xl, (am, bm) = _s5_layer(xl, hl, hc, mods[2, 0:2], finisher(2, "fnet", _fnet_channel_mats()), s5_lambda_re[0],
                             s5_lambda_im[0], s5_log_dt[0], s5_b_re[0], s5_b_im[0], s5_c_re[0], s5_c_im[0],
                             s5_d[0], s5_w_glu[0], s5_b_glu[0], b, n, cl)
    xl = _fnet_layer(xl, am, bm, mods[3, 0:2], finisher(3), fn_w_o[0], fn_b_o[0], b, n, d)
    return xl.reshape(b, n, d)
```

```python
import functools
import math

import numpy as np
import jax
import jax.numpy as jnp
from jax import lax
from jax.experimental import pallas as pl
from jax.experimental.pallas import tpu as pltpu

F32 = jnp.float32
BF16 = jnp.bfloat16
NORM_EPS = 1e-6
LANE = 128
MXU_COLS = 256
ROW_TILE = 512
VMEM_LIMIT = 56 * 1024 * 1024
HIGHEST = lax.Precision.HIGHEST

GRID_W = 64
ROPE_THETA = 10000.0
MLA_HEADS = 8
MLA_NOPE = 128
MLA_ROPE = 64
MLA_V = 128
MLA_VT = MLA_V + 16
MLA_QK = MLA_NOPE + 2 * MLA_ROPE
HYENA_BANDS = 16
HYENA_TARGET = 1e-2
HYENA_FAST = 0.3
HYENA_SLOW = 1.5
S5_T = 16
FNET_GC = 128
FFT_N2 = 128


def _cp(*sem):
    return pltpu.CompilerParams(dimension_semantics=sem, vmem_limit_bytes=VMEM_LIMIT)


def _dot(a, b):
    return jnp.dot(a, b, preferred_element_type=F32)


def _dot_hi(a, b):
    return jnp.dot(a, b, preferred_element_type=F32, precision=HIGHEST)


def _dot_x3(a, b):
    a_hi = a.astype(BF16)
    b_hi = b.astype(BF16)
    a_lo = (a - a_hi.astype(F32)).astype(BF16)
    b_lo = (b - b_hi.astype(F32)).astype(BF16)
    return _dot(a_hi, b_hi) + (_dot(a_hi, b_lo) + _dot(a_lo, b_hi))


def _rms(x, g):
    ms = jnp.mean(x * x, axis=-1, keepdims=True)
    return x * lax.rsqrt(ms + NORM_EPS) * g


def _normmod(x, g, shift, scale):
    return _rms(x, g) * (1.0 + scale) + shift


def _const_spec(shape):
    nd = len(shape)
    return pl.BlockSpec(shape, lambda *_: (0,) * nd)


def _mods_kernel(st_ref, w_ref, b_ref, o_ref):
    st = st_ref[...]
    st = st * jax.nn.sigmoid(st)
    w = w_ref[0]
    rows = [jnp.sum(st[:, r:r + 1] * w, axis=0, keepdims=True) for r in range(3)]
    rows.append(jnp.zeros((5, w.shape[1]), F32))
    o_ref[0] = jnp.concatenate(rows, axis=0) + b_ref[0]


def _mods(c, c_ctx, mod_w, mod_b):
    depth, d, n6 = mod_w.shape
    st = jnp.zeros((d, 8), F32).at[:, 0:2].set(c.T).at[:, 2].set(c_ctx)
    tn = 1024
    out = pl.pallas_call(
        _mods_kernel,
        grid=(depth, n6 // tn),
        in_specs=[_const_spec((d, 8)),
                  pl.BlockSpec((1, d, tn), lambda i, j: (i, 0, j)),
                  pl.BlockSpec((1, 1, tn), lambda i, j: (i, 0, j))],
        out_specs=pl.BlockSpec((1, 8, tn), lambda i, j: (i, 0, j)),
        out_shape=jax.ShapeDtypeStruct((depth, 8, n6), F32),
        compiler_params=_cp("parallel", "parallel"),
        name="mods",
    )(st, mod_w, mod_b.reshape(depth, 1, n6))
    m = out[:, :3].reshape(depth, 3, n6 // d, d)
    return jnp.pad(m, ((0, 0), (0, 0), (0, 8 - n6 // d), (0, 0)))


def _row_specs(tm, d, tpb):
    x_spec = pl.BlockSpec((tm, d), lambda i: (i, 0))
    mod_spec = pl.BlockSpec((1, 8, d), lambda i: (i // tpb, 0, 0))
    return x_spec, mod_spec


def _channel_dft(h, cs):
    gc = FNET_GC
    ab = [_dot(h[:, k * gc:(k + 1) * gc], cs) for k in range(h.shape[1] // gc)]
    return (jnp.concatenate([z[:, :gc] for z in ab], axis=1), jnp.concatenate([z[:, gc:] for z in ab], axis=1))


def _mix_ffn_kernel(x_ref, y_ref, mod_ref, wm_ref, bm_ref, gm_ref, pre_ref, post_ref, w13_ref, w2_ref, *rest,
                    f, cuts, glu, emit):
    if emit == "chunks":
        nmod_ref, ng_ref, o_ref, h_ref, tok_scr = rest
    elif emit == "fnet":
        nmod_ref, ng_ref, cs_ref, o_ref, a_ref, b_ref = rest
    else:
        (o_ref,) = rest
    z = _dot(y_ref[...].astype(BF16), wm_ref[...]) + bm_ref[...]
    if glu:
        d = o_ref.shape[-1]
        z = z[:, :d] * jax.nn.sigmoid(z[:, d:])
    x = x_ref[...] + mod_ref[0, 2:3, :] * _rms(z, gm_ref[...])
    h = _normmod(x, pre_ref[...], mod_ref[0, 3:4, :], mod_ref[0, 4:5, :]).astype(BF16)
    acc = None
    for lo, hi in zip(cuts[:-1], cuts[1:]):
        a = _dot(h, w13_ref[:, lo:hi])
        b = _dot(h, w13_ref[:, f + lo:f + hi])
        gact = (a * jax.nn.sigmoid(a) * b).astype(BF16)
        part = _dot(gact, w2_ref[lo:hi, :])
        acc = part if acc is None else acc + part
    xo = x + mod_ref[0, 5:6, :] * _rms(acc, post_ref[...])
    o_ref[...] = xo
    if emit:
        hn = _normmod(xo, ng_ref[...], nmod_ref[0, 0:1, :], nmod_ref[0, 1:2, :])
        if emit == "chunks":
            tm, d = hn.shape
            for lt in range(d // LANE):
                tok_scr[lt * tm:(lt + 1) * tm, :] = hn[:, lt * LANE:(lt + 1) * LANE]
            for j in range(S5_T):
                for lt in range(d // LANE):
                    rows = tok_scr[pl.ds(lt * tm + j, tm // S5_T, stride=S5_T), :]
                    h_ref[:, j * d + lt * LANE:j * d + (lt + 1) * LANE] = rows.astype(h_ref.dtype)
        else:
            a, b = _channel_dft(hn.astype(BF16), cs_ref[...])
            a_ref[...] = a.astype(a_ref.dtype)
            b_ref[...] = b.astype(b_ref.dtype)


def _mix_ffn_call(x, y, mods, wm, bm, mix_post_g, ffn_pre_g, ffn_post_g, w13, w2, layer, tm, rows_per_batch,
                  glu=False, emit=None):
    m, d = x.shape
    k, nm = wm.shape
    f = w2.shape[1]
    cuts = tuple(range(0, f, 6 * MXU_COLS)) + (f,)
    x_spec, mod_spec = _row_specs(tm, d, rows_per_batch // tm)
    once = lambda shape: pl.BlockSpec(shape, lambda i: (0, 0), pipeline_mode=pl.Buffered(1))
    of_layer = lambda shape: pl.BlockSpec((None,) + shape, lambda i: (layer, 0, 0), pipeline_mode=pl.Buffered(1))
    in_specs = [x_spec, pl.BlockSpec((tm, k), lambda i: (i, 0)), mod_spec,
                once((k, nm)), _const_spec((1, nm)), _const_spec((1, d)), _const_spec((1, d)),
                _const_spec((1, d)), of_layer((d, 2 * f)), of_layer((f, d))]
    args = [x, y, mods, wm, bm.reshape(1, nm), mix_post_g.reshape(1, d), ffn_pre_g.reshape(1, d),
            ffn_post_g.reshape(1, d), w13, w2]
    out_specs, out_shape = [x_spec], [jax.ShapeDtypeStruct((m, d), F32)]
    kind = None
    scratch = []
    if emit is not None:
        kind, next_mods, next_g, cs = emit
        in_specs += [mod_spec, _const_spec((1, d))]
        args += [next_mods, next_g.reshape(1, d)]
        if kind == "fnet":
            in_specs.append(_const_spec(cs.shape))
            args.append(cs)
            out_specs += [x_spec] * 2
            out_shape += [jax.ShapeDtypeStruct((m, d), BF16)] * 2
        else:
            out_specs.append(pl.BlockSpec((tm // S5_T, S5_T * d), lambda i: (i, 0)))
            out_shape.append(jax.ShapeDtypeStruct((m // S5_T, S5_T * d), BF16))
            scratch.append(pltpu.VMEM((tm * (d // LANE), LANE), F32))
    outs = pl.pallas_call(
        functools.partial(_mix_ffn_kernel, f=f, cuts=cuts, glu=glu, emit=kind), grid=(m // tm,),
        in_specs=in_specs, out_specs=out_specs, out_shape=out_shape, scratch_shapes=scratch,
        compiler_params=_cp("parallel"), name="mix_ffn",
    )(*args)
    return outs[0] if emit is None else (outs[0], tuple(outs[1:]))


def _mla_proj_kernel(x_ref, mod_ref, g_ref, w_ref, qg_ref, kvg_ref, wq_ref, wkv_ref, cp_ref, sp_ref,
                     q_ref, k_ref, vt_ref, *, ql, kvl, qscale, positional):
    h = _normmod(x_ref[...], g_ref[...], mod_ref[0, 0:1, :], mod_ref[0, 1:2, :])
    z = _dot(h.astype(BF16), w_ref[...])
    qn = _rms(z[:, :ql], qg_ref[...]).astype(BF16)
    cn = _rms(z[:, ql:ql + kvl], kvg_ref[...]).astype(BF16)
    tk = z.shape[0]
    low = lax.broadcasted_iota(jnp.int32, (tk, 2 * MLA_ROPE), 1) < MLA_ROPE
    if positional:
        cpf = jnp.concatenate([cp_ref[...]] * 2, axis=1)
        spf = jnp.concatenate([sp_ref[...]] * 2, axis=1)

    def rope_slots(pair, keep_raw):
        swapped = pltpu.roll(pair, MLA_ROPE, 1)
        raw = jnp.where(low, 0.0, swapped) if keep_raw else None
        if not positional:
            return raw
        rot = jnp.where(low, pair * cpf + swapped * spf, 0.0)
        return rot + raw if keep_raw else rot

    kr = rope_slots(z[:, ql + kvl:], keep_raw=not positional).astype(BF16)

    zkv = _dot(cn, wkv_ref[...])
    ones_blk = (lax.broadcasted_iota(jnp.int32, (MLA_VT - MLA_V, tk), 0) == 0).astype(BF16)
    for hd in range(MLA_HEADS):
        base = hd * (MLA_NOPE + MLA_V)
        k_ref[0, hd, 0, :, 0:MLA_NOPE] = zkv[:, base:base + MLA_NOPE].astype(BF16)
        k_ref[0, hd, 0, :, MLA_NOPE:MLA_QK] = kr
        vt_ref[0, hd, 0, 0:MLA_V, :] = zkv[:, base + MLA_NOPE:base + MLA_NOPE + MLA_V].T.astype(BF16)
        vt_ref[0, hd, 0, MLA_V:MLA_VT, :] = ones_blk

    zq = _dot(qn, wq_ref[...])
    for hd in range(MLA_HEADS):
        base = hd * MLA_QK
        rp = rope_slots(zq[:, base + MLA_NOPE:base + MLA_QK], keep_raw=True)
        qcat = jnp.concatenate([zq[:, base:base + MLA_NOPE], rp], axis=1) * qscale
        q_ref[0, hd] = qcat.T.astype(BF16)


def _flash_kernel(q_ref, kc_ref, vc_ref, *rest, n_lat):
    if n_lat:
        kl_ref, vl_ref, o_ref, s_scr, acc_scr = rest
    else:
        o_ref, acc_scr = rest
    qt = q_ref[0, 0]

    def qk(k, slot):
        s = _dot(k, qt)
        s_scr[slot] = s
        return jnp.max(s, axis=0, keepdims=True)

    def sm_pv(slot, vt, m, mx):
        m_new = jnp.maximum(m, mx)
        alpha = jnp.exp2(m - m_new)
        p = jnp.exp2(s_scr[slot] - m_new).astype(BF16)
        acc_scr[...] = alpha * acc_scr[...] + _dot(vt, p)
        return m_new

    sc = _dot(kc_ref[0, 0, 0], qt)
    if n_lat:
        mx = qk(kl_ref[0, 0, 0], 0)
    m = jnp.max(sc, axis=0, keepdims=True)
    acc_scr[...] = _dot(vc_ref[0, 0, 0], jnp.exp2(sc - m).astype(BF16))
    if n_lat:

        per = next(p for p in (16, 8, 4, 2) if n_lat % p == 0)

        def body(i, carry):
            m, mx_cur = carry
            c = per * i
            for u in range(per):
                mx_next = qk(kl_ref[0, 0, jnp.minimum(c + u + 1, n_lat - 1)], (u + 1) % 2)
                m = sm_pv(u % 2, vl_ref[0, 0, c + u], m, mx_cur)
                mx_cur = mx_next
            return m, mx_cur

        lax.fori_loop(0, n_lat // per, body, (m, mx))
    acc = acc_scr[...]
    o_ref[0] = (acc[0:MLA_V] / acc[MLA_V:MLA_V + 1]).T.astype(o_ref.dtype)


def _rope_tables(n_lat):
    rows = n_lat // GRID_W
    row = jnp.repeat(jnp.arange(rows, dtype=F32), GRID_W)
    col = jnp.tile(jnp.arange(GRID_W, dtype=F32), rows)
    axis_dim = MLA_ROPE // 2
    inv_freq = 1.0 / (ROPE_THETA ** (jnp.arange(0, axis_dim, 2, dtype=F32) / axis_dim))
    ang_r = row[:, None] * inv_freq
    ang_c = col[:, None] * inv_freq
    cr, sr, cc, sc = jnp.cos(ang_r), jnp.sin(ang_r), jnp.cos(ang_c), jnp.sin(ang_c)
    cp = jnp.concatenate([cr, cr, cc, cc], axis=-1)
    sp = jnp.concatenate([-sr, sr, -sc, sc], axis=-1)
    return cp, sp


_ROPE_SWAP = np.concatenate([np.arange(16, 32), np.arange(0, 16), np.arange(48, 64), np.arange(32, 48)])


def _mla_side(x, mods, pre_g, w_in_ext, q_g, kv_g, w_uq_ext, w_ukv, tabs, b, n, tk, positional):
    m, d = x.shape
    ql, kvl = q_g.shape[-1], kv_g.shape[-1]
    nc = n // tk
    x_spec, mod_spec = _row_specs(tk, d, nc)
    tab_spec = pl.BlockSpec((tk, MLA_ROPE), lambda i: (i % nc, 0))
    qscale = (MLA_NOPE + MLA_ROPE) ** -0.5 * math.log2(math.e)
    return pl.pallas_call(
        functools.partial(_mla_proj_kernel, ql=ql, kvl=kvl, qscale=qscale, positional=positional),
        grid=(m // tk,),
        in_specs=[x_spec, mod_spec, _const_spec((1, d)), _const_spec(w_in_ext.shape),
                  _const_spec((1, ql)), _const_spec((1, kvl)), _const_spec(w_uq_ext.shape),
                  _const_spec(w_ukv.shape), tab_spec, tab_spec],
        out_specs=[pl.BlockSpec((1, MLA_HEADS, MLA_QK, tk), lambda i: (i // nc, 0, 0, i % nc)),
                   pl.BlockSpec((1, MLA_HEADS, 1, tk, MLA_QK), lambda i: (i // nc, 0, i % nc, 0, 0)),
                   pl.BlockSpec((1, MLA_HEADS, 1, MLA_VT, tk), lambda i: (i // nc, 0, i % nc, 0, 0))],
        out_shape=[jax.ShapeDtypeStruct((b, MLA_HEADS, MLA_QK, n), BF16),
                   jax.ShapeDtypeStruct((b, MLA_HEADS, nc, tk, MLA_QK), BF16),
                   jax.ShapeDtypeStruct((b, MLA_HEADS, nc, MLA_VT, tk), BF16)],
        compiler_params=_cp("parallel"), name="mla_proj",
    )(x, mods, pre_g.reshape(1, d), w_in_ext, q_g.reshape(1, ql), kv_g.reshape(1, kvl), w_uq_ext, w_ukv, *tabs)


def _flash_call(qt, kc, vtc, kl, vtl, tq):
    b, hh, _, n = qt.shape
    c = kc.shape[-2]
    n_lat = 0 if kl is None else kl.shape[2]
    in_specs = [pl.BlockSpec((1, 1, MLA_QK, tq), lambda bi, h, i: (bi, h, 0, i)),
                pl.BlockSpec((1, 1, 1, c, MLA_QK), lambda bi, h, i: (bi, h, 0, 0, 0)),
                pl.BlockSpec((1, 1, 1, MLA_VT, c), lambda bi, h, i: (bi, h, 0, 0, 0))]
    args = [qt, kc, vtc]
    scratch = [pltpu.VMEM((MLA_VT, tq), F32)]
    if n_lat:
        assert n_lat % 2 == 0, "latent key chunks are consumed in pairs"
        tk = kl.shape[-2]
        in_specs += [pl.BlockSpec((1, 1, n_lat, tk, MLA_QK), lambda bi, h, i: (bi, h, 0, 0, 0)),
                     pl.BlockSpec((1, 1, n_lat, MLA_VT, tk), lambda bi, h, i: (bi, h, 0, 0, 0))]
        args += [kl, vtl]
        scratch = [pltpu.VMEM((2, tk, tq), F32)] + scratch
    return pl.pallas_call(
        functools.partial(_flash_kernel, n_lat=n_lat), grid=(b, hh, n // tq),
        in_specs=in_specs,
        out_specs=pl.BlockSpec((1, tq, MLA_V), lambda bi, h, i: (bi, i, h)),
        out_shape=jax.ShapeDtypeStruct((b, n, hh * MLA_V), BF16),
        scratch_shapes=scratch,
        compiler_params=_cp("parallel", "parallel", "arbitrary"), name="flash",
    )(*args)


def _mla_layer(xl, xc, mods_l, mods_c, pre_g, fin, w_in, q_g, kv_g, w_uq, w_ukv, w_o, b, n, c):
    d = xl.shape[-1]
    ql, kvl = q_g.shape[-1], kv_g.shape[-1]
    hh = MLA_HEADS
    rope_cols = w_in[:, ql + kvl:]
    w_in_ext = jnp.concatenate([w_in, rope_cols[:, _ROPE_SWAP]], axis=1).astype(BF16)
    wq = w_uq.reshape(ql, hh, MLA_NOPE + MLA_ROPE)
    w_uq_ext = jnp.concatenate([wq, wq[:, :, MLA_NOPE:][:, :, _ROPE_SWAP]], axis=-1)
    w_uq_ext = w_uq_ext.reshape(ql, hh * MLA_QK).astype(BF16)
    w_ukv_b = w_ukv.astype(BF16)
    w_o_b = w_o.astype(BF16)

    tabs = _rope_tables(n)

    tm_l = min(ROW_TILE, n)
    tk_l = min(ROW_TILE, n // 2)
    ql_, kl, vtl = _mla_side(xl, mods_l, pre_g, w_in_ext, q_g, kv_g, w_uq_ext, w_ukv_b, tabs, b, n, tk_l, True)
    qc_, kc, vtc = _mla_side(xc, mods_c, pre_g, w_in_ext, q_g, kv_g, w_uq_ext, w_ukv_b, tabs, b, c, c, False)
    o_lat = _flash_call(ql_, kc, vtc, kl, vtl, min(2 * ROW_TILE, n)).reshape(b * n, hh * MLA_V)
    o_ctx = _flash_call(qc_, kc, vtc, None, None, c).reshape(b * c, hh * MLA_V)
    zb = jnp.zeros((d,), F32)
    xl = fin(xl, o_lat, mods_l, w_o_b, zb, tm_l, n)
    xc = fin(xc, o_ctx, mods_c, w_o_b, zb, c, c)
    return xl, xc


def _hy_in_kernel(x_ref, xp_ref, xn_ref, mod_ref, g_ref, w_ref, b_ref, cw_ref, cb_ref,
                  g0_ref, vg_ref, *, tpb):
    i = pl.program_id(0)
    g = g_ref[...]
    shift, scale = mod_ref[0, 0:1, :], mod_ref[0, 1:2, :]
    xcat = jnp.concatenate([xp_ref[...], x_ref[...], xn_ref[...]], axis=0)
    hcat = _normmod(xcat, g, shift, scale).astype(BF16)
    tm = x_ref.shape[0]
    d = g0_ref.shape[-1]
    first = (i % tpb) == 0
    last = (i % tpb) == tpb - 1
    ridx = lax.broadcasted_iota(jnp.int32, (tm, 1), 0)

    def conv_part(c):
        cols = slice(c * d, (c + 1) * d)
        ucat = _dot(hcat, w_ref[:, cols]) + b_ref[:, cols]
        u = ucat[8:tm + 8]
        prev_row = jnp.where(first, 0.0, ucat[7:8, :])
        next_row = jnp.where(last, 0.0, ucat[tm + 8:tm + 9, :])
        dn = jnp.where(ridx == 0, prev_row, pltpu.roll(u, 1, 0))
        upw = jnp.where(ridx == tm - 1, next_row, pltpu.roll(u, tm - 1, 0))
        return cb_ref[:, cols] + dn * cw_ref[0:1, cols] + u * cw_ref[1:2, cols] + upw * cw_ref[2:3, cols]

    g0_ref[...] = conv_part(0)
    vg_ref[...] = conv_part(2) * conv_part(1)


def _hy_in_call(x, mods, pre_g, w_in, b_in, conv_w, conv_b, tm, n):
    m, d = x.shape
    p = w_in.shape[1]
    tpb = n // tm
    x_spec, mod_spec = _row_specs(tm, d, tpb)
    r8 = tm // 8
    nb8 = m // 8
    prev_spec = pl.BlockSpec((8, d), lambda i: (jnp.maximum(i * r8 - 1, 0), 0))
    next_spec = pl.BlockSpec((8, d), lambda i: (jnp.minimum((i + 1) * r8, nb8 - 1), 0))
    cw = jnp.pad(conv_w, ((0, 8 - conv_w.shape[0]), (0, 0)))
    return pl.pallas_call(
        functools.partial(_hy_in_kernel, tpb=tpb), grid=(m // tm,),
        in_specs=[x_spec, prev_spec, next_spec, mod_spec, _const_spec((1, d)), _const_spec((d, p)),
                  _const_spec((1, p)), _const_spec((8, p)), _const_spec((1, p))],
        out_specs=[x_spec, x_spec],
        out_shape=[jax.ShapeDtypeStruct((m, d), F32), jax.ShapeDtypeStruct((m, d), F32)],
        compiler_params=_cp("parallel"), name="hy_in",
    )(x, x, x, mods, pre_g.reshape(1, d), w_in, b_in.reshape(1, p), cw, conv_b.reshape(1, p))


_PI_SPLIT = (3.140625, 9.67502593994140625e-4, 1.509957990978376432e-7)
_SIN_TAYLOR = (-1.0 / 6, 1.0 / 120, -1.0 / 5040, 1.0 / 362880, -1.0 / 39916800)


def _sin(x):
    kf = jnp.round(x * (1.0 / math.pi))
    r = ((x - kf * _PI_SPLIT[0]) - kf * _PI_SPLIT[1]) - kf * _PI_SPLIT[2]
    r2 = r * r
    p = _SIN_TAYLOR[4]
    for c in _SIN_TAYLOR[3::-1]:
        p = p * r2 + c
    s = r + r * r2 * p
    odd = (kf.astype(jnp.int32) & 1) == 1
    return jnp.where(odd, -s, s)


def _hy_filter_kernel(bands_ref, w1_ref, b1_ref, fq_ref, w2_ref, b2_ref, w3_ref, dl_ref,
                      k_ref, nrm_ref, *, n, tr):
    i = pl.program_id(0)
    bwd = i >= n // tr
    row = lax.broadcasted_iota(jnp.int32, (tr, LANE), 0) + i * tr
    j = jnp.where(bwd, 2 * n - row, row).astype(F32)
    lane = lax.broadcasted_iota(jnp.int32, (tr, LANE), 1)
    t = j * (1.0 / (n - 1))
    arg = (2.0 * math.pi / n) * j * bands_ref[0:1, :] + bands_ref[1:2, :]
    z = jnp.where(lane == 0, t, jnp.where(lane <= 2 * HYENA_BANDS, _sin(arg), 0.0))
    fq = fq_ref[...]
    a = _sin(fq * (_dot_x3(z, w1_ref[...]) + b1_ref[...]))
    for k in range(w2_ref.shape[0]):
        a = _sin(fq * (_dot_x3(a, w2_ref[k]) + b2_ref[k]))
    h = _dot_x3(a, w3_ref[jnp.where(bwd, 1, 0)])
    decay = jnp.exp(-t[:, 0:1] * dl_ref[...])
    k = h * decay
    k = jnp.where(row[:, 0:1] == n, 0.0, k)
    k_ref[...] = k
    part = jnp.sum(jnp.abs(k), axis=0, keepdims=True)

    @pl.when(i == 0)
    def _():
        nrm_ref[...] = jnp.zeros_like(nrm_ref)

    nrm_ref[...] += jnp.broadcast_to(part, nrm_ref.shape)


def _hy_filter_call(n, d, f_w1, f_b1, f_freq, f_w2, f_b2, f_w3):
    fw = f_w1.shape[1]
    tr = min(ROW_TILE, n)
    bands_np = np.zeros((8, LANE), np.float32)
    bands_np[0, 1:1 + HYENA_BANDS] = np.linspace(1e-4, HYENA_BANDS - 1, HYENA_BANDS, dtype=np.float32)
    bands_np[0, 1 + HYENA_BANDS:1 + 2 * HYENA_BANDS] = bands_np[0, 1:1 + HYENA_BANDS]
    bands_np[1, 1:1 + HYENA_BANDS] = 0.5 * np.pi
    bands_np[1, 1 + HYENA_BANDS:1 + 2 * HYENA_BANDS] = np.pi
    w1p = jnp.zeros((LANE, fw), F32).at[:f_w1.shape[0]].set(f_w1)
    deltas = jnp.abs(jnp.linspace(math.log(HYENA_TARGET) / HYENA_SLOW, math.log(HYENA_TARGET) / HYENA_FAST,
                                  d, dtype=F32)).reshape(1, d)
    row = pl.BlockSpec((tr, d), lambda i: (i, 0))
    return pl.pallas_call(
        functools.partial(_hy_filter_kernel, n=n, tr=tr), grid=(2 * n // tr,),
        in_specs=[_const_spec((8, LANE)), _const_spec((LANE, fw)), _const_spec((1, fw)), _const_spec((1, fw)),
                  _const_spec(f_w2.shape), _const_spec((f_w2.shape[0], 1, fw)), _const_spec((2, fw, d)),
                  _const_spec((1, d))],
        out_specs=[row, _const_spec((8, d))],
        out_shape=[jax.ShapeDtypeStruct((2 * n, d), F32), jax.ShapeDtypeStruct((8, d), F32)],
        compiler_params=_cp("arbitrary"), name="hy_filter",
    )(jnp.asarray(bands_np), w1p, f_b1.reshape(1, fw), f_freq.reshape(1, fw), f_w2,
      f_b2.reshape(f_w2.shape[0], 1, fw), jnp.transpose(f_w3.reshape(fw, 2, d), (1, 0, 2)), deltas)


def _dft_cs(nf, nt, period):
    ft = (np.arange(nf)[:, None] * np.arange(nt)[None, :]) % period
    ang = 2.0 * np.pi * ft / period
    return np.cos(ang), np.sin(ang)


def _twiddle_tables(n1, n2):
    nn = n1 * n2
    f1 = jnp.arange(n1, dtype=jnp.int32)
    t2 = jnp.arange(n2, dtype=jnp.int32)
    idx = (f1[:, None] * t2[None, :]) % nn
    ang = idx.astype(F32) * (2.0 * math.pi / nn)
    shape = idx.shape + (LANE,)
    return (jnp.broadcast_to(jnp.cos(ang)[..., None], shape),
            jnp.broadcast_to(jnp.sin(ang)[..., None], shape))


KRON_R = 16
KRON_H = 8
KRON_W = 256


def _cblock(mc):
    return np.block([[mc.real, -mc.imag], [mc.imag, mc.real]])


def _split_radix(n1):
    b = 16 if (n1 % 16 == 0 and n1 >= 64) else 4
    assert n1 % b == 0 and (n1 // b) % 2 == 0
    return n1 // b, b


def _slab_dft_mats(n1, a_in, a_out, sign, neg_im=False, real_in=False, scale=1.0):
    a, b = _split_radix(n1)
    r = KRON_H
    eye = np.eye(r)
    w = lambda num, den: np.exp(sign * 2j * np.pi * num / den)
    ua = np.arange(a)[:, None] * np.arange(a)[None, :]
    vb = np.arange(b)[:, None] * np.arange(b)[None, :]
    if sign < 0:
        m1 = np.kron(w(ua[:, :a_in], a) * scale, eye)
        l1 = _cblock(m1)
        if neg_im:
            l1[:, a_in * r:] *= -1.0
        if real_in:
            l1 = l1[:, :a_in * r]
        l2 = np.stack([_cblock(np.kron(w(vb, b) * w(u * np.arange(b)[None, :], n1), eye)) for u in range(a)])
    else:
        l1 = np.stack([_cblock(np.kron(w(vb, b) * w(u * np.arange(b)[:, None], n1), eye)) for u in range(a)])
        l2 = _cblock(np.kron(w(ua[:a_out, :], a) * scale, eye))
    return jnp.asarray(l1, BF16), jnp.asarray(l2, BF16)


def _kfa_kernel(*refs, nparts, a_in, a, b):
    l1_ref, l2_ref = refs[0], refs[1]
    parts = refs[2:2 + nparts]
    twc_ref, tws_ref, or_ref, oi_ref = refs[2 + nparts:]
    r = KRON_H
    halves = range(KRON_R // r)
    reps = or_ref.shape[-1] // LANE
    l1 = l1_ref[...]
    y1 = []
    for h in halves:
        rows = slice(h * r, (h + 1) * r)
        y1.append([_dot(l1, jnp.concatenate([p[0, aa * b + bb].astype(F32)[rows] for p in parts
                                             for aa in range(a_in)], axis=0).astype(BF16))
                   for bb in range(b)])
    for u in range(a):
        z = []
        for h in halves:
            x = jnp.concatenate([y1[h][bb][u * r:(u + 1) * r] for bb in range(b)]
                                + [y1[h][bb][(a + u) * r:(a + u + 1) * r] for bb in range(b)], axis=0)
            z.append(_dot(l2_ref[u], x.astype(BF16)))
        for v in range(b):
            f1 = u + a * v
            zr = jnp.concatenate([zh[v * r:(v + 1) * r] for zh in z], axis=0)
            zi = jnp.concatenate([zh[(b + v) * r:(b + v + 1) * r] for zh in z], axis=0)
            c = jnp.concatenate([twc_ref[f1]] * reps, axis=1)
            sn = jnp.concatenate([tws_ref[f1]] * reps, axis=1)
            or_ref[0, f1] = (zr * c + zi * sn).astype(or_ref.dtype)
            oi_ref[0, f1] = (zi * c - zr * sn).astype(oi_ref.dtype)


def _kfa_call(parts, nb_out, n1, n2, d, l1, l2, twc, tws):
    a, b = _split_radix(n1)
    a_in = parts[0][0].shape[1] // b
    r, w = KRON_R, min(KRON_W, d)
    in_specs = [_const_spec(l1.shape), pl.BlockSpec(l2.shape, lambda bi, j, k: (0, 0, 0), pipeline_mode=pl.Buffered(1))]
    args = [l1, l2]
    for arr, bi_fixed in parts:
        t1_in = arr.shape[1]
        if bi_fixed is None:
            in_specs.append(pl.BlockSpec((1, t1_in, r, w), lambda bi, j, k: (bi, 0, j, k)))
        else:
            in_specs.append(pl.BlockSpec((1, t1_in, r, w), lambda bi, j, k, f=bi_fixed: (f, 0, j, k)))
        args.append(arr)
    tw_spec = pl.BlockSpec((n1, r, LANE), lambda bi, j, k: (0, j, 0))
    out_spec = pl.BlockSpec((1, n1, r, w), lambda bi, j, k: (bi, 0, j, k))
    out = jax.ShapeDtypeStruct((nb_out, n1, n2, d), BF16)
    return pl.pallas_call(
        functools.partial(_kfa_kernel, nparts=len(parts), a_in=a_in, a=a, b=b),
        grid=(nb_out, n2 // r, d // w),
        in_specs=in_specs + [tw_spec, tw_spec], out_specs=[out_spec, out_spec], out_shape=[out, out],
        compiler_params=_cp("parallel", "parallel", "parallel"), name="kfa",
    )(*args, twc, tws)


def _kfc_kernel(l3_ref, l4_ref, gr_ref, gi_ref, vg_ref, g0_ref, skip_ref, o_ref, *, a, b, a_out):
    r = KRON_H
    halves = range(KRON_R // r)
    y3 = [[] for _ in halves]
    for u in range(a):
        gr = [gr_ref[0, u + a * v].astype(F32) for v in range(b)]
        gi = [gi_ref[0, u + a * v].astype(F32) for v in range(b)]
        for h in halves:
            rows = slice(h * r, (h + 1) * r)
            x = jnp.concatenate([g[rows] for g in gr] + [g[rows] for g in gi], axis=0).astype(BF16)
            y3[h].append(_dot(l3_ref[u], x))
    l4 = l4_ref[...]
    skip = skip_ref[...]
    for bb in range(b):
        y = []
        for h in halves:
            x = jnp.concatenate([y3[h][u][bb * r:(bb + 1) * r] for u in range(a)]
                                + [y3[h][u][(b + bb) * r:(b + bb + 1) * r] for u in range(a)], axis=0)
            y.append(_dot(l4, x.astype(BF16)))
        for sg in range(2):
            for aa in range(a_out):
                t1 = aa * b + bb
                yb = jnp.concatenate([yh[(sg * a_out + aa) * r:(sg * a_out + aa + 1) * r] for yh in y], axis=0)
                o_ref[sg, t1] = ((yb + vg_ref[sg, t1] * skip) * g0_ref[sg, t1]).astype(o_ref.dtype)


def _kfc_call(g1r, g1i, vg4, g04, skip, n1, n2, d, l3, l4):
    a, b = _split_radix(n1)
    nb, t1_out = vg4.shape[:2]
    r, w = KRON_R, min(KRON_W, d)
    slab = pl.BlockSpec((1, n1, r, w), lambda j, k: (0, 0, j, k))
    nat = pl.BlockSpec((nb, t1_out, r, w), lambda j, k: (0, 0, j, k))
    return pl.pallas_call(
        functools.partial(_kfc_kernel, a=a, b=b, a_out=t1_out // b), grid=(n2 // r, d // w),
        in_specs=[pl.BlockSpec(l3.shape, lambda j, k: (0, 0, 0), pipeline_mode=pl.Buffered(1)),
                  _const_spec(l4.shape), slab, slab, nat, nat, pl.BlockSpec((1, w), lambda j, k: (0, k))],
        out_specs=nat, out_shape=jax.ShapeDtypeStruct(vg4.shape, BF16),
        compiler_params=_cp("parallel", "parallel"), name="kfc",
    )(l3, l4, g1r, g1i, vg4, g04, skip.reshape(1, d))


def _hy_b_kernel(fm_ref, fmc_ref, xr_ref, xi_ref, fr_ref, fi_ref, sc_ref, twc_ref, tws_ref,
                 or_ref, oi_ref, *, nf, d):
    fm = fm_ref[...]
    fmc = fmc_ref[...]
    n2 = xr_ref.shape[1]
    sc = sc_ref[...]
    for s in range(nf):
        x = _dot(fm, jnp.concatenate([xr_ref[s], xi_ref[s]], axis=0))
        kk = _dot(fm, jnp.concatenate([fr_ref[s], fi_ref[s]], axis=0))
        xr, xi = x[:n2], x[n2:]
        kr, ki = kk[:n2] * sc, kk[n2:] * sc
        yr = (xr * kr - xi * ki).astype(BF16)
        yi = (xr * ki + xi * kr).astype(BF16)
        g = _dot(fmc, jnp.concatenate([yr, yi], axis=0))
        gr, gi = g[:n2], g[n2:]
        c = jnp.concatenate([twc_ref[s]] * (d // LANE), axis=1)
        sn = jnp.concatenate([tws_ref[s]] * (d // LANE), axis=1)
        or_ref[s] = (gr * c - gi * sn).astype(or_ref.dtype)
        oi_ref[s] = (gi * c + gr * sn).astype(oi_ref.dtype)


def _block_c(cs, sn, sign):
    return np.block([[cs, -sign * sn], [sign * sn, cs]])


def _hy_conv_long(vg, g0, k, nrm, skip, b, n, d):
    n2 = FFT_N2
    nn = 2 * n
    n1 = nn // n2
    rows = n // n2
    a, _ = _split_radix(n1)
    la1, la2 = _slab_dft_mats(n1, a // 2, 0, -1)
    lk1, lk2 = _slab_dft_mats(n1, a, 0, -1, real_in=True)
    lc3, lc4 = _slab_dft_mats(n1, 0, a // 2, +1)
    cs2, sn2 = _dft_cs(n2, n2, n2)
    fm_b = jnp.asarray(_block_c(cs2, sn2, -1.0), BF16)
    fm_bc = jnp.asarray(_block_c(cs2, sn2, 1.0), BF16)
    twc, tws = _twiddle_tables(n1, n2)

    kfr, kfi = _kfa_call([(k.reshape(1, n1, n2, d), 0)], 1, n1, n2, d, lk1, lk2, twc, tws)
    nf = 4 if n1 % 4 == 0 else 1
    scale = (1.0 / (nrm[0:1, :] * nn))
    slab = pl.BlockSpec((nf, n2, d), lambda j: (j, 0, 0))
    shp3 = (n1, n2, d)
    vg4 = vg.reshape(b, rows, n2, d)
    g04 = g0.reshape(b, rows, n2, d)
    x1r, x1i = _kfa_call([(vg4, 0), (vg4, 1)], 1, n1, n2, d, la1, la2, twc, tws)
    tw_slab = pl.BlockSpec((nf, n2, LANE), lambda j: (j, 0, 0))
    g1r, g1i = pl.pallas_call(
        functools.partial(_hy_b_kernel, nf=nf, d=d), grid=(n1 // nf,),
        in_specs=[_const_spec(fm_b.shape), _const_spec(fm_bc.shape), slab, slab, slab, slab, _const_spec((1, d)),
                  tw_slab, tw_slab],
        out_specs=[slab, slab],
        out_shape=[jax.ShapeDtypeStruct(shp3, BF16)] * 2,
        compiler_params=_cp("parallel"), name="hy_b",
    )(fm_b, fm_bc, x1r.reshape(shp3), x1i.reshape(shp3), kfr.reshape(shp3), kfi.reshape(shp3), scale, twc, tws)
    out = _kfc_call(g1r.reshape(1, n1, n2, d), g1i.reshape(1, n1, n2, d), vg4, g04, skip, n1, n2, d, lc3, lc4)
    return out.reshape(b * n, d)


def _hy_short_kernel(fa_ref, fk_ref, fi_ref, vg_ref, g0_ref, k_ref, nrm_ref, skip_ref, o_ref, *, n):
    z = jnp.concatenate([vg_ref[0], vg_ref[1]], axis=0)
    x = _dot_hi(fa_ref[...], z)
    kk = _dot_hi(fk_ref[...], k_ref[...])
    nn = 2 * n
    sc = 1.0 / (nrm_ref[0:1, :] * nn)
    xr, xi = x[:nn], x[nn:]
    kr, ki = kk[:nn] * sc, kk[nn:] * sc
    y = _dot_hi(fi_ref[...], jnp.concatenate([xr * kr - xi * ki, xr * ki + xi * kr], axis=0))
    skip = skip_ref[...]
    for bi in range(2):
        o_ref[bi] = ((y[bi * n:(bi + 1) * n] + vg_ref[bi] * skip) * g0_ref[bi]).astype(o_ref.dtype)


def _hy_conv_short(vg, g0, k, nrm, skip, b, n, d):
    nn = 2 * n
    cs, sn = _dft_cs(nn, n, nn)
    fa = jnp.asarray(_block_c(cs, sn, -1.0), F32)
    csk, snk = _dft_cs(nn, nn, nn)
    fk = jnp.asarray(np.concatenate([csk, -snk], axis=0), F32)
    csi, sni = _dft_cs(n, nn, nn)
    fi = jnp.asarray(_block_c(csi, sni, 1.0), F32)
    cb = 256
    col3 = pl.BlockSpec((b, n, cb), lambda j: (0, 0, j))
    vec = pl.BlockSpec((1, cb), lambda j: (0, j))
    out = pl.pallas_call(
        functools.partial(_hy_short_kernel, n=n), grid=(d // cb,),
        in_specs=[_const_spec(fa.shape), _const_spec(fk.shape), _const_spec(fi.shape), col3, col3,
                  pl.BlockSpec((nn, cb), lambda j: (0, j)), pl.BlockSpec((8, cb), lambda j: (0, j)), vec],
        out_specs=col3, out_shape=jax.ShapeDtypeStruct((b, n, d), BF16),
        compiler_params=_cp("parallel"), name="hy_short",
    )(fa, fk, fi, vg.reshape(b, n, d), g0.reshape(b, n, d), k, nrm, skip.reshape(1, d))
    return out.reshape(b * n, d)


def _hyena_layer(xl, xc, mods_l, mods_c, pre_g, fin, w_in, b_in, conv_w, conv_b, filt, skip,
                 w_out, b_out, b, n, c):
    d = xl.shape[-1]
    w_in_b = w_in.astype(BF16)
    w_out_b = w_out.astype(BF16)
    tm = min(ROW_TILE, n)
    g0, vg = _hy_in_call(xl, mods_l, pre_g, w_in_b, b_in, conv_w, conv_b, tm, n)
    k, nrm = _hy_filter_call(n, d, *filt)
    u_out = _hy_conv_long(vg, g0, k, nrm, skip[0], b, n, d)
    xl, (hl,) = fin(xl, u_out, mods_l, w_out_b, b_out, tm, n)

    g0c, vgc = _hy_in_call(xc, mods_c, pre_g, w_in_b, b_in, conv_w, conv_b, c, c)
    kc, nrmc = _hy_filter_call(c, d, *filt)
    u_out_c = _hy_conv_short(vgc, g0c, kc, nrmc, skip[0], b, c, d)
    xc, (hc,) = fin(xc, u_out_c, mods_c, w_out_b, b_out, c, c, is_ctx=True)
    return xl, hl, hc


def _s5_operators(lam_re, lam_im, log_dt, b_re, b_im, c_re, c_im, d_skip):
    t = S5_T
    g, ns = lam_re.shape[1], lam_re.shape[2]
    gc = b_re.shape[-1]
    gl = LANE // gc
    nblk = g // gl
    lam = lax.complex(lam_re, lam_im)
    dt = jnp.exp(log_dt)[..., None]
    lam_bar = jnp.exp(lam * dt)
    b_bar = ((lam_bar - 1.0) / lam)[..., None] * lax.complex(b_re, b_im)
    c_mat = lax.complex(c_re, c_im)
    pw = jnp.arange(t + 1, dtype=F32)
    lam_pw = jnp.exp((lam * dt)[None] * pw[:, None, None, None])
    hp = HIGHEST
    kern = jnp.einsum('dgcn,tdgn,dgne->dgtce', c_mat, lam_pw[:t], b_bar, precision=hp).real
    dsk = d_skip.reshape(g, gc)
    kt = jnp.swapaxes(kern, -1, -2)
    centre = kt[0][:, 0] + kt[1][:, 0] + jnp.eye(gc, dtype=F32)[None] * dsk[:, :, None]
    ks = jnp.concatenate([kt[1][:, 1:][:, ::-1], centre[:, None], kt[0][:, 1:]], axis=1)
    ks = jnp.transpose(ks.reshape(nblk, gl, 2 * t - 1, gc, gc), (0, 2, 1, 3, 4)).reshape(nblk, 2 * t - 1, LANE, gc)
    same_group = (np.arange(LANE)[:, None] // gc == np.arange(LANE)[None, :] // gc).astype(np.float32)
    d_tab = jnp.tile(ks, (1, 1, 1, gl)) * same_group

    def compact(zc, im_sign):
        z = jnp.concatenate([zc.real, im_sign * zc.imag], axis=-1)
        z = jnp.transpose(z, (2, 1, 0, 3, 4)).reshape(nblk, gl, t, 2, gc, 2 * ns)
        return jnp.transpose(z, (0, 2, 1, 3, 4, 5))

    b_t = jnp.swapaxes(b_bar, -1, -2)
    pf = lam_pw[:t][::-1][:, 0, :, None, :] * b_t[0][None]
    pb = lam_pw[:t][:, 1, :, None, :] * b_t[1][None]
    p_tab = compact(jnp.stack([pf, pb], axis=0), 1.0)
    qf = c_mat[0][None] * lam_pw[1:t + 1, 0][:, :, None, :]
    qb = c_mat[1][None] * lam_pw[1:t + 1][::-1][:, 1][:, :, None, :]
    q_tab = compact(jnp.stack([qf, qb], axis=0), -1.0)

    a = lam_pw[t]
    m_op, p_op, q_op = _s5_expand(d_tab, p_tab, q_tab)
    return m_op, p_op, q_op, a.real.reshape(2, g * ns), a.imag.reshape(2, g * ns)


def _s5_m_kernel(d_ref, o_ref, *, t):
    for s in range(t):
        for tt in range(t):
            o_ref[0, s * LANE:(s + 1) * LANE, tt * LANE:(tt + 1) * LANE] = d_ref[0, tt - s + t - 1].astype(o_ref.dtype)


def _s5_pq_kernel(c_ref, o_ref, *, transpose):
    t, gl, nd, gc, w = c_ref.shape[1:]
    ns = w // 2
    lane_grp = lax.broadcasted_iota(jnp.int32, (gc, gl * ns), 1) // ns
    for j in range(t):
        rows = []
        for g in range(gl):
            cols = []
            for dd in range(nd):
                piece = c_ref[0, j, g, dd]
                for ri in range(2):
                    tiled = jnp.concatenate([piece[:, ri * ns:(ri + 1) * ns]] * gl, axis=1)
                    cols.append(jnp.where(lane_grp == g, tiled, 0.0))
            rows.append(jnp.concatenate(cols, axis=1))
        blk = jnp.concatenate(rows, axis=0)
        if transpose:
            o_ref[0, :, j * LANE:(j + 1) * LANE] = blk.T.astype(o_ref.dtype)
        else:
            o_ref[0, j * LANE:(j + 1) * LANE, :] = blk.astype(o_ref.dtype)


def _s5_expand(d_tab, p_tab, q_tab):
    nblk, nlag = d_tab.shape[:2]
    t = (nlag + 1) // 2
    _, _, gl, nd, gc, w = p_tab.shape
    ncol = nd * gl * w
    whole = lambda shape: pl.BlockSpec((1,) + shape, lambda b: (b,) + (0,) * len(shape))
    m_op = pl.pallas_call(
        functools.partial(_s5_m_kernel, t=t), grid=(nblk,),
        in_specs=[whole((nlag, LANE, LANE))], out_specs=whole((t * LANE, t * LANE)),
        out_shape=jax.ShapeDtypeStruct((nblk, t * LANE, t * LANE), BF16),
        compiler_params=_cp("parallel"), name="s5_m_op",
    )(d_tab)
    tab_spec = whole((t, gl, nd, gc, w))
    p_op = pl.pallas_call(
        functools.partial(_s5_pq_kernel, transpose=False), grid=(nblk,),
        in_specs=[tab_spec], out_specs=whole((t * LANE, ncol)),
        out_shape=jax.ShapeDtypeStruct((nblk, t * LANE, ncol), BF16),
        compiler_params=_cp("parallel"), name="s5_p_op",
    )(p_tab)
    q_op = pl.pallas_call(
        functools.partial(_s5_pq_kernel, transpose=True), grid=(nblk,),
        in_specs=[tab_spec], out_specs=whole((ncol, t * LANE)),
        out_shape=jax.ShapeDtypeStruct((nblk, ncol, t * LANE), BF16),
        compiler_params=_cp("parallel"), name="s5_q_op",
    )(q_tab)
    return m_op, p_op, q_op


def _s5_sum_kernel(*refs, t):
    u_refs = refs[:t]
    p_ref = refs[t]
    outs = refs[t + 1:]
    u = jnp.concatenate([r[...] for r in u_refs], axis=1)
    res = _dot(u, p_ref[0])
    w = res.shape[1] // len(outs)
    for i, o in enumerate(outs):
        o[...] = res[:, i * w:(i + 1) * w]


def _s5_sum_call(h, p_op, rb):
    rows = h.shape[0]
    t = S5_T
    nblk = p_op.shape[0]
    w = p_op.shape[2] // 4
    u_specs = [pl.BlockSpec((rb, LANE), lambda gb, r, s=s: (r, s * nblk + gb)) for s in range(t)]
    out_spec = pl.BlockSpec((rb, w), lambda gb, r: (r, gb))
    return pl.pallas_call(
        functools.partial(_s5_sum_kernel, t=t), grid=(nblk, rows // rb),
        in_specs=u_specs + [pl.BlockSpec((1,) + p_op.shape[1:], lambda gb, r: (gb, 0, 0))],
        out_specs=[out_spec] * 4,
        out_shape=[jax.ShapeDtypeStruct((rows, nblk * w), F32)] * 4,
        compiler_params=_cp("parallel", "parallel"), name="s5_sum",
    )(*([h] * t), p_op)


def _s5_rec_kernel(sr_ref, si_ref, ar_ref, ai_ref, h0r_ref, h0i_ref, hr_ref, hi_ref, fr_ref, fi_ref,
                   cr, ci, *, kb, reverse):
    @pl.when(pl.program_id(1) == 0)
    def _():
        cr[...] = h0r_ref[0]
        ci[...] = h0i_ref[0]

    ar, ai = ar_ref[...], ai_ref[...]

    def body(i, carry):
        hr, hi = carry
        k = kb - 1 - i if reverse else i
        hr_ref[pl.ds(k, 1), :] = hr
        hi_ref[pl.ds(k, 1), :] = hi
        nr = ar * hr - ai * hi + sr_ref[pl.ds(k, 1), :]
        ni = ar * hi + ai * hr + si_ref[pl.ds(k, 1), :]
        return nr, ni

    hr, hi = lax.fori_loop(0, kb, body, (cr[...], ci[...]))
    cr[...] = hr
    ci[...] = hi
    fr_ref[0] = hr
    fi_ref[0] = hi


def _s5_rec_call(sr, si, ar, ai, h0r, h0i, nb_batch, reverse):
    rows, w = sr.shape
    nk = rows // nb_batch
    kb = min(64, nk)
    nb = nk // kb
    blk = (lambda bi, i: (bi * nb + nb - 1 - i, 0)) if reverse else (lambda bi, i: (bi * nb + i, 0))
    row_spec = pl.BlockSpec((kb, w), blk)
    vec = _const_spec((1, w))
    st = pl.BlockSpec((1, 1, w), lambda bi, i: (bi, 0, 0))
    return pl.pallas_call(
        functools.partial(_s5_rec_kernel, kb=kb, reverse=reverse), grid=(nb_batch, nb),
        in_specs=[row_spec, row_spec, vec, vec, st, st],
        out_specs=[row_spec, row_spec, st, st],
        out_shape=[jax.ShapeDtypeStruct((rows, w), F32)] * 2 + [jax.ShapeDtypeStruct((nb_batch, 1, w), F32)] * 2,
        scratch_shapes=[pltpu.VMEM((1, w), F32), pltpu.VMEM((1, w), F32)],
        compiler_params=_cp("parallel", "arbitrary"), name="s5_rec",
    )(sr, si, ar, ai, h0r, h0i)


def _s5_out_kernel(*refs, t):
    u_refs = refs[:t]
    h_refs = refs[t:t + 4]
    m_ref, q_ref, o_ref, tok_scr = refs[t + 4:]
    u = jnp.concatenate([r[...] for r in u_refs], axis=1)
    hcat = jnp.concatenate([r[...].astype(BF16) for r in h_refs], axis=1)
    y = _dot(u, m_ref[0]) + _dot(hcat, q_ref[0])
    g = 0.5 * y * (1.0 + lax.erf(y * (2.0 ** -0.5)))
    rb = g.shape[0]
    for j in range(t):
        tok_scr[pl.ds(j, rb, stride=t), :] = g[:, j * LANE:(j + 1) * LANE]
    o_ref[...] = tok_scr[...].astype(o_ref.dtype)


def _s5_out_call(h, states, m_op, q_op, rb):
    rows = h.shape[0]
    t = S5_T
    nblk = m_op.shape[0]
    w = q_op.shape[1] // 4
    u_specs = [pl.BlockSpec((rb, LANE), lambda gb, r, s=s: (r, s * nblk + gb)) for s in range(t)]
    return pl.pallas_call(
        functools.partial(_s5_out_kernel, t=t), grid=(nblk, rows // rb),
        in_specs=u_specs + [pl.BlockSpec((rb, w), lambda gb, r: (r, gb))] * 4
        + [pl.BlockSpec((1,) + m_op.shape[1:], lambda gb, r: (gb, 0, 0)),
           pl.BlockSpec((1,) + q_op.shape[1:], lambda gb, r: (gb, 0, 0))],
        out_specs=pl.BlockSpec((rb * t, LANE), lambda gb, r: (r, gb)),
        out_shape=jax.ShapeDtypeStruct((rows * t, nblk * LANE), BF16),
        scratch_shapes=[pltpu.VMEM((rb * t, LANE), F32)],
        compiler_params=_cp("parallel", "parallel"), name="s5_out",
    )(*([h] * t), *states, m_op, q_op)


def _s5_layer(xl, hl, hc, mods_l, fin, lam_re, lam_im, log_dt, b_re, b_im, c_re, c_im,
              d_skip, w_glu, b_glu, b, n, c):
    t = S5_T
    m_op, p_op, q_op, a_re, a_im = _s5_operators(lam_re, lam_im, log_dt, b_re, b_im, c_re, c_im, d_skip)
    w = a_re.shape[-1]

    def scan(h, init):
        sfr, sfi, sbr, sbi = _s5_sum_call(h, p_op, min(ROW_TILE, h.shape[0]))
        hfr, hfi, ffr, ffi = _s5_rec_call(sfr, sfi, a_re[0:1], a_im[0:1], init[0], init[1], b, False)
        hbr, hbi, fbr, fbi = _s5_rec_call(sbr, sbi, a_re[1:2], a_im[1:2], init[2], init[3], b, True)
        return (hfr, hfi, hbr, hbi), (ffr, ffi, fbr, fbi)

    zeros = jnp.zeros((b, 1, w), F32)
    _, ctx_final = scan(hc, (zeros,) * 4)
    states, _ = scan(hl, ctx_final)
    nk = n // t
    g_nat = _s5_out_call(hl, states, m_op, q_op, min(ROW_TILE, b * nk))
    return fin(xl, g_nat, mods_l, w_glu.astype(BF16), b_glu, min(ROW_TILE, n), n, glu=True)


def _fnet_channel_mats():
    cc, sc = _dft_cs(FNET_GC, FNET_GC, FNET_GC)
    return jnp.asarray(np.concatenate([cc, sc], axis=1) / np.sqrt(FNET_GC), BF16)


def _fn_c_kernel(l5_ref, l6_ref, xr_ref, xi_ref, o_ref, *, nh):
    r = KRON_R
    n2 = nh * r
    l5 = l5_ref[...]
    y5 = [_dot(l5, jnp.concatenate([xr_ref[0, f], xi_ref[0, f]], axis=0)).astype(BF16) for f in range(r)]
    l6 = l6_ref[...]
    for p in range(nh):
        x = jnp.concatenate([y5[f][p * r:(p + 1) * r] for f in range(r)]
                            + [y5[f][n2 + p * r:n2 + (p + 1) * r] for f in range(r)], axis=0)
        out = _dot(l6, x)
        for q in range(r):
            o_ref[0, p + nh * q] = out[q * r:(q + 1) * r].astype(o_ref.dtype)


def _fnet_layer(xl, am, bm, mods_l, fin, w_o, b_o, b, n, d):
    n2 = FFT_N2
    n1 = n // n2
    tm = min(ROW_TILE, n)
    a, _ = _split_radix(n1)
    l1, l2 = _slab_dft_mats(n1, a, 0, -1, neg_im=True, scale=1.0 / np.sqrt(n))
    twc, tws = _twiddle_tables(n1, n2)
    a4, b4 = am.reshape(b, n1, n2, d), bm.reshape(b, n1, n2, d)
    xr, xi = _kfa_call([(a4, None), (b4, None)], b, n1, n2, d, l1, l2, twc, tws)

    r, w = KRON_R, min(KRON_W, d)
    nh = n2 // r
    assert n1 % r == 0 and n2 % r == 0
    m5 = np.zeros((n2, n2), np.complex128)
    m6 = np.zeros((r * r, r * r), np.complex128)
    for s in range(r):
        for p in range(nh):
            for h in range(nh):
                m5[p * r + s, h * r + s] = np.exp(-2j * np.pi * (p * h / nh + p * s / n2))
        for q in range(r):
            for f in range(r):
                m6[q * r + f, f * r + s] = np.exp(-2j * np.pi * q * s / r)
    l5 = jnp.asarray(_cblock(m5), BF16)
    l6 = jnp.asarray(np.concatenate([m6.real, -m6.imag], axis=1), BF16)
    grp = pl.BlockSpec((1, r, n2, w), lambda bi, fh, k: (bi, fh, 0, k))
    y = pl.pallas_call(
        functools.partial(_fn_c_kernel, nh=nh), grid=(b, n1 // r, d // w),
        in_specs=[_const_spec(l5.shape), _const_spec(l6.shape), grp, grp],
        out_specs=pl.BlockSpec((1, n2, None, r, w), lambda bi, fh, k: (bi, 0, fh, 0, k)),
        out_shape=jax.ShapeDtypeStruct((b, n2, n1 // r, r, d), BF16),
        compiler_params=_cp("parallel", "parallel", "parallel"), name="fn_c",
    )(l5, l6, xr, xi)
    return fin(xl, y.reshape(b * n, d), mods_l, w_o.astype(BF16), b_o, tm, n)


def kernel(x, c, ctx, c_ctx, mod_w, mod_b, mix_pre_g, mix_post_g, ffn_pre_g, ffn_post_g, ffn_w13, ffn_w2,
           mla_w_in, mla_q_norm_g, mla_kv_norm_g, mla_w_uq, mla_w_ukv, mla_w_o,
           hy_w_in, hy_b_in, hy_conv_w, hy_conv_b, hy_f_w1, hy_f_b1, hy_f_freq, hy_f_w2, hy_f_b2, hy_f_w3,
           hy_skip, hy_w_out, hy_b_out,
           s5_lambda_re, s5_lambda_im, s5_log_dt, s5_b_re, s5_b_im, s5_c_re, s5_c_im, s5_d, s5_w_glu, s5_b_glu,
           fn_w_o, fn_b_o):
    b, n, d = x.shape
    cl = ctx.shape[1]
    depth = mod_w.shape[0]
    assert b == 2 and depth == 4, "two batches ride one complex transform; one layer per mixer"
    mods = _mods(c, c_ctx, mod_w, mod_b)
    xl = x.reshape(b * n, d)
    xc = ctx.reshape(b * cl, d)

    w13_all = ffn_w13.astype(BF16)
    w2_all = ffn_w2.astype(BF16)

    def mods_c(i):
        return jnp.broadcast_to(mods[i, 2:3], (b, 8, d))

    def finisher(i, emit_kind=None, cs=None):
        def fin(x_, y_, mods_, wm, bm, tm_, rows_per_batch, glu=False, is_ctx=False):
            emit = None
            if emit_kind is not None:
                emit = (emit_kind, mods_c(i + 1) if is_ctx else mods[i + 1, 0:2], mix_pre_g[i + 1], cs)
            return _mix_ffn_call(x_, y_, mods_, wm, bm, mix_post_g[i], ffn_pre_g[i], ffn_post_g[i],
                                 w13_all, w2_all, i, tm_, rows_per_batch, glu, emit)
        return fin

    xl, xc = _mla_layer(xl, xc, mods[0, 0:2], mods_c(0), mix_pre_g[0], finisher(0), mla_w_in[0],
                        mla_q_norm_g[0], mla_kv_norm_g[0], mla_w_uq[0], mla_w_ukv[0], mla_w_o[0], b, n, cl)
    filt = (hy_f_w1[0], hy_f_b1[0], hy_f_freq[0], hy_f_w2[0], hy_f_b2[0], hy_f_w3[0])
    xl, hl, hc = _hyena_layer(xl, xc, mods[1, 0:2], mods_c(1), mix_pre_g[1], finisher(1, "chunks"), hy_w_in[0],
                              hy_b_in[0], hy_conv_w[0], hy_conv_b[0], filt, hy_skip[0], hy_w_out[0], hy_b_out[0],
                              b, n, cl)
    xl, (am, bm) = _s5_layer(xl, hl, hc, mods[2, 0:2], finisher(2, "fnet", _fnet_channel_mats()), s5_lambda_re[0],
                             s5_lambda_im[0], s5_log_dt[0], s5_b_re[0], s5_b_im[0], s5_c_re[0], s5_c_im[0],
                             s5_d[0], s5_w_glu[0], s5_b_glu[0], b, n, cl)
    xl = _fnet_layer(xl, am, bm, mods[3, 0:2], finisher(3), fn_w_o[0], fn_b_o[0], b, n, d)
    return xl.reshape(b, n, d)
```

```python
import functools
import math

import numpy as np
import jax
import jax.numpy as jnp
from jax import lax
from jax.experimental import pallas as pl
from jax.experimental.pallas import tpu as pltpu

F32 = jnp.float32
BF16 = jnp.bfloat16
NORM_EPS = 1e-6
LANE = 128
MXU_COLS = 256
ROW_TILE = 512
VMEM_LIMIT = 56 * 1024 * 1024
HIGHEST = lax.Precision.HIGHEST

GRID_W = 64
ROPE_THETA = 10000.0
MLA_HEADS = 8
MLA_NOPE = 128
MLA_ROPE = 64
MLA_V = 128
MLA_VT = MLA_V + 16
MLA_QK = MLA_NOPE + 2 * MLA_ROPE
HYENA_BANDS = 16
HYENA_TARGET = 1e-2
HYENA_FAST = 0.3
HYENA_SLOW = 1.5
S5_T = 16
FNET_GC = 128
FFT_N2 = 128


def _cp(*sem):
    return pltpu.CompilerParams(dimension_semantics=sem, vmem_limit_bytes=VMEM_LIMIT)


def _dot(a, b):
    return jnp.dot(a, b, preferred_element_type=F32)


def _dot_hi(a, b):
    return jnp.dot(a, b, preferred_element_type=F32, precision=HIGHEST)


def _dot_x3(a, b):
    a_hi = a.astype(BF16)
    b_hi = b.astype(BF16)
    a_lo = (a - a_hi.astype(F32)).astype(BF16)
    b_lo = (b - b_hi.astype(F32)).astype(BF16)
    return _dot(a_hi, b_hi) + (_dot(a_hi, b_lo) + _dot(a_lo, b_hi))


def _rms(x, g):
    ms = jnp.mean(x * x, axis=-1, keepdims=True)
    return x * lax.rsqrt(ms + NORM_EPS) * g


def _normmod(x, g, shift, scale):
    return _rms(x, g) * (1.0 + scale) + shift


def _const_spec(shape):
    nd = len(shape)
    return pl.BlockSpec(shape, lambda *_: (0,) * nd)


def _mods_kernel(st_ref, w_ref, b_ref, o_ref):
    st = st_ref[...]
    st = st * jax.nn.sigmoid(st)
    w = w_ref[0]
    rows = [jnp.sum(st[:, r:r + 1] * w, axis=0, keepdims=True) for r in range(3)]
    rows.append(jnp.zeros((5, w.shape[1]), F32))
    o_ref[0] = jnp.concatenate(rows, axis=0) + b_ref[0]


def _mods(c, c_ctx, mod_w, mod_b):
    depth, d, n6 = mod_w.shape
    st = jnp.zeros((d, 8), F32).at[:, 0:2].set(c.T).at[:, 2].set(c_ctx)
    tn = 1024
    out = pl.pallas_call(
        _mods_kernel,
        grid=(depth, n6 // tn),
        in_specs=[_const_spec((d, 8)),
                  pl.BlockSpec((1, d, tn), lambda i, j: (i, 0, j)),
                  pl.BlockSpec((1, 1, tn), lambda i, j: (i, 0, j))],
        out_specs=pl.BlockSpec((1, 8, tn), lambda i, j: (i, 0, j)),
        out_shape=jax.ShapeDtypeStruct((depth, 8, n6), F32),
        compiler_params=_cp("parallel", "parallel"),
        name="mods",
    )(st, mod_w, mod_b.reshape(depth, 1, n6))
    m = out[:, :3].reshape(depth, 3, n6 // d, d)
    return jnp.pad(m, ((0, 0), (0, 0), (0, 8 - n6 // d), (0, 0)))


def _row_specs(tm, d, tpb):
    x_spec = pl.BlockSpec((tm, d), lambda i: (i, 0))
    mod_spec = pl.BlockSpec((1, 8, d), lambda i: (i // tpb, 0, 0))
    return x_spec, mod_spec


def _channel_dft(h, cs):
    gc = FNET_GC
    ab = [_dot(h[:, k * gc:(k + 1) * gc], cs) for k in range(h.shape[1] // gc)]
    return (jnp.concatenate([z[:, :gc] for z in ab], axis=1), jnp.concatenate([z[:, gc:] for z in ab], axis=1))


def _mix_ffn_kernel(x_ref, y_ref, mod_ref, wm_ref, bm_ref, gm_ref, pre_ref, post_ref, w13_ref, w2_ref, *rest,
                    f, cuts, glu, emit):
    if emit == "chunks":
        nmod_ref, ng_ref, o_ref, h_ref, tok_scr = rest
    elif emit == "fnet":
        nmod_ref, ng_ref, cs_ref, o_ref, a_ref, b_ref = rest
    else:
        (o_ref,) = rest
    z = _dot(y_ref[...].astype(BF16), wm_ref[...]) + bm_ref[...]
    if glu:
        d = o_ref.shape[-1]
        z = z[:, :d] * jax.nn.sigmoid(z[:, d:])
    x = x_ref[...] + mod_ref[0, 2:3, :] * _rms(z, gm_ref[...])
    h = _normmod(x, pre_ref[...], mod_ref[0, 3:4, :], mod_ref[0, 4:5, :]).astype(BF16)
    acc = None
    for lo, hi in zip(cuts[:-1], cuts[1:]):
        a = _dot(h, w13_ref[:, lo:hi])
        b = _dot(h, w13_ref[:, f + lo:f + hi])
        gact = (a * jax.nn.sigmoid(a) * b).astype(BF16)
        part = _dot(gact, w2_ref[lo:hi, :])
        acc = part if acc is None else acc + part
    xo = x + mod_ref[0, 5:6, :] * _rms(acc, post_ref[...])
    o_ref[...] = xo
    if emit:
        hn = _normmod(xo, ng_ref[...], nmod_ref[0, 0:1, :], nmod_ref[0, 1:2, :])
        if emit == "chunks":
            tm, d = hn.shape
            for lt in range(d // LANE):
                tok_scr[lt * tm:(lt + 1) * tm, :] = hn[:, lt * LANE:(lt + 1) * LANE]
            for j in range(S5_T):
                for lt in range(d // LANE):
                    rows = tok_scr[pl.ds(lt * tm + j, tm // S5_T, stride=S5_T), :]
                    h_ref[:, j * d + lt * LANE:j * d + (lt + 1) * LANE] = rows.astype(h_ref.dtype)
        else:
            a, b = _channel_dft(hn.astype(BF16), cs_ref[...])
            a_ref[...] = a.astype(a_ref.dtype)
            b_ref[...] = b.astype(b_ref.dtype)


def _mix_ffn_call(x, y, mods, wm, bm, mix_post_g, ffn_pre_g, ffn_post_g, w13, w2, layer, tm, rows_per_batch,
                  glu=False, emit=None):
    m, d = x.shape
    k, nm = wm.shape
    f = w2.shape[1]
    cuts = tuple(range(0, f, 6 * MXU_COLS)) + (f,)
    x_spec, mod_spec = _row_specs(tm, d, rows_per_batch // tm)
    once = lambda shape: pl.BlockSpec(shape, lambda i: (0, 0), pipeline_mode=pl.Buffered(1))
    of_layer = lambda shape: pl.BlockSpec((None,) + shape, lambda i: (layer, 0, 0), pipeline_mode=pl.Buffered(1))
    in_specs = [x_spec, pl.BlockSpec((tm, k), lambda i: (i, 0)), mod_spec,
                once((k, nm)), _const_spec((1, nm)), _const_spec((1, d)), _const_spec((1, d)),
                _const_spec((1, d)), of_layer((d, 2 * f)), of_layer((f, d))]
    args = [x, y, mods, wm, bm.reshape(1, nm), mix_post_g.reshape(1, d), ffn_pre_g.reshape(1, d),
            ffn_post_g.reshape(1, d), w13, w2]
    out_specs, out_shape = [x_spec], [jax.ShapeDtypeStruct((m, d), F32)]
    kind = None
    scratch = []
    if emit is not None:
        kind, next_mods, next_g, cs = emit
        in_specs += [mod_spec, _const_spec((1, d))]
        args += [next_mods, next_g.reshape(1, d)]
        if kind == "fnet":
            in_specs.append(_const_spec(cs.shape))
            args.append(cs)
            out_specs += [x_spec] * 2
            out_shape += [jax.ShapeDtypeStruct((m, d), BF16)] * 2
        else:
            out_specs.append(pl.BlockSpec((tm // S5_T, S5_T * d), lambda i: (i, 0)))
            out_shape.append(jax.ShapeDtypeStruct((m // S5_T, S5_T * d), BF16))
            scratch.append(pltpu.VMEM((tm * (d // LANE), LANE), F32))
    outs = pl.pallas_call(
        functools.partial(_mix_ffn_kernel, f=f, cuts=cuts, glu=glu, emit=kind), grid=(m // tm,),
        in_specs=in_specs, out_specs=out_specs, out_shape=out_shape, scratch_shapes=scratch,
        compiler_params=_cp("parallel"), name="mix_ffn",
    )(*args)
    return outs[0] if emit is None else (outs[0], tuple(outs[1:]))


def _mla_proj_kernel(x_ref, mod_ref, g_ref, w_ref, qg_ref, kvg_ref, wq_ref, wkv_ref, cp_ref, sp_ref,
                     q_ref, k_ref, vt_ref, *, ql, kvl, qscale, positional):
    h = _normmod(x_ref[...], g_ref[...], mod_ref[0, 0:1, :], mod_ref[0, 1:2, :])
    z = _dot(h.astype(BF16), w_ref[...])
    qn = _rms(z[:, :ql], qg_ref[...]).astype(BF16)
    cn = _rms(z[:, ql:ql + kvl], kvg_ref[...]).astype(BF16)
    tk = z.shape[0]
    low = lax.broadcasted_iota(jnp.int32, (tk, 2 * MLA_ROPE), 1) < MLA_ROPE
    if positional:
        cpf = jnp.concatenate([cp_ref[...]] * 2, axis=1)
        spf = jnp.concatenate([sp_ref[...]] * 2, axis=1)

    def rope_slots(pair, keep_raw):
        swapped = pltpu.roll(pair, MLA_ROPE, 1)
        raw = jnp.where(low, 0.0, swapped) if keep_raw else None
        if not positional:
            return raw
        rot = jnp.where(low, pair * cpf + swapped * spf, 0.0)
        return rot + raw if keep_raw else rot

    kr = rope_slots(z[:, ql + kvl:], keep_raw=not positional).astype(BF16)

    zkv = _dot(cn, wkv_ref[...])
    ones_blk = (lax.broadcasted_iota(jnp.int32, (MLA_VT - MLA_V, tk), 0) == 0).astype(BF16)
    for hd in range(MLA_HEADS):
        base = hd * (MLA_NOPE + MLA_V)
        k_ref[0, hd, 0, :, 0:MLA_NOPE] = zkv[:, base:base + MLA_NOPE].astype(BF16)
        k_ref[0, hd, 0, :, MLA_NOPE:MLA_QK] = kr
        vt_ref[0, hd, 0, 0:MLA_V, :] = zkv[:, base + MLA_NOPE:base + MLA_NOPE + MLA_V].T.astype(BF16)
        vt_ref[0, hd, 0, MLA_V:MLA_VT, :] = ones_blk

    zq = _dot(qn, wq_ref[...])
    for hd in range(MLA_HEADS):
        base = hd * MLA_QK
        rp = rope_slots(zq[:, base + MLA_NOPE:base + MLA_QK], keep_raw=True)
        qcat = jnp.concatenate([zq[:, base:base + MLA_NOPE], rp], axis=1) * qscale
        q_ref[0, hd] = qcat.T.astype(BF16)


def _flash_kernel(q_ref, kc_ref, vc_ref, *rest, n_lat):
    if n_lat:
        kl_ref, vl_ref, o_ref, s_scr, acc_scr = rest
    else:
        o_ref, acc_scr = rest
    qt = q_ref[0, 0]

    def qk(k, slot):
        s = _dot(k, qt)
        s_scr[slot] = s
        return jnp.max(s, axis=0, keepdims=True)

    def sm_pv(slot, vt, m, mx):
        m_new = jnp.maximum(m, mx)
        alpha = jnp.exp2(m - m_new)
        p = jnp.exp2(s_scr[slot] - m_new).astype(BF16)
        acc_scr[...] = alpha * acc_scr[...] + _dot(vt, p)
        return m_new

    sc = _dot(kc_ref[0, 0, 0], qt)
    if n_lat:
        mx = qk(kl_ref[0, 0, 0], 0)
    m = jnp.max(sc, axis=0, keepdims=True)
    acc_scr[...] = _dot(vc_ref[0, 0, 0], jnp.exp2(sc - m).astype(BF16))
    if n_lat:

        per = next(p for p in (16, 8, 4, 2) if n_lat % p == 0)

        def body(i, carry):
            m, mx_cur = carry
            c = per * i
            for u in range(per):
                mx_next = qk(kl_ref[0, 0, jnp.minimum(c + u + 1, n_lat - 1)], (u + 1) % 2)
                m = sm_pv(u % 2, vl_ref[0, 0, c + u], m, mx_cur)
                mx_cur = mx_next
            return m, mx_cur

        lax.fori_loop(0, n_lat // per, body, (m, mx))
    acc = acc_scr[...]
    o_ref[0] = (acc[0:MLA_V] / acc[MLA_V:MLA_V + 1]).T.astype(o_ref.dtype)


def _rope_tables(n_lat):
    rows = n_lat // GRID_W
    row = jnp.repeat(jnp.arange(rows, dtype=F32), GRID_W)
    col = jnp.tile(jnp.arange(GRID_W, dtype=F32), rows)
    axis_dim = MLA_ROPE // 2
    inv_freq = 1.0 / (ROPE_THETA ** (jnp.arange(0, axis_dim, 2, dtype=F32) / axis_dim))
    ang_r = row[:, None] * inv_freq
    ang_c = col[:, None] * inv_freq
    cr, sr, cc, sc = jnp.cos(ang_r), jnp.sin(ang_r), jnp.cos(ang_c), jnp.sin(ang_c)
    cp = jnp.concatenate([cr, cr, cc, cc], axis=-1)
    sp = jnp.concatenate([-sr, sr, -sc, sc], axis=-1)
    return cp, sp


_ROPE_SWAP = np.concatenate([np.arange(16, 32), np.arange(0, 16), np.arange(48, 64), np.arange(32, 48)])


def _mla_side(x, mods, pre_g, w_in_ext, q_g, kv_g, w_uq_ext, w_ukv, tabs, b, n, tk, positional):
    m, d = x.shape
    ql, kvl = q_g.shape[-1], kv_g.shape[-1]
    nc = n // tk
    x_spec, mod_spec = _row_specs(tk, d, nc)
    tab_spec = pl.BlockSpec((tk, MLA_ROPE), lambda i: (i % nc, 0))
    qscale = (MLA_NOPE + MLA_ROPE) ** -0.5 * math.log2(math.e)
    return pl.pallas_call(
        functools.partial(_mla_proj_kernel, ql=ql, kvl=kvl, qscale=qscale, positional=positional),
        grid=(m // tk,),
        in_specs=[x_spec, mod_spec, _const_spec((1, d)), _const_spec(w_in_ext.shape),
                  _const_spec((1, ql)), _const_spec((1, kvl)), _const_spec(w_uq_ext.shape),
                  _const_spec(w_ukv.shape), tab_spec, tab_spec],
        out_specs=[pl.BlockSpec((1, MLA_HEADS, MLA_QK, tk), lambda i: (i // nc, 0, 0, i % nc)),
                   pl.BlockSpec((1, MLA_HEADS, 1, tk, MLA_QK), lambda i: (i // nc, 0, i % nc, 0, 0)),
                   pl.BlockSpec((1, MLA_HEADS, 1, MLA_VT, tk), lambda i: (i // nc, 0, i % nc, 0, 0))],
        out_shape=[jax.ShapeDtypeStruct((b, MLA_HEADS, MLA_QK, n), BF16),
                   jax.ShapeDtypeStruct((b, MLA_HEADS, nc, tk, MLA_QK), BF16),
                   jax.ShapeDtypeStruct((b, MLA_HEADS, nc, MLA_VT, tk), BF16)],
        compiler_params=_cp("parallel"), name="mla_proj",
    )(x, mods, pre_g.reshape(1, d), w_in_ext, q_g.reshape(1, ql), kv_g.reshape(1, kvl), w_uq_ext, w_ukv, *tabs)


def _flash_call(qt, kc, vtc, kl, vtl, tq):
    b, hh, _, n = qt.shape
    c = kc.shape[-2]
    n_lat = 0 if kl is None else kl.shape[2]
    in_specs = [pl.BlockSpec((1, 1, MLA_QK, tq), lambda bi, h, i: (bi, h, 0, i)),
                pl.BlockSpec((1, 1, 1, c, MLA_QK), lambda bi, h, i: (bi, h, 0, 0, 0)),
                pl.BlockSpec((1, 1, 1, MLA_VT, c), lambda bi, h, i: (bi, h, 0, 0, 0))]
    args = [qt, kc, vtc]
    scratch = [pltpu.VMEM((MLA_VT, tq), F32)]
    if n_lat:
        assert n_lat % 2 == 0, "latent key chunks are consumed in pairs"
        tk = kl.shape[-2]
        in_specs += [pl.BlockSpec((1, 1, n_lat, tk, MLA_QK), lambda bi, h, i: (bi, h, 0, 0, 0)),
                     pl.BlockSpec((1, 1, n_lat, MLA_VT, tk), lambda bi, h, i: (bi, h, 0, 0, 0))]
        args += [kl, vtl]
        scratch = [pltpu.VMEM((2, tk, tq), F32)] + scratch
    return pl.pallas_call(
        functools.partial(_flash_kernel, n_lat=n_lat), grid=(b, hh, n // tq),
        in_specs=in_specs,
        out_specs=pl.BlockSpec((1, tq, MLA_V), lambda bi, h, i: (bi, i, h)),
        out_shape=jax.ShapeDtypeStruct((b, n, hh * MLA_V), BF16),
        scratch_shapes=scratch,
        compiler_params=_cp("parallel", "parallel", "arbitrary"), name="flash",
    )(*args)


def _mla_layer(xl, xc, mods_l, mods_c, pre_g, fin, w_in, q_g, kv_g, w_uq, w_ukv, w_o, b, n, c):
    d = xl.shape[-1]
    ql, kvl = q_g.shape[-1], kv_g.shape[-1]
    hh = MLA_HEADS
    rope_cols = w_in[:, ql + kvl:]
    w_in_ext = jnp.concatenate([w_in, rope_cols[:, _ROPE_SWAP]], axis=1).astype(BF16)
    wq = w_uq.reshape(ql, hh, MLA_NOPE + MLA_ROPE)
    w_uq_ext = jnp.concatenate([wq, wq[:, :, MLA_NOPE:][:, :, _ROPE_SWAP]], axis=-1)
    w_uq_ext = w_uq_ext.reshape(ql, hh * MLA_QK).astype(BF16)
    w_ukv_b = w_ukv.astype(BF16)
    w_o_b = w_o.astype(BF16)

    tabs = _rope_tables(n)

    tm_l = min(ROW_TILE, n)
    tk_l = min(ROW_TILE, n // 2)
    ql_, kl, vtl = _mla_side(xl, mods_l, pre_g, w_in_ext, q_g, kv_g, w_uq_ext, w_ukv_b, tabs, b, n, tk_l, True)
    qc_, kc, vtc = _mla_side(xc, mods_c, pre_g, w_in_ext, q_g, kv_g, w_uq_ext, w_ukv_b, tabs, b, c, c, False)
    o_lat = _flash_call(ql_, kc, vtc, kl, vtl, min(2 * ROW_TILE, n)).reshape(b * n, hh * MLA_V)
    o_ctx = _flash_call(qc_, kc, vtc, None, None, c).reshape(b * c, hh * MLA_V)
    zb = jnp.zeros((d,), F32)
    xl = fin(xl, o_lat, mods_l, w_o_b, zb, tm_l, n)
    xc = fin(xc, o_ctx, mods_c, w_o_b, zb, c, c)
    return xl, xc


def _hy_in_kernel(x_ref, xp_ref, xn_ref, mod_ref, g_ref, w_ref, b_ref, cw_ref, cb_ref,
                  g0_ref, vg_ref, *, tpb):
    i = pl.program_id(0)
    g = g_ref[...]
    shift, scale = mod_ref[0, 0:1, :], mod_ref[0, 1:2, :]
    xcat = jnp.concatenate([xp_ref[...], x_ref[...], xn_ref[...]], axis=0)
    hcat = _normmod(xcat, g, shift, scale).astype(BF16)
    tm = x_ref.shape[0]
    d = g0_ref.shape[-1]
    first = (i % tpb) == 0
    last = (i % tpb) == tpb - 1
    ridx = lax.broadcasted_iota(jnp.int32, (tm, 1), 0)

    def conv_part(c):
        cols = slice(c * d, (c + 1) * d)
        ucat = _dot(hcat, w_ref[:, cols]) + b_ref[:, cols]
        u = ucat[8:tm + 8]
        prev_row = jnp.where(first, 0.0, ucat[7:8, :])
        next_row = jnp.where(last, 0.0, ucat[tm + 8:tm + 9, :])
        dn = jnp.where(ridx == 0, prev_row, pltpu.roll(u, 1, 0))
        upw = jnp.where(ridx == tm - 1, next_row, pltpu.roll(u, tm - 1, 0))
        return cb_ref[:, cols] + dn * cw_ref[0:1, cols] + u * cw_ref[1:2, cols] + upw * cw_ref[2:3, cols]

    g0_ref[...] = conv_part(0).astype(g0_ref.dtype)
    vg_ref[...] = (conv_part(2) * conv_part(1)).astype(vg_ref.dtype)


def _hy_in_call(x, mods, pre_g, w_in, b_in, conv_w, conv_b, tm, n, out_dtype):
    m, d = x.shape
    p = w_in.shape[1]
    tpb = n // tm
    x_spec, mod_spec = _row_specs(tm, d, tpb)
    r8 = tm // 8
    nb8 = m // 8
    prev_spec = pl.BlockSpec((8, d), lambda i: (jnp.maximum(i * r8 - 1, 0), 0))
    next_spec = pl.BlockSpec((8, d), lambda i: (jnp.minimum((i + 1) * r8, nb8 - 1), 0))
    cw = jnp.pad(conv_w, ((0, 8 - conv_w.shape[0]), (0, 0)))
    return pl.pallas_call(
        functools.partial(_hy_in_kernel, tpb=tpb), grid=(m // tm,),
        in_specs=[x_spec, prev_spec, next_spec, mod_spec, _const_spec((1, d)), _const_spec((d, p)),
                  _const_spec((1, p)), _const_spec((8, p)), _const_spec((1, p))],
        out_specs=[x_spec, x_spec],
        out_shape=[jax.ShapeDtypeStruct((m, d), out_dtype), jax.ShapeDtypeStruct((m, d), out_dtype)],
        compiler_params=_cp("parallel"), name="hy_in",
    )(x, x, x, mods, pre_g.reshape(1, d), w_in, b_in.reshape(1, p), cw, conv_b.reshape(1, p))


_PI_SPLIT = (3.140625, 9.67502593994140625e-4, 1.509957990978376432e-7)
_SIN_TAYLOR = (-1.0 / 6, 1.0 / 120, -1.0 / 5040, 1.0 / 362880, -1.0 / 39916800)


def _sin(x):
    kf = jnp.round(x * (1.0 / math.pi))
    r = ((x - kf * _PI_SPLIT[0]) - kf * _PI_SPLIT[1]) - kf * _PI_SPLIT[2]
    r2 = r * r
    p = _SIN_TAYLOR[4]
    for c in _SIN_TAYLOR[3::-1]:
        p = p * r2 + c
    s = r + r * r2 * p
    odd = (kf.astype(jnp.int32) & 1) == 1
    return jnp.where(odd, -s, s)


def _hy_filter_kernel(bands_ref, w1_ref, b1_ref, fq_ref, w2_ref, b2_ref, w3_ref, dl_ref,
                      k_ref, nrm_ref, *, n, tr):
    i = pl.program_id(0)
    bwd = i >= n // tr
    row = lax.broadcasted_iota(jnp.int32, (tr, LANE), 0) + i * tr
    j = jnp.where(bwd, 2 * n - row, row).astype(F32)
    lane = lax.broadcasted_iota(jnp.int32, (tr, LANE), 1)
    t = j * (1.0 / (n - 1))
    arg = (2.0 * math.pi / n) * j * bands_ref[0:1, :] + bands_ref[1:2, :]
    z = jnp.where(lane == 0, t, jnp.where(lane <= 2 * HYENA_BANDS, _sin(arg), 0.0))
    fq = fq_ref[...]
    a = _sin(fq * (_dot_x3(z, w1_ref[...]) + b1_ref[...]))
    for k in range(w2_ref.shape[0]):
        a = _sin(fq * (_dot_x3(a, w2_ref[k]) + b2_ref[k]))
    h = _dot_x3(a, w3_ref[jnp.where(bwd, 1, 0)])
    decay = jnp.exp(-t[:, 0:1] * dl_ref[...])
    k = h * decay
    k = jnp.where(row[:, 0:1] == n, 0.0, k)
    k_ref[...] = k.astype(k_ref.dtype)
    part = jnp.sum(jnp.abs(k), axis=0, keepdims=True)

    @pl.when(i == 0)
    def _():
        nrm_ref[...] = jnp.zeros_like(nrm_ref)

    nrm_ref[...] += jnp.broadcast_to(part, nrm_ref.shape)


def _hy_filter_call(n, d, out_dtype, f_w1, f_b1, f_freq, f_w2, f_b2, f_w3):
    fw = f_w1.shape[1]
    tr = min(ROW_TILE, n)
    bands_np = np.zeros((8, LANE), np.float32)
    bands_np[0, 1:1 + HYENA_BANDS] = np.linspace(1e-4, HYENA_BANDS - 1, HYENA_BANDS, dtype=np.float32)
    bands_np[0, 1 + HYENA_BANDS:1 + 2 * HYENA_BANDS] = bands_np[0, 1:1 + HYENA_BANDS]
    bands_np[1, 1:1 + HYENA_BANDS] = 0.5 * np.pi
    bands_np[1, 1 + HYENA_BANDS:1 + 2 * HYENA_BANDS] = np.pi
    w1p = jnp.zeros((LANE, fw), F32).at[:f_w1.shape[0]].set(f_w1)
    deltas = jnp.abs(jnp.linspace(math.log(HYENA_TARGET) / HYENA_SLOW, math.log(HYENA_TARGET) / HYENA_FAST,
                                  d, dtype=F32)).reshape(1, d)
    row = pl.BlockSpec((tr, d), lambda i: (i, 0))
    return pl.pallas_call(
        functools.partial(_hy_filter_kernel, n=n, tr=tr), grid=(2 * n // tr,),
        in_specs=[_const_spec((8, LANE)), _const_spec((LANE, fw)), _const_spec((1, fw)), _const_spec((1, fw)),
                  _const_spec(f_w2.shape), _const_spec((f_w2.shape[0], 1, fw)), _const_spec((2, fw, d)),
                  _const_spec((1, d))],
        out_specs=[row, _const_spec((8, d))],
        out_shape=[jax.ShapeDtypeStruct((2 * n, d), out_dtype), jax.ShapeDtypeStruct((8, d), F32)],
        compiler_params=_cp("arbitrary"), name="hy_filter",
    )(jnp.asarray(bands_np), w1p, f_b1.reshape(1, fw), f_freq.reshape(1, fw), f_w2,
      f_b2.reshape(f_w2.shape[0], 1, fw), jnp.transpose(f_w3.reshape(fw, 2, d), (1, 0, 2)), deltas)


def _dft_cs(nf, nt, period):
    ft = (np.arange(nf)[:, None] * np.arange(nt)[None, :]) % period
    ang = 2.0 * np.pi * ft / period
    return np.cos(ang), np.sin(ang)


def _twiddle_tables(n1, n2):
    nn = n1 * n2
    f1 = jnp.arange(n1, dtype=jnp.int32)
    t2 = jnp.arange(n2, dtype=jnp.int32)
    idx = (f1[:, None] * t2[None, :]) % nn
    ang = idx.astype(F32) * (2.0 * math.pi / nn)
    shape = idx.shape + (LANE,)
    return (jnp.broadcast_to(jnp.cos(ang)[..., None], shape),
            jnp.broadcast_to(jnp.sin(ang)[..., None], shape))


KRON_R = 16
KRON_H = 8
KRON_W = 256


def _cblock(mc):
    return np.block([[mc.real, -mc.imag], [mc.imag, mc.real]])


def _split_radix(n1):
    b = 16 if (n1 % 16 == 0 and n1 >= 64) else 4
    assert n1 % b == 0 and (n1 // b) % 2 == 0
    return n1 // b, b


def _slab_dft_mats(n1, a_in, a_out, sign, neg_im=False, real_in=False, scale=1.0):
    a, b = _split_radix(n1)
    r = KRON_H
    eye = np.eye(r)
    w = lambda num, den: np.exp(sign * 2j * np.pi * num / den)
    ua = np.arange(a)[:, None] * np.arange(a)[None, :]
    vb = np.arange(b)[:, None] * np.arange(b)[None, :]
    if sign < 0:
        m1 = np.kron(w(ua[:, :a_in], a) * scale, eye)
        l1 = _cblock(m1)
        if neg_im:
            l1[:, a_in * r:] *= -1.0
        if real_in:
            l1 = l1[:, :a_in * r]
        l2 = np.stack([_cblock(np.kron(w(vb, b) * w(u * np.arange(b)[None, :], n1), eye)) for u in range(a)])
    else:
        l1 = np.stack([_cblock(np.kron(w(vb, b) * w(u * np.arange(b)[:, None], n1), eye)) for u in range(a)])
        l2 = _cblock(np.kron(w(ua[:a_out, :], a) * scale, eye))
    return jnp.asarray(l1, BF16), jnp.asarray(l2, BF16)


def _kfa_kernel(*refs, nparts, a_in, a, b):
    l1_ref, l2_ref = refs[0], refs[1]
    parts = refs[2:2 + nparts]
    twc_ref, tws_ref, or_ref, oi_ref = refs[2 + nparts:]
    r = KRON_H
    halves = range(KRON_R // r)
    reps = or_ref.shape[-1] // LANE
    l1 = l1_ref[...]
    y1 = []
    for h in halves:
        rows = slice(h * r, (h + 1) * r)
        y1.append([_dot(l1, jnp.concatenate([p[0, aa * b + bb].astype(F32)[rows] for p in parts
                                             for aa in range(a_in)], axis=0).astype(BF16))
                   for bb in range(b)])
    for u in range(a):
        z = []
        for h in halves:
            x = jnp.concatenate([y1[h][bb][u * r:(u + 1) * r] for bb in range(b)]
                                + [y1[h][bb][(a + u) * r:(a + u + 1) * r] for bb in range(b)], axis=0)
            z.append(_dot(l2_ref[u], x.astype(BF16)))
        for v in range(b):
            f1 = u + a * v
            zr = jnp.concatenate([zh[v * r:(v + 1) * r] for zh in z], axis=0)
            zi = jnp.concatenate([zh[(b + v) * r:(b + v + 1) * r] for zh in z], axis=0)
            c = jnp.concatenate([twc_ref[f1]] * reps, axis=1)
            sn = jnp.concatenate([tws_ref[f1]] * reps, axis=1)
            or_ref[0, f1] = (zr * c + zi * sn).astype(or_ref.dtype)
            oi_ref[0, f1] = (zi * c - zr * sn).astype(oi_ref.dtype)


def _kfa_call(parts, nb_out, n1, n2, d, l1, l2, twc, tws):
    a, b = _split_radix(n1)
    a_in = parts[0][0].shape[1] // b
    r, w = KRON_R, min(KRON_W, d)
    in_specs = [_const_spec(l1.shape), pl.BlockSpec(l2.shape, lambda bi, j, k: (0, 0, 0), pipeline_mode=pl.Buffered(1))]
    args = [l1, l2]
    for arr, bi_fixed in parts:
        t1_in = arr.shape[1]
        if bi_fixed is None:
            in_specs.append(pl.BlockSpec((1, t1_in, r, w), lambda bi, j, k: (bi, 0, j, k)))
        else:
            in_specs.append(pl.BlockSpec((1, t1_in, r, w), lambda bi, j, k, f=bi_fixed: (f, 0, j, k)))
        args.append(arr)
    tw_spec = pl.BlockSpec((n1, r, LANE), lambda bi, j, k: (0, j, 0))
    out_spec = pl.BlockSpec((1, n1, r, w), lambda bi, j, k: (bi, 0, j, k))
    out = jax.ShapeDtypeStruct((nb_out, n1, n2, d), BF16)
    return pl.pallas_call(
        functools.partial(_kfa_kernel, nparts=len(parts), a_in=a_in, a=a, b=b),
        grid=(nb_out, n2 // r, d // w),
        in_specs=in_specs + [tw_spec, tw_spec], out_specs=[out_spec, out_spec], out_shape=[out, out],
        compiler_params=_cp("parallel", "parallel", "parallel"), name="kfa",
    )(*args, twc, tws)


def _kfc_kernel(l3_ref, l4_ref, gr_ref, gi_ref, vg_ref, g0_ref, skip_ref, o_ref, *, a, b, a_out):
    r = KRON_H
    halves = range(KRON_R // r)
    y3 = [[] for _ in halves]
    for u in range(a):
        gr = [gr_ref[0, u + a * v].astype(F32) for v in range(b)]
        gi = [gi_ref[0, u + a * v].astype(F32) for v in range(b)]
        for h in halves:
            rows = slice(h * r, (h + 1) * r)
            x = jnp.concatenate([g[rows] for g in gr] + [g[rows] for g in gi], axis=0).astype(BF16)
            y3[h].append(_dot(l3_ref[u], x))
    l4 = l4_ref[...]
    skip = skip_ref[...]
    for bb in range(b):
        y = []
        for h in halves:
            x = jnp.concatenate([y3[h][u][bb * r:(bb + 1) * r] for u in range(a)]
                                + [y3[h][u][(b + bb) * r:(b + bb + 1) * r] for u in range(a)], axis=0)
            y.append(_dot(l4, x.astype(BF16)))
        for sg in range(2):
            for aa in range(a_out):
                t1 = aa * b + bb
                yb = jnp.concatenate([yh[(sg * a_out + aa) * r:(sg * a_out + aa + 1) * r] for yh in y], axis=0)
                o_ref[sg, t1] = ((yb + vg_ref[sg, t1] * skip) * g0_ref[sg, t1]).astype(o_ref.dtype)


def _kfc_call(g1r, g1i, vg4, g04, skip, n1, n2, d, l3, l4):
    a, b = _split_radix(n1)
    nb, t1_out = vg4.shape[:2]
    r, w = KRON_R, min(KRON_W, d)
    slab = pl.BlockSpec((1, n1, r, w), lambda j, k: (0, 0, j, k))
    nat = pl.BlockSpec((nb, t1_out, r, w), lambda j, k: (0, 0, j, k))
    return pl.pallas_call(
        functools.partial(_kfc_kernel, a=a, b=b, a_out=t1_out // b), grid=(n2 // r, d // w),
        in_specs=[pl.BlockSpec(l3.shape, lambda j, k: (0, 0, 0), pipeline_mode=pl.Buffered(1)),
                  _const_spec(l4.shape), slab, slab, nat, nat, pl.BlockSpec((1, w), lambda j, k: (0, k))],
        out_specs=nat, out_shape=jax.ShapeDtypeStruct(vg4.shape, BF16),
        compiler_params=_cp("parallel", "parallel"), name="kfc",
    )(l3, l4, g1r, g1i, vg4, g04, skip.reshape(1, d))


def _hy_b_kernel(fm_ref, fmc_ref, xr_ref, xi_ref, fr_ref, fi_ref, sc_ref, twc_ref, tws_ref,
                 or_ref, oi_ref, *, nf, d):
    fm = fm_ref[...]
    fmc = fmc_ref[...]
    n2 = xr_ref.shape[1]
    sc = sc_ref[...]
    for s in range(nf):
        x = _dot(fm, jnp.concatenate([xr_ref[s], xi_ref[s]], axis=0))
        kk = _dot(fm, jnp.concatenate([fr_ref[s], fi_ref[s]], axis=0))
        xr, xi = x[:n2], x[n2:]
        kr, ki = kk[:n2] * sc, kk[n2:] * sc
        yr = (xr * kr - xi * ki).astype(BF16)
        yi = (xr * ki + xi * kr).astype(BF16)
        g = _dot(fmc, jnp.concatenate([yr, yi], axis=0))
        gr, gi = g[:n2], g[n2:]
        c = jnp.concatenate([twc_ref[s]] * (d // LANE), axis=1)
        sn = jnp.concatenate([tws_ref[s]] * (d // LANE), axis=1)
        or_ref[s] = (gr * c - gi * sn).astype(or_ref.dtype)
        oi_ref[s] = (gi * c + gr * sn).astype(oi_ref.dtype)


def _block_c(cs, sn, sign):
    return np.block([[cs, -sign * sn], [sign * sn, cs]])


def _hy_conv_long(vg, g0, k, nrm, skip, b, n, d):
    n2 = FFT_N2
    nn = 2 * n
    n1 = nn // n2
    rows = n // n2
    a, _ = _split_radix(n1)
    la1, la2 = _slab_dft_mats(n1, a // 2, 0, -1)
    lk1, lk2 = _slab_dft_mats(n1, a, 0, -1, real_in=True)
    lc3, lc4 = _slab_dft_mats(n1, 0, a // 2, +1)
    cs2, sn2 = _dft_cs(n2, n2, n2)
    fm_b = jnp.asarray(_block_c(cs2, sn2, -1.0), BF16)
    fm_bc = jnp.asarray(_block_c(cs2, sn2, 1.0), BF16)
    twc, tws = _twiddle_tables(n1, n2)

    kfr, kfi = _kfa_call([(k.reshape(1, n1, n2, d), 0)], 1, n1, n2, d, lk1, lk2, twc, tws)
    nf = 4 if n1 % 4 == 0 else 1
    scale = (1.0 / (nrm[0:1, :] * nn))
    slab = pl.BlockSpec((nf, n2, d), lambda j: (j, 0, 0))
    shp3 = (n1, n2, d)
    vg4 = vg.reshape(b, rows, n2, d)
    g04 = g0.reshape(b, rows, n2, d)
    x1r, x1i = _kfa_call([(vg4, 0), (vg4, 1)], 1, n1, n2, d, la1, la2, twc, tws)
    tw_slab = pl.BlockSpec((nf, n2, LANE), lambda j: (j, 0, 0))
    g1r, g1i = pl.pallas_call(
        functools.partial(_hy_b_kernel, nf=nf, d=d), grid=(n1 // nf,),
        in_specs=[_const_spec(fm_b.shape), _const_spec(fm_bc.shape), slab, slab, slab, slab, _const_spec((1, d)),
                  tw_slab, tw_slab],
        out_specs=[slab, slab],
        out_shape=[jax.ShapeDtypeStruct(shp3, BF16)] * 2,
        compiler_params=_cp("parallel"), name="hy_b",
    )(fm_b, fm_bc, x1r.reshape(shp3), x1i.reshape(shp3), kfr.reshape(shp3), kfi.reshape(shp3), scale, twc, tws)
    out = _kfc_call(g1r.reshape(1, n1, n2, d), g1i.reshape(1, n1, n2, d), vg4, g04, skip, n1, n2, d, lc3, lc4)
    return out.reshape(b * n, d)


def _hy_short_kernel(fa_ref, fk_ref, fi_ref, vg_ref, g0_ref, k_ref, nrm_ref, skip_ref, o_ref, *, n):
    z = jnp.concatenate([vg_ref[0], vg_ref[1]], axis=0)
    x = _dot_hi(fa_ref[...], z)
    kk = _dot_hi(fk_ref[...], k_ref[...])
    nn = 2 * n
    sc = 1.0 / (nrm_ref[0:1, :] * nn)
    xr, xi = x[:nn], x[nn:]
    kr, ki = kk[:nn] * sc, kk[nn:] * sc
    y = _dot_hi(fi_ref[...], jnp.concatenate([xr * kr - xi * ki, xr * ki + xi * kr], axis=0))
    skip = skip_ref[...]
    for bi in range(2):
        o_ref[bi] = ((y[bi * n:(bi + 1) * n] + vg_ref[bi] * skip) * g0_ref[bi]).astype(o_ref.dtype)


def _hy_conv_short(vg, g0, k, nrm, skip, b, n, d):
    nn = 2 * n
    cs, sn = _dft_cs(nn, n, nn)
    fa = jnp.asarray(_block_c(cs, sn, -1.0), F32)
    csk, snk = _dft_cs(nn, nn, nn)
    fk = jnp.asarray(np.concatenate([csk, -snk], axis=0), F32)
    csi, sni = _dft_cs(n, nn, nn)
    fi = jnp.asarray(_block_c(csi, sni, 1.0), F32)
    cb = 256
    col3 = pl.BlockSpec((b, n, cb), lambda j: (0, 0, j))
    vec = pl.BlockSpec((1, cb), lambda j: (0, j))
    out = pl.pallas_call(
        functools.partial(_hy_short_kernel, n=n), grid=(d // cb,),
        in_specs=[_const_spec(fa.shape), _const_spec(fk.shape), _const_spec(fi.shape), col3, col3,
                  pl.BlockSpec((nn, cb), lambda j: (0, j)), pl.BlockSpec((8, cb), lambda j: (0, j)), vec],
        out_specs=col3, out_shape=jax.ShapeDtypeStruct((b, n, d), BF16),
        compiler_params=_cp("parallel"), name="hy_short",
    )(fa, fk, fi, vg.reshape(b, n, d), g0.reshape(b, n, d), k, nrm, skip.reshape(1, d))
    return out.reshape(b * n, d)


def _hyena_layer(xl, xc, mods_l, mods_c, pre_g, fin, w_in, b_in, conv_w, conv_b, filt, skip,
                 w_out, b_out, b, n, c):
    d = xl.shape[-1]
    w_in_b = w_in.astype(BF16)
    w_out_b = w_out.astype(BF16)
    tm = min(ROW_TILE, n)
    g0, vg = _hy_in_call(xl, mods_l, pre_g, w_in_b, b_in, conv_w, conv_b, tm, n, BF16)
    k, nrm = _hy_filter_call(n, d, BF16, *filt)
    u_out = _hy_conv_long(vg, g0, k, nrm, skip[0], b, n, d)
    xl, (hl,) = fin(xl, u_out, mods_l, w_out_b, b_out, tm, n)

    g0c, vgc = _hy_in_call(xc, mods_c, pre_g, w_in_b, b_in, conv_w, conv_b, c, c, F32)
    kc, nrmc = _hy_filter_call(c, d, F32, *filt)
    u_out_c = _hy_conv_short(vgc, g0c, kc, nrmc, skip[0], b, c, d)
    xc, (hc,) = fin(xc, u_out_c, mods_c, w_out_b, b_out, c, c, is_ctx=True)
    return xl, hl, hc


def _s5_operators(lam_re, lam_im, log_dt, b_re, b_im, c_re, c_im, d_skip):
    t = S5_T
    g, ns = lam_re.shape[1], lam_re.shape[2]
    gc = b_re.shape[-1]
    gl = LANE // gc
    nblk = g // gl
    lam = lax.complex(lam_re, lam_im)
    dt = jnp.exp(log_dt)[..., None]
    lam_bar = jnp.exp(lam * dt)
    b_bar = ((lam_bar - 1.0) / lam)[..., None] * lax.complex(b_re, b_im)
    c_mat = lax.complex(c_re, c_im)
    pw = jnp.arange(t + 1, dtype=F32)
    lam_pw = jnp.exp((lam * dt)[None] * pw[:, None, None, None])
    hp = HIGHEST
    kern = jnp.einsum('dgcn,tdgn,dgne->dgtce', c_mat, lam_pw[:t], b_bar, precision=hp).real
    dsk = d_skip.reshape(g, gc)
    kt = jnp.swapaxes(kern, -1, -2)
    centre = kt[0][:, 0] + kt[1][:, 0] + jnp.eye(gc, dtype=F32)[None] * dsk[:, :, None]
    ks = jnp.concatenate([kt[1][:, 1:][:, ::-1], centre[:, None], kt[0][:, 1:]], axis=1)
    ks = jnp.transpose(ks.reshape(nblk, gl, 2 * t - 1, gc, gc), (0, 2, 1, 3, 4)).reshape(nblk, 2 * t - 1, LANE, gc)
    same_group = (np.arange(LANE)[:, None] // gc == np.arange(LANE)[None, :] // gc).astype(np.float32)
    d_tab = jnp.tile(ks, (1, 1, 1, gl)) * same_group

    def compact(zc, im_sign):
        z = jnp.concatenate([zc.real, im_sign * zc.imag], axis=-1)
        z = jnp.transpose(z, (2, 1, 0, 3, 4)).reshape(nblk, gl, t, 2, gc, 2 * ns)
        return jnp.transpose(z, (0, 2, 1, 3, 4, 5))

    b_t = jnp.swapaxes(b_bar, -1, -2)
    pf = lam_pw[:t][::-1][:, 0, :, None, :] * b_t[0][None]
    pb = lam_pw[:t][:, 1, :, None, :] * b_t[1][None]
    p_tab = compact(jnp.stack([pf, pb], axis=0), 1.0)
    qf = c_mat[0][None] * lam_pw[1:t + 1, 0][:, :, None, :]
    qb = c_mat[1][None] * lam_pw[1:t + 1][::-1][:, 1][:, :, None, :]
    q_tab = compact(jnp.stack([qf, qb], axis=0), -1.0)

    a = lam_pw[t]
    m_op, p_op, q_op = _s5_expand(d_tab, p_tab, q_tab)
    return m_op, p_op, q_op, a.real.reshape(2, g * ns), a.imag.reshape(2, g * ns)


def _s5_m_kernel(d_ref, o_ref, *, t):
    for s in range(t):
        for tt in range(t):
            o_ref[0, s * LANE:(s + 1) * LANE, tt * LANE:(tt + 1) * LANE] = d_ref[0, tt - s + t - 1].astype(o_ref.dtype)


def _s5_pq_kernel(c_ref, o_ref, *, transpose):
    t, gl, nd, gc, w = c_ref.shape[1:]
    ns = w // 2
    lane_grp = lax.broadcasted_iota(jnp.int32, (gc, gl * ns), 1) // ns
    for j in range(t):
        rows = []
        for g in range(gl):
            cols = []
            for dd in range(nd):
                piece = c_ref[0, j, g, dd]
                for ri in range(2):
                    tiled = jnp.concatenate([piece[:, ri * ns:(ri + 1) * ns]] * gl, axis=1)
                    cols.append(jnp.where(lane_grp == g, tiled, 0.0))
            rows.append(jnp.concatenate(cols, axis=1))
        blk = jnp.concatenate(rows, axis=0)
        if transpose:
            o_ref[0, :, j * LANE:(j + 1) * LANE] = blk.T.astype(o_ref.dtype)
        else:
            o_ref[0, j * LANE:(j + 1) * LANE, :] = blk.astype(o_ref.dtype)


def _s5_expand(d_tab, p_tab, q_tab):
    nblk, nlag = d_tab.shape[:2]
    t = (nlag + 1) // 2
    _, _, gl, nd, gc, w = p_tab.shape
    ncol = nd * gl * w
    whole = lambda shape: pl.BlockSpec((1,) + shape, lambda b: (b,) + (0,) * len(shape))
    m_op = pl.pallas_call(
        functools.partial(_s5_m_kernel, t=t), grid=(nblk,),
        in_specs=[whole((nlag, LANE, LANE))], out_specs=whole((t * LANE, t * LANE)),
        out_shape=jax.ShapeDtypeStruct((nblk, t * LANE, t * LANE), BF16),
        compiler_params=_cp("parallel"), name="s5_m_op",
    )(d_tab)
    tab_spec = whole((t, gl, nd, gc, w))
    p_op = pl.pallas_call(
        functools.partial(_s5_pq_kernel, transpose=False), grid=(nblk,),
        in_specs=[tab_spec], out_specs=whole((t * LANE, ncol)),
        out_shape=jax.ShapeDtypeStruct((nblk, t * LANE, ncol), BF16),
        compiler_params=_cp("parallel"), name="s5_p_op",
    )(p_tab)
    q_op = pl.pallas_call(
        functools.partial(_s5_pq_kernel, transpose=True), grid=(nblk,),
        in_specs=[tab_spec], out_specs=whole((ncol, t * LANE)),
        out_shape=jax.ShapeDtypeStruct((nblk, ncol, t * LANE), BF16),
        compiler_params=_cp("parallel"), name="s5_q_op",
    )(q_tab)
    return m_op, p_op, q_op


def _s5_sum_kernel(*refs, t):
    u_refs = refs[:t]
    p_ref = refs[t]
    outs = refs[t + 1:]
    u = jnp.concatenate([r[...] for r in u_refs], axis=1)
    res = _dot(u, p_ref[0])
    w = res.shape[1] // len(outs)
    for i, o in enumerate(outs):
        o[...] = res[:, i * w:(i + 1) * w]


def _s5_sum_call(h, p_op, rb):
    rows = h.shape[0]
    t = S5_T
    nblk = p_op.shape[0]
    w = p_op.shape[2] // 4
    u_specs = [pl.BlockSpec((rb, LANE), lambda gb, r, s=s: (r, s * nblk + gb)) for s in range(t)]
    out_spec = pl.BlockSpec((rb, w), lambda gb, r: (r, gb))
    return pl.pallas_call(
        functools.partial(_s5_sum_kernel, t=t), grid=(nblk, rows // rb),
        in_specs=u_specs + [pl.BlockSpec((1,) + p_op.shape[1:], lambda gb, r: (gb, 0, 0))],
        out_specs=[out_spec] * 4,
        out_shape=[jax.ShapeDtypeStruct((rows, nblk * w), F32)] * 4,
        compiler_params=_cp("parallel", "parallel"), name="s5_sum",
    )(*([h] * t), p_op)


def _s5_rec_kernel(sr_ref, si_ref, ar_ref, ai_ref, h0r_ref, h0i_ref, hr_ref, hi_ref, fr_ref, fi_ref,
                   cr, ci, *, kb, reverse):
    @pl.when(pl.program_id(1) == 0)
    def _():
        cr[...] = h0r_ref[0]
        ci[...] = h0i_ref[0]

    ar, ai = ar_ref[...], ai_ref[...]

    def body(i, carry):
        hr, hi = carry
        k = kb - 1 - i if reverse else i
        hr_ref[pl.ds(k, 1), :] = hr
        hi_ref[pl.ds(k, 1), :] = hi
        nr = ar * hr - ai * hi + sr_ref[pl.ds(k, 1), :]
        ni = ar * hi + ai * hr + si_ref[pl.ds(k, 1), :]
        return nr, ni

    hr, hi = lax.fori_loop(0, kb, body, (cr[...], ci[...]))
    cr[...] = hr
    ci[...] = hi
    fr_ref[0] = hr
    fi_ref[0] = hi


def _s5_rec_call(sr, si, ar, ai, h0r, h0i, nb_batch, reverse):
    rows, w = sr.shape
    nk = rows // nb_batch
    kb = min(64, nk)
    nb = nk // kb
    blk = (lambda bi, i: (bi * nb + nb - 1 - i, 0)) if reverse else (lambda bi, i: (bi * nb + i, 0))
    row_spec = pl.BlockSpec((kb, w), blk)
    vec = _const_spec((1, w))
    st = pl.BlockSpec((1, 1, w), lambda bi, i: (bi, 0, 0))
    return pl.pallas_call(
        functools.partial(_s5_rec_kernel, kb=kb, reverse=reverse), grid=(nb_batch, nb),
        in_specs=[row_spec, row_spec, vec, vec, st, st],
        out_specs=[row_spec, row_spec, st, st],
        out_shape=[jax.ShapeDtypeStruct((rows, w), F32)] * 2 + [jax.ShapeDtypeStruct((nb_batch, 1, w), F32)] * 2,
        scratch_shapes=[pltpu.VMEM((1, w), F32), pltpu.VMEM((1, w), F32)],
        compiler_params=_cp("parallel", "arbitrary"), name="s5_rec",
    )(sr, si, ar, ai, h0r, h0i)


def _s5_out_kernel(*refs, t):
    u_refs = refs[:t]
    h_refs = refs[t:t + 4]
    m_ref, q_ref, o_ref, tok_scr = refs[t + 4:]
    u = jnp.concatenate([r[...] for r in u_refs], axis=1)
    hcat = jnp.concatenate([r[...].astype(BF16) for r in h_refs], axis=1)
    y = _dot(u, m_ref[0]) + _dot(hcat, q_ref[0])
    g = 0.5 * y * (1.0 + lax.erf(y * (2.0 ** -0.5)))
    rb = g.shape[0]
    for j in range(t):
        tok_scr[pl.ds(j, rb, stride=t), :] = g[:, j * LANE:(j + 1) * LANE]
    o_ref[...] = tok_scr[...].astype(o_ref.dtype)


def _s5_out_call(h, states, m_op, q_op, rb):
    rows = h.shape[0]
    t = S5_T
    nblk = m_op.shape[0]
    w = q_op.shape[1] // 4
    u_specs = [pl.BlockSpec((rb, LANE), lambda gb, r, s=s: (r, s * nblk + gb)) for s in range(t)]
    return pl.pallas_call(
        functools.partial(_s5_out_kernel, t=t), grid=(nblk, rows // rb),
        in_specs=u_specs + [pl.BlockSpec((rb, w), lambda gb, r: (r, gb))] * 4
        + [pl.BlockSpec((1,) + m_op.shape[1:], lambda gb, r: (gb, 0, 0)),
           pl.BlockSpec((1,) + q_op.shape[1:], lambda gb, r: (gb, 0, 0))],
        out_specs=pl.BlockSpec((rb * t, LANE), lambda gb, r: (r, gb)),
        out_shape=jax.ShapeDtypeStruct((rows * t, nblk * LANE), BF16),
        scratch_shapes=[pltpu.VMEM((rb * t, LANE), F32)],
        compiler_params=_cp("parallel", "parallel"), name="s5_out",
    )(*([h] * t), *states, m_op, q_op)


def _s5_layer(xl, hl, hc, mods_l, fin, lam_re, lam_im, log_dt, b_re, b_im, c_re, c_im,
              d_skip, w_glu, b_glu, b, n, c):
    t = S5_T
    m_op, p_op, q_op, a_re, a_im = _s5_operators(lam_re, lam_im, log_dt, b_re, b_im, c_re, c_im, d_skip)
    w = a_re.shape[-1]

    def scan(h, init):
        sfr, sfi, sbr, sbi = _s5_sum_call(h, p_op, min(ROW_TILE, h.shape[0]))
        hfr, hfi, ffr, ffi = _s5_rec_call(sfr, sfi, a_re[0:1], a_im[0:1], init[0], init[1], b, False)
        hbr, hbi, fbr, fbi = _s5_rec_call(sbr, sbi, a_re[1:2], a_im[1:2], init[2], init[3], b, True)
        return (hfr, hfi, hbr, hbi), (ffr, ffi, fbr, fbi)

    zeros = jnp.zeros((b, 1, w), F32)
    _, ctx_final = scan(hc, (zeros,) * 4)
    states, _ = scan(hl, ctx_final)
    nk = n // t
    g_nat = _s5_out_call(hl, states, m_op, q_op, min(ROW_TILE, b * nk))
    return fin(xl, g_nat, mods_l, w_glu.astype(BF16), b_glu, min(ROW_TILE, n), n, glu=True)


def _fnet_channel_mats():
    cc, sc = _dft_cs(FNET_GC, FNET_GC, FNET_GC)
    return jnp.asarray(np.concatenate([cc, sc], axis=1) / np.sqrt(FNET_GC), BF16)


def _fn_c_kernel(l5_ref, l6_ref, xr_ref, xi_ref, o_ref, *, nh):
    r = KRON_R
    n2 = nh * r
    l5 = l5_ref[...]
    y5 = [_dot(l5, jnp.concatenate([xr_ref[0, f], xi_ref[0, f]], axis=0)).astype(BF16) for f in range(r)]
    l6 = l6_ref[...]
    for p in range(nh):
        x = jnp.concatenate([y5[f][p * r:(p + 1) * r] for f in range(r)]
                            + [y5[f][n2 + p * r:n2 + (p + 1) * r] for f in range(r)], axis=0)
        out = _dot(l6, x)
        for q in range(r):
            o_ref[0, p + nh * q] = out[q * r:(q + 1) * r].astype(o_ref.dtype)


def _fnet_layer(xl, am, bm, mods_l, fin, w_o, b_o, b, n, d):
    n2 = FFT_N2
    n1 = n // n2
    tm = min(ROW_TILE, n)
    a, _ = _split_radix(n1)
    l1, l2 = _slab_dft_mats(n1, a, 0, -1, neg_im=True, scale=1.0 / np.sqrt(n))
    twc, tws = _twiddle_tables(n1, n2)
    a4, b4 = am.reshape(b, n1, n2, d), bm.reshape(b, n1, n2, d)
    xr, xi = _kfa_call([(a4, None), (b4, None)], b, n1, n2, d, l1, l2, twc, tws)

    r, w = KRON_R, min(KRON_W, d)
    nh = n2 // r
    assert n1 % r == 0 and n2 % r == 0
    m5 = np.zeros((n2, n2), np.complex128)
    m6 = np.zeros((r * r, r * r), np.complex128)
    for s in range(r):
        for p in range(nh):
            for h in range(nh):
                m5[p * r + s, h * r + s] = np.exp(-2j * np.pi * (p * h / nh + p * s / n2))
        for q in range(r):
            for f in range(r):
                m6[q * r + f, f * r + s] = np.exp(-2j * np.pi * q * s / r)
    l5 = jnp.asarray(_cblock(m5), BF16)
    l6 = jnp.asarray(np.concatenate([m6.real, -m6.imag], axis=1), BF16)
    grp = pl.BlockSpec((1, r, n2, w), lambda bi, fh, k: (bi, fh, 0, k))
    y = pl.pallas_call(
        functools.partial(_fn_c_kernel, nh=nh), grid=(b, n1 // r, d // w),
        in_specs=[_const_spec(l5.shape), _const_spec(l6.shape), grp, grp],
        out_specs=pl.BlockSpec((1, n2, None, r, w), lambda bi, fh, k: (bi, 0, fh, 0, k)),
        out_shape=jax.ShapeDtypeStruct((b, n2, n1 // r, r, d), BF16),
        compiler_params=_cp("parallel", "parallel", "parallel"), name="fn_c",
    )(l5, l6, xr, xi)
    return fin(xl, y.reshape(b * n, d), mods_l, w_o.astype(BF16), b_o, tm, n)


def kernel(x, c, ctx, c_ctx, mod_w, mod_b, mix_pre_g, mix_post_g, ffn_pre_g, ffn_post_g, ffn_w13, ffn_w2,
           mla_w_in, mla_q_norm_g, mla_kv_norm_g, mla_w_uq, mla_w_ukv, mla_w_o,
           hy_w_in, hy_b_in, hy_conv_w, hy_conv_b, hy_f_w1, hy_f_b1, hy_f_freq, hy_f_w2, hy_f_b2, hy_f_w3,
           hy_skip, hy_w_out, hy_b_out,
           s5_lambda_re, s5_lambda_im, s5_log_dt, s5_b_re, s5_b_im, s5_c_re, s5_c_im, s5_d, s5_w_glu, s5_b_glu,
           fn_w_o, fn_b_o):
    b, n, d = x.shape
    cl = ctx.shape[1]
    depth = mod_w.shape[0]
    assert b == 2 and depth == 4, "two batches ride one complex transform; one layer per mixer"
    mods = _mods(c, c_ctx, mod_w, mod_b)
    xl = x.reshape(b * n, d)
    xc = ctx.reshape(b * cl, d)

    w13_all = ffn_w13.astype(BF16)
    w2_all = ffn_w2.astype(BF16)

    def mods_c(i):
        return jnp.broadcast_to(mods[i, 2:3], (b, 8, d))

    def finisher(i, emit_kind=None, cs=None):
        def fin(x_, y_, mods_, wm, bm, tm_, rows_per_batch, glu=False, is_ctx=False):
            emit = None
            if emit_kind is not None:
                emit = (emit_kind, mods_c(i + 1) if is_ctx else mods[i + 1, 0:2], mix_pre_g[i + 1], cs)
            return _mix_ffn_call(x_, y_, mods_, wm, bm, mix_post_g[i], ffn_pre_g[i], ffn_post_g[i],
                                 w13_all, w2_all, i, tm_, rows_per_batch, glu, emit)
        return fin

    xl, xc = _mla_layer(xl, xc, mods[0, 0:2], mods_c(0), mix_pre_g[0], finisher(0), mla_w_in[0],
                        mla_q_norm_g[0], mla_kv_norm_g[0], mla_w_uq[0], mla_w_ukv[0], mla_w_o[0], b, n, cl)
    filt = (hy_f_w1[0], hy_f_b1[0], hy_f_freq[0], hy_f_w2[0], hy_f_b2[0], hy_f_w3[0])
    xl, hl, hc = _hyena_layer(xl, xc, mods[1, 0:2], mods_c(1), mix_pre_g[1], finisher(1, "chunks"), hy_w_in[0],
                              hy_b_in[0], hy_conv_w[0], hy_conv_b[0], filt, hy_skip[0], hy_w_out[0], hy_b_out[0],
                              b, n, cl)
    xl, (am, bm) = _s5_layer(xl, hl, hc, mods[2, 0:2], finisher(2, "fnet", _fnet_channel_mats()), s5_lambda_re[0],
                             s5_lambda_im[0], s5_log_dt[0], s5_b_re[0], s5_b_im[0], s5_c_re[0], s5_c_im[0],
                             s5_d[0], s5_w_glu[0], s5_b_glu[0], b, n, cl)
    xl = _fnet_layer(xl, am, bm, mods[3, 0:2], finisher(3), fn_w_o[0], fn_b_o[0], b, n, d)
    return xl.reshape(b, n, d)
```

```python
import functools
import math

import numpy as np
import jax
import jax.numpy as jnp
from jax import lax
from jax.experimental import pallas as pl
from jax.experimental.pallas import tpu as pltpu

F32 = jnp.float32
BF16 = jnp.bfloat16
NORM_EPS = 1e-6
LANE = 128
MXU_COLS = 256
ROW_TILE = 512
VMEM_LIMIT = 56 * 1024 * 1024
HIGHEST = lax.Precision.HIGHEST

GRID_W = 64
ROPE_THETA = 10000.0
MLA_HEADS = 8
MLA_NOPE = 128
MLA_ROPE = 64
MLA_V = 128
MLA_VT = MLA_V + 16
MLA_QK = MLA_NOPE + 2 * MLA_ROPE
HYENA_BANDS = 16
HYENA_TARGET = 1e-2
HYENA_FAST = 0.3
HYENA_SLOW = 1.5
S5_T = 16
FNET_GC = 128
FFT_N2 = 128


def _cp(*sem):
    return pltpu.CompilerParams(dimension_semantics=sem, vmem_limit_bytes=VMEM_LIMIT)


def _dot(a, b):
    return jnp.dot(a, b, preferred_element_type=F32)


def _dot_hi(a, b):
    return jnp.dot(a, b, preferred_element_type=F32, precision=HIGHEST)


def _dot_x3(a, b):
    a_hi = a.astype(BF16)
    b_hi = b.astype(BF16)
    a_lo = (a - a_hi.astype(F32)).astype(BF16)
    b_lo = (b - b_hi.astype(F32)).astype(BF16)
    return _dot(a_hi, b_hi) + (_dot(a_hi, b_lo) + _dot(a_lo, b_hi))


def _rms(x, g):
    ms = jnp.mean(x * x, axis=-1, keepdims=True)
    return x * lax.rsqrt(ms + NORM_EPS) * g


def _normmod(x, g, shift, scale):
    return _rms(x, g) * (1.0 + scale) + shift


def _const_spec(shape):
    nd = len(shape)
    return pl.BlockSpec(shape, lambda *_: (0,) * nd)


def _mods_kernel(st_ref, w_ref, b_ref, o_ref):
    st = st_ref[...]
    st = st * jax.nn.sigmoid(st)
    w = w_ref[0]
    rows = [jnp.sum(st[:, r:r + 1] * w, axis=0, keepdims=True) for r in range(3)]
    rows.append(jnp.zeros((5, w.shape[1]), F32))
    o_ref[0] = jnp.concatenate(rows, axis=0) + b_ref[0]


def _mods(c, c_ctx, mod_w, mod_b):
    depth, d, n6 = mod_w.shape
    st = jnp.zeros((d, 8), F32).at[:, 0:2].set(c.T).at[:, 2].set(c_ctx)
    tn = 1024
    out = pl.pallas_call(
        _mods_kernel,
        grid=(depth, n6 // tn),
        in_specs=[_const_spec((d, 8)),
                  pl.BlockSpec((1, d, tn), lambda i, j: (i, 0, j)),
                  pl.BlockSpec((1, 1, tn), lambda i, j: (i, 0, j))],
        out_specs=pl.BlockSpec((1, 8, tn), lambda i, j: (i, 0, j)),
        out_shape=jax.ShapeDtypeStruct((depth, 8, n6), F32),
        compiler_params=_cp("parallel", "parallel"),
        name="mods",
    )(st, mod_w, mod_b.reshape(depth, 1, n6))
    m = out[:, :3].reshape(depth, 3, n6 // d, d)
    return jnp.pad(m, ((0, 0), (0, 0), (0, 8 - n6 // d), (0, 0)))


def _row_specs(tm, d, tpb):
    x_spec = pl.BlockSpec((tm, d), lambda i: (i, 0))
    mod_spec = pl.BlockSpec((1, 8, d), lambda i: (i // tpb, 0, 0))
    return x_spec, mod_spec


def _channel_dft(h, cs):
    gc = FNET_GC
    ab = [_dot(h[:, k * gc:(k + 1) * gc], cs) for k in range(h.shape[1] // gc)]
    return (jnp.concatenate([z[:, :gc] for z in ab], axis=1), jnp.concatenate([z[:, gc:] for z in ab], axis=1))


def _mix_ffn_kernel(x_ref, y_ref, mod_ref, wm_ref, bm_ref, gm_ref, pre_ref, post_ref, w13_ref, w2_ref, *rest,
                    f, cuts, glu, emit):
    if emit == "chunks":
        nmod_ref, ng_ref, o_ref, h_ref, tok_scr = rest
    elif emit == "fnet":
        nmod_ref, ng_ref, cs_ref, o_ref, a_ref, b_ref = rest
    else:
        (o_ref,) = rest
    z = _dot(y_ref[...].astype(BF16), wm_ref[...]) + bm_ref[...]
    if glu:
        d = o_ref.shape[-1]
        z = z[:, :d] * jax.nn.sigmoid(z[:, d:])
    x = x_ref[...] + mod_ref[0, 2:3, :] * _rms(z, gm_ref[...])
    h = _normmod(x, pre_ref[...], mod_ref[0, 3:4, :], mod_ref[0, 4:5, :]).astype(BF16)
    acc = None
    for lo, hi in zip(cuts[:-1], cuts[1:]):
        a = _dot(h, w13_ref[:, lo:hi])
        b = _dot(h, w13_ref[:, f + lo:f + hi])
        gact = (a * jax.nn.sigmoid(a) * b).astype(BF16)
        part = _dot(gact, w2_ref[lo:hi, :])
        acc = part if acc is None else acc + part
    xo = x + mod_ref[0, 5:6, :] * _rms(acc, post_ref[...])
    o_ref[...] = xo
    if emit:
        hn = _normmod(xo, ng_ref[...], nmod_ref[0, 0:1, :], nmod_ref[0, 1:2, :])
        if emit == "chunks":
            tm, d = hn.shape
            for lt in range(d // LANE):
                tok_scr[lt * tm:(lt + 1) * tm, :] = hn[:, lt * LANE:(lt + 1) * LANE]
            for j in range(S5_T):
                for lt in range(d // LANE):
                    rows = tok_scr[pl.ds(lt * tm + j, tm // S5_T, stride=S5_T), :]
                    h_ref[:, j * d + lt * LANE:j * d + (lt + 1) * LANE] = rows.astype(h_ref.dtype)
        else:
            a, b = _channel_dft(hn.astype(BF16), cs_ref[...])
            a_ref[...] = a.astype(a_ref.dtype)
            b_ref[...] = b.astype(b_ref.dtype)


def _mix_ffn_call(x, y, mods, wm, bm, mix_post_g, ffn_pre_g, ffn_post_g, w13, w2, layer, tm, rows_per_batch,
                  glu=False, emit=None):
    m, d = x.shape
    k, nm = wm.shape
    f = w2.shape[1]
    cuts = tuple(range(0, f, 6 * MXU_COLS)) + (f,)
    x_spec, mod_spec = _row_specs(tm, d, rows_per_batch // tm)
    once = lambda shape: pl.BlockSpec(shape, lambda i: (0, 0), pipeline_mode=pl.Buffered(1))
    of_layer = lambda shape: pl.BlockSpec((None,) + shape, lambda i: (layer, 0, 0), pipeline_mode=pl.Buffered(1))
    in_specs = [x_spec, pl.BlockSpec((tm, k), lambda i: (i, 0)), mod_spec,
                once((k, nm)), _const_spec((1, nm)), _const_spec((1, d)), _const_spec((1, d)),
                _const_spec((1, d)), of_layer((d, 2 * f)), of_layer((f, d))]
    args = [x, y, mods, wm, bm.reshape(1, nm), mix_post_g.reshape(1, d), ffn_pre_g.reshape(1, d),
            ffn_post_g.reshape(1, d), w13, w2]
    out_specs, out_shape = [x_spec], [jax.ShapeDtypeStruct((m, d), F32)]
    kind = None
    scratch = []
    if emit is not None:
        kind, next_mods, next_g, cs = emit
        in_specs += [mod_spec, _const_spec((1, d))]
        args += [next_mods, next_g.reshape(1, d)]
        if kind == "fnet":
            in_specs.append(_const_spec(cs.shape))
            args.append(cs)
            out_specs += [x_spec] * 2
            out_shape += [jax.ShapeDtypeStruct((m, d), BF16)] * 2
        else:
            out_specs.append(pl.BlockSpec((tm // S5_T, S5_T * d), lambda i: (i, 0)))
            out_shape.append(jax.ShapeDtypeStruct((m // S5_T, S5_T * d), BF16))
            scratch.append(pltpu.VMEM((tm * (d // LANE), LANE), F32))
    outs = pl.pallas_call(
        functools.partial(_mix_ffn_kernel, f=f, cuts=cuts, glu=glu, emit=kind), grid=(m // tm,),
        in_specs=in_specs, out_specs=out_specs, out_shape=out_shape, scratch_shapes=scratch,
        compiler_params=_cp("parallel"), name="mix_ffn",
    )(*args)
    return outs[0] if emit is None else (outs[0], tuple(outs[1:]))


def _mla_proj_kernel(x_ref, mod_ref, g_ref, w_ref, qg_ref, kvg_ref, wq_ref, wkv_ref, cp_ref, sp_ref,
                     q_ref, k_ref, vt_ref, *, ql, kvl, qscale, positional):
    h = _normmod(x_ref[...], g_ref[...], mod_ref[0, 0:1, :], mod_ref[0, 1:2, :])
    z = _dot(h.astype(BF16), w_ref[...])
    qn = _rms(z[:, :ql], qg_ref[...]).astype(BF16)
    cn = _rms(z[:, ql:ql + kvl], kvg_ref[...]).astype(BF16)
    tk = z.shape[0]
    low = lax.broadcasted_iota(jnp.int32, (tk, 2 * MLA_ROPE), 1) < MLA_ROPE
    if positional:
        cpf = jnp.concatenate([cp_ref[...]] * 2, axis=1)
        spf = jnp.concatenate([sp_ref[...]] * 2, axis=1)

    def rope_slots(pair, keep_raw):
        swapped = pltpu.roll(pair, MLA_ROPE, 1)
        raw = jnp.where(low, 0.0, swapped) if keep_raw else None
        if not positional:
            return raw
        rot = jnp.where(low, pair * cpf + swapped * spf, 0.0)
        return rot + raw if keep_raw else rot

    kr = rope_slots(z[:, ql + kvl:], keep_raw=not positional).astype(BF16)

    zkv = _dot(cn, wkv_ref[...])
    ones_blk = (lax.broadcasted_iota(jnp.int32, (MLA_VT - MLA_V, tk), 0) == 0).astype(BF16)
    for hd in range(MLA_HEADS):
        base = hd * (MLA_NOPE + MLA_V)
        k_ref[0, hd, 0, :, 0:MLA_NOPE] = zkv[:, base:base + MLA_NOPE].astype(BF16)
        k_ref[0, hd, 0, :, MLA_NOPE:MLA_QK] = kr
        vt_ref[0, hd, 0, 0:MLA_V, :] = zkv[:, base + MLA_NOPE:base + MLA_NOPE + MLA_V].T.astype(BF16)
        vt_ref[0, hd, 0, MLA_V:MLA_VT, :] = ones_blk

    zq = _dot(qn, wq_ref[...])
    for hd in range(MLA_HEADS):
        base = hd * MLA_QK
        rp = rope_slots(zq[:, base + MLA_NOPE:base + MLA_QK], keep_raw=True)
        qcat = jnp.concatenate([zq[:, base:base + MLA_NOPE], rp], axis=1) * qscale
        q_ref[0, hd] = qcat.T.astype(BF16)


def _flash_kernel(q_ref, kc_ref, vc_ref, *rest, n_lat):
    if n_lat:
        kl_ref, vl_ref, o_ref, s_scr, acc_scr = rest
    else:
        o_ref, acc_scr = rest
    qt = q_ref[0, 0]

    def qk(k, slot):
        s = _dot(k, qt)
        s_scr[slot] = s
        return jnp.max(s, axis=0, keepdims=True)

    def sm_pv(slot, vt, m, mx):
        m_new = jnp.maximum(m, mx)
        alpha = jnp.exp2(m - m_new)
        p = jnp.exp2(s_scr[slot] - m_new).astype(BF16)
        acc_scr[...] = alpha * acc_scr[...] + _dot(vt, p)
        return m_new

    sc = _dot(kc_ref[0, 0, 0], qt)
    if n_lat:
        mx = qk(kl_ref[0, 0, 0], 0)
    m = jnp.max(sc, axis=0, keepdims=True)
    acc_scr[...] = _dot(vc_ref[0, 0, 0], jnp.exp2(sc - m).astype(BF16))
    if n_lat:

        per = next(p for p in (16, 8, 4, 2) if n_lat % p == 0)

        def body(i, carry):
            m, mx_cur = carry
            c = per * i
            for u in range(per):
                mx_next = qk(kl_ref[0, 0, jnp.minimum(c + u + 1, n_lat - 1)], (u + 1) % 2)
                m = sm_pv(u % 2, vl_ref[0, 0, c + u], m, mx_cur)
                mx_cur = mx_next
            return m, mx_cur

        lax.fori_loop(0, n_lat // per, body, (m, mx))
    acc = acc_scr[...]
    o_ref[0] = (acc[0:MLA_V] / acc[MLA_V:MLA_V + 1]).T.astype(o_ref.dtype)


def _rope_tables(n_lat):
    rows = n_lat // GRID_W
    row = jnp.repeat(jnp.arange(rows, dtype=F32), GRID_W)
    col = jnp.tile(jnp.arange(GRID_W, dtype=F32), rows)
    axis_dim = MLA_ROPE // 2
    inv_freq = 1.0 / (ROPE_THETA ** (jnp.arange(0, axis_dim, 2, dtype=F32) / axis_dim))
    ang_r = row[:, None] * inv_freq
    ang_c = col[:, None] * inv_freq
    cr, sr, cc, sc = jnp.cos(ang_r), jnp.sin(ang_r), jnp.cos(ang_c), jnp.sin(ang_c)
    cp = jnp.concatenate([cr, cr, cc, cc], axis=-1)
    sp = jnp.concatenate([-sr, sr, -sc, sc], axis=-1)
    return cp, sp


_ROPE_SWAP = np.concatenate([np.arange(16, 32), np.arange(0, 16), np.arange(48, 64), np.arange(32, 48)])


def _mla_side(x, mods, pre_g, w_in_ext, q_g, kv_g, w_uq_ext, w_ukv, tabs, b, n, tk, positional):
    m, d = x.shape
    ql, kvl = q_g.shape[-1], kv_g.shape[-1]
    nc = n // tk
    x_spec, mod_spec = _row_specs(tk, d, nc)
    tab_spec = pl.BlockSpec((tk, MLA_ROPE), lambda i: (i % nc, 0))
    qscale = (MLA_NOPE + MLA_ROPE) ** -0.5 * math.log2(math.e)
    return pl.pallas_call(
        functools.partial(_mla_proj_kernel, ql=ql, kvl=kvl, qscale=qscale, positional=positional),
        grid=(m // tk,),
        in_specs=[x_spec, mod_spec, _const_spec((1, d)), _const_spec(w_in_ext.shape),
                  _const_spec((1, ql)), _const_spec((1, kvl)), _const_spec(w_uq_ext.shape),
                  _const_spec(w_ukv.shape), tab_spec, tab_spec],
        out_specs=[pl.BlockSpec((1, MLA_HEADS, MLA_QK, tk), lambda i: (i // nc, 0, 0, i % nc)),
                   pl.BlockSpec((1, MLA_HEADS, 1, tk, MLA_QK), lambda i: (i // nc, 0, i % nc, 0, 0)),
                   pl.BlockSpec((1, MLA_HEADS, 1, MLA_VT, tk), lambda i: (i // nc, 0, i % nc, 0, 0))],
        out_shape=[jax.ShapeDtypeStruct((b, MLA_HEADS, MLA_QK, n), BF16),
                   jax.ShapeDtypeStruct((b, MLA_HEADS, nc, tk, MLA_QK), BF16),
                   jax.ShapeDtypeStruct((b, MLA_HEADS, nc, MLA_VT, tk), BF16)],
        compiler_params=_cp("parallel"), name="mla_proj",
    )(x, mods, pre_g.reshape(1, d), w_in_ext, q_g.reshape(1, ql), kv_g.reshape(1, kvl), w_uq_ext, w_ukv, *tabs)


def _flash_call(qt, kc, vtc, kl, vtl, tq):
    b, hh, _, n = qt.shape
    c = kc.shape[-2]
    n_lat = 0 if kl is None else kl.shape[2]
    in_specs = [pl.BlockSpec((1, 1, MLA_QK, tq), lambda bi, h, i: (bi, h, 0, i)),
                pl.BlockSpec((1, 1, 1, c, MLA_QK), lambda bi, h, i: (bi, h, 0, 0, 0)),
                pl.BlockSpec((1, 1, 1, MLA_VT, c), lambda bi, h, i: (bi, h, 0, 0, 0))]
    args = [qt, kc, vtc]
    scratch = [pltpu.VMEM((MLA_VT, tq), F32)]
    if n_lat:
        assert n_lat % 2 == 0, "latent key chunks are consumed in pairs"
        tk = kl.shape[-2]
        in_specs += [pl.BlockSpec((1, 1, n_lat, tk, MLA_QK), lambda bi, h, i: (bi, h, 0, 0, 0)),
                     pl.BlockSpec((1, 1, n_lat, MLA_VT, tk), lambda bi, h, i: (bi, h, 0, 0, 0))]
        args += [kl, vtl]
        scratch = [pltpu.VMEM((2, tk, tq), F32)] + scratch
    return pl.pallas_call(
        functools.partial(_flash_kernel, n_lat=n_lat), grid=(b, hh, n // tq),
        in_specs=in_specs,
        out_specs=pl.BlockSpec((1, tq, MLA_V), lambda bi, h, i: (bi, i, h)),
        out_shape=jax.ShapeDtypeStruct((b, n, hh * MLA_V), BF16),
        scratch_shapes=scratch,
        compiler_params=_cp("parallel", "parallel", "arbitrary"), name="flash",
    )(*args)


def _mla_layer(xl, xc, mods_l, mods_c, pre_g, fin, w_in, q_g, kv_g, w_uq, w_ukv, w_o, b, n, c):
    d = xl.shape[-1]
    ql, kvl = q_g.shape[-1], kv_g.shape[-1]
    hh = MLA_HEADS
    rope_cols = w_in[:, ql + kvl:]
    w_in_ext = jnp.concatenate([w_in, rope_cols[:, _ROPE_SWAP]], axis=1).astype(BF16)
    wq = w_uq.reshape(ql, hh, MLA_NOPE + MLA_ROPE)
    w_uq_ext = jnp.concatenate([wq, wq[:, :, MLA_NOPE:][:, :, _ROPE_SWAP]], axis=-1)
    w_uq_ext = w_uq_ext.reshape(ql, hh * MLA_QK).astype(BF16)
    w_ukv_b = w_ukv.astype(BF16)
    w_o_b = w_o.astype(BF16)

    tabs = _rope_tables(n)

    tm_l = min(ROW_TILE, n)
    tk_l = min(ROW_TILE, n // 2)
    ql_, kl, vtl = _mla_side(xl, mods_l, pre_g, w_in_ext, q_g, kv_g, w_uq_ext, w_ukv_b, tabs, b, n, tk_l, True)
    qc_, kc, vtc = _mla_side(xc, mods_c, pre_g, w_in_ext, q_g, kv_g, w_uq_ext, w_ukv_b, tabs, b, c, c, False)
    o_lat = _flash_call(ql_, kc, vtc, kl, vtl, min(2 * ROW_TILE, n)).reshape(b * n, hh * MLA_V)
    o_ctx = _flash_call(qc_, kc, vtc, None, None, c).reshape(b * c, hh * MLA_V)
    zb = jnp.zeros((d,), F32)
    xl = fin(xl, o_lat, mods_l, w_o_b, zb, tm_l, n)
    xc = fin(xc, o_ctx, mods_c, w_o_b, zb, c, c)
    return xl, xc


def _hy_in_kernel(x_ref, xp_ref, xn_ref, mod_ref, g_ref, w_ref, b_ref, cw_ref, cb_ref,
                  g0_ref, vg_ref, *, tpb):
    i = pl.program_id(0)
    g = g_ref[...]
    shift, scale = mod_ref[0, 0:1, :], mod_ref[0, 1:2, :]
    xcat = jnp.concatenate([xp_ref[...], x_ref[...], xn_ref[...]], axis=0)
    hcat = _normmod(xcat, g, shift, scale).astype(BF16)
    tm = x_ref.shape[0]
    d = g0_ref.shape[-1]
    first = (i % tpb) == 0
    last = (i % tpb) == tpb - 1
    ridx = lax.broadcasted_iota(jnp.int32, (tm, 1), 0)

    def conv_part(c):
        cols = slice(c * d, (c + 1) * d)
        ucat = _dot(hcat, w_ref[:, cols]) + b_ref[:, cols]
        u = ucat[8:tm + 8]
        prev_row = jnp.where(first, 0.0, ucat[7:8, :])
        next_row = jnp.where(last, 0.0, ucat[tm + 8:tm + 9, :])
        dn = jnp.where(ridx == 0, prev_row, pltpu.roll(u, 1, 0))
        upw = jnp.where(ridx == tm - 1, next_row, pltpu.roll(u, tm - 1, 0))
        return cb_ref[:, cols] + dn * cw_ref[0:1, cols] + u * cw_ref[1:2, cols] + upw * cw_ref[2:3, cols]

    g0_ref[...] = conv_part(0).astype(g0_ref.dtype)
    vg_ref[...] = (conv_part(2) * conv_part(1)).astype(vg_ref.dtype)


def _hy_in_call(x, mods, pre_g, w_in, b_in, conv_w, conv_b, tm, n, out_dtype):
    m, d = x.shape
    p = w_in.shape[1]
    tpb = n // tm
    x_spec, mod_spec = _row_specs(tm, d, tpb)
    r8 = tm // 8
    nb8 = m // 8
    prev_spec = pl.BlockSpec((8, d), lambda i: (jnp.maximum(i * r8 - 1, 0), 0))
    next_spec = pl.BlockSpec((8, d), lambda i: (jnp.minimum((i + 1) * r8, nb8 - 1), 0))
    cw = jnp.pad(conv_w, ((0, 8 - conv_w.shape[0]), (0, 0)))
    return pl.pallas_call(
        functools.partial(_hy_in_kernel, tpb=tpb), grid=(m // tm,),
        in_specs=[x_spec, prev_spec, next_spec, mod_spec, _const_spec((1, d)), _const_spec((d, p)),
                  _const_spec((1, p)), _const_spec((8, p)), _const_spec((1, p))],
        out_specs=[x_spec, x_spec],
        out_shape=[jax.ShapeDtypeStruct((m, d), out_dtype), jax.ShapeDtypeStruct((m, d), out_dtype)],
        compiler_params=_cp("parallel"), name="hy_in",
    )(x, x, x, mods, pre_g.reshape(1, d), w_in, b_in.reshape(1, p), cw, conv_b.reshape(1, p))


_PI_SPLIT = (3.140625, 9.67502593994140625e-4, 1.509957990978376432e-7)
_SIN_TAYLOR = (-1.0 / 6, 1.0 / 120, -1.0 / 5040, 1.0 / 362880, -1.0 / 39916800)


def _sin(x):
    kf = jnp.round(x * (1.0 / math.pi))
    r = ((x - kf * _PI_SPLIT[0]) - kf * _PI_SPLIT[1]) - kf * _PI_SPLIT[2]
    r2 = r * r
    p = _SIN_TAYLOR[4]
    for c in _SIN_TAYLOR[3::-1]:
        p = p * r2 + c
    s = r + r * r2 * p
    odd = (kf.astype(jnp.int32) & 1) == 1
    return jnp.where(odd, -s, s)


def _hy_filter_kernel(bands_ref, w1_ref, b1_ref, fq_ref, w2_ref, b2_ref, w3_ref, dl_ref,
                      k_ref, nrm_ref, *, n, tr):
    i = pl.program_id(0)
    bwd = i >= n // tr
    row = lax.broadcasted_iota(jnp.int32, (tr, LANE), 0) + i * tr
    j = jnp.where(bwd, 2 * n - row, row).astype(F32)
    lane = lax.broadcasted_iota(jnp.int32, (tr, LANE), 1)
    t = j * (1.0 / (n - 1))
    arg = (2.0 * math.pi / n) * j * bands_ref[0:1, :] + bands_ref[1:2, :]
    z = jnp.where(lane == 0, t, jnp.where(lane <= 2 * HYENA_BANDS, _sin(arg), 0.0))
    fq = fq_ref[...]
    a = _sin(fq * (_dot_x3(z, w1_ref[...]) + b1_ref[...]))
    for k in range(w2_ref.shape[0]):
        a = _sin(fq * (_dot_x3(a, w2_ref[k]) + b2_ref[k]))
    h = _dot_x3(a, w3_ref[jnp.where(bwd, 1, 0)])
    decay = jnp.exp(-t[:, 0:1] * dl_ref[...])
    k = h * decay
    k = jnp.where(row[:, 0:1] == n, 0.0, k)
    k_ref[...] = k.astype(k_ref.dtype)
    part = jnp.sum(jnp.abs(k), axis=0, keepdims=True)

    @pl.when(i == 0)
    def _():
        nrm_ref[...] = jnp.zeros_like(nrm_ref)

    nrm_ref[...] += jnp.broadcast_to(part, nrm_ref.shape)


def _hy_filter_call(n, d, out_dtype, f_w1, f_b1, f_freq, f_w2, f_b2, f_w3):
    fw = f_w1.shape[1]
    tr = min(ROW_TILE, n)
    bands_np = np.zeros((8, LANE), np.float32)
    bands_np[0, 1:1 + HYENA_BANDS] = np.linspace(1e-4, HYENA_BANDS - 1, HYENA_BANDS, dtype=np.float32)
    bands_np[0, 1 + HYENA_BANDS:1 + 2 * HYENA_BANDS] = bands_np[0, 1:1 + HYENA_BANDS]
    bands_np[1, 1:1 + HYENA_BANDS] = 0.5 * np.pi
    bands_np[1, 1 + HYENA_BANDS:1 + 2 * HYENA_BANDS] = np.pi
    w1p = jnp.zeros((LANE, fw), F32).at[:f_w1.shape[0]].set(f_w1)
    deltas = jnp.abs(jnp.linspace(math.log(HYENA_TARGET) / HYENA_SLOW, math.log(HYENA_TARGET) / HYENA_FAST,
                                  d, dtype=F32)).reshape(1, d)
    row = pl.BlockSpec((tr, d), lambda i: (i, 0))
    return pl.pallas_call(
        functools.partial(_hy_filter_kernel, n=n, tr=tr), grid=(2 * n // tr,),
        in_specs=[_const_spec((8, LANE)), _const_spec((LANE, fw)), _const_spec((1, fw)), _const_spec((1, fw)),
                  _const_spec(f_w2.shape), _const_spec((f_w2.shape[0], 1, fw)), _const_spec((2, fw, d)),
                  _const_spec((1, d))],
        out_specs=[row, _const_spec((8, d))],
        out_shape=[jax.ShapeDtypeStruct((2 * n, d), out_dtype), jax.ShapeDtypeStruct((8, d), F32)],
        compiler_params=_cp("arbitrary"), name="hy_filter",
    )(jnp.asarray(bands_np), w1p, f_b1.reshape(1, fw), f_freq.reshape(1, fw), f_w2,
      f_b2.reshape(f_w2.shape[0], 1, fw), jnp.transpose(f_w3.reshape(fw, 2, d), (1, 0, 2)), deltas)


def _dft_cs(nf, nt, period):
    ft = (np.arange(nf)[:, None] * np.arange(nt)[None, :]) % period
    ang = 2.0 * np.pi * ft / period
    return np.cos(ang), np.sin(ang)


def _twiddle_tables(n1, n2):
    nn = n1 * n2
    f1 = jnp.arange(n1, dtype=jnp.int32)
    t2 = jnp.arange(n2, dtype=jnp.int32)
    idx = (f1[:, None] * t2[None, :]) % nn
    ang = idx.astype(F32) * (2.0 * math.pi / nn)
    shape = idx.shape + (LANE,)
    return (jnp.broadcast_to(jnp.cos(ang)[..., None], shape),
            jnp.broadcast_to(jnp.sin(ang)[..., None], shape))


KRON_R = 16
KRON_H = 8
KRON_W = 256


def _cblock(mc):
    return np.block([[mc.real, -mc.imag], [mc.imag, mc.real]])


def _split_radix(n1):
    b = 16 if (n1 % 16 == 0 and n1 >= 64) else 4
    assert n1 % b == 0 and (n1 // b) % 2 == 0
    return n1 // b, b


def _slab_dft_mats(n1, a_in, a_out, sign, neg_im=False, real_in=False, scale=1.0):
    a, b = _split_radix(n1)
    r = KRON_H
    eye = np.eye(r)
    w = lambda num, den: np.exp(sign * 2j * np.pi * num / den)
    ua = np.arange(a)[:, None] * np.arange(a)[None, :]
    vb = np.arange(b)[:, None] * np.arange(b)[None, :]
    if sign < 0:
        m1 = np.kron(w(ua[:, :a_in], a) * scale, eye)
        l1 = _cblock(m1)
        if neg_im:
            l1[:, a_in * r:] *= -1.0
        if real_in:
            l1 = l1[:, :a_in * r]
        l2 = np.stack([_cblock(np.kron(w(vb, b) * w(u * np.arange(b)[None, :], n1), eye)) for u in range(a)])
    else:
        l1 = np.stack([_cblock(np.kron(w(vb, b) * w(u * np.arange(b)[:, None], n1), eye)) for u in range(a)])
        l2 = _cblock(np.kron(w(ua[:a_out, :], a) * scale, eye))
    return jnp.asarray(l1, BF16), jnp.asarray(l2, BF16)


def _kfa_kernel(*refs, nparts, a_in, a, b):
    l1_ref, l2_ref = refs[0], refs[1]
    parts = refs[2:2 + nparts]
    twc_ref, tws_ref, or_ref, oi_ref = refs[2 + nparts:]
    r = KRON_H
    halves = range(KRON_R // r)
    reps = or_ref.shape[-1] // LANE
    l1 = l1_ref[...]
    y1 = []
    for h in halves:
        rows = slice(h * r, (h + 1) * r)
        y1.append([_dot(l1, jnp.concatenate([p[0, aa * b + bb].astype(F32)[rows] for p in parts
                                             for aa in range(a_in)], axis=0).astype(BF16))
                   for bb in range(b)])
    for u in range(a):
        z = []
        for h in halves:
            x = jnp.concatenate([y1[h][bb][u * r:(u + 1) * r] for bb in range(b)]
                                + [y1[h][bb][(a + u) * r:(a + u + 1) * r] for bb in range(b)], axis=0)
            z.append(_dot(l2_ref[u], x.astype(BF16)))
        for v in range(b):
            f1 = u + a * v
            zr = jnp.concatenate([zh[v * r:(v + 1) * r] for zh in z], axis=0)
            zi = jnp.concatenate([zh[(b + v) * r:(b + v + 1) * r] for zh in z], axis=0)
            c = jnp.concatenate([twc_ref[f1]] * reps, axis=1)
            sn = jnp.concatenate([tws_ref[f1]] * reps, axis=1)
            or_ref[0, f1] = (zr * c + zi * sn).astype(or_ref.dtype)
            oi_ref[0, f1] = (zi * c - zr * sn).astype(oi_ref.dtype)


def _kfa_call(parts, nb_out, n1, n2, d, l1, l2, twc, tws):
    a, b = _split_radix(n1)
    a_in = parts[0][0].shape[1] // b
    r, w = KRON_R, min(KRON_W, d)
    in_specs = [_const_spec(l1.shape), pl.BlockSpec(l2.shape, lambda bi, j, k: (0, 0, 0), pipeline_mode=pl.Buffered(1))]
    args = [l1, l2]
    for arr, bi_fixed in parts:
        t1_in = arr.shape[1]
        if bi_fixed is None:
            in_specs.append(pl.BlockSpec((1, t1_in, r, w), lambda bi, j, k: (bi, 0, j, k)))
        else:
            in_specs.append(pl.BlockSpec((1, t1_in, r, w), lambda bi, j, k, f=bi_fixed: (f, 0, j, k)))
        args.append(arr)
    tw_spec = pl.BlockSpec((n1, r, LANE), lambda bi, j, k: (0, j, 0))
    out_spec = pl.BlockSpec((1, n1, r, w), lambda bi, j, k: (bi, 0, j, k))
    out = jax.ShapeDtypeStruct((nb_out, n1, n2, d), BF16)
    return pl.pallas_call(
        functools.partial(_kfa_kernel, nparts=len(parts), a_in=a_in, a=a, b=b),
        grid=(nb_out, n2 // r, d // w),
        in_specs=in_specs + [tw_spec, tw_spec], out_specs=[out_spec, out_spec], out_shape=[out, out],
        compiler_params=_cp("parallel", "parallel", "parallel"), name="kfa",
    )(*args, twc, tws)


def _kfc_kernel(l3_ref, l4_ref, gr_ref, gi_ref, vg_ref, g0_ref, skip_ref, o_ref, *, a, b, a_out):
    r = KRON_H
    halves = range(KRON_R // r)
    y3 = [[] for _ in halves]
    for u in range(a):
        gr = [gr_ref[0, u + a * v].astype(F32) for v in range(b)]
        gi = [gi_ref[0, u + a * v].astype(F32) for v in range(b)]
        for h in halves:
            rows = slice(h * r, (h + 1) * r)
            x = jnp.concatenate([g[rows] for g in gr] + [g[rows] for g in gi], axis=0).astype(BF16)
            y3[h].append(_dot(l3_ref[u], x))
    l4 = l4_ref[...]
    skip = skip_ref[...]
    for bb in range(b):
        y = []
        for h in halves:
            x = jnp.concatenate([y3[h][u][bb * r:(bb + 1) * r] for u in range(a)]
                                + [y3[h][u][(b + bb) * r:(b + bb + 1) * r] for u in range(a)], axis=0)
            y.append(_dot(l4, x.astype(BF16)))
        for sg in range(2):
            for aa in range(a_out):
                t1 = aa * b + bb
                yb = jnp.concatenate([yh[(sg * a_out + aa) * r:(sg * a_out + aa + 1) * r] for yh in y], axis=0)
                o_ref[sg, t1] = ((yb + vg_ref[sg, t1] * skip) * g0_ref[sg, t1]).astype(o_ref.dtype)


def _kfc_call(g1r, g1i, vg4, g04, skip, n1, n2, d, l3, l4):
    a, b = _split_radix(n1)
    nb, t1_out = vg4.shape[:2]
    r, w = KRON_R, min(KRON_W, d)
    slab = pl.BlockSpec((1, n1, r, w), lambda j, k: (0, 0, j, k))
    nat = pl.BlockSpec((nb, t1_out, r, w), lambda j, k: (0, 0, j, k))
    return pl.pallas_call(
        functools.partial(_kfc_kernel, a=a, b=b, a_out=t1_out // b), grid=(n2 // r, d // w),
        in_specs=[pl.BlockSpec(l3.shape, lambda j, k: (0, 0, 0), pipeline_mode=pl.Buffered(1)),
                  _const_spec(l4.shape), slab, slab, nat, nat, pl.BlockSpec((1, w), lambda j, k: (0, k))],
        out_specs=nat, out_shape=jax.ShapeDtypeStruct(vg4.shape, BF16),
        compiler_params=_cp("parallel", "parallel"), name="kfc",
    )(l3, l4, g1r, g1i, vg4, g04, skip.reshape(1, d))


def _hy_b_kernel(fm_ref, fmc_ref, xr_ref, xi_ref, fr_ref, fi_ref, sc_ref, twc_ref, tws_ref,
                 or_ref, oi_ref, *, nf, d):
    fm = fm_ref[...]
    fmc = fmc_ref[...]
    n2 = xr_ref.shape[1]
    sc = sc_ref[...]
    for s in range(nf):
        x = _dot(fm, jnp.concatenate([xr_ref[s], xi_ref[s]], axis=0))
        kk = _dot(fm, jnp.concatenate([fr_ref[s], fi_ref[s]], axis=0))
        xr, xi = x[:n2], x[n2:]
        kr, ki = kk[:n2] * sc, kk[n2:] * sc
        yr = (xr * kr - xi * ki).astype(BF16)
        yi = (xr * ki + xi * kr).astype(BF16)
        g = _dot(fmc, jnp.concatenate([yr, yi], axis=0))
        gr, gi = g[:n2], g[n2:]
        c = jnp.concatenate([twc_ref[s]] * (d // LANE), axis=1)
        sn = jnp.concatenate([tws_ref[s]] * (d // LANE), axis=1)
        or_ref[s] = (gr * c - gi * sn).astype(or_ref.dtype)
        oi_ref[s] = (gi * c + gr * sn).astype(oi_ref.dtype)


def _block_c(cs, sn, sign):
    return np.block([[cs, -sign * sn], [sign * sn, cs]])


def _hy_conv_long(vg, g0, k, nrm, skip, b, n, d):
    n2 = FFT_N2
    nn = 2 * n
    n1 = nn // n2
    rows = n // n2
    a, _ = _split_radix(n1)
    la1, la2 = _slab_dft_mats(n1, a // 2, 0, -1)
    lk1, lk2 = _slab_dft_mats(n1, a, 0, -1, real_in=True)
    lc3, lc4 = _slab_dft_mats(n1, 0, a // 2, +1)
    cs2, sn2 = _dft_cs(n2, n2, n2)
    fm_b = jnp.asarray(_block_c(cs2, sn2, -1.0), BF16)
    fm_bc = jnp.asarray(_block_c(cs2, sn2, 1.0), BF16)
    twc, tws = _twiddle_tables(n1, n2)

    kfr, kfi = _kfa_call([(k.reshape(1, n1, n2, d), 0)], 1, n1, n2, d, lk1, lk2, twc, tws)
    nf = 4 if n1 % 4 == 0 else 1
    scale = (1.0 / (nrm[0:1, :] * nn))
    slab = pl.BlockSpec((nf, n2, d), lambda j: (j, 0, 0))
    shp3 = (n1, n2, d)
    vg4 = vg.reshape(b, rows, n2, d)
    g04 = g0.reshape(b, rows, n2, d)
    x1r, x1i = _kfa_call([(vg4, 0), (vg4, 1)], 1, n1, n2, d, la1, la2, twc, tws)
    tw_slab = pl.BlockSpec((nf, n2, LANE), lambda j: (j, 0, 0))
    g1r, g1i = pl.pallas_call(
        functools.partial(_hy_b_kernel, nf=nf, d=d), grid=(n1 // nf,),
        in_specs=[_const_spec(fm_b.shape), _const_spec(fm_bc.shape), slab, slab, slab, slab, _const_spec((1, d)),
                  tw_slab, tw_slab],
        out_specs=[slab, slab],
        out_shape=[jax.ShapeDtypeStruct(shp3, BF16)] * 2,
        compiler_params=_cp("parallel"), name="hy_b",
    )(fm_b, fm_bc, x1r.reshape(shp3), x1i.reshape(shp3), kfr.reshape(shp3), kfi.reshape(shp3), scale, twc, tws)
    out = _kfc_call(g1r.reshape(1, n1, n2, d), g1i.reshape(1, n1, n2, d), vg4, g04, skip, n1, n2, d, lc3, lc4)
    return out.reshape(b * n, d)


def _hy_short_kernel(fa_ref, fk_ref, fi_ref, vg_ref, g0_ref, k_ref, nrm_ref, skip_ref, o_ref, *, n):
    z = jnp.concatenate([vg_ref[0], vg_ref[1]], axis=0)
    x = _dot_hi(fa_ref[...], z)
    kk = _dot_hi(fk_ref[...], k_ref[...])
    nn = 2 * n
    sc = 1.0 / (nrm_ref[0:1, :] * nn)
    xr, xi = x[:nn], x[nn:]
    kr, ki = kk[:nn] * sc, kk[nn:] * sc
    y = _dot_hi(fi_ref[...], jnp.concatenate([xr * kr - xi * ki, xr * ki + xi * kr], axis=0))
    skip = skip_ref[...]
    for bi in range(2):
        o_ref[bi] = ((y[bi * n:(bi + 1) * n] + vg_ref[bi] * skip) * g0_ref[bi]).astype(o_ref.dtype)


def _hy_conv_short(vg, g0, k, nrm, skip, b, n, d):
    nn = 2 * n
    cs, sn = _dft_cs(nn, n, nn)
    fa = jnp.asarray(_block_c(cs, sn, -1.0), F32)
    csk, snk = _dft_cs(nn, nn, nn)
    fk = jnp.asarray(np.concatenate([csk, -snk], axis=0), F32)
    csi, sni = _dft_cs(n, nn, nn)
    fi = jnp.asarray(_block_c(csi, sni, 1.0), F32)
    cb = 256
    col3 = pl.BlockSpec((b, n, cb), lambda j: (0, 0, j))
    vec = pl.BlockSpec((1, cb), lambda j: (0, j))
    out = pl.pallas_call(
        functools.partial(_hy_short_kernel, n=n), grid=(d // cb,),
        in_specs=[_const_spec(fa.shape), _const_spec(fk.shape), _const_spec(fi.shape), col3, col3,
                  pl.BlockSpec((nn, cb), lambda j: (0, j)), pl.BlockSpec((8, cb), lambda j: (0, j)), vec],
        out_specs=col3, out_shape=jax.ShapeDtypeStruct((b, n, d), BF16),
        compiler_params=_cp("parallel"), name="hy_short",
    )(fa, fk, fi, vg.reshape(b, n, d), g0.reshape(b, n, d), k, nrm, skip.reshape(1, d))
    return out.reshape(b * n, d)


def _hyena_layer(xl, xc, mods_l, mods_c, pre_g, fin, w_in, b_in, conv_w, conv_b, filt, skip,
                 w_out, b_out, b, n, c):
    d = xl.shape[-1]
    w_in_b = w_in.astype(BF16)
    w_out_b = w_out.astype(BF16)
    tm = min(ROW_TILE, n)
    g0, vg = _hy_in_call(xl, mods_l, pre_g, w_in_b, b_in, conv_w, conv_b, min(2 * ROW_TILE, n), n, BF16)
    k, nrm = _hy_filter_call(n, d, BF16, *filt)
    u_out = _hy_conv_long(vg, g0, k, nrm, skip[0], b, n, d)
    xl, (hl,) = fin(xl, u_out, mods_l, w_out_b, b_out, tm, n)

    g0c, vgc = _hy_in_call(xc, mods_c, pre_g, w_in_b, b_in, conv_w, conv_b, c, c, F32)
    kc, nrmc = _hy_filter_call(c, d, F32, *filt)
    u_out_c = _hy_conv_short(vgc, g0c, kc, nrmc, skip[0], b, c, d)
    xc, (hc,) = fin(xc, u_out_c, mods_c, w_out_b, b_out, c, c, is_ctx=True)
    return xl, hl, hc


def _s5_operators(lam_re, lam_im, log_dt, b_re, b_im, c_re, c_im, d_skip):
    t = S5_T
    g, ns = lam_re.shape[1], lam_re.shape[2]
    gc = b_re.shape[-1]
    gl = LANE // gc
    nblk = g // gl
    lam = lax.complex(lam_re, lam_im)
    dt = jnp.exp(log_dt)[..., None]
    lam_bar = jnp.exp(lam * dt)
    b_bar = ((lam_bar - 1.0) / lam)[..., None] * lax.complex(b_re, b_im)
    c_mat = lax.complex(c_re, c_im)
    pw = jnp.arange(t + 1, dtype=F32)
    lam_pw = jnp.exp((lam * dt)[None] * pw[:, None, None, None])
    hp = HIGHEST
    kern = jnp.einsum('dgcn,tdgn,dgne->dgtce', c_mat, lam_pw[:t], b_bar, precision=hp).real
    dsk = d_skip.reshape(g, gc)
    kt = jnp.swapaxes(kern, -1, -2)
    centre = kt[0][:, 0] + kt[1][:, 0] + jnp.eye(gc, dtype=F32)[None] * dsk[:, :, None]
    ks = jnp.concatenate([kt[1][:, 1:][:, ::-1], centre[:, None], kt[0][:, 1:]], axis=1)
    ks = jnp.transpose(ks.reshape(nblk, gl, 2 * t - 1, gc, gc), (0, 2, 1, 3, 4)).reshape(nblk, 2 * t - 1, LANE, gc)
    same_group = (np.arange(LANE)[:, None] // gc == np.arange(LANE)[None, :] // gc).astype(np.float32)
    d_tab = jnp.tile(ks, (1, 1, 1, gl)) * same_group

    def compact(zc, im_sign):
        z = jnp.concatenate([zc.real, im_sign * zc.imag], axis=-1)
        z = jnp.transpose(z, (2, 1, 0, 3, 4)).reshape(nblk, gl, t, 2, gc, 2 * ns)
        return jnp.transpose(z, (0, 2, 1, 3, 4, 5))

    b_t = jnp.swapaxes(b_bar, -1, -2)
    pf = lam_pw[:t][::-1][:, 0, :, None, :] * b_t[0][None]
    pb = lam_pw[:t][:, 1, :, None, :] * b_t[1][None]
    p_tab = compact(jnp.stack([pf, pb], axis=0), 1.0)
    qf = c_mat[0][None] * lam_pw[1:t + 1, 0][:, :, None, :]
    qb = c_mat[1][None] * lam_pw[1:t + 1][::-1][:, 1][:, :, None, :]
    q_tab = compact(jnp.stack([qf, qb], axis=0), -1.0)

    a = lam_pw[t]
    m_op, p_op, q_op = _s5_expand(d_tab, p_tab, q_tab)
    return m_op, p_op, q_op, a.real.reshape(2, g * ns), a.imag.reshape(2, g * ns)


def _s5_m_kernel(d_ref, o_ref, *, t):
    for s in range(t):
        for tt in range(t):
            o_ref[0, s * LANE:(s + 1) * LANE, tt * LANE:(tt + 1) * LANE] = d_ref[0, tt - s + t - 1].astype(o_ref.dtype)


def _s5_pq_kernel(c_ref, o_ref, *, transpose):
    t, gl, nd, gc, w = c_ref.shape[1:]
    ns = w // 2
    lane_grp = lax.broadcasted_iota(jnp.int32, (gc, gl * ns), 1) // ns
    for j in range(t):
        rows = []
        for g in range(gl):
            cols = []
            for dd in range(nd):
                piece = c_ref[0, j, g, dd]
                for ri in range(2):
                    tiled = jnp.concatenate([piece[:, ri * ns:(ri + 1) * ns]] * gl, axis=1)
                    cols.append(jnp.where(lane_grp == g, tiled, 0.0))
            rows.append(jnp.concatenate(cols, axis=1))
        blk = jnp.concatenate(rows, axis=0)
        if transpose:
            o_ref[0, :, j * LANE:(j + 1) * LANE] = blk.T.astype(o_ref.dtype)
        else:
            o_ref[0, j * LANE:(j + 1) * LANE, :] = blk.astype(o_ref.dtype)


def _s5_expand(d_tab, p_tab, q_tab):
    nblk, nlag = d_tab.shape[:2]
    t = (nlag + 1) // 2
    _, _, gl, nd, gc, w = p_tab.shape
    ncol = nd * gl * w
    whole = lambda shape: pl.BlockSpec((1,) + shape, lambda b: (b,) + (0,) * len(shape))
    m_op = pl.pallas_call(
        functools.partial(_s5_m_kernel, t=t), grid=(nblk,),
        in_specs=[whole((nlag, LANE, LANE))], out_specs=whole((t * LANE, t * LANE)),
        out_shape=jax.ShapeDtypeStruct((nblk, t * LANE, t * LANE), BF16),
        compiler_params=_cp("parallel"), name="s5_m_op",
    )(d_tab)
    tab_spec = whole((t, gl, nd, gc, w))
    p_op = pl.pallas_call(
        functools.partial(_s5_pq_kernel, transpose=False), grid=(nblk,),
        in_specs=[tab_spec], out_specs=whole((t * LANE, ncol)),
        out_shape=jax.ShapeDtypeStruct((nblk, t * LANE, ncol), BF16),
        compiler_params=_cp("parallel"), name="s5_p_op",
    )(p_tab)
    q_op = pl.pallas_call(
        functools.partial(_s5_pq_kernel, transpose=True), grid=(nblk,),
        in_specs=[tab_spec], out_specs=whole((ncol, t * LANE)),
        out_shape=jax.ShapeDtypeStruct((nblk, ncol, t * LANE), BF16),
        compiler_params=_cp("parallel"), name="s5_q_op",
    )(q_tab)
    return m_op, p_op, q_op


def _s5_sum_kernel(*refs, t):
    u_refs = refs[:t]
    p_ref = refs[t]
    outs = refs[t + 1:]
    u = jnp.concatenate([r[...] for r in u_refs], axis=1)
    res = _dot(u, p_ref[0])
    w = res.shape[1] // len(outs)
    for i, o in enumerate(outs):
        o[...] = res[:, i * w:(i + 1) * w]


def _s5_sum_call(h, p_op, rb):
    rows = h.shape[0]
    t = S5_T
    nblk = p_op.shape[0]
    w = p_op.shape[2] // 4
    u_specs = [pl.BlockSpec((rb, LANE), lambda gb, r, s=s: (r, s * nblk + gb)) for s in range(t)]
    out_spec = pl.BlockSpec((rb, w), lambda gb, r: (r, gb))
    return pl.pallas_call(
        functools.partial(_s5_sum_kernel, t=t), grid=(nblk, rows // rb),
        in_specs=u_specs + [pl.BlockSpec((1,) + p_op.shape[1:], lambda gb, r: (gb, 0, 0))],
        out_specs=[out_spec] * 4,
        out_shape=[jax.ShapeDtypeStruct((rows, nblk * w), F32)] * 4,
        compiler_params=_cp("parallel", "parallel"), name="s5_sum",
    )(*([h] * t), p_op)


def _s5_rec_kernel(sr_ref, si_ref, ar_ref, ai_ref, h0r_ref, h0i_ref, hr_ref, hi_ref, fr_ref, fi_ref,
                   cr, ci, *, kb, reverse):
    @pl.when(pl.program_id(1) == 0)
    def _():
        cr[...] = h0r_ref[0]
        ci[...] = h0i_ref[0]

    ar, ai = ar_ref[...], ai_ref[...]

    def body(i, carry):
        hr, hi = carry
        k = kb - 1 - i if reverse else i
        hr_ref[pl.ds(k, 1), :] = hr
        hi_ref[pl.ds(k, 1), :] = hi
        nr = ar * hr - ai * hi + sr_ref[pl.ds(k, 1), :]
        ni = ar * hi + ai * hr + si_ref[pl.ds(k, 1), :]
        return nr, ni

    hr, hi = lax.fori_loop(0, kb, body, (cr[...], ci[...]))
    cr[...] = hr
    ci[...] = hi
    fr_ref[0] = hr
    fi_ref[0] = hi


def _s5_rec_call(sr, si, ar, ai, h0r, h0i, nb_batch, reverse):
    rows, w = sr.shape
    nk = rows // nb_batch
    kb = min(64, nk)
    nb = nk // kb
    blk = (lambda bi, i: (bi * nb + nb - 1 - i, 0)) if reverse else (lambda bi, i: (bi * nb + i, 0))
    row_spec = pl.BlockSpec((kb, w), blk)
    vec = _const_spec((1, w))
    st = pl.BlockSpec((1, 1, w), lambda bi, i: (bi, 0, 0))
    return pl.pallas_call(
        functools.partial(_s5_rec_kernel, kb=kb, reverse=reverse), grid=(nb_batch, nb),
        in_specs=[row_spec, row_spec, vec, vec, st, st],
        out_specs=[row_spec, row_spec, st, st],
        out_shape=[jax.ShapeDtypeStruct((rows, w), F32)] * 2 + [jax.ShapeDtypeStruct((nb_batch, 1, w), F32)] * 2,
        scratch_shapes=[pltpu.VMEM((1, w), F32), pltpu.VMEM((1, w), F32)],
        compiler_params=_cp("parallel", "arbitrary"), name="s5_rec",
    )(sr, si, ar, ai, h0r, h0i)


def _s5_out_kernel(*refs, t):
    u_refs = refs[:t]
    h_refs = refs[t:t + 4]
    m_ref, q_ref, o_ref, tok_scr = refs[t + 4:]
    u = jnp.concatenate([r[...] for r in u_refs], axis=1)
    hcat = jnp.concatenate([r[...].astype(BF16) for r in h_refs], axis=1)
    y = _dot(u, m_ref[0]) + _dot(hcat, q_ref[0])
    g = 0.5 * y * (1.0 + lax.erf(y * (2.0 ** -0.5)))
    rb = g.shape[0]
    for j in range(t):
        tok_scr[pl.ds(j, rb, stride=t), :] = g[:, j * LANE:(j + 1) * LANE]
    o_ref[...] = tok_scr[...].astype(o_ref.dtype)


def _s5_out_call(h, states, m_op, q_op, rb):
    rows = h.shape[0]
    t = S5_T
    nblk = m_op.shape[0]
    w = q_op.shape[1] // 4
    u_specs = [pl.BlockSpec((rb, LANE), lambda gb, r, s=s: (r, s * nblk + gb)) for s in range(t)]
    return pl.pallas_call(
        functools.partial(_s5_out_kernel, t=t), grid=(nblk, rows // rb),
        in_specs=u_specs + [pl.BlockSpec((rb, w), lambda gb, r: (r, gb))] * 4
        + [pl.BlockSpec((1,) + m_op.shape[1:], lambda gb, r: (gb, 0, 0)),
           pl.BlockSpec((1,) + q_op.shape[1:], lambda gb, r: (gb, 0, 0))],
        out_specs=pl.BlockSpec((rb * t, LANE), lambda gb, r: (r, gb)),
        out_shape=jax.ShapeDtypeStruct((rows * t, nblk * LANE), BF16),
        scratch_shapes=[pltpu.VMEM((rb * t, LANE), F32)],
        compiler_params=_cp("parallel", "parallel"), name="s5_out",
    )(*([h] * t), *states, m_op, q_op)


def _s5_layer(xl, hl, hc, mods_l, fin, lam_re, lam_im, log_dt, b_re, b_im, c_re, c_im,
              d_skip, w_glu, b_glu, b, n, c):
    t = S5_T
    m_op, p_op, q_op, a_re, a_im = _s5_operators(lam_re, lam_im, log_dt, b_re, b_im, c_re, c_im, d_skip)
    w = a_re.shape[-1]

    def scan(h, init):
        sfr, sfi, sbr, sbi = _s5_sum_call(h, p_op, min(ROW_TILE, h.shape[0]))
        hfr, hfi, ffr, ffi = _s5_rec_call(sfr, sfi, a_re[0:1], a_im[0:1], init[0], init[1], b, False)
        hbr, hbi, fbr, fbi = _s5_rec_call(sbr, sbi, a_re[1:2], a_im[1:2], init[2], init[3], b, True)
        return (hfr, hfi, hbr, hbi), (ffr, ffi, fbr, fbi)

    zeros = jnp.zeros((b, 1, w), F32)
    _, ctx_final = scan(hc, (zeros,) * 4)
    states, _ = scan(hl, ctx_final)
    nk = n // t
    g_nat = _s5_out_call(hl, states, m_op, q_op, min(ROW_TILE, b * nk))
    return fin(xl, g_nat, mods_l, w_glu.astype(BF16), b_glu, min(ROW_TILE, n), n, glu=True)


def _fnet_channel_mats():
    cc, sc = _dft_cs(FNET_GC, FNET_GC, FNET_GC)
    return jnp.asarray(np.concatenate([cc, sc], axis=1) / np.sqrt(FNET_GC), BF16)


def _fn_c_kernel(l5_ref, l6_ref, xr_ref, xi_ref, o_ref, *, nh):
    r = KRON_R
    n2 = nh * r
    l5 = l5_ref[...]
    y5 = [_dot(l5, jnp.concatenate([xr_ref[0, f], xi_ref[0, f]], axis=0)).astype(BF16) for f in range(r)]
    l6 = l6_ref[...]
    hf = KRON_H
    for p in range(nh):
        outs = []
        for fs in range(0, r, hf):
            x = jnp.concatenate([y5[f][p * r:(p + 1) * r] for f in range(fs, fs + hf)]
                                + [y5[f][n2 + p * r:n2 + (p + 1) * r] for f in range(fs, fs + hf)], axis=0)
            outs.append(_dot(l6, x))
        for q in range(r):
            rows = jnp.concatenate([o[q * hf:(q + 1) * hf] for o in outs], axis=0)
            o_ref[0, p + nh * q] = rows.astype(o_ref.dtype)


def _fnet_layer(xl, am, bm, mods_l, fin, w_o, b_o, b, n, d):
    n2 = FFT_N2
    n1 = n // n2
    tm = min(ROW_TILE, n)
    a, _ = _split_radix(n1)
    l1, l2 = _slab_dft_mats(n1, a, 0, -1, neg_im=True, scale=1.0 / np.sqrt(n))
    twc, tws = _twiddle_tables(n1, n2)
    a4, b4 = am.reshape(b, n1, n2, d), bm.reshape(b, n1, n2, d)
    xr, xi = _kfa_call([(a4, None), (b4, None)], b, n1, n2, d, l1, l2, twc, tws)

    r, w = KRON_R, min(KRON_W, d)
    nh = n2 // r
    assert n1 % r == 0 and n2 % r == 0
    m5 = np.zeros((n2, n2), np.complex128)
    hf = KRON_H
    m6 = np.zeros((r * hf, hf * r), np.complex128)
    for s in range(r):
        for p in range(nh):
            for h in range(nh):
                m5[p * r + s, h * r + s] = np.exp(-2j * np.pi * (p * h / nh + p * s / n2))
        for q in range(r):
            for f in range(hf):
                m6[q * hf + f, f * r + s] = np.exp(-2j * np.pi * q * s / r)
    l5 = jnp.asarray(_cblock(m5), BF16)
    l6 = jnp.asarray(np.concatenate([m6.real, -m6.imag], axis=1), BF16)
    grp = pl.BlockSpec((1, r, n2, w), lambda bi, fh, k: (bi, fh, 0, k))
    y = pl.pallas_call(
        functools.partial(_fn_c_kernel, nh=nh), grid=(b, n1 // r, d // w),
        in_specs=[_const_spec(l5.shape), _const_spec(l6.shape), grp, grp],
        out_specs=pl.BlockSpec((1, n2, None, r, w), lambda bi, fh, k: (bi, 0, fh, 0, k)),
        out_shape=jax.ShapeDtypeStruct((b, n2, n1 // r, r, d), BF16),
        compiler_params=_cp("parallel", "parallel", "parallel"), name="fn_c",
    )(l5, l6, xr, xi)
    return fin(xl, y.reshape(b * n, d), mods_l, w_o.astype(BF16), b_o, tm, n)


def kernel(x, c, ctx, c_ctx, mod_w, mod_b, mix_pre_g, mix_post_g, ffn_pre_g, ffn_post_g, ffn_w13, ffn_w2,
           mla_w_in, mla_q_norm_g, mla_kv_norm_g, mla_w_uq, mla_w_ukv, mla_w_o,
           hy_w_in, hy_b_in, hy_conv_w, hy_conv_b, hy_f_w1, hy_f_b1, hy_f_freq, hy_f_w2, hy_f_b2, hy_f_w3,
           hy_skip, hy_w_out, hy_b_out,
           s5_lambda_re, s5_lambda_im, s5_log_dt, s5_b_re, s5_b_im, s5_c_re, s5_c_im, s5_d, s5_w_glu, s5_b_glu,
           fn_w_o, fn_b_o):
    b, n, d = x.shape
    cl = ctx.shape[1]
    depth = mod_w.shape[0]
    assert b == 2 and depth == 4, "two batches ride one complex transform; one layer per mixer"
    mods = _mods(c, c_ctx, mod_w, mod_b)
    xl = x.reshape(b * n, d)
    xc = ctx.reshape(b * cl, d)

    w13_all = ffn_w13.astype(BF16)
    w2_all = ffn_w2.astype(BF16)

    def mods_c(i):
        return jnp.broadcast_to(mods[i, 2:3], (b, 8, d))

    def finisher(i, emit_kind=None, cs=None):
        def fin(x_, y_, mods_, wm, bm, tm_, rows_per_batch, glu=False, is_ctx=False):
            emit = None
            if emit_kind is not None:
                emit = (emit_kind, mods_c(i + 1) if is_ctx else mods[i + 1, 0:2], mix_pre_g[i + 1], cs)
            return _mix_ffn_call(x_, y_, mods_, wm, bm, mix_post_g[i], ffn_pre_g[i], ffn_post_g[i],
                                 w13_all, w2_all, i, tm_, rows_per_batch, glu, emit)
        return fin

    xl, xc = _mla_layer(xl, xc, mods[0, 0:2], mods_c(0), mix_pre_g[0], finisher(0), mla_w_in[0],
                        mla_q_norm_g[0], mla_kv_norm_g[0], mla_w_uq[0], mla_w_ukv[0], mla_w_o[0], b, n, cl)
    filt = (hy_f_w1[0], hy_f_b1[0], hy_f_freq[0], hy_f_w2[0], hy_f_b2[0], hy_f_w3[0])
    xl, hl, hc = _hyena_layer(xl, xc, mods[1, 0:2], mods_c(1), mix_pre_g[1], finisher(1, "chunks"), hy_w_in[0],
                              hy_b_in[0], hy_conv_w[0], hy_conv_b[0], filt, hy_skip[0], hy_w_out[0], hy_b_out[0],
                              b, n, cl)
    xl, (am, bm) = _s5_layer(xl, hl, hc, mods[2, 0:2], finisher(2, "fnet", _fnet_channel_mats()), s5_lambda_re[0],
                             s5_lambda_im[0], s5_log_dt[0], s5_b_re[0], s5_b_im[0], s5_c_re[0], s5_c_im[0],
                             s5_d[0], s5_w_glu[0], s5_b_glu[0], b, n, cl)
    xl = _fnet_layer(xl, am, bm, mods[3, 0:2], finisher(3), fn_w_o[0], fn_b_o[0], b, n, d)
    return xl.reshape(b, n, d)
```

```python
import functools
import math

import numpy as np
import jax
import jax.numpy as jnp
from jax import lax
from jax.experimental import pallas as pl
from jax.experimental.pallas import tpu as pltpu

F32 = jnp.float32
BF16 = jnp.bfloat16
NORM_EPS = 1e-6
LANE = 128
MXU_COLS = 256
ROW_TILE = 512
VMEM_LIMIT = 56 * 1024 * 1024
HIGHEST = lax.Precision.HIGHEST

GRID_W = 64
ROPE_THETA = 10000.0
MLA_HEADS = 8
MLA_NOPE = 128
MLA_ROPE = 64
MLA_V = 128
MLA_VT = MLA_V + 16
MLA_QK = MLA_NOPE + 2 * MLA_ROPE
HYENA_BANDS = 16
HYENA_TARGET = 1e-2
HYENA_FAST = 0.3
HYENA_SLOW = 1.5
S5_T = 16
FNET_GC = 128
FFT_N2 = 128


def _cp(*sem):
    return pltpu.CompilerParams(dimension_semantics=sem, vmem_limit_bytes=VMEM_LIMIT)


def _dot(a, b):
    return jnp.dot(a, b, preferred_element_type=F32)


def _dot_hi(a, b):
    return jnp.dot(a, b, preferred_element_type=F32, precision=HIGHEST)


def _dot_x3(a, b):
    a_hi = a.astype(BF16)
    b_hi = b.astype(BF16)
    a_lo = (a - a_hi.astype(F32)).astype(BF16)
    b_lo = (b - b_hi.astype(F32)).astype(BF16)
    return _dot(a_hi, b_hi) + (_dot(a_hi, b_lo) + _dot(a_lo, b_hi))


def _rms(x, g):
    ms = jnp.mean(x * x, axis=-1, keepdims=True)
    return x * lax.rsqrt(ms + NORM_EPS) * g


def _normmod(x, g, shift, scale):
    return _rms(x, g) * (1.0 + scale) + shift


def _const_spec(shape):
    nd = len(shape)
    return pl.BlockSpec(shape, lambda *_: (0,) * nd)


def _mods_kernel(st_ref, w_ref, b_ref, o_ref):
    st = st_ref[...]
    st = st * jax.nn.sigmoid(st)
    w = w_ref[0]
    rows = [jnp.sum(st[:, r:r + 1] * w, axis=0, keepdims=True) for r in range(3)]
    rows.append(jnp.zeros((5, w.shape[1]), F32))
    o_ref[0] = jnp.concatenate(rows, axis=0) + b_ref[0]


def _mods(c, c_ctx, mod_w, mod_b):
    depth, d, n6 = mod_w.shape
    st = jnp.zeros((d, 8), F32).at[:, 0:2].set(c.T).at[:, 2].set(c_ctx)
    tn = 1024
    out = pl.pallas_call(
        _mods_kernel,
        grid=(depth, n6 // tn),
        in_specs=[_const_spec((d, 8)),
                  pl.BlockSpec((1, d, tn), lambda i, j: (i, 0, j)),
                  pl.BlockSpec((1, 1, tn), lambda i, j: (i, 0, j))],
        out_specs=pl.BlockSpec((1, 8, tn), lambda i, j: (i, 0, j)),
        out_shape=jax.ShapeDtypeStruct((depth, 8, n6), F32),
        compiler_params=_cp("parallel", "parallel"),
        name="mods",
    )(st, mod_w, mod_b.reshape(depth, 1, n6))
    m = out[:, :3].reshape(depth, 3, n6 // d, d)
    return jnp.pad(m, ((0, 0), (0, 0), (0, 8 - n6 // d), (0, 0)))


def _row_specs(tm, d, tpb):
    x_spec = pl.BlockSpec((tm, d), lambda i: (i, 0))
    mod_spec = pl.BlockSpec((1, 8, d), lambda i: (i // tpb, 0, 0))
    return x_spec, mod_spec


def _channel_dft(h, cs):
    gc = FNET_GC
    ab = [_dot(h[:, k * gc:(k + 1) * gc], cs) for k in range(h.shape[1] // gc)]
    return (jnp.concatenate([z[:, :gc] for z in ab], axis=1), jnp.concatenate([z[:, gc:] for z in ab], axis=1))


def _mix_ffn_kernel(x_ref, y_ref, mod_ref, wm_ref, bm_ref, gm_ref, pre_ref, post_ref, w13_ref, w2_ref, *rest,
                    f, cuts, glu, emit):
    if emit == "chunks":
        nmod_ref, ng_ref, o_ref, h_ref, tok_scr = rest
    elif emit == "fnet":
        nmod_ref, ng_ref, cs_ref, o_ref, a_ref, b_ref = rest
    else:
        (o_ref,) = rest
    z = _dot(y_ref[...].astype(BF16), wm_ref[...]) + bm_ref[...]
    if glu:
        d = o_ref.shape[-1]
        z = z[:, :d] * jax.nn.sigmoid(z[:, d:])
    x = x_ref[...] + mod_ref[0, 2:3, :] * _rms(z, gm_ref[...])
    h = _normmod(x, pre_ref[...], mod_ref[0, 3:4, :], mod_ref[0, 4:5, :]).astype(BF16)
    acc = None
    for lo, hi in zip(cuts[:-1], cuts[1:]):
        a = _dot(h, w13_ref[:, lo:hi])
        b = _dot(h, w13_ref[:, f + lo:f + hi])
        gact = (a * jax.nn.sigmoid(a) * b).astype(BF16)
        part = _dot(gact, w2_ref[lo:hi, :])
        acc = part if acc is None else acc + part
    xo = x + mod_ref[0, 5:6, :] * _rms(acc, post_ref[...])
    o_ref[...] = xo
    if emit:
        hn = _normmod(xo, ng_ref[...], nmod_ref[0, 0:1, :], nmod_ref[0, 1:2, :])
        if emit == "chunks":
            tm, d = hn.shape
            for lt in range(d // LANE):
                tok_scr[lt * tm:(lt + 1) * tm, :] = hn[:, lt * LANE:(lt + 1) * LANE]
            for j in range(S5_T):
                for lt in range(d // LANE):
                    rows = tok_scr[pl.ds(lt * tm + j, tm // S5_T, stride=S5_T), :]
                    h_ref[:, j * d + lt * LANE:j * d + (lt + 1) * LANE] = rows.astype(h_ref.dtype)
        else:
            a, b = _channel_dft(hn.astype(BF16), cs_ref[...])
            a_ref[...] = a.astype(a_ref.dtype)
            b_ref[...] = b.astype(b_ref.dtype)


def _mix_ffn_call(x, y, mods, wm, bm, mix_post_g, ffn_pre_g, ffn_post_g, w13, w2, layer, tm, rows_per_batch,
                  glu=False, emit=None):
    m, d = x.shape
    k, nm = wm.shape
    f = w2.shape[1]
    cuts = tuple(range(0, f, 6 * MXU_COLS)) + (f,)
    x_spec, mod_spec = _row_specs(tm, d, rows_per_batch // tm)
    once = lambda shape: pl.BlockSpec(shape, lambda i: (0, 0), pipeline_mode=pl.Buffered(1))
    of_layer = lambda shape: pl.BlockSpec((None,) + shape, lambda i: (layer, 0, 0), pipeline_mode=pl.Buffered(1))
    in_specs = [x_spec, pl.BlockSpec((tm, k), lambda i: (i, 0)), mod_spec,
                once((k, nm)), _const_spec((1, nm)), _const_spec((1, d)), _const_spec((1, d)),
                _const_spec((1, d)), of_layer((d, 2 * f)), of_layer((f, d))]
    args = [x, y, mods, wm, bm.reshape(1, nm), mix_post_g.reshape(1, d), ffn_pre_g.reshape(1, d),
            ffn_post_g.reshape(1, d), w13, w2]
    out_specs, out_shape = [x_spec], [jax.ShapeDtypeStruct((m, d), F32)]
    kind = None
    scratch = []
    if emit is not None:
        kind, next_mods, next_g, cs = emit
        in_specs += [mod_spec, _const_spec((1, d))]
        args += [next_mods, next_g.reshape(1, d)]
        if kind == "fnet":
            in_specs.append(_const_spec(cs.shape))
            args.append(cs)
            out_specs += [x_spec] * 2
            out_shape += [jax.ShapeDtypeStruct((m, d), BF16)] * 2
        else:
            out_specs.append(pl.BlockSpec((tm // S5_T, S5_T * d), lambda i: (i, 0)))
            out_shape.append(jax.ShapeDtypeStruct((m // S5_T, S5_T * d), BF16))
            scratch.append(pltpu.VMEM((tm * (d // LANE), LANE), F32))
    outs = pl.pallas_call(
        functools.partial(_mix_ffn_kernel, f=f, cuts=cuts, glu=glu, emit=kind), grid=(m // tm,),
        in_specs=in_specs, out_specs=out_specs, out_shape=out_shape, scratch_shapes=scratch,
        compiler_params=_cp("parallel"), name="mix_ffn",
    )(*args)
    return outs[0] if emit is None else (outs[0], tuple(outs[1:]))


def _mla_proj_kernel(x_ref, mod_ref, g_ref, w_ref, qg_ref, kvg_ref, wq_ref, wkv_ref, cp_ref, sp_ref,
                     q_ref, k_ref, vt_ref, *, ql, kvl, qscale, positional):
    h = _normmod(x_ref[...], g_ref[...], mod_ref[0, 0:1, :], mod_ref[0, 1:2, :])
    z = _dot(h.astype(BF16), w_ref[...])
    qn = _rms(z[:, :ql], qg_ref[...]).astype(BF16)
    cn = _rms(z[:, ql:ql + kvl], kvg_ref[...]).astype(BF16)
    tk = z.shape[0]
    low = lax.broadcasted_iota(jnp.int32, (tk, 2 * MLA_ROPE), 1) < MLA_ROPE
    if positional:
        cpf = jnp.concatenate([cp_ref[...]] * 2, axis=1)
        spf = jnp.concatenate([sp_ref[...]] * 2, axis=1)

    def rope_slots(pair, keep_raw):
        swapped = pltpu.roll(pair, MLA_ROPE, 1)
        raw = jnp.where(low, 0.0, swapped) if keep_raw else None
        if not positional:
            return raw
        rot = jnp.where(low, pair * cpf + swapped * spf, 0.0)
        return rot + raw if keep_raw else rot

    kr = rope_slots(z[:, ql + kvl:], keep_raw=not positional).astype(BF16)

    zkv = _dot(cn, wkv_ref[...])
    ones_blk = (lax.broadcasted_iota(jnp.int32, (MLA_VT - MLA_V, tk), 0) == 0).astype(BF16)
    for hd in range(MLA_HEADS):
        base = hd * (MLA_NOPE + MLA_V)
        k_ref[0, hd, 0, :, 0:MLA_NOPE] = zkv[:, base:base + MLA_NOPE].astype(BF16)
        k_ref[0, hd, 0, :, MLA_NOPE:MLA_QK] = kr
        vt_ref[0, hd, 0, 0:MLA_V, :] = zkv[:, base + MLA_NOPE:base + MLA_NOPE + MLA_V].T.astype(BF16)
        vt_ref[0, hd, 0, MLA_V:MLA_VT, :] = ones_blk

    zq = _dot(qn, wq_ref[...])
    for hd in range(MLA_HEADS):
        base = hd * MLA_QK
        rp = rope_slots(zq[:, base + MLA_NOPE:base + MLA_QK], keep_raw=True)
        qcat = jnp.concatenate([zq[:, base:base + MLA_NOPE], rp], axis=1) * qscale
        q_ref[0, hd] = qcat.T.astype(BF16)


def _flash_kernel(q_ref, kc_ref, vc_ref, *rest, n_lat):
    if n_lat:
        kl_ref, vl_ref, o_ref, s_scr, acc_scr = rest
    else:
        o_ref, acc_scr = rest
    qt = q_ref[0, 0]

    def qk(k, slot):
        s = _dot(k, qt)
        s_scr[slot] = s
        return jnp.max(s, axis=0, keepdims=True)

    def sm_pv(slot, vt, m, mx):
        m_new = jnp.maximum(m, mx)
        alpha = jnp.exp2(m - m_new)
        p = jnp.exp2(s_scr[slot] - m_new).astype(BF16)
        acc_scr[...] = alpha * acc_scr[...] + _dot(vt, p)
        return m_new

    sc = _dot(kc_ref[0, 0, 0], qt)
    if n_lat:
        mx = qk(kl_ref[0, 0, 0], 0)
    m = jnp.max(sc, axis=0, keepdims=True)
    acc_scr[...] = _dot(vc_ref[0, 0, 0], jnp.exp2(sc - m).astype(BF16))
    if n_lat:

        per = next(p for p in (16, 8, 4, 2) if n_lat % p == 0)

        def body(i, carry):
            m, mx_cur = carry
            c = per * i
            for u in range(per):
                mx_next = qk(kl_ref[0, 0, jnp.minimum(c + u + 1, n_lat - 1)], (u + 1) % 2)
                m = sm_pv(u % 2, vl_ref[0, 0, c + u], m, mx_cur)
                mx_cur = mx_next
            return m, mx_cur

        lax.fori_loop(0, n_lat // per, body, (m, mx))
    acc = acc_scr[...]
    o_ref[0] = (acc[0:MLA_V] / acc[MLA_V:MLA_V + 1]).T.astype(o_ref.dtype)


def _rope_tables(n_lat):
    rows = n_lat // GRID_W
    row = jnp.repeat(jnp.arange(rows, dtype=F32), GRID_W)
    col = jnp.tile(jnp.arange(GRID_W, dtype=F32), rows)
    axis_dim = MLA_ROPE // 2
    inv_freq = 1.0 / (ROPE_THETA ** (jnp.arange(0, axis_dim, 2, dtype=F32) / axis_dim))
    ang_r = row[:, None] * inv_freq
    ang_c = col[:, None] * inv_freq
    cr, sr, cc, sc = jnp.cos(ang_r), jnp.sin(ang_r), jnp.cos(ang_c), jnp.sin(ang_c)
    cp = jnp.concatenate([cr, cr, cc, cc], axis=-1)
    sp = jnp.concatenate([-sr, sr, -sc, sc], axis=-1)
    return cp, sp


_ROPE_SWAP = np.concatenate([np.arange(16, 32), np.arange(0, 16), np.arange(48, 64), np.arange(32, 48)])


def _mla_side(x, mods, pre_g, w_in_ext, q_g, kv_g, w_uq_ext, w_ukv, tabs, b, n, tk, positional):
    m, d = x.shape
    ql, kvl = q_g.shape[-1], kv_g.shape[-1]
    nc = n // tk
    x_spec, mod_spec = _row_specs(tk, d, nc)
    tab_spec = pl.BlockSpec((tk, MLA_ROPE), lambda i: (i % nc, 0))
    qscale = (MLA_NOPE + MLA_ROPE) ** -0.5 * math.log2(math.e)
    return pl.pallas_call(
        functools.partial(_mla_proj_kernel, ql=ql, kvl=kvl, qscale=qscale, positional=positional),
        grid=(m // tk,),
        in_specs=[x_spec, mod_spec, _const_spec((1, d)), _const_spec(w_in_ext.shape),
                  _const_spec((1, ql)), _const_spec((1, kvl)), _const_spec(w_uq_ext.shape),
                  _const_spec(w_ukv.shape), tab_spec, tab_spec],
        out_specs=[pl.BlockSpec((1, MLA_HEADS, MLA_QK, tk), lambda i: (i // nc, 0, 0, i % nc)),
                   pl.BlockSpec((1, MLA_HEADS, 1, tk, MLA_QK), lambda i: (i // nc, 0, i % nc, 0, 0)),
                   pl.BlockSpec((1, MLA_HEADS, 1, MLA_VT, tk), lambda i: (i // nc, 0, i % nc, 0, 0))],
        out_shape=[jax.ShapeDtypeStruct((b, MLA_HEADS, MLA_QK, n), BF16),
                   jax.ShapeDtypeStruct((b, MLA_HEADS, nc, tk, MLA_QK), BF16),
                   jax.ShapeDtypeStruct((b, MLA_HEADS, nc, MLA_VT, tk), BF16)],
        compiler_params=_cp("parallel"), name="mla_proj",
    )(x, mods, pre_g.reshape(1, d), w_in_ext, q_g.reshape(1, ql), kv_g.reshape(1, kvl), w_uq_ext, w_ukv, *tabs)


def _flash_call(qt, kc, vtc, kl, vtl, tq):
    b, hh, _, n = qt.shape
    c = kc.shape[-2]
    n_lat = 0 if kl is None else kl.shape[2]
    in_specs = [pl.BlockSpec((1, 1, MLA_QK, tq), lambda bi, h, i: (bi, h, 0, i)),
                pl.BlockSpec((1, 1, 1, c, MLA_QK), lambda bi, h, i: (bi, h, 0, 0, 0)),
                pl.BlockSpec((1, 1, 1, MLA_VT, c), lambda bi, h, i: (bi, h, 0, 0, 0))]
    args = [qt, kc, vtc]
    scratch = [pltpu.VMEM((MLA_VT, tq), F32)]
    if n_lat:
        assert n_lat % 2 == 0, "latent key chunks are consumed in pairs"
        tk = kl.shape[-2]
        in_specs += [pl.BlockSpec((1, 1, n_lat, tk, MLA_QK), lambda bi, h, i: (bi, h, 0, 0, 0)),
                     pl.BlockSpec((1, 1, n_lat, MLA_VT, tk), lambda bi, h, i: (bi, h, 0, 0, 0))]
        args += [kl, vtl]
        scratch = [pltpu.VMEM((2, tk, tq), F32)] + scratch
    return pl.pallas_call(
        functools.partial(_flash_kernel, n_lat=n_lat), grid=(b, hh, n // tq),
        in_specs=in_specs,
        out_specs=pl.BlockSpec((1, tq, MLA_V), lambda bi, h, i: (bi, i, h)),
        out_shape=jax.ShapeDtypeStruct((b, n, hh * MLA_V), BF16),
        scratch_shapes=scratch,
        compiler_params=_cp("parallel", "parallel", "arbitrary"), name="flash",
    )(*args)


def _mla_layer(xl, xc, mods_l, mods_c, pre_g, fin, w_in, q_g, kv_g, w_uq, w_ukv, w_o, b, n, c):
    d = xl.shape[-1]
    ql, kvl = q_g.shape[-1], kv_g.shape[-1]
    hh = MLA_HEADS
    rope_cols = w_in[:, ql + kvl:]
    w_in_ext = jnp.concatenate([w_in, rope_cols[:, _ROPE_SWAP]], axis=1).astype(BF16)
    wq = w_uq.reshape(ql, hh, MLA_NOPE + MLA_ROPE)
    w_uq_ext = jnp.concatenate([wq, wq[:, :, MLA_NOPE:][:, :, _ROPE_SWAP]], axis=-1)
    w_uq_ext = w_uq_ext.reshape(ql, hh * MLA_QK).astype(BF16)
    w_ukv_b = w_ukv.astype(BF16)
    w_o_b = w_o.astype(BF16)

    tabs = _rope_tables(n)

    tm_l = min(ROW_TILE, n)
    tk_l = min(ROW_TILE, n // 2)
    ql_, kl, vtl = _mla_side(xl, mods_l, pre_g, w_in_ext, q_g, kv_g, w_uq_ext, w_ukv_b, tabs, b, n, tk_l, True)
    qc_, kc, vtc = _mla_side(xc, mods_c, pre_g, w_in_ext, q_g, kv_g, w_uq_ext, w_ukv_b, tabs, b, c, c, False)
    o_lat = _flash_call(ql_, kc, vtc, kl, vtl, min(2 * ROW_TILE, n)).reshape(b * n, hh * MLA_V)
    o_ctx = _flash_call(qc_, kc, vtc, None, None, c).reshape(b * c, hh * MLA_V)
    zb = jnp.zeros((d,), F32)
    xl = fin(xl, o_lat, mods_l, w_o_b, zb, tm_l, n)
    xc = fin(xc, o_ctx, mods_c, w_o_b, zb, c, c)
    return xl, xc


def _hy_in_kernel(x_ref, xp_ref, xn_ref, mod_ref, g_ref, w_ref, b_ref, cw_ref, cb_ref,
                  g0_ref, vg_ref, *, tpb):
    i = pl.program_id(0)
    g = g_ref[...]
    shift, scale = mod_ref[0, 0:1, :], mod_ref[0, 1:2, :]
    xcat = jnp.concatenate([xp_ref[...], x_ref[...], xn_ref[...]], axis=0)
    hcat = _normmod(xcat, g, shift, scale).astype(BF16)
    tm = x_ref.shape[0]
    d = g0_ref.shape[-1]
    first = (i % tpb) == 0
    last = (i % tpb) == tpb - 1
    ridx = lax.broadcasted_iota(jnp.int32, (tm, 1), 0)

    def conv_part(c):
        cols = slice(c * d, (c + 1) * d)
        ucat = _dot(hcat, w_ref[:, cols]) + b_ref[:, cols]
        u = ucat[8:tm + 8]
        prev_row = jnp.where(first, 0.0, ucat[7:8, :])
        next_row = jnp.where(last, 0.0, ucat[tm + 8:tm + 9, :])
        dn = jnp.where(ridx == 0, prev_row, pltpu.roll(u, 1, 0))
        upw = jnp.where(ridx == tm - 1, next_row, pltpu.roll(u, tm - 1, 0))
        return cb_ref[:, cols] + dn * cw_ref[0:1, cols] + u * cw_ref[1:2, cols] + upw * cw_ref[2:3, cols]

    g0_ref[...] = conv_part(0).astype(g0_ref.dtype)
    vg_ref[...] = (conv_part(2) * conv_part(1)).astype(vg_ref.dtype)


def _hy_in_call(x, mods, pre_g, w_in, b_in, conv_w, conv_b, tm, n, out_dtype):
    m, d = x.shape
    p = w_in.shape[1]
    tpb = n // tm
    x_spec, mod_spec = _row_specs(tm, d, tpb)
    r8 = tm // 8
    nb8 = m // 8
    prev_spec = pl.BlockSpec((8, d), lambda i: (jnp.maximum(i * r8 - 1, 0), 0))
    next_spec = pl.BlockSpec((8, d), lambda i: (jnp.minimum((i + 1) * r8, nb8 - 1), 0))
    cw = jnp.pad(conv_w, ((0, 8 - conv_w.shape[0]), (0, 0)))
    return pl.pallas_call(
        functools.partial(_hy_in_kernel, tpb=tpb), grid=(m // tm,),
        in_specs=[x_spec, prev_spec, next_spec, mod_spec, _const_spec((1, d)), _const_spec((d, p)),
                  _const_spec((1, p)), _const_spec((8, p)), _const_spec((1, p))],
        out_specs=[x_spec, x_spec],
        out_shape=[jax.ShapeDtypeStruct((m, d), out_dtype), jax.ShapeDtypeStruct((m, d), out_dtype)],
        compiler_params=_cp("parallel"), name="hy_in",
    )(x, x, x, mods, pre_g.reshape(1, d), w_in, b_in.reshape(1, p), cw, conv_b.reshape(1, p))


_PI_SPLIT = (3.140625, 9.67502593994140625e-4, 1.509957990978376432e-7)
_SIN_TAYLOR = (-1.0 / 6, 1.0 / 120, -1.0 / 5040, 1.0 / 362880, -1.0 / 39916800)


def _sin(x):
    kf = jnp.round(x * (1.0 / math.pi))
    r = ((x - kf * _PI_SPLIT[0]) - kf * _PI_SPLIT[1]) - kf * _PI_SPLIT[2]
    r2 = r * r
    p = _SIN_TAYLOR[4]
    for c in _SIN_TAYLOR[3::-1]:
        p = p * r2 + c
    s = r + r * r2 * p
    odd = (kf.astype(jnp.int32) & 1) == 1
    return jnp.where(odd, -s, s)


def _hy_filter_kernel(bands_ref, w1_ref, b1_ref, fq_ref, w2_ref, b2_ref, w3_ref, dl_ref,
                      k_ref, nrm_ref, *, n, tr):
    i = pl.program_id(0)
    bwd = i >= n // tr
    row = lax.broadcasted_iota(jnp.int32, (tr, LANE), 0) + i * tr
    j = jnp.where(bwd, 2 * n - row, row).astype(F32)
    lane = lax.broadcasted_iota(jnp.int32, (tr, LANE), 1)
    t = j * (1.0 / (n - 1))
    arg = (2.0 * math.pi / n) * j * bands_ref[0:1, :] + bands_ref[1:2, :]
    z = jnp.where(lane == 0, t, jnp.where(lane <= 2 * HYENA_BANDS, _sin(arg), 0.0))
    fq = fq_ref[...]
    a = _sin(fq * (_dot_x3(z, w1_ref[...]) + b1_ref[...]))
    for k in range(w2_ref.shape[0]):
        a = _sin(fq * (_dot_x3(a, w2_ref[k]) + b2_ref[k]))
    h = _dot_x3(a, w3_ref[jnp.where(bwd, 1, 0)])
    decay = jnp.exp(-t[:, 0:1] * dl_ref[...])
    k = h * decay
    k = jnp.where(row[:, 0:1] == n, 0.0, k)
    k_ref[...] = k.astype(k_ref.dtype)
    part = jnp.sum(jnp.abs(k), axis=0, keepdims=True)

    @pl.when(i == 0)
    def _():
        nrm_ref[...] = jnp.zeros_like(nrm_ref)

    nrm_ref[...] += jnp.broadcast_to(part, nrm_ref.shape)


def _hy_filter_call(n, d, out_dtype, f_w1, f_b1, f_freq, f_w2, f_b2, f_w3):
    fw = f_w1.shape[1]
    tr = min(ROW_TILE, n)
    bands_np = np.zeros((8, LANE), np.float32)
    bands_np[0, 1:1 + HYENA_BANDS] = np.linspace(1e-4, HYENA_BANDS - 1, HYENA_BANDS, dtype=np.float32)
    bands_np[0, 1 + HYENA_BANDS:1 + 2 * HYENA_BANDS] = bands_np[0, 1:1 + HYENA_BANDS]
    bands_np[1, 1:1 + HYENA_BANDS] = 0.5 * np.pi
    bands_np[1, 1 + HYENA_BANDS:1 + 2 * HYENA_BANDS] = np.pi
    w1p = jnp.zeros((LANE, fw), F32).at[:f_w1.shape[0]].set(f_w1)
    deltas = jnp.abs(jnp.linspace(math.log(HYENA_TARGET) / HYENA_SLOW, math.log(HYENA_TARGET) / HYENA_FAST,
                                  d, dtype=F32)).reshape(1, d)
    row = pl.BlockSpec((tr, d), lambda i: (i, 0))
    return pl.pallas_call(
        functools.partial(_hy_filter_kernel, n=n, tr=tr), grid=(2 * n // tr,),
        in_specs=[_const_spec((8, LANE)), _const_spec((LANE, fw)), _const_spec((1, fw)), _const_spec((1, fw)),
                  _const_spec(f_w2.shape), _const_spec((f_w2.shape[0], 1, fw)), _const_spec((2, fw, d)),
                  _const_spec((1, d))],
        out_specs=[row, _const_spec((8, d))],
        out_shape=[jax.ShapeDtypeStruct((2 * n, d), out_dtype), jax.ShapeDtypeStruct((8, d), F32)],
        compiler_params=_cp("arbitrary"), name="hy_filter",
    )(jnp.asarray(bands_np), w1p, f_b1.reshape(1, fw), f_freq.reshape(1, fw), f_w2,
      f_b2.reshape(f_w2.shape[0], 1, fw), jnp.transpose(f_w3.reshape(fw, 2, d), (1, 0, 2)), deltas)


def _dft_cs(nf, nt, period):
    ft = (np.arange(nf)[:, None] * np.arange(nt)[None, :]) % period
    ang = 2.0 * np.pi * ft / period
    return np.cos(ang), np.sin(ang)


def _twiddle_tables(n1, n2):
    nn = n1 * n2
    f1 = jnp.arange(n1, dtype=jnp.int32)
    t2 = jnp.arange(n2, dtype=jnp.int32)
    idx = (f1[:, None] * t2[None, :]) % nn
    ang = idx.astype(F32) * (2.0 * math.pi / nn)
    shape = idx.shape + (LANE,)
    return (jnp.broadcast_to(jnp.cos(ang)[..., None], shape),
            jnp.broadcast_to(jnp.sin(ang)[..., None], shape))


KRON_R = 16
KRON_H = 8
KRON_W = 256


def _cblock(mc):
    return np.block([[mc.real, -mc.imag], [mc.imag, mc.real]])


def _split_radix(n1):
    b = 16 if (n1 % 16 == 0 and n1 >= 64) else 4
    assert n1 % b == 0 and (n1 // b) % 2 == 0
    return n1 // b, b


def _slab_dft_mats(n1, a_in, a_out, sign, neg_im=False, real_in=False, scale=1.0):
    a, b = _split_radix(n1)
    r = KRON_H
    eye = np.eye(r)
    w = lambda num, den: np.exp(sign * 2j * np.pi * num / den)
    ua = np.arange(a)[:, None] * np.arange(a)[None, :]
    vb = np.arange(b)[:, None] * np.arange(b)[None, :]
    if sign < 0:
        m1 = np.kron(w(ua[:, :a_in], a) * scale, eye)
        l1 = _cblock(m1)
        if neg_im:
            l1[:, a_in * r:] *= -1.0
        if real_in:
            l1 = l1[:, :a_in * r]
        l2 = np.stack([_cblock(np.kron(w(vb, b) * w(u * np.arange(b)[None, :], n1), eye)) for u in range(a)])
    else:
        l1 = np.stack([_cblock(np.kron(w(vb, b) * w(u * np.arange(b)[:, None], n1), eye)) for u in range(a)])
        l2 = _cblock(np.kron(w(ua[:a_out, :], a) * scale, eye))
    return jnp.asarray(l1, BF16), jnp.asarray(l2, BF16)


def _kfa_kernel(*refs, nparts, a_in, a, b):
    l1_ref, l2_ref = refs[0], refs[1]
    parts = refs[2:2 + nparts]
    twc_ref, tws_ref, or_ref, oi_ref = refs[2 + nparts:]
    r = KRON_H
    halves = range(KRON_R // r)
    reps = or_ref.shape[-1] // LANE
    l1 = l1_ref[...]
    y1 = []
    for h in halves:
        rows = slice(h * r, (h + 1) * r)
        y1.append([_dot(l1, jnp.concatenate([p[0, aa * b + bb].astype(F32)[rows] for p in parts
                                             for aa in range(a_in)], axis=0).astype(BF16))
                   for bb in range(b)])
    for u in range(a):
        z = []
        for h in halves:
            x = jnp.concatenate([y1[h][bb][u * r:(u + 1) * r] for bb in range(b)]
                                + [y1[h][bb][(a + u) * r:(a + u + 1) * r] for bb in range(b)], axis=0)
            z.append(_dot(l2_ref[u], x.astype(BF16)))
        for v in range(b):
            f1 = u + a * v
            zr = jnp.concatenate([zh[v * r:(v + 1) * r] for zh in z], axis=0)
            zi = jnp.concatenate([zh[(b + v) * r:(b + v + 1) * r] for zh in z], axis=0)
            c = jnp.concatenate([twc_ref[f1]] * reps, axis=1)
            sn = jnp.concatenate([tws_ref[f1]] * reps, axis=1)
            or_ref[0, f1] = (zr * c + zi * sn).astype(or_ref.dtype)
            oi_ref[0, f1] = (zi * c - zr * sn).astype(oi_ref.dtype)


def _kfa_call(parts, nb_out, n1, n2, d, l1, l2, twc, tws):
    a, b = _split_radix(n1)
    a_in = parts[0][0].shape[1] // b
    r, w = KRON_R, min(KRON_W, d)
    in_specs = [_const_spec(l1.shape), pl.BlockSpec(l2.shape, lambda bi, j, k: (0, 0, 0), pipeline_mode=pl.Buffered(1))]
    args = [l1, l2]
    for arr, bi_fixed in parts:
        t1_in = arr.shape[1]
        if bi_fixed is None:
            in_specs.append(pl.BlockSpec((1, t1_in, r, w), lambda bi, j, k: (bi, 0, j, k)))
        else:
            in_specs.append(pl.BlockSpec((1, t1_in, r, w), lambda bi, j, k, f=bi_fixed: (f, 0, j, k)))
        args.append(arr)
    tw_spec = pl.BlockSpec((n1, r, LANE), lambda bi, j, k: (0, j, 0))
    out_spec = pl.BlockSpec((1, n1, r, w), lambda bi, j, k: (bi, 0, j, k))
    out = jax.ShapeDtypeStruct((nb_out, n1, n2, d), BF16)
    return pl.pallas_call(
        functools.partial(_kfa_kernel, nparts=len(parts), a_in=a_in, a=a, b=b),
        grid=(nb_out, n2 // r, d // w),
        in_specs=in_specs + [tw_spec, tw_spec], out_specs=[out_spec, out_spec], out_shape=[out, out],
        compiler_params=_cp("parallel", "parallel", "parallel"), name="kfa",
    )(*args, twc, tws)


def _kfc_kernel(l3_ref, l4_ref, gr_ref, gi_ref, vg_ref, g0_ref, skip_ref, o_ref, *, a, b, a_out):
    r = KRON_H
    halves = range(KRON_R // r)
    y3 = [[] for _ in halves]
    for u in range(a):
        gr = [gr_ref[0, u + a * v].astype(F32) for v in range(b)]
        gi = [gi_ref[0, u + a * v].astype(F32) for v in range(b)]
        for h in halves:
            rows = slice(h * r, (h + 1) * r)
            x = jnp.concatenate([g[rows] for g in gr] + [g[rows] for g in gi], axis=0).astype(BF16)
            y3[h].append(_dot(l3_ref[u], x))
    l4 = l4_ref[...]
    skip = skip_ref[...]
    for bb in range(b):
        y = []
        for h in halves:
            x = jnp.concatenate([y3[h][u][bb * r:(bb + 1) * r] for u in range(a)]
                                + [y3[h][u][(b + bb) * r:(b + bb + 1) * r] for u in range(a)], axis=0)
            y.append(_dot(l4, x.astype(BF16)))
        for sg in range(2):
            for aa in range(a_out):
                t1 = aa * b + bb
                yb = jnp.concatenate([yh[(sg * a_out + aa) * r:(sg * a_out + aa + 1) * r] for yh in y], axis=0)
                o_ref[sg, t1] = ((yb + vg_ref[sg, t1] * skip) * g0_ref[sg, t1]).astype(o_ref.dtype)


def _kfc_call(g1r, g1i, vg4, g04, skip, n1, n2, d, l3, l4):
    a, b = _split_radix(n1)
    nb, t1_out = vg4.shape[:2]
    r, w = KRON_R, min(KRON_W, d)
    slab = pl.BlockSpec((1, n1, r, w), lambda j, k: (0, 0, j, k))
    nat = pl.BlockSpec((nb, t1_out, r, w), lambda j, k: (0, 0, j, k))
    return pl.pallas_call(
        functools.partial(_kfc_kernel, a=a, b=b, a_out=t1_out // b), grid=(n2 // r, d // w),
        in_specs=[pl.BlockSpec(l3.shape, lambda j, k: (0, 0, 0), pipeline_mode=pl.Buffered(1)),
                  _const_spec(l4.shape), slab, slab, nat, nat, pl.BlockSpec((1, w), lambda j, k: (0, k))],
        out_specs=nat, out_shape=jax.ShapeDtypeStruct(vg4.shape, BF16),
        compiler_params=_cp("parallel", "parallel"), name="kfc",
    )(l3, l4, g1r, g1i, vg4, g04, skip.reshape(1, d))


def _hy_b_kernel(fm_ref, fmc_ref, xr_ref, xi_ref, fr_ref, fi_ref, sc_ref, twc_ref, tws_ref,
                 or_ref, oi_ref, *, nf, d):
    fm = fm_ref[...]
    fmc = fmc_ref[...]
    n2 = xr_ref.shape[1]
    sc = sc_ref[...]
    for s in range(nf):
        x = _dot(fm, jnp.concatenate([xr_ref[s], xi_ref[s]], axis=0))
        kk = _dot(fm, jnp.concatenate([fr_ref[s], fi_ref[s]], axis=0))
        xr, xi = x[:n2], x[n2:]
        kr, ki = kk[:n2] * sc, kk[n2:] * sc
        yr = (xr * kr - xi * ki).astype(BF16)
        yi = (xr * ki + xi * kr).astype(BF16)
        g = _dot(fmc, jnp.concatenate([yr, yi], axis=0))
        gr, gi = g[:n2], g[n2:]
        c = jnp.concatenate([twc_ref[s]] * (d // LANE), axis=1)
        sn = jnp.concatenate([tws_ref[s]] * (d // LANE), axis=1)
        or_ref[s] = (gr * c - gi * sn).astype(or_ref.dtype)
        oi_ref[s] = (gi * c + gr * sn).astype(oi_ref.dtype)


def _block_c(cs, sn, sign):
    return np.block([[cs, -sign * sn], [sign * sn, cs]])


def _hy_conv_long(vg, g0, k, nrm, skip, b, n, d):
    n2 = FFT_N2
    nn = 2 * n
    n1 = nn // n2
    rows = n // n2
    a, _ = _split_radix(n1)
    la1, la2 = _slab_dft_mats(n1, a // 2, 0, -1)
    lk1, lk2 = _slab_dft_mats(n1, a, 0, -1, real_in=True)
    lc3, lc4 = _slab_dft_mats(n1, 0, a // 2, +1)
    cs2, sn2 = _dft_cs(n2, n2, n2)
    fm_b = jnp.asarray(_block_c(cs2, sn2, -1.0), BF16)
    fm_bc = jnp.asarray(_block_c(cs2, sn2, 1.0), BF16)
    twc, tws = _twiddle_tables(n1, n2)

    kfr, kfi = _kfa_call([(k.reshape(1, n1, n2, d), 0)], 1, n1, n2, d, lk1, lk2, twc, tws)
    nf = next(k for k in (8, 4, 1) if n1 % k == 0)
    scale = (1.0 / (nrm[0:1, :] * nn))
    slab = pl.BlockSpec((nf, n2, d), lambda j: (j, 0, 0))
    shp3 = (n1, n2, d)
    vg4 = vg.reshape(b, rows, n2, d)
    g04 = g0.reshape(b, rows, n2, d)
    x1r, x1i = _kfa_call([(vg4, 0), (vg4, 1)], 1, n1, n2, d, la1, la2, twc, tws)
    tw_slab = pl.BlockSpec((nf, n2, LANE), lambda j: (j, 0, 0))
    g1r, g1i = pl.pallas_call(
        functools.partial(_hy_b_kernel, nf=nf, d=d), grid=(n1 // nf,),
        in_specs=[_const_spec(fm_b.shape), _const_spec(fm_bc.shape), slab, slab, slab, slab, _const_spec((1, d)),
                  tw_slab, tw_slab],
        out_specs=[slab, slab],
        out_shape=[jax.ShapeDtypeStruct(shp3, BF16)] * 2,
        compiler_params=_cp("parallel"), name="hy_b",
    )(fm_b, fm_bc, x1r.reshape(shp3), x1i.reshape(shp3), kfr.reshape(shp3), kfi.reshape(shp3), scale, twc, tws)
    out = _kfc_call(g1r.reshape(1, n1, n2, d), g1i.reshape(1, n1, n2, d), vg4, g04, skip, n1, n2, d, lc3, lc4)
    return out.reshape(b * n, d)


def _hy_short_kernel(fa_ref, fk_ref, fi_ref, vg_ref, g0_ref, k_ref, nrm_ref, skip_ref, o_ref, *, n):
    z = jnp.concatenate([vg_ref[0], vg_ref[1]], axis=0)
    x = _dot_hi(fa_ref[...], z)
    kk = _dot_hi(fk_ref[...], k_ref[...])
    nn = 2 * n
    sc = 1.0 / (nrm_ref[0:1, :] * nn)
    xr, xi = x[:nn], x[nn:]
    kr, ki = kk[:nn] * sc, kk[nn:] * sc
    y = _dot_hi(fi_ref[...], jnp.concatenate([xr * kr - xi * ki, xr * ki + xi * kr], axis=0))
    skip = skip_ref[...]
    for bi in range(2):
        o_ref[bi] = ((y[bi * n:(bi + 1) * n] + vg_ref[bi] * skip) * g0_ref[bi]).astype(o_ref.dtype)


def _hy_conv_short(vg, g0, k, nrm, skip, b, n, d):
    nn = 2 * n
    cs, sn = _dft_cs(nn, n, nn)
    fa = jnp.asarray(_block_c(cs, sn, -1.0), F32)
    csk, snk = _dft_cs(nn, nn, nn)
    fk = jnp.asarray(np.concatenate([csk, -snk], axis=0), F32)
    csi, sni = _dft_cs(n, nn, nn)
    fi = jnp.asarray(_block_c(csi, sni, 1.0), F32)
    cb = 256
    col3 = pl.BlockSpec((b, n, cb), lambda j: (0, 0, j))
    vec = pl.BlockSpec((1, cb), lambda j: (0, j))
    out = pl.pallas_call(
        functools.partial(_hy_short_kernel, n=n), grid=(d // cb,),
        in_specs=[_const_spec(fa.shape), _const_spec(fk.shape), _const_spec(fi.shape), col3, col3,
                  pl.BlockSpec((nn, cb), lambda j: (0, j)), pl.BlockSpec((8, cb), lambda j: (0, j)), vec],
        out_specs=col3, out_shape=jax.ShapeDtypeStruct((b, n, d), BF16),
        compiler_params=_cp("parallel"), name="hy_short",
    )(fa, fk, fi, vg.reshape(b, n, d), g0.reshape(b, n, d), k, nrm, skip.reshape(1, d))
    return out.reshape(b * n, d)


def _hyena_layer(xl, xc, mods_l, mods_c, pre_g, fin, w_in, b_in, conv_w, conv_b, filt, skip,
                 w_out, b_out, b, n, c):
    d = xl.shape[-1]
    w_in_b = w_in.astype(BF16)
    w_out_b = w_out.astype(BF16)
    tm = min(ROW_TILE, n)
    g0, vg = _hy_in_call(xl, mods_l, pre_g, w_in_b, b_in, conv_w, conv_b, min(2 * ROW_TILE, n), n, BF16)
    k, nrm = _hy_filter_call(n, d, BF16, *filt)
    u_out = _hy_conv_long(vg, g0, k, nrm, skip[0], b, n, d)
    xl, (hl,) = fin(xl, u_out, mods_l, w_out_b, b_out, tm, n)

    g0c, vgc = _hy_in_call(xc, mods_c, pre_g, w_in_b, b_in, conv_w, conv_b, c, c, F32)
    kc, nrmc = _hy_filter_call(c, d, F32, *filt)
    u_out_c = _hy_conv_short(vgc, g0c, kc, nrmc, skip[0], b, c, d)
    xc, (hc,) = fin(xc, u_out_c, mods_c, w_out_b, b_out, c, c, is_ctx=True)
    return xl, hl, hc


def _s5_operators(lam_re, lam_im, log_dt, b_re, b_im, c_re, c_im, d_skip):
    t = S5_T
    g, ns = lam_re.shape[1], lam_re.shape[2]
    gc = b_re.shape[-1]
    gl = LANE // gc
    nblk = g // gl
    lam = lax.complex(lam_re, lam_im)
    dt = jnp.exp(log_dt)[..., None]
    lam_bar = jnp.exp(lam * dt)
    b_bar = ((lam_bar - 1.0) / lam)[..., None] * lax.complex(b_re, b_im)
    c_mat = lax.complex(c_re, c_im)
    pw = jnp.arange(t + 1, dtype=F32)
    lam_pw = jnp.exp((lam * dt)[None] * pw[:, None, None, None])
    hp = HIGHEST
    kern = jnp.einsum('dgcn,tdgn,dgne->dgtce', c_mat, lam_pw[:t], b_bar, precision=hp).real
    dsk = d_skip.reshape(g, gc)
    kt = jnp.swapaxes(kern, -1, -2)
    centre = kt[0][:, 0] + kt[1][:, 0] + jnp.eye(gc, dtype=F32)[None] * dsk[:, :, None]
    ks = jnp.concatenate([kt[1][:, 1:][:, ::-1], centre[:, None], kt[0][:, 1:]], axis=1)
    ks = jnp.transpose(ks.reshape(nblk, gl, 2 * t - 1, gc, gc), (0, 2, 1, 3, 4)).reshape(nblk, 2 * t - 1, LANE, gc)
    same_group = (np.arange(LANE)[:, None] // gc == np.arange(LANE)[None, :] // gc).astype(np.float32)
    d_tab = jnp.tile(ks, (1, 1, 1, gl)) * same_group

    def compact(zc, im_sign):
        z = jnp.concatenate([zc.real, im_sign * zc.imag], axis=-1)
        z = jnp.transpose(z, (2, 1, 0, 3, 4)).reshape(nblk, gl, t, 2, gc, 2 * ns)
        return jnp.transpose(z, (0, 2, 1, 3, 4, 5))

    b_t = jnp.swapaxes(b_bar, -1, -2)
    pf = lam_pw[:t][::-1][:, 0, :, None, :] * b_t[0][None]
    pb = lam_pw[:t][:, 1, :, None, :] * b_t[1][None]
    p_tab = compact(jnp.stack([pf, pb], axis=0), 1.0)
    qf = c_mat[0][None] * lam_pw[1:t + 1, 0][:, :, None, :]
    qb = c_mat[1][None] * lam_pw[1:t + 1][::-1][:, 1][:, :, None, :]
    q_tab = compact(jnp.stack([qf, qb], axis=0), -1.0)

    a = lam_pw[t]
    m_op, p_op, q_op = _s5_expand(d_tab, p_tab, q_tab)
    return m_op, p_op, q_op, a.real.reshape(2, g * ns), a.imag.reshape(2, g * ns)


def _s5_m_kernel(d_ref, o_ref, *, t):
    for s in range(t):
        for tt in range(t):
            o_ref[0, s * LANE:(s + 1) * LANE, tt * LANE:(tt + 1) * LANE] = d_ref[0, tt - s + t - 1].astype(o_ref.dtype)


def _s5_pq_kernel(c_ref, o_ref, *, transpose):
    t, gl, nd, gc, w = c_ref.shape[1:]
    ns = w // 2
    lane_grp = lax.broadcasted_iota(jnp.int32, (gc, gl * ns), 1) // ns
    for j in range(t):
        rows = []
        for g in range(gl):
            cols = []
            for dd in range(nd):
                piece = c_ref[0, j, g, dd]
                for ri in range(2):
                    tiled = jnp.concatenate([piece[:, ri * ns:(ri + 1) * ns]] * gl, axis=1)
                    cols.append(jnp.where(lane_grp == g, tiled, 0.0))
            rows.append(jnp.concatenate(cols, axis=1))
        blk = jnp.concatenate(rows, axis=0)
        if transpose:
            o_ref[0, :, j * LANE:(j + 1) * LANE] = blk.T.astype(o_ref.dtype)
        else:
            o_ref[0, j * LANE:(j + 1) * LANE, :] = blk.astype(o_ref.dtype)


def _s5_expand(d_tab, p_tab, q_tab):
    nblk, nlag = d_tab.shape[:2]
    t = (nlag + 1) // 2
    _, _, gl, nd, gc, w = p_tab.shape
    ncol = nd * gl * w
    whole = lambda shape: pl.BlockSpec((1,) + shape, lambda b: (b,) + (0,) * len(shape))
    m_op = pl.pallas_call(
        functools.partial(_s5_m_kernel, t=t), grid=(nblk,),
        in_specs=[whole((nlag, LANE, LANE))], out_specs=whole((t * LANE, t * LANE)),
        out_shape=jax.ShapeDtypeStruct((nblk, t * LANE, t * LANE), BF16),
        compiler_params=_cp("parallel"), name="s5_m_op",
    )(d_tab)
    tab_spec = whole((t, gl, nd, gc, w))
    p_op = pl.pallas_call(
        functools.partial(_s5_pq_kernel, transpose=False), grid=(nblk,),
        in_specs=[tab_spec], out_specs=whole((t * LANE, ncol)),
        out_shape=jax.ShapeDtypeStruct((nblk, t * LANE, ncol), BF16),
        compiler_params=_cp("parallel"), name="s5_p_op",
    )(p_tab)
    q_op = pl.pallas_call(
        functools.partial(_s5_pq_kernel, transpose=True), grid=(nblk,),
        in_specs=[tab_spec], out_specs=whole((ncol, t * LANE)),
        out_shape=jax.ShapeDtypeStruct((nblk, ncol, t * LANE), BF16),
        compiler_params=_cp("parallel"), name="s5_q_op",
    )(q_tab)
    return m_op, p_op, q_op


def _s5_sum_kernel(*refs, t):
    u_refs = refs[:t]
    p_ref = refs[t]
    outs = refs[t + 1:]
    u = jnp.concatenate([r[...] for r in u_refs], axis=1)
    res = _dot(u, p_ref[0])
    w = res.shape[1] // len(outs)
    for i, o in enumerate(outs):
        o[...] = res[:, i * w:(i + 1) * w]


def _s5_sum_call(h, p_op, rb):
    rows = h.shape[0]
    t = S5_T
    nblk = p_op.shape[0]
    w = p_op.shape[2] // 4
    u_specs = [pl.BlockSpec((rb, LANE), lambda gb, r, s=s: (r, s * nblk + gb)) for s in range(t)]
    out_spec = pl.BlockSpec((rb, w), lambda gb, r: (r, gb))
    return pl.pallas_call(
        functools.partial(_s5_sum_kernel, t=t), grid=(nblk, rows // rb),
        in_specs=u_specs + [pl.BlockSpec((1,) + p_op.shape[1:], lambda gb, r: (gb, 0, 0))],
        out_specs=[out_spec] * 4,
        out_shape=[jax.ShapeDtypeStruct((rows, nblk * w), F32)] * 4,
        compiler_params=_cp("parallel", "parallel"), name="s5_sum",
    )(*([h] * t), p_op)


def _s5_rec_kernel(sr_ref, si_ref, ar_ref, ai_ref, h0r_ref, h0i_ref, hr_ref, hi_ref, fr_ref, fi_ref,
                   cr, ci, *, kb, reverse):
    @pl.when(pl.program_id(1) == 0)
    def _():
        cr[...] = h0r_ref[0]
        ci[...] = h0i_ref[0]

    ar, ai = ar_ref[...], ai_ref[...]

    def body(i, carry):
        hr, hi = carry
        k = kb - 1 - i if reverse else i
        hr_ref[pl.ds(k, 1), :] = hr
        hi_ref[pl.ds(k, 1), :] = hi
        nr = ar * hr - ai * hi + sr_ref[pl.ds(k, 1), :]
        ni = ar * hi + ai * hr + si_ref[pl.ds(k, 1), :]
        return nr, ni

    hr, hi = lax.fori_loop(0, kb, body, (cr[...], ci[...]))
    cr[...] = hr
    ci[...] = hi
    fr_ref[0] = hr
    fi_ref[0] = hi


def _s5_rec_call(sr, si, ar, ai, h0r, h0i, nb_batch, reverse):
    rows, w = sr.shape
    nk = rows // nb_batch
    kb = min(64, nk)
    nb = nk // kb
    blk = (lambda bi, i: (bi * nb + nb - 1 - i, 0)) if reverse else (lambda bi, i: (bi * nb + i, 0))
    row_spec = pl.BlockSpec((kb, w), blk)
    vec = _const_spec((1, w))
    st = pl.BlockSpec((1, 1, w), lambda bi, i: (bi, 0, 0))
    return pl.pallas_call(
        functools.partial(_s5_rec_kernel, kb=kb, reverse=reverse), grid=(nb_batch, nb),
        in_specs=[row_spec, row_spec, vec, vec, st, st],
        out_specs=[row_spec, row_spec, st, st],
        out_shape=[jax.ShapeDtypeStruct((rows, w), F32)] * 2 + [jax.ShapeDtypeStruct((nb_batch, 1, w), F32)] * 2,
        scratch_shapes=[pltpu.VMEM((1, w), F32), pltpu.VMEM((1, w), F32)],
        compiler_params=_cp("parallel", "arbitrary"), name="s5_rec",
    )(sr, si, ar, ai, h0r, h0i)


def _s5_out_kernel(*refs, t):
    u_refs = refs[:t]
    h_refs = refs[t:t + 4]
    m_ref, q_ref, o_ref, tok_scr = refs[t + 4:]
    u = jnp.concatenate([r[...] for r in u_refs], axis=1)
    hcat = jnp.concatenate([r[...].astype(BF16) for r in h_refs], axis=1)
    y = _dot(u, m_ref[0]) + _dot(hcat, q_ref[0])
    g = 0.5 * y * (1.0 + lax.erf(y * (2.0 ** -0.5)))
    rb = g.shape[0]
    for j in range(t):
        tok_scr[pl.ds(j, rb, stride=t), :] = g[:, j * LANE:(j + 1) * LANE]
    o_ref[...] = tok_scr[...].astype(o_ref.dtype)


def _s5_out_call(h, states, m_op, q_op, rb):
    rows = h.shape[0]
    t = S5_T
    nblk = m_op.shape[0]
    w = q_op.shape[1] // 4
    u_specs = [pl.BlockSpec((rb, LANE), lambda gb, r, s=s: (r, s * nblk + gb)) for s in range(t)]
    return pl.pallas_call(
        functools.partial(_s5_out_kernel, t=t), grid=(nblk, rows // rb),
        in_specs=u_specs + [pl.BlockSpec((rb, w), lambda gb, r: (r, gb))] * 4
        + [pl.BlockSpec((1,) + m_op.shape[1:], lambda gb, r: (gb, 0, 0)),
           pl.BlockSpec((1,) + q_op.shape[1:], lambda gb, r: (gb, 0, 0))],
        out_specs=pl.BlockSpec((rb * t, LANE), lambda gb, r: (r, gb)),
        out_shape=jax.ShapeDtypeStruct((rows * t, nblk * LANE), BF16),
        scratch_shapes=[pltpu.VMEM((rb * t, LANE), F32)],
        compiler_params=_cp("parallel", "parallel"), name="s5_out",
    )(*([h] * t), *states, m_op, q_op)


def _s5_layer(xl, hl, hc, mods_l, fin, lam_re, lam_im, log_dt, b_re, b_im, c_re, c_im,
              d_skip, w_glu, b_glu, b, n, c):
    t = S5_T
    m_op, p_op, q_op, a_re, a_im = _s5_operators(lam_re, lam_im, log_dt, b_re, b_im, c_re, c_im, d_skip)
    w = a_re.shape[-1]

    def scan(h, init):
        sfr, sfi, sbr, sbi = _s5_sum_call(h, p_op, min(ROW_TILE, h.shape[0]))
        hfr, hfi, ffr, ffi = _s5_rec_call(sfr, sfi, a_re[0:1], a_im[0:1], init[0], init[1], b, False)
        hbr, hbi, fbr, fbi = _s5_rec_call(sbr, sbi, a_re[1:2], a_im[1:2], init[2], init[3], b, True)
        return (hfr, hfi, hbr, hbi), (ffr, ffi, fbr, fbi)

    zeros = jnp.zeros((b, 1, w), F32)
    _, ctx_final = scan(hc, (zeros,) * 4)
    states, _ = scan(hl, ctx_final)
    nk = n // t
    g_nat = _s5_out_call(hl, states, m_op, q_op, min(ROW_TILE, b * nk))
    return fin(xl, g_nat, mods_l, w_glu.astype(BF16), b_glu, min(ROW_TILE, n), n, glu=True)


def _fnet_channel_mats():
    cc, sc = _dft_cs(FNET_GC, FNET_GC, FNET_GC)
    return jnp.asarray(np.concatenate([cc, sc], axis=1) / np.sqrt(FNET_GC), BF16)


def _fn_c_kernel(l5_ref, l6_ref, xr_ref, xi_ref, o_ref, *, nh):
    r = KRON_R
    n2 = nh * r
    l5 = l5_ref[...]
    y5 = [_dot(l5, jnp.concatenate([xr_ref[0, f], xi_ref[0, f]], axis=0)).astype(BF16) for f in range(r)]
    l6 = l6_ref[...]
    hf = KRON_H
    for p in range(nh):
        outs = []
        for fs in range(0, r, hf):
            x = jnp.concatenate([y5[f][p * r:(p + 1) * r] for f in range(fs, fs + hf)]
                                + [y5[f][n2 + p * r:n2 + (p + 1) * r] for f in range(fs, fs + hf)], axis=0)
            outs.append(_dot(l6, x))
        for q in range(r):
            rows = jnp.concatenate([o[q * hf:(q + 1) * hf] for o in outs], axis=0)
            o_ref[0, p + nh * q] = rows.astype(o_ref.dtype)


def _fnet_layer(xl, am, bm, mods_l, fin, w_o, b_o, b, n, d):
    n2 = FFT_N2
    n1 = n // n2
    tm = min(ROW_TILE, n)
    a, _ = _split_radix(n1)
    l1, l2 = _slab_dft_mats(n1, a, 0, -1, neg_im=True, scale=1.0 / np.sqrt(n))
    twc, tws = _twiddle_tables(n1, n2)
    a4, b4 = am.reshape(b, n1, n2, d), bm.reshape(b, n1, n2, d)
    xr, xi = _kfa_call([(a4, None), (b4, None)], b, n1, n2, d, l1, l2, twc, tws)

    r, w = KRON_R, min(KRON_W, d)
    nh = n2 // r
    assert n1 % r == 0 and n2 % r == 0
    m5 = np.zeros((n2, n2), np.complex128)
    hf = KRON_H
    m6 = np.zeros((r * hf, hf * r), np.complex128)
    for s in range(r):
        for p in range(nh):
            for h in range(nh):
                m5[p * r + s, h * r + s] = np.exp(-2j * np.pi * (p * h / nh + p * s / n2))
        for q in range(r):
            for f in range(hf):
                m6[q * hf + f, f * r + s] = np.exp(-2j * np.pi * q * s / r)
    l5 = jnp.asarray(_cblock(m5), BF16)
    l6 = jnp.asarray(np.concatenate([m6.real, -m6.imag], axis=1), BF16)
    grp = pl.BlockSpec((1, r, n2, w), lambda bi, fh, k: (bi, fh, 0, k))
    y = pl.pallas_call(
        functools.partial(_fn_c_kernel, nh=nh), grid=(b, n1 // r, d // w),
        in_specs=[_const_spec(l5.shape), _const_spec(l6.shape), grp, grp],
        out_specs=pl.BlockSpec((1, n2, None, r, w), lambda bi, fh, k: (bi, 0, fh, 0, k)),
        out_shape=jax.ShapeDtypeStruct((b, n2, n1 // r, r, d), BF16),
        compiler_params=_cp("parallel", "parallel", "parallel"), name="fn_c",
    )(l5, l6, xr, xi)
    return fin(xl, y.reshape(b * n, d), mods_l, w_o.astype(BF16), b_o, tm, n)


def kernel(x, c, ctx, c_ctx, mod_w, mod_b, mix_pre_g, mix_post_g, ffn_pre_g, ffn_post_g, ffn_w13, ffn_w2,
           mla_w_in, mla_q_norm_g, mla_kv_norm_g, mla_w_uq, mla_w_ukv, mla_w_o,
           hy_w_in, hy_b_in, hy_conv_w, hy_conv_b, hy_f_w1, hy_f_b1, hy_f_freq, hy_f_w2, hy_f_b2, hy_f_w3,
           hy_skip, hy_w_out, hy_b_out,
           s5_lambda_re, s5_lambda_im, s5_log_dt, s5_b_re, s5_b_im, s5_c_re, s5_c_im, s5_d, s5_w_glu, s5_b_glu,
           fn_w_o, fn_b_o):
    b, n, d = x.shape
    cl = ctx.shape[1]
    depth = mod_w.shape[0]
    assert b == 2 and depth == 4, "two batches ride one complex transform; one layer per mixer"
    mods = _mods(c, c_ctx, mod_w, mod_b)
    xl = x.reshape(b * n, d)
    xc = ctx.reshape(b * cl, d)

    w13_all = ffn_w13.astype(BF16)
    w2_all = ffn_w2.astype(BF16)

    def mods_c(i):
        return jnp.broadcast_to(mods[i, 2:3], (b, 8, d))

    def finisher(i, emit_kind=None, cs=None):
        def fin(x_, y_, mods_, wm, bm, tm_, rows_per_batch, glu=False, is_ctx=False):
            emit = None
            if emit_kind is not None:
                emit = (emit_kind, mods_c(i + 1) if is_ctx else mods[i + 1, 0:2], mix_pre_g[i + 1], cs)
            return _mix_ffn_call(x_, y_, mods_, wm, bm, mix_post_g[i], ffn_pre_g[i], ffn_post_g[i],
                                 w13_all, w2_all, i, tm_, rows_per_batch, glu, emit)
        return fin

    xl, xc = _mla_layer(xl, xc, mods[0, 0:2], mods_c(0), mix_pre_g[0], finisher(0), mla_w_in[0],
                        mla_q_norm_g[0], mla_kv_norm_g[0], mla_w_uq[0], mla_w_ukv[0], mla_w_o[0], b, n, cl)
    filt = (hy_f_w1[0], hy_f_b1[0], hy_f_freq[0], hy_f_w2[0], hy_f_b2[0], hy_f_w3[0])
    xl, hl, hc = _hyena_layer(xl, xc, mods[1, 0:2], mods_c(1), mix_pre_g[1], finisher(1, "chunks"), hy_w_in[0],
                              hy_b_in[0], hy_conv_w[0], hy_conv_b[0], filt, hy_skip[0], hy_w_out[0], hy_b_out[0],
                              b, n, cl)
    xl, (am, bm) = _s5_layer(xl, hl, hc, mods[2, 0:2], finisher(2, "fnet", _fnet_channel_mats()), s5_lambda_re[0],
                             s5_lambda_im[0], s5_log_dt[0], s5_b_re[0], s5_b_im[0], s5_c_re[0], s5_c_im[0],
                             s5_d[0], s5_w_glu[0], s5_b_glu[0], b, n, cl)
    xl = _fnet_layer(xl, am, bm, mods[3, 0:2], finisher(3), fn_w_o[0], fn_b_o[0], b, n, d)
    return xl.reshape(b, n, d)
```

```python
import functools
import math

import numpy as np
import jax
import jax.numpy as jnp
from jax import lax
from jax.experimental import pallas as pl
from jax.experimental.pallas import tpu as pltpu

F32 = jnp.float32
BF16 = jnp.bfloat16
NORM_EPS = 1e-6
LANE = 128
MXU_COLS = 256
ROW_TILE = 512
VMEM_LIMIT = 56 * 1024 * 1024
HIGHEST = lax.Precision.HIGHEST

GRID_W = 64
ROPE_THETA = 10000.0
MLA_HEADS = 8
MLA_NOPE = 128
MLA_ROPE = 64
MLA_V = 128
MLA_VT = MLA_V + 16
MLA_QK = MLA_NOPE + 2 * MLA_ROPE
HYENA_BANDS = 16
HYENA_TARGET = 1e-2
HYENA_FAST = 0.3
HYENA_SLOW = 1.5
S5_T = 16
FNET_GC = 128
FFT_N2 = 128


def _cp(*sem):
    return pltpu.CompilerParams(dimension_semantics=sem, vmem_limit_bytes=VMEM_LIMIT)


def _dot(a, b):
    return jnp.dot(a, b, preferred_element_type=F32)


def _dot_hi(a, b):
    return jnp.dot(a, b, preferred_element_type=F32, precision=HIGHEST)


def _dot_x3(a, b):
    a_hi = a.astype(BF16)
    b_hi = b.astype(BF16)
    a_lo = (a - a_hi.astype(F32)).astype(BF16)
    b_lo = (b - b_hi.astype(F32)).astype(BF16)
    return _dot(a_hi, b_hi) + (_dot(a_hi, b_lo) + _dot(a_lo, b_hi))


def _rms(x, g):
    ms = jnp.mean(x * x, axis=-1, keepdims=True)
    return x * lax.rsqrt(ms + NORM_EPS) * g


def _normmod(x, g, shift, scale):
    return _rms(x, g) * (1.0 + scale) + shift


def _const_spec(shape):
    nd = len(shape)
    return pl.BlockSpec(shape, lambda *_: (0,) * nd)


def _mods_kernel(st_ref, w_ref, b_ref, o_ref):
    st = st_ref[...]
    st = st * jax.nn.sigmoid(st)
    w = w_ref[0]
    rows = [jnp.sum(st[:, r:r + 1] * w, axis=0, keepdims=True) for r in range(3)]
    rows.append(jnp.zeros((5, w.shape[1]), F32))
    o_ref[0] = jnp.concatenate(rows, axis=0) + b_ref[0]


def _mods(c, c_ctx, mod_w, mod_b):
    depth, d, n6 = mod_w.shape
    st = jnp.zeros((d, 8), F32).at[:, 0:2].set(c.T).at[:, 2].set(c_ctx)
    tn = 1024
    out = pl.pallas_call(
        _mods_kernel,
        grid=(depth, n6 // tn),
        in_specs=[_const_spec((d, 8)),
                  pl.BlockSpec((1, d, tn), lambda i, j: (i, 0, j)),
                  pl.BlockSpec((1, 1, tn), lambda i, j: (i, 0, j))],
        out_specs=pl.BlockSpec((1, 8, tn), lambda i, j: (i, 0, j)),
        out_shape=jax.ShapeDtypeStruct((depth, 8, n6), F32),
        compiler_params=_cp("parallel", "parallel"),
        name="mods",
    )(st, mod_w, mod_b.reshape(depth, 1, n6))
    m = out[:, :3].reshape(depth, 3, n6 // d, d)
    return jnp.pad(m, ((0, 0), (0, 0), (0, 8 - n6 // d), (0, 0)))


def _row_specs(tm, d, tpb):
    x_spec = pl.BlockSpec((tm, d), lambda i: (i, 0))
    mod_spec = pl.BlockSpec((1, 8, d), lambda i: (i // tpb, 0, 0))
    return x_spec, mod_spec


def _channel_dft(h, cs):
    gc = FNET_GC
    ab = [_dot(h[:, k * gc:(k + 1) * gc], cs) for k in range(h.shape[1] // gc)]
    return (jnp.concatenate([z[:, :gc] for z in ab], axis=1), jnp.concatenate([z[:, gc:] for z in ab], axis=1))


def _mix_ffn_kernel(x_ref, y_ref, mod_ref, wm_ref, bm_ref, gm_ref, pre_ref, post_ref, w13_ref, w2_ref, *rest,
                    f, cuts, glu, emit):
    if emit == "chunks":
        nmod_ref, ng_ref, o_ref, h_ref, tok_scr = rest
    elif emit == "fnet":
        nmod_ref, ng_ref, cs_ref, o_ref, a_ref, b_ref = rest
    else:
        (o_ref,) = rest
    z = _dot(y_ref[...].astype(BF16), wm_ref[...]) + bm_ref[...]
    if glu:
        d = o_ref.shape[-1]
        z = z[:, :d] * jax.nn.sigmoid(z[:, d:])
    x = x_ref[...] + mod_ref[0, 2:3, :] * _rms(z, gm_ref[...])
    h = _normmod(x, pre_ref[...], mod_ref[0, 3:4, :], mod_ref[0, 4:5, :]).astype(BF16)
    acc = None
    for lo, hi in zip(cuts[:-1], cuts[1:]):
        a = _dot(h, w13_ref[:, lo:hi])
        b = _dot(h, w13_ref[:, f + lo:f + hi])
        gact = (a * jax.nn.sigmoid(a) * b).astype(BF16)
        part = _dot(gact, w2_ref[lo:hi, :])
        acc = part if acc is None else acc + part
    xo = x + mod_ref[0, 5:6, :] * _rms(acc, post_ref[...])
    o_ref[...] = xo
    if emit:
        hn = _normmod(xo, ng_ref[...], nmod_ref[0, 0:1, :], nmod_ref[0, 1:2, :])
        if emit == "chunks":
            tm, d = hn.shape
            for lt in range(d // LANE):
                tok_scr[lt * tm:(lt + 1) * tm, :] = hn[:, lt * LANE:(lt + 1) * LANE]
            for j in range(S5_T):
                for lt in range(d // LANE):
                    rows = tok_scr[pl.ds(lt * tm + j, tm // S5_T, stride=S5_T), :]
                    h_ref[:, j * d + lt * LANE:j * d + (lt + 1) * LANE] = rows.astype(h_ref.dtype)
        else:
            a, b = _channel_dft(hn.astype(BF16), cs_ref[...])
            a_ref[...] = a.astype(a_ref.dtype)
            b_ref[...] = b.astype(b_ref.dtype)


def _mix_ffn_call(x, y, mods, wm, bm, mix_post_g, ffn_pre_g, ffn_post_g, w13, w2, layer, tm, rows_per_batch,
                  glu=False, emit=None):
    m, d = x.shape
    k, nm = wm.shape
    f = w2.shape[1]
    cuts = tuple(range(0, f, 6 * MXU_COLS)) + (f,)
    x_spec, mod_spec = _row_specs(tm, d, rows_per_batch // tm)
    once = lambda shape: pl.BlockSpec(shape, lambda i: (0, 0), pipeline_mode=pl.Buffered(1))
    of_layer = lambda shape: pl.BlockSpec((None,) + shape, lambda i: (layer, 0, 0), pipeline_mode=pl.Buffered(1))
    in_specs = [x_spec, pl.BlockSpec((tm, k), lambda i: (i, 0)), mod_spec,
                once((k, nm)), _const_spec((1, nm)), _const_spec((1, d)), _const_spec((1, d)),
                _const_spec((1, d)), of_layer((d, 2 * f)), of_layer((f, d))]
    args = [x, y, mods, wm, bm.reshape(1, nm), mix_post_g.reshape(1, d), ffn_pre_g.reshape(1, d),
            ffn_post_g.reshape(1, d), w13, w2]
    out_specs, out_shape = [x_spec], [jax.ShapeDtypeStruct((m, d), F32)]
    kind = None
    scratch = []
    if emit is not None:
        kind, next_mods, next_g, cs = emit
        in_specs += [mod_spec, _const_spec((1, d))]
        args += [next_mods, next_g.reshape(1, d)]
        if kind == "fnet":
            in_specs.append(_const_spec(cs.shape))
            args.append(cs)
            out_specs += [x_spec] * 2
            out_shape += [jax.ShapeDtypeStruct((m, d), BF16)] * 2
        else:
            out_specs.append(pl.BlockSpec((tm // S5_T, S5_T * d), lambda i: (i, 0)))
            out_shape.append(jax.ShapeDtypeStruct((m // S5_T, S5_T * d), BF16))
            scratch.append(pltpu.VMEM((tm * (d // LANE), LANE), F32))
    outs = pl.pallas_call(
        functools.partial(_mix_ffn_kernel, f=f, cuts=cuts, glu=glu, emit=kind), grid=(m // tm,),
        in_specs=in_specs, out_specs=out_specs, out_shape=out_shape, scratch_shapes=scratch,
        compiler_params=_cp("parallel"), name="mix_ffn",
    )(*args)
    return outs[0] if emit is None else (outs[0], tuple(outs[1:]))


def _mla_proj_kernel(x_ref, mod_ref, g_ref, w_ref, qg_ref, kvg_ref, wq_ref, wkv_ref, cp_ref, sp_ref,
                     q_ref, k_ref, vt_ref, *, ql, kvl, qscale, positional):
    h = _normmod(x_ref[...], g_ref[...], mod_ref[0, 0:1, :], mod_ref[0, 1:2, :])
    z = _dot(h.astype(BF16), w_ref[...])
    qn = _rms(z[:, :ql], qg_ref[...]).astype(BF16)
    cn = _rms(z[:, ql:ql + kvl], kvg_ref[...]).astype(BF16)
    tk = z.shape[0]
    low = lax.broadcasted_iota(jnp.int32, (tk, 2 * MLA_ROPE), 1) < MLA_ROPE
    if positional:
        cpf = jnp.concatenate([cp_ref[...]] * 2, axis=1)
        spf = jnp.concatenate([sp_ref[...]] * 2, axis=1)

    def rope_slots(pair, keep_raw):
        swapped = pltpu.roll(pair, MLA_ROPE, 1)
        raw = jnp.where(low, 0.0, swapped) if keep_raw else None
        if not positional:
            return raw
        rot = jnp.where(low, pair * cpf + swapped * spf, 0.0)
        return rot + raw if keep_raw else rot

    kr = rope_slots(z[:, ql + kvl:], keep_raw=not positional).astype(BF16)

    zkv = _dot(cn, wkv_ref[...])
    ones_blk = (lax.broadcasted_iota(jnp.int32, (MLA_VT - MLA_V, tk), 0) == 0).astype(BF16)
    for hd in range(MLA_HEADS):
        base = hd * (MLA_NOPE + MLA_V)
        k_ref[0, hd, 0, :, 0:MLA_NOPE] = zkv[:, base:base + MLA_NOPE].astype(BF16)
        k_ref[0, hd, 0, :, MLA_NOPE:MLA_QK] = kr
        vt_ref[0, hd, 0, 0:MLA_V, :] = zkv[:, base + MLA_NOPE:base + MLA_NOPE + MLA_V].T.astype(BF16)
        vt_ref[0, hd, 0, MLA_V:MLA_VT, :] = ones_blk

    zq = _dot(qn, wq_ref[...])
    for hd in range(MLA_HEADS):
        base = hd * MLA_QK
        rp = rope_slots(zq[:, base + MLA_NOPE:base + MLA_QK], keep_raw=True)
        qcat = jnp.concatenate([zq[:, base:base + MLA_NOPE], rp], axis=1) * qscale
        q_ref[0, hd] = qcat.T.astype(BF16)


def _flash_kernel(q_ref, kc_ref, vc_ref, *rest, n_lat):
    if n_lat:
        kl_ref, vl_ref, o_ref, s_scr, acc_scr = rest
    else:
        o_ref, acc_scr = rest
    qt = q_ref[0, 0]

    def qk(k, slot):
        s = _dot(k, qt)
        s_scr[slot] = s
        return jnp.max(s, axis=0, keepdims=True)

    def sm_pv(slot, vt, m, mx):
        m_new = jnp.maximum(m, mx)
        alpha = jnp.exp2(m - m_new)
        p = jnp.exp2(s_scr[slot] - m_new).astype(BF16)
        acc_scr[...] = alpha * acc_scr[...] + _dot(vt, p)
        return m_new

    sc = _dot(kc_ref[0, 0, 0], qt)
    if n_lat:
        mx = qk(kl_ref[0, 0, 0], 0)
    m = jnp.max(sc, axis=0, keepdims=True)
    acc_scr[...] = _dot(vc_ref[0, 0, 0], jnp.exp2(sc - m).astype(BF16))
    if n_lat:

        per = next(p for p in (16, 8, 4, 2) if n_lat % p == 0)

        def body(i, carry):
            m, mx_cur = carry
            c = per * i
            for u in range(per):
                mx_next = qk(kl_ref[0, 0, jnp.minimum(c + u + 1, n_lat - 1)], (u + 1) % 2)
                m = sm_pv(u % 2, vl_ref[0, 0, c + u], m, mx_cur)
                mx_cur = mx_next
            return m, mx_cur

        lax.fori_loop(0, n_lat // per, body, (m, mx))
    acc = acc_scr[...]
    o_ref[0] = (acc[0:MLA_V] / acc[MLA_V:MLA_V + 1]).T.astype(o_ref.dtype)


def _rope_tables(n_lat):
    rows = n_lat // GRID_W
    row = jnp.repeat(jnp.arange(rows, dtype=F32), GRID_W)
    col = jnp.tile(jnp.arange(GRID_W, dtype=F32), rows)
    axis_dim = MLA_ROPE // 2
    inv_freq = 1.0 / (ROPE_THETA ** (jnp.arange(0, axis_dim, 2, dtype=F32) / axis_dim))
    ang_r = row[:, None] * inv_freq
    ang_c = col[:, None] * inv_freq
    cr, sr, cc, sc = jnp.cos(ang_r), jnp.sin(ang_r), jnp.cos(ang_c), jnp.sin(ang_c)
    cp = jnp.concatenate([cr, cr, cc, cc], axis=-1)
    sp = jnp.concatenate([-sr, sr, -sc, sc], axis=-1)
    return cp, sp


_ROPE_SWAP = np.concatenate([np.arange(16, 32), np.arange(0, 16), np.arange(48, 64), np.arange(32, 48)])


def _mla_side(x, mods, pre_g, w_in_ext, q_g, kv_g, w_uq_ext, w_ukv, tabs, b, n, tk, positional):
    m, d = x.shape
    ql, kvl = q_g.shape[-1], kv_g.shape[-1]
    nc = n // tk
    x_spec, mod_spec = _row_specs(tk, d, nc)
    tab_spec = pl.BlockSpec((tk, MLA_ROPE), lambda i: (i % nc, 0))
    qscale = (MLA_NOPE + MLA_ROPE) ** -0.5 * math.log2(math.e)
    return pl.pallas_call(
        functools.partial(_mla_proj_kernel, ql=ql, kvl=kvl, qscale=qscale, positional=positional),
        grid=(m // tk,),
        in_specs=[x_spec, mod_spec, _const_spec((1, d)), _const_spec(w_in_ext.shape),
                  _const_spec((1, ql)), _const_spec((1, kvl)), _const_spec(w_uq_ext.shape),
                  _const_spec(w_ukv.shape), tab_spec, tab_spec],
        out_specs=[pl.BlockSpec((1, MLA_HEADS, MLA_QK, tk), lambda i: (i // nc, 0, 0, i % nc)),
                   pl.BlockSpec((1, MLA_HEADS, 1, tk, MLA_QK), lambda i: (i // nc, 0, i % nc, 0, 0)),
                   pl.BlockSpec((1, MLA_HEADS, 1, MLA_VT, tk), lambda i: (i // nc, 0, i % nc, 0, 0))],
        out_shape=[jax.ShapeDtypeStruct((b, MLA_HEADS, MLA_QK, n), BF16),
                   jax.ShapeDtypeStruct((b, MLA_HEADS, nc, tk, MLA_QK), BF16),
                   jax.ShapeDtypeStruct((b, MLA_HEADS, nc, MLA_VT, tk), BF16)],
        compiler_params=_cp("parallel"), name="mla_proj",
    )(x, mods, pre_g.reshape(1, d), w_in_ext, q_g.reshape(1, ql), kv_g.reshape(1, kvl), w_uq_ext, w_ukv, *tabs)


def _flash_call(qt, kc, vtc, kl, vtl, tq):
    b, hh, _, n = qt.shape
    c = kc.shape[-2]
    n_lat = 0 if kl is None else kl.shape[2]
    in_specs = [pl.BlockSpec((1, 1, MLA_QK, tq), lambda bi, h, i: (bi, h, 0, i)),
                pl.BlockSpec((1, 1, 1, c, MLA_QK), lambda bi, h, i: (bi, h, 0, 0, 0)),
                pl.BlockSpec((1, 1, 1, MLA_VT, c), lambda bi, h, i: (bi, h, 0, 0, 0))]
    args = [qt, kc, vtc]
    scratch = [pltpu.VMEM((MLA_VT, tq), F32)]
    if n_lat:
        assert n_lat % 2 == 0, "latent key chunks are consumed in pairs"
        tk = kl.shape[-2]
        in_specs += [pl.BlockSpec((1, 1, n_lat, tk, MLA_QK), lambda bi, h, i: (bi, h, 0, 0, 0)),
                     pl.BlockSpec((1, 1, n_lat, MLA_VT, tk), lambda bi, h, i: (bi, h, 0, 0, 0))]
        args += [kl, vtl]
        scratch = [pltpu.VMEM((2, tk, tq), F32)] + scratch
    return pl.pallas_call(
        functools.partial(_flash_kernel, n_lat=n_lat), grid=(b, hh, n // tq),
        in_specs=in_specs,
        out_specs=pl.BlockSpec((1, tq, MLA_V), lambda bi, h, i: (bi, i, h)),
        out_shape=jax.ShapeDtypeStruct((b, n, hh * MLA_V), BF16),
        scratch_shapes=scratch,
        compiler_params=_cp("parallel", "parallel", "arbitrary"), name="flash",
    )(*args)


def _mla_layer(xl, xc, mods_l, mods_c, pre_g, fin, w_in, q_g, kv_g, w_uq, w_ukv, w_o, b, n, c):
    d = xl.shape[-1]
    ql, kvl = q_g.shape[-1], kv_g.shape[-1]
    hh = MLA_HEADS
    rope_cols = w_in[:, ql + kvl:]
    w_in_ext = jnp.concatenate([w_in, rope_cols[:, _ROPE_SWAP]], axis=1).astype(BF16)
    wq = w_uq.reshape(ql, hh, MLA_NOPE + MLA_ROPE)
    w_uq_ext = jnp.concatenate([wq, wq[:, :, MLA_NOPE:][:, :, _ROPE_SWAP]], axis=-1)
    w_uq_ext = w_uq_ext.reshape(ql, hh * MLA_QK).astype(BF16)
    w_ukv_b = w_ukv.astype(BF16)
    w_o_b = w_o.astype(BF16)

    tabs = _rope_tables(n)

    tm_l = min(ROW_TILE, n)
    tk_l = min(ROW_TILE, n // 2)
    ql_, kl, vtl = _mla_side(xl, mods_l, pre_g, w_in_ext, q_g, kv_g, w_uq_ext, w_ukv_b, tabs, b, n, tk_l, True)
    qc_, kc, vtc = _mla_side(xc, mods_c, pre_g, w_in_ext, q_g, kv_g, w_uq_ext, w_ukv_b, tabs, b, c, c, False)
    o_lat = _flash_call(ql_, kc, vtc, kl, vtl, min(2 * ROW_TILE, n)).reshape(b * n, hh * MLA_V)
    o_ctx = _flash_call(qc_, kc, vtc, None, None, c).reshape(b * c, hh * MLA_V)
    zb = jnp.zeros((d,), F32)
    xl = fin(xl, o_lat, mods_l, w_o_b, zb, tm_l, n)
    xc = fin(xc, o_ctx, mods_c, w_o_b, zb, c, c)
    return xl, xc


def _hy_in_kernel(x_ref, xp_ref, xn_ref, mod_ref, g_ref, w_ref, b_ref, cw_ref, cb_ref,
                  g0_ref, vg_ref, *, tpb):
    i = pl.program_id(0)
    g = g_ref[...]
    shift, scale = mod_ref[0, 0:1, :], mod_ref[0, 1:2, :]
    xcat = jnp.concatenate([xp_ref[...], x_ref[...], xn_ref[...]], axis=0)
    hcat = _normmod(xcat, g, shift, scale).astype(BF16)
    tm = x_ref.shape[0]
    d = g0_ref.shape[-1]
    first = (i % tpb) == 0
    last = (i % tpb) == tpb - 1
    ridx = lax.broadcasted_iota(jnp.int32, (tm, 1), 0)

    def conv_part(c):
        cols = slice(c * d, (c + 1) * d)
        ucat = _dot(hcat, w_ref[:, cols]) + b_ref[:, cols]
        u = ucat[8:tm + 8]
        prev_row = jnp.where(first, 0.0, ucat[7:8, :])
        next_row = jnp.where(last, 0.0, ucat[tm + 8:tm + 9, :])
        dn = jnp.where(ridx == 0, prev_row, pltpu.roll(u, 1, 0))
        upw = jnp.where(ridx == tm - 1, next_row, pltpu.roll(u, tm - 1, 0))
        return cb_ref[:, cols] + dn * cw_ref[0:1, cols] + u * cw_ref[1:2, cols] + upw * cw_ref[2:3, cols]

    g0_ref[...] = conv_part(0).astype(g0_ref.dtype)
    vg_ref[...] = (conv_part(2) * conv_part(1)).astype(vg_ref.dtype)


def _hy_in_call(x, mods, pre_g, w_in, b_in, conv_w, conv_b, tm, n, out_dtype):
    m, d = x.shape
    p = w_in.shape[1]
    tpb = n // tm
    x_spec, mod_spec = _row_specs(tm, d, tpb)
    r8 = tm // 8
    nb8 = m // 8
    prev_spec = pl.BlockSpec((8, d), lambda i: (jnp.maximum(i * r8 - 1, 0), 0))
    next_spec = pl.BlockSpec((8, d), lambda i: (jnp.minimum((i + 1) * r8, nb8 - 1), 0))
    cw = jnp.pad(conv_w, ((0, 8 - conv_w.shape[0]), (0, 0)))
    return pl.pallas_call(
        functools.partial(_hy_in_kernel, tpb=tpb), grid=(m // tm,),
        in_specs=[x_spec, prev_spec, next_spec, mod_spec, _const_spec((1, d)), _const_spec((d, p)),
                  _const_spec((1, p)), _const_spec((8, p)), _const_spec((1, p))],
        out_specs=[x_spec, x_spec],
        out_shape=[jax.ShapeDtypeStruct((m, d), out_dtype), jax.ShapeDtypeStruct((m, d), out_dtype)],
        compiler_params=_cp("parallel"), name="hy_in",
    )(x, x, x, mods, pre_g.reshape(1, d), w_in, b_in.reshape(1, p), cw, conv_b.reshape(1, p))


_PI_SPLIT = (3.140625, 9.67502593994140625e-4, 1.509957990978376432e-7)
_SIN_TAYLOR = (-1.0 / 6, 1.0 / 120, -1.0 / 5040, 1.0 / 362880, -1.0 / 39916800)


def _sin(x):
    kf = jnp.round(x * (1.0 / math.pi))
    r = ((x - kf * _PI_SPLIT[0]) - kf * _PI_SPLIT[1]) - kf * _PI_SPLIT[2]
    r2 = r * r
    p = _SIN_TAYLOR[4]
    for c in _SIN_TAYLOR[3::-1]:
        p = p * r2 + c
    s = r + r * r2 * p
    odd = (kf.astype(jnp.int32) & 1) == 1
    return jnp.where(odd, -s, s)


def _hy_filter_kernel(bands_ref, w1_ref, b1_ref, fq_ref, w2_ref, b2_ref, w3_ref, dl_ref,
                      k_ref, nrm_ref, *, n, tr):
    i = pl.program_id(0)
    bwd = i >= n // tr
    row = lax.broadcasted_iota(jnp.int32, (tr, LANE), 0) + i * tr
    j = jnp.where(bwd, 2 * n - row, row).astype(F32)
    lane = lax.broadcasted_iota(jnp.int32, (tr, LANE), 1)
    t = j * (1.0 / (n - 1))
    arg = (2.0 * math.pi / n) * j * bands_ref[0:1, :] + bands_ref[1:2, :]
    z = jnp.where(lane == 0, t, jnp.where(lane <= 2 * HYENA_BANDS, _sin(arg), 0.0))
    fq = fq_ref[...]
    a = _sin(fq * (_dot_x3(z, w1_ref[...]) + b1_ref[...]))
    for k in range(w2_ref.shape[0]):
        a = _sin(fq * (_dot_x3(a, w2_ref[k]) + b2_ref[k]))
    h = _dot_x3(a, w3_ref[jnp.where(bwd, 1, 0)])
    decay = jnp.exp(-t[:, 0:1] * dl_ref[...])
    k = h * decay
    k = jnp.where(row[:, 0:1] == n, 0.0, k)
    k_ref[...] = k.astype(k_ref.dtype)
    part = jnp.sum(jnp.abs(k), axis=0, keepdims=True)

    @pl.when(i == 0)
    def _():
        nrm_ref[...] = jnp.zeros_like(nrm_ref)

    nrm_ref[...] += jnp.broadcast_to(part, nrm_ref.shape)


def _hy_filter_call(n, d, out_dtype, f_w1, f_b1, f_freq, f_w2, f_b2, f_w3):
    fw = f_w1.shape[1]
    tr = min(ROW_TILE, n)
    bands_np = np.zeros((8, LANE), np.float32)
    bands_np[0, 1:1 + HYENA_BANDS] = np.linspace(1e-4, HYENA_BANDS - 1, HYENA_BANDS, dtype=np.float32)
    bands_np[0, 1 + HYENA_BANDS:1 + 2 * HYENA_BANDS] = bands_np[0, 1:1 + HYENA_BANDS]
    bands_np[1, 1:1 + HYENA_BANDS] = 0.5 * np.pi
    bands_np[1, 1 + HYENA_BANDS:1 + 2 * HYENA_BANDS] = np.pi
    w1p = jnp.zeros((LANE, fw), F32).at[:f_w1.shape[0]].set(f_w1)
    deltas = jnp.abs(jnp.linspace(math.log(HYENA_TARGET) / HYENA_SLOW, math.log(HYENA_TARGET) / HYENA_FAST,
                                  d, dtype=F32)).reshape(1, d)
    row = pl.BlockSpec((tr, d), lambda i: (i, 0))
    return pl.pallas_call(
        functools.partial(_hy_filter_kernel, n=n, tr=tr), grid=(2 * n // tr,),
        in_specs=[_const_spec((8, LANE)), _const_spec((LANE, fw)), _const_spec((1, fw)), _const_spec((1, fw)),
                  _const_spec(f_w2.shape), _const_spec((f_w2.shape[0], 1, fw)), _const_spec((2, fw, d)),
                  _const_spec((1, d))],
        out_specs=[row, _const_spec((8, d))],
        out_shape=[jax.ShapeDtypeStruct((2 * n, d), out_dtype), jax.ShapeDtypeStruct((8, d), F32)],
        compiler_params=_cp("arbitrary"), name="hy_filter",
    )(jnp.asarray(bands_np), w1p, f_b1.reshape(1, fw), f_freq.reshape(1, fw), f_w2,
      f_b2.reshape(f_w2.shape[0], 1, fw), jnp.transpose(f_w3.reshape(fw, 2, d), (1, 0, 2)), deltas)


def _dft_cs(nf, nt, period):
    ft = (np.arange(nf)[:, None] * np.arange(nt)[None, :]) % period
    ang = 2.0 * np.pi * ft / period
    return np.cos(ang), np.sin(ang)


def _twiddle_tables(n1, n2):
    nn = n1 * n2
    f1 = jnp.arange(n1, dtype=jnp.int32)
    t2 = jnp.arange(n2, dtype=jnp.int32)
    idx = (f1[:, None] * t2[None, :]) % nn
    ang = idx.astype(F32) * (2.0 * math.pi / nn)
    shape = idx.shape + (LANE,)
    return (jnp.broadcast_to(jnp.cos(ang)[..., None], shape),
            jnp.broadcast_to(jnp.sin(ang)[..., None], shape))


KRON_R = 16
KRON_H = 8
KRON_W = 256


def _cblock(mc):
    return np.block([[mc.real, -mc.imag], [mc.imag, mc.real]])


def _split_radix(n1):
    b = 16 if (n1 % 16 == 0 and n1 >= 64) else 4
    assert n1 % b == 0 and (n1 // b) % 2 == 0
    return n1 // b, b


def _slab_dft_mats(n1, a_in, a_out, sign, neg_im=False, real_in=False, scale=1.0):
    a, b = _split_radix(n1)
    r = KRON_H
    eye = np.eye(r)
    w = lambda num, den: np.exp(sign * 2j * np.pi * num / den)
    ua = np.arange(a)[:, None] * np.arange(a)[None, :]
    vb = np.arange(b)[:, None] * np.arange(b)[None, :]
    if sign < 0:
        m1 = np.kron(w(ua[:, :a_in], a) * scale, eye)
        l1 = _cblock(m1)
        if neg_im:
            l1[:, a_in * r:] *= -1.0
        if real_in:
            l1 = l1[:, :a_in * r]
        l2 = np.stack([_cblock(np.kron(w(vb, b) * w(u * np.arange(b)[None, :], n1), eye)) for u in range(a)])
    else:
        l1 = np.stack([_cblock(np.kron(w(vb, b) * w(u * np.arange(b)[:, None], n1), eye)) for u in range(a)])
        l2 = _cblock(np.kron(w(ua[:a_out, :], a) * scale, eye))
    return jnp.asarray(l1, BF16), jnp.asarray(l2, BF16)


def _kfa_kernel(*refs, nparts, a_in, a, b):
    l1_ref, l2_ref = refs[0], refs[1]
    parts = refs[2:2 + nparts]
    twc_ref, tws_ref, or_ref, oi_ref = refs[2 + nparts:]
    r = KRON_H
    halves = range(KRON_R // r)
    reps = or_ref.shape[-1] // LANE
    l1 = l1_ref[...]
    y1 = []
    for h in halves:
        rows = slice(h * r, (h + 1) * r)
        y1.append([_dot(l1, jnp.concatenate([p[0, aa * b + bb].astype(F32)[rows] for p in parts
                                             for aa in range(a_in)], axis=0).astype(BF16))
                   for bb in range(b)])
    for u in range(a):
        z = []
        for h in halves:
            x = jnp.concatenate([y1[h][bb][u * r:(u + 1) * r] for bb in range(b)]
                                + [y1[h][bb][(a + u) * r:(a + u + 1) * r] for bb in range(b)], axis=0)
            z.append(_dot(l2_ref[u], x.astype(BF16)))
        for v in range(b):
            f1 = u + a * v
            zr = jnp.concatenate([zh[v * r:(v + 1) * r] for zh in z], axis=0)
            zi = jnp.concatenate([zh[(b + v) * r:(b + v + 1) * r] for zh in z], axis=0)
            c = jnp.concatenate([twc_ref[f1]] * reps, axis=1)
            sn = jnp.concatenate([tws_ref[f1]] * reps, axis=1)
            or_ref[0, f1] = (zr * c + zi * sn).astype(or_ref.dtype)
            oi_ref[0, f1] = (zi * c - zr * sn).astype(oi_ref.dtype)


def _kfa_call(parts, nb_out, n1, n2, d, l1, l2, twc, tws):
    a, b = _split_radix(n1)
    a_in = parts[0][0].shape[1] // b
    r, w = KRON_R, min(KRON_W, d)
    in_specs = [_const_spec(l1.shape), pl.BlockSpec(l2.shape, lambda bi, j, k: (0, 0, 0), pipeline_mode=pl.Buffered(1))]
    args = [l1, l2]
    for arr, bi_fixed in parts:
        t1_in = arr.shape[1]
        if bi_fixed is None:
            in_specs.append(pl.BlockSpec((1, t1_in, r, w), lambda bi, j, k: (bi, 0, j, k)))
        else:
            in_specs.append(pl.BlockSpec((1, t1_in, r, w), lambda bi, j, k, f=bi_fixed: (f, 0, j, k)))
        args.append(arr)
    tw_spec = pl.BlockSpec((n1, r, LANE), lambda bi, j, k: (0, j, 0))
    out_spec = pl.BlockSpec((1, n1, r, w), lambda bi, j, k: (bi, 0, j, k))
    out = jax.ShapeDtypeStruct((nb_out, n1, n2, d), BF16)
    return pl.pallas_call(
        functools.partial(_kfa_kernel, nparts=len(parts), a_in=a_in, a=a, b=b),
        grid=(nb_out, n2 // r, d // w),
        in_specs=in_specs + [tw_spec, tw_spec], out_specs=[out_spec, out_spec], out_shape=[out, out],
        compiler_params=_cp("parallel", "parallel", "parallel"), name="kfa",
    )(*args, twc, tws)


def _kfc_kernel(l3_ref, l4_ref, gr_ref, gi_ref, vg_ref, g0_ref, skip_ref, o_ref, *, a, b, a_out):
    r = KRON_H
    halves = range(KRON_R // r)
    y3 = [[] for _ in halves]
    for u in range(a):
        gr = [gr_ref[0, u + a * v].astype(F32) for v in range(b)]
        gi = [gi_ref[0, u + a * v].astype(F32) for v in range(b)]
        for h in halves:
            rows = slice(h * r, (h + 1) * r)
            x = jnp.concatenate([g[rows] for g in gr] + [g[rows] for g in gi], axis=0).astype(BF16)
            y3[h].append(_dot(l3_ref[u], x))
    l4 = l4_ref[...]
    skip = skip_ref[...]
    for bb in range(b):
        y = []
        for h in halves:
            x = jnp.concatenate([y3[h][u][bb * r:(bb + 1) * r] for u in range(a)]
                                + [y3[h][u][(b + bb) * r:(b + bb + 1) * r] for u in range(a)], axis=0)
            y.append(_dot(l4, x.astype(BF16)))
        for sg in range(2):
            for aa in range(a_out):
                t1 = aa * b + bb
                yb = jnp.concatenate([yh[(sg * a_out + aa) * r:(sg * a_out + aa + 1) * r] for yh in y], axis=0)
                o_ref[sg, t1] = ((yb + vg_ref[sg, t1] * skip) * g0_ref[sg, t1]).astype(o_ref.dtype)


def _kfc_call(g1r, g1i, vg4, g04, skip, n1, n2, d, l3, l4):
    a, b = _split_radix(n1)
    nb, t1_out = vg4.shape[:2]
    r, w = KRON_R, min(KRON_W, d)
    slab = pl.BlockSpec((1, n1, r, w), lambda j, k: (0, 0, j, k))
    nat = pl.BlockSpec((nb, t1_out, r, w), lambda j, k: (0, 0, j, k))
    return pl.pallas_call(
        functools.partial(_kfc_kernel, a=a, b=b, a_out=t1_out // b), grid=(n2 // r, d // w),
        in_specs=[pl.BlockSpec(l3.shape, lambda j, k: (0, 0, 0), pipeline_mode=pl.Buffered(1)),
                  _const_spec(l4.shape), slab, slab, nat, nat, pl.BlockSpec((1, w), lambda j, k: (0, k))],
        out_specs=nat, out_shape=jax.ShapeDtypeStruct(vg4.shape, BF16),
        compiler_params=_cp("parallel", "parallel"), name="kfc",
    )(l3, l4, g1r, g1i, vg4, g04, skip.reshape(1, d))


def _hy_b_kernel(fm_ref, fmc_ref, xr_ref, xi_ref, fr_ref, fi_ref, sc_ref, twc_ref, tws_ref,
                 or_ref, oi_ref, *, nf, d):
    fm = fm_ref[...]
    fmc = fmc_ref[...]
    n2 = xr_ref.shape[1]
    sc = sc_ref[...]
    for s in range(nf):
        x = _dot(fm, jnp.concatenate([xr_ref[s], xi_ref[s]], axis=0))
        kk = _dot(fm, jnp.concatenate([fr_ref[s], fi_ref[s]], axis=0))
        xr, xi = x[:n2], x[n2:]
        kr, ki = kk[:n2] * sc, kk[n2:] * sc
        yr = (xr * kr - xi * ki).astype(BF16)
        yi = (xr * ki + xi * kr).astype(BF16)
        g = _dot(fmc, jnp.concatenate([yr, yi], axis=0))
        gr, gi = g[:n2], g[n2:]
        c = jnp.concatenate([twc_ref[s]] * (d // LANE), axis=1)
        sn = jnp.concatenate([tws_ref[s]] * (d // LANE), axis=1)
        or_ref[s] = (gr * c - gi * sn).astype(or_ref.dtype)
        oi_ref[s] = (gi * c + gr * sn).astype(oi_ref.dtype)


def _block_c(cs, sn, sign):
    return np.block([[cs, -sign * sn], [sign * sn, cs]])


def _hy_conv_long(vg, g0, k, nrm, skip, b, n, d):
    n2 = FFT_N2
    nn = 2 * n
    n1 = nn // n2
    rows = n // n2
    a, _ = _split_radix(n1)
    la1, la2 = _slab_dft_mats(n1, a // 2, 0, -1)
    lk1, lk2 = _slab_dft_mats(n1, a, 0, -1, real_in=True)
    lc3, lc4 = _slab_dft_mats(n1, 0, a // 2, +1)
    cs2, sn2 = _dft_cs(n2, n2, n2)
    fm_b = jnp.asarray(_block_c(cs2, sn2, -1.0), BF16)
    fm_bc = jnp.asarray(_block_c(cs2, sn2, 1.0), BF16)
    twc, tws = _twiddle_tables(n1, n2)

    kfr, kfi = _kfa_call([(k.reshape(1, n1, n2, d), 0)], 1, n1, n2, d, lk1, lk2, twc, tws)
    nf = next(k for k in (8, 4, 1) if n1 % k == 0)
    scale = (1.0 / (nrm[0:1, :] * nn))
    slab = pl.BlockSpec((nf, n2, d), lambda j: (j, 0, 0))
    shp3 = (n1, n2, d)
    vg4 = vg.reshape(b, rows, n2, d)
    g04 = g0.reshape(b, rows, n2, d)
    x1r, x1i = _kfa_call([(vg4, 0), (vg4, 1)], 1, n1, n2, d, la1, la2, twc, tws)
    tw_slab = pl.BlockSpec((nf, n2, LANE), lambda j: (j, 0, 0))
    g1r, g1i = pl.pallas_call(
        functools.partial(_hy_b_kernel, nf=nf, d=d), grid=(n1 // nf,),
        in_specs=[_const_spec(fm_b.shape), _const_spec(fm_bc.shape), slab, slab, slab, slab, _const_spec((1, d)),
                  tw_slab, tw_slab],
        out_specs=[slab, slab],
        out_shape=[jax.ShapeDtypeStruct(shp3, BF16)] * 2,
        compiler_params=_cp("parallel"), name="hy_b",
    )(fm_b, fm_bc, x1r.reshape(shp3), x1i.reshape(shp3), kfr.reshape(shp3), kfi.reshape(shp3), scale, twc, tws)
    out = _kfc_call(g1r.reshape(1, n1, n2, d), g1i.reshape(1, n1, n2, d), vg4, g04, skip, n1, n2, d, lc3, lc4)
    return out.reshape(b * n, d)


def _hy_short_kernel(fa_ref, fk_ref, fi_ref, vg_ref, g0_ref, k_ref, nrm_ref, skip_ref, o_ref, *, n):
    z = jnp.concatenate([vg_ref[0], vg_ref[1]], axis=0)
    x = _dot_hi(fa_ref[...], z)
    kk = _dot_hi(fk_ref[...], k_ref[...])
    nn = 2 * n
    sc = 1.0 / (nrm_ref[0:1, :] * nn)
    xr, xi = x[:nn], x[nn:]
    kr, ki = kk[:nn] * sc, kk[nn:] * sc
    y = _dot_hi(fi_ref[...], jnp.concatenate([xr * kr - xi * ki, xr * ki + xi * kr], axis=0))
    skip = skip_ref[...]
    for bi in range(2):
        o_ref[bi] = ((y[bi * n:(bi + 1) * n] + vg_ref[bi] * skip) * g0_ref[bi]).astype(o_ref.dtype)


def _hy_conv_short(vg, g0, k, nrm, skip, b, n, d):
    nn = 2 * n
    cs, sn = _dft_cs(nn, n, nn)
    fa = jnp.asarray(_block_c(cs, sn, -1.0), F32)
    csk, snk = _dft_cs(nn, nn, nn)
    fk = jnp.asarray(np.concatenate([csk, -snk], axis=0), F32)
    csi, sni = _dft_cs(n, nn, nn)
    fi = jnp.asarray(_block_c(csi, sni, 1.0), F32)
    cb = 256
    col3 = pl.BlockSpec((b, n, cb), lambda j: (0, 0, j))
    vec = pl.BlockSpec((1, cb), lambda j: (0, j))
    out = pl.pallas_call(
        functools.partial(_hy_short_kernel, n=n), grid=(d // cb,),
        in_specs=[_const_spec(fa.shape), _const_spec(fk.shape), _const_spec(fi.shape), col3, col3,
                  pl.BlockSpec((nn, cb), lambda j: (0, j)), pl.BlockSpec((8, cb), lambda j: (0, j)), vec],
        out_specs=col3, out_shape=jax.ShapeDtypeStruct((b, n, d), BF16),
        compiler_params=_cp("parallel"), name="hy_short",
    )(fa, fk, fi, vg.reshape(b, n, d), g0.reshape(b, n, d), k, nrm, skip.reshape(1, d))
    return out.reshape(b * n, d)


def _hyena_layer(xl, xc, mods_l, mods_c, pre_g, fin, w_in, b_in, conv_w, conv_b, filt, skip,
                 w_out, b_out, b, n, c):
    d = xl.shape[-1]
    w_in_b = w_in.astype(BF16)
    w_out_b = w_out.astype(BF16)
    tm = min(ROW_TILE, n)
    g0, vg = _hy_in_call(xl, mods_l, pre_g, w_in_b, b_in, conv_w, conv_b, min(2 * ROW_TILE, n), n, BF16)
    k, nrm = _hy_filter_call(n, d, BF16, *filt)
    u_out = _hy_conv_long(vg, g0, k, nrm, skip[0], b, n, d)
    xl, (hl,) = fin(xl, u_out, mods_l, w_out_b, b_out, tm, n)

    g0c, vgc = _hy_in_call(xc, mods_c, pre_g, w_in_b, b_in, conv_w, conv_b, c, c, F32)
    kc, nrmc = _hy_filter_call(c, d, F32, *filt)
    u_out_c = _hy_conv_short(vgc, g0c, kc, nrmc, skip[0], b, c, d)
    xc, (hc,) = fin(xc, u_out_c, mods_c, w_out_b, b_out, c, c, is_ctx=True)
    return xl, hl, hc


def _s5_operators(lam_re, lam_im, log_dt, b_re, b_im, c_re, c_im, d_skip):
    t = S5_T
    g, ns = lam_re.shape[1], lam_re.shape[2]
    gc = b_re.shape[-1]
    gl = LANE // gc
    nblk = g // gl
    lam = lax.complex(lam_re, lam_im)
    dt = jnp.exp(log_dt)[..., None]
    lam_bar = jnp.exp(lam * dt)
    b_bar = ((lam_bar - 1.0) / lam)[..., None] * lax.complex(b_re, b_im)
    c_mat = lax.complex(c_re, c_im)
    pw = jnp.arange(t + 1, dtype=F32)
    lam_pw = jnp.exp((lam * dt)[None] * pw[:, None, None, None])
    hp = HIGHEST
    kern = jnp.einsum('dgcn,tdgn,dgne->dgtce', c_mat, lam_pw[:t], b_bar, precision=hp).real
    dsk = d_skip.reshape(g, gc)
    kt = jnp.swapaxes(kern, -1, -2)
    centre = kt[0][:, 0] + kt[1][:, 0] + jnp.eye(gc, dtype=F32)[None] * dsk[:, :, None]
    ks = jnp.concatenate([kt[1][:, 1:][:, ::-1], centre[:, None], kt[0][:, 1:]], axis=1)
    ks = jnp.transpose(ks.reshape(nblk, gl, 2 * t - 1, gc, gc), (0, 2, 1, 3, 4)).reshape(nblk, 2 * t - 1, LANE, gc)
    same_group = (np.arange(LANE)[:, None] // gc == np.arange(LANE)[None, :] // gc).astype(np.float32)
    d_tab = jnp.tile(ks, (1, 1, 1, gl)) * same_group

    def compact(zc, im_sign):
        z = jnp.concatenate([zc.real, im_sign * zc.imag], axis=-1)
        z = jnp.transpose(z, (2, 1, 0, 3, 4)).reshape(nblk, gl, t, 2, gc, 2 * ns)
        return jnp.transpose(z, (0, 2, 1, 3, 4, 5))

    b_t = jnp.swapaxes(b_bar, -1, -2)
    pf = lam_pw[:t][::-1][:, 0, :, None, :] * b_t[0][None]
    pb = lam_pw[:t][:, 1, :, None, :] * b_t[1][None]
    p_tab = compact(jnp.stack([pf, pb], axis=0), 1.0)
    qf = c_mat[0][None] * lam_pw[1:t + 1, 0][:, :, None, :]
    qb = c_mat[1][None] * lam_pw[1:t + 1][::-1][:, 1][:, :, None, :]
    q_tab = compact(jnp.stack([qf, qb], axis=0), -1.0)

    a = lam_pw[t]
    m_op, p_op, q_op = _s5_expand(d_tab, p_tab, q_tab)
    return m_op, p_op, q_op, a.real.reshape(2, g * ns), a.imag.reshape(2, g * ns)


def _s5_m_kernel(d_ref, o_ref, *, t):
    for s in range(t):
        for tt in range(t):
            o_ref[0, s * LANE:(s + 1) * LANE, tt * LANE:(tt + 1) * LANE] = d_ref[0, tt - s + t - 1].astype(o_ref.dtype)


def _s5_pq_kernel(c_ref, o_ref, *, transpose):
    t, gl, nd, gc, w = c_ref.shape[1:]
    ns = w // 2
    lane_grp = lax.broadcasted_iota(jnp.int32, (gc, gl * ns), 1) // ns
    for j in range(t):
        rows = []
        for g in range(gl):
            cols = []
            for dd in range(nd):
                piece = c_ref[0, j, g, dd]
                for ri in range(2):
                    tiled = jnp.concatenate([piece[:, ri * ns:(ri + 1) * ns]] * gl, axis=1)
                    cols.append(jnp.where(lane_grp == g, tiled, 0.0))
            rows.append(jnp.concatenate(cols, axis=1))
        blk = jnp.concatenate(rows, axis=0)
        if transpose:
            o_ref[0, :, j * LANE:(j + 1) * LANE] = blk.T.astype(o_ref.dtype)
        else:
            o_ref[0, j * LANE:(j + 1) * LANE, :] = blk.astype(o_ref.dtype)


def _s5_expand(d_tab, p_tab, q_tab):
    nblk, nlag = d_tab.shape[:2]
    t = (nlag + 1) // 2
    _, _, gl, nd, gc, w = p_tab.shape
    ncol = nd * gl * w
    whole = lambda shape: pl.BlockSpec((1,) + shape, lambda b: (b,) + (0,) * len(shape))
    m_op = pl.pallas_call(
        functools.partial(_s5_m_kernel, t=t), grid=(nblk,),
        in_specs=[whole((nlag, LANE, LANE))], out_specs=whole((t * LANE, t * LANE)),
        out_shape=jax.ShapeDtypeStruct((nblk, t * LANE, t * LANE), BF16),
        compiler_params=_cp("parallel"), name="s5_m_op",
    )(d_tab)
    tab_spec = whole((t, gl, nd, gc, w))
    p_op = pl.pallas_call(
        functools.partial(_s5_pq_kernel, transpose=False), grid=(nblk,),
        in_specs=[tab_spec], out_specs=whole((t * LANE, ncol)),
        out_shape=jax.ShapeDtypeStruct((nblk, t * LANE, ncol), BF16),
        compiler_params=_cp("parallel"), name="s5_p_op",
    )(p_tab)
    q_op = pl.pallas_call(
        functools.partial(_s5_pq_kernel, transpose=True), grid=(nblk,),
        in_specs=[tab_spec], out_specs=whole((ncol, t * LANE)),
        out_shape=jax.ShapeDtypeStruct((nblk, ncol, t * LANE), BF16),
        compiler_params=_cp("parallel"), name="s5_q_op",
    )(q_tab)
    return m_op, p_op, q_op


def _s5_sum_kernel(*refs, t):
    u_refs = refs[:t]
    p_ref = refs[t]
    outs = refs[t + 1:]
    u = jnp.concatenate([r[...] for r in u_refs], axis=1)
    res = _dot(u, p_ref[0])
    w = res.shape[1] // len(outs)
    for i, o in enumerate(outs):
        o[...] = res[:, i * w:(i + 1) * w]


def _s5_sum_call(h, p_op, rb):
    rows = h.shape[0]
    t = S5_T
    nblk = p_op.shape[0]
    w = p_op.shape[2] // 4
    u_specs = [pl.BlockSpec((rb, LANE), lambda gb, r, s=s: (r, s * nblk + gb)) for s in range(t)]
    out_spec = pl.BlockSpec((rb, w), lambda gb, r: (r, gb))
    return pl.pallas_call(
        functools.partial(_s5_sum_kernel, t=t), grid=(nblk, rows // rb),
        in_specs=u_specs + [pl.BlockSpec((1,) + p_op.shape[1:], lambda gb, r: (gb, 0, 0))],
        out_specs=[out_spec] * 4,
        out_shape=[jax.ShapeDtypeStruct((rows, nblk * w), F32)] * 4,
        compiler_params=_cp("parallel", "parallel"), name="s5_sum",
    )(*([h] * t), p_op)


def _s5_rec_kernel(sr_ref, si_ref, ar_ref, ai_ref, h0r_ref, h0i_ref, hr_ref, hi_ref, fr_ref, fi_ref,
                   cr, ci, *, kb, reverse):
    @pl.when(pl.program_id(1) == 0)
    def _():
        cr[...] = h0r_ref[0]
        ci[...] = h0i_ref[0]

    ar, ai = ar_ref[...], ai_ref[...]

    def body(i, carry):
        hr, hi = carry
        k = kb - 1 - i if reverse else i
        hr_ref[pl.ds(k, 1), :] = hr
        hi_ref[pl.ds(k, 1), :] = hi
        nr = ar * hr - ai * hi + sr_ref[pl.ds(k, 1), :]
        ni = ar * hi + ai * hr + si_ref[pl.ds(k, 1), :]
        return nr, ni

    hr, hi = lax.fori_loop(0, kb, body, (cr[...], ci[...]))
    cr[...] = hr
    ci[...] = hi
    fr_ref[0] = hr
    fi_ref[0] = hi


def _s5_rec_call(sr, si, ar, ai, h0r, h0i, nb_batch, reverse):
    rows, w = sr.shape
    nk = rows // nb_batch
    kb = min(128, nk)
    nb = nk // kb
    blk = (lambda bi, i: (bi * nb + nb - 1 - i, 0)) if reverse else (lambda bi, i: (bi * nb + i, 0))
    row_spec = pl.BlockSpec((kb, w), blk)
    vec = _const_spec((1, w))
    st = pl.BlockSpec((1, 1, w), lambda bi, i: (bi, 0, 0))
    return pl.pallas_call(
        functools.partial(_s5_rec_kernel, kb=kb, reverse=reverse), grid=(nb_batch, nb),
        in_specs=[row_spec, row_spec, vec, vec, st, st],
        out_specs=[row_spec, row_spec, st, st],
        out_shape=[jax.ShapeDtypeStruct((rows, w), F32)] * 2 + [jax.ShapeDtypeStruct((nb_batch, 1, w), F32)] * 2,
        scratch_shapes=[pltpu.VMEM((1, w), F32), pltpu.VMEM((1, w), F32)],
        compiler_params=_cp("parallel", "arbitrary"), name="s5_rec",
    )(sr, si, ar, ai, h0r, h0i)


def _s5_out_kernel(*refs, t):
    u_refs = refs[:t]
    h_refs = refs[t:t + 4]
    m_ref, q_ref, o_ref, tok_scr = refs[t + 4:]
    u = jnp.concatenate([r[...] for r in u_refs], axis=1)
    hcat = jnp.concatenate([r[...].astype(BF16) for r in h_refs], axis=1)
    y = _dot(u, m_ref[0]) + _dot(hcat, q_ref[0])
    g = 0.5 * y * (1.0 + lax.erf(y * (2.0 ** -0.5)))
    rb = g.shape[0]
    for j in range(t):
        tok_scr[pl.ds(j, rb, stride=t), :] = g[:, j * LANE:(j + 1) * LANE]
    o_ref[...] = tok_scr[...].astype(o_ref.dtype)


def _s5_out_call(h, states, m_op, q_op, rb):
    rows = h.shape[0]
    t = S5_T
    nblk = m_op.shape[0]
    w = q_op.shape[1] // 4
    u_specs = [pl.BlockSpec((rb, LANE), lambda gb, r, s=s: (r, s * nblk + gb)) for s in range(t)]
    return pl.pallas_call(
        functools.partial(_s5_out_kernel, t=t), grid=(nblk, rows // rb),
        in_specs=u_specs + [pl.BlockSpec((rb, w), lambda gb, r: (r, gb))] * 4
        + [pl.BlockSpec((1,) + m_op.shape[1:], lambda gb, r: (gb, 0, 0)),
           pl.BlockSpec((1,) + q_op.shape[1:], lambda gb, r: (gb, 0, 0))],
        out_specs=pl.BlockSpec((rb * t, LANE), lambda gb, r: (r, gb)),
        out_shape=jax.ShapeDtypeStruct((rows * t, nblk * LANE), BF16),
        scratch_shapes=[pltpu.VMEM((rb * t, LANE), F32)],
        compiler_params=_cp("parallel", "parallel"), name="s5_out",
    )(*([h] * t), *states, m_op, q_op)


def _s5_layer(xl, hl, hc, mods_l, fin, lam_re, lam_im, log_dt, b_re, b_im, c_re, c_im,
              d_skip, w_glu, b_glu, b, n, c):
    t = S5_T
    m_op, p_op, q_op, a_re, a_im = _s5_operators(lam_re, lam_im, log_dt, b_re, b_im, c_re, c_im, d_skip)
    w = a_re.shape[-1]

    def scan(h, init):
        sfr, sfi, sbr, sbi = _s5_sum_call(h, p_op, min(ROW_TILE, h.shape[0]))
        hfr, hfi, ffr, ffi = _s5_rec_call(sfr, sfi, a_re[0:1], a_im[0:1], init[0], init[1], b, False)
        hbr, hbi, fbr, fbi = _s5_rec_call(sbr, sbi, a_re[1:2], a_im[1:2], init[2], init[3], b, True)
        return (hfr, hfi, hbr, hbi), (ffr, ffi, fbr, fbi)

    zeros = jnp.zeros((b, 1, w), F32)
    _, ctx_final = scan(hc, (zeros,) * 4)
    states, _ = scan(hl, ctx_final)
    nk = n // t
    g_nat = _s5_out_call(hl, states, m_op, q_op, min(ROW_TILE, b * nk))
    return fin(xl, g_nat, mods_l, w_glu.astype(BF16), b_glu, min(ROW_TILE, n), n, glu=True)


def _fnet_channel_mats():
    cc, sc = _dft_cs(FNET_GC, FNET_GC, FNET_GC)
    return jnp.asarray(np.concatenate([cc, sc], axis=1) / np.sqrt(FNET_GC), BF16)


def _fn_c_kernel(l5_ref, l6_ref, xr_ref, xi_ref, o_ref, *, nh):
    r = KRON_R
    n2 = nh * r
    l5 = l5_ref[...]
    y5 = [_dot(l5, jnp.concatenate([xr_ref[0, f], xi_ref[0, f]], axis=0)).astype(BF16) for f in range(r)]
    l6 = l6_ref[...]
    hf = KRON_H
    for p in range(nh):
        outs = []
        for fs in range(0, r, hf):
            x = jnp.concatenate([y5[f][p * r:(p + 1) * r] for f in range(fs, fs + hf)]
                                + [y5[f][n2 + p * r:n2 + (p + 1) * r] for f in range(fs, fs + hf)], axis=0)
            outs.append(_dot(l6, x))
        for q in range(r):
            rows = jnp.concatenate([o[q * hf:(q + 1) * hf] for o in outs], axis=0)
            o_ref[0, p + nh * q] = rows.astype(o_ref.dtype)


def _fnet_layer(xl, am, bm, mods_l, fin, w_o, b_o, b, n, d):
    n2 = FFT_N2
    n1 = n // n2
    tm = min(ROW_TILE, n)
    a, _ = _split_radix(n1)
    l1, l2 = _slab_dft_mats(n1, a, 0, -1, neg_im=True, scale=1.0 / np.sqrt(n))
    twc, tws = _twiddle_tables(n1, n2)
    a4, b4 = am.reshape(b, n1, n2, d), bm.reshape(b, n1, n2, d)
    xr, xi = _kfa_call([(a4, None), (b4, None)], b, n1, n2, d, l1, l2, twc, tws)

    r, w = KRON_R, min(KRON_W, d)
    nh = n2 // r
    assert n1 % r == 0 and n2 % r == 0
    m5 = np.zeros((n2, n2), np.complex128)
    hf = KRON_H
    m6 = np.zeros((r * hf, hf * r), np.complex128)
    for s in range(r):
        for p in range(nh):
            for h in range(nh):
                m5[p * r + s, h * r + s] = np.exp(-2j * np.pi * (p * h / nh + p * s / n2))
        for q in range(r):
            for f in range(hf):
                m6[q * hf + f, f * r + s] = np.exp(-2j * np.pi * q * s / r)
    l5 = jnp.asarray(_cblock(m5), BF16)
    l6 = jnp.asarray(np.concatenate([m6.real, -m6.imag], axis=1), BF16)
    grp = pl.BlockSpec((1, r, n2, w), lambda bi, fh, k: (bi, fh, 0, k))
    y = pl.pallas_call(
        functools.partial(_fn_c_kernel, nh=nh), grid=(b, n1 // r, d // w),
        in_specs=[_const_spec(l5.shape), _const_spec(l6.shape), grp, grp],
        out_specs=pl.BlockSpec((1, n2, None, r, w), lambda bi, fh, k: (bi, 0, fh, 0, k)),
        out_shape=jax.ShapeDtypeStruct((b, n2, n1 // r, r, d), BF16),
        compiler_params=_cp("parallel", "parallel", "parallel"), name="fn_c",
    )(l5, l6, xr, xi)
    return fin(xl, y.reshape(b * n, d), mods_l, w_o.astype(BF16), b_o, tm, n)


def kernel(x, c, ctx, c_ctx, mod_w, mod_b, mix_pre_g, mix_post_g, ffn_pre_g, ffn_post_g, ffn_w13, ffn_w2,
           mla_w_in, mla_q_norm_g, mla_kv_norm_g, mla_w_uq, mla_w_ukv, mla_w_o,
           hy_w_in, hy_b_in, hy_conv_w, hy_conv_b, hy_f_w1, hy_f_b1, hy_f_freq, hy_f_w2, hy_f_b2, hy_f_w3,
           hy_skip, hy_w_out, hy_b_out,
           s5_lambda_re, s5_lambda_im, s5_log_dt, s5_b_re, s5_b_im, s5_c_re, s5_c_im, s5_d, s5_w_glu, s5_b_glu,
           fn_w_o, fn_b_o):
    b, n, d = x.shape
    cl = ctx.shape[1]
    depth = mod_w.shape[0]
    assert b == 2 and depth == 4, "two batches ride one complex transform; one layer per mixer"
    mods = _mods(c, c_ctx, mod_w, mod_b)
    xl = x.reshape(b * n, d)
    xc = ctx.reshape(b * cl, d)

    w13_all = ffn_w13.astype(BF16)
    w2_all = ffn_w2.astype(BF16)

    def mods_c(i):
        return jnp.broadcast_to(mods[i, 2:3], (b, 8, d))

    def finisher(i, emit_kind=None, cs=None):
        def fin(x_, y_, mods_, wm, bm, tm_, rows_per_batch, glu=False, is_ctx=False):
            emit = None
            if emit_kind is not None:
                emit = (emit_kind, mods_c(i + 1) if is_ctx else mods[i + 1, 0:2], mix_pre_g[i + 1], cs)
            return _mix_ffn_call(x_, y_, mods_, wm, bm, mix_post_g[i], ffn_pre_g[i], ffn_post_g[i],
                                 w13_all, w2_all, i, tm_, rows_per_batch, glu, emit)
        return fin

    xl, xc = _mla_layer(xl, xc, mods[0, 0:2], mods_c(0), mix_pre_g[0], finisher(0), mla_w_in[0],
                        mla_q_norm_g[0], mla_kv_norm_g[0], mla_w_uq[0], mla_w_ukv[0], mla_w_o[0], b, n, cl)
    filt = (hy_f_w1[0], hy_f_b1[0], hy_f_freq[0], hy_f_w2[0], hy_f_b2[0], hy_f_w3[0])
    xl, hl, hc = _hyena_layer(xl, xc, mods[1, 0:2], mods_c(1), mix_pre_g[1], finisher(1, "chunks"), hy_w_in[0],
                              hy_b_in[0], hy_conv_w[0], hy_conv_b[0], filt, hy_skip[0], hy_w_out[0], hy_b_out[0],
                              b, n, cl)
    xl, (am, bm) = _s5_layer(xl, hl, hc, mods[2, 0:2], finisher(2, "fnet", _fnet_channel_mats()), s5_lambda_re[0],
                             s5_lambda_im[0], s5_log_dt[0], s5_b_re[0], s5_b_im[0], s5_c_re[0], s5_c_im[0],
                             s5_d[0], s5_w_glu[0], s5_b_glu[0], b, n, cl)
    xl = _fnet_layer(xl, am, bm, mods[3, 0:2], finisher(3), fn_w_o[0], fn_b_o[0], b, n, d)
    return xl.reshape(b, n, d)
```
